```python
import math
import jax
import jax.numpy as jnp
from jax import lax
import numpy as np


D_MODEL = 1024
BATCH = 32
SEQ = 2048
DEPTH = 2

CTX_LEN = 256
GRID_W = 64
NORM_EPS = 1e-6

MLSTM_WIDTH = 512
MLSTM_HEADS = 4
MLSTM_HEAD_DIM = MLSTM_WIDTH // MLSTM_HEADS
MLSTM_CHUNK = 64

RWKV_WIDTH = 512
RWKV_HEAD_DIM = 64
RWKV_HEADS = RWKV_WIDTH // RWKV_HEAD_DIM
RWKV_DECAY_LORA = 64
RWKV_ICLR_LORA = 64
RWKV_LN_EPS = 64e-5

HGRN_WIDTH = 512
HGRN_HEADS = 4
HGRN_CHUNK = 32

HYENA_WIDTH = 512
HYENA_BANDS = 16
HYENA_POS_DIM = 1 + 2 * HYENA_BANDS
HYENA_FILTER_HIDDEN = 64
HYENA_SIN_FREQ = 1.0
HYENA_FAST_DECAY = 0.3
HYENA_SLOW_DECAY = 1.5
HYENA_TARGET = 1e-2
HYENA_MOD_SHIFT = 0.05

EVEN_SPLITS = (MLSTM_WIDTH,) * 5 + (4 * MLSTM_HEADS,) + (RWKV_WIDTH,) * 4 + (2 * RWKV_DECAY_LORA, 2 * RWKV_ICLR_LORA)
EVEN_IN = sum(EVEN_SPLITS)
ODD_SPLITS = (HGRN_WIDTH,) * 5 + (HYENA_WIDTH,) * 4
ODD_IN = sum(ODD_SPLITS)
MIX_WIDTH_EVEN = MLSTM_WIDTH + RWKV_WIDTH
MIX_WIDTH_ODD = HGRN_WIDTH + HYENA_WIDTH

kernel_name = 'hybrid_mlstm_rwkv7_hgrn2_hyena_dit'


def split_cols(u, sizes):
    return jnp.split(u, np.cumsum(sizes)[:-1].tolist(), axis=-1)


def rmsnorm(x, g):
    xf = x.astype(jnp.float32)
    y = xf * lax.rsqrt(jnp.mean(xf * xf, axis=-1, keepdims=True) + NORM_EPS)
    return (y * g.astype(jnp.float32)).astype(x.dtype)


def heads_rmsnorm(y, g, n_heads):
    B, L, W = y.shape
    yh = y.astype(jnp.float32).reshape(B, L, n_heads, W // n_heads)
    yh = yh * lax.rsqrt(jnp.mean(yh * yh, axis=-1, keepdims=True) + NORM_EPS)
    return yh.reshape(B, L, W) * g.astype(jnp.float32)


def heads_layernorm(y, w, b, n_heads, eps):
    B, L, W = y.shape
    yh = y.astype(jnp.float32).reshape(B, L, n_heads, W // n_heads)
    mu = jnp.mean(yh, axis=-1, keepdims=True)
    var = jnp.mean(jnp.square(yh - mu), axis=-1, keepdims=True)
    yh = (yh - mu) * lax.rsqrt(var + eps)
    return yh.reshape(B, L, W) * w.astype(jnp.float32) + b.astype(jnp.float32)


def adaln(cvec, w_mod, b_mod):
    m = jax.nn.silu(cvec) @ w_mod + b_mod
    return jnp.split(m, 3, axis=-1)


def to_heads(u, n_heads):
    B, L, W = u.shape
    return u.reshape(B, L, n_heads, W // n_heads).transpose(0, 2, 1, 3).astype(jnp.float32)


def from_heads(y):
    B, H, L, d = y.shape
    return y.transpose(0, 2, 1, 3).reshape(B, L, H * d)


def to_chunks(a, T):
    B, H, L = a.shape[:3]
    return jnp.moveaxis(a.reshape(B, H, L // T, T, *a.shape[3:]), 2, 0)


def from_chunks(a):
    nc, B, H, T = a.shape[:4]
    return jnp.moveaxis(a, 0, 2).reshape(B, H, nc * T, *a.shape[4:])


def conv3(u, w):
    up = jnp.pad(u, ((0, 0), (1, 1), (0, 0)))
    return up[:, :-2] * w[0] + up[:, 1:-1] * w[1] + up[:, 2:] * w[2]


def token_shift(u, mu):
    up = jnp.pad(u, ((0, 0), (1, 1), (0, 0)))
    return u + mu * (0.5 * (up[:, :-2] + up[:, 2:]) - u)


def raster_to_column(h):
    B, L, D = h.shape
    rows = L // GRID_W
    return h.reshape(B, rows, GRID_W, D).transpose(0, 2, 1, 3).reshape(B, L, D)


def column_to_raster(h):
    B, L, D = h.shape
    rows = L // GRID_W
    return h.reshape(B, GRID_W, rows, D).transpose(0, 2, 1, 3).reshape(B, L, D)


def bidirectional(scan_fn, init, ctx_shared, ctx_dirs, lat_shared, lat_dirs):
    y_ctx, y_lat = 0.0, 0.0
    for d in range(2):
        orient = (lambda a: a) if d == 0 else (lambda a: jnp.flip(a, axis=2))
        yc, state = scan_fn(tuple(orient(a) for a in ctx_shared + ctx_dirs[d]), init)
        yl, _ = scan_fn(tuple(orient(a) for a in lat_shared + lat_dirs[d]), state)
        y_ctx = y_ctx + orient(yc)
        y_lat = y_lat + orient(yl)
    return y_lat, y_ctx


def mlstm_chunk_scan(inputs, state):
    q, k, v, ig, lf = inputs
    B, H, L, d = q.shape
    mask = jnp.tril(jnp.ones((MLSTM_CHUNK, MLSTM_CHUNK), dtype=bool))

    def step(carry, blk):
        C, n, m = carry
        qc, kc, vc, ic, fc = blk
        b = jnp.cumsum(fc, axis=-1)
        logw = jnp.where(mask, b[..., :, None] - b[..., None, :] + ic[..., None, :], -jnp.inf)
        inter = b + m[..., None]
        m_t = jnp.maximum(inter, jnp.max(logw, axis=-1))
        w = jnp.exp(logw - m_t[..., None])
        s = jnp.exp(inter - m_t)
        qk = jnp.einsum('bhtd,bhsd->bhts', qc, kc) * w
        num = s[..., None] * jnp.einsum('bhtd,bhde->bhte', qc, C) + jnp.einsum('bhts,bhse->bhte', qk, vc)
        den = s * jnp.einsum('bhtd,bhd->bht', qc, n) + jnp.sum(qk, axis=-1)
        h = num / jnp.maximum(jnp.abs(den), jnp.exp(-m_t))[..., None]
        m_new = m_t[..., -1]
        g = jnp.exp(b[..., -1:] - b + ic - m_new[..., None])
        decay = jnp.exp(b[..., -1] + m - m_new)
        C = decay[..., None, None] * C + jnp.einsum('bhs,bhsd,bhse->bhde', g, kc, vc)
        n = decay[..., None] * n + jnp.einsum('bhs,bhsd->bhd', g, kc)
        return (C, n, m_new), h

    state, hs = lax.scan(step, state, tuple(to_chunks(a, MLSTM_CHUNK) for a in inputs))
    return from_chunks(hs), state


def rwkv_scan(inputs, S):
    def step(S, xs):
        r, v, kk, w, k, a = xs
        sa = jnp.einsum('bhvk,bhk->bhv', S, -kk)
        S = S * w[:, :, None, :] + sa[..., None] * (kk * a)[:, :, None, :] + v[..., None] * k[:, :, None, :]
        return S, jnp.einsum('bhvk,bhk->bhv', S, r)

    S, ys = lax.scan(step, S, tuple(jnp.moveaxis(t, 2, 0) for t in inputs))
    return jnp.moveaxis(ys, 0, 2), S


def hgrn_chunk_scan(inputs, S):
    mask = jnp.tril(jnp.ones((HGRN_CHUNK, HGRN_CHUNK), dtype=bool))[:, :, None]

    def step(S, blk):
        qc, vc, kc, gc = blk
        b = jnp.cumsum(gc, axis=2)
        o = jnp.einsum('bhtd,bhde->bhte', qc * jnp.exp(b), S)
        pair = jnp.exp(jnp.where(mask, b[:, :, :, None] - b[:, :, None], -jnp.inf))
        A = jnp.einsum('bhtd,bhtsd,bhsd->bhts', qc, pair, kc)
        o = o + jnp.einsum('bhts,bhse->bhte', A, vc)
        b_end = b[:, :, -1]
        S = jnp.exp(b_end)[..., None] * S + jnp.einsum('bhsd,bhse->bhde', kc * jnp.exp(b_end[:, :, None] - b), vc)
        return S, o

    S, os_ = lax.scan(step, S, tuple(to_chunks(a, HGRN_CHUNK) for a in inputs))
    return from_chunks(os_), S


def hyena_filters(L, p):
    f32 = jnp.float32
    pos = jnp.arange(L, dtype=f32)
    t = pos / L
    bands = jnp.linspace(1e-4, HYENA_BANDS - 1, HYENA_BANDS, dtype=f32)
    ang = (2.0 * math.pi / L) * pos[:, None] * bands
    z = jnp.concatenate([t[:, None], jnp.cos(ang), jnp.sin(ang)], axis=-1)
    hid = jnp.sin(HYENA_SIN_FREQ * (z @ p['y_w1'].astype(f32) + p['y_b1'].astype(f32)))
    hid = jnp.sin(HYENA_SIN_FREQ * (hid @ p['y_w2'].astype(f32) + p['y_b2'].astype(f32)))
    filt = (hid @ p['y_w3'].astype(f32)).reshape(L, 2, HYENA_WIDTH)
    max_decay = math.log(HYENA_TARGET) / HYENA_FAST_DECAY
    min_decay = math.log(HYENA_TARGET) / HYENA_SLOW_DECAY
    deltas = jnp.abs(jnp.linspace(min_decay, max_decay, HYENA_WIDTH, dtype=f32))
    window = jnp.exp(-t[:, None] * deltas) + HYENA_MOD_SHIFT
    filt = filt * window[:, None]
    filt = filt / jnp.sum(jnp.abs(filt), axis=(0, 1), keepdims=True)
    return filt[:, 0], filt[:, 1]


def long_conv_bidir(u, h_fwd, h_bwd, bias):
    B, L, C = u.shape
    kern = jnp.concatenate([h_fwd[:1] + h_bwd[:1], h_fwd[1:], jnp.zeros((1, C), jnp.float32),
                            jnp.flip(h_bwd[1:], axis=0)], axis=0)
    uf = u.astype(jnp.float32)
    spec = jnp.fft.rfft(uf, n=2 * L, axis=1) * jnp.fft.rfft(kern, axis=0)[None]
    y = jnp.fft.irfft(spec, n=2 * L, axis=1)[:, :L]
    return y + uf * bias.astype(jnp.float32)


def hyena_branch(v_in, g0, g1, p):
    L = v_in.shape[1]
    s = conv3(jnp.concatenate([v_in, g0, g1], axis=-1), p['y_short_w']) + p['y_short_b']
    v, x0, x1 = jnp.split(s, 3, axis=-1)
    h_fwd, h_bwd = hyena_filters(L, p)
    return x0 * long_conv_bidir(x1 * v, h_fwd, h_bwd, p['y_bias'])


def even_prep(h, p):
    B, L, _ = h.shape
    f32 = jnp.float32
    mq, mk, mv, mo, mz, mg, rr, rk, rv, rz, wd, ad = split_cols(h @ p['w_in'], EVEN_SPLITS)
    q = jax.nn.silu(conv3(mq, p['m_conv_w'][:, :MLSTM_WIDTH]))
    k = jax.nn.silu(conv3(mk, p['m_conv_w'][:, MLSTM_WIDTH:])) * (MLSTM_HEAD_DIM ** -0.5)
    g = (mg + p['m_gate_b']).astype(f32).reshape(B, L, 2, 2, MLSTM_HEADS).transpose(2, 3, 0, 4, 1)
    m_shared = (to_heads(q, MLSTM_HEADS), to_heads(k, MLSTM_HEADS), to_heads(mv, MLSTM_HEADS))
    m_dirs = tuple((g[d, 0], jax.nn.log_sigmoid(g[d, 1])) for d in range(2))
    r = token_shift(rr.astype(f32), p['r_mu'][0])
    kr = token_shift(rk.astype(f32), p['r_mu'][1])
    v = token_shift(rv.astype(f32), p['r_mu'][2])
    kk = to_heads(kr * p['r_k_k'], RWKV_HEADS)
    kk = kk / jnp.maximum(jnp.sqrt(jnp.sum(kk * kk, axis=-1, keepdims=True)), 1e-12)
    wd = jnp.tanh(wd.astype(f32)).reshape(B, L, 2, RWKV_DECAY_LORA)
    w_raw = p['r_w0'] + jnp.einsum('bldr,drc->bldc', wd, p['r_w_up'])
    decay = jnp.exp(-jnp.exp(-jax.nn.softplus(-w_raw) - 0.5))
    ad = ad.astype(f32).reshape(B, L, 2, RWKV_ICLR_LORA)
    a = jax.nn.sigmoid(p['r_a0'] + jnp.einsum('bldr,drc->bldc', ad, p['r_a_up']))
    kt = kr[:, :, None] * (1.0 + (a - 1.0) * p['r_k_a'])
    r_shared = (to_heads(r, RWKV_HEADS), to_heads(v, RWKV_HEADS), kk)
    r_dirs = tuple((to_heads(decay[:, :, d], RWKV_HEADS), to_heads(kt[:, :, d], RWKV_HEADS),
                    to_heads(a[:, :, d], RWKV_HEADS)) for d in range(2))
    coef = jnp.einsum('blhn,bldhn,hn->blh', r.reshape(B, L, RWKV_HEADS, RWKV_HEAD_DIM),
                      kt.reshape(B, L, 2, RWKV_HEADS, RWKV_HEAD_DIM), p['r_r_k'])
    bonus = (coef[..., None] * v.reshape(B, L, RWKV_HEADS, RWKV_HEAD_DIM)).reshape(B, L, RWKV_WIDTH)
    return {'m_shared': m_shared, 'm_dirs': m_dirs, 'mo': mo, 'mz': mz,
            'r_shared': r_shared, 'r_dirs': r_dirs, 'bonus': bonus, 'rz': rz}


def mixer_even(h_lat, h_ctx, p, need_ctx):
    lat = even_prep(h_lat, p)
    ctx = even_prep(h_ctx, p)
    B = h_lat.shape[0]
    f32 = jnp.float32
    m_init = (jnp.zeros((B, MLSTM_HEADS, MLSTM_HEAD_DIM, MLSTM_HEAD_DIM), f32),
              jnp.zeros((B, MLSTM_HEADS, MLSTM_HEAD_DIM), f32),
              jnp.zeros((B, MLSTM_HEADS), f32))
    r_init = jnp.zeros((B, RWKV_HEADS, RWKV_HEAD_DIM, RWKV_HEAD_DIM), f32)
    m_lat, m_ctx = bidirectional(mlstm_chunk_scan, m_init, ctx['m_shared'], ctx['m_dirs'], lat['m_shared'], lat['m_dirs'])
    r_lat, r_ctx = bidirectional(rwkv_scan, r_init, ctx['r_shared'], ctx['r_dirs'], lat['r_shared'], lat['r_dirs'])

    def finish(pre, m_h, r_h, dtype):
        m_y = heads_rmsnorm(from_heads(m_h), p['m_norm_g'], MLSTM_HEADS) * jax.nn.sigmoid(pre['mo']) * jax.nn.silu(pre['mz'])
        r_y = heads_layernorm(from_heads(r_h) + pre['bonus'], p['r_ln_w'], p['r_ln_b'], RWKV_HEADS, RWKV_LN_EPS) * jax.nn.silu(pre['rz'])
        return jnp.concatenate([m_y, r_y], axis=-1).astype(dtype) @ p['w_out']

    y_lat = finish(lat, m_lat, r_lat, h_lat.dtype)
    y_ctx = finish(ctx, m_ctx, r_ctx, h_ctx.dtype) if need_ctx else None
    return y_lat, y_ctx


def odd_prep(h, p, lb):
    gq, gi, gf_fwd, gf_bwd, gz, yv, y0, y1, yz = split_cols(h @ p['w_in'], ODD_SPLITS)

    def forget(ff):
        log_f = jnp.logaddexp(jnp.log(lb), jnp.log1p(-lb) + jax.nn.log_sigmoid(ff.astype(jnp.float32)))
        return (to_heads(-jnp.expm1(log_f), HGRN_HEADS), to_heads(log_f, HGRN_HEADS))

    return {'g_shared': (to_heads(gq, HGRN_HEADS), to_heads(gi, HGRN_HEADS)),
            'g_dirs': (forget(gf_fwd), forget(gf_bwd)), 'gz': gz,
            'yv': yv, 'y0': y0, 'y1': y1, 'yz': yz}


def mixer_odd(h_lat, h_ctx, p, li, need_ctx):
    probs = jax.nn.softmax(p['lb_all'].astype(jnp.float32), axis=0)
    lb = (jnp.cumsum(probs, axis=0) - probs[0])[li]
    lat = odd_prep(h_lat, p, lb)
    ctx = odd_prep(h_ctx, p, lb)
    B = h_lat.shape[0]
    dh = HGRN_WIDTH // HGRN_HEADS
    init = jnp.zeros((B, HGRN_HEADS, dh, dh), jnp.float32)
    g_lat, g_ctx = bidirectional(hgrn_chunk_scan, init, ctx['g_shared'], ctx['g_dirs'], lat['g_shared'], lat['g_dirs'])

    def finish(pre, g_h, dtype):
        g_y = heads_rmsnorm(from_heads(g_h), p['g_norm_g'], HGRN_HEADS) * jax.nn.silu(pre['gz'])
        y_y = hyena_branch(pre['yv'], pre['y0'], pre['y1'], p) * jax.nn.silu(pre['yz'])
        return jnp.concatenate([g_y, y_y], axis=-1).astype(dtype) @ p['w_out']

    y_lat = finish(lat, g_lat, h_lat.dtype)
    y_ctx = finish(ctx, g_ctx, h_ctx.dtype) if need_ctx else None
    return y_lat, y_ctx


def trunk_layer(x, xc, c, c_ctx, li, p, need_ctx):
    shift, scale, gate = adaln(c, p['mod_w'], p['mod_b'])
    c_shift, c_scale, c_gate = adaln(c_ctx, p['mod_w'], p['mod_b'])
    h = rmsnorm(x, p['norm_g']) * (1.0 + scale[:, None]) + shift[:, None]
    hc = rmsnorm(xc, p['norm_g']) * (1.0 + c_scale) + c_shift
    if li % 2 == 0:
        y, yc = mixer_even(h, hc, p, need_ctx)
    else:
        y, yc = mixer_odd(raster_to_column(h), hc, p, li, need_ctx)
        y = column_to_raster(y)
    x = x + gate[:, None] * y
    if need_ctx:
        xc = xc + c_gate * yc
    return x, xc


def setup_inputs(seed: int = 0) -> dict:
    key = jax.random.key(seed)
    keys = jax.random.split(key, 64)
    count = [0]
    f32 = jnp.float32

    def nrm(shape, scale=1.0):
        k = keys[count[0]]
        count[0] += 1
        return scale * jax.random.normal(k, shape, f32)

    D = D_MODEL
    H = HYENA_FILTER_HIDDEN
    ident3 = jnp.array([0.0, 1.0, 0.0], f32)[:, None]
    f_bias = jnp.linspace(3.0, 6.0, MLSTM_HEADS, dtype=f32)
    return {
        'x': nrm((BATCH, SEQ, D)),
        'c': nrm((BATCH, D)),
        'ctx': nrm((BATCH, CTX_LEN, D)),
        'c_ctx': nrm((D,)),
        'l0_norm_g': 1.0 + nrm((D,), 0.02),
        'l0_mod_w': nrm((D, 3 * D), 0.5 * D ** -0.5),
        'l0_mod_b': nrm((3 * D,), 0.01),
        'l0_w_in': nrm((D, EVEN_IN), D ** -0.5),
        'l0_w_out': nrm((MIX_WIDTH_EVEN, D), MIX_WIDTH_EVEN ** -0.5),
        'l0_mlstm_conv_w': ident3 + nrm((3, 2 * MLSTM_WIDTH), 0.2),
        'l0_mlstm_gate_b': jnp.stack([nrm((2, MLSTM_HEADS), 0.1), f_bias + nrm((2, MLSTM_HEADS), 0.1)], axis=1).reshape(-1),
        'l0_mlstm_norm_g': 1.0 + nrm((MLSTM_WIDTH,), 0.02),
        'l0_rwkv_mu': 0.5 + nrm((3, RWKV_WIDTH), 0.1),
        'l0_rwkv_w0': jnp.linspace(-6.5, -1.5, RWKV_WIDTH, dtype=f32)[None] + nrm((2, RWKV_WIDTH), 0.1),
        'l0_rwkv_w_up': nrm((2, RWKV_DECAY_LORA, RWKV_WIDTH), 0.1 * RWKV_DECAY_LORA ** -0.5),
        'l0_rwkv_a0': nrm((2, RWKV_WIDTH), 0.1),
        'l0_rwkv_a_up': nrm((2, RWKV_ICLR_LORA, RWKV_WIDTH), 0.1 * RWKV_ICLR_LORA ** -0.5),
        'l0_rwkv_k_k': 0.85 + nrm((RWKV_WIDTH,), 0.02),
        'l0_rwkv_k_a': 1.0 + nrm((RWKV_WIDTH,), 0.02),
        'l0_rwkv_r_k': nrm((RWKV_HEADS, RWKV_HEAD_DIM), 0.1),
        'l0_rwkv_ln_w': 1.0 + nrm((RWKV_WIDTH,), 0.02),
        'l0_rwkv_ln_b': nrm((RWKV_WIDTH,), 0.01),
        'hgrn_lower_bounds': nrm((DEPTH, HGRN_WIDTH), 0.1),
        'l1_norm_g': 1.0 + nrm((D,), 0.02),
        'l1_mod_w': nrm((D, 3 * D), 0.5 * D ** -0.5),
        'l1_mod_b': nrm((3 * D,), 0.01),
        'l1_w_in': nrm((D, ODD_IN), D ** -0.5),
        'l1_w_out': nrm((MIX_WIDTH_ODD, D), MIX_WIDTH_ODD ** -0.5),
        'l1_hgrn_norm_g': 1.0 + nrm((HGRN_WIDTH,), 0.02),
        'l1_hyena_short_w': ident3 + nrm((3, 3 * HYENA_WIDTH), 0.2),
        'l1_hyena_short_b': nrm((3 * HYENA_WIDTH,), 0.01),
        'l1_hyena_w1': nrm((HYENA_POS_DIM, H), HYENA_POS_DIM ** -0.5),
        'l1_hyena_b1': nrm((H,), 0.1),
        'l1_hyena_w2': nrm((H, H), H ** -0.5),
        'l1_hyena_b2': nrm((H,), 0.1),
        'l1_hyena_w3': nrm((H, 2 * HYENA_WIDTH), H ** -0.5),
        'l1_hyena_bias': nrm((HYENA_WIDTH,), 0.5),
        'final_norm_g': 1.0 + nrm((D,), 0.02),
    }


def reference(x, c, ctx, c_ctx, l0_norm_g, l0_mod_w, l0_mod_b, l0_w_in, l0_w_out,
              l0_mlstm_conv_w, l0_mlstm_gate_b, l0_mlstm_norm_g,
              l0_rwkv_mu, l0_rwkv_w0, l0_rwkv_w_up, l0_rwkv_a0, l0_rwkv_a_up,
              l0_rwkv_k_k, l0_rwkv_k_a, l0_rwkv_r_k, l0_rwkv_ln_w, l0_rwkv_ln_b,
              hgrn_lower_bounds,
              l1_norm_g, l1_mod_w, l1_mod_b, l1_w_in, l1_w_out, l1_hgrn_norm_g,
              l1_hyena_short_w, l1_hyena_short_b, l1_hyena_w1, l1_hyena_b1,
              l1_hyena_w2, l1_hyena_b2, l1_hyena_w3, l1_hyena_bias, final_norm_g):
    even_p = {'norm_g': l0_norm_g, 'mod_w': l0_mod_w, 'mod_b': l0_mod_b, 'w_in': l0_w_in, 'w_out': l0_w_out,
              'm_conv_w': l0_mlstm_conv_w, 'm_gate_b': l0_mlstm_gate_b, 'm_norm_g': l0_mlstm_norm_g,
              'r_mu': l0_rwkv_mu, 'r_w0': l0_rwkv_w0, 'r_w_up': l0_rwkv_w_up, 'r_a0': l0_rwkv_a0,
              'r_a_up': l0_rwkv_a_up, 'r_k_k': l0_rwkv_k_k, 'r_k_a': l0_rwkv_k_a, 'r_r_k': l0_rwkv_r_k,
              'r_ln_w': l0_rwkv_ln_w, 'r_ln_b': l0_rwkv_ln_b}
    odd_p = {'norm_g': l1_norm_g, 'mod_w': l1_mod_w, 'mod_b': l1_mod_b, 'w_in': l1_w_in, 'w_out': l1_w_out,
             'lb_all': hgrn_lower_bounds, 'g_norm_g': l1_hgrn_norm_g,
             'y_short_w': l1_hyena_short_w, 'y_short_b': l1_hyena_short_b,
             'y_w1': l1_hyena_w1, 'y_b1': l1_hyena_b1, 'y_w2': l1_hyena_w2, 'y_b2': l1_hyena_b2,
             'y_w3': l1_hyena_w3, 'y_bias': l1_hyena_bias}
    layer_params = (even_p, odd_p)
    xc = ctx
    for li in range(DEPTH):
        x, xc = trunk_layer(x, xc, c, c_ctx, li, layer_params[li], li < DEPTH - 1)
    return rmsnorm(x, final_norm_g)
```

```python
import functools
import math

import jax
import jax.numpy as jnp
import numpy as np
from jax import lax
from jax.experimental import pallas as pl
from jax.experimental.pallas import tpu as pltpu

F32 = jnp.float32
BF16 = jnp.bfloat16

GRID_WIDTH = 64
RMS_EPS = 1e-6
MLSTM_N_HEADS = 4
RWKV_HEAD = 64
RWKV_LN_EPSILON = 64e-5
HGRN_N_HEADS = 4
HYENA_N_BANDS = 16
HYENA_FAST = 0.3
HYENA_SLOW = 1.5
HYENA_TGT = 1e-2
HYENA_SHIFT = 0.05

LANES = 128
MXU_DIM = 256
VMEM_LIMIT = 52 * 1024 * 1024

MIX_CHUNK = 128
RWKV_CHUNK = 64
PROJ_ROWS = 768
DFT_TILE = 512


def _bdot(a, b):
    return jnp.dot(a.astype(BF16), b.astype(BF16), preferred_element_type=F32)


def _bdot_nt(a, b):
    return lax.dot_general(a.astype(BF16), b.astype(BF16), (((1,), (1,)), ((), ())),
                           preferred_element_type=F32)


def _bdot_tn(a, b):
    return lax.dot_general(a.astype(BF16), b.astype(BF16), (((0,), (0,)), ((), ())),
                           preferred_element_type=F32)


def _split3(x):
    hi = x.astype(BF16)
    r1 = x - hi.astype(F32)
    mid = r1.astype(BF16)
    lo = (r1 - mid.astype(F32)).astype(BF16)
    return hi, mid, lo


def _sel_dot(sel, x):
    hi, mid, lo = _split3(x)
    d = functools.partial(jnp.dot, preferred_element_type=F32)
    return d(sel, hi) + d(sel, mid) + d(sel, lo)


def _dot_sel(x, sel):
    hi, mid, lo = _split3(x)
    d = functools.partial(jnp.dot, preferred_element_type=F32)
    return d(hi, sel) + d(mid, sel) + d(lo, sel)


def _sigmoid(x):
    return 1.0 / (1.0 + jnp.exp(-x))


def _silu(x):
    return x * _sigmoid(x)


def _iota(shape, dim):
    return lax.broadcasted_iota(jnp.int32, shape, dim)


def _neighbor_rows(ref, t0, rows, n_total, n_ctx):
    has_prev = jnp.logical_and(t0 != 0, t0 != n_ctx)
    has_next = jnp.logical_and(t0 + rows != n_ctx, t0 + rows != n_total)
    prev = ref[0, pl.ds(jnp.maximum(t0 - 1, 0), 1), :]
    nxt = ref[0, pl.ds(jnp.minimum(t0 + rows, n_total - 1), 1), :]
    return jnp.where(has_prev, prev, 0.0), jnp.where(has_next, nxt, 0.0)


def _shifted(cur, prev_row, next_row):
    rows = cur.shape[0]
    rid = _iota(cur.shape, 0)
    down = jnp.where(rid == 0, prev_row, pltpu.roll(cur, 1, 0))
    up = jnp.where(rid == rows - 1, next_row, pltpu.roll(cur, rows - 1, 0))
    return down, up


def _chunk_with_neighbors(ref, t0, rows, n_total, n_ctx):
    cur = ref[0, pl.ds(t0, rows), :]
    prev_row, next_row = _neighbor_rows(ref, t0, rows, n_total, n_ctx)
    down, up = _shifted(cur, prev_row, next_row)
    return cur, down, up


def _bwd_chunk(i, n_chunks, n_ctx_chunks):
    return jnp.where(i < n_ctx_chunks, n_ctx_chunks - 1 - i, n_chunks - 1 + n_ctx_chunks - i)


def _compiler_params(semantics):
    return pltpu.CompilerParams(dimension_semantics=semantics, vmem_limit_bytes=VMEM_LIMIT)


def _mod_kernel(c_ref, w0_ref, b0_ref, w1_ref, b1_ref, o0_ref, o1_ref):
    s = _silu(c_ref[...])
    o0_ref[...] = _bdot(s, w0_ref[...]) + b0_ref[...]
    o1_ref[...] = _bdot(s, w1_ref[...]) + b1_ref[...]


def _modulation(cc, w0, b0, w1, b1):
    rows, d = cc.shape
    n = w0.shape[1]
    tile = d
    grid = (n // tile,)
    wspec = pl.BlockSpec((d, tile), lambda j: (0, j))
    bspec = pl.BlockSpec((1, tile), lambda j: (0, j))
    ospec = pl.BlockSpec((rows, tile), lambda j: (0, j))
    return pl.pallas_call(
        _mod_kernel,
        grid=grid,
        in_specs=[pl.BlockSpec((rows, d), lambda j: (0, 0)), wspec, bspec, wspec, bspec],
        out_specs=[ospec, ospec],
        out_shape=[jax.ShapeDtypeStruct((rows, n), F32)] * 2,
        compiler_params=_compiler_params(("arbitrary",)),
        name="adaln_modulation",
    )(cc, w0, b0.reshape(1, n), w1, b1.reshape(1, n))


def _proj_in_kernel(*refs, rows, n_ctx, with_gates):
    if with_gates:
        x_ref, g_ref, ml_ref, mc_ref, w_ref, wg_ref, gb_ref, u_ref, gt_ref, h_scr = refs
    else:
        x_ref, g_ref, ml_ref, mc_ref, w_ref, u_ref, h_scr = refs
    i = pl.program_id(1)
    n = pl.program_id(2)

    @pl.when(n == 0)
    def _():
        x = x_ref[0]
        y = x * lax.rsqrt(jnp.mean(x * x, axis=-1, keepdims=True) + RMS_EPS) * g_ref[...]
        row = i * rows + _iota((rows, 1), 0)
        is_ctx = row < n_ctx
        ml = ml_ref[0]
        mc = mc_ref[0]
        shift = jnp.where(is_ctx, mc[0:1], ml[0:1])
        scale = jnp.where(is_ctx, mc[1:2], ml[1:2])
        h = (y * (1.0 + scale) + shift).astype(BF16)
        h_scr[...] = h
        if with_gates:
            gt_ref[0] = _bdot_nt(wg_ref[...], h) + gb_ref[:, 0:1]

    u_ref[0] = jnp.dot(h_scr[...], w_ref[...], preferred_element_type=F32)


def _proj_in(xa, norm_g, mod3, w16, n_tile, n_ctx, gate_w=None, gate_b=None):
    bsz, s, d = xa.shape
    n = w16.shape[1]
    rows = PROJ_ROWS
    grid = (bsz, s // rows, n // n_tile)
    ctx_row = mod3.shape[0] - 1
    with_gates = gate_w is not None
    in_specs = [
        pl.BlockSpec((1, rows, d), lambda b, i, j: (b, i, 0)),
        pl.BlockSpec((1, d), lambda b, i, j: (0, 0)),
        pl.BlockSpec((1, 3, d), lambda b, i, j: (b, 0, 0)),
        pl.BlockSpec((1, 3, d), lambda b, i, j: (ctx_row, 0, 0)),
        pl.BlockSpec((d, n_tile), lambda b, i, j: (0, j)),
    ]
    args = [xa, norm_g.reshape(1, d), mod3, mod3, w16]
    out_specs = [pl.BlockSpec((1, rows, n_tile), lambda b, i, j: (b, i, j))]
    out_shape = [jax.ShapeDtypeStruct((bsz, s, n), F32)]
    if with_gates:
        ng = gate_w.shape[0]
        in_specs += [pl.BlockSpec((ng, d), lambda b, i, j: (0, 0)),
                     pl.BlockSpec((ng, LANES), lambda b, i, j: (0, 0))]
        args += [gate_w, gate_b]
        out_specs.append(pl.BlockSpec((1, ng, rows), lambda b, i, j: (b, 0, i)))
        out_shape.append(jax.ShapeDtypeStruct((bsz, ng, s), F32))
    return pl.pallas_call(
        functools.partial(_proj_in_kernel, rows=rows, n_ctx=n_ctx, with_gates=with_gates),
        grid=grid,
        in_specs=in_specs,
        out_specs=out_specs,
        out_shape=out_shape,
        scratch_shapes=[pltpu.VMEM((rows, d), BF16)],
        compiler_params=_compiler_params(("parallel", "arbitrary", "arbitrary")),
        name="norm_mod_proj_in",
    )(*args)


def _proj_out_kernel(ya_ref, yb_ref, x_ref, ml_ref, mc_ref, w_ref, fg_ref, o_ref, *, rows, n_ctx, final_norm):
    i = pl.program_id(1)
    half = ya_ref.shape[2]
    y = _bdot(ya_ref[0], w_ref[0:half, :]) + _bdot(yb_ref[0], w_ref[half:, :])
    row = i * rows + _iota((rows, 1), 0)
    gate = jnp.where(row < n_ctx, mc_ref[0][2:3], ml_ref[0][2:3])
    x = x_ref[0] + gate * y
    if final_norm:
        x = x * lax.rsqrt(jnp.mean(x * x, axis=-1, keepdims=True) + RMS_EPS) * fg_ref[...]
    o_ref[0] = x


def _proj_out(ya, yb, x, mod3, w16, n_ctx, final_g=None):
    bsz, s, d = x.shape
    half = ya.shape[2]
    rows = PROJ_ROWS if s % PROJ_ROWS == 0 else 512
    grid = (bsz, s // rows)
    ctx_row = mod3.shape[0] - 1
    final_norm = final_g is not None
    fg = final_g if final_norm else jnp.ones((d,), F32)
    tok = lambda w: pl.BlockSpec((1, rows, w), lambda b, i: (b, i, 0))
    return pl.pallas_call(
        functools.partial(_proj_out_kernel, rows=rows, n_ctx=n_ctx, final_norm=final_norm),
        grid=grid,
        in_specs=[tok(half), tok(half), tok(d),
                  pl.BlockSpec((1, 3, d), lambda b, i: (b, 0, 0)),
                  pl.BlockSpec((1, 3, d), lambda b, i: (ctx_row, 0, 0)),
                  pl.BlockSpec((2 * half, d), lambda b, i: (0, 0)),
                  pl.BlockSpec((1, d), lambda b, i: (0, 0))],
        out_specs=tok(d),
        out_shape=jax.ShapeDtypeStruct((bsz, s, d), F32),
        compiler_params=_compiler_params(("parallel", "arbitrary")),
        name="proj_out_residual",
    )(ya, yb, x, mod3, mod3, w16, fg.reshape(1, d))


def _cumsum_lanes(x, reverse):
    n = x.shape[-1]
    axis = x.ndim - 1
    lane = _iota(x.shape, axis)
    sh = 1
    while sh < n:
        if reverse:
            x = x + jnp.where(lane < n - sh, pltpu.roll(x, n - sh, axis), 0.0)
        else:
            x = x + jnp.where(lane >= sh, pltpu.roll(x, sh, axis), 0.0)
        sh *= 2
    return x


def _mlstm_step(q, k, v_ext, gates, d, c_ext, m):
    reverse = d == 1
    t = q.shape[0]
    dh = q.shape[1]
    log_f = jnp.minimum(gates, 0.0) - jnp.log1p(jnp.exp(-jnp.abs(gates)))
    cum = _cumsum_lanes(log_f, reverse)
    ig_row = gates[2 * d:2 * d + 1]
    b_row = cum[2 * d + 1:2 * d + 2]
    row_id = _iota(gates.shape, 0)
    stacked = jnp.concatenate([jnp.where(row_id % 2 == 0, gates, cum), jnp.zeros((t - 8, t), F32)], axis=0)
    cols = stacked.T
    ig_col = cols[:, 2 * d:2 * d + 1]
    b_col = cols[:, 2 * d + 1:2 * d + 2]
    r_i = _iota((t, t), 0)
    c_i = _iota((t, t), 1)
    mask = (c_i >= r_i) if reverse else (c_i <= r_i)
    logw = jnp.where(mask, b_col + (ig_row - b_row), -jnp.inf)
    inter = b_col + m
    m_t = jnp.maximum(inter, jnp.max(logw, axis=-1, keepdims=True))
    w = jnp.exp(logw - m_t)
    s = jnp.exp(inter - m_t)
    qk = _bdot_nt(q, k) * w
    num_ext = s * _bdot(q, c_ext) + _bdot(qk, v_ext)
    den = num_ext[:, dh:dh + 1]
    h = num_ext[:, :dh] / jnp.maximum(jnp.abs(den), jnp.exp(-m_t))
    last = 0 if reverse else t - 1
    m_new = m_t[last:last + 1]
    b_last = b_col[last:last + 1]
    g_col = jnp.exp(b_last - b_col + ig_col - m_new)
    decay = jnp.exp(b_last + m - m_new)
    c_new = decay * c_ext + _bdot_tn(g_col * k, v_ext)
    return h, c_new, m_new


def _mlstm_kernel(q_ref, k_ref, v_ref, o_ref, z_ref, gt_ref, cwq_ref, cwk_ref, ng_ref, out_ref,
                  qa_scr, ka_scr, hf_scr, hb_scr, c_scr, *, n_ctx):
    s = q_ref.shape[1]
    dh = q_ref.shape[2]
    t = MIX_CHUNK
    n_chunks = s // t
    n_ctx_chunks = n_ctx // t
    k_scale = dh ** -0.5

    def prep(j, carry):
        t0 = pl.multiple_of(j * t, t)
        for src, cw, dst, scale in ((q_ref, cwq_ref, qa_scr, 1.0), (k_ref, cwk_ref, ka_scr, k_scale)):
            cur, down, up = _chunk_with_neighbors(src, t0, t, s, n_ctx)
            conv = down * cw[0:1, :] + cur * cw[1:2, :] + up * cw[2:3, :]
            dst[pl.ds(t0, t), :] = _silu(conv) * scale
        return carry

    lax.fori_loop(0, n_chunks, prep, 0)

    c_scr[...] = jnp.zeros(c_scr.shape, F32)
    ones_col = (_iota((t, dh), 1) == 0).astype(F32)

    def scan(i, carry):
        m_f, m_b = carry
        outs = []
        for d, (chunk, m_in, h_scr) in enumerate(((i, m_f, hf_scr),
                                                  (_bwd_chunk(i, n_chunks, n_ctx_chunks), m_b, hb_scr))):
            t0 = pl.multiple_of(chunk * t, t)
            q = qa_scr[pl.ds(t0, t), :]
            k = ka_scr[pl.ds(t0, t), :]
            v_ext = jnp.concatenate([v_ref[0, pl.ds(t0, t), :], ones_col], axis=1)
            gates = gt_ref[0, 0, :, pl.ds(t0, t)]
            h, c_new, m_new = _mlstm_step(q, k, v_ext, gates, d, c_scr[d], m_in)
            c_scr[d] = c_new
            h_scr[pl.ds(t0, t), :] = h
            outs.append(m_new)
        return tuple(outs)

    zero = jnp.zeros((1, 1), F32)
    lax.fori_loop(0, n_chunks, scan, (zero, zero))

    def finish(j, carry):
        t0 = pl.multiple_of(j * t, t)
        h = hf_scr[pl.ds(t0, t), :] + hb_scr[pl.ds(t0, t), :]
        y = h * lax.rsqrt(jnp.mean(h * h, axis=-1, keepdims=True) + RMS_EPS) * ng_ref[...]
        out_ref[0, pl.ds(t0, t), :] = y * _sigmoid(o_ref[0, pl.ds(t0, t), :]) * _silu(z_ref[0, pl.ds(t0, t), :])
        return carry

    lax.fori_loop(0, n_chunks, finish, 0)


def _mlstm(u, gt, conv_w, norm_g, n_ctx):
    bsz, s, _ = u.shape
    nh = MLSTM_N_HEADS
    dh = LANES
    width = nh * dh
    col = lambda k: pl.BlockSpec((1, s, dh), lambda b, h, k=k: (b, 0, k * nh + h))
    par = lambda k: pl.BlockSpec((3, dh), lambda b, h, k=k: (0, k * nh + h))
    return pl.pallas_call(
        functools.partial(_mlstm_kernel, n_ctx=n_ctx),
        grid=(bsz, nh),
        in_specs=[col(0), col(1), col(2), col(3), col(4),
                  pl.BlockSpec((1, 1, 8, s), lambda b, h: (b, h, 0, 0)),
                  par(0), par(1),
                  pl.BlockSpec((1, dh), lambda b, h: (0, h))],
        out_specs=pl.BlockSpec((1, s, dh), lambda b, h: (b, 0, h)),
        out_shape=jax.ShapeDtypeStruct((bsz, s, width), F32),
        scratch_shapes=[pltpu.VMEM((s, dh), F32)] * 4 + [pltpu.VMEM((2, dh, 2 * dh), F32)],
        compiler_params=_compiler_params(("parallel", "arbitrary")),
        name="mlstm_mixer",
    )(u, u, u, u, u, gt, conv_w, conv_w, norm_g.reshape(1, width))


def _head_stack(x, lane_lo):
    return jnp.concatenate([jnp.where(lane_lo, x, 0.0), jnp.where(lane_lo, 0.0, x)], axis=0)


def _rwkv_step(r, v, kk, lw, ka, kt, ht, tri, mask_strict, mask_incl2, merge_masks, eye, lane_lo, reverse):
    t = r.shape[0]
    cum = _sel_dot(tri, lw)
    last = 0 if reverse else t - 1
    cum_end = cum[last:last + 1]
    e_inv = jnp.exp(-cum)
    e_end = jnp.exp(cum_end - cum)
    stack = lambda x: _head_stack(x, lane_lo)
    ar = jnp.concatenate([stack(-kk * jnp.exp(cum - lw)), stack(r * jnp.exp(cum))], axis=0)
    bk = jnp.concatenate([stack(ka * e_inv), stack(kt * e_inv)], axis=0)
    bk_end = jnp.concatenate([stack(ka * e_end), stack(kt * e_end)], axis=0)
    m_all = _bdot_nt(ar, bk)
    n2 = 2 * t
    m_ab = jnp.where(mask_strict, m_all[:n2, :n2], 0.0)
    m_ak = jnp.where(mask_strict, m_all[:n2, n2:], 0.0)
    m_low = jnp.where(mask_incl2, m_all[n2:, :], 0.0)
    inv = eye + jnp.where(merge_masks[0], m_ab, 0.0)
    for merge in merge_masks[1:]:
        inv = inv + _bdot(inv, _bdot(jnp.where(merge, m_ab, 0.0), inv))
    arh = _bdot_nt(ar, ht)
    vs = stack(v)
    u = _bdot(inv, arh[:n2] + _bdot(m_ak, vs))
    uv = jnp.concatenate([u, vs], axis=0)
    ys = arh[n2:] + _bdot(m_low, uv)
    y = ys[:t] + ys[t:]
    ht_new = ht * jnp.exp(cum_end) + _bdot_tn(uv, bk_end)
    return y, ht_new


def _rwkv_kernel(rr_ref, rk_ref, rv_ref, rz_ref, wd_ref, ad_ref, mu_ref, kk_ref, ka_ref, rkk_ref,
                 lnw_ref, lnb_ref, w0_ref, a0_ref, wup_ref, aup_ref, out_ref,
                 r_scr, v_scr, kk_scr, lw_scr, ka_scr, kt_scr, bonus_scr, y_scr, ht_scr, *, n_ctx):
    s = rr_ref.shape[1]
    w = rr_ref.shape[2]
    p_rows = MIX_CHUNK
    t = RWKV_CHUNK
    n_chunks = s // t
    n_ctx_chunks = n_ctx // t
    head_sum = ((_iota((w, w), 0) // RWKV_HEAD) == (_iota((w, w), 1) // RWKV_HEAD)).astype(BF16)
    inv_head = 1.0 / RWKV_HEAD

    def prep(j, carry):
        t0 = pl.multiple_of(j * p_rows, p_rows)
        mixed = []
        for idx, src in enumerate((rr_ref, rk_ref, rv_ref)):
            cur, down, up = _chunk_with_neighbors(src, t0, p_rows, s, n_ctx)
            mixed.append(cur + mu_ref[idx:idx + 1, :] * (0.5 * (down + up) - cur))
        r, kr, v = mixed
        kk = kr * kk_ref[...]
        norm = jnp.sqrt(_dot_sel(kk * kk, head_sum))
        kk = kk / jnp.maximum(norm, 1e-12)
        w_raw = _bdot(jnp.tanh(wd_ref[0, pl.ds(t0, p_rows), :]), wup_ref[0]) + w0_ref[0]
        a = _sigmoid(_bdot(ad_ref[0, pl.ds(t0, p_rows), :], aup_ref[0]) + a0_ref[0])
        lw = -math.exp(-0.5) * _sigmoid(w_raw)
        kt_sum = jnp.zeros_like(kr)
        for d in range(2):
            a_d = a[:, d * w:(d + 1) * w]
            kt_d = kr * (1.0 + (a_d - 1.0) * ka_ref[...])
            kt_sum = kt_sum + kt_d
            lw_scr[d, pl.ds(t0, p_rows), :] = lw[:, d * w:(d + 1) * w]
            ka_scr[d, pl.ds(t0, p_rows), :] = kk * a_d
            kt_scr[d, pl.ds(t0, p_rows), :] = kt_d
        coef = _dot_sel(r * kt_sum * rkk_ref[...], head_sum)
        r_scr[pl.ds(t0, p_rows), :] = r
        v_scr[pl.ds(t0, p_rows), :] = v
        kk_scr[pl.ds(t0, p_rows), :] = kk
        bonus_scr[pl.ds(t0, p_rows), :] = coef * v
        return carry

    lax.fori_loop(0, s // p_rows, prep, 0)

    n2 = 2 * t
    r_i = _iota((n2, n2), 0)
    c_i = _iota((n2, n2), 1)
    same = (r_i // t) == (c_i // t)
    rt = r_i % t
    ct = c_i % t
    eye = (r_i == c_i).astype(F32)
    lane_lo = _iota((t, w), 1) < RWKV_HEAD
    tri_r = _iota((t, t), 0)
    tri_c = _iota((t, t), 1)
    consts = []
    for reverse in (False, True):
        strict = jnp.logical_and(same, (ct > rt) if reverse else (ct < rt))
        incl = jnp.logical_and(same, (ct >= rt) if reverse else (ct <= rt))
        tri = ((tri_c >= tri_r) if reverse else (tri_c <= tri_r)).astype(BF16)
        merges = []
        c = 1
        while c < t:
            hi_r = (r_i % (2 * c)) >= c
            hi_c = (c_i % (2 * c)) >= c
            cross = jnp.logical_and(hi_c, jnp.logical_not(hi_r)) if reverse else jnp.logical_and(hi_r, jnp.logical_not(hi_c))
            merges.append(jnp.logical_and((r_i // (2 * c)) == (c_i // (2 * c)), cross))
            c *= 2
        consts.append((tri, strict, jnp.concatenate([incl, incl], axis=1), merges))

    ht_scr[...] = jnp.zeros(ht_scr.shape, F32)

    def scan(i, carry):
        for d, chunk in enumerate((i, _bwd_chunk(i, n_chunks, n_ctx_chunks))):
            t0 = pl.multiple_of(chunk * t, t)
            sl = pl.ds(t0, t)
            tri, strict, incl2, merges = consts[d]
            y, ht_new = _rwkv_step(r_scr[sl, :], v_scr[sl, :], kk_scr[sl, :], lw_scr[d, sl, :],
                                   ka_scr[d, sl, :], kt_scr[d, sl, :], ht_scr[d],
                                   tri, strict, incl2, merges, eye, lane_lo, reverse=(d == 1))
            ht_scr[d] = ht_new
            y_scr[d, sl, :] = y
        return carry

    lax.fori_loop(0, n_chunks, scan, 0)

    def finish(j, carry):
        t0 = pl.multiple_of(j * p_rows, p_rows)
        sl = pl.ds(t0, p_rows)
        y = y_scr[0, sl, :] + y_scr[1, sl, :] + bonus_scr[sl, :]
        mu = _dot_sel(y, head_sum) * inv_head
        yc = y - mu
        var = _dot_sel(yc * yc, head_sum) * inv_head
        yn = yc * lax.rsqrt(var + RWKV_LN_EPSILON) * lnw_ref[...] + lnb_ref[...]
        out_ref[0, sl, :] = yn * _silu(rz_ref[0, sl, :])
        return carry

    lax.fori_loop(0, s // p_rows, finish, 0)


def _rwkv(u, col0, p, n_ctx):
    bsz, s, _ = u.shape
    w = LANES
    width = p["mu"].shape[1]
    n_pairs = width // w
    base = col0 // w
    col = lambda k: pl.BlockSpec((1, s, w), lambda b, h, k=k: (b, 0, base + k * n_pairs + h))
    lora = lambda k: pl.BlockSpec((1, s, w), lambda b, h, k=k: (b, 0, base + 4 * n_pairs + k))
    vec = lambda rows: pl.BlockSpec((rows, w), lambda b, h: (0, h))
    cat = pl.BlockSpec((1, 1, 2 * w), lambda b, h: (h, 0, 0))
    up = pl.BlockSpec((1, w, 2 * w), lambda b, h: (h, 0, 0))
    seq = pltpu.VMEM((s, w), F32)
    seq2 = pltpu.VMEM((2, s, w), F32)
    return pl.pallas_call(
        functools.partial(_rwkv_kernel, n_ctx=n_ctx),
        grid=(bsz, n_pairs),
        in_specs=[col(0), col(1), col(2), col(3), lora(0), lora(1),
                  vec(3), vec(1), vec(1), vec(1), vec(1), vec(1), cat, cat, up, up],
        out_specs=pl.BlockSpec((1, s, w), lambda b, h: (b, 0, h)),
        out_shape=jax.ShapeDtypeStruct((bsz, s, width), F32),
        scratch_shapes=[seq, seq, seq, seq2, seq2, seq2, seq, seq2, pltpu.VMEM((2, w, w), F32)],
        compiler_params=_compiler_params(("parallel", "arbitrary")),
        name="rwkv7_mixer",
    )(u, u, u, u, u, u, p["mu"], p["k_k"], p["k_a"], p["r_k"], p["ln_w"], p["ln_b"],
      p["w0"], p["a0"], p["w_up"], p["a_up"])


def _hgrn_step(q, v, ff, lb, st, tri, reverse):
    t = q.shape[0]
    e = jnp.exp(-jnp.abs(ff))
    big = 1.0 / (1.0 + e)
    small = e / (1.0 + e)
    pos = ff >= 0.0
    sig = jnp.where(pos, big, small)
    f = lb + (1.0 - lb) * sig
    lg = jnp.log(f)
    k = (1.0 - lb) * jnp.where(pos, small, big)
    b = _sel_dot(tri, lg)
    o = _bdot_nt(q * jnp.exp(b), st)

    rid = _iota((t, q.shape[1]), 0)
    r_i = _iota((t, t), 0)
    c_i = _iota((t, t), 1)
    zero_row = jnp.zeros((1, q.shape[1]), F32)
    if reverse:
        before = _shifted(b, zero_row, zero_row)[1]
    else:
        before = _shifted(b, zero_row, zero_row)[0]
    edge = b
    a = jnp.where(r_i == c_i, jnp.sum(q * k, axis=-1, keepdims=True), 0.0)
    c = 1
    while c < t:
        odd = (rid // c) % 2 == 1
        q_side = jnp.logical_not(odd) if reverse else odd
        qt = jnp.where(q_side, q * jnp.exp(b - before), 0.0)
        kt = jnp.where(q_side, 0.0, k * jnp.exp(edge - b))
        a = a + jnp.where((r_i // (2 * c)) == (c_i // (2 * c)), _bdot_nt(qt, kt), 0.0)
        upper = (rid % (2 * c)) >= c
        if reverse:
            before = jnp.where(upper, before, pltpu.roll(before, t - c, 0))
            edge = jnp.where(upper, pltpu.roll(edge, c, 0), edge)
        else:
            before = jnp.where(upper, pltpu.roll(before, c, 0), before)
            edge = jnp.where(upper, edge, pltpu.roll(edge, t - c, 0))
        c *= 2
    o = o + _bdot(a, v)
    last = 0 if reverse else t - 1
    b_end = b[last:last + 1]
    st_new = st * jnp.exp(b_end) + _bdot_tn(v, k * jnp.exp(b_end - b))
    return o, st_new


def _hgrn_kernel(q_ref, i_ref, ff_ref, fb_ref, z_ref, lb_ref, ng_ref, out_ref, o_scr, st_scr, *, n_ctx, layer):
    s = q_ref.shape[1]
    dh = q_ref.shape[2]
    t = MIX_CHUNK
    n_chunks = s // t
    n_ctx_chunks = n_ctx // t
    lbs = lb_ref[...]
    ex = jnp.exp(lbs - jnp.max(lbs, axis=0, keepdims=True))
    probs = ex / jnp.sum(ex, axis=0, keepdims=True)
    csum = probs[0:1]
    for l in range(1, layer + 1):
        csum = csum + probs[l:l + 1]
    lb = csum - probs[0:1]
    tri_r = _iota((t, t), 0)
    tri_c = _iota((t, t), 1)
    tris = [(tri_c <= tri_r).astype(BF16), (tri_c >= tri_r).astype(BF16)]
    st_scr[...] = jnp.zeros(st_scr.shape, F32)

    def scan(i, carry):
        for d, (chunk, f_ref) in enumerate(((i, ff_ref), (_bwd_chunk(i, n_chunks, n_ctx_chunks), fb_ref))):
            t0 = pl.multiple_of(chunk * t, t)
            sl = pl.ds(t0, t)
            o, st_new = _hgrn_step(q_ref[0, sl, :], i_ref[0, sl, :], f_ref[0, sl, :], lb, st_scr[d],
                                   tris[d], reverse=(d == 1))
            st_scr[d] = st_new
            o_scr[d, sl, :] = o
        return carry

    lax.fori_loop(0, n_chunks, scan, 0)

    def finish(j, carry):
        t0 = pl.multiple_of(n_ctx + j * t, t)
        sl = pl.ds(t0, t)
        o = o_scr[0, sl, :] + o_scr[1, sl, :]
        y = o * lax.rsqrt(jnp.mean(o * o, axis=-1, keepdims=True) + RMS_EPS) * ng_ref[...]
        out_ref[0, pl.ds(pl.multiple_of(j * t, t), t), :] = y * _silu(z_ref[0, sl, :])
        return carry

    lax.fori_loop(0, n_chunks - n_ctx_chunks, finish, 0)


def _hgrn(u, lb_all, norm_g, n_ctx, layer):
    bsz, s, _ = u.shape
    nh = HGRN_N_HEADS
    dh = LANES
    width = nh * dh
    depth = lb_all.shape[0]
    col = lambda k: pl.BlockSpec((1, s, dh), lambda b, h, k=k: (b, 0, k * nh + h))
    return pl.pallas_call(
        functools.partial(_hgrn_kernel, n_ctx=n_ctx, layer=layer),
        grid=(bsz, nh),
        in_specs=[col(0), col(1), col(2), col(3), col(4),
                  pl.BlockSpec((depth, dh), lambda b, h: (0, h)),
                  pl.BlockSpec((1, dh), lambda b, h: (0, h))],
        out_specs=pl.BlockSpec((1, s - n_ctx, dh), lambda b, h: (b, 0, h)),
        out_shape=jax.ShapeDtypeStruct((bsz, s - n_ctx, width), F32),
        scratch_shapes=[pltpu.VMEM((2, s, dh), F32), pltpu.VMEM((2, dh, dh), F32)],
        compiler_params=_compiler_params(("parallel", "arbitrary")),
        name="hgrn2_mixer",
    )(u, u, u, u, u, lb_all, norm_g.reshape(1, width))


def _hyena_filter_kernel(z_ref, w1_ref, b1_ref, w2_ref, b2_ref, w3f_ref, w3b_ref, dl_ref, hf_ref, hb_ref):
    hp = functools.partial(jnp.dot, precision=lax.Precision.HIGHEST, preferred_element_type=F32)
    n = z_ref.shape[0]
    hid = jnp.sin(hp(z_ref[...], w1_ref[...]) + b1_ref[...])
    hid = jnp.sin(hp(hid, w2_ref[...]) + b2_ref[...])
    pos = _iota((n, 1), 0).astype(F32) * (1.0 / n)
    window = jnp.exp(-pos * dl_ref[...]) + HYENA_SHIFT
    f0 = hp(hid, w3f_ref[...]) * window
    f1 = hp(hid, w3b_ref[...]) * window
    nrm = jnp.sum(jnp.abs(f0), axis=0, keepdims=True) + jnp.sum(jnp.abs(f1), axis=0, keepdims=True)
    hf_ref[...] = f0 / nrm
    hb_ref[...] = f1 / nrm


def _hyena_filters(n, w1, b1, w2, b2, w3, width):
    pos = np.arange(n, dtype=np.float64)
    bands = np.linspace(1e-4, HYENA_N_BANDS - 1, HYENA_N_BANDS)
    ang = (2.0 * math.pi / n) * pos[:, None] * bands
    z = np.concatenate([(pos / n)[:, None], np.cos(ang), np.sin(ang)], axis=-1)
    z = np.pad(z, ((0, 0), (0, LANES - z.shape[1]))).astype(np.float32)
    max_decay = math.log(HYENA_TGT) / HYENA_FAST
    min_decay = math.log(HYENA_TGT) / HYENA_SLOW
    deltas = np.abs(np.linspace(min_decay, max_decay, width)).astype(np.float32)[None]
    feat, hid = w1.shape
    w1p = jnp.pad(w1, ((0, LANES - feat), (0, LANES - hid)))
    w2p = jnp.pad(w2, ((0, LANES - hid), (0, LANES - hid)))
    w3p = jnp.pad(w3, ((0, LANES - hid), (0, 0)))
    b1p = jnp.pad(b1, (0, LANES - hid)).reshape(1, LANES)
    b2p = jnp.pad(b2, (0, LANES - hid)).reshape(1, LANES)
    n_tiles = width // LANES
    full = lambda shape: pl.BlockSpec(shape, lambda j: (0, 0))
    out = pl.BlockSpec((n, LANES), lambda j: (0, j))
    hf, hb = pl.pallas_call(
        _hyena_filter_kernel,
        grid=(n_tiles,),
        in_specs=[full((n, LANES)), full((LANES, LANES)), full((1, LANES)), full((LANES, LANES)), full((1, LANES)),
                  pl.BlockSpec((LANES, LANES), lambda j: (0, j)),
                  pl.BlockSpec((LANES, LANES), lambda j: (0, n_tiles + j)),
                  pl.BlockSpec((1, LANES), lambda j: (0, j))],
        out_specs=[out, out],
        out_shape=[jax.ShapeDtypeStruct((n, width), F32)] * 2,
        compiler_params=_compiler_params(("arbitrary",)),
        name="hyena_filters",
    )(jnp.asarray(z), w1p, b1p, w2p, b2p, w3p, w3p, jnp.asarray(deltas))
    return jnp.concatenate([hf, hb], axis=1)


def _hyena_pre_kernel(yv_ref, y0_ref, y1_ref, yz_ref, swv_ref, sw0_ref, sw1_ref, sbv_ref, sb0_ref, sb1_ref,
                      yb_ref, p_ref, e_ref, pb_ref, *, n_ctx):
    s = yv_ref.shape[1]
    rows = MIX_CHUNK

    def body(j, carry):
        t0 = pl.multiple_of(n_ctx + j * rows, rows)
        conv = []
        for src, sw, sb in ((yv_ref, swv_ref, sbv_ref), (y0_ref, sw0_ref, sb0_ref), (y1_ref, sw1_ref, sb1_ref)):
            cur, down, up = _chunk_with_neighbors(src, t0, rows, s, n_ctx)
            conv.append(down * sw[0:1, :] + cur * sw[1:2, :] + up * sw[2:3, :] + sb[...])
        v, x0, x1 = conv
        p = x1 * v
        o0 = pl.multiple_of(j * rows, rows)
        p_ref[0, pl.ds(o0, rows), :] = p.astype(BF16)
        pb_ref[0, pl.ds(o0, rows), :] = p * yb_ref[...]
        e_ref[0, pl.ds(o0, rows), :] = x0 * _silu(yz_ref[0, pl.ds(t0, rows), :])
        return carry

    lax.fori_loop(0, (s - n_ctx) // rows, body, 0)


def _hyena_pre(u, col0, short_w, short_b, y_bias, n_ctx):
    bsz, s, _ = u.shape
    w = y_bias.shape[0]
    tiles = w // LANES
    base = col0 // LANES
    n = s - n_ctx
    col = lambda k: pl.BlockSpec((1, s, LANES), lambda b, j, k=k: (b, 0, base + k * tiles + j))
    par = lambda rows, k: pl.BlockSpec((rows, LANES), lambda b, j, k=k: (0, k * tiles + j))
    out = pl.BlockSpec((1, n, LANES), lambda b, j: (b, 0, j))
    sb = short_b.reshape(1, 3 * w)
    return pl.pallas_call(
        functools.partial(_hyena_pre_kernel, n_ctx=n_ctx),
        grid=(bsz, tiles),
        in_specs=[col(0), col(1), col(2), col(3),
                  par(3, 0), par(3, 1), par(3, 2), par(1, 0), par(1, 1), par(1, 2), par(1, 0)],
        out_specs=[out, out, out],
        out_shape=[jax.ShapeDtypeStruct((bsz, n, w), BF16),
                   jax.ShapeDtypeStruct((bsz, n, w), F32),
                   jax.ShapeDtypeStruct((bsz, n, w), F32)],
        compiler_params=_compiler_params(("parallel", "arbitrary")),
        name="hyena_short_conv",
    )(u, u, u, u, short_w, short_w, short_w, sb, sb, sb, y_bias.reshape(1, w))


def _dft_tables(n):
    big = 2 * n
    half = DFT_TILE // 2
    idx = jnp.arange(n, dtype=jnp.int32)
    ang = ((idx[:, None] * idx[None, :]) % big).astype(F32) * (2.0 * math.pi / big)
    cos = jnp.cos(ang)
    sin = jnp.sin(ang)
    alt = jnp.where(idx % 2 == 0, 1.0, -1.0).astype(F32)
    first_row = (idx == 0)[:, None]
    first_col = (idx == 0)[None, :]
    im = jnp.where(first_row, alt[None, :], -sin)
    fwd = jnp.stack([cos.reshape(n // half, half, n), im.reshape(n // half, half, n)], axis=1).reshape(big, n)
    wre = jnp.where(first_col, 1.0, 2.0) * cos * (1.0 / big)
    wim = jnp.where(first_col, alt[:, None], -2.0 * sin) * (1.0 / big)
    inv = jnp.stack([wre.reshape(n, n // half, half), wim.reshape(n, n // half, half)], axis=2).reshape(n, big)
    return fwd.astype(BF16), inv.astype(BF16)


def _spectrum_kernel(f_ref, x_ref, o_ref):
    o_ref[...] = jnp.dot(f_ref[...], x_ref[...].astype(BF16), preferred_element_type=F32)


def _filter_spectrum(fwd, hk):
    big, n = fwd.shape
    cols = hk.shape[1]
    return pl.pallas_call(
        _spectrum_kernel,
        grid=(big // DFT_TILE,),
        in_specs=[pl.BlockSpec((DFT_TILE, n), lambda i: (i, 0)),
                  pl.BlockSpec((n, cols), lambda i: (0, 0))],
        out_specs=pl.BlockSpec((DFT_TILE, cols), lambda i: (i, 0)),
        out_shape=jax.ShapeDtypeStruct((big, cols), F32),
        compiler_params=_compiler_params(("parallel",)),
        name="hyena_filter_spectrum",
    )(fwd, hk)


def _conv_spectrum_kernel(f_ref, p_ref, ks_ref, z_ref):
    i = pl.program_id(1)
    half = DFT_TILE // 2
    w = p_ref.shape[2]
    acc = jnp.dot(f_ref[...], p_ref[0], preferred_element_type=F32)
    s_re, s_im = acc[:half], acc[half:]
    ks = ks_ref[...]
    k_re = ks[:half, :w] + ks[:half, w:]
    k_im = ks[half:, :w] - ks[half:, w:]
    z_re = s_re * k_re - s_im * k_im
    z_im = s_re * k_im + s_im * k_re
    packed = jnp.logical_and(_iota((half, w), 0) == 0, i == 0)
    z_re = jnp.where(packed, s_re * k_re, z_re)
    z_im = jnp.where(packed, s_im * (ks[half:, :w] + ks[half:, w:]), z_im)
    z_ref[0, :half, :] = z_re.astype(BF16)
    z_ref[0, half:, :] = z_im.astype(BF16)


def _conv_spectrum(fwd, p16, kspec):
    bsz, n, w = p16.shape
    big = fwd.shape[0]
    return pl.pallas_call(
        _conv_spectrum_kernel,
        grid=(bsz, big // DFT_TILE),
        in_specs=[pl.BlockSpec((DFT_TILE, n), lambda b, i: (i, 0)),
                  pl.BlockSpec((1, n, w), lambda b, i: (b, 0, 0)),
                  pl.BlockSpec((DFT_TILE, 2 * w), lambda b, i: (i, 0))],
        out_specs=pl.BlockSpec((1, DFT_TILE, w), lambda b, i: (b, i, 0)),
        out_shape=jax.ShapeDtypeStruct((bsz, big, w), BF16),
        compiler_params=_compiler_params(("parallel", "arbitrary")),
        name="hyena_forward_dft",
    )(fwd, p16, kspec)


def _conv_inverse_kernel(g_ref, z_ref, e_ref, pb_ref, o_ref):
    y = jnp.dot(g_ref[...], z_ref[0], preferred_element_type=F32)
    o_ref[0] = e_ref[0] * (y + pb_ref[0])


def _conv_inverse(inv, z16, e, pb):
    bsz, big, w = z16.shape
    n = inv.shape[0]
    tile = DFT_TILE
    tok = pl.BlockSpec((1, tile, w), lambda b, i: (b, i, 0))
    return pl.pallas_call(
        _conv_inverse_kernel,
        grid=(bsz, n // tile),
        in_specs=[pl.BlockSpec((tile, big), lambda b, i: (i, 0)),
                  pl.BlockSpec((1, big, w), lambda b, i: (b, 0, 0)),
                  tok, tok],
        out_specs=tok,
        out_shape=jax.ShapeDtypeStruct((bsz, n, w), F32),
        compiler_params=_compiler_params(("parallel", "arbitrary")),
        name="hyena_inverse_dft",
    )(inv, z16, e, pb)


def _even_weight_layout(w_in, gate_b):
    d = w_in.shape[0]
    mw = MLSTM_N_HEADS * LANES
    g0 = 5 * mw
    g1 = g0 + 4 * MLSTM_N_HEADS
    main = jnp.concatenate([w_in[:, :g0], w_in[:, g1:]], axis=1).astype(BF16)
    wg = w_in[:, g0:g1].reshape(d, 2, 2, MLSTM_N_HEADS)
    wg = jnp.transpose(wg, (3, 1, 2, 0)).reshape(MLSTM_N_HEADS, 4, d)
    wg = jnp.concatenate([wg, jnp.zeros_like(wg)], axis=1).reshape(MLSTM_N_HEADS * 8, d).astype(BF16)
    gb = jnp.transpose(gate_b.reshape(2, 2, MLSTM_N_HEADS), (2, 0, 1)).reshape(MLSTM_N_HEADS, 4)
    gb = jnp.concatenate([gb, jnp.zeros_like(gb)], axis=1).reshape(MLSTM_N_HEADS * 8, 1)
    return main, wg, jnp.broadcast_to(gb, (MLSTM_N_HEADS * 8, LANES))


def _rwkv_params(mu, w0, w_up, a0, a_up, k_k, k_a, r_k, ln_w, ln_b):
    width = mu.shape[1]
    n_pairs = width // LANES
    row = lambda x: x.reshape(1, width)

    def cat_dirs(x):
        return jnp.transpose(x.reshape(2, n_pairs, LANES), (1, 0, 2)).reshape(n_pairs, 1, 2 * LANES)

    def block_up(x):
        lora = x.shape[1]
        xp = jnp.transpose(x.reshape(2, lora, n_pairs, LANES), (2, 0, 1, 3))
        z = jnp.zeros_like(xp[:, 0])
        top = jnp.concatenate([xp[:, 0], z], axis=2)
        bot = jnp.concatenate([z, xp[:, 1]], axis=2)
        return jnp.concatenate([top, bot], axis=1).astype(BF16)

    return {"mu": mu, "k_k": row(k_k), "k_a": row(k_a), "r_k": row(r_k), "ln_w": row(ln_w), "ln_b": row(ln_b),
            "w0": cat_dirs(w0), "a0": cat_dirs(a0), "w_up": block_up(w_up), "a_up": block_up(a_up)}


def _raster_to_column(h):
    b, n, d = h.shape
    rows = n // GRID_WIDTH
    return h.reshape(b, rows, GRID_WIDTH, d).transpose(0, 2, 1, 3).reshape(b, n, d)


def _column_to_raster(h):
    b, n, d = h.shape
    rows = n // GRID_WIDTH
    return h.reshape(b, GRID_WIDTH, rows, d).transpose(0, 2, 1, 3).reshape(b, n, d)


def kernel(x, c, ctx, c_ctx, l0_norm_g, l0_mod_w, l0_mod_b, l0_w_in, l0_w_out, l0_mlstm_conv_w, l0_mlstm_gate_b, l0_mlstm_norm_g, l0_rwkv_mu, l0_rwkv_w0, l0_rwkv_w_up, l0_rwkv_a0, l0_rwkv_a_up, l0_rwkv_k_k, l0_rwkv_k_a, l0_rwkv_r_k, l0_rwkv_ln_w, l0_rwkv_ln_b, hgrn_lower_bounds, l1_norm_g, l1_mod_w, l1_mod_b, l1_w_in, l1_w_out, l1_hgrn_norm_g, l1_hyena_short_w, l1_hyena_short_b, l1_hyena_w1, l1_hyena_b1, l1_hyena_w2, l1_hyena_b2, l1_hyena_w3, l1_hyena_bias, final_norm_g):
    bsz, n_lat, d = x.shape
    n_ctx = ctx.shape[1]

    pad = (-(bsz + 1)) % 8
    cc = jnp.concatenate([c, c_ctx[None], jnp.zeros((pad, d), F32)], axis=0)
    mod0, mod1 = _modulation(cc, l0_mod_w, l0_mod_b, l1_mod_w, l1_mod_b)
    mod0 = mod0[:bsz + 1].reshape(bsz + 1, 3, d)
    mod1 = mod1[:bsz + 1].reshape(bsz + 1, 3, d)

    xa = jnp.concatenate([ctx, x], axis=1)
    w_main, w_gate, b_gate = _even_weight_layout(l0_w_in, l0_mlstm_gate_b)
    n0 = w_main.shape[1]
    u0, gt0 = _proj_in(xa, l0_norm_g, mod0, w_main, n0 // 2, n_ctx, w_gate, b_gate)
    gt0 = gt0.reshape(bsz, MLSTM_N_HEADS, 8, n_ctx + n_lat)
    y_m = _mlstm(u0, gt0, l0_mlstm_conv_w, l0_mlstm_norm_g, n_ctx)
    rp = _rwkv_params(l0_rwkv_mu, l0_rwkv_w0, l0_rwkv_w_up, l0_rwkv_a0, l0_rwkv_a_up, l0_rwkv_k_k,
                      l0_rwkv_k_a, l0_rwkv_r_k, l0_rwkv_ln_w, l0_rwkv_ln_b)
    y_r = _rwkv(u0, 5 * MLSTM_N_HEADS * LANES, rp, n_ctx)
    xa1 = _proj_out(y_m, y_r, xa, mod0, l0_w_out.astype(BF16), n_ctx)

    x1c = _raster_to_column(xa1[:, n_ctx:])
    xa1c = jnp.concatenate([xa1[:, :n_ctx], x1c], axis=1)
    w1 = l1_w_in.astype(BF16)
    (u1,) = _proj_in(xa1c, l1_norm_g, mod1, w1, w1.shape[1] // 2, n_ctx)
    y_g = _hgrn(u1, hgrn_lower_bounds, l1_hgrn_norm_g, n_ctx, layer=1)
    hw = l1_hyena_bias.shape[0]
    hk = _hyena_filters(n_lat, l1_hyena_w1, l1_hyena_b1, l1_hyena_w2, l1_hyena_b2, l1_hyena_w3, hw)
    fwd, inv = _dft_tables(n_lat)
    kspec = _filter_spectrum(fwd, hk)
    p16, e, pb = _hyena_pre(u1, 5 * HGRN_N_HEADS * LANES, l1_hyena_short_w, l1_hyena_short_b, l1_hyena_bias, n_ctx)
    z16 = _conv_spectrum(fwd, p16, kspec)
    y_y = _conv_inverse(inv, z16, e, pb)
    out_c = _proj_out(y_g, y_y, x1c, mod1[:bsz + 1], l1_w_out.astype(BF16), 0, final_g=final_norm_g)
    return _column_to_raster(out_c)
```

```python
import functools
import math

import jax
import jax.numpy as jnp
import numpy as np
from jax import lax
from jax.experimental import pallas as pl
from jax.experimental.pallas import tpu as pltpu

F32 = jnp.float32
BF16 = jnp.bfloat16

GRID_WIDTH = 64
RMS_EPS = 1e-6
MLSTM_N_HEADS = 4
RWKV_HEAD = 64
RWKV_LN_EPSILON = 64e-5
HGRN_N_HEADS = 4
HYENA_N_BANDS = 16
HYENA_FAST = 0.3
HYENA_SLOW = 1.5
HYENA_TGT = 1e-2
HYENA_SHIFT = 0.05

LANES = 128
MXU_DIM = 256
VMEM_LIMIT = 52 * 1024 * 1024

MIX_CHUNK = 128
RWKV_CHUNK = 64
RWKV_GROUP = 4
MLSTM_GROUP = 3
HGRN_GROUP = 3
PROJ_ROWS = 768
DFT_TILE = 512


def _bdot(a, b):
    return jnp.dot(a.astype(BF16), b.astype(BF16), preferred_element_type=F32)


def _bdot_nt(a, b):
    return lax.dot_general(a.astype(BF16), b.astype(BF16), (((1,), (1,)), ((), ())),
                           preferred_element_type=F32)


def _bdot_tn(a, b):
    return lax.dot_general(a.astype(BF16), b.astype(BF16), (((0,), (0,)), ((), ())),
                           preferred_element_type=F32)


def _split3(x):
    hi = x.astype(BF16)
    r1 = x - hi.astype(F32)
    mid = r1.astype(BF16)
    lo = (r1 - mid.astype(F32)).astype(BF16)
    return hi, mid, lo


def _sel_dot(sel, x):
    hi, mid, lo = _split3(x)
    d = functools.partial(jnp.dot, preferred_element_type=F32)
    return d(sel, hi) + d(sel, mid) + d(sel, lo)


def _dot_sel(x, sel):
    hi, mid, lo = _split3(x)
    d = functools.partial(jnp.dot, preferred_element_type=F32)
    return d(hi, sel) + d(mid, sel) + d(lo, sel)


def _sigmoid(x):
    return 1.0 / (1.0 + jnp.exp(-x))


def _silu(x):
    return x * _sigmoid(x)


def _iota(shape, dim):
    return lax.broadcasted_iota(jnp.int32, shape, dim)


def _neighbor_rows(ref, t0, rows, n_total, n_ctx):
    has_prev = jnp.logical_and(t0 != 0, t0 != n_ctx)
    has_next = jnp.logical_and(t0 + rows != n_ctx, t0 + rows != n_total)
    prev = ref[0, pl.ds(jnp.maximum(t0 - 1, 0), 1), :]
    nxt = ref[0, pl.ds(jnp.minimum(t0 + rows, n_total - 1), 1), :]
    return jnp.where(has_prev, prev, 0.0), jnp.where(has_next, nxt, 0.0)


def _shifted(cur, prev_row, next_row):
    rows = cur.shape[0]
    rid = _iota(cur.shape, 0)
    down = jnp.where(rid == 0, prev_row, pltpu.roll(cur, 1, 0))
    up = jnp.where(rid == rows - 1, next_row, pltpu.roll(cur, rows - 1, 0))
    return down, up


def _chunk_with_neighbors(ref, t0, rows, n_total, n_ctx):
    cur = ref[0, pl.ds(t0, rows), :]
    prev_row, next_row = _neighbor_rows(ref, t0, rows, n_total, n_ctx)
    down, up = _shifted(cur, prev_row, next_row)
    return cur, down, up


def _bwd_chunk(i, n_chunks, n_ctx_chunks):
    return jnp.where(i < n_ctx_chunks, n_ctx_chunks - 1 - i, n_chunks - 1 + n_ctx_chunks - i)


def _compiler_params(semantics):
    return pltpu.CompilerParams(dimension_semantics=semantics, vmem_limit_bytes=VMEM_LIMIT)


def _mod_kernel(c_ref, w0_ref, b0_ref, w1_ref, b1_ref, o0_ref, o1_ref):
    s = _silu(c_ref[...])
    o0_ref[...] = _bdot(s, w0_ref[...]) + b0_ref[...]
    o1_ref[...] = _bdot(s, w1_ref[...]) + b1_ref[...]


def _modulation(cc, w0, b0, w1, b1):
    rows, d = cc.shape
    n = w0.shape[1]
    tile = d
    grid = (n // tile,)
    wspec = pl.BlockSpec((d, tile), lambda j: (0, j))
    bspec = pl.BlockSpec((1, tile), lambda j: (0, j))
    ospec = pl.BlockSpec((rows, tile), lambda j: (0, j))
    return pl.pallas_call(
        _mod_kernel,
        grid=grid,
        in_specs=[pl.BlockSpec((rows, d), lambda j: (0, 0)), wspec, bspec, wspec, bspec],
        out_specs=[ospec, ospec],
        out_shape=[jax.ShapeDtypeStruct((rows, n), F32)] * 2,
        compiler_params=_compiler_params(("arbitrary",)),
        name="adaln_modulation",
    )(cc, w0, b0.reshape(1, n), w1, b1.reshape(1, n))


def _proj_in_kernel(*refs, rows, n_ctx, with_gates):
    if with_gates:
        x_ref, g_ref, ml_ref, mc_ref, w_ref, wg_ref, gb_ref, u_ref, gt_ref, h_scr = refs
    else:
        x_ref, g_ref, ml_ref, mc_ref, w_ref, u_ref, h_scr = refs
    i = pl.program_id(1)
    n = pl.program_id(2)

    @pl.when(n == 0)
    def _():
        x = x_ref[0]
        y = x * lax.rsqrt(jnp.mean(x * x, axis=-1, keepdims=True) + RMS_EPS) * g_ref[...]
        row = i * rows + _iota((rows, 1), 0)
        is_ctx = row < n_ctx
        ml = ml_ref[0]
        mc = mc_ref[0]
        shift = jnp.where(is_ctx, mc[0:1], ml[0:1])
        scale = jnp.where(is_ctx, mc[1:2], ml[1:2])
        h = (y * (1.0 + scale) + shift).astype(BF16)
        h_scr[...] = h
        if with_gates:
            gt_ref[0] = _bdot_nt(wg_ref[...], h) + gb_ref[:, 0:1]

    u_ref[0] = jnp.dot(h_scr[...], w_ref[...], preferred_element_type=F32)


def _proj_in(xa, norm_g, mod3, w16, n_tile, n_ctx, gate_w=None, gate_b=None):
    bsz, s, d = xa.shape
    n = w16.shape[1]
    rows = PROJ_ROWS
    grid = (bsz, s // rows, n // n_tile)
    ctx_row = mod3.shape[0] - 1
    with_gates = gate_w is not None
    in_specs = [
        pl.BlockSpec((1, rows, d), lambda b, i, j: (b, i, 0)),
        pl.BlockSpec((1, d), lambda b, i, j: (0, 0)),
        pl.BlockSpec((1, 3, d), lambda b, i, j: (b, 0, 0)),
        pl.BlockSpec((1, 3, d), lambda b, i, j: (ctx_row, 0, 0)),
        pl.BlockSpec((d, n_tile), lambda b, i, j: (0, j)),
    ]
    args = [xa, norm_g.reshape(1, d), mod3, mod3, w16]
    out_specs = [pl.BlockSpec((1, rows, n_tile), lambda b, i, j: (b, i, j))]
    out_shape = [jax.ShapeDtypeStruct((bsz, s, n), F32)]
    if with_gates:
        ng = gate_w.shape[0]
        in_specs += [pl.BlockSpec((ng, d), lambda b, i, j: (0, 0)),
                     pl.BlockSpec((ng, LANES), lambda b, i, j: (0, 0))]
        args += [gate_w, gate_b]
        out_specs.append(pl.BlockSpec((1, ng, rows), lambda b, i, j: (b, 0, i)))
        out_shape.append(jax.ShapeDtypeStruct((bsz, ng, s), F32))
    return pl.pallas_call(
        functools.partial(_proj_in_kernel, rows=rows, n_ctx=n_ctx, with_gates=with_gates),
        grid=grid,
        in_specs=in_specs,
        out_specs=out_specs,
        out_shape=out_shape,
        scratch_shapes=[pltpu.VMEM((rows, d), BF16)],
        compiler_params=_compiler_params(("parallel", "arbitrary", "arbitrary")),
        name="norm_mod_proj_in",
    )(*args)


def _proj_out_kernel(ya_ref, yb_ref, x_ref, ml_ref, mc_ref, w_ref, fg_ref, o_ref, *, rows, n_ctx, final_norm):
    i = pl.program_id(1)
    half = ya_ref.shape[2]
    y = _bdot(ya_ref[0], w_ref[0:half, :]) + _bdot(yb_ref[0], w_ref[half:, :])
    row = i * rows + _iota((rows, 1), 0)
    gate = jnp.where(row < n_ctx, mc_ref[0][2:3], ml_ref[0][2:3])
    x = x_ref[0] + gate * y
    if final_norm:
        x = x * lax.rsqrt(jnp.mean(x * x, axis=-1, keepdims=True) + RMS_EPS) * fg_ref[...]
    o_ref[0] = x


def _proj_out(ya, yb, x, mod3, w16, n_ctx, final_g=None):
    bsz, s, d = x.shape
    half = ya.shape[2]
    rows = PROJ_ROWS if s % PROJ_ROWS == 0 else 512
    grid = (bsz, s // rows)
    ctx_row = mod3.shape[0] - 1
    final_norm = final_g is not None
    fg = final_g if final_norm else jnp.ones((d,), F32)
    tok = lambda w: pl.BlockSpec((1, rows, w), lambda b, i: (b, i, 0))
    return pl.pallas_call(
        functools.partial(_proj_out_kernel, rows=rows, n_ctx=n_ctx, final_norm=final_norm),
        grid=grid,
        in_specs=[tok(half), tok(half), tok(d),
                  pl.BlockSpec((1, 3, d), lambda b, i: (b, 0, 0)),
                  pl.BlockSpec((1, 3, d), lambda b, i: (ctx_row, 0, 0)),
                  pl.BlockSpec((2 * half, d), lambda b, i: (0, 0)),
                  pl.BlockSpec((1, d), lambda b, i: (0, 0))],
        out_specs=tok(d),
        out_shape=jax.ShapeDtypeStruct((bsz, s, d), F32),
        compiler_params=_compiler_params(("parallel", "arbitrary")),
        name="proj_out_residual",
    )(ya, yb, x, mod3, mod3, w16, fg.reshape(1, d))


def _cumsum_lanes(x, reverse):
    n = x.shape[-1]
    axis = x.ndim - 1
    lane = _iota(x.shape, axis)
    sh = 1
    while sh < n:
        if reverse:
            x = x + jnp.where(lane < n - sh, pltpu.roll(x, n - sh, axis), 0.0)
        else:
            x = x + jnp.where(lane >= sh, pltpu.roll(x, sh, axis), 0.0)
        sh *= 2
    return x


def _mlstm_chunk_operators(problems):
    t = problems[0][0].shape[0]
    r_i = _iota((t, t), 0)
    c_i = _iota((t, t), 1)
    pre = []
    for q, k, v_ext, gates, d in problems:
        reverse = d == 1
        log_f = jnp.minimum(gates, 0.0) - jnp.log1p(jnp.exp(-jnp.abs(gates)))
        cum = _cumsum_lanes(log_f, reverse)
        ig_row = gates[2 * d:2 * d + 1]
        b_row = cum[2 * d + 1:2 * d + 2]
        row_id = _iota(gates.shape, 0)
        stacked = jnp.concatenate([jnp.where(row_id % 2 == 0, gates, cum), jnp.zeros((t - 8, t), F32)], axis=0)
        cols = stacked.T
        ig_col = cols[:, 2 * d:2 * d + 1]
        b_col = cols[:, 2 * d + 1:2 * d + 2]
        mask = (c_i >= r_i) if reverse else (c_i <= r_i)
        logw = jnp.where(mask, b_col + (ig_row - b_row), -jnp.inf)
        mu = jnp.max(logw, axis=-1, keepdims=True)
        last = 0 if reverse else t - 1
        b_last = b_col[last:last + 1]
        gamma = mu[last:last + 1]
        g_col = jnp.exp(b_last - b_col + ig_col - gamma)
        pre.append(dict(w=jnp.exp(logw - mu), gk=g_col * k, b=b_col, mu=mu, b_last=b_last, gamma=gamma))
    qks = [_bdot_nt(p[0], p[1]) * x["w"] for p, x in zip(problems, pre)]
    intras = [_bdot(qk, p[2]) for qk, p in zip(qks, problems)]
    kvs = [_bdot_tn(x["gk"], p[2]) for x, p in zip(pre, problems)]
    return [(intra, kv, x["b"], x["mu"], x["b_last"], x["gamma"]) for intra, kv, x in zip(intras, kvs, pre)]


def _mlstm_kernel(q_ref, k_ref, v_ref, o_ref, z_ref, gt_ref, cwq_ref, cwk_ref, ng_ref, out_ref,
                  qa_scr, ka_scr, h_scr, intra_scr, kv_scr, b_scr, mu_scr, tail_scr, *, n_ctx):
    s = q_ref.shape[1]
    dh = q_ref.shape[2]
    t = MIX_CHUNK
    n_chunks = s // t
    n_ctx_chunks = n_ctx // t
    k_scale = dh ** -0.5

    def prep(j, carry):
        t0 = pl.multiple_of(j * t, t)
        for src, cw, dst, scale in ((q_ref, cwq_ref, qa_scr, 1.0), (k_ref, cwk_ref, ka_scr, k_scale)):
            cur, down, up = _chunk_with_neighbors(src, t0, t, s, n_ctx)
            conv = down * cw[0:1, :] + cur * cw[1:2, :] + up * cw[2:3, :]
            dst[pl.ds(t0, t), :] = _silu(conv) * scale
        return carry

    lax.fori_loop(0, n_chunks, prep, 0)

    ones_col = (_iota((t, dh), 1) == 0).astype(F32)

    def operators(gi, carry):
        problems, where = [], []
        for kk in range(MLSTM_GROUP):
            chunk = gi * MLSTM_GROUP + kk
            sl = pl.ds(pl.multiple_of(chunk * t, t), t)
            q, k = qa_scr[sl, :], ka_scr[sl, :]
            v_ext = jnp.concatenate([v_ref[0, sl, :], ones_col], axis=1)
            gates = gt_ref[0, 0, :, sl]
            for d in range(2):
                problems.append((q, k, v_ext, gates, d))
                where.append((d, chunk, sl))
        for (d, chunk, sl), (intra, kv, b, mu, b_last, gamma) in zip(where, _mlstm_chunk_operators(problems)):
            intra_scr[d, sl, :] = intra
            kv_scr[d, chunk] = kv
            b_scr[d, sl, :] = b
            mu_scr[d, sl, :] = mu
            tail_scr[d, chunk] = jnp.concatenate([jnp.broadcast_to(b_last, (1, dh)), jnp.broadcast_to(gamma, (1, dh))],
                                                 axis=0)
        return carry

    lax.fori_loop(0, n_chunks // MLSTM_GROUP, operators, 0)

    def scan(i, carry):
        chunks = (i, _bwd_chunk(i, n_chunks, n_ctx_chunks))
        sls = [pl.ds(pl.multiple_of(c * t, t), t) for c in chunks]
        inters = [_bdot(qa_scr[sl, :], c_ext) for sl, (c_ext, _) in zip(sls, carry)]
        new = []
        for d, (chunk, sl, inter, (c_ext, m)) in enumerate(zip(chunks, sls, inters, carry)):
            b = b_scr[d, sl, :]
            mu = mu_scr[d, sl, :]
            tail = tail_scr[d, chunk]
            b_last, gamma = tail[0:1, 0:1], tail[1:2, 0:1]
            m_t = jnp.maximum(b + m, mu)
            num = jnp.exp(b + m - m_t) * inter + jnp.exp(mu - m_t) * intra_scr[d, sl, :]
            den = num[:, dh:dh + 1]
            h_scr[d, sl, :] = num[:, :dh] / jnp.maximum(jnp.abs(den), jnp.exp(-m_t))
            m_new = jnp.maximum(b_last + m, gamma)
            new.append((jnp.exp(b_last + m - m_new) * c_ext + jnp.exp(gamma - m_new) * kv_scr[d, chunk], m_new))
        return tuple(new)

    zero = (jnp.zeros((dh, 2 * dh), F32), jnp.zeros((1, 1), F32))
    lax.fori_loop(0, n_chunks, scan, (zero, zero))

    def finish(j, carry):
        t0 = pl.multiple_of(j * t, t)
        h = h_scr[0, pl.ds(t0, t), :] + h_scr[1, pl.ds(t0, t), :]
        y = h * lax.rsqrt(jnp.mean(h * h, axis=-1, keepdims=True) + RMS_EPS) * ng_ref[...]
        out_ref[0, pl.ds(t0, t), :] = y * _sigmoid(o_ref[0, pl.ds(t0, t), :]) * _silu(z_ref[0, pl.ds(t0, t), :])
        return carry

    lax.fori_loop(0, n_chunks, finish, 0)


def _mlstm(u, gt, conv_w, norm_g, n_ctx):
    bsz, s, _ = u.shape
    nh = MLSTM_N_HEADS
    dh = LANES
    width = nh * dh
    n_chunks = s // MIX_CHUNK
    assert n_chunks % MLSTM_GROUP == 0 and n_ctx % MIX_CHUNK == 0
    col = lambda k: pl.BlockSpec((1, s, dh), lambda b, h, k=k: (b, 0, k * nh + h))
    par = lambda k: pl.BlockSpec((3, dh), lambda b, h, k=k: (0, k * nh + h))
    return pl.pallas_call(
        functools.partial(_mlstm_kernel, n_ctx=n_ctx),
        grid=(bsz, nh),
        in_specs=[col(0), col(1), col(2), col(3), col(4),
                  pl.BlockSpec((1, 1, 8, s), lambda b, h: (b, h, 0, 0)),
                  par(0), par(1),
                  pl.BlockSpec((1, dh), lambda b, h: (0, h))],
        out_specs=pl.BlockSpec((1, s, dh), lambda b, h: (b, 0, h)),
        out_shape=jax.ShapeDtypeStruct((bsz, s, width), F32),
        scratch_shapes=[pltpu.VMEM((s, dh), F32), pltpu.VMEM((s, dh), F32), pltpu.VMEM((2, s, dh), F32),
                        pltpu.VMEM((2, s, 2 * dh), F32), pltpu.VMEM((2, n_chunks, dh, 2 * dh), F32),
                        pltpu.VMEM((2, s, 1), F32), pltpu.VMEM((2, s, 1), F32),
                        pltpu.VMEM((2, n_chunks, 2, dh), F32)],
        compiler_params=_compiler_params(("parallel", "arbitrary")),
        name="mlstm_mixer",
    )(u, u, u, u, u, gt, conv_w, conv_w, norm_g.reshape(1, width))


def _head_stack(x, lane_lo):
    return jnp.concatenate([jnp.where(lane_lo, x, 0.0), jnp.where(lane_lo, 0.0, x)], axis=0)


def _rwkv_chunk_operators(problems, consts, eye, lane_lo):
    t, w = problems[0][0].shape
    n2 = 2 * t
    stack = lambda x: _head_stack(x, lane_lo)
    zeros = jnp.zeros((n2, w), F32)
    dirs = [p[6] for p in problems]
    cums = [_sel_dot(consts[d][0], p[3]) for p, d in zip(problems, dirs)]
    pre = []
    for (r, v, kk, lw, ka, kt, d), cum in zip(problems, cums):
        last = 0 if d == 1 else t - 1
        cum_end = cum[last:last + 1]
        e_inv = jnp.exp(-cum)
        e_end = jnp.exp(cum_end - cum)
        a_s = stack(-kk * jnp.exp(cum - lw))
        r_s = stack(r * jnp.exp(cum))
        pre.append(dict(a_s=a_s, r_s=r_s, vs=stack(v), g=jnp.exp(cum_end),
                        ar=jnp.concatenate([a_s, r_s], axis=0),
                        bk=jnp.concatenate([stack(ka * e_inv), stack(kt * e_inv)], axis=0),
                        bk_end=jnp.concatenate([stack(ka * e_end), stack(kt * e_end)], axis=0)))
    m_alls = [_bdot_nt(q["ar"], q["bk"]) for q in pre]
    m_abs = [jnp.where(consts[d][1], m[:n2, :n2], 0.0) for m, d in zip(m_alls, dirs)]
    m_aks = [jnp.where(consts[d][1], m[:n2, n2:], 0.0) for m, d in zip(m_alls, dirs)]
    m_lows = [jnp.where(consts[d][2], m[n2:, :], 0.0) for m, d in zip(m_alls, dirs)]
    invs = [eye + jnp.where(consts[d][3][0], m, 0.0) for m, d in zip(m_abs, dirs)]
    for level in range(1, len(consts[0][3])):
        inner = [_bdot(jnp.where(consts[d][3][level], m, 0.0), x) for m, x, d in zip(m_abs, invs, dirs)]
        invs = [x + _bdot(x, y) for x, y in zip(invs, inner)]
    mv = [_bdot(m, q["vs"]) for m, q in zip(m_aks, pre)]
    solved = [_bdot(x, jnp.concatenate([q["a_s"], y], axis=1)) for x, q, y in zip(invs, pre, mv)]
    zms = [jnp.concatenate([sv, jnp.concatenate([zeros, q["vs"]], axis=1)], axis=0) for sv, q in zip(solved, pre)]
    ry1s = [jnp.concatenate([q["r_s"], zeros], axis=1) + _bdot(m, z) for q, m, z in zip(pre, m_lows, zms)]
    pqs = [_bdot_tn(z, q["bk_end"]) for z, q in zip(zms, pre)]
    out = []
    for ry1, pq, q in zip(ry1s, pqs, pre):
        folded = ry1[:t] + ry1[t:]
        out.append((folded[:, :w], folded[:, w:], pq[:w], pq[w:], q["g"]))
    return out


def _rwkv_kernel(rr_ref, rk_ref, rv_ref, rz_ref, wd_ref, ad_ref, mu_ref, kk_ref, ka_ref, rkk_ref,
                 lnw_ref, lnb_ref, w0_ref, a0_ref, wup_ref, aup_ref, out_ref,
                 r_scr, v_scr, kk_scr, lw_scr, ka_scr, kt_scr, bonus_scr, y_scr, ry_scr, pt_scr, qt_scr, g_scr,
                 *, n_ctx):
    s = rr_ref.shape[1]
    w = rr_ref.shape[2]
    p_rows = MIX_CHUNK
    t = RWKV_CHUNK
    n_chunks = s // t
    n_ctx_chunks = n_ctx // t
    head_sum = ((_iota((w, w), 0) // RWKV_HEAD) == (_iota((w, w), 1) // RWKV_HEAD)).astype(BF16)
    inv_head = 1.0 / RWKV_HEAD

    def prep(j, carry):
        t0 = pl.multiple_of(j * p_rows, p_rows)
        mixed = []
        for idx, src in enumerate((rr_ref, rk_ref, rv_ref)):
            cur, down, up = _chunk_with_neighbors(src, t0, p_rows, s, n_ctx)
            mixed.append(cur + mu_ref[idx:idx + 1, :] * (0.5 * (down + up) - cur))
        r, kr, v = mixed
        kk = kr * kk_ref[...]
        norm = jnp.sqrt(_dot_sel(kk * kk, head_sum))
        kk = kk / jnp.maximum(norm, 1e-12)
        w_raw = _bdot(jnp.tanh(wd_ref[0, pl.ds(t0, p_rows), :]), wup_ref[0]) + w0_ref[0]
        a = _sigmoid(_bdot(ad_ref[0, pl.ds(t0, p_rows), :], aup_ref[0]) + a0_ref[0])
        lw = -math.exp(-0.5) * _sigmoid(w_raw)
        kt_sum = jnp.zeros_like(kr)
        for d in range(2):
            a_d = a[:, d * w:(d + 1) * w]
            kt_d = kr * (1.0 + (a_d - 1.0) * ka_ref[...])
            kt_sum = kt_sum + kt_d
            lw_scr[d, pl.ds(t0, p_rows), :] = lw[:, d * w:(d + 1) * w]
            ka_scr[d, pl.ds(t0, p_rows), :] = kk * a_d
            kt_scr[d, pl.ds(t0, p_rows), :] = kt_d
        coef = _dot_sel(r * kt_sum * rkk_ref[...], head_sum)
        r_scr[pl.ds(t0, p_rows), :] = r
        v_scr[pl.ds(t0, p_rows), :] = v
        kk_scr[pl.ds(t0, p_rows), :] = kk
        bonus_scr[pl.ds(t0, p_rows), :] = coef * v
        return carry

    lax.fori_loop(0, s // p_rows, prep, 0)

    n2 = 2 * t
    r_i = _iota((n2, n2), 0)
    c_i = _iota((n2, n2), 1)
    same = (r_i // t) == (c_i // t)
    rt = r_i % t
    ct = c_i % t
    eye = (r_i == c_i).astype(F32)
    lane_lo = _iota((t, w), 1) < RWKV_HEAD
    tri_r = _iota((t, t), 0)
    tri_c = _iota((t, t), 1)
    consts = []
    for reverse in (False, True):
        strict = jnp.logical_and(same, (ct > rt) if reverse else (ct < rt))
        incl = jnp.logical_and(same, (ct >= rt) if reverse else (ct <= rt))
        tri = ((tri_c >= tri_r) if reverse else (tri_c <= tri_r)).astype(BF16)
        merges = []
        c = 1
        while c < t:
            hi_r = (r_i % (2 * c)) >= c
            hi_c = (c_i % (2 * c)) >= c
            cross = jnp.logical_and(hi_c, jnp.logical_not(hi_r)) if reverse else jnp.logical_and(hi_r, jnp.logical_not(hi_c))
            merges.append(jnp.logical_and((r_i // (2 * c)) == (c_i // (2 * c)), cross))
            c *= 2
        consts.append((tri, strict, jnp.concatenate([incl, incl], axis=1), merges))

    def operators(gi, carry):
        problems, where = [], []
        for k in range(RWKV_GROUP):
            chunk = gi * RWKV_GROUP + k
            sl = pl.ds(pl.multiple_of(chunk * t, t), t)
            r, v, kk = r_scr[sl, :], v_scr[sl, :], kk_scr[sl, :]
            for d in range(2):
                problems.append((r, v, kk, lw_scr[d, sl, :], ka_scr[d, sl, :], kt_scr[d, sl, :], d))
                where.append((d, chunk, sl))
        for (d, chunk, sl), (ry, y1, pt, qt, g) in zip(where, _rwkv_chunk_operators(problems, consts, eye, lane_lo)):
            ry_scr[d, sl, :] = ry.astype(BF16)
            y_scr[d, sl, :] = y1
            pt_scr[d, chunk] = pt.astype(BF16)
            qt_scr[d, chunk] = qt
            g_scr[d, chunk] = g
        return carry

    lax.fori_loop(0, n_chunks // RWKV_GROUP, operators, 0)

    def scan(i, carry):
        states = []
        for d, (chunk, ht) in enumerate(zip((i, _bwd_chunk(i, n_chunks, n_ctx_chunks)), carry)):
            sl = pl.ds(pl.multiple_of(chunk * t, t), t)
            y_scr[d, sl, :] = _bdot_nt(ry_scr[d, sl, :], ht) + y_scr[d, sl, :]
            states.append(ht * g_scr[d, chunk] + _bdot(ht, pt_scr[d, chunk]) + qt_scr[d, chunk])
        return tuple(states)

    zero_state = jnp.zeros((w, w), F32)
    lax.fori_loop(0, n_chunks, scan, (zero_state, zero_state))

    def finish(j, carry):
        t0 = pl.multiple_of(j * p_rows, p_rows)
        sl = pl.ds(t0, p_rows)
        y = y_scr[0, sl, :] + y_scr[1, sl, :] + bonus_scr[sl, :]
        mu = _dot_sel(y, head_sum) * inv_head
        yc = y - mu
        var = _dot_sel(yc * yc, head_sum) * inv_head
        yn = yc * lax.rsqrt(var + RWKV_LN_EPSILON) * lnw_ref[...] + lnb_ref[...]
        out_ref[0, sl, :] = yn * _silu(rz_ref[0, sl, :])
        return carry

    lax.fori_loop(0, s // p_rows, finish, 0)


def _rwkv(u, col0, p, n_ctx):
    bsz, s, _ = u.shape
    w = LANES
    width = p["mu"].shape[1]
    n_pairs = width // w
    base = col0 // w
    col = lambda k: pl.BlockSpec((1, s, w), lambda b, h, k=k: (b, 0, base + k * n_pairs + h))
    lora = lambda k: pl.BlockSpec((1, s, w), lambda b, h, k=k: (b, 0, base + 4 * n_pairs + k))
    vec = lambda rows: pl.BlockSpec((rows, w), lambda b, h: (0, h))
    cat = pl.BlockSpec((1, 1, 2 * w), lambda b, h: (h, 0, 0))
    up = pl.BlockSpec((1, w, 2 * w), lambda b, h: (h, 0, 0))
    seq = pltpu.VMEM((s, w), F32)
    seq2 = pltpu.VMEM((2, s, w), F32)
    n_chunks = s // RWKV_CHUNK
    assert n_chunks % RWKV_GROUP == 0 and n_ctx % RWKV_CHUNK == 0
    operators = [pltpu.VMEM((2, s, w), BF16), pltpu.VMEM((2, n_chunks, w, w), BF16),
                 pltpu.VMEM((2, n_chunks, w, w), F32), pltpu.VMEM((2, n_chunks, 1, w), F32)]
    return pl.pallas_call(
        functools.partial(_rwkv_kernel, n_ctx=n_ctx),
        grid=(bsz, n_pairs),
        in_specs=[col(0), col(1), col(2), col(3), lora(0), lora(1),
                  vec(3), vec(1), vec(1), vec(1), vec(1), vec(1), cat, cat, up, up],
        out_specs=pl.BlockSpec((1, s, w), lambda b, h: (b, 0, h)),
        out_shape=jax.ShapeDtypeStruct((bsz, s, width), F32),
        scratch_shapes=[seq, seq, seq, seq2, seq2, seq2, seq, seq2] + operators,
        compiler_params=_compiler_params(("parallel", "arbitrary")),
        name="rwkv7_mixer",
    )(u, u, u, u, u, u, p["mu"], p["k_k"], p["k_a"], p["r_k"], p["ln_w"], p["ln_b"],
      p["w0"], p["a0"], p["w_up"], p["a_up"])


def _hgrn_chunk_operators(problems, lb, tris):
    t, w = problems[0][0].shape
    rid = _iota((t, w), 0)
    r_i = _iota((t, t), 0)
    c_i = _iota((t, t), 1)
    zero_row = jnp.zeros((1, w), F32)
    dirs = [p[3] for p in problems]
    lgs, ks = [], []
    for q, v, ff, d in problems:
        e = jnp.exp(-jnp.abs(ff))
        big = 1.0 / (1.0 + e)
        small = e / (1.0 + e)
        pos = ff >= 0.0
        lgs.append(jnp.log(lb + (1.0 - lb) * jnp.where(pos, big, small)))
        ks.append((1.0 - lb) * jnp.where(pos, small, big))
    bs = [_sel_dot(tris[d], lg) for lg, d in zip(lgs, dirs)]
    befores = [_shifted(b, zero_row, zero_row)[1 if d == 1 else 0] for b, d in zip(bs, dirs)]
    edges = list(bs)
    accs = [jnp.where(r_i == c_i, jnp.sum(p[0] * k, axis=-1, keepdims=True), 0.0) for p, k in zip(problems, ks)]
    c = 1
    while c < t:
        odd = (rid // c) % 2 == 1
        upper = (rid % (2 * c)) >= c
        same_block = (r_i // (2 * c)) == (c_i // (2 * c))
        qts, kts = [], []
        for p, k, b, before, edge, d in zip(problems, ks, bs, befores, edges, dirs):
            q_side = jnp.logical_not(odd) if d == 1 else odd
            qts.append(jnp.where(q_side, p[0] * jnp.exp(b - before), 0.0))
            kts.append(jnp.where(q_side, 0.0, k * jnp.exp(edge - b)))
        prods = [_bdot_nt(qt, kt) for qt, kt in zip(qts, kts)]
        accs = [a + jnp.where(same_block, pr, 0.0) for a, pr in zip(accs, prods)]
        for i, d in enumerate(dirs):
            if d == 1:
                befores[i] = jnp.where(upper, befores[i], pltpu.roll(befores[i], t - c, 0))
                edges[i] = jnp.where(upper, pltpu.roll(edges[i], c, 0), edges[i])
            else:
                befores[i] = jnp.where(upper, pltpu.roll(befores[i], c, 0), befores[i])
                edges[i] = jnp.where(upper, edges[i], pltpu.roll(edges[i], t - c, 0))
        c *= 2
    o_intras = [_bdot(a, p[1]) for a, p in zip(accs, problems)]
    b_ends = [b[(0 if d == 1 else t - 1):(1 if d == 1 else t)] for b, d in zip(bs, dirs)]
    kvs = [_bdot_tn(p[1], k * jnp.exp(be - b)) for p, k, b, be in zip(problems, ks, bs, b_ends)]
    return [(p[0] * jnp.exp(b), oi, kv, jnp.exp(be)) for p, b, oi, kv, be in zip(problems, bs, o_intras, kvs, b_ends)]


def _hgrn_kernel(q_ref, i_ref, ff_ref, fb_ref, z_ref, lb_ref, ng_ref, out_ref, o_scr, qe_scr, kv_scr, g_scr,
                 *, n_ctx, layer):
    s = q_ref.shape[1]
    dh = q_ref.shape[2]
    t = MIX_CHUNK
    n_chunks = s // t
    n_ctx_chunks = n_ctx // t
    lbs = lb_ref[...]
    ex = jnp.exp(lbs - jnp.max(lbs, axis=0, keepdims=True))
    probs = ex / jnp.sum(ex, axis=0, keepdims=True)
    csum = probs[0:1]
    for l in range(1, layer + 1):
        csum = csum + probs[l:l + 1]
    lb = csum - probs[0:1]
    tri_r = _iota((t, t), 0)
    tri_c = _iota((t, t), 1)
    tris = [(tri_c <= tri_r).astype(BF16), (tri_c >= tri_r).astype(BF16)]

    def operators(gi, carry):
        problems, where = [], []
        for kk in range(HGRN_GROUP):
            chunk = gi * HGRN_GROUP + kk
            sl = pl.ds(pl.multiple_of(chunk * t, t), t)
            q, v = q_ref[0, sl, :], i_ref[0, sl, :]
            for d, f_ref in enumerate((ff_ref, fb_ref)):
                problems.append((q, v, f_ref[0, sl, :], d))
                where.append((d, chunk, sl))
        for (d, chunk, sl), (qe, o_intra, kv, g) in zip(where, _hgrn_chunk_operators(problems, lb, tris)):
            qe_scr[d, sl, :] = qe.astype(BF16)
            o_scr[d, sl, :] = o_intra
            kv_scr[d, chunk] = kv
            g_scr[d, chunk] = g
        return carry

    lax.fori_loop(0, n_chunks // HGRN_GROUP, operators, 0)

    def scan(i, carry):
        chunks = (i, _bwd_chunk(i, n_chunks, n_ctx_chunks))
        sls = [pl.ds(pl.multiple_of(c * t, t), t) for c in chunks]
        inters = [_bdot_nt(qe_scr[d, sl, :], st) for d, (sl, st) in enumerate(zip(sls, carry))]
        new = []
        for d, (chunk, sl, inter, st) in enumerate(zip(chunks, sls, inters, carry)):
            o_scr[d, sl, :] = o_scr[d, sl, :] + inter
            new.append(st * g_scr[d, chunk] + kv_scr[d, chunk])
        return tuple(new)

    zero_state = jnp.zeros((dh, dh), F32)
    lax.fori_loop(0, n_chunks, scan, (zero_state, zero_state))

    def finish(j, carry):
        t0 = pl.multiple_of(n_ctx + j * t, t)
        sl = pl.ds(t0, t)
        o = o_scr[0, sl, :] + o_scr[1, sl, :]
        y = o * lax.rsqrt(jnp.mean(o * o, axis=-1, keepdims=True) + RMS_EPS) * ng_ref[...]
        out_ref[0, pl.ds(pl.multiple_of(j * t, t), t), :] = y * _silu(z_ref[0, sl, :])
        return carry

    lax.fori_loop(0, n_chunks - n_ctx_chunks, finish, 0)


def _hgrn(u, lb_all, norm_g, n_ctx, layer):
    bsz, s, _ = u.shape
    nh = HGRN_N_HEADS
    dh = LANES
    width = nh * dh
    depth = lb_all.shape[0]
    n_chunks = s // MIX_CHUNK
    assert n_chunks % HGRN_GROUP == 0 and n_ctx % MIX_CHUNK == 0
    col = lambda k: pl.BlockSpec((1, s, dh), lambda b, h, k=k: (b, 0, k * nh + h))
    return pl.pallas_call(
        functools.partial(_hgrn_kernel, n_ctx=n_ctx, layer=layer),
        grid=(bsz, nh),
        in_specs=[col(0), col(1), col(2), col(3), col(4),
                  pl.BlockSpec((depth, dh), lambda b, h: (0, h)),
                  pl.BlockSpec((1, dh), lambda b, h: (0, h))],
        out_specs=pl.BlockSpec((1, s - n_ctx, dh), lambda b, h: (b, 0, h)),
        out_shape=jax.ShapeDtypeStruct((bsz, s - n_ctx, width), F32),
        scratch_shapes=[pltpu.VMEM((2, s, dh), F32), pltpu.VMEM((2, s, dh), BF16),
                        pltpu.VMEM((2, n_chunks, dh, dh), F32), pltpu.VMEM((2, n_chunks, 1, dh), F32)],
        compiler_params=_compiler_params(("parallel", "arbitrary")),
        name="hgrn2_mixer",
    )(u, u, u, u, u, lb_all, norm_g.reshape(1, width))


def _hyena_filter_kernel(z_ref, w1_ref, b1_ref, w2_ref, b2_ref, w3f_ref, w3b_ref, dl_ref, hf_ref, hb_ref):
    hp = functools.partial(jnp.dot, precision=lax.Precision.HIGHEST, preferred_element_type=F32)
    n = z_ref.shape[0]
    hid = jnp.sin(hp(z_ref[...], w1_ref[...]) + b1_ref[...])
    hid = jnp.sin(hp(hid, w2_ref[...]) + b2_ref[...])
    pos = _iota((n, 1), 0).astype(F32) * (1.0 / n)
    window = jnp.exp(-pos * dl_ref[...]) + HYENA_SHIFT
    f0 = hp(hid, w3f_ref[...]) * window
    f1 = hp(hid, w3b_ref[...]) * window
    nrm = jnp.sum(jnp.abs(f0), axis=0, keepdims=True) + jnp.sum(jnp.abs(f1), axis=0, keepdims=True)
    hf_ref[...] = f0 / nrm
    hb_ref[...] = f1 / nrm


def _hyena_filters(n, w1, b1, w2, b2, w3, width):
    pos = np.arange(n, dtype=np.float64)
    bands = np.linspace(1e-4, HYENA_N_BANDS - 1, HYENA_N_BANDS)
    ang = (2.0 * math.pi / n) * pos[:, None] * bands
    z = np.concatenate([(pos / n)[:, None], np.cos(ang), np.sin(ang)], axis=-1)
    z = np.pad(z, ((0, 0), (0, LANES - z.shape[1]))).astype(np.float32)
    max_decay = math.log(HYENA_TGT) / HYENA_FAST
    min_decay = math.log(HYENA_TGT) / HYENA_SLOW
    deltas = np.abs(np.linspace(min_decay, max_decay, width)).astype(np.float32)[None]
    feat, hid = w1.shape
    w1p = jnp.pad(w1, ((0, LANES - feat), (0, LANES - hid)))
    w2p = jnp.pad(w2, ((0, LANES - hid), (0, LANES - hid)))
    w3p = jnp.pad(w3, ((0, LANES - hid), (0, 0)))
    b1p = jnp.pad(b1, (0, LANES - hid)).reshape(1, LANES)
    b2p = jnp.pad(b2, (0, LANES - hid)).reshape(1, LANES)
    n_tiles = width // LANES
    full = lambda shape: pl.BlockSpec(shape, lambda j: (0, 0))
    out = pl.BlockSpec((n, LANES), lambda j: (0, j))
    hf, hb = pl.pallas_call(
        _hyena_filter_kernel,
        grid=(n_tiles,),
        in_specs=[full((n, LANES)), full((LANES, LANES)), full((1, LANES)), full((LANES, LANES)), full((1, LANES)),
                  pl.BlockSpec((LANES, LANES), lambda j: (0, j)),
                  pl.BlockSpec((LANES, LANES), lambda j: (0, n_tiles + j)),
                  pl.BlockSpec((1, LANES), lambda j: (0, j))],
        out_specs=[out, out],
        out_shape=[jax.ShapeDtypeStruct((n, width), F32)] * 2,
        compiler_params=_compiler_params(("arbitrary",)),
        name="hyena_filters",
    )(jnp.asarray(z), w1p, b1p, w2p, b2p, w3p, w3p, jnp.asarray(deltas))
    return jnp.concatenate([hf, hb], axis=1)


def _hyena_pre_kernel(yv_ref, y0_ref, y1_ref, yz_ref, swv_ref, sw0_ref, sw1_ref, sbv_ref, sb0_ref, sb1_ref,
                      yb_ref, p_ref, e_ref, pb_ref, *, n_ctx):
    s = yv_ref.shape[1]
    rows = MIX_CHUNK

    def body(j, carry):
        t0 = pl.multiple_of(n_ctx + j * rows, rows)
        conv = []
        for src, sw, sb in ((yv_ref, swv_ref, sbv_ref), (y0_ref, sw0_ref, sb0_ref), (y1_ref, sw1_ref, sb1_ref)):
            cur, down, up = _chunk_with_neighbors(src, t0, rows, s, n_ctx)
            conv.append(down * sw[0:1, :] + cur * sw[1:2, :] + up * sw[2:3, :] + sb[...])
        v, x0, x1 = conv
        p = x1 * v
        o0 = pl.multiple_of(j * rows, rows)
        p_ref[0, pl.ds(o0, rows), :] = p.astype(BF16)
        pb_ref[0, pl.ds(o0, rows), :] = p * yb_ref[...]
        e_ref[0, pl.ds(o0, rows), :] = x0 * _silu(yz_ref[0, pl.ds(t0, rows), :])
        return carry

    lax.fori_loop(0, (s - n_ctx) // rows, body, 0)


def _hyena_pre(u, col0, short_w, short_b, y_bias, n_ctx):
    bsz, s, _ = u.shape
    w = y_bias.shape[0]
    tiles = w // LANES
    base = col0 // LANES
    n = s - n_ctx
    col = lambda k: pl.BlockSpec((1, s, LANES), lambda b, j, k=k: (b, 0, base + k * tiles + j))
    par = lambda rows, k: pl.BlockSpec((rows, LANES), lambda b, j, k=k: (0, k * tiles + j))
    out = pl.BlockSpec((1, n, LANES), lambda b, j: (b, 0, j))
    sb = short_b.reshape(1, 3 * w)
    return pl.pallas_call(
        functools.partial(_hyena_pre_kernel, n_ctx=n_ctx),
        grid=(bsz, tiles),
        in_specs=[col(0), col(1), col(2), col(3),
                  par(3, 0), par(3, 1), par(3, 2), par(1, 0), par(1, 1), par(1, 2), par(1, 0)],
        out_specs=[out, out, out],
        out_shape=[jax.ShapeDtypeStruct((bsz, n, w), BF16),
                   jax.ShapeDtypeStruct((bsz, n, w), F32),
                   jax.ShapeDtypeStruct((bsz, n, w), F32)],
        compiler_params=_compiler_params(("parallel", "arbitrary")),
        name="hyena_short_conv",
    )(u, u, u, u, short_w, short_w, short_w, sb, sb, sb, y_bias.reshape(1, w))


def _dft_tables(n):
    big = 2 * n
    half = DFT_TILE // 2
    idx = jnp.arange(n, dtype=jnp.int32)
    ang = ((idx[:, None] * idx[None, :]) % big).astype(F32) * (2.0 * math.pi / big)
    cos = jnp.cos(ang)
    sin = jnp.sin(ang)
    alt = jnp.where(idx % 2 == 0, 1.0, -1.0).astype(F32)
    first_row = (idx == 0)[:, None]
    first_col = (idx == 0)[None, :]
    im = jnp.where(first_row, alt[None, :], -sin)
    fwd = jnp.stack([cos.reshape(n // half, half, n), im.reshape(n // half, half, n)], axis=1).reshape(big, n)
    wre = jnp.where(first_col, 1.0, 2.0) * cos * (1.0 / big)
    wim = jnp.where(first_col, alt[:, None], -2.0 * sin) * (1.0 / big)
    inv = jnp.stack([wre.reshape(n, n // half, half), wim.reshape(n, n // half, half)], axis=2).reshape(n, big)
    return fwd.astype(BF16), inv.astype(BF16)


def _spectrum_kernel(f_ref, x_ref, o_ref):
    o_ref[...] = jnp.dot(f_ref[...], x_ref[...].astype(BF16), preferred_element_type=F32)


def _filter_spectrum(fwd, hk):
    big, n = fwd.shape
    cols = hk.shape[1]
    return pl.pallas_call(
        _spectrum_kernel,
        grid=(big // DFT_TILE,),
        in_specs=[pl.BlockSpec((DFT_TILE, n), lambda i: (i, 0)),
                  pl.BlockSpec((n, cols), lambda i: (0, 0))],
        out_specs=pl.BlockSpec((DFT_TILE, cols), lambda i: (i, 0)),
        out_shape=jax.ShapeDtypeStruct((big, cols), F32),
        compiler_params=_compiler_params(("parallel",)),
        name="hyena_filter_spectrum",
    )(fwd, hk)


def _conv_spectrum_kernel(f_ref, p_ref, ks_ref, z_ref):
    i = pl.program_id(1)
    half = DFT_TILE // 2
    w = p_ref.shape[2]
    acc = jnp.dot(f_ref[...], p_ref[0], preferred_element_type=F32)
    s_re, s_im = acc[:half], acc[half:]
    ks = ks_ref[...]
    k_re = ks[:half, :w] + ks[:half, w:]
    k_im = ks[half:, :w] - ks[half:, w:]
    z_re = s_re * k_re - s_im * k_im
    z_im = s_re * k_im + s_im * k_re
    packed = jnp.logical_and(_iota((half, w), 0) == 0, i == 0)
    z_re = jnp.where(packed, s_re * k_re, z_re)
    z_im = jnp.where(packed, s_im * (ks[half:, :w] + ks[half:, w:]), z_im)
    z_ref[0, :half, :] = z_re.astype(BF16)
    z_ref[0, half:, :] = z_im.astype(BF16)


def _conv_spectrum(fwd, p16, kspec):
    bsz, n, w = p16.shape
    big = fwd.shape[0]
    return pl.pallas_call(
        _conv_spectrum_kernel,
        grid=(bsz, big // DFT_TILE),
        in_specs=[pl.BlockSpec((DFT_TILE, n), lambda b, i: (i, 0)),
                  pl.BlockSpec((1, n, w), lambda b, i: (b, 0, 0)),
                  pl.BlockSpec((DFT_TILE, 2 * w), lambda b, i: (i, 0))],
        out_specs=pl.BlockSpec((1, DFT_TILE, w), lambda b, i: (b, i, 0)),
        out_shape=jax.ShapeDtypeStruct((bsz, big, w), BF16),
        compiler_params=_compiler_params(("parallel", "arbitrary")),
        name="hyena_forward_dft",
    )(fwd, p16, kspec)


def _conv_inverse_kernel(g_ref, z_ref, e_ref, pb_ref, o_ref):
    y = jnp.dot(g_ref[...], z_ref[0], preferred_element_type=F32)
    o_ref[0] = e_ref[0] * (y + pb_ref[0])


def _conv_inverse(inv, z16, e, pb):
    bsz, big, w = z16.shape
    n = inv.shape[0]
    tile = DFT_TILE
    tok = pl.BlockSpec((1, tile, w), lambda b, i: (b, i, 0))
    return pl.pallas_call(
        _conv_inverse_kernel,
        grid=(bsz, n // tile),
        in_specs=[pl.BlockSpec((tile, big), lambda b, i: (i, 0)),
                  pl.BlockSpec((1, big, w), lambda b, i: (b, 0, 0)),
                  tok, tok],
        out_specs=tok,
        out_shape=jax.ShapeDtypeStruct((bsz, n, w), F32),
        compiler_params=_compiler_params(("parallel", "arbitrary")),
        name="hyena_inverse_dft",
    )(inv, z16, e, pb)


def _even_weight_layout(w_in, gate_b):
    d = w_in.shape[0]
    mw = MLSTM_N_HEADS * LANES
    g0 = 5 * mw
    g1 = g0 + 4 * MLSTM_N_HEADS
    main = jnp.concatenate([w_in[:, :g0], w_in[:, g1:]], axis=1).astype(BF16)
    wg = w_in[:, g0:g1].reshape(d, 2, 2, MLSTM_N_HEADS)
    wg = jnp.transpose(wg, (3, 1, 2, 0)).reshape(MLSTM_N_HEADS, 4, d)
    wg = jnp.concatenate([wg, jnp.zeros_like(wg)], axis=1).reshape(MLSTM_N_HEADS * 8, d).astype(BF16)
    gb = jnp.transpose(gate_b.reshape(2, 2, MLSTM_N_HEADS), (2, 0, 1)).reshape(MLSTM_N_HEADS, 4)
    gb = jnp.concatenate([gb, jnp.zeros_like(gb)], axis=1).reshape(MLSTM_N_HEADS * 8, 1)
    return main, wg, jnp.broadcast_to(gb, (MLSTM_N_HEADS * 8, LANES))


def _rwkv_params(mu, w0, w_up, a0, a_up, k_k, k_a, r_k, ln_w, ln_b):
    width = mu.shape[1]
    n_pairs = width // LANES
    row = lambda x: x.reshape(1, width)

    def cat_dirs(x):
        return jnp.transpose(x.reshape(2, n_pairs, LANES), (1, 0, 2)).reshape(n_pairs, 1, 2 * LANES)

    def block_up(x):
        lora = x.shape[1]
        xp = jnp.transpose(x.reshape(2, lora, n_pairs, LANES), (2, 0, 1, 3))
        z = jnp.zeros_like(xp[:, 0])
        top = jnp.concatenate([xp[:, 0], z], axis=2)
        bot = jnp.concatenate([z, xp[:, 1]], axis=2)
        return jnp.concatenate([top, bot], axis=1).astype(BF16)

    return {"mu": mu, "k_k": row(k_k), "k_a": row(k_a), "r_k": row(r_k), "ln_w": row(ln_w), "ln_b": row(ln_b),
            "w0": cat_dirs(w0), "a0": cat_dirs(a0), "w_up": block_up(w_up), "a_up": block_up(a_up)}


def _raster_to_column(h):
    b, n, d = h.shape
    rows = n // GRID_WIDTH
    return h.reshape(b, rows, GRID_WIDTH, d).transpose(0, 2, 1, 3).reshape(b, n, d)


def _column_to_raster(h):
    b, n, d = h.shape
    rows = n // GRID_WIDTH
    return h.reshape(b, GRID_WIDTH, rows, d).transpose(0, 2, 1, 3).reshape(b, n, d)


def kernel(x, c, ctx, c_ctx, l0_norm_g, l0_mod_w, l0_mod_b, l0_w_in, l0_w_out, l0_mlstm_conv_w, l0_mlstm_gate_b, l0_mlstm_norm_g, l0_rwkv_mu, l0_rwkv_w0, l0_rwkv_w_up, l0_rwkv_a0, l0_rwkv_a_up, l0_rwkv_k_k, l0_rwkv_k_a, l0_rwkv_r_k, l0_rwkv_ln_w, l0_rwkv_ln_b, hgrn_lower_bounds, l1_norm_g, l1_mod_w, l1_mod_b, l1_w_in, l1_w_out, l1_hgrn_norm_g, l1_hyena_short_w, l1_hyena_short_b, l1_hyena_w1, l1_hyena_b1, l1_hyena_w2, l1_hyena_b2, l1_hyena_w3, l1_hyena_bias, final_norm_g):
    bsz, n_lat, d = x.shape
    n_ctx = ctx.shape[1]

    pad = (-(bsz + 1)) % 8
    cc = jnp.concatenate([c, c_ctx[None], jnp.zeros((pad, d), F32)], axis=0)
    mod0, mod1 = _modulation(cc, l0_mod_w, l0_mod_b, l1_mod_w, l1_mod_b)
    mod0 = mod0[:bsz + 1].reshape(bsz + 1, 3, d)
    mod1 = mod1[:bsz + 1].reshape(bsz + 1, 3, d)

    xa = jnp.concatenate([ctx, x], axis=1)
    w_main, w_gate, b_gate = _even_weight_layout(l0_w_in, l0_mlstm_gate_b)
    n0 = w_main.shape[1]
    u0, gt0 = _proj_in(xa, l0_norm_g, mod0, w_main, n0 // 2, n_ctx, w_gate, b_gate)
    gt0 = gt0.reshape(bsz, MLSTM_N_HEADS, 8, n_ctx + n_lat)
    y_m = _mlstm(u0, gt0, l0_mlstm_conv_w, l0_mlstm_norm_g, n_ctx)
    rp = _rwkv_params(l0_rwkv_mu, l0_rwkv_w0, l0_rwkv_w_up, l0_rwkv_a0, l0_rwkv_a_up, l0_rwkv_k_k,
                      l0_rwkv_k_a, l0_rwkv_r_k, l0_rwkv_ln_w, l0_rwkv_ln_b)
    y_r = _rwkv(u0, 5 * MLSTM_N_HEADS * LANES, rp, n_ctx)
    xa1 = _proj_out(y_m, y_r, xa, mod0, l0_w_out.astype(BF16), n_ctx)

    x1c = _raster_to_column(xa1[:, n_ctx:])
    xa1c = jnp.concatenate([xa1[:, :n_ctx], x1c], axis=1)
    w1 = l1_w_in.astype(BF16)
    (u1,) = _proj_in(xa1c, l1_norm_g, mod1, w1, w1.shape[1] // 2, n_ctx)
    y_g = _hgrn(u1, hgrn_lower_bounds, l1_hgrn_norm_g, n_ctx, layer=1)
    hw = l1_hyena_bias.shape[0]
    hk = _hyena_filters(n_lat, l1_hyena_w1, l1_hyena_b1, l1_hyena_w2, l1_hyena_b2, l1_hyena_w3, hw)
    fwd, inv = _dft_tables(n_lat)
    kspec = _filter_spectrum(fwd, hk)
    p16, e, pb = _hyena_pre(u1, 5 * HGRN_N_HEADS * LANES, l1_hyena_short_w, l1_hyena_short_b, l1_hyena_bias, n_ctx)
    z16 = _conv_spectrum(fwd, p16, kspec)
    y_y = _conv_inverse(inv, z16, e, pb)
    out_c = _proj_out(y_g, y_y, x1c, mod1[:bsz + 1], l1_w_out.astype(BF16), 0, final_g=final_norm_g)
    return _column_to_raster(out_c)
```

```python
import functools
import math

import jax
import jax.numpy as jnp
import numpy as np
from jax import lax
from jax.experimental import pallas as pl
from jax.experimental.pallas import tpu as pltpu

F32 = jnp.float32
BF16 = jnp.bfloat16

GRID_WIDTH = 64
RMS_EPS = 1e-6
MLSTM_N_HEADS = 4
RWKV_HEAD = 64
RWKV_LN_EPSILON = 64e-5
HGRN_N_HEADS = 4
HYENA_N_BANDS = 16
HYENA_FAST = 0.3
HYENA_SLOW = 1.5
HYENA_TGT = 1e-2
HYENA_SHIFT = 0.05

LANES = 128
MXU_DIM = 256
VMEM_LIMIT = 52 * 1024 * 1024

MIX_CHUNK = 128
RWKV_CHUNK = 64
RWKV_GROUP = 6
RWKV_PREP_ROWS = 256
MLSTM_GROUP = 6
HGRN_GROUP = 3
PROJ_ROWS = 768
DFT_TILE = 512
DFT_SPLIT = 64


def _bdot(a, b):
    return jnp.dot(a.astype(BF16), b.astype(BF16), preferred_element_type=F32)


def _bdot_nt(a, b):
    return lax.dot_general(a.astype(BF16), b.astype(BF16), (((1,), (1,)), ((), ())),
                           preferred_element_type=F32)


def _bdot_tn(a, b):
    return lax.dot_general(a.astype(BF16), b.astype(BF16), (((0,), (0,)), ((), ())),
                           preferred_element_type=F32)


def _split3(x):
    hi = x.astype(BF16)
    r1 = x - hi.astype(F32)
    mid = r1.astype(BF16)
    lo = (r1 - mid.astype(F32)).astype(BF16)
    return hi, mid, lo


def _sel_dot(sel, x):
    hi, mid, lo = _split3(x)
    d = functools.partial(jnp.dot, preferred_element_type=F32)
    return d(sel, hi) + d(sel, mid) + d(sel, lo)


def _dot_sel(x, sel):
    hi, mid, lo = _split3(x)
    d = functools.partial(jnp.dot, preferred_element_type=F32)
    return d(hi, sel) + d(mid, sel) + d(lo, sel)


def _sigmoid(x):
    return 1.0 / (1.0 + jnp.exp(-x))


def _silu(x):
    return x * _sigmoid(x)


def _iota(shape, dim):
    return lax.broadcasted_iota(jnp.int32, shape, dim)


def _neighbor_rows(ref, t0, rows, n_total, n_ctx):
    has_prev = jnp.logical_and(t0 != 0, t0 != n_ctx)
    has_next = jnp.logical_and(t0 + rows != n_ctx, t0 + rows != n_total)
    prev = ref[0, pl.ds(jnp.maximum(t0 - 1, 0), 1), :]
    nxt = ref[0, pl.ds(jnp.minimum(t0 + rows, n_total - 1), 1), :]
    return jnp.where(has_prev, prev, 0.0), jnp.where(has_next, nxt, 0.0)


def _shifted(cur, prev_row, next_row):
    rows = cur.shape[0]
    rid = _iota(cur.shape, 0)
    down = jnp.where(rid == 0, prev_row, pltpu.roll(cur, 1, 0))
    up = jnp.where(rid == rows - 1, next_row, pltpu.roll(cur, rows - 1, 0))
    return down, up


def _chunk_with_neighbors(ref, t0, rows, n_total, n_ctx):
    cur = ref[0, pl.ds(t0, rows), :]
    prev_row, next_row = _neighbor_rows(ref, t0, rows, n_total, n_ctx)
    down, up = _shifted(cur, prev_row, next_row)
    return cur, down, up


def _bwd_chunk(i, n_chunks, n_ctx_chunks):
    return jnp.where(i < n_ctx_chunks, n_ctx_chunks - 1 - i, n_chunks - 1 + n_ctx_chunks - i)


def _compiler_params(semantics):
    return pltpu.CompilerParams(dimension_semantics=semantics, vmem_limit_bytes=VMEM_LIMIT)


def _mod_kernel(c_ref, w0_ref, b0_ref, w1_ref, b1_ref, o0_ref, o1_ref):
    s = _silu(c_ref[...])
    o0_ref[...] = _bdot(s, w0_ref[...]) + b0_ref[...]
    o1_ref[...] = _bdot(s, w1_ref[...]) + b1_ref[...]


def _modulation(cc, w0, b0, w1, b1):
    rows, d = cc.shape
    n = w0.shape[1]
    tile = d
    grid = (n // tile,)
    wspec = pl.BlockSpec((d, tile), lambda j: (0, j))
    bspec = pl.BlockSpec((1, tile), lambda j: (0, j))
    ospec = pl.BlockSpec((rows, tile), lambda j: (0, j))
    return pl.pallas_call(
        _mod_kernel,
        grid=grid,
        in_specs=[pl.BlockSpec((rows, d), lambda j: (0, 0)), wspec, bspec, wspec, bspec],
        out_specs=[ospec, ospec],
        out_shape=[jax.ShapeDtypeStruct((rows, n), F32)] * 2,
        compiler_params=_compiler_params(("arbitrary",)),
        name="adaln_modulation",
    )(cc, w0, b0.reshape(1, n), w1, b1.reshape(1, n))


def _proj_in_kernel(*refs, rows, n_ctx, with_gates):
    if with_gates:
        x_ref, g_ref, ml_ref, mc_ref, w_ref, wg_ref, gb_ref, u_ref, gt_ref, h_scr = refs
    else:
        x_ref, g_ref, ml_ref, mc_ref, w_ref, u_ref, h_scr = refs
    i = pl.program_id(1)
    n = pl.program_id(2)

    @pl.when(n == 0)
    def _():
        x = x_ref[0]
        y = x * lax.rsqrt(jnp.mean(x * x, axis=-1, keepdims=True) + RMS_EPS) * g_ref[...]
        row = i * rows + _iota((rows, 1), 0)
        is_ctx = row < n_ctx
        ml = ml_ref[0]
        mc = mc_ref[0]
        shift = jnp.where(is_ctx, mc[0:1], ml[0:1])
        scale = jnp.where(is_ctx, mc[1:2], ml[1:2])
        h = (y * (1.0 + scale) + shift).astype(BF16)
        h_scr[...] = h
        if with_gates:
            gt_ref[0] = _bdot_nt(wg_ref[...], h) + gb_ref[:, 0:1]

    u_ref[0] = jnp.dot(h_scr[...], w_ref[...], preferred_element_type=F32)


def _proj_in(xa, norm_g, mod3, w16, n_tile, n_ctx, gate_w=None, gate_b=None):
    bsz, s, d = xa.shape
    n = w16.shape[1]
    rows = PROJ_ROWS
    grid = (bsz, s // rows, n // n_tile)
    ctx_row = mod3.shape[0] - 1
    with_gates = gate_w is not None
    in_specs = [
        pl.BlockSpec((1, rows, d), lambda b, i, j: (b, i, 0)),
        pl.BlockSpec((1, d), lambda b, i, j: (0, 0)),
        pl.BlockSpec((1, 3, d), lambda b, i, j: (b, 0, 0)),
        pl.BlockSpec((1, 3, d), lambda b, i, j: (ctx_row, 0, 0)),
        pl.BlockSpec((d, n_tile), lambda b, i, j: (0, j)),
    ]
    args = [xa, norm_g.reshape(1, d), mod3, mod3, w16]
    out_specs = [pl.BlockSpec((1, rows, n_tile), lambda b, i, j: (b, i, j))]
    out_shape = [jax.ShapeDtypeStruct((bsz, s, n), F32)]
    if with_gates:
        ng = gate_w.shape[0]
        in_specs += [pl.BlockSpec((ng, d), lambda b, i, j: (0, 0)),
                     pl.BlockSpec((ng, LANES), lambda b, i, j: (0, 0))]
        args += [gate_w, gate_b]
        out_specs.append(pl.BlockSpec((1, ng, rows), lambda b, i, j: (b, 0, i)))
        out_shape.append(jax.ShapeDtypeStruct((bsz, ng, s), F32))
    return pl.pallas_call(
        functools.partial(_proj_in_kernel, rows=rows, n_ctx=n_ctx, with_gates=with_gates),
        grid=grid,
        in_specs=in_specs,
        out_specs=out_specs,
        out_shape=out_shape,
        scratch_shapes=[pltpu.VMEM((rows, d), BF16)],
        compiler_params=_compiler_params(("parallel", "arbitrary", "arbitrary")),
        name="norm_mod_proj_in",
    )(*args)


def _proj_out_kernel(ya_ref, yb_ref, x_ref, ml_ref, mc_ref, w_ref, fg_ref, o_ref, *, rows, n_ctx, final_norm):
    i = pl.program_id(1)
    half = ya_ref.shape[2]
    y = _bdot(ya_ref[0], w_ref[0:half, :]) + _bdot(yb_ref[0], w_ref[half:, :])
    row = i * rows + _iota((rows, 1), 0)
    gate = jnp.where(row < n_ctx, mc_ref[0][2:3], ml_ref[0][2:3])
    x = x_ref[0] + gate * y
    if final_norm:
        x = x * lax.rsqrt(jnp.mean(x * x, axis=-1, keepdims=True) + RMS_EPS) * fg_ref[...]
    o_ref[0] = x


def _proj_out(ya, yb, x, mod3, w16, n_ctx, final_g=None):
    bsz, s, d = x.shape
    half = ya.shape[2]
    rows = PROJ_ROWS if s % PROJ_ROWS == 0 else 512
    grid = (bsz, s // rows)
    ctx_row = mod3.shape[0] - 1
    final_norm = final_g is not None
    fg = final_g if final_norm else jnp.ones((d,), F32)
    tok = lambda w: pl.BlockSpec((1, rows, w), lambda b, i: (b, i, 0))
    return pl.pallas_call(
        functools.partial(_proj_out_kernel, rows=rows, n_ctx=n_ctx, final_norm=final_norm),
        grid=grid,
        in_specs=[tok(half), tok(half), tok(d),
                  pl.BlockSpec((1, 3, d), lambda b, i: (b, 0, 0)),
                  pl.BlockSpec((1, 3, d), lambda b, i: (ctx_row, 0, 0)),
                  pl.BlockSpec((2 * half, d), lambda b, i: (0, 0)),
                  pl.BlockSpec((1, d), lambda b, i: (0, 0))],
        out_specs=tok(d),
        out_shape=jax.ShapeDtypeStruct((bsz, s, d), F32),
        compiler_params=_compiler_params(("parallel", "arbitrary")),
        name="proj_out_residual",
    )(ya, yb, x, mod3, mod3, w16, fg.reshape(1, d))


def _mlstm_chunk_operators(chunks, causal):
    t = chunks[0][0].shape[0]
    lane = _iota((8, t), 1)
    row_id = _iota((8, t), 0)
    log_fs = [jnp.minimum(c[3], 0.0) - jnp.log1p(jnp.exp(-jnp.abs(c[3]))) for c in chunks]
    cum_f, cum_b = list(log_fs), list(log_fs)
    sh = 1
    while sh < t:
        cum_f = [x + jnp.where(lane >= sh, pltpu.roll(x, sh, 1), 0.0) for x in cum_f]
        cum_b = [x + jnp.where(lane < t - sh, pltpu.roll(x, t - sh, 1), 0.0) for x in cum_b]
        sh *= 2
    pad = jnp.zeros((t - 8, t), F32)
    tiles = [jnp.concatenate([jnp.where(row_id % 2 == 0, c[3], jnp.where(row_id == 1, f, b)), pad], axis=0)
             for c, f, b in zip(chunks, cum_f, cum_b)]
    cols = [x.T for x in tiles]
    problems = []
    for c, f, b, col in zip(chunks, cum_f, cum_b, cols):
        for d in range(2):
            b_row = (f, b)[d][2 * d + 1:2 * d + 2]
            problems.append(dict(q=c[0], k=c[1], v_ext=c[2], d=d, ig_row=c[3][2 * d:2 * d + 1], b_row=b_row,
                                 ig_col=col[:, 2 * d:2 * d + 1], b_col=col[:, 2 * d + 1:2 * d + 2]))
    logws = [jnp.where(causal[p["d"]], p["b_col"] + (p["ig_row"] - p["b_row"]), -jnp.inf) for p in problems]
    mus = [jnp.max(x, axis=-1, keepdims=True) for x in logws]
    ws = [jnp.exp(x - mu) for x, mu in zip(logws, mus)]
    lasts = [0 if p["d"] == 1 else t - 1 for p in problems]
    b_lasts = [p["b_col"][i:i + 1] for p, i in zip(problems, lasts)]
    gammas = [mu[i:i + 1] for mu, i in zip(mus, lasts)]
    gks = [jnp.exp(bl - p["b_col"] + p["ig_col"] - gm) * p["k"] for p, bl, gm in zip(problems, b_lasts, gammas)]
    qks = [_bdot_nt(p["q"], p["k"]) * w for p, w in zip(problems, ws)]
    intras = [_bdot(qk, p["v_ext"]) for qk, p in zip(qks, problems)]
    kvs = [_bdot_tn(gk, p["v_ext"]) for gk, p in zip(gks, problems)]
    return [(intra, kv, p["b_col"], mu, bl, gm)
            for intra, kv, p, mu, bl, gm in zip(intras, kvs, problems, mus, b_lasts, gammas)]


def _mlstm_kernel(q_ref, k_ref, v_ref, o_ref, z_ref, gt_ref, cwq_ref, cwk_ref, ng_ref, out_ref,
                  qa_scr, ka_scr, h_scr, intra_scr, kv_scr, b_scr, mu_scr, tail_scr, *, n_ctx):
    s = q_ref.shape[1]
    dh = q_ref.shape[2]
    t = MIX_CHUNK
    n_chunks = s // t
    n_ctx_chunks = n_ctx // t
    k_scale = dh ** -0.5

    def prep(j, carry):
        t0 = pl.multiple_of(j * t, t)
        for src, cw, dst, scale in ((q_ref, cwq_ref, qa_scr, 1.0), (k_ref, cwk_ref, ka_scr, k_scale)):
            cur, down, up = _chunk_with_neighbors(src, t0, t, s, n_ctx)
            conv = down * cw[0:1, :] + cur * cw[1:2, :] + up * cw[2:3, :]
            dst[pl.ds(t0, t), :] = _silu(conv) * scale
        return carry

    lax.fori_loop(0, n_chunks, prep, 0)

    ones_col = (_iota((t, dh), 1) == 0).astype(F32)
    causal = [_iota((t, t), 1) <= _iota((t, t), 0), _iota((t, t), 1) >= _iota((t, t), 0)]

    def operators(gi, carry):
        chunks, where = [], []
        for kk in range(MLSTM_GROUP):
            chunk = gi * MLSTM_GROUP + kk
            sl = pl.ds(pl.multiple_of(chunk * t, t), t)
            v_ext = jnp.concatenate([v_ref[0, sl, :], ones_col], axis=1)
            chunks.append((qa_scr[sl, :], ka_scr[sl, :], v_ext, gt_ref[0, 0, :, sl]))
            where += [(0, chunk, sl), (1, chunk, sl)]
        for (d, chunk, sl), (intra, kv, b, mu, b_last, gamma) in zip(where, _mlstm_chunk_operators(chunks, causal)):
            intra_scr[d, sl, :] = intra
            kv_scr[d, chunk] = kv
            b_scr[d, sl, :] = b
            mu_scr[d, sl, :] = mu
            tail_scr[d, chunk] = jnp.concatenate([jnp.broadcast_to(b_last, (1, dh)), jnp.broadcast_to(gamma, (1, dh))],
                                                 axis=0)
        return carry

    lax.fori_loop(0, n_chunks // MLSTM_GROUP, operators, 0)

    def scan(i, carry):
        chunks = (i, _bwd_chunk(i, n_chunks, n_ctx_chunks))
        sls = [pl.ds(pl.multiple_of(c * t, t), t) for c in chunks]
        inters = [_bdot(qa_scr[sl, :], c_ext) for sl, (c_ext, _) in zip(sls, carry)]
        new = []
        for d, (chunk, sl, inter, (c_ext, m)) in enumerate(zip(chunks, sls, inters, carry)):
            b = b_scr[d, sl, :]
            mu = mu_scr[d, sl, :]
            tail = tail_scr[d, chunk]
            b_last, gamma = tail[0:1, 0:1], tail[1:2, 0:1]
            m_t = jnp.maximum(b + m, mu)
            num = jnp.exp(b + m - m_t) * inter + jnp.exp(mu - m_t) * intra_scr[d, sl, :]
            den = num[:, dh:dh + 1]
            h_scr[d, sl, :] = num[:, :dh] / jnp.maximum(jnp.abs(den), jnp.exp(-m_t))
            m_new = jnp.maximum(b_last + m, gamma)
            new.append((jnp.exp(b_last + m - m_new) * c_ext + jnp.exp(gamma - m_new) * kv_scr[d, chunk], m_new))
        return tuple(new)

    zero = (jnp.zeros((dh, 2 * dh), F32), jnp.zeros((1, 1), F32))
    lax.fori_loop(0, n_chunks, scan, (zero, zero))

    def finish(j, carry):
        t0 = pl.multiple_of(j * t, t)
        h = h_scr[0, pl.ds(t0, t), :] + h_scr[1, pl.ds(t0, t), :]
        y = h * lax.rsqrt(jnp.mean(h * h, axis=-1, keepdims=True) + RMS_EPS) * ng_ref[...]
        out_ref[0, pl.ds(t0, t), :] = y * _sigmoid(o_ref[0, pl.ds(t0, t), :]) * _silu(z_ref[0, pl.ds(t0, t), :])
        return carry

    lax.fori_loop(0, n_chunks, finish, 0)


def _mlstm(u, gt, conv_w, norm_g, n_ctx):
    bsz, s, _ = u.shape
    nh = MLSTM_N_HEADS
    dh = LANES
    width = nh * dh
    n_chunks = s // MIX_CHUNK
    assert n_chunks % MLSTM_GROUP == 0 and n_ctx % MIX_CHUNK == 0
    col = lambda k: pl.BlockSpec((1, s, dh), lambda b, h, k=k: (b, 0, k * nh + h))
    par = lambda k: pl.BlockSpec((3, dh), lambda b, h, k=k: (0, k * nh + h))
    return pl.pallas_call(
        functools.partial(_mlstm_kernel, n_ctx=n_ctx),
        grid=(bsz, nh),
        in_specs=[col(0), col(1), col(2), col(3), col(4),
                  pl.BlockSpec((1, 1, 8, s), lambda b, h: (b, h, 0, 0)),
                  par(0), par(1),
                  pl.BlockSpec((1, dh), lambda b, h: (0, h))],
        out_specs=pl.BlockSpec((1, s, dh), lambda b, h: (b, 0, h)),
        out_shape=jax.ShapeDtypeStruct((bsz, s, width), F32),
        scratch_shapes=[pltpu.VMEM((s, dh), F32), pltpu.VMEM((s, dh), F32), pltpu.VMEM((2, s, dh), F32),
                        pltpu.VMEM((2, s, 2 * dh), F32), pltpu.VMEM((2, n_chunks, dh, 2 * dh), F32),
                        pltpu.VMEM((2, s, 1), F32), pltpu.VMEM((2, s, 1), F32),
                        pltpu.VMEM((2, n_chunks, 2, dh), F32)],
        compiler_params=_compiler_params(("parallel", "arbitrary")),
        name="mlstm_mixer",
    )(u, u, u, u, u, gt, conv_w, conv_w, norm_g.reshape(1, width))


def _head_stack(x, lane_lo):
    return jnp.concatenate([jnp.where(lane_lo, x, 0.0), jnp.where(lane_lo, 0.0, x)], axis=0)


def _rwkv_chunk_operators(problems, consts, eye, lane_lo):
    t, w = problems[0][0].shape
    n2 = 2 * t
    stack = lambda x: _head_stack(x, lane_lo)
    zeros = jnp.zeros((n2, w), F32)
    dirs = [p[6] for p in problems]
    cums = [_sel_dot(consts[d][0], p[3]) for p, d in zip(problems, dirs)]
    pre = []
    for (r, v, kk, lw, ka, kt, d), cum in zip(problems, cums):
        last = 0 if d == 1 else t - 1
        cum_end = cum[last:last + 1]
        e_inv = jnp.exp(-cum)
        e_end = jnp.exp(cum_end - cum)
        a_s = stack(-kk * jnp.exp(cum - lw))
        r_s = stack(r * jnp.exp(cum))
        pre.append(dict(a_s=a_s, r_s=r_s, vs=stack(v), g=jnp.exp(cum_end),
                        ar=jnp.concatenate([a_s, r_s], axis=0),
                        bk=jnp.concatenate([stack(ka * e_inv), stack(kt * e_inv)], axis=0),
                        bk_end=jnp.concatenate([stack(ka * e_end), stack(kt * e_end)], axis=0)))
    m_alls = [_bdot_nt(q["ar"], q["bk"]) for q in pre]
    m_abs = [jnp.where(consts[d][1], m[:n2, :n2], 0.0) for m, d in zip(m_alls, dirs)]
    m_aks = [jnp.where(consts[d][1], m[:n2, n2:], 0.0) for m, d in zip(m_alls, dirs)]
    m_lows = [jnp.where(consts[d][2], m[n2:, :], 0.0) for m, d in zip(m_alls, dirs)]
    invs = [eye + jnp.where(consts[d][3][0], m, 0.0) for m, d in zip(m_abs, dirs)]
    for level in range(1, len(consts[0][3])):
        inner = [_bdot(jnp.where(consts[d][3][level], m, 0.0), x) for m, x, d in zip(m_abs, invs, dirs)]
        invs = [x + _bdot(x, y) for x, y in zip(invs, inner)]
    mv = [_bdot(m, q["vs"]) for m, q in zip(m_aks, pre)]
    solved = [_bdot(x, jnp.concatenate([q["a_s"], y], axis=1)) for x, q, y in zip(invs, pre, mv)]
    zms = [jnp.concatenate([sv, jnp.concatenate([zeros, q["vs"]], axis=1)], axis=0) for sv, q in zip(solved, pre)]
    ry1s = [jnp.concatenate([q["r_s"], zeros], axis=1) + _bdot(m, z) for q, m, z in zip(pre, m_lows, zms)]
    pqs = [_bdot_tn(z, q["bk_end"]) for z, q in zip(zms, pre)]
    out = []
    for ry1, pq, q in zip(ry1s, pqs, pre):
        folded = ry1[:t] + ry1[t:]
        out.append((folded[:, :w], folded[:, w:], pq[:w], pq[w:], q["g"]))
    return out


def _rwkv_kernel(rr_ref, rk_ref, rv_ref, rz_ref, wd_ref, ad_ref, mu_ref, kk_ref, ka_ref, rkk_ref,
                 lnw_ref, lnb_ref, w0_ref, a0_ref, wup_ref, aup_ref, out_ref,
                 r_scr, v_scr, kk_scr, lw_scr, ka_scr, kt_scr, bonus_scr, y_scr, ry_scr, pt_scr, qt_scr, g_scr,
                 *, n_ctx):
    s = rr_ref.shape[1]
    w = rr_ref.shape[2]
    p_rows = RWKV_PREP_ROWS
    t = RWKV_CHUNK
    n_chunks = s // t
    n_ctx_chunks = n_ctx // t
    head_sum = ((_iota((w, w), 0) // RWKV_HEAD) == (_iota((w, w), 1) // RWKV_HEAD)).astype(BF16)
    inv_head = 1.0 / RWKV_HEAD

    def prep(j, carry):
        t0 = pl.multiple_of(j * p_rows, p_rows)
        mixed = []
        for idx, src in enumerate((rr_ref, rk_ref, rv_ref)):
            cur, down, up = _chunk_with_neighbors(src, t0, p_rows, s, n_ctx)
            mixed.append(cur + mu_ref[idx:idx + 1, :] * (0.5 * (down + up) - cur))
        r, kr, v = mixed
        kk = kr * kk_ref[...]
        norm = jnp.sqrt(_dot_sel(kk * kk, head_sum))
        kk = kk / jnp.maximum(norm, 1e-12)
        w_raw = _bdot(jnp.tanh(wd_ref[0, pl.ds(t0, p_rows), :]), wup_ref[0]) + w0_ref[0]
        a = _sigmoid(_bdot(ad_ref[0, pl.ds(t0, p_rows), :], aup_ref[0]) + a0_ref[0])
        lw = -math.exp(-0.5) * _sigmoid(w_raw)
        kt_sum = jnp.zeros_like(kr)
        for d in range(2):
            a_d = a[:, d * w:(d + 1) * w]
            kt_d = kr * (1.0 + (a_d - 1.0) * ka_ref[...])
            kt_sum = kt_sum + kt_d
            lw_scr[d, pl.ds(t0, p_rows), :] = lw[:, d * w:(d + 1) * w]
            ka_scr[d, pl.ds(t0, p_rows), :] = kk * a_d
            kt_scr[d, pl.ds(t0, p_rows), :] = kt_d
        coef = _dot_sel(r * kt_sum * rkk_ref[...], head_sum)
        r_scr[pl.ds(t0, p_rows), :] = r
        v_scr[pl.ds(t0, p_rows), :] = v
        kk_scr[pl.ds(t0, p_rows), :] = kk
        bonus_scr[pl.ds(t0, p_rows), :] = coef * v
        return carry

    lax.fori_loop(0, s // p_rows, prep, 0)

    n2 = 2 * t
    r_i = _iota((n2, n2), 0)
    c_i = _iota((n2, n2), 1)
    same = (r_i // t) == (c_i // t)
    rt = r_i % t
    ct = c_i % t
    eye = (r_i == c_i).astype(F32)
    lane_lo = _iota((t, w), 1) < RWKV_HEAD
    tri_r = _iota((t, t), 0)
    tri_c = _iota((t, t), 1)
    consts = []
    for reverse in (False, True):
        strict = jnp.logical_and(same, (ct > rt) if reverse else (ct < rt))
        incl = jnp.logical_and(same, (ct >= rt) if reverse else (ct <= rt))
        tri = ((tri_c >= tri_r) if reverse else (tri_c <= tri_r)).astype(BF16)
        merges = []
        c = 1
        while c < t:
            hi_r = (r_i % (2 * c)) >= c
            hi_c = (c_i % (2 * c)) >= c
            cross = jnp.logical_and(hi_c, jnp.logical_not(hi_r)) if reverse else jnp.logical_and(hi_r, jnp.logical_not(hi_c))
            merges.append(jnp.logical_and((r_i // (2 * c)) == (c_i // (2 * c)), cross))
            c *= 2
        consts.append((tri, strict, jnp.concatenate([incl, incl], axis=1), merges))

    def operators(gi, carry):
        problems, where = [], []
        for k in range(RWKV_GROUP):
            chunk = gi * RWKV_GROUP + k
            sl = pl.ds(pl.multiple_of(chunk * t, t), t)
            r, v, kk = r_scr[sl, :], v_scr[sl, :], kk_scr[sl, :]
            for d in range(2):
                problems.append((r, v, kk, lw_scr[d, sl, :], ka_scr[d, sl, :], kt_scr[d, sl, :], d))
                where.append((d, chunk, sl))
        for (d, chunk, sl), (ry, y1, pt, qt, g) in zip(where, _rwkv_chunk_operators(problems, consts, eye, lane_lo)):
            ry_scr[d, sl, :] = ry.astype(BF16)
            y_scr[d, sl, :] = y1
            pt_scr[d, chunk] = pt.astype(BF16)
            qt_scr[d, chunk] = qt
            g_scr[d, chunk] = g
        return carry

    lax.fori_loop(0, n_chunks // RWKV_GROUP, operators, 0)

    def scan(i, carry):
        states = []
        for d, (chunk, ht) in enumerate(zip((i, _bwd_chunk(i, n_chunks, n_ctx_chunks)), carry)):
            sl = pl.ds(pl.multiple_of(chunk * t, t), t)
            y_scr[d, sl, :] = _bdot_nt(ry_scr[d, sl, :], ht) + y_scr[d, sl, :]
            states.append(ht * g_scr[d, chunk] + _bdot(ht, pt_scr[d, chunk]) + qt_scr[d, chunk])
        return tuple(states)

    zero_state = jnp.zeros((w, w), F32)
    lax.fori_loop(0, n_chunks, scan, (zero_state, zero_state))

    def finish(j, carry):
        t0 = pl.multiple_of(j * p_rows, p_rows)
        sl = pl.ds(t0, p_rows)
        y = y_scr[0, sl, :] + y_scr[1, sl, :] + bonus_scr[sl, :]
        mu = _dot_sel(y, head_sum) * inv_head
        yc = y - mu
        var = _dot_sel(yc * yc, head_sum) * inv_head
        yn = yc * lax.rsqrt(var + RWKV_LN_EPSILON) * lnw_ref[...] + lnb_ref[...]
        out_ref[0, sl, :] = yn * _silu(rz_ref[0, sl, :])
        return carry

    lax.fori_loop(0, s // p_rows, finish, 0)


def _rwkv(u, col0, p, n_ctx):
    bsz, s, _ = u.shape
    w = LANES
    width = p["mu"].shape[1]
    n_pairs = width // w
    base = col0 // w
    col = lambda k: pl.BlockSpec((1, s, w), lambda b, h, k=k: (b, 0, base + k * n_pairs + h))
    lora = lambda k: pl.BlockSpec((1, s, w), lambda b, h, k=k: (b, 0, base + 4 * n_pairs + k))
    vec = lambda rows: pl.BlockSpec((rows, w), lambda b, h: (0, h))
    cat = pl.BlockSpec((1, 1, 2 * w), lambda b, h: (h, 0, 0))
    up = pl.BlockSpec((1, w, 2 * w), lambda b, h: (h, 0, 0))
    seq = pltpu.VMEM((s, w), F32)
    seq2 = pltpu.VMEM((2, s, w), F32)
    n_chunks = s // RWKV_CHUNK
    assert n_chunks % RWKV_GROUP == 0 and s % RWKV_PREP_ROWS == 0 and n_ctx % RWKV_PREP_ROWS == 0
    operators = [pltpu.VMEM((2, s, w), BF16), pltpu.VMEM((2, n_chunks, w, w), BF16),
                 pltpu.VMEM((2, n_chunks, w, w), F32), pltpu.VMEM((2, n_chunks, 1, w), F32)]
    return pl.pallas_call(
        functools.partial(_rwkv_kernel, n_ctx=n_ctx),
        grid=(bsz, n_pairs),
        in_specs=[col(0), col(1), col(2), col(3), lora(0), lora(1),
                  vec(3), vec(1), vec(1), vec(1), vec(1), vec(1), cat, cat, up, up],
        out_specs=pl.BlockSpec((1, s, w), lambda b, h: (b, 0, h)),
        out_shape=jax.ShapeDtypeStruct((bsz, s, width), F32),
        scratch_shapes=[seq, seq, seq, seq2, seq2, seq2, seq, seq2] + operators,
        compiler_params=_compiler_params(("parallel", "arbitrary")),
        name="rwkv7_mixer",
    )(u, u, u, u, u, u, p["mu"], p["k_k"], p["k_a"], p["r_k"], p["ln_w"], p["ln_b"],
      p["w0"], p["a0"], p["w_up"], p["a_up"])


def _hgrn_level_masks(t, w):
    rid = _iota((t, w), 0)
    r_i = _iota((t, t), 0)
    c_i = _iota((t, t), 1)
    levels = []
    c = 1
    while c < t:
        same_block = (r_i // (2 * c)) == (c_i // (2 * c))
        up_r = (r_i % (2 * c)) >= c
        up_c = (c_i % (2 * c)) >= c
        pair = [jnp.logical_and(same_block, jnp.logical_and(up_r, jnp.logical_not(up_c))),
                jnp.logical_and(same_block, jnp.logical_and(up_c, jnp.logical_not(up_r)))]
        levels.append((c, (rid % (2 * c)) >= c, pair))
        c *= 2
    return levels


def _hgrn_chunk_operators(problems, lb, tris, levels):
    t, w = problems[0][0].shape
    r_i = _iota((t, t), 0)
    c_i = _iota((t, t), 1)
    zero_row = jnp.zeros((1, w), F32)
    dirs = [p[3] for p in problems]
    lgs, ks = [], []
    for q, v, ff, d in problems:
        e = jnp.exp(-jnp.abs(ff))
        big = 1.0 / (1.0 + e)
        small = e / (1.0 + e)
        pos = ff >= 0.0
        lgs.append(jnp.log(lb + (1.0 - lb) * jnp.where(pos, big, small)))
        ks.append((1.0 - lb) * jnp.where(pos, small, big))
    bs = [_sel_dot(tris[d], lg) for lg, d in zip(lgs, dirs)]
    befores = [_shifted(b, zero_row, zero_row)[1 if d == 1 else 0] for b, d in zip(bs, dirs)]
    edges = list(bs)
    accs = [jnp.where(r_i == c_i, jnp.sum(p[0] * k, axis=-1, keepdims=True), 0.0) for p, k in zip(problems, ks)]
    for c, upper, pair in levels:
        qts = [p[0] * jnp.exp(b - before) for p, b, before in zip(problems, bs, befores)]
        kts = [k * jnp.exp(edge - b) for k, b, edge in zip(ks, bs, edges)]
        prods = [_bdot_nt(qt, kt) for qt, kt in zip(qts, kts)]
        accs = [a + jnp.where(pair[d], pr, 0.0) for a, pr, d in zip(accs, prods, dirs)]
        for i, d in enumerate(dirs):
            if d == 1:
                befores[i] = jnp.where(upper, befores[i], pltpu.roll(befores[i], t - c, 0))
                edges[i] = jnp.where(upper, pltpu.roll(edges[i], c, 0), edges[i])
            else:
                befores[i] = jnp.where(upper, pltpu.roll(befores[i], c, 0), befores[i])
                edges[i] = jnp.where(upper, edges[i], pltpu.roll(edges[i], t - c, 0))
    o_intras = [_bdot(a, p[1]) for a, p in zip(accs, problems)]
    b_ends = [b[(0 if d == 1 else t - 1):(1 if d == 1 else t)] for b, d in zip(bs, dirs)]
    kvs = [_bdot_tn(p[1], k * jnp.exp(be - b)) for p, k, b, be in zip(problems, ks, bs, b_ends)]
    return [(p[0] * jnp.exp(b), oi, kv, jnp.exp(be)) for p, b, oi, kv, be in zip(problems, bs, o_intras, kvs, b_ends)]


def _hgrn_kernel(q_ref, i_ref, ff_ref, fb_ref, z_ref, lb_ref, ng_ref, out_ref, o_scr, qe_scr, kv_scr, g_scr,
                 *, n_ctx, layer):
    s = q_ref.shape[1]
    dh = q_ref.shape[2]
    t = MIX_CHUNK
    n_chunks = s // t
    n_ctx_chunks = n_ctx // t
    lbs = lb_ref[...]
    ex = jnp.exp(lbs - jnp.max(lbs, axis=0, keepdims=True))
    probs = ex / jnp.sum(ex, axis=0, keepdims=True)
    csum = probs[0:1]
    for l in range(1, layer + 1):
        csum = csum + probs[l:l + 1]
    lb = csum - probs[0:1]
    tri_r = _iota((t, t), 0)
    tri_c = _iota((t, t), 1)
    tris = [(tri_c <= tri_r).astype(BF16), (tri_c >= tri_r).astype(BF16)]
    levels = _hgrn_level_masks(t, dh)

    def operators(gi, carry):
        problems, where = [], []
        for kk in range(HGRN_GROUP):
            chunk = gi * HGRN_GROUP + kk
            sl = pl.ds(pl.multiple_of(chunk * t, t), t)
            q, v = q_ref[0, sl, :], i_ref[0, sl, :]
            for d, f_ref in enumerate((ff_ref, fb_ref)):
                problems.append((q, v, f_ref[0, sl, :], d))
                where.append((d, chunk, sl))
        for (d, chunk, sl), (qe, o_intra, kv, g) in zip(where, _hgrn_chunk_operators(problems, lb, tris, levels)):
            qe_scr[d, sl, :] = qe.astype(BF16)
            o_scr[d, sl, :] = o_intra
            kv_scr[d, chunk] = kv
            g_scr[d, chunk] = g
        return carry

    lax.fori_loop(0, n_chunks // HGRN_GROUP, operators, 0)

    def scan(i, carry):
        chunks = (i, _bwd_chunk(i, n_chunks, n_ctx_chunks))
        sls = [pl.ds(pl.multiple_of(c * t, t), t) for c in chunks]
        inters = [_bdot_nt(qe_scr[d, sl, :], st) for d, (sl, st) in enumerate(zip(sls, carry))]
        new = []
        for d, (chunk, sl, inter, st) in enumerate(zip(chunks, sls, inters, carry)):
            o_scr[d, sl, :] = o_scr[d, sl, :] + inter
            new.append(st * g_scr[d, chunk] + kv_scr[d, chunk])
        return tuple(new)

    zero_state = jnp.zeros((dh, dh), F32)
    lax.fori_loop(0, n_chunks, scan, (zero_state, zero_state))

    def finish(j, carry):
        t0 = pl.multiple_of(n_ctx + j * t, t)
        sl = pl.ds(t0, t)
        o = o_scr[0, sl, :] + o_scr[1, sl, :]
        y = o * lax.rsqrt(jnp.mean(o * o, axis=-1, keepdims=True) + RMS_EPS) * ng_ref[...]
        out_ref[0, pl.ds(pl.multiple_of(j * t, t), t), :] = y * _silu(z_ref[0, sl, :])
        return carry

    lax.fori_loop(0, n_chunks - n_ctx_chunks, finish, 0)


def _hgrn(u, lb_all, norm_g, n_ctx, layer):
    bsz, s, _ = u.shape
    nh = HGRN_N_HEADS
    dh = LANES
    width = nh * dh
    depth = lb_all.shape[0]
    n_chunks = s // MIX_CHUNK
    assert n_chunks % HGRN_GROUP == 0 and n_ctx % MIX_CHUNK == 0
    col = lambda k: pl.BlockSpec((1, s, dh), lambda b, h, k=k: (b, 0, k * nh + h))
    return pl.pallas_call(
        functools.partial(_hgrn_kernel, n_ctx=n_ctx, layer=layer),
        grid=(bsz, nh),
        in_specs=[col(0), col(1), col(2), col(3), col(4),
                  pl.BlockSpec((depth, dh), lambda b, h: (0, h)),
                  pl.BlockSpec((1, dh), lambda b, h: (0, h))],
        out_specs=pl.BlockSpec((1, s - n_ctx, dh), lambda b, h: (b, 0, h)),
        out_shape=jax.ShapeDtypeStruct((bsz, s - n_ctx, width), F32),
        scratch_shapes=[pltpu.VMEM((2, s, dh), F32), pltpu.VMEM((2, s, dh), BF16),
                        pltpu.VMEM((2, n_chunks, dh, dh), F32), pltpu.VMEM((2, n_chunks, 1, dh), F32)],
        compiler_params=_compiler_params(("parallel", "arbitrary")),
        name="hgrn2_mixer",
    )(u, u, u, u, u, lb_all, norm_g.reshape(1, width))


def _hyena_filter_kernel(z_ref, w1_ref, b1_ref, w2_ref, b2_ref, w3f_ref, w3b_ref, dl_ref, hf_ref, hb_ref):
    hp = functools.partial(jnp.dot, precision=lax.Precision.HIGHEST, preferred_element_type=F32)
    n = z_ref.shape[0]
    hid = jnp.sin(hp(z_ref[...], w1_ref[...]) + b1_ref[...])
    hid = jnp.sin(hp(hid, w2_ref[...]) + b2_ref[...])
    pos = _iota((n, 1), 0).astype(F32) * (1.0 / n)
    window = jnp.exp(-pos * dl_ref[...]) + HYENA_SHIFT
    f0 = hp(hid, w3f_ref[...]) * window
    f1 = hp(hid, w3b_ref[...]) * window
    nrm = jnp.sum(jnp.abs(f0), axis=0, keepdims=True) + jnp.sum(jnp.abs(f1), axis=0, keepdims=True)
    hf_ref[...] = f0 / nrm
    hb_ref[...] = f1 / nrm


def _hyena_filters(n, w1, b1, w2, b2, w3, width):
    pos = np.arange(n, dtype=np.float64)
    bands = np.linspace(1e-4, HYENA_N_BANDS - 1, HYENA_N_BANDS)
    ang = (2.0 * math.pi / n) * pos[:, None] * bands
    z = np.concatenate([(pos / n)[:, None], np.cos(ang), np.sin(ang)], axis=-1)
    z = np.pad(z, ((0, 0), (0, LANES - z.shape[1]))).astype(np.float32)
    max_decay = math.log(HYENA_TGT) / HYENA_FAST
    min_decay = math.log(HYENA_TGT) / HYENA_SLOW
    deltas = np.abs(np.linspace(min_decay, max_decay, width)).astype(np.float32)[None]
    feat, hid = w1.shape
    w1p = jnp.pad(w1, ((0, LANES - feat), (0, LANES - hid)))
    w2p = jnp.pad(w2, ((0, LANES - hid), (0, LANES - hid)))
    w3p = jnp.pad(w3, ((0, LANES - hid), (0, 0)))
    b1p = jnp.pad(b1, (0, LANES - hid)).reshape(1, LANES)
    b2p = jnp.pad(b2, (0, LANES - hid)).reshape(1, LANES)
    n_tiles = width // LANES
    full = lambda shape: pl.BlockSpec(shape, lambda j: (0, 0))
    out = pl.BlockSpec((n, LANES), lambda j: (0, j))
    hf, hb = pl.pallas_call(
        _hyena_filter_kernel,
        grid=(n_tiles,),
        in_specs=[full((n, LANES)), full((LANES, LANES)), full((1, LANES)), full((LANES, LANES)), full((1, LANES)),
                  pl.BlockSpec((LANES, LANES), lambda j: (0, j)),
                  pl.BlockSpec((LANES, LANES), lambda j: (0, n_tiles + j)),
                  pl.BlockSpec((1, LANES), lambda j: (0, j))],
        out_specs=[out, out],
        out_shape=[jax.ShapeDtypeStruct((n, width), F32)] * 2,
        compiler_params=_compiler_params(("arbitrary",)),
        name="hyena_filters",
    )(jnp.asarray(z), w1p, b1p, w2p, b2p, w3p, w3p, jnp.asarray(deltas))
    return jnp.concatenate([hf, hb], axis=1)


def _hyena_pre_kernel(yv_ref, y0_ref, y1_ref, yz_ref, swv_ref, sw0_ref, sw1_ref, sbv_ref, sb0_ref, sb1_ref,
                      yb_ref, p_ref, e_ref, pb_ref, *, n_ctx):
    s = yv_ref.shape[1]
    rows = MIX_CHUNK

    def body(j, carry):
        t0 = pl.multiple_of(n_ctx + j * rows, rows)
        conv = []
        for src, sw, sb in ((yv_ref, swv_ref, sbv_ref), (y0_ref, sw0_ref, sb0_ref), (y1_ref, sw1_ref, sb1_ref)):
            cur, down, up = _chunk_with_neighbors(src, t0, rows, s, n_ctx)
            conv.append(down * sw[0:1, :] + cur * sw[1:2, :] + up * sw[2:3, :] + sb[...])
        v, x0, x1 = conv
        p = x1 * v
        o0 = pl.multiple_of(j * rows, rows)
        p_ref[0, pl.ds(o0, rows), :] = p.astype(BF16)
        pb_ref[0, pl.ds(o0, rows), :] = p * yb_ref[...]
        e_ref[0, pl.ds(o0, rows), :] = x0 * _silu(yz_ref[0, pl.ds(t0, rows), :])
        return carry

    lax.fori_loop(0, (s - n_ctx) // rows, body, 0)


def _hyena_pre(u, col0, short_w, short_b, y_bias, n_ctx):
    bsz, s, _ = u.shape
    w = y_bias.shape[0]
    tiles = w // LANES
    base = col0 // LANES
    n = s - n_ctx
    col = lambda k: pl.BlockSpec((1, s, LANES), lambda b, j, k=k: (b, 0, base + k * tiles + j))
    par = lambda rows, k: pl.BlockSpec((rows, LANES), lambda b, j, k=k: (0, k * tiles + j))
    out = pl.BlockSpec((1, n, LANES), lambda b, j: (b, 0, j))
    sb = short_b.reshape(1, 3 * w)
    return pl.pallas_call(
        functools.partial(_hyena_pre_kernel, n_ctx=n_ctx),
        grid=(bsz, tiles),
        in_specs=[col(0), col(1), col(2), col(3),
                  par(3, 0), par(3, 1), par(3, 2), par(1, 0), par(1, 1), par(1, 2), par(1, 0)],
        out_specs=[out, out, out],
        out_shape=[jax.ShapeDtypeStruct((bsz, n, w), BF16),
                   jax.ShapeDtypeStruct((bsz, n, w), F32),
                   jax.ShapeDtypeStruct((bsz, n, w), F32)],
        compiler_params=_compiler_params(("parallel", "arbitrary")),
        name="hyena_short_conv",
    )(u, u, u, u, short_w, short_w, short_w, sb, sb, sb, y_bias.reshape(1, w))


def _dft_tables(n):
    big = 2 * n
    half = DFT_TILE // 2
    idx = jnp.arange(n, dtype=jnp.int32)
    split = DFT_SPLIT
    lo = jnp.arange(split, dtype=jnp.int32)
    hi = jnp.arange(n // split, dtype=jnp.int32)
    ang_lo = ((lo[:, None] * idx[None, :]) % big).astype(F32) * (2.0 * math.pi / big)
    ang_hi = ((hi[:, None] * idx[None, :]) % (big // split)).astype(F32) * (2.0 * math.pi * split / big)
    c_lo, s_lo = jnp.cos(ang_lo)[None], jnp.sin(ang_lo)[None]
    c_hi, s_hi = jnp.cos(ang_hi)[:, None], jnp.sin(ang_hi)[:, None]
    cos = (c_hi * c_lo - s_hi * s_lo).reshape(n, n)
    sin = (s_hi * c_lo + c_hi * s_lo).reshape(n, n)
    alt = jnp.where(idx % 2 == 0, 1.0, -1.0).astype(F32)
    first_row = (idx == 0)[:, None]
    first_col = (idx == 0)[None, :]
    im = jnp.where(first_row, alt[None, :], -sin)
    fwd = jnp.stack([cos.reshape(n // half, half, n), im.reshape(n // half, half, n)], axis=1).reshape(big, n)
    wre = jnp.where(first_col, 1.0, 2.0) * cos * (1.0 / big)
    wim = jnp.where(first_col, alt[:, None], -2.0 * sin) * (1.0 / big)
    inv = jnp.stack([wre.reshape(n, n // half, half), wim.reshape(n, n // half, half)], axis=2).reshape(n, big)
    return fwd.astype(BF16), inv.astype(BF16)


def _spectrum_kernel(f_ref, x_ref, o_ref):
    o_ref[...] = jnp.dot(f_ref[...], x_ref[...].astype(BF16), preferred_element_type=F32)


def _filter_spectrum(fwd, hk):
    big, n = fwd.shape
    cols = hk.shape[1]
    return pl.pallas_call(
        _spectrum_kernel,
        grid=(big // DFT_TILE,),
        in_specs=[pl.BlockSpec((DFT_TILE, n), lambda i: (i, 0)),
                  pl.BlockSpec((n, cols), lambda i: (0, 0))],
        out_specs=pl.BlockSpec((DFT_TILE, cols), lambda i: (i, 0)),
        out_shape=jax.ShapeDtypeStruct((big, cols), F32),
        compiler_params=_compiler_params(("parallel",)),
        name="hyena_filter_spectrum",
    )(fwd, hk)


def _conv_spectrum_kernel(f_ref, p_ref, ks_ref, z_ref):
    i = pl.program_id(1)
    half = DFT_TILE // 2
    w = p_ref.shape[2]
    acc = jnp.dot(f_ref[...], p_ref[0], preferred_element_type=F32)
    s_re, s_im = acc[:half], acc[half:]
    ks = ks_ref[...]
    k_re = ks[:half, :w] + ks[:half, w:]
    k_im = ks[half:, :w] - ks[half:, w:]
    z_re = s_re * k_re - s_im * k_im
    z_im = s_re * k_im + s_im * k_re
    packed = jnp.logical_and(_iota((half, w), 0) == 0, i == 0)
    z_re = jnp.where(packed, s_re * k_re, z_re)
    z_im = jnp.where(packed, s_im * (ks[half:, :w] + ks[half:, w:]), z_im)
    z_ref[0, :half, :] = z_re.astype(BF16)
    z_ref[0, half:, :] = z_im.astype(BF16)


def _conv_spectrum(fwd, p16, kspec):
    bsz, n, w = p16.shape
    big = fwd.shape[0]
    return pl.pallas_call(
        _conv_spectrum_kernel,
        grid=(bsz, big // DFT_TILE),
        in_specs=[pl.BlockSpec((DFT_TILE, n), lambda b, i: (i, 0)),
                  pl.BlockSpec((1, n, w), lambda b, i: (b, 0, 0)),
                  pl.BlockSpec((DFT_TILE, 2 * w), lambda b, i: (i, 0))],
        out_specs=pl.BlockSpec((1, DFT_TILE, w), lambda b, i: (b, i, 0)),
        out_shape=jax.ShapeDtypeStruct((bsz, big, w), BF16),
        compiler_params=_compiler_params(("parallel", "arbitrary")),
        name="hyena_forward_dft",
    )(fwd, p16, kspec)


def _conv_inverse_kernel(g_ref, z_ref, e_ref, pb_ref, o_ref):
    y = jnp.dot(g_ref[...], z_ref[0], preferred_element_type=F32)
    o_ref[0] = e_ref[0] * (y + pb_ref[0])


def _conv_inverse(inv, z16, e, pb):
    bsz, big, w = z16.shape
    n = inv.shape[0]
    tile = DFT_TILE
    tok = pl.BlockSpec((1, tile, w), lambda b, i: (b, i, 0))
    return pl.pallas_call(
        _conv_inverse_kernel,
        grid=(bsz, n // tile),
        in_specs=[pl.BlockSpec((tile, big), lambda b, i: (i, 0)),
                  pl.BlockSpec((1, big, w), lambda b, i: (b, 0, 0)),
                  tok, tok],
        out_specs=tok,
        out_shape=jax.ShapeDtypeStruct((bsz, n, w), F32),
        compiler_params=_compiler_params(("parallel", "arbitrary")),
        name="hyena_inverse_dft",
    )(inv, z16, e, pb)


def _even_weight_layout(w_in, gate_b):
    d = w_in.shape[0]
    mw = MLSTM_N_HEADS * LANES
    g0 = 5 * mw
    g1 = g0 + 4 * MLSTM_N_HEADS
    main = jnp.concatenate([w_in[:, :g0], w_in[:, g1:]], axis=1).astype(BF16)
    wg = w_in[:, g0:g1].reshape(d, 2, 2, MLSTM_N_HEADS)
    wg = jnp.transpose(wg, (3, 1, 2, 0)).reshape(MLSTM_N_HEADS, 4, d)
    wg = jnp.concatenate([wg, jnp.zeros_like(wg)], axis=1).reshape(MLSTM_N_HEADS * 8, d).astype(BF16)
    gb = jnp.transpose(gate_b.reshape(2, 2, MLSTM_N_HEADS), (2, 0, 1)).reshape(MLSTM_N_HEADS, 4)
    gb = jnp.concatenate([gb, jnp.zeros_like(gb)], axis=1).reshape(MLSTM_N_HEADS * 8, 1)
    return main, wg, jnp.broadcast_to(gb, (MLSTM_N_HEADS * 8, LANES))


def _rwkv_params(mu, w0, w_up, a0, a_up, k_k, k_a, r_k, ln_w, ln_b):
    width = mu.shape[1]
    n_pairs = width // LANES
    row = lambda x: x.reshape(1, width)

    def cat_dirs(x):
        return jnp.transpose(x.reshape(2, n_pairs, LANES), (1, 0, 2)).reshape(n_pairs, 1, 2 * LANES)

    def block_up(x):
        lora = x.shape[1]
        xp = jnp.transpose(x.reshape(2, lora, n_pairs, LANES), (2, 0, 1, 3))
        z = jnp.zeros_like(xp[:, 0])
        top = jnp.concatenate([xp[:, 0], z], axis=2)
        bot = jnp.concatenate([z, xp[:, 1]], axis=2)
        return jnp.concatenate([top, bot], axis=1).astype(BF16)

    return {"mu": mu, "k_k": row(k_k), "k_a": row(k_a), "r_k": row(r_k), "ln_w": row(ln_w), "ln_b": row(ln_b),
            "w0": cat_dirs(w0), "a0": cat_dirs(a0), "w_up": block_up(w_up), "a_up": block_up(a_up)}


def _raster_to_column(h):
    b, n, d = h.shape
    rows = n // GRID_WIDTH
    return h.reshape(b, rows, GRID_WIDTH, d).transpose(0, 2, 1, 3).reshape(b, n, d)


def _column_to_raster(h):
    b, n, d = h.shape
    rows = n // GRID_WIDTH
    return h.reshape(b, GRID_WIDTH, rows, d).transpose(0, 2, 1, 3).reshape(b, n, d)


def kernel(x, c, ctx, c_ctx, l0_norm_g, l0_mod_w, l0_mod_b, l0_w_in, l0_w_out, l0_mlstm_conv_w, l0_mlstm_gate_b, l0_mlstm_norm_g, l0_rwkv_mu, l0_rwkv_w0, l0_rwkv_w_up, l0_rwkv_a0, l0_rwkv_a_up, l0_rwkv_k_k, l0_rwkv_k_a, l0_rwkv_r_k, l0_rwkv_ln_w, l0_rwkv_ln_b, hgrn_lower_bounds, l1_norm_g, l1_mod_w, l1_mod_b, l1_w_in, l1_w_out, l1_hgrn_norm_g, l1_hyena_short_w, l1_hyena_short_b, l1_hyena_w1, l1_hyena_b1, l1_hyena_w2, l1_hyena_b2, l1_hyena_w3, l1_hyena_bias, final_norm_g):
    bsz, n_lat, d = x.shape
    n_ctx = ctx.shape[1]

    pad = (-(bsz + 1)) % 8
    cc = jnp.concatenate([c, c_ctx[None], jnp.zeros((pad, d), F32)], axis=0)
    mod0, mod1 = _modulation(cc, l0_mod_w, l0_mod_b, l1_mod_w, l1_mod_b)
    mod0 = mod0[:bsz + 1].reshape(bsz + 1, 3, d)
    mod1 = mod1[:bsz + 1].reshape(bsz + 1, 3, d)

    xa = jnp.concatenate([ctx, x], axis=1)
    w_main, w_gate, b_gate = _even_weight_layout(l0_w_in, l0_mlstm_gate_b)
    n0 = w_main.shape[1]
    u0, gt0 = _proj_in(xa, l0_norm_g, mod0, w_main, n0 // 2, n_ctx, w_gate, b_gate)
    gt0 = gt0.reshape(bsz, MLSTM_N_HEADS, 8, n_ctx + n_lat)
    y_m = _mlstm(u0, gt0, l0_mlstm_conv_w, l0_mlstm_norm_g, n_ctx)
    rp = _rwkv_params(l0_rwkv_mu, l0_rwkv_w0, l0_rwkv_w_up, l0_rwkv_a0, l0_rwkv_a_up, l0_rwkv_k_k,
                      l0_rwkv_k_a, l0_rwkv_r_k, l0_rwkv_ln_w, l0_rwkv_ln_b)
    y_r = _rwkv(u0, 5 * MLSTM_N_HEADS * LANES, rp, n_ctx)
    xa1 = _proj_out(y_m, y_r, xa, mod0, l0_w_out.astype(BF16), n_ctx)

    x1c = _raster_to_column(xa1[:, n_ctx:])
    xa1c = jnp.concatenate([xa1[:, :n_ctx], x1c], axis=1)
    w1 = l1_w_in.astype(BF16)
    (u1,) = _proj_in(xa1c, l1_norm_g, mod1, w1, w1.shape[1] // 2, n_ctx)
    y_g = _hgrn(u1, hgrn_lower_bounds, l1_hgrn_norm_g, n_ctx, layer=1)
    hw = l1_hyena_bias.shape[0]
    hk = _hyena_filters(n_lat, l1_hyena_w1, l1_hyena_b1, l1_hyena_w2, l1_hyena_b2, l1_hyena_w3, hw)
    fwd, inv = _dft_tables(n_lat)
    kspec = _filter_spectrum(fwd, hk)
    p16, e, pb = _hyena_pre(u1, 5 * HGRN_N_HEADS * LANES, l1_hyena_short_w, l1_hyena_short_b, l1_hyena_bias, n_ctx)
    z16 = _conv_spectrum(fwd, p16, kspec)
    y_y = _conv_inverse(inv, z16, e, pb)
    out_c = _proj_out(y_g, y_y, x1c, mod1[:bsz + 1], l1_w_out.astype(BF16), 0, final_g=final_norm_g)
    return _column_to_raster(out_c)
```

```python
import functools
import math

import jax
import jax.numpy as jnp
import numpy as np
from jax import lax
from jax.experimental import pallas as pl
from jax.experimental.pallas import tpu as pltpu

F32 = jnp.float32
BF16 = jnp.bfloat16

GRID_WIDTH = 64
RMS_EPS = 1e-6
MLSTM_N_HEADS = 4
RWKV_HEAD = 64
RWKV_LN_EPSILON = 64e-5
HGRN_N_HEADS = 4
HYENA_N_BANDS = 16
HYENA_FAST = 0.3
HYENA_SLOW = 1.5
HYENA_TGT = 1e-2
HYENA_SHIFT = 0.05

LANES = 128
MXU_DIM = 256
VMEM_LIMIT = 52 * 1024 * 1024

MIX_CHUNK = 128
RWKV_CHUNK = 64
RWKV_GROUP = 6
RWKV_PREP_ROWS = 256
MLSTM_GROUP = 6
HGRN_GROUP = 3
PROJ_ROWS = 768
FINAL_ROWS = 512
DFT_TILE = 512
DFT_SPLIT = 64


def _bdot(a, b):
    return jnp.dot(a.astype(BF16), b.astype(BF16), preferred_element_type=F32)


def _bdot_nt(a, b):
    return lax.dot_general(a.astype(BF16), b.astype(BF16), (((1,), (1,)), ((), ())),
                           preferred_element_type=F32)


def _bdot_tn(a, b):
    return lax.dot_general(a.astype(BF16), b.astype(BF16), (((0,), (0,)), ((), ())),
                           preferred_element_type=F32)


def _split3(x):
    hi = x.astype(BF16)
    r1 = x - hi.astype(F32)
    mid = r1.astype(BF16)
    lo = (r1 - mid.astype(F32)).astype(BF16)
    return hi, mid, lo


def _sel_dot(sel, x):
    hi, mid, lo = _split3(x)
    d = functools.partial(jnp.dot, preferred_element_type=F32)
    return d(sel, hi) + d(sel, mid) + d(sel, lo)


def _dot_sel(x, sel):
    hi, mid, lo = _split3(x)
    d = functools.partial(jnp.dot, preferred_element_type=F32)
    return d(hi, sel) + d(mid, sel) + d(lo, sel)


def _sigmoid(x):
    return 1.0 / (1.0 + jnp.exp(-x))


def _silu(x):
    return x * _sigmoid(x)


def _iota(shape, dim):
    return lax.broadcasted_iota(jnp.int32, shape, dim)


def _neighbor_rows(ref, t0, rows, n_total, split):
    has_prev = jnp.logical_and(t0 != 0, t0 != split)
    has_next = jnp.logical_and(t0 + rows != split, t0 + rows != n_total)
    prev = ref[0, pl.ds(jnp.maximum(t0 - 1, 0), 1), :]
    nxt = ref[0, pl.ds(jnp.minimum(t0 + rows, n_total - 1), 1), :]
    return jnp.where(has_prev, prev, 0.0), jnp.where(has_next, nxt, 0.0)


def _shifted(cur, prev_row, next_row):
    rows = cur.shape[0]
    rid = _iota(cur.shape, 0)
    down = jnp.where(rid == 0, prev_row, pltpu.roll(cur, 1, 0))
    up = jnp.where(rid == rows - 1, next_row, pltpu.roll(cur, rows - 1, 0))
    return down, up


def _chunk_with_neighbors(ref, t0, rows, n_total, split):
    cur = ref[0, pl.ds(t0, rows), :]
    prev_row, next_row = _neighbor_rows(ref, t0, rows, n_total, split)
    down, up = _shifted(cur, prev_row, next_row)
    return cur, down, up


def _scan_chunks(i, n_chunks, n_ctx_chunks):
    fwd = jnp.where(i < n_ctx_chunks, n_chunks - n_ctx_chunks + i, i - n_ctx_chunks)
    return fwd, n_chunks - 1 - i


def _compiler_params(semantics):
    return pltpu.CompilerParams(dimension_semantics=semantics, vmem_limit_bytes=VMEM_LIMIT)


def _mod_kernel(c_ref, w0_ref, b0_ref, w1_ref, b1_ref, o0_ref, o1_ref):
    s = _silu(c_ref[...])
    o0_ref[...] = _bdot(s, w0_ref[...]) + b0_ref[...]
    o1_ref[...] = _bdot(s, w1_ref[...]) + b1_ref[...]


def _modulation(cc, w0, b0, w1, b1):
    rows, d = cc.shape
    n = w0.shape[1]
    tile = d
    grid = (n // tile,)
    wspec = pl.BlockSpec((d, tile), lambda j: (0, j))
    bspec = pl.BlockSpec((1, tile), lambda j: (0, j))
    ospec = pl.BlockSpec((rows, tile), lambda j: (0, j))
    return pl.pallas_call(
        _mod_kernel,
        grid=grid,
        in_specs=[pl.BlockSpec((rows, d), lambda j: (0, 0)), wspec, bspec, wspec, bspec],
        out_specs=[ospec, ospec],
        out_shape=[jax.ShapeDtypeStruct((rows, n), F32)] * 2,
        compiler_params=_compiler_params(("arbitrary",)),
        name="adaln_modulation",
    )(cc, w0, b0.reshape(1, n), w1, b1.reshape(1, n))


def _token_tile(x_ref, c_ref, i, rows):
    n_lat_tail = rows - c_ref.shape[1]
    is_ctx = jnp.logical_and(i == pl.num_programs(1) - 1, _iota((rows, 1), 0) >= n_lat_tail)
    ctx_rows = jnp.concatenate([jnp.zeros((n_lat_tail, c_ref.shape[2]), F32), c_ref[0]], axis=0)
    return jnp.where(is_ctx, ctx_rows, x_ref[0]), is_ctx


def _proj_in_kernel(*refs, rows, with_gates):
    if with_gates:
        x_ref, c_ref, g_ref, ml_ref, mc_ref, w_ref, wg_ref, gb_ref, u_ref, gt_ref, h_scr = refs
    else:
        x_ref, c_ref, g_ref, ml_ref, mc_ref, w_ref, u_ref, h_scr = refs
    i = pl.program_id(1)
    n = pl.program_id(2)

    @pl.when(n == 0)
    def _():
        x, is_ctx = _token_tile(x_ref, c_ref, i, rows)
        y = x * lax.rsqrt(jnp.mean(x * x, axis=-1, keepdims=True) + RMS_EPS) * g_ref[...]
        ml = ml_ref[0]
        mc = mc_ref[0]
        shift = jnp.where(is_ctx, mc[0:1], ml[0:1])
        scale = jnp.where(is_ctx, mc[1:2], ml[1:2])
        h = (y * (1.0 + scale) + shift).astype(BF16)
        h_scr[...] = h
        if with_gates:
            gt_ref[0] = _bdot_nt(wg_ref[...], h) + gb_ref[:, 0:1]

    u_ref[0] = jnp.dot(h_scr[...], w_ref[...], preferred_element_type=F32)


def _proj_in(x, ctx, norm_g, mod3, w16, n_tile, gate_w=None, gate_b=None):
    bsz, n_lat, d = x.shape
    n_ctx = ctx.shape[1]
    s = n_lat + n_ctx
    n = w16.shape[1]
    rows = PROJ_ROWS
    assert s % rows == 0 and (n_lat % rows) + n_ctx == rows
    grid = (bsz, s // rows, n // n_tile)
    ctx_row = mod3.shape[0] - 1
    with_gates = gate_w is not None
    in_specs = [
        pl.BlockSpec((1, rows, d), lambda b, i, j: (b, i, 0)),
        pl.BlockSpec((1, n_ctx, d), lambda b, i, j: (b, 0, 0)),
        pl.BlockSpec((1, d), lambda b, i, j: (0, 0)),
        pl.BlockSpec((1, 3, d), lambda b, i, j: (b, 0, 0)),
        pl.BlockSpec((1, 3, d), lambda b, i, j: (ctx_row, 0, 0)),
        pl.BlockSpec((d, n_tile), lambda b, i, j: (0, j)),
    ]
    args = [x, ctx, norm_g.reshape(1, d), mod3, mod3, w16]
    out_specs = [pl.BlockSpec((1, rows, n_tile), lambda b, i, j: (b, i, j))]
    out_shape = [jax.ShapeDtypeStruct((bsz, s, n), F32)]
    if with_gates:
        ng = gate_w.shape[0]
        in_specs += [pl.BlockSpec((ng, d), lambda b, i, j: (0, 0)),
                     pl.BlockSpec((ng, LANES), lambda b, i, j: (0, 0))]
        args += [gate_w, gate_b]
        out_specs.append(pl.BlockSpec((1, ng, rows), lambda b, i, j: (b, 0, i)))
        out_shape.append(jax.ShapeDtypeStruct((bsz, ng, s), F32))
    return pl.pallas_call(
        functools.partial(_proj_in_kernel, rows=rows, with_gates=with_gates),
        grid=grid,
        in_specs=in_specs,
        out_specs=out_specs,
        out_shape=out_shape,
        scratch_shapes=[pltpu.VMEM((rows, d), BF16)],
        compiler_params=_compiler_params(("parallel", "arbitrary", "arbitrary")),
        name="norm_mod_proj_in",
    )(*args)


def _proj_out_kernel(ya_ref, yb_ref, x_ref, c_ref, ml_ref, mc_ref, w_ref, ox_ref, oc_ref, *, rows):
    i = pl.program_id(1)
    half = ya_ref.shape[2]
    y = _bdot(ya_ref[0], w_ref[0:half, :]) + _bdot(yb_ref[0], w_ref[half:, :])
    x, is_ctx = _token_tile(x_ref, c_ref, i, rows)
    x = x + jnp.where(is_ctx, mc_ref[0][2:3], ml_ref[0][2:3]) * y
    ox_ref[0] = x

    @pl.when(i == pl.num_programs(1) - 1)
    def _():
        oc_ref[0] = x[rows - c_ref.shape[1]:]


def _proj_out(ya, yb, x, ctx, mod3, w16):
    bsz, n_lat, d = x.shape
    n_ctx = ctx.shape[1]
    s = n_lat + n_ctx
    half = ya.shape[2]
    rows = PROJ_ROWS
    assert s % rows == 0 and (n_lat % rows) + n_ctx == rows
    ctx_row = mod3.shape[0] - 1
    tok = lambda w: pl.BlockSpec((1, rows, w), lambda b, i: (b, i, 0))
    seg = pl.BlockSpec((1, n_ctx, d), lambda b, i: (b, 0, 0))
    return pl.pallas_call(
        functools.partial(_proj_out_kernel, rows=rows),
        grid=(bsz, s // rows),
        in_specs=[tok(half), tok(half), tok(d), seg,
                  pl.BlockSpec((1, 3, d), lambda b, i: (b, 0, 0)),
                  pl.BlockSpec((1, 3, d), lambda b, i: (ctx_row, 0, 0)),
                  pl.BlockSpec((2 * half, d), lambda b, i: (0, 0))],
        out_specs=[tok(d), seg],
        out_shape=[jax.ShapeDtypeStruct((bsz, n_lat, d), F32), jax.ShapeDtypeStruct((bsz, n_ctx, d), F32)],
        compiler_params=_compiler_params(("parallel", "arbitrary")),
        name="proj_out_residual",
    )(ya, yb, x, ctx, mod3, mod3, w16)


def _proj_out_final_kernel(ya_ref, yb_ref, x_ref, ml_ref, w_ref, fg_ref, o_ref):
    half = ya_ref.shape[2]
    y = _bdot(ya_ref[0], w_ref[0:half, :]) + _bdot(yb_ref[0], w_ref[half:, :])
    x = x_ref[0] + ml_ref[0][2:3] * y
    o_ref[0] = x * lax.rsqrt(jnp.mean(x * x, axis=-1, keepdims=True) + RMS_EPS) * fg_ref[...]


def _proj_out_final(ya, yb, x, mod3, w16, final_g):
    bsz, n_lat, d = x.shape
    half = ya.shape[2]
    rows = FINAL_ROWS
    assert n_lat % rows == 0
    tok = lambda w: pl.BlockSpec((1, rows, w), lambda b, i: (b, i, 0))
    return pl.pallas_call(
        _proj_out_final_kernel,
        grid=(bsz, n_lat // rows),
        in_specs=[tok(half), tok(half), tok(d),
                  pl.BlockSpec((1, 3, d), lambda b, i: (b, 0, 0)),
                  pl.BlockSpec((2 * half, d), lambda b, i: (0, 0)),
                  pl.BlockSpec((1, d), lambda b, i: (0, 0))],
        out_specs=tok(d),
        out_shape=jax.ShapeDtypeStruct((bsz, n_lat, d), F32),
        compiler_params=_compiler_params(("parallel", "arbitrary")),
        name="proj_out_final_norm",
    )(ya, yb, x, mod3, w16, final_g.reshape(1, d))


def _mlstm_chunk_operators(chunks, causal):
    t = chunks[0][0].shape[0]
    lane = _iota((8, t), 1)
    row_id = _iota((8, t), 0)
    log_fs = [jnp.minimum(c[3], 0.0) - jnp.log1p(jnp.exp(-jnp.abs(c[3]))) for c in chunks]
    cum_f, cum_b = list(log_fs), list(log_fs)
    sh = 1
    while sh < t:
        cum_f = [x + jnp.where(lane >= sh, pltpu.roll(x, sh, 1), 0.0) for x in cum_f]
        cum_b = [x + jnp.where(lane < t - sh, pltpu.roll(x, t - sh, 1), 0.0) for x in cum_b]
        sh *= 2
    pad = jnp.zeros((t - 8, t), F32)
    tiles = [jnp.concatenate([jnp.where(row_id % 2 == 0, c[3], jnp.where(row_id == 1, f, b)), pad], axis=0)
             for c, f, b in zip(chunks, cum_f, cum_b)]
    cols = [x.T for x in tiles]
    problems = []
    for c, f, b, col in zip(chunks, cum_f, cum_b, cols):
        for d in range(2):
            b_row = (f, b)[d][2 * d + 1:2 * d + 2]
            problems.append(dict(q=c[0], k=c[1], v_ext=c[2], d=d, ig_row=c[3][2 * d:2 * d + 1], b_row=b_row,
                                 ig_col=col[:, 2 * d:2 * d + 1], b_col=col[:, 2 * d + 1:2 * d + 2]))
    logws = [jnp.where(causal[p["d"]], p["b_col"] + (p["ig_row"] - p["b_row"]), -jnp.inf) for p in problems]
    mus = [jnp.max(x, axis=-1, keepdims=True) for x in logws]
    ws = [jnp.exp(x - mu) for x, mu in zip(logws, mus)]
    lasts = [0 if p["d"] == 1 else t - 1 for p in problems]
    b_lasts = [p["b_col"][i:i + 1] for p, i in zip(problems, lasts)]
    gammas = [mu[i:i + 1] for mu, i in zip(mus, lasts)]
    gks = [jnp.exp(bl - p["b_col"] + p["ig_col"] - gm) * p["k"] for p, bl, gm in zip(problems, b_lasts, gammas)]
    qks = [_bdot_nt(p["q"], p["k"]) * w for p, w in zip(problems, ws)]
    intras = [_bdot(qk, p["v_ext"]) for qk, p in zip(qks, problems)]
    kvs = [_bdot_tn(gk, p["v_ext"]) for gk, p in zip(gks, problems)]
    return [(intra, kv, p["b_col"], mu, bl, gm)
            for intra, kv, p, mu, bl, gm in zip(intras, kvs, problems, mus, b_lasts, gammas)]


def _mlstm_kernel(q_ref, k_ref, v_ref, o_ref, z_ref, gt_ref, cwq_ref, cwk_ref, ng_ref, out_ref,
                  qa_scr, ka_scr, h_scr, intra_scr, kv_scr, b_scr, mu_scr, tail_scr, *, n_ctx):
    s = q_ref.shape[1]
    dh = q_ref.shape[2]
    t = MIX_CHUNK
    n_chunks = s // t
    n_ctx_chunks = n_ctx // t
    k_scale = dh ** -0.5

    def prep(j, carry):
        t0 = pl.multiple_of(j * t, t)
        for src, cw, dst, scale in ((q_ref, cwq_ref, qa_scr, 1.0), (k_ref, cwk_ref, ka_scr, k_scale)):
            cur, down, up = _chunk_with_neighbors(src, t0, t, s, s - n_ctx)
            conv = down * cw[0:1, :] + cur * cw[1:2, :] + up * cw[2:3, :]
            dst[pl.ds(t0, t), :] = _silu(conv) * scale
        return carry

    lax.fori_loop(0, n_chunks, prep, 0)

    ones_col = (_iota((t, dh), 1) == 0).astype(F32)
    causal = [_iota((t, t), 1) <= _iota((t, t), 0), _iota((t, t), 1) >= _iota((t, t), 0)]

    def operators(gi, carry):
        chunks, where = [], []
        for kk in range(MLSTM_GROUP):
            chunk = gi * MLSTM_GROUP + kk
            sl = pl.ds(pl.multiple_of(chunk * t, t), t)
            v_ext = jnp.concatenate([v_ref[0, sl, :], ones_col], axis=1)
            chunks.append((qa_scr[sl, :], ka_scr[sl, :], v_ext, gt_ref[0, 0, :, sl]))
            where += [(0, chunk, sl), (1, chunk, sl)]
        for (d, chunk, sl), (intra, kv, b, mu, b_last, gamma) in zip(where, _mlstm_chunk_operators(chunks, causal)):
            intra_scr[d, sl, :] = intra
            kv_scr[d, chunk] = kv
            b_scr[d, sl, :] = b
            mu_scr[d, sl, :] = mu
            tail_scr[d, chunk] = jnp.concatenate([jnp.broadcast_to(b_last, (1, dh)), jnp.broadcast_to(gamma, (1, dh))],
                                                 axis=0)
        return carry

    lax.fori_loop(0, n_chunks // MLSTM_GROUP, operators, 0)

    def scan(i, carry):
        chunks = _scan_chunks(i, n_chunks, n_ctx_chunks)
        sls = [pl.ds(pl.multiple_of(c * t, t), t) for c in chunks]
        inters = [_bdot(qa_scr[sl, :], c_ext) for sl, (c_ext, _) in zip(sls, carry)]
        new = []
        for d, (chunk, sl, inter, (c_ext, m)) in enumerate(zip(chunks, sls, inters, carry)):
            b = b_scr[d, sl, :]
            mu = mu_scr[d, sl, :]
            tail = tail_scr[d, chunk]
            b_last, gamma = tail[0:1, 0:1], tail[1:2, 0:1]
            m_t = jnp.maximum(b + m, mu)
            num = jnp.exp(b + m - m_t) * inter + jnp.exp(mu - m_t) * intra_scr[d, sl, :]
            den = num[:, dh:dh + 1]
            h_scr[d, sl, :] = num[:, :dh] / jnp.maximum(jnp.abs(den), jnp.exp(-m_t))
            m_new = jnp.maximum(b_last + m, gamma)
            new.append((jnp.exp(b_last + m - m_new) * c_ext + jnp.exp(gamma - m_new) * kv_scr[d, chunk], m_new))
        return tuple(new)

    zero = (jnp.zeros((dh, 2 * dh), F32), jnp.zeros((1, 1), F32))
    lax.fori_loop(0, n_chunks, scan, (zero, zero))

    def finish(j, carry):
        t0 = pl.multiple_of(j * t, t)
        h = h_scr[0, pl.ds(t0, t), :] + h_scr[1, pl.ds(t0, t), :]
        y = h * lax.rsqrt(jnp.mean(h * h, axis=-1, keepdims=True) + RMS_EPS) * ng_ref[...]
        out_ref[0, pl.ds(t0, t), :] = y * _sigmoid(o_ref[0, pl.ds(t0, t), :]) * _silu(z_ref[0, pl.ds(t0, t), :])
        return carry

    lax.fori_loop(0, n_chunks, finish, 0)


def _mlstm(u, gt, conv_w, norm_g, n_ctx):
    bsz, s, _ = u.shape
    nh = MLSTM_N_HEADS
    dh = LANES
    width = nh * dh
    n_chunks = s // MIX_CHUNK
    assert n_chunks % MLSTM_GROUP == 0 and n_ctx % MIX_CHUNK == 0
    col = lambda k: pl.BlockSpec((1, s, dh), lambda b, h, k=k: (b, 0, k * nh + h))
    par = lambda k: pl.BlockSpec((3, dh), lambda b, h, k=k: (0, k * nh + h))
    return pl.pallas_call(
        functools.partial(_mlstm_kernel, n_ctx=n_ctx),
        grid=(bsz, nh),
        in_specs=[col(0), col(1), col(2), col(3), col(4),
                  pl.BlockSpec((1, 1, 8, s), lambda b, h: (b, h, 0, 0)),
                  par(0), par(1),
                  pl.BlockSpec((1, dh), lambda b, h: (0, h))],
        out_specs=pl.BlockSpec((1, s, dh), lambda b, h: (b, 0, h)),
        out_shape=jax.ShapeDtypeStruct((bsz, s, width), F32),
        scratch_shapes=[pltpu.VMEM((s, dh), F32), pltpu.VMEM((s, dh), F32), pltpu.VMEM((2, s, dh), F32),
                        pltpu.VMEM((2, s, 2 * dh), F32), pltpu.VMEM((2, n_chunks, dh, 2 * dh), F32),
                        pltpu.VMEM((2, s, 1), F32), pltpu.VMEM((2, s, 1), F32),
                        pltpu.VMEM((2, n_chunks, 2, dh), F32)],
        compiler_params=_compiler_params(("parallel", "arbitrary")),
        name="mlstm_mixer",
    )(u, u, u, u, u, gt, conv_w, conv_w, norm_g.reshape(1, width))


def _head_stack(x, lane_lo):
    return jnp.concatenate([jnp.where(lane_lo, x, 0.0), jnp.where(lane_lo, 0.0, x)], axis=0)


def _rwkv_chunk_operators(problems, consts, eye, lane_lo):
    t, w = problems[0][0].shape
    n2 = 2 * t
    stack = lambda x: _head_stack(x, lane_lo)
    zeros = jnp.zeros((n2, w), F32)
    dirs = [p[6] for p in problems]
    cums = [_sel_dot(consts[d][0], p[3]) for p, d in zip(problems, dirs)]
    pre = []
    for (r, v, kk, lw, ka, kt, d), cum in zip(problems, cums):
        last = 0 if d == 1 else t - 1
        cum_end = cum[last:last + 1]
        e_inv = jnp.exp(-cum)
        e_end = jnp.exp(cum_end - cum)
        a_s = stack(-kk * jnp.exp(cum - lw))
        r_s = stack(r * jnp.exp(cum))
        pre.append(dict(a_s=a_s, r_s=r_s, vs=stack(v), g=jnp.exp(cum_end),
                        ar=jnp.concatenate([a_s, r_s], axis=0),
                        bk=jnp.concatenate([stack(ka * e_inv), stack(kt * e_inv)], axis=0),
                        bk_end=jnp.concatenate([stack(ka * e_end), stack(kt * e_end)], axis=0)))
    m_alls = [_bdot_nt(q["ar"], q["bk"]) for q in pre]
    m_abs = [jnp.where(consts[d][1], m[:n2, :n2], 0.0) for m, d in zip(m_alls, dirs)]
    m_aks = [jnp.where(consts[d][1], m[:n2, n2:], 0.0) for m, d in zip(m_alls, dirs)]
    m_lows = [jnp.where(consts[d][2], m[n2:, :], 0.0) for m, d in zip(m_alls, dirs)]
    invs = [eye + jnp.where(consts[d][3][0], m, 0.0) for m, d in zip(m_abs, dirs)]
    for level in range(1, len(consts[0][3])):
        inner = [_bdot(jnp.where(consts[d][3][level], m, 0.0), x) for m, x, d in zip(m_abs, invs, dirs)]
        invs = [x + _bdot(x, y) for x, y in zip(invs, inner)]
    mv = [_bdot(m, q["vs"]) for m, q in zip(m_aks, pre)]
    solved = [_bdot(x, jnp.concatenate([q["a_s"], y], axis=1)) for x, q, y in zip(invs, pre, mv)]
    zms = [jnp.concatenate([sv, jnp.concatenate([zeros, q["vs"]], axis=1)], axis=0) for sv, q in zip(solved, pre)]
    ry1s = [jnp.concatenate([q["r_s"], zeros], axis=1) + _bdot(m, z) for q, m, z in zip(pre, m_lows, zms)]
    pqs = [_bdot_tn(z, q["bk_end"]) for z, q in zip(zms, pre)]
    out = []
    for ry1, pq, q in zip(ry1s, pqs, pre):
        folded = ry1[:t] + ry1[t:]
        out.append((folded[:, :w], folded[:, w:], pq[:w], pq[w:], q["g"]))
    return out


def _rwkv_kernel(rr_ref, rk_ref, rv_ref, rz_ref, wd_ref, ad_ref, mu_ref, kk_ref, ka_ref, rkk_ref,
                 lnw_ref, lnb_ref, w0_ref, a0_ref, wup_ref, aup_ref, out_ref,
                 r_scr, v_scr, kk_scr, lw_scr, ka_scr, kt_scr, bonus_scr, y_scr, ry_scr, pt_scr, qt_scr, g_scr,
                 *, n_ctx):
    s = rr_ref.shape[1]
    w = rr_ref.shape[2]
    p_rows = RWKV_PREP_ROWS
    t = RWKV_CHUNK
    n_chunks = s // t
    n_ctx_chunks = n_ctx // t
    head_sum = ((_iota((w, w), 0) // RWKV_HEAD) == (_iota((w, w), 1) // RWKV_HEAD)).astype(BF16)
    inv_head = 1.0 / RWKV_HEAD

    def prep(j, carry):
        t0 = pl.multiple_of(j * p_rows, p_rows)
        mixed = []
        for idx, src in enumerate((rr_ref, rk_ref, rv_ref)):
            cur, down, up = _chunk_with_neighbors(src, t0, p_rows, s, s - n_ctx)
            mixed.append(cur + mu_ref[idx:idx + 1, :] * (0.5 * (down + up) - cur))
        r, kr, v = mixed
        kk = kr * kk_ref[...]
        norm = jnp.sqrt(_dot_sel(kk * kk, head_sum))
        kk = kk / jnp.maximum(norm, 1e-12)
        w_raw = _bdot(jnp.tanh(wd_ref[0, pl.ds(t0, p_rows), :]), wup_ref[0]) + w0_ref[0]
        a = _sigmoid(_bdot(ad_ref[0, pl.ds(t0, p_rows), :], aup_ref[0]) + a0_ref[0])
        lw = -math.exp(-0.5) * _sigmoid(w_raw)
        kt_sum = jnp.zeros_like(kr)
        for d in range(2):
            a_d = a[:, d * w:(d + 1) * w]
            kt_d = kr * (1.0 + (a_d - 1.0) * ka_ref[...])
            kt_sum = kt_sum + kt_d
            lw_scr[d, pl.ds(t0, p_rows), :] = lw[:, d * w:(d + 1) * w]
            ka_scr[d, pl.ds(t0, p_rows), :] = kk * a_d
            kt_scr[d, pl.ds(t0, p_rows), :] = kt_d
        coef = _dot_sel(r * kt_sum * rkk_ref[...], head_sum)
        r_scr[pl.ds(t0, p_rows), :] = r
        v_scr[pl.ds(t0, p_rows), :] = v
        kk_scr[pl.ds(t0, p_rows), :] = kk
        bonus_scr[pl.ds(t0, p_rows), :] = coef * v
        return carry

    lax.fori_loop(0, s // p_rows, prep, 0)

    n2 = 2 * t
    r_i = _iota((n2, n2), 0)
    c_i = _iota((n2, n2), 1)
    same = (r_i // t) == (c_i // t)
    rt = r_i % t
    ct = c_i % t
    eye = (r_i == c_i).astype(F32)
    lane_lo = _iota((t, w), 1) < RWKV_HEAD
    tri_r = _iota((t, t), 0)
    tri_c = _iota((t, t), 1)
    consts = []
    for reverse in (False, True):
        strict = jnp.logical_and(same, (ct > rt) if reverse else (ct < rt))
        incl = jnp.logical_and(same, (ct >= rt) if reverse else (ct <= rt))
        tri = ((tri_c >= tri_r) if reverse else (tri_c <= tri_r)).astype(BF16)
        merges = []
        c = 1
        while c < t:
            hi_r = (r_i % (2 * c)) >= c
            hi_c = (c_i % (2 * c)) >= c
            cross = jnp.logical_and(hi_c, jnp.logical_not(hi_r)) if reverse else jnp.logical_and(hi_r, jnp.logical_not(hi_c))
            merges.append(jnp.logical_and((r_i // (2 * c)) == (c_i // (2 * c)), cross))
            c *= 2
        consts.append((tri, strict, jnp.concatenate([incl, incl], axis=1), merges))

    def operators(gi, carry):
        problems, where = [], []
        for k in range(RWKV_GROUP):
            chunk = gi * RWKV_GROUP + k
            sl = pl.ds(pl.multiple_of(chunk * t, t), t)
            r, v, kk = r_scr[sl, :], v_scr[sl, :], kk_scr[sl, :]
            for d in range(2):
                problems.append((r, v, kk, lw_scr[d, sl, :], ka_scr[d, sl, :], kt_scr[d, sl, :], d))
                where.append((d, chunk, sl))
        for (d, chunk, sl), (ry, y1, pt, qt, g) in zip(where, _rwkv_chunk_operators(problems, consts, eye, lane_lo)):
            ry_scr[d, sl, :] = ry.astype(BF16)
            y_scr[d, sl, :] = y1
            pt_scr[d, chunk] = pt.astype(BF16)
            qt_scr[d, chunk] = qt
            g_scr[d, chunk] = g
        return carry

    lax.fori_loop(0, n_chunks // RWKV_GROUP, operators, 0)

    def scan(i, carry):
        states = []
        for d, (chunk, ht) in enumerate(zip(_scan_chunks(i, n_chunks, n_ctx_chunks), carry)):
            sl = pl.ds(pl.multiple_of(chunk * t, t), t)
            y_scr[d, sl, :] = _bdot_nt(ry_scr[d, sl, :], ht) + y_scr[d, sl, :]
            states.append(ht * g_scr[d, chunk] + _bdot(ht, pt_scr[d, chunk]) + qt_scr[d, chunk])
        return tuple(states)

    zero_state = jnp.zeros((w, w), F32)
    lax.fori_loop(0, n_chunks, scan, (zero_state, zero_state))

    def finish(j, carry):
        t0 = pl.multiple_of(j * p_rows, p_rows)
        sl = pl.ds(t0, p_rows)
        y = y_scr[0, sl, :] + y_scr[1, sl, :] + bonus_scr[sl, :]
        mu = _dot_sel(y, head_sum) * inv_head
        yc = y - mu
        var = _dot_sel(yc * yc, head_sum) * inv_head
        yn = yc * lax.rsqrt(var + RWKV_LN_EPSILON) * lnw_ref[...] + lnb_ref[...]
        out_ref[0, sl, :] = yn * _silu(rz_ref[0, sl, :])
        return carry

    lax.fori_loop(0, s // p_rows, finish, 0)


def _rwkv(u, col0, p, n_ctx):
    bsz, s, _ = u.shape
    w = LANES
    width = p["mu"].shape[1]
    n_pairs = width // w
    base = col0 // w
    col = lambda k: pl.BlockSpec((1, s, w), lambda b, h, k=k: (b, 0, base + k * n_pairs + h))
    lora = lambda k: pl.BlockSpec((1, s, w), lambda b, h, k=k: (b, 0, base + 4 * n_pairs + k))
    vec = lambda rows: pl.BlockSpec((rows, w), lambda b, h: (0, h))
    cat = pl.BlockSpec((1, 1, 2 * w), lambda b, h: (h, 0, 0))
    up = pl.BlockSpec((1, w, 2 * w), lambda b, h: (h, 0, 0))
    seq = pltpu.VMEM((s, w), F32)
    seq2 = pltpu.VMEM((2, s, w), F32)
    n_chunks = s // RWKV_CHUNK
    assert n_chunks % RWKV_GROUP == 0 and s % RWKV_PREP_ROWS == 0 and n_ctx % RWKV_PREP_ROWS == 0
    operators = [pltpu.VMEM((2, s, w), BF16), pltpu.VMEM((2, n_chunks, w, w), BF16),
                 pltpu.VMEM((2, n_chunks, w, w), F32), pltpu.VMEM((2, n_chunks, 1, w), F32)]
    return pl.pallas_call(
        functools.partial(_rwkv_kernel, n_ctx=n_ctx),
        grid=(bsz, n_pairs),
        in_specs=[col(0), col(1), col(2), col(3), lora(0), lora(1),
                  vec(3), vec(1), vec(1), vec(1), vec(1), vec(1), cat, cat, up, up],
        out_specs=pl.BlockSpec((1, s, w), lambda b, h: (b, 0, h)),
        out_shape=jax.ShapeDtypeStruct((bsz, s, width), F32),
        scratch_shapes=[seq, seq, seq, seq2, seq2, seq2, seq, seq2] + operators,
        compiler_params=_compiler_params(("parallel", "arbitrary")),
        name="rwkv7_mixer",
    )(u, u, u, u, u, u, p["mu"], p["k_k"], p["k_a"], p["r_k"], p["ln_w"], p["ln_b"],
      p["w0"], p["a0"], p["w_up"], p["a_up"])


def _hgrn_level_masks(t, w):
    rid = _iota((t, w), 0)
    r_i = _iota((t, t), 0)
    c_i = _iota((t, t), 1)
    levels = []
    c = 1
    while c < t:
        same_block = (r_i // (2 * c)) == (c_i // (2 * c))
        up_r = (r_i % (2 * c)) >= c
        up_c = (c_i % (2 * c)) >= c
        pair = [jnp.logical_and(same_block, jnp.logical_and(up_r, jnp.logical_not(up_c))),
                jnp.logical_and(same_block, jnp.logical_and(up_c, jnp.logical_not(up_r)))]
        levels.append((c, (rid % (2 * c)) >= c, pair))
        c *= 2
    return levels


def _hgrn_chunk_operators(problems, lb, tris, levels):
    t, w = problems[0][0].shape
    r_i = _iota((t, t), 0)
    c_i = _iota((t, t), 1)
    zero_row = jnp.zeros((1, w), F32)
    dirs = [p[3] for p in problems]
    lgs, ks = [], []
    for q, v, ff, d in problems:
        e = jnp.exp(-jnp.abs(ff))
        big = 1.0 / (1.0 + e)
        small = e / (1.0 + e)
        pos = ff >= 0.0
        lgs.append(jnp.log(lb + (1.0 - lb) * jnp.where(pos, big, small)))
        ks.append((1.0 - lb) * jnp.where(pos, small, big))
    bs = [_sel_dot(tris[d], lg) for lg, d in zip(lgs, dirs)]
    befores = [_shifted(b, zero_row, zero_row)[1 if d == 1 else 0] for b, d in zip(bs, dirs)]
    edges = list(bs)
    accs = [jnp.where(r_i == c_i, jnp.sum(p[0] * k, axis=-1, keepdims=True), 0.0) for p, k in zip(problems, ks)]
    for c, upper, pair in levels:
        qts = [p[0] * jnp.exp(b - before) for p, b, before in zip(problems, bs, befores)]
        kts = [k * jnp.exp(edge - b) for k, b, edge in zip(ks, bs, edges)]
        prods = [_bdot_nt(qt, kt) for qt, kt in zip(qts, kts)]
        accs = [a + jnp.where(pair[d], pr, 0.0) for a, pr, d in zip(accs, prods, dirs)]
        for i, d in enumerate(dirs):
            if d == 1:
                befores[i] = jnp.where(upper, befores[i], pltpu.roll(befores[i], t - c, 0))
                edges[i] = jnp.where(upper, pltpu.roll(edges[i], c, 0), edges[i])
            else:
                befores[i] = jnp.where(upper, pltpu.roll(befores[i], c, 0), befores[i])
                edges[i] = jnp.where(upper, edges[i], pltpu.roll(edges[i], t - c, 0))
    o_intras = [_bdot(a, p[1]) for a, p in zip(accs, problems)]
    b_ends = [b[(0 if d == 1 else t - 1):(1 if d == 1 else t)] for b, d in zip(bs, dirs)]
    kvs = [_bdot_tn(p[1], k * jnp.exp(be - b)) for p, k, b, be in zip(problems, ks, bs, b_ends)]
    return [(p[0] * jnp.exp(b), oi, kv, jnp.exp(be)) for p, b, oi, kv, be in zip(problems, bs, o_intras, kvs, b_ends)]


def _hgrn_kernel(q_ref, i_ref, ff_ref, fb_ref, z_ref, lb_ref, ng_ref, out_ref, o_scr, qe_scr, kv_scr, g_scr,
                 *, n_ctx, layer):
    s = q_ref.shape[1]
    dh = q_ref.shape[2]
    t = MIX_CHUNK
    n_chunks = s // t
    n_ctx_chunks = n_ctx // t
    lbs = lb_ref[...]
    ex = jnp.exp(lbs - jnp.max(lbs, axis=0, keepdims=True))
    probs = ex / jnp.sum(ex, axis=0, keepdims=True)
    csum = probs[0:1]
    for l in range(1, layer + 1):
        csum = csum + probs[l:l + 1]
    lb = csum - probs[0:1]
    tri_r = _iota((t, t), 0)
    tri_c = _iota((t, t), 1)
    tris = [(tri_c <= tri_r).astype(BF16), (tri_c >= tri_r).astype(BF16)]
    levels = _hgrn_level_masks(t, dh)

    def operators(gi, carry):
        problems, where = [], []
        for kk in range(HGRN_GROUP):
            chunk = gi * HGRN_GROUP + kk
            sl = pl.ds(pl.multiple_of(chunk * t, t), t)
            q, v = q_ref[0, sl, :], i_ref[0, sl, :]
            for d, f_ref in enumerate((ff_ref, fb_ref)):
                problems.append((q, v, f_ref[0, sl, :], d))
                where.append((d, chunk, sl))
        for (d, chunk, sl), (qe, o_intra, kv, g) in zip(where, _hgrn_chunk_operators(problems, lb, tris, levels)):
            qe_scr[d, sl, :] = qe.astype(BF16)
            o_scr[d, sl, :] = o_intra
            kv_scr[d, chunk] = kv
            g_scr[d, chunk] = g
        return carry

    lax.fori_loop(0, n_chunks // HGRN_GROUP, operators, 0)

    def scan(i, carry):
        chunks = _scan_chunks(i, n_chunks, n_ctx_chunks)
        sls = [pl.ds(pl.multiple_of(c * t, t), t) for c in chunks]
        inters = [_bdot_nt(qe_scr[d, sl, :], st) for d, (sl, st) in enumerate(zip(sls, carry))]
        new = []
        for d, (chunk, sl, inter, st) in enumerate(zip(chunks, sls, inters, carry)):
            o_scr[d, sl, :] = o_scr[d, sl, :] + inter
            new.append(st * g_scr[d, chunk] + kv_scr[d, chunk])
        return tuple(new)

    zero_state = jnp.zeros((dh, dh), F32)
    lax.fori_loop(0, n_chunks, scan, (zero_state, zero_state))

    def finish(j, carry):
        sl = pl.ds(pl.multiple_of(j * t, t), t)
        o = o_scr[0, sl, :] + o_scr[1, sl, :]
        y = o * lax.rsqrt(jnp.mean(o * o, axis=-1, keepdims=True) + RMS_EPS) * ng_ref[...]
        out_ref[0, sl, :] = y * _silu(z_ref[0, sl, :])
        return carry

    lax.fori_loop(0, n_chunks - n_ctx_chunks, finish, 0)


def _hgrn(u, lb_all, norm_g, n_ctx, layer):
    bsz, s, _ = u.shape
    nh = HGRN_N_HEADS
    dh = LANES
    width = nh * dh
    depth = lb_all.shape[0]
    n_chunks = s // MIX_CHUNK
    assert n_chunks % HGRN_GROUP == 0 and n_ctx % MIX_CHUNK == 0
    col = lambda k: pl.BlockSpec((1, s, dh), lambda b, h, k=k: (b, 0, k * nh + h))
    return pl.pallas_call(
        functools.partial(_hgrn_kernel, n_ctx=n_ctx, layer=layer),
        grid=(bsz, nh),
        in_specs=[col(0), col(1), col(2), col(3), col(4),
                  pl.BlockSpec((depth, dh), lambda b, h: (0, h)),
                  pl.BlockSpec((1, dh), lambda b, h: (0, h))],
        out_specs=pl.BlockSpec((1, s - n_ctx, dh), lambda b, h: (b, 0, h)),
        out_shape=jax.ShapeDtypeStruct((bsz, s - n_ctx, width), F32),
        scratch_shapes=[pltpu.VMEM((2, s, dh), F32), pltpu.VMEM((2, s, dh), BF16),
                        pltpu.VMEM((2, n_chunks, dh, dh), F32), pltpu.VMEM((2, n_chunks, 1, dh), F32)],
        compiler_params=_compiler_params(("parallel", "arbitrary")),
        name="hgrn2_mixer",
    )(u, u, u, u, u, lb_all, norm_g.reshape(1, width))


def _hyena_filter_kernel(z_ref, w1_ref, b1_ref, w2_ref, b2_ref, w3f_ref, w3b_ref, dl_ref, hf_ref, hb_ref):
    hp = functools.partial(jnp.dot, precision=lax.Precision.HIGHEST, preferred_element_type=F32)
    n = z_ref.shape[0]
    hid = jnp.sin(hp(z_ref[...], w1_ref[...]) + b1_ref[...])
    hid = jnp.sin(hp(hid, w2_ref[...]) + b2_ref[...])
    pos = _iota((n, 1), 0).astype(F32) * (1.0 / n)
    window = jnp.exp(-pos * dl_ref[...]) + HYENA_SHIFT
    f0 = hp(hid, w3f_ref[...]) * window
    f1 = hp(hid, w3b_ref[...]) * window
    nrm = jnp.sum(jnp.abs(f0), axis=0, keepdims=True) + jnp.sum(jnp.abs(f1), axis=0, keepdims=True)
    hf_ref[...] = f0 / nrm
    hb_ref[...] = f1 / nrm


def _hyena_filters(n, w1, b1, w2, b2, w3, width):
    pos = np.arange(n, dtype=np.float64)
    bands = np.linspace(1e-4, HYENA_N_BANDS - 1, HYENA_N_BANDS)
    ang = (2.0 * math.pi / n) * pos[:, None] * bands
    z = np.concatenate([(pos / n)[:, None], np.cos(ang), np.sin(ang)], axis=-1)
    z = np.pad(z, ((0, 0), (0, LANES - z.shape[1]))).astype(np.float32)
    max_decay = math.log(HYENA_TGT) / HYENA_FAST
    min_decay = math.log(HYENA_TGT) / HYENA_SLOW
    deltas = np.abs(np.linspace(min_decay, max_decay, width)).astype(np.float32)[None]
    feat, hid = w1.shape
    w1p = jnp.pad(w1, ((0, LANES - feat), (0, LANES - hid)))
    w2p = jnp.pad(w2, ((0, LANES - hid), (0, LANES - hid)))
    w3p = jnp.pad(w3, ((0, LANES - hid), (0, 0)))
    b1p = jnp.pad(b1, (0, LANES - hid)).reshape(1, LANES)
    b2p = jnp.pad(b2, (0, LANES - hid)).reshape(1, LANES)
    n_tiles = width // LANES
    full = lambda shape: pl.BlockSpec(shape, lambda j: (0, 0))
    out = pl.BlockSpec((n, LANES), lambda j: (0, j))
    hf, hb = pl.pallas_call(
        _hyena_filter_kernel,
        grid=(n_tiles,),
        in_specs=[full((n, LANES)), full((LANES, LANES)), full((1, LANES)), full((LANES, LANES)), full((1, LANES)),
                  pl.BlockSpec((LANES, LANES), lambda j: (0, j)),
                  pl.BlockSpec((LANES, LANES), lambda j: (0, n_tiles + j)),
                  pl.BlockSpec((1, LANES), lambda j: (0, j))],
        out_specs=[out, out],
        out_shape=[jax.ShapeDtypeStruct((n, width), F32)] * 2,
        compiler_params=_compiler_params(("arbitrary",)),
        name="hyena_filters",
    )(jnp.asarray(z), w1p, b1p, w2p, b2p, w3p, w3p, jnp.asarray(deltas))
    return jnp.concatenate([hf, hb], axis=1)


def _hyena_pre_kernel(yv_ref, y0_ref, y1_ref, yz_ref, swv_ref, sw0_ref, sw1_ref, sbv_ref, sb0_ref, sb1_ref,
                      yb_ref, p_ref, e_ref, pb_ref, *, n_ctx):
    s = yv_ref.shape[1]
    rows = MIX_CHUNK

    def body(j, carry):
        t0 = pl.multiple_of(j * rows, rows)
        conv = []
        for src, sw, sb in ((yv_ref, swv_ref, sbv_ref), (y0_ref, sw0_ref, sb0_ref), (y1_ref, sw1_ref, sb1_ref)):
            cur, down, up = _chunk_with_neighbors(src, t0, rows, s, s - n_ctx)
            conv.append(down * sw[0:1, :] + cur * sw[1:2, :] + up * sw[2:3, :] + sb[...])
        v, x0, x1 = conv
        p = x1 * v
        o0 = pl.multiple_of(j * rows, rows)
        p_ref[0, pl.ds(o0, rows), :] = p.astype(BF16)
        pb_ref[0, pl.ds(o0, rows), :] = p * yb_ref[...]
        e_ref[0, pl.ds(o0, rows), :] = x0 * _silu(yz_ref[0, pl.ds(t0, rows), :])
        return carry

    lax.fori_loop(0, (s - n_ctx) // rows, body, 0)


def _hyena_pre(u, col0, short_w, short_b, y_bias, n_ctx):
    bsz, s, _ = u.shape
    w = y_bias.shape[0]
    tiles = w // LANES
    base = col0 // LANES
    n = s - n_ctx
    col = lambda k: pl.BlockSpec((1, s, LANES), lambda b, j, k=k: (b, 0, base + k * tiles + j))
    par = lambda rows, k: pl.BlockSpec((rows, LANES), lambda b, j, k=k: (0, k * tiles + j))
    out = pl.BlockSpec((1, n, LANES), lambda b, j: (b, 0, j))
    sb = short_b.reshape(1, 3 * w)
    return pl.pallas_call(
        functools.partial(_hyena_pre_kernel, n_ctx=n_ctx),
        grid=(bsz, tiles),
        in_specs=[col(0), col(1), col(2), col(3),
                  par(3, 0), par(3, 1), par(3, 2), par(1, 0), par(1, 1), par(1, 2), par(1, 0)],
        out_specs=[out, out, out],
        out_shape=[jax.ShapeDtypeStruct((bsz, n, w), BF16),
                   jax.ShapeDtypeStruct((bsz, n, w), F32),
                   jax.ShapeDtypeStruct((bsz, n, w), F32)],
        compiler_params=_compiler_params(("parallel", "arbitrary")),
        name="hyena_short_conv",
    )(u, u, u, u, short_w, short_w, short_w, sb, sb, sb, y_bias.reshape(1, w))


def _dft_tables(n):
    big = 2 * n
    half = DFT_TILE // 2
    idx = jnp.arange(n, dtype=jnp.int32)
    split = DFT_SPLIT
    lo = jnp.arange(split, dtype=jnp.int32)
    hi = jnp.arange(n // split, dtype=jnp.int32)
    ang_lo = ((lo[:, None] * idx[None, :]) % big).astype(F32) * (2.0 * math.pi / big)
    ang_hi = ((hi[:, None] * idx[None, :]) % (big // split)).astype(F32) * (2.0 * math.pi * split / big)
    c_lo, s_lo = jnp.cos(ang_lo)[None], jnp.sin(ang_lo)[None]
    c_hi, s_hi = jnp.cos(ang_hi)[:, None], jnp.sin(ang_hi)[:, None]
    cos = (c_hi * c_lo - s_hi * s_lo).reshape(n, n)
    sin = (s_hi * c_lo + c_hi * s_lo).reshape(n, n)
    alt = jnp.where(idx % 2 == 0, 1.0, -1.0).astype(F32)
    first_row = (idx == 0)[:, None]
    im = jnp.where(first_row, alt[None, :], -sin)
    fwd = jnp.stack([cos.reshape(n // half, half, n), im.reshape(n // half, half, n)], axis=1).reshape(big, n)
    weight = jnp.where(first_row, 1.0, 2.0) * (1.0 / big)
    weight = jnp.stack([weight.reshape(n // half, half, 1)] * 2, axis=1).reshape(big, 1)
    return fwd.astype(BF16), (fwd * weight).T.astype(BF16)


def _spectrum_kernel(f_ref, x_ref, o_ref):
    o_ref[...] = jnp.dot(f_ref[...], x_ref[...].astype(BF16), preferred_element_type=F32)


def _filter_spectrum(fwd, hk):
    big, n = fwd.shape
    cols = hk.shape[1]
    return pl.pallas_call(
        _spectrum_kernel,
        grid=(big // DFT_TILE,),
        in_specs=[pl.BlockSpec((DFT_TILE, n), lambda i: (i, 0)),
                  pl.BlockSpec((n, cols), lambda i: (0, 0))],
        out_specs=pl.BlockSpec((DFT_TILE, cols), lambda i: (i, 0)),
        out_shape=jax.ShapeDtypeStruct((big, cols), F32),
        compiler_params=_compiler_params(("parallel",)),
        name="hyena_filter_spectrum",
    )(fwd, hk)


def _conv_spectrum_kernel(f_ref, p_ref, ks_ref, z_ref):
    i = pl.program_id(1)
    half = DFT_TILE // 2
    w = p_ref.shape[2]
    acc = jnp.dot(f_ref[...], p_ref[0], preferred_element_type=F32)
    s_re, s_im = acc[:half], acc[half:]
    ks = ks_ref[...]
    k_re = ks[:half, :w] + ks[:half, w:]
    k_im = ks[half:, :w] - ks[half:, w:]
    z_re = s_re * k_re - s_im * k_im
    z_im = s_re * k_im + s_im * k_re
    packed = jnp.logical_and(_iota((half, w), 0) == 0, i == 0)
    z_re = jnp.where(packed, s_re * k_re, z_re)
    z_im = jnp.where(packed, s_im * (ks[half:, :w] + ks[half:, w:]), z_im)
    z_ref[0, :half, :] = z_re.astype(BF16)
    z_ref[0, half:, :] = z_im.astype(BF16)


def _conv_spectrum(fwd, p16, kspec):
    bsz, n, w = p16.shape
    big = fwd.shape[0]
    return pl.pallas_call(
        _conv_spectrum_kernel,
        grid=(bsz, big // DFT_TILE),
        in_specs=[pl.BlockSpec((DFT_TILE, n), lambda b, i: (i, 0)),
                  pl.BlockSpec((1, n, w), lambda b, i: (b, 0, 0)),
                  pl.BlockSpec((DFT_TILE, 2 * w), lambda b, i: (i, 0))],
        out_specs=pl.BlockSpec((1, DFT_TILE, w), lambda b, i: (b, i, 0)),
        out_shape=jax.ShapeDtypeStruct((bsz, big, w), BF16),
        compiler_params=_compiler_params(("parallel", "arbitrary")),
        name="hyena_forward_dft",
    )(fwd, p16, kspec)


def _conv_inverse_kernel(g_ref, z_ref, e_ref, pb_ref, o_ref):
    y = jnp.dot(g_ref[...], z_ref[0], preferred_element_type=F32)
    o_ref[0] = e_ref[0] * (y + pb_ref[0])


def _conv_inverse(inv, z16, e, pb):
    bsz, big, w = z16.shape
    n = inv.shape[0]
    tile = DFT_TILE
    tok = pl.BlockSpec((1, tile, w), lambda b, i: (b, i, 0))
    return pl.pallas_call(
        _conv_inverse_kernel,
        grid=(bsz, n // tile),
        in_specs=[pl.BlockSpec((tile, big), lambda b, i: (i, 0)),
                  pl.BlockSpec((1, big, w), lambda b, i: (b, 0, 0)),
                  tok, tok],
        out_specs=tok,
        out_shape=jax.ShapeDtypeStruct((bsz, n, w), F32),
        compiler_params=_compiler_params(("parallel", "arbitrary")),
        name="hyena_inverse_dft",
    )(inv, z16, e, pb)


def _even_weight_layout(w_in, gate_b):
    d = w_in.shape[0]
    mw = MLSTM_N_HEADS * LANES
    g0 = 5 * mw
    g1 = g0 + 4 * MLSTM_N_HEADS
    main = jnp.concatenate([w_in[:, :g0], w_in[:, g1:]], axis=1).astype(BF16)
    wg = w_in[:, g0:g1].reshape(d, 2, 2, MLSTM_N_HEADS)
    wg = jnp.transpose(wg, (3, 1, 2, 0)).reshape(MLSTM_N_HEADS, 4, d)
    wg = jnp.concatenate([wg, jnp.zeros_like(wg)], axis=1).reshape(MLSTM_N_HEADS * 8, d).astype(BF16)
    gb = jnp.transpose(gate_b.reshape(2, 2, MLSTM_N_HEADS), (2, 0, 1)).reshape(MLSTM_N_HEADS, 4)
    gb = jnp.concatenate([gb, jnp.zeros_like(gb)], axis=1).reshape(MLSTM_N_HEADS * 8, 1)
    return main, wg, jnp.broadcast_to(gb, (MLSTM_N_HEADS * 8, LANES))


def _rwkv_params(mu, w0, w_up, a0, a_up, k_k, k_a, r_k, ln_w, ln_b):
    width = mu.shape[1]
    n_pairs = width // LANES
    row = lambda x: x.reshape(1, width)

    def cat_dirs(x):
        return jnp.transpose(x.reshape(2, n_pairs, LANES), (1, 0, 2)).reshape(n_pairs, 1, 2 * LANES)

    def block_up(x):
        lora = x.shape[1]
        xp = jnp.transpose(x.reshape(2, lora, n_pairs, LANES), (2, 0, 1, 3))
        z = jnp.zeros_like(xp[:, 0])
        top = jnp.concatenate([xp[:, 0], z], axis=2)
        bot = jnp.concatenate([z, xp[:, 1]], axis=2)
        return jnp.concatenate([top, bot], axis=1).astype(BF16)

    return {"mu": mu, "k_k": row(k_k), "k_a": row(k_a), "r_k": row(r_k), "ln_w": row(ln_w), "ln_b": row(ln_b),
            "w0": cat_dirs(w0), "a0": cat_dirs(a0), "w_up": block_up(w_up), "a_up": block_up(a_up)}


def _raster_to_column(h):
    b, n, d = h.shape
    rows = n // GRID_WIDTH
    return h.reshape(b, rows, GRID_WIDTH, d).transpose(0, 2, 1, 3).reshape(b, n, d)


def _column_to_raster(h):
    b, n, d = h.shape
    rows = n // GRID_WIDTH
    return h.reshape(b, GRID_WIDTH, rows, d).transpose(0, 2, 1, 3).reshape(b, n, d)


def kernel(x, c, ctx, c_ctx, l0_norm_g, l0_mod_w, l0_mod_b, l0_w_in, l0_w_out, l0_mlstm_conv_w, l0_mlstm_gate_b, l0_mlstm_norm_g, l0_rwkv_mu, l0_rwkv_w0, l0_rwkv_w_up, l0_rwkv_a0, l0_rwkv_a_up, l0_rwkv_k_k, l0_rwkv_k_a, l0_rwkv_r_k, l0_rwkv_ln_w, l0_rwkv_ln_b, hgrn_lower_bounds, l1_norm_g, l1_mod_w, l1_mod_b, l1_w_in, l1_w_out, l1_hgrn_norm_g, l1_hyena_short_w, l1_hyena_short_b, l1_hyena_w1, l1_hyena_b1, l1_hyena_w2, l1_hyena_b2, l1_hyena_w3, l1_hyena_bias, final_norm_g):
    bsz, n_lat, d = x.shape
    n_ctx = ctx.shape[1]

    pad = (-(bsz + 1)) % 8
    cc = jnp.concatenate([c, c_ctx[None], jnp.zeros((pad, d), F32)], axis=0)
    mod0, mod1 = _modulation(cc, l0_mod_w, l0_mod_b, l1_mod_w, l1_mod_b)
    mod0 = mod0[:bsz + 1].reshape(bsz + 1, 3, d)
    mod1 = mod1[:bsz + 1].reshape(bsz + 1, 3, d)

    w_main, w_gate, b_gate = _even_weight_layout(l0_w_in, l0_mlstm_gate_b)
    n0 = w_main.shape[1]
    u0, gt0 = _proj_in(x, ctx, l0_norm_g, mod0, w_main, n0 // 2, w_gate, b_gate)
    gt0 = gt0.reshape(bsz, MLSTM_N_HEADS, 8, n_ctx + n_lat)
    y_m = _mlstm(u0, gt0, l0_mlstm_conv_w, l0_mlstm_norm_g, n_ctx)
    rp = _rwkv_params(l0_rwkv_mu, l0_rwkv_w0, l0_rwkv_w_up, l0_rwkv_a0, l0_rwkv_a_up, l0_rwkv_k_k,
                      l0_rwkv_k_a, l0_rwkv_r_k, l0_rwkv_ln_w, l0_rwkv_ln_b)
    y_r = _rwkv(u0, 5 * MLSTM_N_HEADS * LANES, rp, n_ctx)
    x1, ctx1 = _proj_out(y_m, y_r, x, ctx, mod0, l0_w_out.astype(BF16))

    x1c = _raster_to_column(x1)
    w1 = l1_w_in.astype(BF16)
    (u1,) = _proj_in(x1c, ctx1, l1_norm_g, mod1, w1, w1.shape[1] // 2)
    y_g = _hgrn(u1, hgrn_lower_bounds, l1_hgrn_norm_g, n_ctx, layer=1)
    hw = l1_hyena_bias.shape[0]
    hk = _hyena_filters(n_lat, l1_hyena_w1, l1_hyena_b1, l1_hyena_w2, l1_hyena_b2, l1_hyena_w3, hw)
    fwd, inv = _dft_tables(n_lat)
    kspec = _filter_spectrum(fwd, hk)
    p16, e, pb = _hyena_pre(u1, 5 * HGRN_N_HEADS * LANES, l1_hyena_short_w, l1_hyena_short_b, l1_hyena_bias, n_ctx)
    z16 = _conv_spectrum(fwd, p16, kspec)
    y_y = _conv_inverse(inv, z16, e, pb)
    out_c = _proj_out_final(y_g, y_y, x1c, mod1, l1_w_out.astype(BF16), final_norm_g)
    return _column_to_raster(out_c)
```

```python
import functools
import math

import jax
import jax.numpy as jnp
import numpy as np
from jax import lax
from jax.experimental import pallas as pl
from jax.experimental.pallas import tpu as pltpu

F32 = jnp.float32
BF16 = jnp.bfloat16

GRID_WIDTH = 64
RMS_EPS = 1e-6
MLSTM_N_HEADS = 4
RWKV_HEAD = 64
RWKV_LN_EPSILON = 64e-5
HGRN_N_HEADS = 4
HYENA_N_BANDS = 16
HYENA_FAST = 0.3
HYENA_SLOW = 1.5
HYENA_TGT = 1e-2
HYENA_SHIFT = 0.05
LOG2_E = 1.0 / math.log(2.0)

LANES = 128
SUBLANES = 8
MXU_DIM = 256
VMEM_LIMIT = 52 * 1024 * 1024

MIX_CHUNK = 128
RWKV_CHUNK = 64
RWKV_GROUP = 9
RWKV_PREP_ROWS = 256
MLSTM_GROUP = 6
HGRN_GROUP = 3
PROJ_ROWS = 768
FINAL_ROWS = 512
DFT_TILE = 512
DFT_UNITS = 2
DFT_SPLIT = 64


def _bdot(a, b):
    return jnp.dot(a.astype(BF16), b.astype(BF16), preferred_element_type=F32)


def _bdot_nt(a, b):
    return lax.dot_general(a.astype(BF16), b.astype(BF16), (((1,), (1,)), ((), ())),
                           preferred_element_type=F32)


def _bdot_tn(a, b):
    return lax.dot_general(a.astype(BF16), b.astype(BF16), (((0,), (0,)), ((), ())),
                           preferred_element_type=F32)


def _split3(x):
    hi = x.astype(BF16)
    r1 = x - hi.astype(F32)
    mid = r1.astype(BF16)
    lo = (r1 - mid.astype(F32)).astype(BF16)
    return hi, mid, lo


def _sel_dot(sel, x):
    hi, mid, lo = _split3(x)
    d = functools.partial(jnp.dot, preferred_element_type=F32)
    return d(sel, hi) + d(sel, mid) + d(sel, lo)


def _dot_sel(x, sel):
    hi, mid, lo = _split3(x)
    d = functools.partial(jnp.dot, preferred_element_type=F32)
    return d(hi, sel) + d(mid, sel) + d(lo, sel)


def _sigmoid(x):
    return 1.0 / (1.0 + jnp.exp(-x))


def _silu(x):
    return x * _sigmoid(x)


def _iota(shape, dim):
    return lax.broadcasted_iota(jnp.int32, shape, dim)


def _neighbor_rows(ref, t0, rows, n_total, split):
    has_prev = jnp.logical_and(t0 != 0, t0 != split)
    has_next = jnp.logical_and(t0 + rows != split, t0 + rows != n_total)
    prev = ref[0, pl.ds(jnp.maximum(t0 - 1, 0), 1), :]
    nxt = ref[0, pl.ds(jnp.minimum(t0 + rows, n_total - 1), 1), :]
    return jnp.where(has_prev, prev, 0.0), jnp.where(has_next, nxt, 0.0)


def _shifted(cur, prev_row, next_row):
    rows = cur.shape[0]
    rid = _iota(cur.shape, 0)
    down = jnp.where(rid == 0, prev_row, pltpu.roll(cur, 1, 0))
    up = jnp.where(rid == rows - 1, next_row, pltpu.roll(cur, rows - 1, 0))
    return down, up


def _chunk_with_neighbors(ref, t0, rows, n_total, split):
    cur = ref[0, pl.ds(t0, rows), :]
    prev_row, next_row = _neighbor_rows(ref, t0, rows, n_total, split)
    down, up = _shifted(cur, prev_row, next_row)
    return cur, down, up


def _scan_chunks(i, n_chunks, n_ctx_chunks):
    fwd = jnp.where(i < n_ctx_chunks, n_chunks - n_ctx_chunks + i, i - n_ctx_chunks)
    return fwd, n_chunks - 1 - i


def _compiler_params(semantics):
    return pltpu.CompilerParams(dimension_semantics=semantics, vmem_limit_bytes=VMEM_LIMIT)


def _mod_kernel(c_ref, w0_ref, b0_ref, w1_ref, b1_ref, o0_ref, o1_ref):
    s = _silu(c_ref[...])
    o0_ref[...] = _bdot(s, w0_ref[...]) + b0_ref[...]
    o1_ref[...] = _bdot(s, w1_ref[...]) + b1_ref[...]


def _modulation(cc, w0, b0, w1, b1):
    rows, d = cc.shape
    n = w0.shape[1]
    tile = d
    grid = (n // tile,)
    wspec = pl.BlockSpec((d, tile), lambda j: (0, j))
    bspec = pl.BlockSpec((1, tile), lambda j: (0, j))
    ospec = pl.BlockSpec((rows, tile), lambda j: (0, j))
    return pl.pallas_call(
        _mod_kernel,
        grid=grid,
        in_specs=[pl.BlockSpec((rows, d), lambda j: (0, 0)), wspec, bspec, wspec, bspec],
        out_specs=[ospec, ospec],
        out_shape=[jax.ShapeDtypeStruct((rows, n), F32)] * 2,
        compiler_params=_compiler_params(("arbitrary",)),
        name="adaln_modulation",
    )(cc, w0, b0.reshape(1, n), w1, b1.reshape(1, n))


def _token_tile(x_ref, c_ref, i, rows):
    n_lat_tail = rows - c_ref.shape[1]
    is_ctx = jnp.logical_and(i == pl.num_programs(1) - 1, _iota((rows, 1), 0) >= n_lat_tail)
    ctx_rows = jnp.concatenate([jnp.zeros((n_lat_tail, c_ref.shape[2]), F32), c_ref[0]], axis=0)
    return jnp.where(is_ctx, ctx_rows, x_ref[0]), is_ctx


def _proj_in_kernel(*refs, rows, with_gates):
    if with_gates:
        x_ref, c_ref, g_ref, ml_ref, mc_ref, w_ref, wg_ref, gb_ref, u_ref, gt_ref, h_scr = refs
    else:
        x_ref, c_ref, g_ref, ml_ref, mc_ref, w_ref, u_ref, h_scr = refs
    i = pl.program_id(1)
    n = pl.program_id(2)

    @pl.when(n == 0)
    def _():
        x, is_ctx = _token_tile(x_ref, c_ref, i, rows)
        y = x * lax.rsqrt(jnp.mean(x * x, axis=-1, keepdims=True) + RMS_EPS) * g_ref[...]
        ml = ml_ref[0]
        mc = mc_ref[0]
        shift = jnp.where(is_ctx, mc[0:1], ml[0:1])
        scale = jnp.where(is_ctx, mc[1:2], ml[1:2])
        h = (y * (1.0 + scale) + shift).astype(BF16)
        h_scr[...] = h
        if with_gates:
            gt_ref[0] = _bdot_nt(wg_ref[...], h) + gb_ref[:, 0:1]

    u_ref[0] = jnp.dot(h_scr[...], w_ref[...], preferred_element_type=F32)


def _proj_in(x, ctx, norm_g, mod3, w16, n_tile, gate_w=None, gate_b=None):
    bsz, n_lat, d = x.shape
    n_ctx = ctx.shape[1]
    s = n_lat + n_ctx
    n = w16.shape[1]
    rows = PROJ_ROWS
    assert s % rows == 0 and (n_lat % rows) + n_ctx == rows
    grid = (bsz, s // rows, n // n_tile)
    ctx_row = mod3.shape[0] - 1
    with_gates = gate_w is not None
    in_specs = [
        pl.BlockSpec((1, rows, d), lambda b, i, j: (b, i, 0)),
        pl.BlockSpec((1, n_ctx, d), lambda b, i, j: (b, 0, 0)),
        pl.BlockSpec((1, d), lambda b, i, j: (0, 0)),
        pl.BlockSpec((1, 3, d), lambda b, i, j: (b, 0, 0)),
        pl.BlockSpec((1, 3, d), lambda b, i, j: (ctx_row, 0, 0)),
        pl.BlockSpec((d, n_tile), lambda b, i, j: (0, j)),
    ]
    args = [x, ctx, norm_g.reshape(1, d), mod3, mod3, w16]
    out_specs = [pl.BlockSpec((1, rows, n_tile), lambda b, i, j: (b, i, j))]
    out_shape = [jax.ShapeDtypeStruct((bsz, s, n), F32)]
    if with_gates:
        ng = gate_w.shape[0]
        in_specs += [pl.BlockSpec((ng, d), lambda b, i, j: (0, 0)),
                     pl.BlockSpec((ng, LANES), lambda b, i, j: (0, 0))]
        args += [gate_w, gate_b]
        out_specs.append(pl.BlockSpec((1, ng, rows), lambda b, i, j: (b, 0, i)))
        out_shape.append(jax.ShapeDtypeStruct((bsz, ng, s), F32))
    return pl.pallas_call(
        functools.partial(_proj_in_kernel, rows=rows, with_gates=with_gates),
        grid=grid,
        in_specs=in_specs,
        out_specs=out_specs,
        out_shape=out_shape,
        scratch_shapes=[pltpu.VMEM((rows, d), BF16)],
        compiler_params=_compiler_params(("parallel", "arbitrary", "arbitrary")),
        name="norm_mod_proj_in",
    )(*args)


def _proj_out_kernel(ya_ref, yb_ref, x_ref, c_ref, ml_ref, mc_ref, w_ref, ox_ref, oc_ref, *, rows):
    i = pl.program_id(1)
    half = ya_ref.shape[2]
    y = _bdot(ya_ref[0], w_ref[0:half, :]) + _bdot(yb_ref[0], w_ref[half:, :])
    x, is_ctx = _token_tile(x_ref, c_ref, i, rows)
    x = x + jnp.where(is_ctx, mc_ref[0][2:3], ml_ref[0][2:3]) * y
    ox_ref[0] = x

    @pl.when(i == pl.num_programs(1) - 1)
    def _():
        oc_ref[0] = x[rows - c_ref.shape[1]:]


def _proj_out(ya, yb, x, ctx, mod3, w16):
    bsz, n_lat, d = x.shape
    n_ctx = ctx.shape[1]
    s = n_lat + n_ctx
    half = ya.shape[2]
    rows = PROJ_ROWS
    assert s % rows == 0 and (n_lat % rows) + n_ctx == rows
    ctx_row = mod3.shape[0] - 1
    tok = lambda w: pl.BlockSpec((1, rows, w), lambda b, i: (b, i, 0))
    seg = pl.BlockSpec((1, n_ctx, d), lambda b, i: (b, 0, 0))
    return pl.pallas_call(
        functools.partial(_proj_out_kernel, rows=rows),
        grid=(bsz, s // rows),
        in_specs=[tok(half), tok(half), tok(d), seg,
                  pl.BlockSpec((1, 3, d), lambda b, i: (b, 0, 0)),
                  pl.BlockSpec((1, 3, d), lambda b, i: (ctx_row, 0, 0)),
                  pl.BlockSpec((2 * half, d), lambda b, i: (0, 0))],
        out_specs=[tok(d), seg],
        out_shape=[jax.ShapeDtypeStruct((bsz, n_lat, d), F32), jax.ShapeDtypeStruct((bsz, n_ctx, d), F32)],
        compiler_params=_compiler_params(("parallel", "arbitrary")),
        name="proj_out_residual",
    )(ya, yb, x, ctx, mod3, mod3, w16)


def _proj_out_final_kernel(ya_ref, yb_ref, x_ref, ml_ref, w_ref, fg_ref, o_ref):
    half = ya_ref.shape[2]
    y = _bdot(ya_ref[0], w_ref[0:half, :]) + _bdot(yb_ref[0], w_ref[half:, :])
    x = x_ref[0] + ml_ref[0][2:3] * y
    o_ref[0] = x * lax.rsqrt(jnp.mean(x * x, axis=-1, keepdims=True) + RMS_EPS) * fg_ref[...]


def _proj_out_final(ya, yb, x, mod3, w16, final_g):
    bsz, n_lat, d = x.shape
    half = ya.shape[2]
    rows = FINAL_ROWS
    assert n_lat % rows == 0
    tok = lambda w: pl.BlockSpec((1, rows, w), lambda b, i: (b, i, 0))
    return pl.pallas_call(
        _proj_out_final_kernel,
        grid=(bsz, n_lat // rows),
        in_specs=[tok(half), tok(half), tok(d),
                  pl.BlockSpec((1, 3, d), lambda b, i: (b, 0, 0)),
                  pl.BlockSpec((2 * half, d), lambda b, i: (0, 0)),
                  pl.BlockSpec((1, d), lambda b, i: (0, 0))],
        out_specs=tok(d),
        out_shape=jax.ShapeDtypeStruct((bsz, n_lat, d), F32),
        compiler_params=_compiler_params(("parallel", "arbitrary")),
        name="proj_out_final_norm",
    )(ya, yb, x, mod3, w16, final_g.reshape(1, d))


def _mlstm_chunk_operators(chunks, causal):
    t = chunks[0][0].shape[0]
    lane = _iota((8, t), 1)
    row_id = _iota((8, t), 0)
    log_fs = [jnp.minimum(c[3], 0.0) - jnp.log1p(jnp.exp(-jnp.abs(c[3]))) for c in chunks]
    cum_f, cum_b = list(log_fs), list(log_fs)
    sh = 1
    while sh < t:
        cum_f = [x + jnp.where(lane >= sh, pltpu.roll(x, sh, 1), 0.0) for x in cum_f]
        cum_b = [x + jnp.where(lane < t - sh, pltpu.roll(x, t - sh, 1), 0.0) for x in cum_b]
        sh *= 2
    pad = jnp.zeros((t - 8, t), F32)
    tiles = [jnp.concatenate([jnp.where(row_id % 2 == 0, c[3], jnp.where(row_id == 1, f, b)), pad], axis=0)
             for c, f, b in zip(chunks, cum_f, cum_b)]
    cols = [x.T for x in tiles]
    problems = []
    for c, f, b, col in zip(chunks, cum_f, cum_b, cols):
        for d in range(2):
            b_row = (f, b)[d][2 * d + 1:2 * d + 2]
            problems.append(dict(q=c[0], k=c[1], v_ext=c[2], d=d, ig_row=c[3][2 * d:2 * d + 1], b_row=b_row,
                                 ig_col=col[:, 2 * d:2 * d + 1], b_col=col[:, 2 * d + 1:2 * d + 2]))
    logws = [jnp.where(causal[p["d"]], p["b_col"] + (p["ig_row"] - p["b_row"]), -jnp.inf) for p in problems]
    mus = [jnp.max(x, axis=-1, keepdims=True) for x in logws]
    ws = [jnp.exp(x - mu) for x, mu in zip(logws, mus)]
    lasts = [0 if p["d"] == 1 else t - 1 for p in problems]
    b_lasts = [p["b_col"][i:i + 1] for p, i in zip(problems, lasts)]
    gammas = [mu[i:i + 1] for mu, i in zip(mus, lasts)]
    gks = [jnp.exp(bl - p["b_col"] + p["ig_col"] - gm) * p["k"] for p, bl, gm in zip(problems, b_lasts, gammas)]
    qks = [_bdot_nt(p["q"], p["k"]) * w for p, w in zip(problems, ws)]
    intras = [_bdot(qk, p["v_ext"]) for qk, p in zip(qks, problems)]
    kvs = [_bdot_tn(gk, p["v_ext"]) for gk, p in zip(gks, problems)]
    return [(intra, kv, p["b_col"], mu, bl, gm)
            for intra, kv, p, mu, bl, gm in zip(intras, kvs, problems, mus, b_lasts, gammas)]


def _mlstm_kernel(q_ref, k_ref, v_ref, o_ref, z_ref, gt_ref, cwq_ref, cwk_ref, ng_ref, out_ref,
                  qa_scr, ka_scr, h_scr, intra_scr, kv_scr, b_scr, mu_scr, tail_scr, *, n_ctx):
    s = q_ref.shape[1]
    dh = q_ref.shape[2]
    t = MIX_CHUNK
    n_chunks = s // t
    n_ctx_chunks = n_ctx // t
    k_scale = dh ** -0.5

    def prep(j, carry):
        t0 = pl.multiple_of(j * t, t)
        for src, cw, dst, scale in ((q_ref, cwq_ref, qa_scr, 1.0), (k_ref, cwk_ref, ka_scr, k_scale)):
            cur, down, up = _chunk_with_neighbors(src, t0, t, s, s - n_ctx)
            conv = down * cw[0:1, :] + cur * cw[1:2, :] + up * cw[2:3, :]
            dst[pl.ds(t0, t), :] = _silu(conv) * scale
        return carry

    lax.fori_loop(0, n_chunks, prep, 0)

    ones_col = (_iota((t, dh), 1) == 0).astype(F32)
    causal = [_iota((t, t), 1) <= _iota((t, t), 0), _iota((t, t), 1) >= _iota((t, t), 0)]

    def operators(gi, carry):
        chunks, where = [], []
        for kk in range(MLSTM_GROUP):
            chunk = gi * MLSTM_GROUP + kk
            sl = pl.ds(pl.multiple_of(chunk * t, t), t)
            v_ext = jnp.concatenate([v_ref[0, sl, :], ones_col], axis=1)
            chunks.append((qa_scr[sl, :], ka_scr[sl, :], v_ext, gt_ref[0, 0, :, sl]))
            where += [(0, chunk, sl), (1, chunk, sl)]
        for (d, chunk, sl), (intra, kv, b, mu, b_last, gamma) in zip(where, _mlstm_chunk_operators(chunks, causal)):
            intra_scr[d, sl, :] = intra
            kv_scr[d, chunk] = kv
            b_scr[d, sl, :] = b
            mu_scr[d, sl, :] = mu
            tail_scr[d, chunk] = jnp.concatenate([jnp.broadcast_to(b_last, (1, dh)), jnp.broadcast_to(gamma, (1, dh))],
                                                 axis=0)
        return carry

    lax.fori_loop(0, n_chunks // MLSTM_GROUP, operators, 0)

    def scan(i, carry):
        chunks = _scan_chunks(i, n_chunks, n_ctx_chunks)
        sls = [pl.ds(pl.multiple_of(c * t, t), t) for c in chunks]
        inters = [_bdot(qa_scr[sl, :], c_ext) for sl, (c_ext, _) in zip(sls, carry)]
        new = []
        for d, (chunk, sl, inter, (c_ext, m)) in enumerate(zip(chunks, sls, inters, carry)):
            b = b_scr[d, sl, :]
            mu = mu_scr[d, sl, :]
            tail = tail_scr[d, chunk]
            b_last, gamma = tail[0:1, 0:1], tail[1:2, 0:1]
            m_t = jnp.maximum(b + m, mu)
            num = jnp.exp(b + m - m_t) * inter + jnp.exp(mu - m_t) * intra_scr[d, sl, :]
            den = num[:, dh:dh + 1]
            h_scr[d, sl, :] = num[:, :dh] / jnp.maximum(jnp.abs(den), jnp.exp(-m_t))
            m_new = jnp.maximum(b_last + m, gamma)
            new.append((jnp.exp(b_last + m - m_new) * c_ext + jnp.exp(gamma - m_new) * kv_scr[d, chunk], m_new))
        return tuple(new)

    zero = (jnp.zeros((dh, 2 * dh), F32), jnp.zeros((1, 1), F32))
    lax.fori_loop(0, n_chunks, scan, (zero, zero))

    def finish(j, carry):
        t0 = pl.multiple_of(j * t, t)
        h = h_scr[0, pl.ds(t0, t), :] + h_scr[1, pl.ds(t0, t), :]
        y = h * lax.rsqrt(jnp.mean(h * h, axis=-1, keepdims=True) + RMS_EPS) * ng_ref[...]
        out_ref[0, pl.ds(t0, t), :] = y * _sigmoid(o_ref[0, pl.ds(t0, t), :]) * _silu(z_ref[0, pl.ds(t0, t), :])
        return carry

    lax.fori_loop(0, n_chunks, finish, 0)


def _mlstm(u, gt, conv_w, norm_g, n_ctx):
    bsz, s, _ = u.shape
    nh = MLSTM_N_HEADS
    dh = LANES
    width = nh * dh
    n_chunks = s // MIX_CHUNK
    assert n_chunks % MLSTM_GROUP == 0 and n_ctx % MIX_CHUNK == 0
    col = lambda k: pl.BlockSpec((1, s, dh), lambda b, h, k=k: (b, 0, k * nh + h))
    par = lambda k: pl.BlockSpec((3, dh), lambda b, h, k=k: (0, k * nh + h))
    return pl.pallas_call(
        functools.partial(_mlstm_kernel, n_ctx=n_ctx),
        grid=(bsz, nh),
        in_specs=[col(0), col(1), col(2), col(3), col(4),
                  pl.BlockSpec((1, 1, 8, s), lambda b, h: (b, h, 0, 0)),
                  par(0), par(1),
                  pl.BlockSpec((1, dh), lambda b, h: (0, h))],
        out_specs=pl.BlockSpec((1, s, dh), lambda b, h: (b, 0, h)),
        out_shape=jax.ShapeDtypeStruct((bsz, s, width), F32),
        scratch_shapes=[pltpu.VMEM((s, dh), F32), pltpu.VMEM((s, dh), F32), pltpu.VMEM((2, s, dh), F32),
                        pltpu.VMEM((2, s, 2 * dh), F32), pltpu.VMEM((2, n_chunks, dh, 2 * dh), F32),
                        pltpu.VMEM((2, s, 1), F32), pltpu.VMEM((2, s, 1), F32),
                        pltpu.VMEM((2, n_chunks, 2, dh), F32)],
        compiler_params=_compiler_params(("parallel", "arbitrary")),
        name="mlstm_mixer",
    )(u, u, u, u, u, gt, conv_w, conv_w, norm_g.reshape(1, width))


def _head_stack(x, lane_lo):
    return jnp.concatenate([jnp.where(lane_lo, x, 0.0), jnp.where(lane_lo, 0.0, x)], axis=0)


def _half_rows(x, c, upper):
    start = c if upper else 0
    return jnp.concatenate([x[r + start:r + start + c] for r in range(0, x.shape[0], 2 * c)], axis=0)


def _merge_rows(other, part, c, upper):
    pieces = []
    for k in range(part.shape[0] // c):
        pair = (other[k * c:(k + 1) * c], part[k * c:(k + 1) * c])
        pieces += pair if upper else pair[::-1]
    return jnp.concatenate(pieces, axis=0)


def _spread_rows(part, c, upper):
    return _merge_rows(jnp.zeros_like(part), part, c, upper)


def _rwkv_chunk_operators(problems, consts, eye, lane_lo):
    t, w = problems[0][0].shape
    n2 = 2 * t
    stack = lambda x: _head_stack(x, lane_lo)
    zeros = jnp.zeros((n2, w), F32)
    dirs = [p[6] for p in problems]
    rid = _iota((t, w), 0)
    cums = [p[3] for p in problems]
    sh = 1
    while sh < t:
        cums = [x + (jnp.where(rid < t - sh, pltpu.roll(x, t - sh, 0), 0.0) if d == 1 else
                     jnp.where(rid >= sh, pltpu.roll(x, sh, 0), 0.0)) for x, d in zip(cums, dirs)]
        sh *= 2
    pre = []
    for (r, v, kk, lw, ka, kt, d), cum in zip(problems, cums):
        last = 0 if d == 1 else t - 1
        cum_end = cum[last:last + 1]
        e_inv = jnp.exp(-cum)
        e_end = jnp.exp(cum_end - cum)
        a_s = stack(-kk * jnp.exp(cum - lw))
        r_s = stack(r * jnp.exp(cum))
        pre.append(dict(a_s=a_s, r_s=r_s, vs=stack(v), g=jnp.exp(cum_end),
                        ar=jnp.concatenate([a_s, r_s], axis=0),
                        bk=jnp.concatenate([stack(ka * e_inv), stack(kt * e_inv)], axis=0),
                        bk_end=jnp.concatenate([stack(ka * e_end), stack(kt * e_end)], axis=0)))
    m_alls = [_bdot_nt(q["ar"], q["bk"]) for q in pre]
    m_abs = [jnp.where(consts[d]["strict"], m[:n2, :n2], 0.0) for m, d in zip(m_alls, dirs)]
    m_aks = [jnp.where(consts[d]["strict"], m[:n2, n2:], 0.0) for m, d in zip(m_alls, dirs)]
    m_lows = [jnp.where(consts[d]["incl2"], m[n2:, :], 0.0) for m, d in zip(m_alls, dirs)]
    invs = [eye + jnp.where(consts[d]["merges"][0][1], m, 0.0) for m, d in zip(m_abs, dirs)]
    for level in range(1, len(consts[0]["merges"])):
        c = consts[0]["merges"][level][0]
        if c < SUBLANES:
            inner = [_bdot(jnp.where(consts[d]["merges"][level][1], m, 0.0), x) for m, x, d in zip(m_abs, invs, dirs)]
            invs = [x + _bdot(x, y) for x, y in zip(invs, inner)]
        else:
            ups = [d == 0 for d in dirs]
            c_rows = [jnp.where(consts[d]["merges"][level][1], _half_rows(m, c, up), 0.0) for m, d, up in zip(m_abs, dirs, ups)]
            inner = [_bdot(cr, x) for cr, x in zip(c_rows, invs)]
            x_rows = [_half_rows(x, c, up) for x, up in zip(invs, ups)]
            upd = [xr + _bdot(xr, _spread_rows(y, c, up)) for xr, y, up in zip(x_rows, inner, ups)]
            invs = [_merge_rows(_half_rows(x, c, not up), u, c, up) for x, u, up in zip(invs, upd, ups)]
    mv = [_bdot(m, q["vs"]) for m, q in zip(m_aks, pre)]
    solved = [_bdot(x, jnp.concatenate([q["a_s"], y], axis=1)) for x, q, y in zip(invs, pre, mv)]
    zms = [jnp.concatenate([sv, jnp.concatenate([zeros, q["vs"]], axis=1)], axis=0) for sv, q in zip(solved, pre)]
    ry1s = [jnp.concatenate([q["r_s"], zeros], axis=1) + _bdot(m, z) for q, m, z in zip(pre, m_lows, zms)]
    pqs = [_bdot_tn(z, q["bk_end"]) for z, q in zip(zms, pre)]
    out = []
    for ry1, pq, q in zip(ry1s, pqs, pre):
        folded = ry1[:t] + ry1[t:]
        out.append((folded[:, :w], folded[:, w:], pq[:w], pq[w:], q["g"]))
    return out


def _rwkv_kernel(rr_ref, rk_ref, rv_ref, rz_ref, wd_ref, ad_ref, mu_ref, kk_ref, ka_ref, rkk_ref,
                 lnw_ref, lnb_ref, w0_ref, a0_ref, wup_ref, aup_ref, out_ref,
                 r_scr, v_scr, kk_scr, lw_scr, ka_scr, kt_scr, bonus_scr, y_scr, ry_scr, pt_scr, qt_scr, g_scr,
                 *, n_ctx):
    s = rr_ref.shape[1]
    w = rr_ref.shape[2]
    p_rows = RWKV_PREP_ROWS
    t = RWKV_CHUNK
    n_chunks = s // t
    n_ctx_chunks = n_ctx // t
    head_sum = ((_iota((w, w), 0) // RWKV_HEAD) == (_iota((w, w), 1) // RWKV_HEAD)).astype(BF16)
    inv_head = 1.0 / RWKV_HEAD

    def prep(j, carry):
        t0 = pl.multiple_of(j * p_rows, p_rows)
        mixed = []
        for idx, src in enumerate((rr_ref, rk_ref, rv_ref)):
            cur, down, up = _chunk_with_neighbors(src, t0, p_rows, s, s - n_ctx)
            mixed.append(cur + mu_ref[idx:idx + 1, :] * (0.5 * (down + up) - cur))
        r, kr, v = mixed
        kk = kr * kk_ref[...]
        norm = jnp.sqrt(_dot_sel(kk * kk, head_sum))
        kk = kk / jnp.maximum(norm, 1e-12)
        w_raw = _bdot(jnp.tanh(wd_ref[0, pl.ds(t0, p_rows), :]), wup_ref[0]) + w0_ref[0]
        a = _sigmoid(_bdot(ad_ref[0, pl.ds(t0, p_rows), :], aup_ref[0]) + a0_ref[0])
        lw = -math.exp(-0.5) * _sigmoid(w_raw)
        kt_sum = jnp.zeros_like(kr)
        for d in range(2):
            a_d = a[:, d * w:(d + 1) * w]
            kt_d = kr * (1.0 + (a_d - 1.0) * ka_ref[...])
            kt_sum = kt_sum + kt_d
            lw_scr[d, pl.ds(t0, p_rows), :] = lw[:, d * w:(d + 1) * w]
            ka_scr[d, pl.ds(t0, p_rows), :] = kk * a_d
            kt_scr[d, pl.ds(t0, p_rows), :] = kt_d
        coef = _dot_sel(r * kt_sum * rkk_ref[...], head_sum)
        r_scr[pl.ds(t0, p_rows), :] = r
        v_scr[pl.ds(t0, p_rows), :] = v
        kk_scr[pl.ds(t0, p_rows), :] = kk
        bonus_scr[pl.ds(t0, p_rows), :] = coef * v
        return carry

    lax.fori_loop(0, s // p_rows, prep, 0)

    n2 = 2 * t
    r_i = _iota((n2, n2), 0)
    c_i = _iota((n2, n2), 1)
    same = (r_i // t) == (c_i // t)
    rt = r_i % t
    ct = c_i % t
    eye = (r_i == c_i).astype(F32)
    lane_lo = _iota((t, w), 1) < RWKV_HEAD
    consts = []
    for reverse in (False, True):
        strict = jnp.logical_and(same, (ct > rt) if reverse else (ct < rt))
        incl = jnp.logical_and(same, (ct >= rt) if reverse else (ct <= rt))
        merges = []
        c = 1
        while c < t:
            hi_r = (r_i % (2 * c)) >= c
            hi_c = (c_i % (2 * c)) >= c
            cross = jnp.logical_and(hi_c, jnp.logical_not(hi_r)) if reverse else jnp.logical_and(hi_r, jnp.logical_not(hi_c))
            mask = jnp.logical_and((r_i // (2 * c)) == (c_i // (2 * c)), cross)
            merges.append((c, _half_rows(mask, c, not reverse)) if c >= SUBLANES else (c, mask))
            c *= 2
        consts.append(dict(strict=strict, incl2=jnp.concatenate([incl, incl], axis=1), merges=merges))

    def operators(gi, carry):
        problems, where = [], []
        for k in range(RWKV_GROUP):
            chunk = gi * RWKV_GROUP + k
            sl = pl.ds(pl.multiple_of(chunk * t, t), t)
            r, v, kk = r_scr[sl, :], v_scr[sl, :], kk_scr[sl, :]
            for d in range(2):
                problems.append((r, v, kk, lw_scr[d, sl, :], ka_scr[d, sl, :], kt_scr[d, sl, :], d))
                where.append((d, chunk, sl))
        for (d, chunk, sl), (ry, y1, pt, qt, g) in zip(where, _rwkv_chunk_operators(problems, consts, eye, lane_lo)):
            ry_scr[d, sl, :] = ry.astype(BF16)
            y_scr[d, sl, :] = y1
            pt_scr[d, chunk] = pt.astype(BF16)
            qt_scr[d, chunk] = qt
            g_scr[d, chunk] = g
        return carry

    lax.fori_loop(0, n_chunks // RWKV_GROUP, operators, 0)

    def scan(i, carry):
        states = []
        for d, (chunk, ht) in enumerate(zip(_scan_chunks(i, n_chunks, n_ctx_chunks), carry)):
            sl = pl.ds(pl.multiple_of(chunk * t, t), t)
            y_scr[d, sl, :] = _bdot_nt(ry_scr[d, sl, :], ht) + y_scr[d, sl, :]
            states.append(ht * g_scr[d, chunk] + _bdot(ht, pt_scr[d, chunk]) + qt_scr[d, chunk])
        return tuple(states)

    zero_state = jnp.zeros((w, w), F32)
    lax.fori_loop(0, n_chunks, scan, (zero_state, zero_state))

    def finish(j, carry):
        t0 = pl.multiple_of(j * p_rows, p_rows)
        sl = pl.ds(t0, p_rows)
        y = y_scr[0, sl, :] + y_scr[1, sl, :] + bonus_scr[sl, :]
        mu = _dot_sel(y, head_sum) * inv_head
        yc = y - mu
        var = _dot_sel(yc * yc, head_sum) * inv_head
        yn = yc * lax.rsqrt(var + RWKV_LN_EPSILON) * lnw_ref[...] + lnb_ref[...]
        out_ref[0, sl, :] = yn * _silu(rz_ref[0, sl, :])
        return carry

    lax.fori_loop(0, s // p_rows, finish, 0)


def _rwkv(u, col0, p, n_ctx):
    bsz, s, _ = u.shape
    w = LANES
    width = p["mu"].shape[1]
    n_pairs = width // w
    base = col0 // w
    col = lambda k: pl.BlockSpec((1, s, w), lambda b, h, k=k: (b, 0, base + k * n_pairs + h))
    lora = lambda k: pl.BlockSpec((1, s, w), lambda b, h, k=k: (b, 0, base + 4 * n_pairs + k))
    vec = lambda rows: pl.BlockSpec((rows, w), lambda b, h: (0, h))
    cat = pl.BlockSpec((1, 1, 2 * w), lambda b, h: (h, 0, 0))
    up = pl.BlockSpec((1, w, 2 * w), lambda b, h: (h, 0, 0))
    seq = pltpu.VMEM((s, w), F32)
    seq2 = pltpu.VMEM((2, s, w), F32)
    n_chunks = s // RWKV_CHUNK
    assert n_chunks % RWKV_GROUP == 0 and s % RWKV_PREP_ROWS == 0 and n_ctx % RWKV_PREP_ROWS == 0
    operators = [pltpu.VMEM((2, s, w), BF16), pltpu.VMEM((2, n_chunks, w, w), BF16),
                 pltpu.VMEM((2, n_chunks, w, w), F32), pltpu.VMEM((2, n_chunks, 1, w), F32)]
    return pl.pallas_call(
        functools.partial(_rwkv_kernel, n_ctx=n_ctx),
        grid=(bsz, n_pairs),
        in_specs=[col(0), col(1), col(2), col(3), lora(0), lora(1),
                  vec(3), vec(1), vec(1), vec(1), vec(1), vec(1), cat, cat, up, up],
        out_specs=pl.BlockSpec((1, s, w), lambda b, h: (b, 0, h)),
        out_shape=jax.ShapeDtypeStruct((bsz, s, width), F32),
        scratch_shapes=[seq, seq, seq, seq2, seq2, seq2, seq, seq2] + operators,
        compiler_params=_compiler_params(("parallel", "arbitrary")),
        name="rwkv7_mixer",
    )(u, u, u, u, u, u, p["mu"], p["k_k"], p["k_a"], p["r_k"], p["ln_w"], p["ln_b"],
      p["w0"], p["a0"], p["w_up"], p["a_up"])


def _hgrn_level_masks(t, w):
    rid = _iota((t, w), 0)
    r_i = _iota((t, t), 0)
    c_i = _iota((t, t), 1)
    levels = []
    c = 1
    while c < t:
        same_block = (r_i // (2 * c)) == (c_i // (2 * c))
        up_r = (r_i % (2 * c)) >= c
        up_c = (c_i % (2 * c)) >= c
        pair = [jnp.logical_and(same_block, jnp.logical_and(up_r, jnp.logical_not(up_c))),
                jnp.logical_and(same_block, jnp.logical_and(up_c, jnp.logical_not(up_r)))]
        levels.append((c, (rid % (2 * c)) >= c, pair))
        c *= 2
    return levels


def _hgrn_chunk_operators(problems, lb, tris, levels):
    t, w = problems[0][0].shape
    r_i = _iota((t, t), 0)
    c_i = _iota((t, t), 1)
    zero_row = jnp.zeros((1, w), F32)
    dirs = [p[3] for p in problems]
    lgs, ks = [], []
    for q, v, ff, d in problems:
        e = jnp.exp(-jnp.abs(ff))
        big = 1.0 / (1.0 + e)
        small = e / (1.0 + e)
        pos = ff >= 0.0
        lgs.append(jnp.log(lb + (1.0 - lb) * jnp.where(pos, big, small)) * LOG2_E)
        ks.append((1.0 - lb) * jnp.where(pos, small, big))
    bs = [_sel_dot(tris[d], lg) for lg, d in zip(lgs, dirs)]
    befores = [_shifted(b, zero_row, zero_row)[1 if d == 1 else 0] for b, d in zip(bs, dirs)]
    edges = list(bs)
    accs = [jnp.where(r_i == c_i, jnp.sum(p[0] * k, axis=-1, keepdims=True), 0.0) for p, k in zip(problems, ks)]
    for c, upper, pair in levels:
        qts = [p[0] * jnp.exp2(b - before) for p, b, before in zip(problems, bs, befores)]
        kts = [k * jnp.exp2(edge - b) for k, b, edge in zip(ks, bs, edges)]
        prods = [_bdot_nt(qt, kt) for qt, kt in zip(qts, kts)]
        accs = [a + jnp.where(pair[d], pr, 0.0) for a, pr, d in zip(accs, prods, dirs)]
        for i, d in enumerate(dirs):
            if d == 1:
                befores[i] = jnp.where(upper, befores[i], pltpu.roll(befores[i], t - c, 0))
                edges[i] = jnp.where(upper, pltpu.roll(edges[i], c, 0), edges[i])
            else:
                befores[i] = jnp.where(upper, pltpu.roll(befores[i], c, 0), befores[i])
                edges[i] = jnp.where(upper, edges[i], pltpu.roll(edges[i], t - c, 0))
    o_intras = [_bdot(a, p[1]) for a, p in zip(accs, problems)]
    b_ends = [b[(0 if d == 1 else t - 1):(1 if d == 1 else t)] for b, d in zip(bs, dirs)]
    kvs = [_bdot_tn(p[1], k * jnp.exp2(be - b)) for p, k, b, be in zip(problems, ks, bs, b_ends)]
    return [(p[0] * jnp.exp2(b), oi, kv, jnp.exp2(be)) for p, b, oi, kv, be in zip(problems, bs, o_intras, kvs, b_ends)]


def _hgrn_kernel(q_ref, i_ref, ff_ref, fb_ref, z_ref, lb_ref, ng_ref, out_ref, o_scr, qe_scr, kv_scr, g_scr,
                 *, n_ctx, layer):
    s = q_ref.shape[1]
    dh = q_ref.shape[2]
    t = MIX_CHUNK
    n_chunks = s // t
    n_ctx_chunks = n_ctx // t
    lbs = lb_ref[...]
    ex = jnp.exp(lbs - jnp.max(lbs, axis=0, keepdims=True))
    probs = ex / jnp.sum(ex, axis=0, keepdims=True)
    csum = probs[0:1]
    for l in range(1, layer + 1):
        csum = csum + probs[l:l + 1]
    lb = csum - probs[0:1]
    tri_r = _iota((t, t), 0)
    tri_c = _iota((t, t), 1)
    tris = [(tri_c <= tri_r).astype(BF16), (tri_c >= tri_r).astype(BF16)]
    levels = _hgrn_level_masks(t, dh)

    def operators(gi, carry):
        problems, where = [], []
        for kk in range(HGRN_GROUP):
            chunk = gi * HGRN_GROUP + kk
            sl = pl.ds(pl.multiple_of(chunk * t, t), t)
            q, v = q_ref[0, sl, :], i_ref[0, sl, :]
            for d, f_ref in enumerate((ff_ref, fb_ref)):
                problems.append((q, v, f_ref[0, sl, :], d))
                where.append((d, chunk, sl))
        for (d, chunk, sl), (qe, o_intra, kv, g) in zip(where, _hgrn_chunk_operators(problems, lb, tris, levels)):
            qe_scr[d, sl, :] = qe.astype(BF16)
            o_scr[d, sl, :] = o_intra
            kv_scr[d, chunk] = kv
            g_scr[d, chunk] = g
        return carry

    lax.fori_loop(0, n_chunks // HGRN_GROUP, operators, 0)

    def scan(i, carry):
        chunks = _scan_chunks(i, n_chunks, n_ctx_chunks)
        sls = [pl.ds(pl.multiple_of(c * t, t), t) for c in chunks]
        inters = [_bdot_nt(qe_scr[d, sl, :], st) for d, (sl, st) in enumerate(zip(sls, carry))]
        new = []
        for d, (chunk, sl, inter, st) in enumerate(zip(chunks, sls, inters, carry)):
            o_scr[d, sl, :] = o_scr[d, sl, :] + inter
            new.append(st * g_scr[d, chunk] + kv_scr[d, chunk])
        return tuple(new)

    zero_state = jnp.zeros((dh, dh), F32)
    lax.fori_loop(0, n_chunks, scan, (zero_state, zero_state))

    def finish(j, carry):
        sl = pl.ds(pl.multiple_of(j * t, t), t)
        o = o_scr[0, sl, :] + o_scr[1, sl, :]
        y = o * lax.rsqrt(jnp.mean(o * o, axis=-1, keepdims=True) + RMS_EPS) * ng_ref[...]
        out_ref[0, sl, :] = y * _silu(z_ref[0, sl, :])
        return carry

    lax.fori_loop(0, n_chunks - n_ctx_chunks, finish, 0)


def _hgrn(u, lb_all, norm_g, n_ctx, layer):
    bsz, s, _ = u.shape
    nh = HGRN_N_HEADS
    dh = LANES
    width = nh * dh
    depth = lb_all.shape[0]
    n_chunks = s // MIX_CHUNK
    assert n_chunks % HGRN_GROUP == 0 and n_ctx % MIX_CHUNK == 0
    col = lambda k: pl.BlockSpec((1, s, dh), lambda b, h, k=k: (b, 0, k * nh + h))
    return pl.pallas_call(
        functools.partial(_hgrn_kernel, n_ctx=n_ctx, layer=layer),
        grid=(bsz, nh),
        in_specs=[col(0), col(1), col(2), col(3), col(4),
                  pl.BlockSpec((depth, dh), lambda b, h: (0, h)),
                  pl.BlockSpec((1, dh), lambda b, h: (0, h))],
        out_specs=pl.BlockSpec((1, s - n_ctx, dh), lambda b, h: (b, 0, h)),
        out_shape=jax.ShapeDtypeStruct((bsz, s - n_ctx, width), F32),
        scratch_shapes=[pltpu.VMEM((2, s, dh), F32), pltpu.VMEM((2, s, dh), BF16),
                        pltpu.VMEM((2, n_chunks, dh, dh), F32), pltpu.VMEM((2, n_chunks, 1, dh), F32)],
        compiler_params=_compiler_params(("parallel", "arbitrary")),
        name="hgrn2_mixer",
    )(u, u, u, u, u, lb_all, norm_g.reshape(1, width))


def _hyena_filter_kernel(z_ref, w1_ref, b1_ref, w2_ref, b2_ref, w3f_ref, w3b_ref, dl_ref, hf_ref, hb_ref):
    hp = functools.partial(jnp.dot, precision=lax.Precision.HIGHEST, preferred_element_type=F32)
    n = z_ref.shape[0]
    hid = jnp.sin(hp(z_ref[...], w1_ref[...]) + b1_ref[...])
    hid = jnp.sin(hp(hid, w2_ref[...]) + b2_ref[...])
    pos = _iota((n, 1), 0).astype(F32) * (1.0 / n)
    window = jnp.exp(-pos * dl_ref[...]) + HYENA_SHIFT
    f0 = hp(hid, w3f_ref[...]) * window
    f1 = hp(hid, w3b_ref[...]) * window
    nrm = jnp.sum(jnp.abs(f0), axis=0, keepdims=True) + jnp.sum(jnp.abs(f1), axis=0, keepdims=True)
    hf_ref[...] = f0 / nrm
    hb_ref[...] = f1 / nrm


def _hyena_filters(n, w1, b1, w2, b2, w3, width):
    pos = np.arange(n, dtype=np.float64)
    bands = np.linspace(1e-4, HYENA_N_BANDS - 1, HYENA_N_BANDS)
    ang = (2.0 * math.pi / n) * pos[:, None] * bands
    z = np.concatenate([(pos / n)[:, None], np.cos(ang), np.sin(ang)], axis=-1)
    z = np.pad(z, ((0, 0), (0, LANES - z.shape[1]))).astype(np.float32)
    max_decay = math.log(HYENA_TGT) / HYENA_FAST
    min_decay = math.log(HYENA_TGT) / HYENA_SLOW
    deltas = np.abs(np.linspace(min_decay, max_decay, width)).astype(np.float32)[None]
    feat, hid = w1.shape
    w1p = jnp.pad(w1, ((0, LANES - feat), (0, LANES - hid)))
    w2p = jnp.pad(w2, ((0, LANES - hid), (0, LANES - hid)))
    w3p = jnp.pad(w3, ((0, LANES - hid), (0, 0)))
    b1p = jnp.pad(b1, (0, LANES - hid)).reshape(1, LANES)
    b2p = jnp.pad(b2, (0, LANES - hid)).reshape(1, LANES)
    n_tiles = width // LANES
    full = lambda shape: pl.BlockSpec(shape, lambda j: (0, 0))
    out = pl.BlockSpec((n, LANES), lambda j: (0, j))
    hf, hb = pl.pallas_call(
        _hyena_filter_kernel,
        grid=(n_tiles,),
        in_specs=[full((n, LANES)), full((LANES, LANES)), full((1, LANES)), full((LANES, LANES)), full((1, LANES)),
                  pl.BlockSpec((LANES, LANES), lambda j: (0, j)),
                  pl.BlockSpec((LANES, LANES), lambda j: (0, n_tiles + j)),
                  pl.BlockSpec((1, LANES), lambda j: (0, j))],
        out_specs=[out, out],
        out_shape=[jax.ShapeDtypeStruct((n, width), F32)] * 2,
        compiler_params=_compiler_params(("arbitrary",)),
        name="hyena_filters",
    )(jnp.asarray(z), w1p, b1p, w2p, b2p, w3p, w3p, jnp.asarray(deltas))
    return jnp.concatenate([hf, hb], axis=1)


def _hyena_pre_kernel(yv_ref, y0_ref, y1_ref, yz_ref, swv_ref, sw0_ref, sw1_ref, sbv_ref, sb0_ref, sb1_ref,
                      yb_ref, p_ref, e_ref, pb_ref, *, n_ctx):
    s = yv_ref.shape[1]
    rows = MIX_CHUNK

    def body(j, carry):
        t0 = pl.multiple_of(j * rows, rows)
        conv = []
        for src, sw, sb in ((yv_ref, swv_ref, sbv_ref), (y0_ref, sw0_ref, sb0_ref), (y1_ref, sw1_ref, sb1_ref)):
            cur, down, up = _chunk_with_neighbors(src, t0, rows, s, s - n_ctx)
            conv.append(down * sw[0:1, :] + cur * sw[1:2, :] + up * sw[2:3, :] + sb[...])
        v, x0, x1 = conv
        p = x1 * v
        o0 = pl.multiple_of(j * rows, rows)
        p_ref[0, pl.ds(o0, rows), :] = p.astype(BF16)
        pb_ref[0, pl.ds(o0, rows), :] = p * yb_ref[...]
        e_ref[0, pl.ds(o0, rows), :] = x0 * _silu(yz_ref[0, pl.ds(t0, rows), :])
        return carry

    lax.fori_loop(0, (s - n_ctx) // rows, body, 0)


def _hyena_pre(u, col0, short_w, short_b, y_bias, n_ctx):
    bsz, s, _ = u.shape
    w = y_bias.shape[0]
    tiles = w // LANES
    base = col0 // LANES
    n = s - n_ctx
    col = lambda k: pl.BlockSpec((1, s, LANES), lambda b, j, k=k: (b, 0, base + k * tiles + j))
    par = lambda rows, k: pl.BlockSpec((rows, LANES), lambda b, j, k=k: (0, k * tiles + j))
    out = pl.BlockSpec((1, n, LANES), lambda b, j: (b, 0, j))
    sb = short_b.reshape(1, 3 * w)
    return pl.pallas_call(
        functools.partial(_hyena_pre_kernel, n_ctx=n_ctx),
        grid=(bsz, tiles),
        in_specs=[col(0), col(1), col(2), col(3),
                  par(3, 0), par(3, 1), par(3, 2), par(1, 0), par(1, 1), par(1, 2), par(1, 0)],
        out_specs=[out, out, out],
        out_shape=[jax.ShapeDtypeStruct((bsz, n, w), BF16),
                   jax.ShapeDtypeStruct((bsz, n, w), F32),
                   jax.ShapeDtypeStruct((bsz, n, w), F32)],
        compiler_params=_compiler_params(("parallel", "arbitrary")),
        name="hyena_short_conv",
    )(u, u, u, u, short_w, short_w, short_w, sb, sb, sb, y_bias.reshape(1, w))


def _dft_tables(n):
    big = 2 * n
    half = DFT_TILE // 2
    idx = jnp.arange(n, dtype=jnp.int32)
    split = DFT_SPLIT
    lo = jnp.arange(split, dtype=jnp.int32)
    hi = jnp.arange(n // split, dtype=jnp.int32)
    ang_lo = ((lo[:, None] * idx[None, :]) % big).astype(F32) * (2.0 * math.pi / big)
    ang_hi = ((hi[:, None] * idx[None, :]) % (big // split)).astype(F32) * (2.0 * math.pi * split / big)
    c_lo, s_lo = jnp.cos(ang_lo)[None], jnp.sin(ang_lo)[None]
    c_hi, s_hi = jnp.cos(ang_hi)[:, None], jnp.sin(ang_hi)[:, None]
    cos = (c_hi * c_lo - s_hi * s_lo).reshape(n, n)
    sin = (s_hi * c_lo + c_hi * s_lo).reshape(n, n)
    alt = jnp.where(idx % 2 == 0, 1.0, -1.0).astype(F32)
    first_row = (idx == 0)[:, None]
    im = jnp.where(first_row, alt[None, :], -sin)
    fwd = jnp.stack([cos.reshape(n // half, half, n), im.reshape(n // half, half, n)], axis=1).reshape(big, n)
    weight = jnp.where(first_row, 1.0, 2.0) * (1.0 / big)
    weight = jnp.stack([weight.reshape(n // half, half, 1)] * 2, axis=1).reshape(big, 1)
    return fwd.astype(BF16), (fwd * weight).T.astype(BF16)


def _spectrum_kernel(f_ref, x_ref, o_ref):
    half = DFT_TILE // 2
    w = o_ref.shape[1]
    acc = jnp.dot(f_ref[...], x_ref[...].astype(BF16), preferred_element_type=F32)
    o_ref[:half, :] = acc[:half, :w] + acc[:half, w:]
    k_im = acc[half:, :w] - acc[half:, w:]
    packed = jnp.logical_and(_iota((half, w), 0) == 0, pl.program_id(0) == 0)
    o_ref[half:, :] = jnp.where(packed, acc[half:, :w] + acc[half:, w:], k_im)


def _filter_spectrum(fwd, hk):
    big, n = fwd.shape
    cols = hk.shape[1]
    return pl.pallas_call(
        _spectrum_kernel,
        grid=(big // DFT_TILE,),
        in_specs=[pl.BlockSpec((DFT_TILE, n), lambda i: (i, 0)),
                  pl.BlockSpec((n, cols), lambda i: (0, 0))],
        out_specs=pl.BlockSpec((DFT_TILE, cols // 2), lambda i: (i, 0)),
        out_shape=jax.ShapeDtypeStruct((big, cols // 2), F32),
        compiler_params=_compiler_params(("arbitrary",)),
        name="hyena_filter_spectrum",
    )(fwd, hk)


def _conv_spectrum_kernel(f_ref, p_ref, ks_ref, z_ref):
    i = pl.program_id(1)
    half = DFT_TILE // 2
    w = p_ref.shape[2]
    units = [slice(k * DFT_TILE, (k + 1) * DFT_TILE) for k in range(DFT_UNITS)]
    accs = [jnp.dot(f_ref[u, :], p_ref[0], preferred_element_type=F32) for u in units]
    for k, (u, acc) in enumerate(zip(units, accs)):
        s_re, s_im = acc[:half], acc[half:]
        k_re, k_im = ks_ref[u, :][:half], ks_ref[u, :][half:]
        z_re = s_re * k_re - s_im * k_im
        z_im = s_re * k_im + s_im * k_re
        if k == 0:
            packed = jnp.logical_and(_iota((half, w), 0) == 0, i == 0)
            z_re = jnp.where(packed, s_re * k_re, z_re)
            z_im = jnp.where(packed, s_im * k_im, z_im)
        z_ref[0, u, :] = jnp.concatenate([z_re, z_im], axis=0).astype(BF16)


def _conv_spectrum(fwd, p16, kspec):
    bsz, n, w = p16.shape
    big = fwd.shape[0]
    rows = DFT_TILE * DFT_UNITS
    return pl.pallas_call(
        _conv_spectrum_kernel,
        grid=(bsz, big // rows),
        in_specs=[pl.BlockSpec((rows, n), lambda b, i: (i, 0)),
                  pl.BlockSpec((1, n, w), lambda b, i: (b, 0, 0)),
                  pl.BlockSpec((rows, w), lambda b, i: (i, 0))],
        out_specs=pl.BlockSpec((1, rows, w), lambda b, i: (b, i, 0)),
        out_shape=jax.ShapeDtypeStruct((bsz, big, w), BF16),
        compiler_params=_compiler_params(("parallel", "arbitrary")),
        name="hyena_forward_dft",
    )(fwd, p16, kspec)


def _conv_inverse_kernel(g_ref, z_ref, e_ref, pb_ref, o_ref):
    y = jnp.dot(g_ref[...], z_ref[0], preferred_element_type=F32)
    o_ref[0] = e_ref[0] * (y + pb_ref[0])


def _conv_inverse(inv, z16, e, pb):
    bsz, big, w = z16.shape
    n = inv.shape[0]
    tile = DFT_TILE
    tok = pl.BlockSpec((1, tile, w), lambda b, i: (b, i, 0))
    return pl.pallas_call(
        _conv_inverse_kernel,
        grid=(bsz, n // tile),
        in_specs=[pl.BlockSpec((tile, big), lambda b, i: (i, 0)),
                  pl.BlockSpec((1, big, w), lambda b, i: (b, 0, 0)),
                  tok, tok],
        out_specs=tok,
        out_shape=jax.ShapeDtypeStruct((bsz, n, w), F32),
        compiler_params=_compiler_params(("parallel", "arbitrary")),
        name="hyena_inverse_dft",
    )(inv, z16, e, pb)


def _even_weight_layout(w_in, gate_b):
    d = w_in.shape[0]
    mw = MLSTM_N_HEADS * LANES
    g0 = 5 * mw
    g1 = g0 + 4 * MLSTM_N_HEADS
    main = jnp.concatenate([w_in[:, :g0], w_in[:, g1:]], axis=1).astype(BF16)
    wg = w_in[:, g0:g1].reshape(d, 2, 2, MLSTM_N_HEADS)
    wg = jnp.transpose(wg, (3, 1, 2, 0)).reshape(MLSTM_N_HEADS, 4, d)
    wg = jnp.concatenate([wg, jnp.zeros_like(wg)], axis=1).reshape(MLSTM_N_HEADS * 8, d).astype(BF16)
    gb = jnp.transpose(gate_b.reshape(2, 2, MLSTM_N_HEADS), (2, 0, 1)).reshape(MLSTM_N_HEADS, 4)
    gb = jnp.concatenate([gb, jnp.zeros_like(gb)], axis=1).reshape(MLSTM_N_HEADS * 8, 1)
    return main, wg, jnp.broadcast_to(gb, (MLSTM_N_HEADS * 8, LANES))


def _rwkv_params(mu, w0, w_up, a0, a_up, k_k, k_a, r_k, ln_w, ln_b):
    width = mu.shape[1]
    n_pairs = width // LANES
    row = lambda x: x.reshape(1, width)

    def cat_dirs(x):
        return jnp.transpose(x.reshape(2, n_pairs, LANES), (1, 0, 2)).reshape(n_pairs, 1, 2 * LANES)

    def block_up(x):
        lora = x.shape[1]
        xp = jnp.transpose(x.reshape(2, lora, n_pairs, LANES), (2, 0, 1, 3))
        z = jnp.zeros_like(xp[:, 0])
        top = jnp.concatenate([xp[:, 0], z], axis=2)
        bot = jnp.concatenate([z, xp[:, 1]], axis=2)
        return jnp.concatenate([top, bot], axis=1).astype(BF16)

    return {"mu": mu, "k_k": row(k_k), "k_a": row(k_a), "r_k": row(r_k), "ln_w": row(ln_w), "ln_b": row(ln_b),
            "w0": cat_dirs(w0), "a0": cat_dirs(a0), "w_up": block_up(w_up), "a_up": block_up(a_up)}


def _raster_to_column(h):
    b, n, d = h.shape
    rows = n // GRID_WIDTH
    return h.reshape(b, rows, GRID_WIDTH, d).transpose(0, 2, 1, 3).reshape(b, n, d)


def _column_to_raster(h):
    b, n, d = h.shape
    rows = n // GRID_WIDTH
    return h.reshape(b, GRID_WIDTH, rows, d).transpose(0, 2, 1, 3).reshape(b, n, d)


def kernel(x, c, ctx, c_ctx, l0_norm_g, l0_mod_w, l0_mod_b, l0_w_in, l0_w_out, l0_mlstm_conv_w, l0_mlstm_gate_b, l0_mlstm_norm_g, l0_rwkv_mu, l0_rwkv_w0, l0_rwkv_w_up, l0_rwkv_a0, l0_rwkv_a_up, l0_rwkv_k_k, l0_rwkv_k_a, l0_rwkv_r_k, l0_rwkv_ln_w, l0_rwkv_ln_b, hgrn_lower_bounds, l1_norm_g, l1_mod_w, l1_mod_b, l1_w_in, l1_w_out, l1_hgrn_norm_g, l1_hyena_short_w, l1_hyena_short_b, l1_hyena_w1, l1_hyena_b1, l1_hyena_w2, l1_hyena_b2, l1_hyena_w3, l1_hyena_bias, final_norm_g):
    bsz, n_lat, d = x.shape
    n_ctx = ctx.shape[1]

    pad = (-(bsz + 1)) % 8
    cc = jnp.concatenate([c, c_ctx[None], jnp.zeros((pad, d), F32)], axis=0)
    mod0, mod1 = _modulation(cc, l0_mod_w, l0_mod_b, l1_mod_w, l1_mod_b)
    mod0 = mod0[:bsz + 1].reshape(bsz + 1, 3, d)
    mod1 = mod1[:bsz + 1].reshape(bsz + 1, 3, d)

    w_main, w_gate, b_gate = _even_weight_layout(l0_w_in, l0_mlstm_gate_b)
    n0 = w_main.shape[1]
    u0, gt0 = _proj_in(x, ctx, l0_norm_g, mod0, w_main, n0 // 2, w_gate, b_gate)
    gt0 = gt0.reshape(bsz, MLSTM_N_HEADS, 8, n_ctx + n_lat)
    y_m = _mlstm(u0, gt0, l0_mlstm_conv_w, l0_mlstm_norm_g, n_ctx)
    rp = _rwkv_params(l0_rwkv_mu, l0_rwkv_w0, l0_rwkv_w_up, l0_rwkv_a0, l0_rwkv_a_up, l0_rwkv_k_k,
                      l0_rwkv_k_a, l0_rwkv_r_k, l0_rwkv_ln_w, l0_rwkv_ln_b)
    y_r = _rwkv(u0, 5 * MLSTM_N_HEADS * LANES, rp, n_ctx)
    x1, ctx1 = _proj_out(y_m, y_r, x, ctx, mod0, l0_w_out.astype(BF16))

    x1c = _raster_to_column(x1)
    w1 = l1_w_in.astype(BF16)
    (u1,) = _proj_in(x1c, ctx1, l1_norm_g, mod1, w1, w1.shape[1] // 2)
    y_g = _hgrn(u1, hgrn_lower_bounds, l1_hgrn_norm_g, n_ctx, layer=1)
    hw = l1_hyena_bias.shape[0]
    hk = _hyena_filters(n_lat, l1_hyena_w1, l1_hyena_b1, l1_hyena_w2, l1_hyena_b2, l1_hyena_w3, hw)
    fwd, inv = _dft_tables(n_lat)
    kspec = _filter_spectrum(fwd, hk)
    p16, e, pb = _hyena_pre(u1, 5 * HGRN_N_HEADS * LANES, l1_hyena_short_w, l1_hyena_short_b, l1_hyena_bias, n_ctx)
    z16 = _conv_spectrum(fwd, p16, kspec)
    y_y = _conv_inverse(inv, z16, e, pb)
    out_c = _proj_out_final(y_g, y_y, x1c, mod1, l1_w_out.astype(BF16), final_norm_g)
    return _column_to_raster(out_c)
```

```python
import functools
import math

import jax
import jax.numpy as jnp
import numpy as np
from jax import lax
from jax.experimental import pallas as pl
from jax.experimental.pallas import tpu as pltpu

F32 = jnp.float32
BF16 = jnp.bfloat16

GRID_WIDTH = 64
RMS_EPS = 1e-6
MLSTM_N_HEADS = 4
RWKV_HEAD = 64
RWKV_LN_EPSILON = 64e-5
HGRN_N_HEADS = 4
HYENA_N_BANDS = 16
HYENA_FAST = 0.3
HYENA_SLOW = 1.5
HYENA_TGT = 1e-2
HYENA_SHIFT = 0.05
LOG2_E = 1.0 / math.log(2.0)

LANES = 128
SUBLANES = 8
MXU_DIM = 256
VMEM_LIMIT = 52 * 1024 * 1024

MIX_CHUNK = 128
RWKV_CHUNK = 64
RWKV_GROUP = 9
MIX_PREP_ROWS = 256
MIX_FINISH_ROWS = 256
RWKV_PREP_ROWS = 256
RWKV_FINISH_ROWS = 768
MLSTM_GROUP = 6
HGRN_GROUP = 3
PROJ_ROWS = 768
FINAL_ROWS = 512
DFT_TILE = 512
DFT_UNITS = 2
DFT_SPLIT = 64


def _bdot(a, b):
    return jnp.dot(a.astype(BF16), b.astype(BF16), preferred_element_type=F32)


def _bdot_nt(a, b):
    return lax.dot_general(a.astype(BF16), b.astype(BF16), (((1,), (1,)), ((), ())),
                           preferred_element_type=F32)


def _bdot_tn(a, b):
    return lax.dot_general(a.astype(BF16), b.astype(BF16), (((0,), (0,)), ((), ())),
                           preferred_element_type=F32)


def _split3(x):
    hi = x.astype(BF16)
    r1 = x - hi.astype(F32)
    mid = r1.astype(BF16)
    lo = (r1 - mid.astype(F32)).astype(BF16)
    return hi, mid, lo


def _sel_dot(sel, x):
    hi, mid, lo = _split3(x)
    d = functools.partial(jnp.dot, preferred_element_type=F32)
    return d(sel, hi) + d(sel, mid) + d(sel, lo)


def _dot_sel(x, sel):
    hi, mid, lo = _split3(x)
    d = functools.partial(jnp.dot, preferred_element_type=F32)
    return d(hi, sel) + d(mid, sel) + d(lo, sel)


def _sigmoid(x):
    return 1.0 / (1.0 + jnp.exp(-x))


def _silu(x):
    return x * _sigmoid(x)


def _iota(shape, dim):
    return lax.broadcasted_iota(jnp.int32, shape, dim)


def _neighbor_rows(ref, t0, rows, n_total, split):
    has_prev = jnp.logical_and(t0 != 0, t0 != split)
    has_next = jnp.logical_and(t0 + rows != split, t0 + rows != n_total)
    prev = ref[0, pl.ds(jnp.maximum(t0 - 1, 0), 1), :]
    nxt = ref[0, pl.ds(jnp.minimum(t0 + rows, n_total - 1), 1), :]
    return jnp.where(has_prev, prev, 0.0), jnp.where(has_next, nxt, 0.0)


def _shifted(cur, prev_row, next_row):
    rows = cur.shape[0]
    rid = _iota(cur.shape, 0)
    down = jnp.where(rid == 0, prev_row, pltpu.roll(cur, 1, 0))
    up = jnp.where(rid == rows - 1, next_row, pltpu.roll(cur, rows - 1, 0))
    return down, up


def _chunk_with_neighbors(ref, t0, rows, n_total, split):
    cur = ref[0, pl.ds(t0, rows), :]
    prev_row, next_row = _neighbor_rows(ref, t0, rows, n_total, split)
    down, up = _shifted(cur, prev_row, next_row)
    return cur, down, up


def _scan_chunks(i, n_chunks, n_ctx_chunks):
    fwd = jnp.where(i < n_ctx_chunks, n_chunks - n_ctx_chunks + i, i - n_ctx_chunks)
    return fwd, n_chunks - 1 - i


def _compiler_params(semantics):
    return pltpu.CompilerParams(dimension_semantics=semantics, vmem_limit_bytes=VMEM_LIMIT)


def _mod_kernel(c_ref, w0_ref, b0_ref, w1_ref, b1_ref, o0_ref, o1_ref):
    s = _silu(c_ref[...])
    o0_ref[...] = _bdot(s, w0_ref[...]) + b0_ref[...]
    o1_ref[...] = _bdot(s, w1_ref[...]) + b1_ref[...]


def _modulation(cc, w0, b0, w1, b1):
    rows, d = cc.shape
    n = w0.shape[1]
    tile = d
    grid = (n // tile,)
    wspec = pl.BlockSpec((d, tile), lambda j: (0, j))
    bspec = pl.BlockSpec((1, tile), lambda j: (0, j))
    ospec = pl.BlockSpec((rows, tile), lambda j: (0, j))
    return pl.pallas_call(
        _mod_kernel,
        grid=grid,
        in_specs=[pl.BlockSpec((rows, d), lambda j: (0, 0)), wspec, bspec, wspec, bspec],
        out_specs=[ospec, ospec],
        out_shape=[jax.ShapeDtypeStruct((rows, n), F32)] * 2,
        compiler_params=_compiler_params(("arbitrary",)),
        name="adaln_modulation",
    )(cc, w0, b0.reshape(1, n), w1, b1.reshape(1, n))


def _token_tile(x_ref, c_ref, i, rows):
    n_lat_tail = rows - c_ref.shape[1]
    is_ctx = jnp.logical_and(i == pl.num_programs(1) - 1, _iota((rows, 1), 0) >= n_lat_tail)
    ctx_rows = jnp.concatenate([jnp.zeros((n_lat_tail, c_ref.shape[2]), F32), c_ref[0]], axis=0)
    return jnp.where(is_ctx, ctx_rows, x_ref[0]), is_ctx


def _proj_in_kernel(*refs, rows, with_gates):
    if with_gates:
        x_ref, c_ref, g_ref, ml_ref, mc_ref, w_ref, wg_ref, gb_ref, u_ref, gt_ref, h_scr = refs
    else:
        x_ref, c_ref, g_ref, ml_ref, mc_ref, w_ref, u_ref, h_scr = refs
    i = pl.program_id(1)
    n = pl.program_id(2)

    @pl.when(n == 0)
    def _():
        x, is_ctx = _token_tile(x_ref, c_ref, i, rows)
        y = x * lax.rsqrt(jnp.mean(x * x, axis=-1, keepdims=True) + RMS_EPS) * g_ref[...]
        ml = ml_ref[0]
        mc = mc_ref[0]
        shift = jnp.where(is_ctx, mc[0:1], ml[0:1])
        scale = jnp.where(is_ctx, mc[1:2], ml[1:2])
        h = (y * (1.0 + scale) + shift).astype(BF16)
        h_scr[...] = h
        if with_gates:
            gt_ref[0] = _bdot_nt(wg_ref[...], h) + gb_ref[:, 0:1]

    u_ref[0] = jnp.dot(h_scr[...], w_ref[...], preferred_element_type=F32)


def _proj_in(x, ctx, norm_g, mod3, w16, n_tile, gate_w=None, gate_b=None):
    bsz, n_lat, d = x.shape
    n_ctx = ctx.shape[1]
    s = n_lat + n_ctx
    n = w16.shape[1]
    rows = PROJ_ROWS
    assert s % rows == 0 and (n_lat % rows) + n_ctx == rows
    grid = (bsz, s // rows, n // n_tile)
    ctx_row = mod3.shape[0] - 1
    with_gates = gate_w is not None
    in_specs = [
        pl.BlockSpec((1, rows, d), lambda b, i, j: (b, i, 0)),
        pl.BlockSpec((1, n_ctx, d), lambda b, i, j: (b, 0, 0)),
        pl.BlockSpec((1, d), lambda b, i, j: (0, 0)),
        pl.BlockSpec((1, 3, d), lambda b, i, j: (b, 0, 0)),
        pl.BlockSpec((1, 3, d), lambda b, i, j: (ctx_row, 0, 0)),
        pl.BlockSpec((d, n_tile), lambda b, i, j: (0, j)),
    ]
    args = [x, ctx, norm_g.reshape(1, d), mod3, mod3, w16]
    out_specs = [pl.BlockSpec((1, rows, n_tile), lambda b, i, j: (b, i, j))]
    out_shape = [jax.ShapeDtypeStruct((bsz, s, n), F32)]
    if with_gates:
        ng = gate_w.shape[0]
        in_specs += [pl.BlockSpec((ng, d), lambda b, i, j: (0, 0)),
                     pl.BlockSpec((ng, LANES), lambda b, i, j: (0, 0))]
        args += [gate_w, gate_b]
        out_specs.append(pl.BlockSpec((1, ng, rows), lambda b, i, j: (b, 0, i)))
        out_shape.append(jax.ShapeDtypeStruct((bsz, ng, s), F32))
    return pl.pallas_call(
        functools.partial(_proj_in_kernel, rows=rows, with_gates=with_gates),
        grid=grid,
        in_specs=in_specs,
        out_specs=out_specs,
        out_shape=out_shape,
        scratch_shapes=[pltpu.VMEM((rows, d), BF16)],
        compiler_params=_compiler_params(("parallel", "arbitrary", "arbitrary")),
        name="norm_mod_proj_in",
    )(*args)


def _proj_out_kernel(ya_ref, yb_ref, x_ref, c_ref, ml_ref, mc_ref, w_ref, ox_ref, oc_ref, *, rows):
    i = pl.program_id(1)
    half = ya_ref.shape[2]
    y = _bdot(ya_ref[0], w_ref[0:half, :]) + _bdot(yb_ref[0], w_ref[half:, :])
    x, is_ctx = _token_tile(x_ref, c_ref, i, rows)
    x = x + jnp.where(is_ctx, mc_ref[0][2:3], ml_ref[0][2:3]) * y
    ox_ref[0] = x

    @pl.when(i == pl.num_programs(1) - 1)
    def _():
        oc_ref[0] = x[rows - c_ref.shape[1]:]


def _proj_out(ya, yb, x, ctx, mod3, w16):
    bsz, n_lat, d = x.shape
    n_ctx = ctx.shape[1]
    s = n_lat + n_ctx
    half = ya.shape[2]
    rows = PROJ_ROWS
    assert s % rows == 0 and (n_lat % rows) + n_ctx == rows
    ctx_row = mod3.shape[0] - 1
    tok = lambda w: pl.BlockSpec((1, rows, w), lambda b, i: (b, i, 0))
    seg = pl.BlockSpec((1, n_ctx, d), lambda b, i: (b, 0, 0))
    return pl.pallas_call(
        functools.partial(_proj_out_kernel, rows=rows),
        grid=(bsz, s // rows),
        in_specs=[tok(half), tok(half), tok(d), seg,
                  pl.BlockSpec((1, 3, d), lambda b, i: (b, 0, 0)),
                  pl.BlockSpec((1, 3, d), lambda b, i: (ctx_row, 0, 0)),
                  pl.BlockSpec((2 * half, d), lambda b, i: (0, 0))],
        out_specs=[tok(d), seg],
        out_shape=[jax.ShapeDtypeStruct((bsz, n_lat, d), F32), jax.ShapeDtypeStruct((bsz, n_ctx, d), F32)],
        compiler_params=_compiler_params(("parallel", "arbitrary")),
        name="proj_out_residual",
    )(ya, yb, x, ctx, mod3, mod3, w16)


def _proj_out_final_kernel(ya_ref, yb_ref, x_ref, ml_ref, w_ref, fg_ref, o_ref):
    half = ya_ref.shape[2]
    y = _bdot(ya_ref[0], w_ref[0:half, :]) + _bdot(yb_ref[0], w_ref[half:, :])
    x = x_ref[0] + ml_ref[0][2:3] * y
    o_ref[0] = x * lax.rsqrt(jnp.mean(x * x, axis=-1, keepdims=True) + RMS_EPS) * fg_ref[...]


def _proj_out_final(ya, yb, x, mod3, w16, final_g):
    bsz, n_lat, d = x.shape
    half = ya.shape[2]
    rows = FINAL_ROWS
    assert n_lat % rows == 0
    tok = lambda w: pl.BlockSpec((1, rows, w), lambda b, i: (b, i, 0))
    return pl.pallas_call(
        _proj_out_final_kernel,
        grid=(bsz, n_lat // rows),
        in_specs=[tok(half), tok(half), tok(d),
                  pl.BlockSpec((1, 3, d), lambda b, i: (b, 0, 0)),
                  pl.BlockSpec((2 * half, d), lambda b, i: (0, 0)),
                  pl.BlockSpec((1, d), lambda b, i: (0, 0))],
        out_specs=tok(d),
        out_shape=jax.ShapeDtypeStruct((bsz, n_lat, d), F32),
        compiler_params=_compiler_params(("parallel", "arbitrary")),
        name="proj_out_final_norm",
    )(ya, yb, x, mod3, w16, final_g.reshape(1, d))


def _mlstm_chunk_operators(chunks, causal):
    t = chunks[0][0].shape[0]
    lane = _iota((8, t), 1)
    row_id = _iota((8, t), 0)
    log_fs = [jnp.minimum(c[3], 0.0) - jnp.log1p(jnp.exp(-jnp.abs(c[3]))) for c in chunks]
    cum_f, cum_b = list(log_fs), list(log_fs)
    sh = 1
    while sh < t:
        cum_f = [x + jnp.where(lane >= sh, pltpu.roll(x, sh, 1), 0.0) for x in cum_f]
        cum_b = [x + jnp.where(lane < t - sh, pltpu.roll(x, t - sh, 1), 0.0) for x in cum_b]
        sh *= 2
    pad = jnp.zeros((t - 8, t), F32)
    tiles = [jnp.concatenate([jnp.where(row_id % 2 == 0, c[3], jnp.where(row_id == 1, f, b)), pad], axis=0)
             for c, f, b in zip(chunks, cum_f, cum_b)]
    cols = [x.T for x in tiles]
    problems = []
    for c, f, b, col in zip(chunks, cum_f, cum_b, cols):
        for d in range(2):
            b_row = (f, b)[d][2 * d + 1:2 * d + 2]
            problems.append(dict(q=c[0], k=c[1], v_ext=c[2], d=d, ig_row=c[3][2 * d:2 * d + 1], b_row=b_row,
                                 ig_col=col[:, 2 * d:2 * d + 1], b_col=col[:, 2 * d + 1:2 * d + 2]))
    logws = [jnp.where(causal[p["d"]], p["b_col"] + (p["ig_row"] - p["b_row"]), -jnp.inf) for p in problems]
    mus = [jnp.max(x, axis=-1, keepdims=True) for x in logws]
    ws = [jnp.exp(x - mu) for x, mu in zip(logws, mus)]
    lasts = [0 if p["d"] == 1 else t - 1 for p in problems]
    b_lasts = [p["b_col"][i:i + 1] for p, i in zip(problems, lasts)]
    gammas = [mu[i:i + 1] for mu, i in zip(mus, lasts)]
    gks = [jnp.exp(bl - p["b_col"] + p["ig_col"] - gm) * p["k"] for p, bl, gm in zip(problems, b_lasts, gammas)]
    qks = [_bdot_nt(p["q"], p["k"]) * w for p, w in zip(problems, ws)]
    intras = [_bdot(qk, p["v_ext"]) for qk, p in zip(qks, problems)]
    kvs = [_bdot_tn(gk, p["v_ext"]) for gk, p in zip(gks, problems)]
    dh = chunks[0][0].shape[1]
    return [(intra, kv, jnp.broadcast_to(mu - p["b_col"], (t, dh)), jnp.broadcast_to(mu, (t, dh)), bl, gm)
            for intra, kv, p, mu, bl, gm in zip(intras, kvs, problems, mus, b_lasts, gammas)]


def _mlstm_kernel(q_ref, k_ref, v_ref, o_ref, z_ref, gt_ref, cwq_ref, cwk_ref, ng_ref, out_ref,
                  qa_scr, ka_scr, h_scr, intra_scr, kv_scr, delta_scr, mu_scr, tail_scr, *, n_ctx):
    s = q_ref.shape[1]
    dh = q_ref.shape[2]
    t = MIX_CHUNK
    n_chunks = s // t
    n_ctx_chunks = n_ctx // t
    k_scale = dh ** -0.5

    p_rows = MIX_PREP_ROWS

    def prep(j, carry):
        t0 = pl.multiple_of(j * p_rows, p_rows)
        for src, cw, dst, scale in ((q_ref, cwq_ref, qa_scr, 1.0), (k_ref, cwk_ref, ka_scr, k_scale)):
            cur, down, up = _chunk_with_neighbors(src, t0, p_rows, s, s - n_ctx)
            conv = down * cw[0:1, :] + cur * cw[1:2, :] + up * cw[2:3, :]
            dst[pl.ds(t0, p_rows), :] = _silu(conv) * scale
        return carry

    lax.fori_loop(0, s // p_rows, prep, 0)

    ones_col = jnp.ones((t, dh), F32)
    causal = [_iota((t, t), 1) <= _iota((t, t), 0), _iota((t, t), 1) >= _iota((t, t), 0)]

    def operators(gi, carry):
        chunks, where = [], []
        for kk in range(MLSTM_GROUP):
            chunk = gi * MLSTM_GROUP + kk
            sl = pl.ds(pl.multiple_of(chunk * t, t), t)
            v_ext = jnp.concatenate([v_ref[0, sl, :], ones_col], axis=1)
            chunks.append((qa_scr[sl, :], ka_scr[sl, :], v_ext, gt_ref[0, 0, :, sl]))
            where += [(0, chunk, sl), (1, chunk, sl)]
        for (d, chunk, sl), (intra, kv, delta, mu, b_last, gamma) in zip(where, _mlstm_chunk_operators(chunks, causal)):
            intra_scr[d, sl, :] = intra
            kv_scr[d, chunk] = kv
            delta_scr[d, sl, :] = delta
            mu_scr[d, sl, :] = mu
            tail_scr[d, chunk] = jnp.concatenate([jnp.broadcast_to(b_last, (1, dh)), jnp.broadcast_to(gamma, (1, dh))],
                                                 axis=0)
        return carry

    lax.fori_loop(0, n_chunks // MLSTM_GROUP, operators, 0)

    def scan(i, carry):
        chunks = _scan_chunks(i, n_chunks, n_ctx_chunks)
        sls = [pl.ds(pl.multiple_of(c * t, t), t) for c in chunks]
        inters = [_bdot(qa_scr[sl, :], c_ext) for sl, (c_ext, _) in zip(sls, carry)]
        new = []
        for d, (chunk, sl, inter, (c_ext, m)) in enumerate(zip(chunks, sls, inters, carry)):
            tail = tail_scr[d, chunk]
            b_last, gamma = tail[0:1, 0:1], tail[1:2, 0:1]
            z = delta_scr[d, sl, :] - m
            s_inter = jnp.exp(-jnp.maximum(z, 0.0))
            s_intra = jnp.exp(jnp.minimum(z, 0.0))
            floor = jnp.exp(jnp.minimum(z, 0.0) - mu_scr[d, sl, :])
            intra = intra_scr[d, sl, :]
            num = s_inter * inter[:, :dh] + s_intra * intra[:, :dh]
            den = s_inter * inter[:, dh:] + s_intra * intra[:, dh:]
            h_scr[d, sl, :] = num / jnp.maximum(jnp.abs(den), floor)
            m_new = jnp.maximum(b_last + m, gamma)
            new.append((jnp.exp(b_last + m - m_new) * c_ext + jnp.exp(gamma - m_new) * kv_scr[d, chunk], m_new))
        return tuple(new)

    zero = (jnp.zeros((dh, 2 * dh), F32), jnp.zeros((1, 1), F32))
    lax.fori_loop(0, n_chunks, scan, (zero, zero))

    f_rows = MIX_FINISH_ROWS

    def finish(j, carry):
        sl = pl.ds(pl.multiple_of(j * f_rows, f_rows), f_rows)
        h = h_scr[0, sl, :] + h_scr[1, sl, :]
        y = h * lax.rsqrt(jnp.mean(h * h, axis=-1, keepdims=True) + RMS_EPS) * ng_ref[...]
        out_ref[0, sl, :] = y * _sigmoid(o_ref[0, sl, :]) * _silu(z_ref[0, sl, :])
        return carry

    lax.fori_loop(0, s // f_rows, finish, 0)


def _mlstm(u, gt, conv_w, norm_g, n_ctx):
    bsz, s, _ = u.shape
    nh = MLSTM_N_HEADS
    dh = LANES
    width = nh * dh
    n_chunks = s // MIX_CHUNK
    assert n_chunks % MLSTM_GROUP == 0 and n_ctx % MIX_CHUNK == 0
    col = lambda k: pl.BlockSpec((1, s, dh), lambda b, h, k=k: (b, 0, k * nh + h))
    par = lambda k: pl.BlockSpec((3, dh), lambda b, h, k=k: (0, k * nh + h))
    return pl.pallas_call(
        functools.partial(_mlstm_kernel, n_ctx=n_ctx),
        grid=(bsz, nh),
        in_specs=[col(0), col(1), col(2), col(3), col(4),
                  pl.BlockSpec((1, 1, 8, s), lambda b, h: (b, h, 0, 0)),
                  par(0), par(1),
                  pl.BlockSpec((1, dh), lambda b, h: (0, h))],
        out_specs=pl.BlockSpec((1, s, dh), lambda b, h: (b, 0, h)),
        out_shape=jax.ShapeDtypeStruct((bsz, s, width), F32),
        scratch_shapes=[pltpu.VMEM((s, dh), F32), pltpu.VMEM((s, dh), F32), pltpu.VMEM((2, s, dh), F32),
                        pltpu.VMEM((2, s, 2 * dh), F32), pltpu.VMEM((2, n_chunks, dh, 2 * dh), F32),
                        pltpu.VMEM((2, s, dh), F32), pltpu.VMEM((2, s, dh), F32),
                        pltpu.VMEM((2, n_chunks, 2, dh), F32)],
        compiler_params=_compiler_params(("parallel", "arbitrary")),
        name="mlstm_mixer",
    )(u, u, u, u, u, gt, conv_w, conv_w, norm_g.reshape(1, width))


def _head_stack(x, lane_lo):
    return jnp.concatenate([jnp.where(lane_lo, x, 0.0), jnp.where(lane_lo, 0.0, x)], axis=0)


def _half_rows(x, c, upper):
    start = c if upper else 0
    return jnp.concatenate([x[r + start:r + start + c] for r in range(0, x.shape[0], 2 * c)], axis=0)


def _merge_rows(other, part, c, upper):
    pieces = []
    for k in range(part.shape[0] // c):
        pair = (other[k * c:(k + 1) * c], part[k * c:(k + 1) * c])
        pieces += pair if upper else pair[::-1]
    return jnp.concatenate(pieces, axis=0)


def _spread_rows(part, c, upper):
    return _merge_rows(jnp.zeros_like(part), part, c, upper)


def _rwkv_chunk_operators(problems, consts, eye, lane_lo):
    t, w = problems[0][0].shape
    n2 = 2 * t
    stack = lambda x: _head_stack(x, lane_lo)
    zeros = jnp.zeros((n2, w), F32)
    dirs = [p[6] for p in problems]
    rid = _iota((t, w), 0)
    cums = [p[3] for p in problems]
    sh = 1
    while sh < t:
        cums = [x + (jnp.where(rid < t - sh, pltpu.roll(x, t - sh, 0), 0.0) if d == 1 else
                     jnp.where(rid >= sh, pltpu.roll(x, sh, 0), 0.0)) for x, d in zip(cums, dirs)]
        sh *= 2
    pre = []
    for (r, v, kk, lw, ka, kt, d), cum in zip(problems, cums):
        last = 0 if d == 1 else t - 1
        cum_end = cum[last:last + 1]
        e_inv = jnp.exp(-cum)
        e_end = jnp.exp(cum_end - cum)
        a_s = stack(-kk * jnp.exp(cum - lw))
        r_s = stack(r * jnp.exp(cum))
        pre.append(dict(a_s=a_s, r_s=r_s, vs=stack(v), g=jnp.exp(cum_end),
                        ar=jnp.concatenate([a_s, r_s], axis=0),
                        bk=jnp.concatenate([stack(ka * e_inv), stack(kt * e_inv)], axis=0),
                        bk_end=jnp.concatenate([stack(ka * e_end), stack(kt * e_end)], axis=0)))
    m_alls = [_bdot_nt(q["ar"], q["bk"]) for q in pre]
    m_abs = [jnp.where(consts[d]["strict"], m[:n2, :n2], 0.0) for m, d in zip(m_alls, dirs)]
    m_aks = [jnp.where(consts[d]["strict"], m[:n2, n2:], 0.0) for m, d in zip(m_alls, dirs)]
    m_lows = [jnp.where(consts[d]["incl2"], m[n2:, :], 0.0) for m, d in zip(m_alls, dirs)]
    invs = [eye + jnp.where(consts[d]["merges"][0][1], m, 0.0) for m, d in zip(m_abs, dirs)]
    for level in range(1, len(consts[0]["merges"])):
        c = consts[0]["merges"][level][0]
        if c < SUBLANES:
            inner = [_bdot(jnp.where(consts[d]["merges"][level][1], m, 0.0), x) for m, x, d in zip(m_abs, invs, dirs)]
            invs = [x + _bdot(x, y) for x, y in zip(invs, inner)]
        else:
            ups = [d == 0 for d in dirs]
            c_rows = [jnp.where(consts[d]["merges"][level][1], _half_rows(m, c, up), 0.0) for m, d, up in zip(m_abs, dirs, ups)]
            inner = [_bdot(cr, x) for cr, x in zip(c_rows, invs)]
            x_rows = [_half_rows(x, c, up) for x, up in zip(invs, ups)]
            upd = [xr + _bdot(xr, _spread_rows(y, c, up)) for xr, y, up in zip(x_rows, inner, ups)]
            invs = [_merge_rows(_half_rows(x, c, not up), u, c, up) for x, u, up in zip(invs, upd, ups)]
    mv = [_bdot(m, q["vs"]) for m, q in zip(m_aks, pre)]
    solved = [_bdot(x, jnp.concatenate([q["a_s"], y], axis=1)) for x, q, y in zip(invs, pre, mv)]
    zms = [jnp.concatenate([sv, jnp.concatenate([zeros, q["vs"]], axis=1)], axis=0) for sv, q in zip(solved, pre)]
    ry1s = [jnp.concatenate([q["r_s"], zeros], axis=1) + _bdot(m, z) for q, m, z in zip(pre, m_lows, zms)]
    pqs = [_bdot_tn(z, q["bk_end"]) for z, q in zip(zms, pre)]
    out = []
    for ry1, pq, q in zip(ry1s, pqs, pre):
        folded = ry1[:t] + ry1[t:]
        out.append((folded[:, :w], folded[:, w:], pq[:w], pq[w:], q["g"]))
    return out


def _rwkv_kernel(rr_ref, rk_ref, rv_ref, rz_ref, wd_ref, ad_ref, mu_ref, kk_ref, ka_ref, rkk_ref,
                 lnw_ref, lnb_ref, w0_ref, a0_ref, wup_ref, aup_ref, out_ref,
                 r_scr, v_scr, kk_scr, lw_scr, ka_scr, kt_scr, bonus_scr, y_scr, ry_scr, pt_scr, qt_scr, g_scr,
                 *, n_ctx):
    s = rr_ref.shape[1]
    w = rr_ref.shape[2]
    p_rows = RWKV_PREP_ROWS
    t = RWKV_CHUNK
    n_chunks = s // t
    n_ctx_chunks = n_ctx // t
    head_sum = ((_iota((w, w), 0) // RWKV_HEAD) == (_iota((w, w), 1) // RWKV_HEAD)).astype(BF16)
    inv_head = 1.0 / RWKV_HEAD

    def prep(j, carry):
        t0 = pl.multiple_of(j * p_rows, p_rows)
        mixed = []
        for idx, src in enumerate((rr_ref, rk_ref, rv_ref)):
            cur, down, up = _chunk_with_neighbors(src, t0, p_rows, s, s - n_ctx)
            mixed.append(cur + mu_ref[idx:idx + 1, :] * (0.5 * (down + up) - cur))
        r, kr, v = mixed
        kk = kr * kk_ref[...]
        norm = jnp.sqrt(_dot_sel(kk * kk, head_sum))
        kk = kk / jnp.maximum(norm, 1e-12)
        w_raw = _bdot(jnp.tanh(wd_ref[0, pl.ds(t0, p_rows), :]), wup_ref[0]) + w0_ref[0]
        a = _sigmoid(_bdot(ad_ref[0, pl.ds(t0, p_rows), :], aup_ref[0]) + a0_ref[0])
        lw = -math.exp(-0.5) * _sigmoid(w_raw)
        kt_sum = jnp.zeros_like(kr)
        for d in range(2):
            a_d = a[:, d * w:(d + 1) * w]
            kt_d = kr * (1.0 + (a_d - 1.0) * ka_ref[...])
            kt_sum = kt_sum + kt_d
            lw_scr[d, pl.ds(t0, p_rows), :] = lw[:, d * w:(d + 1) * w]
            ka_scr[d, pl.ds(t0, p_rows), :] = kk * a_d
            kt_scr[d, pl.ds(t0, p_rows), :] = kt_d
        coef = _dot_sel(r * kt_sum * rkk_ref[...], head_sum)
        r_scr[pl.ds(t0, p_rows), :] = r
        v_scr[pl.ds(t0, p_rows), :] = v
        kk_scr[pl.ds(t0, p_rows), :] = kk
        bonus_scr[pl.ds(t0, p_rows), :] = coef * v
        return carry

    lax.fori_loop(0, s // p_rows, prep, 0)

    n2 = 2 * t
    r_i = _iota((n2, n2), 0)
    c_i = _iota((n2, n2), 1)
    same = (r_i // t) == (c_i // t)
    rt = r_i % t
    ct = c_i % t
    eye = (r_i == c_i).astype(F32)
    lane_lo = _iota((t, w), 1) < RWKV_HEAD
    consts = []
    for reverse in (False, True):
        strict = jnp.logical_and(same, (ct > rt) if reverse else (ct < rt))
        incl = jnp.logical_and(same, (ct >= rt) if reverse else (ct <= rt))
        merges = []
        c = 1
        while c < t:
            hi_r = (r_i % (2 * c)) >= c
            hi_c = (c_i % (2 * c)) >= c
            cross = jnp.logical_and(hi_c, jnp.logical_not(hi_r)) if reverse else jnp.logical_and(hi_r, jnp.logical_not(hi_c))
            mask = jnp.logical_and((r_i // (2 * c)) == (c_i // (2 * c)), cross)
            merges.append((c, _half_rows(mask, c, not reverse)) if c >= SUBLANES else (c, mask))
            c *= 2
        consts.append(dict(strict=strict, incl2=jnp.concatenate([incl, incl], axis=1), merges=merges))

    def operators(gi, carry):
        problems, where = [], []
        for k in range(RWKV_GROUP):
            chunk = gi * RWKV_GROUP + k
            sl = pl.ds(pl.multiple_of(chunk * t, t), t)
            r, v, kk = r_scr[sl, :], v_scr[sl, :], kk_scr[sl, :]
            for d in range(2):
                problems.append((r, v, kk, lw_scr[d, sl, :], ka_scr[d, sl, :], kt_scr[d, sl, :], d))
                where.append((d, chunk, sl))
        for (d, chunk, sl), (ry, y1, pt, qt, g) in zip(where, _rwkv_chunk_operators(problems, consts, eye, lane_lo)):
            ry_scr[d, sl, :] = ry.astype(BF16)
            y_scr[d, sl, :] = y1
            pt_scr[d, chunk] = pt.astype(BF16)
            qt_scr[d, chunk] = qt
            g_scr[d, chunk] = g
        return carry

    lax.fori_loop(0, n_chunks // RWKV_GROUP, operators, 0)

    def scan(i, carry):
        states = []
        for d, (chunk, ht) in enumerate(zip(_scan_chunks(i, n_chunks, n_ctx_chunks), carry)):
            sl = pl.ds(pl.multiple_of(chunk * t, t), t)
            y_scr[d, sl, :] = _bdot_nt(ry_scr[d, sl, :], ht) + y_scr[d, sl, :]
            states.append(ht * g_scr[d, chunk] + _bdot(ht, pt_scr[d, chunk]) + qt_scr[d, chunk])
        return tuple(states)

    zero_state = jnp.zeros((w, w), F32)
    lax.fori_loop(0, n_chunks, scan, (zero_state, zero_state))

    f_rows = RWKV_FINISH_ROWS

    def finish(j, carry):
        sl = pl.ds(pl.multiple_of(j * f_rows, f_rows), f_rows)
        y = y_scr[0, sl, :] + y_scr[1, sl, :] + bonus_scr[sl, :]
        mu = _dot_sel(y, head_sum) * inv_head
        yc = y - mu
        var = _dot_sel(yc * yc, head_sum) * inv_head
        yn = yc * lax.rsqrt(var + RWKV_LN_EPSILON) * lnw_ref[...] + lnb_ref[...]
        out_ref[0, sl, :] = yn * _silu(rz_ref[0, sl, :])
        return carry

    lax.fori_loop(0, s // f_rows, finish, 0)


def _rwkv(u, col0, p, n_ctx):
    bsz, s, _ = u.shape
    w = LANES
    width = p["mu"].shape[1]
    n_pairs = width // w
    base = col0 // w
    col = lambda k: pl.BlockSpec((1, s, w), lambda b, h, k=k: (b, 0, base + k * n_pairs + h))
    lora = lambda k: pl.BlockSpec((1, s, w), lambda b, h, k=k: (b, 0, base + 4 * n_pairs + k))
    vec = lambda rows: pl.BlockSpec((rows, w), lambda b, h: (0, h))
    cat = pl.BlockSpec((1, 1, 2 * w), lambda b, h: (h, 0, 0))
    up = pl.BlockSpec((1, w, 2 * w), lambda b, h: (h, 0, 0))
    seq = pltpu.VMEM((s, w), F32)
    seq2 = pltpu.VMEM((2, s, w), F32)
    n_chunks = s // RWKV_CHUNK
    assert n_chunks % RWKV_GROUP == 0 and s % RWKV_PREP_ROWS == 0 and n_ctx % RWKV_PREP_ROWS == 0
    assert s % RWKV_FINISH_ROWS == 0
    operators = [pltpu.VMEM((2, s, w), BF16), pltpu.VMEM((2, n_chunks, w, w), BF16),
                 pltpu.VMEM((2, n_chunks, w, w), F32), pltpu.VMEM((2, n_chunks, 1, w), F32)]
    return pl.pallas_call(
        functools.partial(_rwkv_kernel, n_ctx=n_ctx),
        grid=(bsz, n_pairs),
        in_specs=[col(0), col(1), col(2), col(3), lora(0), lora(1),
                  vec(3), vec(1), vec(1), vec(1), vec(1), vec(1), cat, cat, up, up],
        out_specs=pl.BlockSpec((1, s, w), lambda b, h: (b, 0, h)),
        out_shape=jax.ShapeDtypeStruct((bsz, s, width), F32),
        scratch_shapes=[seq, seq, seq, seq2, seq2, seq2, seq, seq2] + operators,
        compiler_params=_compiler_params(("parallel", "arbitrary")),
        name="rwkv7_mixer",
    )(u, u, u, u, u, u, p["mu"], p["k_k"], p["k_a"], p["r_k"], p["ln_w"], p["ln_b"],
      p["w0"], p["a0"], p["w_up"], p["a_up"])


def _hgrn_level_masks(t, w):
    rid = _iota((t, w), 0)
    r_i = _iota((t, t), 0)
    c_i = _iota((t, t), 1)
    levels = []
    c = 1
    while c < t:
        same_block = (r_i // (2 * c)) == (c_i // (2 * c))
        up_r = (r_i % (2 * c)) >= c
        up_c = (c_i % (2 * c)) >= c
        pair = [jnp.logical_and(same_block, jnp.logical_and(up_r, jnp.logical_not(up_c))),
                jnp.logical_and(same_block, jnp.logical_and(up_c, jnp.logical_not(up_r)))]
        levels.append((c, (rid % (2 * c)) >= c, pair))
        c *= 2
    return levels


def _hgrn_chunk_operators(problems, lb, tris, levels):
    t, w = problems[0][0].shape
    r_i = _iota((t, t), 0)
    c_i = _iota((t, t), 1)
    zero_row = jnp.zeros((1, w), F32)
    dirs = [p[3] for p in problems]
    lgs, ks = [], []
    for q, v, ff, d in problems:
        e = jnp.exp(-jnp.abs(ff))
        big = 1.0 / (1.0 + e)
        small = e / (1.0 + e)
        pos = ff >= 0.0
        lgs.append(jnp.log(lb + (1.0 - lb) * jnp.where(pos, big, small)) * LOG2_E)
        ks.append((1.0 - lb) * jnp.where(pos, small, big))
    bs = [_sel_dot(tris[d], lg) for lg, d in zip(lgs, dirs)]
    befores = [_shifted(b, zero_row, zero_row)[1 if d == 1 else 0] for b, d in zip(bs, dirs)]
    edges = list(bs)
    accs = [jnp.where(r_i == c_i, jnp.sum(p[0] * k, axis=-1, keepdims=True), 0.0) for p, k in zip(problems, ks)]
    for c, upper, pair in levels:
        qts = [p[0] * jnp.exp2(b - before) for p, b, before in zip(problems, bs, befores)]
        kts = [k * jnp.exp2(edge - b) for k, b, edge in zip(ks, bs, edges)]
        prods = [_bdot_nt(qt, kt) for qt, kt in zip(qts, kts)]
        accs = [a + jnp.where(pair[d], pr, 0.0) for a, pr, d in zip(accs, prods, dirs)]
        for i, d in enumerate(dirs):
            if d == 1:
                befores[i] = jnp.where(upper, befores[i], pltpu.roll(befores[i], t - c, 0))
                edges[i] = jnp.where(upper, pltpu.roll(edges[i], c, 0), edges[i])
            else:
                befores[i] = jnp.where(upper, pltpu.roll(befores[i], c, 0), befores[i])
                edges[i] = jnp.where(upper, edges[i], pltpu.roll(edges[i], t - c, 0))
    o_intras = [_bdot(a, p[1]) for a, p in zip(accs, problems)]
    b_ends = [b[(0 if d == 1 else t - 1):(1 if d == 1 else t)] for b, d in zip(bs, dirs)]
    kvs = [_bdot_tn(p[1], k * jnp.exp2(be - b)) for p, k, b, be in zip(problems, ks, bs, b_ends)]
    return [(p[0] * jnp.exp2(b), oi, kv, jnp.exp2(be)) for p, b, oi, kv, be in zip(problems, bs, o_intras, kvs, b_ends)]


def _hgrn_kernel(q_ref, i_ref, ff_ref, fb_ref, z_ref, lb_ref, ng_ref, out_ref, o_scr, qe_scr, kv_scr, g_scr,
                 *, n_ctx, layer):
    s = q_ref.shape[1]
    dh = q_ref.shape[2]
    t = MIX_CHUNK
    n_chunks = s // t
    n_ctx_chunks = n_ctx // t
    lbs = lb_ref[...]
    ex = jnp.exp(lbs - jnp.max(lbs, axis=0, keepdims=True))
    probs = ex / jnp.sum(ex, axis=0, keepdims=True)
    csum = probs[0:1]
    for l in range(1, layer + 1):
        csum = csum + probs[l:l + 1]
    lb = csum - probs[0:1]
    tri_r = _iota((t, t), 0)
    tri_c = _iota((t, t), 1)
    tris = [(tri_c <= tri_r).astype(BF16), (tri_c >= tri_r).astype(BF16)]
    levels = _hgrn_level_masks(t, dh)

    def operators(gi, carry):
        problems, where = [], []
        for kk in range(HGRN_GROUP):
            chunk = gi * HGRN_GROUP + kk
            sl = pl.ds(pl.multiple_of(chunk * t, t), t)
            q, v = q_ref[0, sl, :], i_ref[0, sl, :]
            for d, f_ref in enumerate((ff_ref, fb_ref)):
                problems.append((q, v, f_ref[0, sl, :], d))
                where.append((d, chunk, sl))
        for (d, chunk, sl), (qe, o_intra, kv, g) in zip(where, _hgrn_chunk_operators(problems, lb, tris, levels)):
            qe_scr[d, sl, :] = qe.astype(BF16)
            o_scr[d, sl, :] = o_intra
            kv_scr[d, chunk] = kv
            g_scr[d, chunk] = g
        return carry

    lax.fori_loop(0, n_chunks // HGRN_GROUP, operators, 0)

    def scan(i, carry):
        chunks = _scan_chunks(i, n_chunks, n_ctx_chunks)
        sls = [pl.ds(pl.multiple_of(c * t, t), t) for c in chunks]
        inters = [_bdot_nt(qe_scr[d, sl, :], st) for d, (sl, st) in enumerate(zip(sls, carry))]
        new = []
        for d, (chunk, sl, inter, st) in enumerate(zip(chunks, sls, inters, carry)):
            o_scr[d, sl, :] = o_scr[d, sl, :] + inter
            new.append(st * g_scr[d, chunk] + kv_scr[d, chunk])
        return tuple(new)

    zero_state = jnp.zeros((dh, dh), F32)
    lax.fori_loop(0, n_chunks, scan, (zero_state, zero_state))

    f_rows = MIX_FINISH_ROWS

    def finish(j, carry):
        sl = pl.ds(pl.multiple_of(j * f_rows, f_rows), f_rows)
        o = o_scr[0, sl, :] + o_scr[1, sl, :]
        y = o * lax.rsqrt(jnp.mean(o * o, axis=-1, keepdims=True) + RMS_EPS) * ng_ref[...]
        out_ref[0, sl, :] = y * _silu(z_ref[0, sl, :])
        return carry

    lax.fori_loop(0, (s - n_ctx) // f_rows, finish, 0)


def _hgrn(u, lb_all, norm_g, n_ctx, layer):
    bsz, s, _ = u.shape
    nh = HGRN_N_HEADS
    dh = LANES
    width = nh * dh
    depth = lb_all.shape[0]
    n_chunks = s // MIX_CHUNK
    assert n_chunks % HGRN_GROUP == 0 and n_ctx % MIX_CHUNK == 0
    col = lambda k: pl.BlockSpec((1, s, dh), lambda b, h, k=k: (b, 0, k * nh + h))
    return pl.pallas_call(
        functools.partial(_hgrn_kernel, n_ctx=n_ctx, layer=layer),
        grid=(bsz, nh),
        in_specs=[col(0), col(1), col(2), col(3), col(4),
                  pl.BlockSpec((depth, dh), lambda b, h: (0, h)),
                  pl.BlockSpec((1, dh), lambda b, h: (0, h))],
        out_specs=pl.BlockSpec((1, s - n_ctx, dh), lambda b, h: (b, 0, h)),
        out_shape=jax.ShapeDtypeStruct((bsz, s - n_ctx, width), F32),
        scratch_shapes=[pltpu.VMEM((2, s, dh), F32), pltpu.VMEM((2, s, dh), BF16),
                        pltpu.VMEM((2, n_chunks, dh, dh), F32), pltpu.VMEM((2, n_chunks, 1, dh), F32)],
        compiler_params=_compiler_params(("parallel", "arbitrary")),
        name="hgrn2_mixer",
    )(u, u, u, u, u, lb_all, norm_g.reshape(1, width))


def _hyena_filter_kernel(z_ref, w1_ref, b1_ref, w2_ref, b2_ref, w3f_ref, w3b_ref, dl_ref, hf_ref, hb_ref):
    hp = functools.partial(jnp.dot, precision=lax.Precision.HIGHEST, preferred_element_type=F32)
    n = z_ref.shape[0]
    hid = jnp.sin(hp(z_ref[...], w1_ref[...]) + b1_ref[...])
    hid = jnp.sin(hp(hid, w2_ref[...]) + b2_ref[...])
    pos = _iota((n, 1), 0).astype(F32) * (1.0 / n)
    window = jnp.exp(-pos * dl_ref[...]) + HYENA_SHIFT
    f0 = hp(hid, w3f_ref[...]) * window
    f1 = hp(hid, w3b_ref[...]) * window
    nrm = jnp.sum(jnp.abs(f0), axis=0, keepdims=True) + jnp.sum(jnp.abs(f1), axis=0, keepdims=True)
    hf_ref[...] = f0 / nrm
    hb_ref[...] = f1 / nrm


def _hyena_filters(n, w1, b1, w2, b2, w3, width):
    pos = np.arange(n, dtype=np.float64)
    bands = np.linspace(1e-4, HYENA_N_BANDS - 1, HYENA_N_BANDS)
    ang = (2.0 * math.pi / n) * pos[:, None] * bands
    z = np.concatenate([(pos / n)[:, None], np.cos(ang), np.sin(ang)], axis=-1)
    z = np.pad(z, ((0, 0), (0, LANES - z.shape[1]))).astype(np.float32)
    max_decay = math.log(HYENA_TGT) / HYENA_FAST
    min_decay = math.log(HYENA_TGT) / HYENA_SLOW
    deltas = np.abs(np.linspace(min_decay, max_decay, width)).astype(np.float32)[None]
    feat, hid = w1.shape
    w1p = jnp.pad(w1, ((0, LANES - feat), (0, LANES - hid)))
    w2p = jnp.pad(w2, ((0, LANES - hid), (0, LANES - hid)))
    w3p = jnp.pad(w3, ((0, LANES - hid), (0, 0)))
    b1p = jnp.pad(b1, (0, LANES - hid)).reshape(1, LANES)
    b2p = jnp.pad(b2, (0, LANES - hid)).reshape(1, LANES)
    n_tiles = width // LANES
    full = lambda shape: pl.BlockSpec(shape, lambda j: (0, 0))
    out = pl.BlockSpec((n, LANES), lambda j: (0, j))
    hf, hb = pl.pallas_call(
        _hyena_filter_kernel,
        grid=(n_tiles,),
        in_specs=[full((n, LANES)), full((LANES, LANES)), full((1, LANES)), full((LANES, LANES)), full((1, LANES)),
                  pl.BlockSpec((LANES, LANES), lambda j: (0, j)),
                  pl.BlockSpec((LANES, LANES), lambda j: (0, n_tiles + j)),
                  pl.BlockSpec((1, LANES), lambda j: (0, j))],
        out_specs=[out, out],
        out_shape=[jax.ShapeDtypeStruct((n, width), F32)] * 2,
        compiler_params=_compiler_params(("arbitrary",)),
        name="hyena_filters",
    )(jnp.asarray(z), w1p, b1p, w2p, b2p, w3p, w3p, jnp.asarray(deltas))
    return jnp.concatenate([hf, hb], axis=1)


def _hyena_pre_kernel(yv_ref, y0_ref, y1_ref, yz_ref, swv_ref, sw0_ref, sw1_ref, sbv_ref, sb0_ref, sb1_ref,
                      yb_ref, p_ref, e_ref, pb_ref, *, n_ctx):
    s = yv_ref.shape[1]
    rows = MIX_CHUNK

    def body(j, carry):
        t0 = pl.multiple_of(j * rows, rows)
        conv = []
        for src, sw, sb in ((yv_ref, swv_ref, sbv_ref), (y0_ref, sw0_ref, sb0_ref), (y1_ref, sw1_ref, sb1_ref)):
            cur, down, up = _chunk_with_neighbors(src, t0, rows, s, s - n_ctx)
            conv.append(down * sw[0:1, :] + cur * sw[1:2, :] + up * sw[2:3, :] + sb[...])
        v, x0, x1 = conv
        p = x1 * v
        o0 = pl.multiple_of(j * rows, rows)
        p_ref[0, pl.ds(o0, rows), :] = p.astype(BF16)
        pb_ref[0, pl.ds(o0, rows), :] = p * yb_ref[...]
        e_ref[0, pl.ds(o0, rows), :] = x0 * _silu(yz_ref[0, pl.ds(t0, rows), :])
        return carry

    lax.fori_loop(0, (s - n_ctx) // rows, body, 0)


def _hyena_pre(u, col0, short_w, short_b, y_bias, n_ctx):
    bsz, s, _ = u.shape
    w = y_bias.shape[0]
    tiles = w // LANES
    base = col0 // LANES
    n = s - n_ctx
    col = lambda k: pl.BlockSpec((1, s, LANES), lambda b, j, k=k: (b, 0, base + k * tiles + j))
    par = lambda rows, k: pl.BlockSpec((rows, LANES), lambda b, j, k=k: (0, k * tiles + j))
    out = pl.BlockSpec((1, n, LANES), lambda b, j: (b, 0, j))
    sb = short_b.reshape(1, 3 * w)
    return pl.pallas_call(
        functools.partial(_hyena_pre_kernel, n_ctx=n_ctx),
        grid=(bsz, tiles),
        in_specs=[col(0), col(1), col(2), col(3),
                  par(3, 0), par(3, 1), par(3, 2), par(1, 0), par(1, 1), par(1, 2), par(1, 0)],
        out_specs=[out, out, out],
        out_shape=[jax.ShapeDtypeStruct((bsz, n, w), BF16),
                   jax.ShapeDtypeStruct((bsz, n, w), F32),
                   jax.ShapeDtypeStruct((bsz, n, w), F32)],
        compiler_params=_compiler_params(("parallel", "arbitrary")),
        name="hyena_short_conv",
    )(u, u, u, u, short_w, short_w, short_w, sb, sb, sb, y_bias.reshape(1, w))


def _dft_tables(n):
    big = 2 * n
    half = DFT_TILE // 2
    idx = jnp.arange(n, dtype=jnp.int32)
    split = DFT_SPLIT
    lo = jnp.arange(split, dtype=jnp.int32)
    hi = jnp.arange(n // split, dtype=jnp.int32)
    ang_lo = ((lo[:, None] * idx[None, :]) % big).astype(F32) * (2.0 * math.pi / big)
    ang_hi = ((hi[:, None] * idx[None, :]) % (big // split)).astype(F32) * (2.0 * math.pi * split / big)
    c_lo, s_lo = jnp.cos(ang_lo)[None], jnp.sin(ang_lo)[None]
    c_hi, s_hi = jnp.cos(ang_hi)[:, None], jnp.sin(ang_hi)[:, None]
    cos = (c_hi * c_lo - s_hi * s_lo).reshape(n, n)
    sin = (s_hi * c_lo + c_hi * s_lo).reshape(n, n)
    alt = jnp.where(idx % 2 == 0, 1.0, -1.0).astype(F32)
    first_row = (idx == 0)[:, None]
    im = jnp.where(first_row, alt[None, :], -sin)
    fwd = jnp.stack([cos.reshape(n // half, half, n), im.reshape(n // half, half, n)], axis=1).reshape(big, n)
    weight = jnp.where(first_row, 1.0, 2.0) * (1.0 / big)
    weight = jnp.stack([weight.reshape(n // half, half, 1)] * 2, axis=1).reshape(big, 1)
    return fwd.astype(BF16), (fwd * weight).T.astype(BF16)


def _spectrum_kernel(f_ref, x_ref, o_ref):
    half = DFT_TILE // 2
    w = o_ref.shape[1]
    acc = jnp.dot(f_ref[...], x_ref[...].astype(BF16), preferred_element_type=F32)
    o_ref[:half, :] = acc[:half, :w] + acc[:half, w:]
    k_im = acc[half:, :w] - acc[half:, w:]
    packed = jnp.logical_and(_iota((half, w), 0) == 0, pl.program_id(0) == 0)
    o_ref[half:, :] = jnp.where(packed, acc[half:, :w] + acc[half:, w:], k_im)


def _filter_spectrum(fwd, hk):
    big, n = fwd.shape
    cols = hk.shape[1]
    return pl.pallas_call(
        _spectrum_kernel,
        grid=(big // DFT_TILE,),
        in_specs=[pl.BlockSpec((DFT_TILE, n), lambda i: (i, 0)),
                  pl.BlockSpec((n, cols), lambda i: (0, 0))],
        out_specs=pl.BlockSpec((DFT_TILE, cols // 2), lambda i: (i, 0)),
        out_shape=jax.ShapeDtypeStruct((big, cols // 2), F32),
        compiler_params=_compiler_params(("arbitrary",)),
        name="hyena_filter_spectrum",
    )(fwd, hk)


def _conv_spectrum_kernel(f_ref, p_ref, ks_ref, z_ref):
    i = pl.program_id(1)
    half = DFT_TILE // 2
    w = p_ref.shape[2]
    units = [slice(k * DFT_TILE, (k + 1) * DFT_TILE) for k in range(DFT_UNITS)]
    accs = [jnp.dot(f_ref[u, :], p_ref[0], preferred_element_type=F32) for u in units]
    for k, (u, acc) in enumerate(zip(units, accs)):
        s_re, s_im = acc[:half], acc[half:]
        k_re, k_im = ks_ref[u, :][:half], ks_ref[u, :][half:]
        z_re = s_re * k_re - s_im * k_im
        z_im = s_re * k_im + s_im * k_re
        if k == 0:
            packed = jnp.logical_and(_iota((half, w), 0) == 0, i == 0)
            z_re = jnp.where(packed, s_re * k_re, z_re)
            z_im = jnp.where(packed, s_im * k_im, z_im)
        z_ref[0, u, :] = jnp.concatenate([z_re, z_im], axis=0).astype(BF16)


def _conv_spectrum(fwd, p16, kspec):
    bsz, n, w = p16.shape
    big = fwd.shape[0]
    rows = DFT_TILE * DFT_UNITS
    return pl.pallas_call(
        _conv_spectrum_kernel,
        grid=(bsz, big // rows),
        in_specs=[pl.BlockSpec((rows, n), lambda b, i: (i, 0)),
                  pl.BlockSpec((1, n, w), lambda b, i: (b, 0, 0)),
                  pl.BlockSpec((rows, w), lambda b, i: (i, 0))],
        out_specs=pl.BlockSpec((1, rows, w), lambda b, i: (b, i, 0)),
        out_shape=jax.ShapeDtypeStruct((bsz, big, w), BF16),
        compiler_params=_compiler_params(("parallel", "arbitrary")),
        name="hyena_forward_dft",
    )(fwd, p16, kspec)


def _conv_inverse_kernel(g_ref, z_ref, e_ref, pb_ref, o_ref):
    y = jnp.dot(g_ref[...], z_ref[0], preferred_element_type=F32)
    o_ref[0] = e_ref[0] * (y + pb_ref[0])


def _conv_inverse(inv, z16, e, pb):
    bsz, big, w = z16.shape
    n = inv.shape[0]
    tile = DFT_TILE
    tok = pl.BlockSpec((1, tile, w), lambda b, i: (b, i, 0))
    return pl.pallas_call(
        _conv_inverse_kernel,
        grid=(bsz, n // tile),
        in_specs=[pl.BlockSpec((tile, big), lambda b, i: (i, 0)),
                  pl.BlockSpec((1, big, w), lambda b, i: (b, 0, 0)),
                  tok, tok],
        out_specs=tok,
        out_shape=jax.ShapeDtypeStruct((bsz, n, w), F32),
        compiler_params=_compiler_params(("parallel", "arbitrary")),
        name="hyena_inverse_dft",
    )(inv, z16, e, pb)


def _even_weight_layout(w_in, gate_b):
    d = w_in.shape[0]
    mw = MLSTM_N_HEADS * LANES
    g0 = 5 * mw
    g1 = g0 + 4 * MLSTM_N_HEADS
    main = jnp.concatenate([w_in[:, :g0], w_in[:, g1:]], axis=1).astype(BF16)
    wg = w_in[:, g0:g1].reshape(d, 2, 2, MLSTM_N_HEADS)
    wg = jnp.transpose(wg, (3, 1, 2, 0)).reshape(MLSTM_N_HEADS, 4, d)
    wg = jnp.concatenate([wg, jnp.zeros_like(wg)], axis=1).reshape(MLSTM_N_HEADS * 8, d).astype(BF16)
    gb = jnp.transpose(gate_b.reshape(2, 2, MLSTM_N_HEADS), (2, 0, 1)).reshape(MLSTM_N_HEADS, 4)
    gb = jnp.concatenate([gb, jnp.zeros_like(gb)], axis=1).reshape(MLSTM_N_HEADS * 8, 1)
    return main, wg, jnp.broadcast_to(gb, (MLSTM_N_HEADS * 8, LANES))


def _rwkv_params(mu, w0, w_up, a0, a_up, k_k, k_a, r_k, ln_w, ln_b):
    width = mu.shape[1]
    n_pairs = width // LANES
    row = lambda x: x.reshape(1, width)

    def cat_dirs(x):
        return jnp.transpose(x.reshape(2, n_pairs, LANES), (1, 0, 2)).reshape(n_pairs, 1, 2 * LANES)

    def block_up(x):
        lora = x.shape[1]
        xp = jnp.transpose(x.reshape(2, lora, n_pairs, LANES), (2, 0, 1, 3))
        z = jnp.zeros_like(xp[:, 0])
        top = jnp.concatenate([xp[:, 0], z], axis=2)
        bot = jnp.concatenate([z, xp[:, 1]], axis=2)
        return jnp.concatenate([top, bot], axis=1).astype(BF16)

    return {"mu": mu, "k_k": row(k_k), "k_a": row(k_a), "r_k": row(r_k), "ln_w": row(ln_w), "ln_b": row(ln_b),
            "w0": cat_dirs(w0), "a0": cat_dirs(a0), "w_up": block_up(w_up), "a_up": block_up(a_up)}


def _raster_to_column(h):
    b, n, d = h.shape
    rows = n // GRID_WIDTH
    return h.reshape(b, rows, GRID_WIDTH, d).transpose(0, 2, 1, 3).reshape(b, n, d)


def _column_to_raster(h):
    b, n, d = h.shape
    rows = n // GRID_WIDTH
    return h.reshape(b, GRID_WIDTH, rows, d).transpose(0, 2, 1, 3).reshape(b, n, d)


def kernel(x, c, ctx, c_ctx, l0_norm_g, l0_mod_w, l0_mod_b, l0_w_in, l0_w_out, l0_mlstm_conv_w, l0_mlstm_gate_b, l0_mlstm_norm_g, l0_rwkv_mu, l0_rwkv_w0, l0_rwkv_w_up, l0_rwkv_a0, l0_rwkv_a_up, l0_rwkv_k_k, l0_rwkv_k_a, l0_rwkv_r_k, l0_rwkv_ln_w, l0_rwkv_ln_b, hgrn_lower_bounds, l1_norm_g, l1_mod_w, l1_mod_b, l1_w_in, l1_w_out, l1_hgrn_norm_g, l1_hyena_short_w, l1_hyena_short_b, l1_hyena_w1, l1_hyena_b1, l1_hyena_w2, l1_hyena_b2, l1_hyena_w3, l1_hyena_bias, final_norm_g):
    bsz, n_lat, d = x.shape
    n_ctx = ctx.shape[1]

    pad = (-(bsz + 1)) % 8
    cc = jnp.concatenate([c, c_ctx[None], jnp.zeros((pad, d), F32)], axis=0)
    mod0, mod1 = _modulation(cc, l0_mod_w, l0_mod_b, l1_mod_w, l1_mod_b)
    mod0 = mod0[:bsz + 1].reshape(bsz + 1, 3, d)
    mod1 = mod1[:bsz + 1].reshape(bsz + 1, 3, d)

    w_main, w_gate, b_gate = _even_weight_layout(l0_w_in, l0_mlstm_gate_b)
    n0 = w_main.shape[1]
    u0, gt0 = _proj_in(x, ctx, l0_norm_g, mod0, w_main, n0 // 2, w_gate, b_gate)
    gt0 = gt0.reshape(bsz, MLSTM_N_HEADS, 8, n_ctx + n_lat)
    y_m = _mlstm(u0, gt0, l0_mlstm_conv_w, l0_mlstm_norm_g, n_ctx)
    rp = _rwkv_params(l0_rwkv_mu, l0_rwkv_w0, l0_rwkv_w_up, l0_rwkv_a0, l0_rwkv_a_up, l0_rwkv_k_k,
                      l0_rwkv_k_a, l0_rwkv_r_k, l0_rwkv_ln_w, l0_rwkv_ln_b)
    y_r = _rwkv(u0, 5 * MLSTM_N_HEADS * LANES, rp, n_ctx)
    x1, ctx1 = _proj_out(y_m, y_r, x, ctx, mod0, l0_w_out.astype(BF16))

    x1c = _raster_to_column(x1)
    w1 = l1_w_in.astype(BF16)
    (u1,) = _proj_in(x1c, ctx1, l1_norm_g, mod1, w1, w1.shape[1] // 2)
    y_g = _hgrn(u1, hgrn_lower_bounds, l1_hgrn_norm_g, n_ctx, layer=1)
    hw = l1_hyena_bias.shape[0]
    hk = _hyena_filters(n_lat, l1_hyena_w1, l1_hyena_b1, l1_hyena_w2, l1_hyena_b2, l1_hyena_w3, hw)
    fwd, inv = _dft_tables(n_lat)
    kspec = _filter_spectrum(fwd, hk)
    p16, e, pb = _hyena_pre(u1, 5 * HGRN_N_HEADS * LANES, l1_hyena_short_w, l1_hyena_short_b, l1_hyena_bias, n_ctx)
    z16 = _conv_spectrum(fwd, p16, kspec)
    y_y = _conv_inverse(inv, z16, e, pb)
    out_c = _proj_out_final(y_g, y_y, x1c, mod1, l1_w_out.astype(BF16), final_norm_g)
    return _column_to_raster(out_c)
```

```python
import functools
import math

import jax
import jax.numpy as jnp
import numpy as np
from jax import lax
from jax.experimental import pallas as pl
from jax.experimental.pallas import tpu as pltpu

F32 = jnp.float32
BF16 = jnp.bfloat16

GRID_WIDTH = 64
RMS_EPS = 1e-6
MLSTM_N_HEADS = 4
RWKV_HEAD = 64
RWKV_LN_EPSILON = 64e-5
HGRN_N_HEADS = 4
HYENA_N_BANDS = 16
HYENA_FAST = 0.3
HYENA_SLOW = 1.5
HYENA_TGT = 1e-2
HYENA_SHIFT = 0.05
LOG2_E = 1.0 / math.log(2.0)

LANES = 128
SUBLANES = 8
ROW_GROUP = 16
MXU_DIM = 256
VMEM_LIMIT = 52 * 1024 * 1024

MIX_CHUNK = 128
RWKV_CHUNK = 64
RWKV_GROUP = 9
MIX_PREP_ROWS = 256
MIX_FINISH_ROWS = 256
RWKV_PREP_ROWS = 256
RWKV_FINISH_ROWS = 768
MLSTM_GROUP = 6
HGRN_GROUP = 3
PROJ_ROWS = 768
FINAL_ROWS = 512
DFT_TILE = 512
DFT_UNITS = 2
DFT_SPLIT = 64


def _bdot(a, b):
    return jnp.dot(a.astype(BF16), b.astype(BF16), preferred_element_type=F32)


def _bdot_nt(a, b):
    return lax.dot_general(a.astype(BF16), b.astype(BF16), (((1,), (1,)), ((), ())),
                           preferred_element_type=F32)


def _bdot_tn(a, b):
    return lax.dot_general(a.astype(BF16), b.astype(BF16), (((0,), (0,)), ((), ())),
                           preferred_element_type=F32)


def _split3(x):
    hi = x.astype(BF16)
    r1 = x - hi.astype(F32)
    mid = r1.astype(BF16)
    lo = (r1 - mid.astype(F32)).astype(BF16)
    return hi, mid, lo


def _sel_dot(sel, x):
    hi, mid, lo = _split3(x)
    d = functools.partial(jnp.dot, preferred_element_type=F32)
    return d(sel, hi) + d(sel, mid) + d(sel, lo)


def _dot_sel(x, sel):
    hi, mid, lo = _split3(x)
    d = functools.partial(jnp.dot, preferred_element_type=F32)
    return d(hi, sel) + d(mid, sel) + d(lo, sel)


def _sigmoid(x):
    return 1.0 / (1.0 + jnp.exp(-x))


def _silu(x):
    return x * _sigmoid(x)


def _iota(shape, dim):
    return lax.broadcasted_iota(jnp.int32, shape, dim)


def _neighbor_rows(ref, t0, rows, n_total, split):
    has_prev = jnp.logical_and(t0 != 0, t0 != split)
    has_next = jnp.logical_and(t0 + rows != split, t0 + rows != n_total)
    g = ROW_GROUP
    before = ref[0, pl.ds(pl.multiple_of(jnp.maximum(t0 - g, 0), g), g), :].astype(F32)
    after = ref[0, pl.ds(pl.multiple_of(jnp.minimum(t0 + rows, n_total - g), g), g), :].astype(F32)
    return jnp.where(has_prev, before[g - 1:g], 0.0), jnp.where(has_next, after[0:1], 0.0)


def _shifted(cur, prev_row, next_row):
    rows = cur.shape[0]
    rid = _iota(cur.shape, 0)
    down = jnp.where(rid == 0, prev_row, pltpu.roll(cur, 1, 0))
    up = jnp.where(rid == rows - 1, next_row, pltpu.roll(cur, rows - 1, 0))
    return down, up


def _chunk_with_neighbors(ref, t0, rows, n_total, split):
    cur = ref[0, pl.ds(t0, rows), :].astype(F32)
    prev_row, next_row = _neighbor_rows(ref, t0, rows, n_total, split)
    down, up = _shifted(cur, prev_row, next_row)
    return cur, down, up


def _scan_chunks(i, n_chunks, n_ctx_chunks):
    fwd = jnp.where(i < n_ctx_chunks, n_chunks - n_ctx_chunks + i, i - n_ctx_chunks)
    return fwd, n_chunks - 1 - i


def _compiler_params(semantics):
    return pltpu.CompilerParams(dimension_semantics=semantics, vmem_limit_bytes=VMEM_LIMIT)


def _mod_kernel(c_ref, w0_ref, b0_ref, w1_ref, b1_ref, o0_ref, o1_ref):
    s = _silu(c_ref[...])
    o0_ref[...] = _bdot(s, w0_ref[...]) + b0_ref[...]
    o1_ref[...] = _bdot(s, w1_ref[...]) + b1_ref[...]


def _modulation(cc, w0, b0, w1, b1):
    rows, d = cc.shape
    n = w0.shape[1]
    tile = d
    grid = (n // tile,)
    wspec = pl.BlockSpec((d, tile), lambda j: (0, j))
    bspec = pl.BlockSpec((1, tile), lambda j: (0, j))
    ospec = pl.BlockSpec((rows, tile), lambda j: (0, j))
    return pl.pallas_call(
        _mod_kernel,
        grid=grid,
        in_specs=[pl.BlockSpec((rows, d), lambda j: (0, 0)), wspec, bspec, wspec, bspec],
        out_specs=[ospec, ospec],
        out_shape=[jax.ShapeDtypeStruct((rows, n), F32)] * 2,
        compiler_params=_compiler_params(("arbitrary",)),
        name="adaln_modulation",
    )(cc, w0, b0.reshape(1, n), w1, b1.reshape(1, n))


def _token_tile(x_ref, c_ref, i, rows):
    n_lat_tail = rows - c_ref.shape[1]
    is_ctx = jnp.logical_and(i == pl.num_programs(1) - 1, _iota((rows, 1), 0) >= n_lat_tail)
    ctx_rows = jnp.concatenate([jnp.zeros((n_lat_tail, c_ref.shape[2]), F32), c_ref[0]], axis=0)
    return jnp.where(is_ctx, ctx_rows, x_ref[0]), is_ctx


def _proj_in_kernel(*refs, rows, with_gates):
    if with_gates:
        x_ref, c_ref, g_ref, ml_ref, mc_ref, w_ref, wg_ref, gb_ref, u_ref, gt_ref, h_scr = refs
    else:
        x_ref, c_ref, g_ref, ml_ref, mc_ref, w_ref, u_ref, h_scr = refs
    i = pl.program_id(1)
    n = pl.program_id(2)

    @pl.when(n == 0)
    def _():
        x, is_ctx = _token_tile(x_ref, c_ref, i, rows)
        y = x * lax.rsqrt(jnp.mean(x * x, axis=-1, keepdims=True) + RMS_EPS) * g_ref[...]
        ml = ml_ref[0]
        mc = mc_ref[0]
        shift = jnp.where(is_ctx, mc[0:1], ml[0:1])
        scale = jnp.where(is_ctx, mc[1:2], ml[1:2])
        h = (y * (1.0 + scale) + shift).astype(BF16)
        h_scr[...] = h
        if with_gates:
            gt_ref[0] = _bdot_nt(wg_ref[...], h) + gb_ref[:, 0:1]

    u_ref[0] = jnp.dot(h_scr[...], w_ref[...], preferred_element_type=F32).astype(u_ref.dtype)


def _proj_in(x, ctx, norm_g, mod3, w16, n_tile, gate_w=None, gate_b=None):
    bsz, n_lat, d = x.shape
    n_ctx = ctx.shape[1]
    s = n_lat + n_ctx
    n = w16.shape[1]
    rows = PROJ_ROWS
    assert s % rows == 0 and (n_lat % rows) + n_ctx == rows
    grid = (bsz, s // rows, n // n_tile)
    ctx_row = mod3.shape[0] - 1
    with_gates = gate_w is not None
    in_specs = [
        pl.BlockSpec((1, rows, d), lambda b, i, j: (b, i, 0)),
        pl.BlockSpec((1, n_ctx, d), lambda b, i, j: (b, 0, 0)),
        pl.BlockSpec((1, d), lambda b, i, j: (0, 0)),
        pl.BlockSpec((1, 3, d), lambda b, i, j: (b, 0, 0)),
        pl.BlockSpec((1, 3, d), lambda b, i, j: (ctx_row, 0, 0)),
        pl.BlockSpec((d, n_tile), lambda b, i, j: (0, j)),
    ]
    args = [x, ctx, norm_g.reshape(1, d), mod3, mod3, w16]
    out_specs = [pl.BlockSpec((1, rows, n_tile), lambda b, i, j: (b, i, j))]
    out_shape = [jax.ShapeDtypeStruct((bsz, s, n), BF16)]
    if with_gates:
        ng = gate_w.shape[0]
        in_specs += [pl.BlockSpec((ng, d), lambda b, i, j: (0, 0)),
                     pl.BlockSpec((ng, LANES), lambda b, i, j: (0, 0))]
        args += [gate_w, gate_b]
        out_specs.append(pl.BlockSpec((1, ng, rows), lambda b, i, j: (b, 0, i)))
        out_shape.append(jax.ShapeDtypeStruct((bsz, ng, s), F32))
    return pl.pallas_call(
        functools.partial(_proj_in_kernel, rows=rows, with_gates=with_gates),
        grid=grid,
        in_specs=in_specs,
        out_specs=out_specs,
        out_shape=out_shape,
        scratch_shapes=[pltpu.VMEM((rows, d), BF16)],
        compiler_params=_compiler_params(("parallel", "arbitrary", "arbitrary")),
        name="norm_mod_proj_in",
    )(*args)


def _proj_out_kernel(ya_ref, yb_ref, x_ref, c_ref, ml_ref, mc_ref, w_ref, ox_ref, oc_ref, *, rows):
    i = pl.program_id(1)
    half = ya_ref.shape[2]
    y = _bdot(ya_ref[0], w_ref[0:half, :]) + _bdot(yb_ref[0], w_ref[half:, :])
    x, is_ctx = _token_tile(x_ref, c_ref, i, rows)
    x = x + jnp.where(is_ctx, mc_ref[0][2:3], ml_ref[0][2:3]) * y
    ox_ref[0] = x

    @pl.when(i == pl.num_programs(1) - 1)
    def _():
        oc_ref[0] = x[rows - c_ref.shape[1]:]


def _proj_out(ya, yb, x, ctx, mod3, w16):
    bsz, n_lat, d = x.shape
    n_ctx = ctx.shape[1]
    s = n_lat + n_ctx
    half = ya.shape[2]
    rows = PROJ_ROWS
    assert s % rows == 0 and (n_lat % rows) + n_ctx == rows
    ctx_row = mod3.shape[0] - 1
    tok = lambda w: pl.BlockSpec((1, rows, w), lambda b, i: (b, i, 0))
    seg = pl.BlockSpec((1, n_ctx, d), lambda b, i: (b, 0, 0))
    return pl.pallas_call(
        functools.partial(_proj_out_kernel, rows=rows),
        grid=(bsz, s // rows),
        in_specs=[tok(half), tok(half), tok(d), seg,
                  pl.BlockSpec((1, 3, d), lambda b, i: (b, 0, 0)),
                  pl.BlockSpec((1, 3, d), lambda b, i: (ctx_row, 0, 0)),
                  pl.BlockSpec((2 * half, d), lambda b, i: (0, 0))],
        out_specs=[tok(d), seg],
        out_shape=[jax.ShapeDtypeStruct((bsz, n_lat, d), F32), jax.ShapeDtypeStruct((bsz, n_ctx, d), F32)],
        compiler_params=_compiler_params(("parallel", "arbitrary")),
        name="proj_out_residual",
    )(ya, yb, x, ctx, mod3, mod3, w16)


def _proj_out_final_kernel(ya_ref, yb_ref, x_ref, ml_ref, w_ref, fg_ref, o_ref):
    half = ya_ref.shape[2]
    y = _bdot(ya_ref[0], w_ref[0:half, :]) + _bdot(yb_ref[0], w_ref[half:, :])
    x = x_ref[0] + ml_ref[0][2:3] * y
    o_ref[0] = x * lax.rsqrt(jnp.mean(x * x, axis=-1, keepdims=True) + RMS_EPS) * fg_ref[...]


def _proj_out_final(ya, yb, x, mod3, w16, final_g):
    bsz, n_lat, d = x.shape
    half = ya.shape[2]
    rows = FINAL_ROWS
    assert n_lat % rows == 0
    tok = lambda w: pl.BlockSpec((1, rows, w), lambda b, i: (b, i, 0))
    return pl.pallas_call(
        _proj_out_final_kernel,
        grid=(bsz, n_lat // rows),
        in_specs=[tok(half), tok(half), tok(d),
                  pl.BlockSpec((1, 3, d), lambda b, i: (b, 0, 0)),
                  pl.BlockSpec((2 * half, d), lambda b, i: (0, 0)),
                  pl.BlockSpec((1, d), lambda b, i: (0, 0))],
        out_specs=tok(d),
        out_shape=jax.ShapeDtypeStruct((bsz, n_lat, d), F32),
        compiler_params=_compiler_params(("parallel", "arbitrary")),
        name="proj_out_final_norm",
    )(ya, yb, x, mod3, w16, final_g.reshape(1, d))


def _mlstm_chunk_operators(chunks, causal):
    t = chunks[0][0].shape[0]
    lane = _iota((8, t), 1)
    row_id = _iota((8, t), 0)
    log_fs = [jnp.minimum(c[3], 0.0) - jnp.log1p(jnp.exp(-jnp.abs(c[3]))) for c in chunks]
    cum_f, cum_b = list(log_fs), list(log_fs)
    sh = 1
    while sh < t:
        cum_f = [x + jnp.where(lane >= sh, pltpu.roll(x, sh, 1), 0.0) for x in cum_f]
        cum_b = [x + jnp.where(lane < t - sh, pltpu.roll(x, t - sh, 1), 0.0) for x in cum_b]
        sh *= 2
    pad = jnp.zeros((t - 8, t), F32)
    tiles = [jnp.concatenate([jnp.where(row_id % 2 == 0, c[3], jnp.where(row_id == 1, f, b)), pad], axis=0)
             for c, f, b in zip(chunks, cum_f, cum_b)]
    cols = [x.T for x in tiles]
    problems = []
    for c, f, b, col in zip(chunks, cum_f, cum_b, cols):
        for d in range(2):
            b_row = (f, b)[d][2 * d + 1:2 * d + 2]
            problems.append(dict(q=c[0], k=c[1], v_ext=c[2], d=d, ig_row=c[3][2 * d:2 * d + 1], b_row=b_row,
                                 ig_col=col[:, 2 * d:2 * d + 1], b_col=col[:, 2 * d + 1:2 * d + 2]))
    logws = [jnp.where(causal[p["d"]], p["b_col"] + (p["ig_row"] - p["b_row"]), -jnp.inf) for p in problems]
    mus = [jnp.max(x, axis=-1, keepdims=True) for x in logws]
    ws = [jnp.exp(x - mu) for x, mu in zip(logws, mus)]
    lasts = [0 if p["d"] == 1 else t - 1 for p in problems]
    b_lasts = [p["b_col"][i:i + 1] for p, i in zip(problems, lasts)]
    gammas = [mu[i:i + 1] for mu, i in zip(mus, lasts)]
    gks = [jnp.exp(bl - p["b_col"] + p["ig_col"] - gm) * p["k"] for p, bl, gm in zip(problems, b_lasts, gammas)]
    qks = [_bdot_nt(p["q"], p["k"]) * w for p, w in zip(problems, ws)]
    intras = [_bdot(qk, p["v_ext"]) for qk, p in zip(qks, problems)]
    kvs = [_bdot_tn(gk, p["v_ext"]) for gk, p in zip(gks, problems)]
    dh = chunks[0][0].shape[1]
    return [(intra, kv, jnp.broadcast_to(mu - p["b_col"], (t, dh)), jnp.broadcast_to(mu, (t, dh)), bl, gm)
            for intra, kv, p, mu, bl, gm in zip(intras, kvs, problems, mus, b_lasts, gammas)]


def _mlstm_kernel(q_ref, k_ref, v_ref, o_ref, z_ref, gt_ref, cwq_ref, cwk_ref, ng_ref, out_ref,
                  qa_scr, ka_scr, h_scr, intra_scr, kv_scr, delta_scr, mu_scr, tail_scr, *, n_ctx):
    s = q_ref.shape[1]
    dh = q_ref.shape[2]
    t = MIX_CHUNK
    n_chunks = s // t
    n_ctx_chunks = n_ctx // t
    k_scale = dh ** -0.5

    p_rows = MIX_PREP_ROWS

    def prep(j, carry):
        t0 = pl.multiple_of(j * p_rows, p_rows)
        for src, cw, dst, scale in ((q_ref, cwq_ref, qa_scr, 1.0), (k_ref, cwk_ref, ka_scr, k_scale)):
            cur, down, up = _chunk_with_neighbors(src, t0, p_rows, s, s - n_ctx)
            conv = down * cw[0:1, :] + cur * cw[1:2, :] + up * cw[2:3, :]
            dst[pl.ds(t0, p_rows), :] = _silu(conv) * scale
        return carry

    lax.fori_loop(0, s // p_rows, prep, 0)

    ones_col = jnp.ones((t, dh), F32)
    causal = [_iota((t, t), 1) <= _iota((t, t), 0), _iota((t, t), 1) >= _iota((t, t), 0)]

    def operators(gi, carry):
        chunks, where = [], []
        for kk in range(MLSTM_GROUP):
            chunk = gi * MLSTM_GROUP + kk
            sl = pl.ds(pl.multiple_of(chunk * t, t), t)
            v_ext = jnp.concatenate([v_ref[0, sl, :].astype(F32), ones_col], axis=1)
            chunks.append((qa_scr[sl, :], ka_scr[sl, :], v_ext, gt_ref[0, 0, :, sl]))
            where += [(0, chunk, sl), (1, chunk, sl)]
        for (d, chunk, sl), (intra, kv, delta, mu, b_last, gamma) in zip(where, _mlstm_chunk_operators(chunks, causal)):
            intra_scr[d, sl, :] = intra
            kv_scr[d, chunk] = kv
            delta_scr[d, sl, :] = delta
            mu_scr[d, sl, :] = mu
            tail_scr[d, chunk] = jnp.concatenate([jnp.broadcast_to(b_last, (1, dh)), jnp.broadcast_to(gamma, (1, dh))],
                                                 axis=0)
        return carry

    lax.fori_loop(0, n_chunks // MLSTM_GROUP, operators, 0)

    def scan(i, carry):
        chunks = _scan_chunks(i, n_chunks, n_ctx_chunks)
        sls = [pl.ds(pl.multiple_of(c * t, t), t) for c in chunks]
        inters = [_bdot(qa_scr[sl, :], c_ext) for sl, (c_ext, _) in zip(sls, carry)]
        new = []
        for d, (chunk, sl, inter, (c_ext, m)) in enumerate(zip(chunks, sls, inters, carry)):
            tail = tail_scr[d, chunk]
            b_last, gamma = tail[0:1, 0:1], tail[1:2, 0:1]
            z = delta_scr[d, sl, :] - m
            s_inter = jnp.exp(-jnp.maximum(z, 0.0))
            s_intra = jnp.exp(jnp.minimum(z, 0.0))
            floor = jnp.exp(jnp.minimum(z, 0.0) - mu_scr[d, sl, :])
            intra = intra_scr[d, sl, :]
            num = s_inter * inter[:, :dh] + s_intra * intra[:, :dh]
            den = s_inter * inter[:, dh:] + s_intra * intra[:, dh:]
            h_scr[d, sl, :] = num / jnp.maximum(jnp.abs(den), floor)
            m_new = jnp.maximum(b_last + m, gamma)
            new.append((jnp.exp(b_last + m - m_new) * c_ext + jnp.exp(gamma - m_new) * kv_scr[d, chunk], m_new))
        return tuple(new)

    zero = (jnp.zeros((dh, 2 * dh), F32), jnp.zeros((1, 1), F32))
    lax.fori_loop(0, n_chunks, scan, (zero, zero))

    f_rows = MIX_FINISH_ROWS

    def finish(j, carry):
        sl = pl.ds(pl.multiple_of(j * f_rows, f_rows), f_rows)
        h = h_scr[0, sl, :] + h_scr[1, sl, :]
        y = h * lax.rsqrt(jnp.mean(h * h, axis=-1, keepdims=True) + RMS_EPS) * ng_ref[...]
        out_ref[0, sl, :] = y * _sigmoid(o_ref[0, sl, :].astype(F32)) * _silu(z_ref[0, sl, :].astype(F32))
        return carry

    lax.fori_loop(0, s // f_rows, finish, 0)


def _mlstm(u, gt, conv_w, norm_g, n_ctx):
    bsz, s, _ = u.shape
    nh = MLSTM_N_HEADS
    dh = LANES
    width = nh * dh
    n_chunks = s // MIX_CHUNK
    assert n_chunks % MLSTM_GROUP == 0 and n_ctx % MIX_CHUNK == 0
    col = lambda k: pl.BlockSpec((1, s, dh), lambda b, h, k=k: (b, 0, k * nh + h))
    par = lambda k: pl.BlockSpec((3, dh), lambda b, h, k=k: (0, k * nh + h))
    return pl.pallas_call(
        functools.partial(_mlstm_kernel, n_ctx=n_ctx),
        grid=(bsz, nh),
        in_specs=[col(0), col(1), col(2), col(3), col(4),
                  pl.BlockSpec((1, 1, 8, s), lambda b, h: (b, h, 0, 0)),
                  par(0), par(1),
                  pl.BlockSpec((1, dh), lambda b, h: (0, h))],
        out_specs=pl.BlockSpec((1, s, dh), lambda b, h: (b, 0, h)),
        out_shape=jax.ShapeDtypeStruct((bsz, s, width), F32),
        scratch_shapes=[pltpu.VMEM((s, dh), F32), pltpu.VMEM((s, dh), F32), pltpu.VMEM((2, s, dh), F32),
                        pltpu.VMEM((2, s, 2 * dh), F32), pltpu.VMEM((2, n_chunks, dh, 2 * dh), F32),
                        pltpu.VMEM((2, s, dh), F32), pltpu.VMEM((2, s, dh), F32),
                        pltpu.VMEM((2, n_chunks, 2, dh), F32)],
        compiler_params=_compiler_params(("parallel", "arbitrary")),
        name="mlstm_mixer",
    )(u, u, u, u, u, gt, conv_w, conv_w, norm_g.reshape(1, width))


def _head_stack(x, lane_lo):
    return jnp.concatenate([jnp.where(lane_lo, x, 0.0), jnp.where(lane_lo, 0.0, x)], axis=0)


def _half_rows(x, c, upper):
    start = c if upper else 0
    return jnp.concatenate([x[r + start:r + start + c] for r in range(0, x.shape[0], 2 * c)], axis=0)


def _merge_rows(other, part, c, upper):
    pieces = []
    for k in range(part.shape[0] // c):
        pair = (other[k * c:(k + 1) * c], part[k * c:(k + 1) * c])
        pieces += pair if upper else pair[::-1]
    return jnp.concatenate(pieces, axis=0)


def _spread_rows(part, c, upper):
    return _merge_rows(jnp.zeros_like(part), part, c, upper)


def _rwkv_chunk_operators(problems, consts, eye, lane_lo):
    t, w = problems[0][0].shape
    n2 = 2 * t
    stack = lambda x: _head_stack(x, lane_lo)
    zeros = jnp.zeros((n2, w), F32)
    dirs = [p[6] for p in problems]
    rid = _iota((t, w), 0)
    cums = [p[3] for p in problems]
    sh = 1
    while sh < t:
        cums = [x + (jnp.where(rid < t - sh, pltpu.roll(x, t - sh, 0), 0.0) if d == 1 else
                     jnp.where(rid >= sh, pltpu.roll(x, sh, 0), 0.0)) for x, d in zip(cums, dirs)]
        sh *= 2
    pre = []
    for (r, v, kk, lw, ka, kt, d), cum in zip(problems, cums):
        last = 0 if d == 1 else t - 1
        cum_end = cum[last:last + 1]
        e_inv = jnp.exp(-cum)
        e_end = jnp.exp(cum_end - cum)
        a_s = stack(-kk * jnp.exp(cum - lw))
        r_s = stack(r * jnp.exp(cum))
        pre.append(dict(a_s=a_s, r_s=r_s, vs=stack(v), g=jnp.exp(cum_end),
                        ar=jnp.concatenate([a_s, r_s], axis=0),
                        bk=jnp.concatenate([stack(ka * e_inv), stack(kt * e_inv)], axis=0),
                        bk_end=jnp.concatenate([stack(ka * e_end), stack(kt * e_end)], axis=0)))
    m_alls = [_bdot_nt(q["ar"], q["bk"]) for q in pre]
    m_abs = [jnp.where(consts[d]["strict"], m[:n2, :n2], 0.0) for m, d in zip(m_alls, dirs)]
    m_aks = [jnp.where(consts[d]["strict"], m[:n2, n2:], 0.0) for m, d in zip(m_alls, dirs)]
    m_lows = [jnp.where(consts[d]["incl2"], m[n2:, :], 0.0) for m, d in zip(m_alls, dirs)]
    invs = [eye + jnp.where(consts[d]["merges"][0][1], m, 0.0) for m, d in zip(m_abs, dirs)]
    for level in range(1, len(consts[0]["merges"])):
        c = consts[0]["merges"][level][0]
        if c < SUBLANES:
            inner = [_bdot(jnp.where(consts[d]["merges"][level][1], m, 0.0), x) for m, x, d in zip(m_abs, invs, dirs)]
            invs = [x + _bdot(x, y) for x, y in zip(invs, inner)]
        else:
            ups = [d == 0 for d in dirs]
            c_rows = [jnp.where(consts[d]["merges"][level][1], _half_rows(m, c, up), 0.0) for m, d, up in zip(m_abs, dirs, ups)]
            inner = [_bdot(cr, x) for cr, x in zip(c_rows, invs)]
            x_rows = [_half_rows(x, c, up) for x, up in zip(invs, ups)]
            upd = [xr + _bdot(xr, _spread_rows(y, c, up)) for xr, y, up in zip(x_rows, inner, ups)]
            invs = [_merge_rows(_half_rows(x, c, not up), u, c, up) for x, u, up in zip(invs, upd, ups)]
    mv = [_bdot(m, q["vs"]) for m, q in zip(m_aks, pre)]
    solved = [_bdot(x, jnp.concatenate([q["a_s"], y], axis=1)) for x, q, y in zip(invs, pre, mv)]
    zms = [jnp.concatenate([sv, jnp.concatenate([zeros, q["vs"]], axis=1)], axis=0) for sv, q in zip(solved, pre)]
    ry1s = [jnp.concatenate([q["r_s"], zeros], axis=1) + _bdot(m, z) for q, m, z in zip(pre, m_lows, zms)]
    pqs = [_bdot_tn(z, q["bk_end"]) for z, q in zip(zms, pre)]
    out = []
    for ry1, pq, q in zip(ry1s, pqs, pre):
        folded = ry1[:t] + ry1[t:]
        out.append((folded[:, :w], folded[:, w:], pq[:w], pq[w:], q["g"]))
    return out


def _rwkv_kernel(rr_ref, rk_ref, rv_ref, rz_ref, wd_ref, ad_ref, mu_ref, kk_ref, ka_ref, rkk_ref,
                 lnw_ref, lnb_ref, w0_ref, a0_ref, wup_ref, aup_ref, out_ref,
                 r_scr, v_scr, kk_scr, lw_scr, ka_scr, kt_scr, bonus_scr, y_scr, ry_scr, pt_scr, qt_scr, g_scr,
                 *, n_ctx):
    s = rr_ref.shape[1]
    w = rr_ref.shape[2]
    p_rows = RWKV_PREP_ROWS
    t = RWKV_CHUNK
    n_chunks = s // t
    n_ctx_chunks = n_ctx // t
    head_sum = ((_iota((w, w), 0) // RWKV_HEAD) == (_iota((w, w), 1) // RWKV_HEAD)).astype(BF16)
    inv_head = 1.0 / RWKV_HEAD

    def prep(j, carry):
        t0 = pl.multiple_of(j * p_rows, p_rows)
        mixed = []
        for idx, src in enumerate((rr_ref, rk_ref, rv_ref)):
            cur, down, up = _chunk_with_neighbors(src, t0, p_rows, s, s - n_ctx)
            mixed.append(cur + mu_ref[idx:idx + 1, :] * (0.5 * (down + up) - cur))
        r, kr, v = mixed
        kk = kr * kk_ref[...]
        norm = jnp.sqrt(_dot_sel(kk * kk, head_sum))
        kk = kk / jnp.maximum(norm, 1e-12)
        w_raw = _bdot(jnp.tanh(wd_ref[0, pl.ds(t0, p_rows), :].astype(F32)), wup_ref[0]) + w0_ref[0]
        a = _sigmoid(_bdot(ad_ref[0, pl.ds(t0, p_rows), :], aup_ref[0]) + a0_ref[0])
        lw = -math.exp(-0.5) * _sigmoid(w_raw)
        kt_sum = jnp.zeros_like(kr)
        for d in range(2):
            a_d = a[:, d * w:(d + 1) * w]
            kt_d = kr * (1.0 + (a_d - 1.0) * ka_ref[...])
            kt_sum = kt_sum + kt_d
            lw_scr[d, pl.ds(t0, p_rows), :] = lw[:, d * w:(d + 1) * w]
            ka_scr[d, pl.ds(t0, p_rows), :] = kk * a_d
            kt_scr[d, pl.ds(t0, p_rows), :] = kt_d
        coef = _dot_sel(r * kt_sum * rkk_ref[...], head_sum)
        r_scr[pl.ds(t0, p_rows), :] = r
        v_scr[pl.ds(t0, p_rows), :] = v
        kk_scr[pl.ds(t0, p_rows), :] = kk
        bonus_scr[pl.ds(t0, p_rows), :] = coef * v
        return carry

    lax.fori_loop(0, s // p_rows, prep, 0)

    n2 = 2 * t
    r_i = _iota((n2, n2), 0)
    c_i = _iota((n2, n2), 1)
    same = (r_i // t) == (c_i // t)
    rt = r_i % t
    ct = c_i % t
    eye = (r_i == c_i).astype(F32)
    lane_lo = _iota((t, w), 1) < RWKV_HEAD
    consts = []
    for reverse in (False, True):
        strict = jnp.logical_and(same, (ct > rt) if reverse else (ct < rt))
        incl = jnp.logical_and(same, (ct >= rt) if reverse else (ct <= rt))
        merges = []
        c = 1
        while c < t:
            hi_r = (r_i % (2 * c)) >= c
            hi_c = (c_i % (2 * c)) >= c
            cross = jnp.logical_and(hi_c, jnp.logical_not(hi_r)) if reverse else jnp.logical_and(hi_r, jnp.logical_not(hi_c))
            mask = jnp.logical_and((r_i // (2 * c)) == (c_i // (2 * c)), cross)
            merges.append((c, _half_rows(mask, c, not reverse)) if c >= SUBLANES else (c, mask))
            c *= 2
        consts.append(dict(strict=strict, incl2=jnp.concatenate([incl, incl], axis=1), merges=merges))

    def operators(gi, carry):
        problems, where = [], []
        for k in range(RWKV_GROUP):
            chunk = gi * RWKV_GROUP + k
            sl = pl.ds(pl.multiple_of(chunk * t, t), t)
            r, v, kk = r_scr[sl, :], v_scr[sl, :], kk_scr[sl, :]
            for d in range(2):
                problems.append((r, v, kk, lw_scr[d, sl, :], ka_scr[d, sl, :], kt_scr[d, sl, :], d))
                where.append((d, chunk, sl))
        for (d, chunk, sl), (ry, y1, pt, qt, g) in zip(where, _rwkv_chunk_operators(problems, consts, eye, lane_lo)):
            ry_scr[d, sl, :] = ry.astype(BF16)
            y_scr[d, sl, :] = y1
            pt_scr[d, chunk] = pt.astype(BF16)
            qt_scr[d, chunk] = qt
            g_scr[d, chunk] = g
        return carry

    lax.fori_loop(0, n_chunks // RWKV_GROUP, operators, 0)

    def scan(i, carry):
        states = []
        for d, (chunk, ht) in enumerate(zip(_scan_chunks(i, n_chunks, n_ctx_chunks), carry)):
            sl = pl.ds(pl.multiple_of(chunk * t, t), t)
            y_scr[d, sl, :] = _bdot_nt(ry_scr[d, sl, :], ht) + y_scr[d, sl, :]
            states.append(ht * g_scr[d, chunk] + _bdot(ht, pt_scr[d, chunk]) + qt_scr[d, chunk])
        return tuple(states)

    zero_state = jnp.zeros((w, w), F32)
    lax.fori_loop(0, n_chunks, scan, (zero_state, zero_state))

    f_rows = RWKV_FINISH_ROWS

    def finish(j, carry):
        sl = pl.ds(pl.multiple_of(j * f_rows, f_rows), f_rows)
        y = y_scr[0, sl, :] + y_scr[1, sl, :] + bonus_scr[sl, :]
        mu = _dot_sel(y, head_sum) * inv_head
        yc = y - mu
        var = _dot_sel(yc * yc, head_sum) * inv_head
        yn = yc * lax.rsqrt(var + RWKV_LN_EPSILON) * lnw_ref[...] + lnb_ref[...]
        out_ref[0, sl, :] = yn * _silu(rz_ref[0, sl, :].astype(F32))
        return carry

    lax.fori_loop(0, s // f_rows, finish, 0)


def _rwkv(u, col0, p, n_ctx):
    bsz, s, _ = u.shape
    w = LANES
    width = p["mu"].shape[1]
    n_pairs = width // w
    base = col0 // w
    col = lambda k: pl.BlockSpec((1, s, w), lambda b, h, k=k: (b, 0, base + k * n_pairs + h))
    lora = lambda k: pl.BlockSpec((1, s, w), lambda b, h, k=k: (b, 0, base + 4 * n_pairs + k))
    vec = lambda rows: pl.BlockSpec((rows, w), lambda b, h: (0, h))
    cat = pl.BlockSpec((1, 1, 2 * w), lambda b, h: (h, 0, 0))
    up = pl.BlockSpec((1, w, 2 * w), lambda b, h: (h, 0, 0))
    seq = pltpu.VMEM((s, w), F32)
    seq2 = pltpu.VMEM((2, s, w), F32)
    n_chunks = s // RWKV_CHUNK
    assert n_chunks % RWKV_GROUP == 0 and s % RWKV_PREP_ROWS == 0 and n_ctx % RWKV_PREP_ROWS == 0
    assert s % RWKV_FINISH_ROWS == 0
    operators = [pltpu.VMEM((2, s, w), BF16), pltpu.VMEM((2, n_chunks, w, w), BF16),
                 pltpu.VMEM((2, n_chunks, w, w), F32), pltpu.VMEM((2, n_chunks, 1, w), F32)]
    return pl.pallas_call(
        functools.partial(_rwkv_kernel, n_ctx=n_ctx),
        grid=(bsz, n_pairs),
        in_specs=[col(0), col(1), col(2), col(3), lora(0), lora(1),
                  vec(3), vec(1), vec(1), vec(1), vec(1), vec(1), cat, cat, up, up],
        out_specs=pl.BlockSpec((1, s, w), lambda b, h: (b, 0, h)),
        out_shape=jax.ShapeDtypeStruct((bsz, s, width), F32),
        scratch_shapes=[seq, seq, seq, seq2, seq2, seq2, seq, seq2] + operators,
        compiler_params=_compiler_params(("parallel", "arbitrary")),
        name="rwkv7_mixer",
    )(u, u, u, u, u, u, p["mu"], p["k_k"], p["k_a"], p["r_k"], p["ln_w"], p["ln_b"],
      p["w0"], p["a0"], p["w_up"], p["a_up"])


def _hgrn_level_masks(t, w):
    rid = _iota((t, w), 0)
    r_i = _iota((t, t), 0)
    c_i = _iota((t, t), 1)
    levels = []
    c = 1
    while c < t:
        same_block = (r_i // (2 * c)) == (c_i // (2 * c))
        up_r = (r_i % (2 * c)) >= c
        up_c = (c_i % (2 * c)) >= c
        pair = [jnp.logical_and(same_block, jnp.logical_and(up_r, jnp.logical_not(up_c))),
                jnp.logical_and(same_block, jnp.logical_and(up_c, jnp.logical_not(up_r)))]
        levels.append((c, (rid % (2 * c)) >= c, pair))
        c *= 2
    return levels


def _hgrn_chunk_operators(problems, lb, tris, levels):
    t, w = problems[0][0].shape
    r_i = _iota((t, t), 0)
    c_i = _iota((t, t), 1)
    zero_row = jnp.zeros((1, w), F32)
    dirs = [p[3] for p in problems]
    lgs, ks = [], []
    for q, v, ff, d in problems:
        e = jnp.exp(-jnp.abs(ff))
        big = 1.0 / (1.0 + e)
        small = e / (1.0 + e)
        pos = ff >= 0.0
        lgs.append(jnp.log(lb + (1.0 - lb) * jnp.where(pos, big, small)) * LOG2_E)
        ks.append((1.0 - lb) * jnp.where(pos, small, big))
    bs = [_sel_dot(tris[d], lg) for lg, d in zip(lgs, dirs)]
    befores = [_shifted(b, zero_row, zero_row)[1 if d == 1 else 0] for b, d in zip(bs, dirs)]
    edges = list(bs)
    accs = [jnp.where(r_i == c_i, jnp.sum(p[0] * k, axis=-1, keepdims=True), 0.0) for p, k in zip(problems, ks)]
    for c, upper, pair in levels:
        qts = [p[0] * jnp.exp2(b - before) for p, b, before in zip(problems, bs, befores)]
        kts = [k * jnp.exp2(edge - b) for k, b, edge in zip(ks, bs, edges)]
        prods = [_bdot_nt(qt, kt) for qt, kt in zip(qts, kts)]
        accs = [a + jnp.where(pair[d], pr, 0.0) for a, pr, d in zip(accs, prods, dirs)]
        for i, d in enumerate(dirs):
            if d == 1:
                befores[i] = jnp.where(upper, befores[i], pltpu.roll(befores[i], t - c, 0))
                edges[i] = jnp.where(upper, pltpu.roll(edges[i], c, 0), edges[i])
            else:
                befores[i] = jnp.where(upper, pltpu.roll(befores[i], c, 0), befores[i])
                edges[i] = jnp.where(upper, edges[i], pltpu.roll(edges[i], t - c, 0))
    o_intras = [_bdot(a, p[1]) for a, p in zip(accs, problems)]
    b_ends = [b[(0 if d == 1 else t - 1):(1 if d == 1 else t)] for b, d in zip(bs, dirs)]
    kvs = [_bdot_tn(p[1], k * jnp.exp2(be - b)) for p, k, b, be in zip(problems, ks, bs, b_ends)]
    return [(p[0] * jnp.exp2(b), oi, kv, jnp.exp2(be)) for p, b, oi, kv, be in zip(problems, bs, o_intras, kvs, b_ends)]


def _hgrn_kernel(q_ref, i_ref, ff_ref, fb_ref, z_ref, lb_ref, ng_ref, out_ref, o_scr, qe_scr, kv_scr, g_scr,
                 *, n_ctx, layer):
    s = q_ref.shape[1]
    dh = q_ref.shape[2]
    t = MIX_CHUNK
    n_chunks = s // t
    n_ctx_chunks = n_ctx // t
    lbs = lb_ref[...]
    ex = jnp.exp(lbs - jnp.max(lbs, axis=0, keepdims=True))
    probs = ex / jnp.sum(ex, axis=0, keepdims=True)
    csum = probs[0:1]
    for l in range(1, layer + 1):
        csum = csum + probs[l:l + 1]
    lb = csum - probs[0:1]
    tri_r = _iota((t, t), 0)
    tri_c = _iota((t, t), 1)
    tris = [(tri_c <= tri_r).astype(BF16), (tri_c >= tri_r).astype(BF16)]
    levels = _hgrn_level_masks(t, dh)

    def operators(gi, carry):
        problems, where = [], []
        for kk in range(HGRN_GROUP):
            chunk = gi * HGRN_GROUP + kk
            sl = pl.ds(pl.multiple_of(chunk * t, t), t)
            q, v = q_ref[0, sl, :].astype(F32), i_ref[0, sl, :].astype(F32)
            for d, f_ref in enumerate((ff_ref, fb_ref)):
                problems.append((q, v, f_ref[0, sl, :].astype(F32), d))
                where.append((d, chunk, sl))
        for (d, chunk, sl), (qe, o_intra, kv, g) in zip(where, _hgrn_chunk_operators(problems, lb, tris, levels)):
            qe_scr[d, sl, :] = qe.astype(BF16)
            o_scr[d, sl, :] = o_intra
            kv_scr[d, chunk] = kv
            g_scr[d, chunk] = g
        return carry

    lax.fori_loop(0, n_chunks // HGRN_GROUP, operators, 0)

    def scan(i, carry):
        chunks = _scan_chunks(i, n_chunks, n_ctx_chunks)
        sls = [pl.ds(pl.multiple_of(c * t, t), t) for c in chunks]
        inters = [_bdot_nt(qe_scr[d, sl, :], st) for d, (sl, st) in enumerate(zip(sls, carry))]
        new = []
        for d, (chunk, sl, inter, st) in enumerate(zip(chunks, sls, inters, carry)):
            o_scr[d, sl, :] = o_scr[d, sl, :] + inter
            new.append(st * g_scr[d, chunk] + kv_scr[d, chunk])
        return tuple(new)

    zero_state = jnp.zeros((dh, dh), F32)
    lax.fori_loop(0, n_chunks, scan, (zero_state, zero_state))

    f_rows = MIX_FINISH_ROWS

    def finish(j, carry):
        sl = pl.ds(pl.multiple_of(j * f_rows, f_rows), f_rows)
        o = o_scr[0, sl, :] + o_scr[1, sl, :]
        y = o * lax.rsqrt(jnp.mean(o * o, axis=-1, keepdims=True) + RMS_EPS) * ng_ref[...]
        out_ref[0, sl, :] = y * _silu(z_ref[0, sl, :].astype(F32))
        return carry

    lax.fori_loop(0, (s - n_ctx) // f_rows, finish, 0)


def _hgrn(u, lb_all, norm_g, n_ctx, layer):
    bsz, s, _ = u.shape
    nh = HGRN_N_HEADS
    dh = LANES
    width = nh * dh
    depth = lb_all.shape[0]
    n_chunks = s // MIX_CHUNK
    assert n_chunks % HGRN_GROUP == 0 and n_ctx % MIX_CHUNK == 0
    col = lambda k: pl.BlockSpec((1, s, dh), lambda b, h, k=k: (b, 0, k * nh + h))
    return pl.pallas_call(
        functools.partial(_hgrn_kernel, n_ctx=n_ctx, layer=layer),
        grid=(bsz, nh),
        in_specs=[col(0), col(1), col(2), col(3), col(4),
                  pl.BlockSpec((depth, dh), lambda b, h: (0, h)),
                  pl.BlockSpec((1, dh), lambda b, h: (0, h))],
        out_specs=pl.BlockSpec((1, s - n_ctx, dh), lambda b, h: (b, 0, h)),
        out_shape=jax.ShapeDtypeStruct((bsz, s - n_ctx, width), F32),
        scratch_shapes=[pltpu.VMEM((2, s, dh), F32), pltpu.VMEM((2, s, dh), BF16),
                        pltpu.VMEM((2, n_chunks, dh, dh), F32), pltpu.VMEM((2, n_chunks, 1, dh), F32)],
        compiler_params=_compiler_params(("parallel", "arbitrary")),
        name="hgrn2_mixer",
    )(u, u, u, u, u, lb_all, norm_g.reshape(1, width))


def _hyena_filter_kernel(z_ref, w1_ref, b1_ref, w2_ref, b2_ref, w3f_ref, w3b_ref, dl_ref, hf_ref, hb_ref):
    hp = functools.partial(jnp.dot, precision=lax.Precision.HIGHEST, preferred_element_type=F32)
    n = z_ref.shape[0]
    hid = jnp.sin(hp(z_ref[...], w1_ref[...]) + b1_ref[...])
    hid = jnp.sin(hp(hid, w2_ref[...]) + b2_ref[...])
    pos = _iota((n, 1), 0).astype(F32) * (1.0 / n)
    window = jnp.exp(-pos * dl_ref[...]) + HYENA_SHIFT
    f0 = hp(hid, w3f_ref[...]) * window
    f1 = hp(hid, w3b_ref[...]) * window
    nrm = jnp.sum(jnp.abs(f0), axis=0, keepdims=True) + jnp.sum(jnp.abs(f1), axis=0, keepdims=True)
    hf_ref[...] = f0 / nrm
    hb_ref[...] = f1 / nrm


def _hyena_filters(n, w1, b1, w2, b2, w3, width):
    pos = np.arange(n, dtype=np.float64)
    bands = np.linspace(1e-4, HYENA_N_BANDS - 1, HYENA_N_BANDS)
    ang = (2.0 * math.pi / n) * pos[:, None] * bands
    z = np.concatenate([(pos / n)[:, None], np.cos(ang), np.sin(ang)], axis=-1)
    z = np.pad(z, ((0, 0), (0, LANES - z.shape[1]))).astype(np.float32)
    max_decay = math.log(HYENA_TGT) / HYENA_FAST
    min_decay = math.log(HYENA_TGT) / HYENA_SLOW
    deltas = np.abs(np.linspace(min_decay, max_decay, width)).astype(np.float32)[None]
    feat, hid = w1.shape
    w1p = jnp.pad(w1, ((0, LANES - feat), (0, LANES - hid)))
    w2p = jnp.pad(w2, ((0, LANES - hid), (0, LANES - hid)))
    w3p = jnp.pad(w3, ((0, LANES - hid), (0, 0)))
    b1p = jnp.pad(b1, (0, LANES - hid)).reshape(1, LANES)
    b2p = jnp.pad(b2, (0, LANES - hid)).reshape(1, LANES)
    n_tiles = width // LANES
    full = lambda shape: pl.BlockSpec(shape, lambda j: (0, 0))
    out = pl.BlockSpec((n, LANES), lambda j: (0, j))
    hf, hb = pl.pallas_call(
        _hyena_filter_kernel,
        grid=(n_tiles,),
        in_specs=[full((n, LANES)), full((LANES, LANES)), full((1, LANES)), full((LANES, LANES)), full((1, LANES)),
                  pl.BlockSpec((LANES, LANES), lambda j: (0, j)),
                  pl.BlockSpec((LANES, LANES), lambda j: (0, n_tiles + j)),
                  pl.BlockSpec((1, LANES), lambda j: (0, j))],
        out_specs=[out, out],
        out_shape=[jax.ShapeDtypeStruct((n, width), F32)] * 2,
        compiler_params=_compiler_params(("arbitrary",)),
        name="hyena_filters",
    )(jnp.asarray(z), w1p, b1p, w2p, b2p, w3p, w3p, jnp.asarray(deltas))
    return jnp.concatenate([hf, hb], axis=1)


def _hyena_pre_kernel(yv_ref, y0_ref, y1_ref, yz_ref, swv_ref, sw0_ref, sw1_ref, sbv_ref, sb0_ref, sb1_ref,
                      yb_ref, p_ref, e_ref, pb_ref, *, n_ctx):
    s = yv_ref.shape[1]
    rows = MIX_CHUNK

    def body(j, carry):
        t0 = pl.multiple_of(j * rows, rows)
        conv = []
        for src, sw, sb in ((yv_ref, swv_ref, sbv_ref), (y0_ref, sw0_ref, sb0_ref), (y1_ref, sw1_ref, sb1_ref)):
            cur, down, up = _chunk_with_neighbors(src, t0, rows, s, s - n_ctx)
            conv.append(down * sw[0:1, :] + cur * sw[1:2, :] + up * sw[2:3, :] + sb[...])
        v, x0, x1 = conv
        p = x1 * v
        o0 = pl.multiple_of(j * rows, rows)
        p_ref[0, pl.ds(o0, rows), :] = p.astype(BF16)
        pb_ref[0, pl.ds(o0, rows), :] = p * yb_ref[...]
        e_ref[0, pl.ds(o0, rows), :] = x0 * _silu(yz_ref[0, pl.ds(t0, rows), :].astype(F32))
        return carry

    lax.fori_loop(0, (s - n_ctx) // rows, body, 0)


def _hyena_pre(u, col0, short_w, short_b, y_bias, n_ctx):
    bsz, s, _ = u.shape
    w = y_bias.shape[0]
    tiles = w // LANES
    base = col0 // LANES
    n = s - n_ctx
    col = lambda k: pl.BlockSpec((1, s, LANES), lambda b, j, k=k: (b, 0, base + k * tiles + j))
    par = lambda rows, k: pl.BlockSpec((rows, LANES), lambda b, j, k=k: (0, k * tiles + j))
    out = pl.BlockSpec((1, n, LANES), lambda b, j: (b, 0, j))
    sb = short_b.reshape(1, 3 * w)
    return pl.pallas_call(
        functools.partial(_hyena_pre_kernel, n_ctx=n_ctx),
        grid=(bsz, tiles),
        in_specs=[col(0), col(1), col(2), col(3),
                  par(3, 0), par(3, 1), par(3, 2), par(1, 0), par(1, 1), par(1, 2), par(1, 0)],
        out_specs=[out, out, out],
        out_shape=[jax.ShapeDtypeStruct((bsz, n, w), BF16),
                   jax.ShapeDtypeStruct((bsz, n, w), F32),
                   jax.ShapeDtypeStruct((bsz, n, w), F32)],
        compiler_params=_compiler_params(("parallel", "arbitrary")),
        name="hyena_short_conv",
    )(u, u, u, u, short_w, short_w, short_w, sb, sb, sb, y_bias.reshape(1, w))


def _dft_tables(n):
    big = 2 * n
    half = DFT_TILE // 2
    idx = jnp.arange(n, dtype=jnp.int32)
    split = DFT_SPLIT
    lo = jnp.arange(split, dtype=jnp.int32)
    hi = jnp.arange(n // split, dtype=jnp.int32)
    ang_lo = ((lo[:, None] * idx[None, :]) % big).astype(F32) * (2.0 * math.pi / big)
    ang_hi = ((hi[:, None] * idx[None, :]) % (big // split)).astype(F32) * (2.0 * math.pi * split / big)
    c_lo, s_lo = jnp.cos(ang_lo)[None], jnp.sin(ang_lo)[None]
    c_hi, s_hi = jnp.cos(ang_hi)[:, None], jnp.sin(ang_hi)[:, None]
    cos = (c_hi * c_lo - s_hi * s_lo).reshape(n, n)
    sin = (s_hi * c_lo + c_hi * s_lo).reshape(n, n)
    alt = jnp.where(idx % 2 == 0, 1.0, -1.0).astype(F32)
    first_row = (idx == 0)[:, None]
    im = jnp.where(first_row, alt[None, :], -sin)
    fwd = jnp.stack([cos.reshape(n // half, half, n), im.reshape(n // half, half, n)], axis=1).reshape(big, n)
    weight = jnp.where(first_row, 1.0, 2.0) * (1.0 / big)
    weight = jnp.stack([weight.reshape(n // half, half, 1)] * 2, axis=1).reshape(big, 1)
    return fwd.astype(BF16), (fwd * weight).T.astype(BF16)


def _spectrum_kernel(f_ref, x_ref, o_ref):
    half = DFT_TILE // 2
    w = o_ref.shape[1]
    acc = jnp.dot(f_ref[...], x_ref[...].astype(BF16), preferred_element_type=F32)
    o_ref[:half, :] = acc[:half, :w] + acc[:half, w:]
    k_im = acc[half:, :w] - acc[half:, w:]
    packed = jnp.logical_and(_iota((half, w), 0) == 0, pl.program_id(0) == 0)
    o_ref[half:, :] = jnp.where(packed, acc[half:, :w] + acc[half:, w:], k_im)


def _filter_spectrum(fwd, hk):
    big, n = fwd.shape
    cols = hk.shape[1]
    return pl.pallas_call(
        _spectrum_kernel,
        grid=(big // DFT_TILE,),
        in_specs=[pl.BlockSpec((DFT_TILE, n), lambda i: (i, 0)),
                  pl.BlockSpec((n, cols), lambda i: (0, 0))],
        out_specs=pl.BlockSpec((DFT_TILE, cols // 2), lambda i: (i, 0)),
        out_shape=jax.ShapeDtypeStruct((big, cols // 2), F32),
        compiler_params=_compiler_params(("arbitrary",)),
        name="hyena_filter_spectrum",
    )(fwd, hk)


def _conv_spectrum_kernel(f_ref, p_ref, ks_ref, z_ref):
    i = pl.program_id(1)
    half = DFT_TILE // 2
    w = p_ref.shape[2]
    units = [slice(k * DFT_TILE, (k + 1) * DFT_TILE) for k in range(DFT_UNITS)]
    accs = [jnp.dot(f_ref[u, :], p_ref[0], preferred_element_type=F32) for u in units]
    for k, (u, acc) in enumerate(zip(units, accs)):
        s_re, s_im = acc[:half], acc[half:]
        k_re, k_im = ks_ref[u, :][:half], ks_ref[u, :][half:]
        z_re = s_re * k_re - s_im * k_im
        z_im = s_re * k_im + s_im * k_re
        if k == 0:
            packed = jnp.logical_and(_iota((half, w), 0) == 0, i == 0)
            z_re = jnp.where(packed, s_re * k_re, z_re)
            z_im = jnp.where(packed, s_im * k_im, z_im)
        z_ref[0, u, :] = jnp.concatenate([z_re, z_im], axis=0).astype(BF16)


def _conv_spectrum(fwd, p16, kspec):
    bsz, n, w = p16.shape
    big = fwd.shape[0]
    rows = DFT_TILE * DFT_UNITS
    return pl.pallas_call(
        _conv_spectrum_kernel,
        grid=(bsz, big // rows),
        in_specs=[pl.BlockSpec((rows, n), lambda b, i: (i, 0)),
                  pl.BlockSpec((1, n, w), lambda b, i: (b, 0, 0)),
                  pl.BlockSpec((rows, w), lambda b, i: (i, 0))],
        out_specs=pl.BlockSpec((1, rows, w), lambda b, i: (b, i, 0)),
        out_shape=jax.ShapeDtypeStruct((bsz, big, w), BF16),
        compiler_params=_compiler_params(("parallel", "arbitrary")),
        name="hyena_forward_dft",
    )(fwd, p16, kspec)


def _conv_inverse_kernel(g_ref, z_ref, e_ref, pb_ref, o_ref):
    y = jnp.dot(g_ref[...], z_ref[0], preferred_element_type=F32)
    o_ref[0] = e_ref[0] * (y + pb_ref[0])


def _conv_inverse(inv, z16, e, pb):
    bsz, big, w = z16.shape
    n = inv.shape[0]
    tile = DFT_TILE
    tok = pl.BlockSpec((1, tile, w), lambda b, i: (b, i, 0))
    return pl.pallas_call(
        _conv_inverse_kernel,
        grid=(bsz, n // tile),
        in_specs=[pl.BlockSpec((tile, big), lambda b, i: (i, 0)),
                  pl.BlockSpec((1, big, w), lambda b, i: (b, 0, 0)),
                  tok, tok],
        out_specs=tok,
        out_shape=jax.ShapeDtypeStruct((bsz, n, w), F32),
        compiler_params=_compiler_params(("parallel", "arbitrary")),
        name="hyena_inverse_dft",
    )(inv, z16, e, pb)


def _even_weight_layout(w_in, gate_b):
    d = w_in.shape[0]
    mw = MLSTM_N_HEADS * LANES
    g0 = 5 * mw
    g1 = g0 + 4 * MLSTM_N_HEADS
    main = jnp.concatenate([w_in[:, :g0], w_in[:, g1:]], axis=1).astype(BF16)
    wg = w_in[:, g0:g1].reshape(d, 2, 2, MLSTM_N_HEADS)
    wg = jnp.transpose(wg, (3, 1, 2, 0)).reshape(MLSTM_N_HEADS, 4, d)
    wg = jnp.concatenate([wg, jnp.zeros_like(wg)], axis=1).reshape(MLSTM_N_HEADS * 8, d).astype(BF16)
    gb = jnp.transpose(gate_b.reshape(2, 2, MLSTM_N_HEADS), (2, 0, 1)).reshape(MLSTM_N_HEADS, 4)
    gb = jnp.concatenate([gb, jnp.zeros_like(gb)], axis=1).reshape(MLSTM_N_HEADS * 8, 1)
    return main, wg, jnp.broadcast_to(gb, (MLSTM_N_HEADS * 8, LANES))


def _rwkv_params(mu, w0, w_up, a0, a_up, k_k, k_a, r_k, ln_w, ln_b):
    width = mu.shape[1]
    n_pairs = width // LANES
    row = lambda x: x.reshape(1, width)

    def cat_dirs(x):
        return jnp.transpose(x.reshape(2, n_pairs, LANES), (1, 0, 2)).reshape(n_pairs, 1, 2 * LANES)

    def block_up(x):
        lora = x.shape[1]
        xp = jnp.transpose(x.reshape(2, lora, n_pairs, LANES), (2, 0, 1, 3))
        z = jnp.zeros_like(xp[:, 0])
        top = jnp.concatenate([xp[:, 0], z], axis=2)
        bot = jnp.concatenate([z, xp[:, 1]], axis=2)
        return jnp.concatenate([top, bot], axis=1).astype(BF16)

    return {"mu": mu, "k_k": row(k_k), "k_a": row(k_a), "r_k": row(r_k), "ln_w": row(ln_w), "ln_b": row(ln_b),
            "w0": cat_dirs(w0), "a0": cat_dirs(a0), "w_up": block_up(w_up), "a_up": block_up(a_up)}


def _raster_to_column(h):
    b, n, d = h.shape
    rows = n // GRID_WIDTH
    return h.reshape(b, rows, GRID_WIDTH, d).transpose(0, 2, 1, 3).reshape(b, n, d)


def _column_to_raster(h):
    b, n, d = h.shape
    rows = n // GRID_WIDTH
    return h.reshape(b, GRID_WIDTH, rows, d).transpose(0, 2, 1, 3).reshape(b, n, d)


def kernel(x, c, ctx, c_ctx, l0_norm_g, l0_mod_w, l0_mod_b, l0_w_in, l0_w_out, l0_mlstm_conv_w, l0_mlstm_gate_b, l0_mlstm_norm_g, l0_rwkv_mu, l0_rwkv_w0, l0_rwkv_w_up, l0_rwkv_a0, l0_rwkv_a_up, l0_rwkv_k_k, l0_rwkv_k_a, l0_rwkv_r_k, l0_rwkv_ln_w, l0_rwkv_ln_b, hgrn_lower_bounds, l1_norm_g, l1_mod_w, l1_mod_b, l1_w_in, l1_w_out, l1_hgrn_norm_g, l1_hyena_short_w, l1_hyena_short_b, l1_hyena_w1, l1_hyena_b1, l1_hyena_w2, l1_hyena_b2, l1_hyena_w3, l1_hyena_bias, final_norm_g):
    bsz, n_lat, d = x.shape
    n_ctx = ctx.shape[1]

    pad = (-(bsz + 1)) % 8
    cc = jnp.concatenate([c, c_ctx[None], jnp.zeros((pad, d), F32)], axis=0)
    mod0, mod1 = _modulation(cc, l0_mod_w, l0_mod_b, l1_mod_w, l1_mod_b)
    mod0 = mod0[:bsz + 1].reshape(bsz + 1, 3, d)
    mod1 = mod1[:bsz + 1].reshape(bsz + 1, 3, d)

    w_main, w_gate, b_gate = _even_weight_layout(l0_w_in, l0_mlstm_gate_b)
    n0 = w_main.shape[1]
    u0, gt0 = _proj_in(x, ctx, l0_norm_g, mod0, w_main, n0 // 2, w_gate, b_gate)
    gt0 = gt0.reshape(bsz, MLSTM_N_HEADS, 8, n_ctx + n_lat)
    y_m = _mlstm(u0, gt0, l0_mlstm_conv_w, l0_mlstm_norm_g, n_ctx)
    rp = _rwkv_params(l0_rwkv_mu, l0_rwkv_w0, l0_rwkv_w_up, l0_rwkv_a0, l0_rwkv_a_up, l0_rwkv_k_k,
                      l0_rwkv_k_a, l0_rwkv_r_k, l0_rwkv_ln_w, l0_rwkv_ln_b)
    y_r = _rwkv(u0, 5 * MLSTM_N_HEADS * LANES, rp, n_ctx)
    x1, ctx1 = _proj_out(y_m, y_r, x, ctx, mod0, l0_w_out.astype(BF16))

    x1c = _raster_to_column(x1)
    w1 = l1_w_in.astype(BF16)
    (u1,) = _proj_in(x1c, ctx1, l1_norm_g, mod1, w1, w1.shape[1] // 2)
    y_g = _hgrn(u1, hgrn_lower_bounds, l1_hgrn_norm_g, n_ctx, layer=1)
    hw = l1_hyena_bias.shape[0]
    hk = _hyena_filters(n_lat, l1_hyena_w1, l1_hyena_b1, l1_hyena_w2, l1_hyena_b2, l1_hyena_w3, hw)
    fwd, inv = _dft_tables(n_lat)
    kspec = _filter_spectrum(fwd, hk)
    p16, e, pb = _hyena_pre(u1, 5 * HGRN_N_HEADS * LANES, l1_hyena_short_w, l1_hyena_short_b, l1_hyena_bias, n_ctx)
    z16 = _conv_spectrum(fwd, p16, kspec)
    y_y = _conv_inverse(inv, z16, e, pb)
    out_c = _proj_out_final(y_g, y_y, x1c, mod1, l1_w_out.astype(BF16), final_norm_g)
    return _column_to_raster(out_c)
```

```python
import functools
import math

import jax
import jax.numpy as jnp
import numpy as np
from jax import lax
from jax.experimental import pallas as pl
from jax.experimental.pallas import tpu as pltpu

F32 = jnp.float32
BF16 = jnp.bfloat16

GRID_WIDTH = 64
RMS_EPS = 1e-6
MLSTM_N_HEADS = 4
RWKV_HEAD = 64
RWKV_LN_EPSILON = 64e-5
HGRN_N_HEADS = 4
HYENA_N_BANDS = 16
HYENA_FAST = 0.3
HYENA_SLOW = 1.5
HYENA_TGT = 1e-2
HYENA_SHIFT = 0.05
LOG2_E = 1.0 / math.log(2.0)

LANES = 128
SUBLANES = 8
ROW_GROUP = 16
MXU_DIM = 256
VMEM_LIMIT = 52 * 1024 * 1024

MIX_CHUNK = 128
RWKV_CHUNK = 64
RWKV_GROUP = 9
MIX_PREP_ROWS = 256
MIX_FINISH_ROWS = 256
RWKV_PREP_ROWS = 256
RWKV_FINISH_ROWS = 768
SCAN_UNROLL = 3
MLSTM_GROUP = 6
HGRN_GROUP = 3
PROJ_ROWS = 768
FINAL_ROWS = 512
DFT_TILE = 512
DFT_UNITS = 2
DFT_SPLIT = 64


def _bdot(a, b):
    return jnp.dot(a.astype(BF16), b.astype(BF16), preferred_element_type=F32)


def _bdot_nt(a, b):
    return lax.dot_general(a.astype(BF16), b.astype(BF16), (((1,), (1,)), ((), ())),
                           preferred_element_type=F32)


def _bdot_tn(a, b):
    return lax.dot_general(a.astype(BF16), b.astype(BF16), (((0,), (0,)), ((), ())),
                           preferred_element_type=F32)


def _split3(x):
    hi = x.astype(BF16)
    r1 = x - hi.astype(F32)
    mid = r1.astype(BF16)
    lo = (r1 - mid.astype(F32)).astype(BF16)
    return hi, mid, lo


def _sel_dot(sel, x):
    hi, mid, lo = _split3(x)
    d = functools.partial(jnp.dot, preferred_element_type=F32)
    return d(sel, hi) + d(sel, mid) + d(sel, lo)


def _dot_sel(x, sel):
    hi, mid, lo = _split3(x)
    d = functools.partial(jnp.dot, preferred_element_type=F32)
    return d(hi, sel) + d(mid, sel) + d(lo, sel)


def _sigmoid(x):
    return 1.0 / (1.0 + jnp.exp(-x))


def _silu(x):
    return x * _sigmoid(x)


def _iota(shape, dim):
    return lax.broadcasted_iota(jnp.int32, shape, dim)


def _neighbor_rows(ref, t0, rows, n_total, split):
    has_prev = jnp.logical_and(t0 != 0, t0 != split)
    has_next = jnp.logical_and(t0 + rows != split, t0 + rows != n_total)
    g = ROW_GROUP
    before = ref[0, pl.ds(pl.multiple_of(jnp.maximum(t0 - g, 0), g), g), :].astype(F32)
    after = ref[0, pl.ds(pl.multiple_of(jnp.minimum(t0 + rows, n_total - g), g), g), :].astype(F32)
    return jnp.where(has_prev, before[g - 1:g], 0.0), jnp.where(has_next, after[0:1], 0.0)


def _shifted(cur, prev_row, next_row):
    rows = cur.shape[0]
    rid = _iota(cur.shape, 0)
    down = jnp.where(rid == 0, prev_row, pltpu.roll(cur, 1, 0))
    up = jnp.where(rid == rows - 1, next_row, pltpu.roll(cur, rows - 1, 0))
    return down, up


def _chunk_with_neighbors(ref, t0, rows, n_total, split):
    cur = ref[0, pl.ds(t0, rows), :].astype(F32)
    prev_row, next_row = _neighbor_rows(ref, t0, rows, n_total, split)
    down, up = _shifted(cur, prev_row, next_row)
    return cur, down, up


def _scan_chunks(i, n_chunks, n_ctx_chunks):
    fwd = jnp.where(i < n_ctx_chunks, n_chunks - n_ctx_chunks + i, i - n_ctx_chunks)
    return fwd, n_chunks - 1 - i


def _compiler_params(semantics):
    return pltpu.CompilerParams(dimension_semantics=semantics, vmem_limit_bytes=VMEM_LIMIT)


def _mod_kernel(c_ref, w0_ref, b0_ref, w1_ref, b1_ref, o0_ref, o1_ref):
    s = _silu(c_ref[...])
    o0_ref[...] = _bdot(s, w0_ref[...]) + b0_ref[...]
    o1_ref[...] = _bdot(s, w1_ref[...]) + b1_ref[...]


def _modulation(cc, w0, b0, w1, b1):
    rows, d = cc.shape
    n = w0.shape[1]
    tile = d
    grid = (n // tile,)
    wspec = pl.BlockSpec((d, tile), lambda j: (0, j))
    bspec = pl.BlockSpec((1, tile), lambda j: (0, j))
    ospec = pl.BlockSpec((rows, tile), lambda j: (0, j))
    return pl.pallas_call(
        _mod_kernel,
        grid=grid,
        in_specs=[pl.BlockSpec((rows, d), lambda j: (0, 0)), wspec, bspec, wspec, bspec],
        out_specs=[ospec, ospec],
        out_shape=[jax.ShapeDtypeStruct((rows, n), F32)] * 2,
        compiler_params=_compiler_params(("arbitrary",)),
        name="adaln_modulation",
    )(cc, w0, b0.reshape(1, n), w1, b1.reshape(1, n))


def _token_tile(x_ref, c_ref, i, rows):
    n_lat_tail = rows - c_ref.shape[1]
    is_ctx = jnp.logical_and(i == pl.num_programs(1) - 1, _iota((rows, 1), 0) >= n_lat_tail)
    ctx_rows = jnp.concatenate([jnp.zeros((n_lat_tail, c_ref.shape[2]), F32), c_ref[0]], axis=0)
    return jnp.where(is_ctx, ctx_rows, x_ref[0]), is_ctx


def _proj_in_kernel(*refs, rows, with_gates):
    if with_gates:
        x_ref, c_ref, g_ref, ml_ref, mc_ref, w_ref, wg_ref, gb_ref, u_ref, gt_ref, h_scr = refs
    else:
        x_ref, c_ref, g_ref, ml_ref, mc_ref, w_ref, u_ref, h_scr = refs
    i = pl.program_id(1)
    n = pl.program_id(2)

    @pl.when(n == 0)
    def _():
        x, is_ctx = _token_tile(x_ref, c_ref, i, rows)
        y = x * lax.rsqrt(jnp.mean(x * x, axis=-1, keepdims=True) + RMS_EPS) * g_ref[...]
        ml = ml_ref[0]
        mc = mc_ref[0]
        shift = jnp.where(is_ctx, mc[0:1], ml[0:1])
        scale = jnp.where(is_ctx, mc[1:2], ml[1:2])
        h = (y * (1.0 + scale) + shift).astype(BF16)
        h_scr[...] = h
        if with_gates:
            gt_ref[0] = _bdot_nt(wg_ref[...], h) + gb_ref[:, 0:1]

    u_ref[0] = jnp.dot(h_scr[...], w_ref[...], preferred_element_type=F32).astype(u_ref.dtype)


def _proj_in(x, ctx, norm_g, mod3, w16, n_tile, gate_w=None, gate_b=None):
    bsz, n_lat, d = x.shape
    n_ctx = ctx.shape[1]
    s = n_lat + n_ctx
    n = w16.shape[1]
    rows = PROJ_ROWS
    assert s % rows == 0 and (n_lat % rows) + n_ctx == rows
    grid = (bsz, s // rows, n // n_tile)
    ctx_row = mod3.shape[0] - 1
    with_gates = gate_w is not None
    in_specs = [
        pl.BlockSpec((1, rows, d), lambda b, i, j: (b, i, 0)),
        pl.BlockSpec((1, n_ctx, d), lambda b, i, j: (b, 0, 0)),
        pl.BlockSpec((1, d), lambda b, i, j: (0, 0)),
        pl.BlockSpec((1, 3, d), lambda b, i, j: (b, 0, 0)),
        pl.BlockSpec((1, 3, d), lambda b, i, j: (ctx_row, 0, 0)),
        pl.BlockSpec((d, n_tile), lambda b, i, j: (0, j)),
    ]
    args = [x, ctx, norm_g.reshape(1, d), mod3, mod3, w16]
    out_specs = [pl.BlockSpec((1, rows, n_tile), lambda b, i, j: (b, i, j))]
    out_shape = [jax.ShapeDtypeStruct((bsz, s, n), BF16)]
    if with_gates:
        ng = gate_w.shape[0]
        in_specs += [pl.BlockSpec((ng, d), lambda b, i, j: (0, 0)),
                     pl.BlockSpec((ng, LANES), lambda b, i, j: (0, 0))]
        args += [gate_w, gate_b]
        out_specs.append(pl.BlockSpec((1, ng, rows), lambda b, i, j: (b, 0, i)))
        out_shape.append(jax.ShapeDtypeStruct((bsz, ng, s), F32))
    return pl.pallas_call(
        functools.partial(_proj_in_kernel, rows=rows, with_gates=with_gates),
        grid=grid,
        in_specs=in_specs,
        out_specs=out_specs,
        out_shape=out_shape,
        scratch_shapes=[pltpu.VMEM((rows, d), BF16)],
        compiler_params=_compiler_params(("parallel", "arbitrary", "arbitrary")),
        name="norm_mod_proj_in",
    )(*args)


def _proj_out_kernel(ya_ref, yb_ref, x_ref, c_ref, ml_ref, mc_ref, w_ref, ox_ref, oc_ref, *, rows):
    i = pl.program_id(1)
    half = ya_ref.shape[2]
    y = _bdot(ya_ref[0], w_ref[0:half, :]) + _bdot(yb_ref[0], w_ref[half:, :])
    x, is_ctx = _token_tile(x_ref, c_ref, i, rows)
    x = x + jnp.where(is_ctx, mc_ref[0][2:3], ml_ref[0][2:3]) * y
    ox_ref[0] = x

    @pl.when(i == pl.num_programs(1) - 1)
    def _():
        oc_ref[0] = x[rows - c_ref.shape[1]:]


def _proj_out(ya, yb, x, ctx, mod3, w16):
    bsz, n_lat, d = x.shape
    n_ctx = ctx.shape[1]
    s = n_lat + n_ctx
    half = ya.shape[2]
    rows = PROJ_ROWS
    assert s % rows == 0 and (n_lat % rows) + n_ctx == rows
    ctx_row = mod3.shape[0] - 1
    tok = lambda w: pl.BlockSpec((1, rows, w), lambda b, i: (b, i, 0))
    seg = pl.BlockSpec((1, n_ctx, d), lambda b, i: (b, 0, 0))
    return pl.pallas_call(
        functools.partial(_proj_out_kernel, rows=rows),
        grid=(bsz, s // rows),
        in_specs=[tok(half), tok(half), tok(d), seg,
                  pl.BlockSpec((1, 3, d), lambda b, i: (b, 0, 0)),
                  pl.BlockSpec((1, 3, d), lambda b, i: (ctx_row, 0, 0)),
                  pl.BlockSpec((2 * half, d), lambda b, i: (0, 0))],
        out_specs=[tok(d), seg],
        out_shape=[jax.ShapeDtypeStruct((bsz, n_lat, d), F32), jax.ShapeDtypeStruct((bsz, n_ctx, d), F32)],
        compiler_params=_compiler_params(("parallel", "arbitrary")),
        name="proj_out_residual",
    )(ya, yb, x, ctx, mod3, mod3, w16)


def _proj_out_final_kernel(ya_ref, yb_ref, x_ref, ml_ref, w_ref, fg_ref, o_ref):
    half = ya_ref.shape[2]
    y = _bdot(ya_ref[0], w_ref[0:half, :]) + _bdot(yb_ref[0], w_ref[half:, :])
    x = x_ref[0] + ml_ref[0][2:3] * y
    o_ref[0] = x * lax.rsqrt(jnp.mean(x * x, axis=-1, keepdims=True) + RMS_EPS) * fg_ref[...]


def _proj_out_final(ya, yb, x, mod3, w16, final_g):
    bsz, n_lat, d = x.shape
    half = ya.shape[2]
    rows = FINAL_ROWS
    assert n_lat % rows == 0
    tok = lambda w: pl.BlockSpec((1, rows, w), lambda b, i: (b, i, 0))
    return pl.pallas_call(
        _proj_out_final_kernel,
        grid=(bsz, n_lat // rows),
        in_specs=[tok(half), tok(half), tok(d),
                  pl.BlockSpec((1, 3, d), lambda b, i: (b, 0, 0)),
                  pl.BlockSpec((2 * half, d), lambda b, i: (0, 0)),
                  pl.BlockSpec((1, d), lambda b, i: (0, 0))],
        out_specs=tok(d),
        out_shape=jax.ShapeDtypeStruct((bsz, n_lat, d), F32),
        compiler_params=_compiler_params(("parallel", "arbitrary")),
        name="proj_out_final_norm",
    )(ya, yb, x, mod3, w16, final_g.reshape(1, d))


def _mlstm_chunk_operators(chunks, causal):
    t = chunks[0][0].shape[0]
    lane = _iota((8, t), 1)
    row_id = _iota((8, t), 0)
    log_fs = [jnp.minimum(c[3], 0.0) - jnp.log1p(jnp.exp(-jnp.abs(c[3]))) for c in chunks]
    cum_f, cum_b = list(log_fs), list(log_fs)
    sh = 1
    while sh < t:
        cum_f = [x + jnp.where(lane >= sh, pltpu.roll(x, sh, 1), 0.0) for x in cum_f]
        cum_b = [x + jnp.where(lane < t - sh, pltpu.roll(x, t - sh, 1), 0.0) for x in cum_b]
        sh *= 2
    pad = jnp.zeros((t - 8, t), F32)
    tiles = [jnp.concatenate([jnp.where(row_id % 2 == 0, c[3], jnp.where(row_id == 1, f, b)), pad], axis=0)
             for c, f, b in zip(chunks, cum_f, cum_b)]
    cols = [x.T for x in tiles]
    problems = []
    for c, f, b, col in zip(chunks, cum_f, cum_b, cols):
        for d in range(2):
            b_row = (f, b)[d][2 * d + 1:2 * d + 2]
            problems.append(dict(q=c[0], k=c[1], v_ext=c[2], d=d, ig_row=c[3][2 * d:2 * d + 1], b_row=b_row,
                                 ig_col=col[:, 2 * d:2 * d + 1], b_col=col[:, 2 * d + 1:2 * d + 2]))
    logws = [jnp.where(causal[p["d"]], p["b_col"] + (p["ig_row"] - p["b_row"]), -jnp.inf) for p in problems]
    mus = [jnp.max(x, axis=-1, keepdims=True) for x in logws]
    ws = [jnp.exp(x - mu) for x, mu in zip(logws, mus)]
    lasts = [0 if p["d"] == 1 else t - 1 for p in problems]
    b_lasts = [p["b_col"][i:i + 1] for p, i in zip(problems, lasts)]
    gammas = [mu[i:i + 1] for mu, i in zip(mus, lasts)]
    gks = [jnp.exp(bl - p["b_col"] + p["ig_col"] - gm) * p["k"] for p, bl, gm in zip(problems, b_lasts, gammas)]
    qks = [_bdot_nt(p["q"], p["k"]) * w for p, w in zip(problems, ws)]
    intras = [_bdot(qk, p["v_ext"]) for qk, p in zip(qks, problems)]
    kvs = [_bdot_tn(gk, p["v_ext"]) for gk, p in zip(gks, problems)]
    dh = chunks[0][0].shape[1]
    return [(intra, kv, jnp.broadcast_to(mu - p["b_col"], (t, dh)), jnp.broadcast_to(mu, (t, dh)), bl, gm)
            for intra, kv, p, mu, bl, gm in zip(intras, kvs, problems, mus, b_lasts, gammas)]


def _mlstm_kernel(q_ref, k_ref, v_ref, o_ref, z_ref, gt_ref, cwq_ref, cwk_ref, ng_ref, out_ref,
                  qa_scr, ka_scr, h_scr, intra_scr, kv_scr, delta_scr, mu_scr, tail_scr, *, n_ctx):
    s = q_ref.shape[1]
    dh = q_ref.shape[2]
    t = MIX_CHUNK
    n_chunks = s // t
    n_ctx_chunks = n_ctx // t
    k_scale = dh ** -0.5

    p_rows = MIX_PREP_ROWS

    def prep(j, carry):
        t0 = pl.multiple_of(j * p_rows, p_rows)
        for src, cw, dst, scale in ((q_ref, cwq_ref, qa_scr, 1.0), (k_ref, cwk_ref, ka_scr, k_scale)):
            cur, down, up = _chunk_with_neighbors(src, t0, p_rows, s, s - n_ctx)
            conv = down * cw[0:1, :] + cur * cw[1:2, :] + up * cw[2:3, :]
            dst[pl.ds(t0, p_rows), :] = _silu(conv) * scale
        return carry

    lax.fori_loop(0, s // p_rows, prep, 0)

    ones_col = jnp.ones((t, dh), F32)
    causal = [_iota((t, t), 1) <= _iota((t, t), 0), _iota((t, t), 1) >= _iota((t, t), 0)]

    def operators(gi, carry):
        chunks, where = [], []
        for kk in range(MLSTM_GROUP):
            chunk = gi * MLSTM_GROUP + kk
            sl = pl.ds(pl.multiple_of(chunk * t, t), t)
            v_ext = jnp.concatenate([v_ref[0, sl, :].astype(F32), ones_col], axis=1)
            chunks.append((qa_scr[sl, :], ka_scr[sl, :], v_ext, gt_ref[0, 0, :, sl]))
            where += [(0, chunk, sl), (1, chunk, sl)]
        for (d, chunk, sl), (intra, kv, delta, mu, b_last, gamma) in zip(where, _mlstm_chunk_operators(chunks, causal)):
            intra_scr[d, sl, :] = intra
            kv_scr[d, chunk] = kv
            delta_scr[d, sl, :] = delta
            mu_scr[d, sl, :] = mu
            tail_scr[d, chunk] = jnp.concatenate([jnp.broadcast_to(b_last, (1, dh)), jnp.broadcast_to(gamma, (1, dh))],
                                                 axis=0)
        return carry

    lax.fori_loop(0, n_chunks // MLSTM_GROUP, operators, 0)

    def scan(trip, carry):
        steps = []
        for k in range(SCAN_UNROLL):
            chunks = _scan_chunks(trip * SCAN_UNROLL + k, n_chunks, n_ctx_chunks)
            new = []
            for d, (chunk, (c_ext, m)) in enumerate(zip(chunks, carry)):
                steps.append((d, chunk, pl.ds(pl.multiple_of(chunk * t, t), t), c_ext, m))
                tail = tail_scr[d, chunk]
                b_last, gamma = tail[0:1, 0:1], tail[1:2, 0:1]
                m_new = jnp.maximum(b_last + m, gamma)
                new.append((jnp.exp(b_last + m - m_new) * c_ext + jnp.exp(gamma - m_new) * kv_scr[d, chunk], m_new))
            carry = tuple(new)
        inters = [_bdot(qa_scr[sl, :], c_ext) for _, _, sl, c_ext, _ in steps]
        for (d, chunk, sl, _, m), inter in zip(steps, inters):
            z = delta_scr[d, sl, :] - m
            s_inter = jnp.exp(-jnp.maximum(z, 0.0))
            s_intra = jnp.exp(jnp.minimum(z, 0.0))
            floor = jnp.exp(jnp.minimum(z, 0.0) - mu_scr[d, sl, :])
            intra = intra_scr[d, sl, :]
            num = s_inter * inter[:, :dh] + s_intra * intra[:, :dh]
            den = s_inter * inter[:, dh:] + s_intra * intra[:, dh:]
            h_scr[d, sl, :] = num / jnp.maximum(jnp.abs(den), floor)
        return carry

    zero = (jnp.zeros((dh, 2 * dh), F32), jnp.zeros((1, 1), F32))
    lax.fori_loop(0, n_chunks // SCAN_UNROLL, scan, (zero, zero))

    f_rows = MIX_FINISH_ROWS

    def finish(j, carry):
        sl = pl.ds(pl.multiple_of(j * f_rows, f_rows), f_rows)
        h = h_scr[0, sl, :] + h_scr[1, sl, :]
        y = h * lax.rsqrt(jnp.mean(h * h, axis=-1, keepdims=True) + RMS_EPS) * ng_ref[...]
        gated = y * _sigmoid(o_ref[0, sl, :].astype(F32)) * _silu(z_ref[0, sl, :].astype(F32))
        out_ref[0, sl, :] = gated.astype(out_ref.dtype)
        return carry

    lax.fori_loop(0, s // f_rows, finish, 0)


def _mlstm(u, gt, conv_w, norm_g, n_ctx):
    bsz, s, _ = u.shape
    nh = MLSTM_N_HEADS
    dh = LANES
    width = nh * dh
    n_chunks = s // MIX_CHUNK
    assert n_chunks % MLSTM_GROUP == 0 and n_chunks % SCAN_UNROLL == 0 and n_ctx % MIX_CHUNK == 0
    col = lambda k: pl.BlockSpec((1, s, dh), lambda b, h, k=k: (b, 0, k * nh + h))
    par = lambda k: pl.BlockSpec((3, dh), lambda b, h, k=k: (0, k * nh + h))
    return pl.pallas_call(
        functools.partial(_mlstm_kernel, n_ctx=n_ctx),
        grid=(bsz, nh),
        in_specs=[col(0), col(1), col(2), col(3), col(4),
                  pl.BlockSpec((1, 1, 8, s), lambda b, h: (b, h, 0, 0)),
                  par(0), par(1),
                  pl.BlockSpec((1, dh), lambda b, h: (0, h))],
        out_specs=pl.BlockSpec((1, s, dh), lambda b, h: (b, 0, h)),
        out_shape=jax.ShapeDtypeStruct((bsz, s, width), BF16),
        scratch_shapes=[pltpu.VMEM((s, dh), F32), pltpu.VMEM((s, dh), F32), pltpu.VMEM((2, s, dh), F32),
                        pltpu.VMEM((2, s, 2 * dh), F32), pltpu.VMEM((2, n_chunks, dh, 2 * dh), F32),
                        pltpu.VMEM((2, s, dh), F32), pltpu.VMEM((2, s, dh), F32),
                        pltpu.VMEM((2, n_chunks, 2, dh), F32)],
        compiler_params=_compiler_params(("parallel", "arbitrary")),
        name="mlstm_mixer",
    )(u, u, u, u, u, gt, conv_w, conv_w, norm_g.reshape(1, width))


def _head_stack(x, lane_lo):
    return jnp.concatenate([jnp.where(lane_lo, x, 0.0), jnp.where(lane_lo, 0.0, x)], axis=0)


def _half_rows(x, c, upper):
    start = c if upper else 0
    return jnp.concatenate([x[r + start:r + start + c] for r in range(0, x.shape[0], 2 * c)], axis=0)


def _merge_rows(other, part, c, upper):
    pieces = []
    for k in range(part.shape[0] // c):
        pair = (other[k * c:(k + 1) * c], part[k * c:(k + 1) * c])
        pieces += pair if upper else pair[::-1]
    return jnp.concatenate(pieces, axis=0)


def _spread_rows(part, c, upper):
    return _merge_rows(jnp.zeros_like(part), part, c, upper)


def _rwkv_chunk_operators(problems, consts, eye, lane_lo):
    t, w = problems[0][0].shape
    n2 = 2 * t
    stack = lambda x: _head_stack(x, lane_lo)
    zeros = jnp.zeros((n2, w), F32)
    dirs = [p[6] for p in problems]
    rid = _iota((t, w), 0)
    cums = [p[3] for p in problems]
    sh = 1
    while sh < t:
        cums = [x + (jnp.where(rid < t - sh, pltpu.roll(x, t - sh, 0), 0.0) if d == 1 else
                     jnp.where(rid >= sh, pltpu.roll(x, sh, 0), 0.0)) for x, d in zip(cums, dirs)]
        sh *= 2
    pre = []
    for (r, v, kk, lw, ka, kt, d), cum in zip(problems, cums):
        last = 0 if d == 1 else t - 1
        cum_end = cum[last:last + 1]
        e_inv = jnp.exp(-cum)
        e_end = jnp.exp(cum_end - cum)
        a_s = stack(-kk * jnp.exp(cum - lw))
        r_s = stack(r * jnp.exp(cum))
        pre.append(dict(a_s=a_s, r_s=r_s, vs=stack(v), g=jnp.exp(cum_end),
                        ar=jnp.concatenate([a_s, r_s], axis=0),
                        bk=jnp.concatenate([stack(ka * e_inv), stack(kt * e_inv)], axis=0),
                        bk_end=jnp.concatenate([stack(ka * e_end), stack(kt * e_end)], axis=0)))
    m_alls = [_bdot_nt(q["ar"], q["bk"]) for q in pre]
    m_abs = [jnp.where(consts[d]["strict"], m[:n2, :n2], 0.0) for m, d in zip(m_alls, dirs)]
    m_aks = [jnp.where(consts[d]["strict"], m[:n2, n2:], 0.0) for m, d in zip(m_alls, dirs)]
    m_lows = [jnp.where(consts[d]["incl2"], m[n2:, :], 0.0) for m, d in zip(m_alls, dirs)]
    invs = [eye + jnp.where(consts[d]["merges"][0][1], m, 0.0) for m, d in zip(m_abs, dirs)]
    for level in range(1, len(consts[0]["merges"])):
        c = consts[0]["merges"][level][0]
        if c < SUBLANES:
            inner = [_bdot(jnp.where(consts[d]["merges"][level][1], m, 0.0), x) for m, x, d in zip(m_abs, invs, dirs)]
            invs = [x + _bdot(x, y) for x, y in zip(invs, inner)]
        else:
            ups = [d == 0 for d in dirs]
            c_rows = [jnp.where(consts[d]["merges"][level][1], _half_rows(m, c, up), 0.0) for m, d, up in zip(m_abs, dirs, ups)]
            inner = [_bdot(cr, x) for cr, x in zip(c_rows, invs)]
            x_rows = [_half_rows(x, c, up) for x, up in zip(invs, ups)]
            upd = [xr + _bdot(xr, _spread_rows(y, c, up)) for xr, y, up in zip(x_rows, inner, ups)]
            invs = [_merge_rows(_half_rows(x, c, not up), u, c, up) for x, u, up in zip(invs, upd, ups)]
    mv = [_bdot(m, q["vs"]) for m, q in zip(m_aks, pre)]
    solved = [_bdot(x, jnp.concatenate([q["a_s"], y], axis=1)) for x, q, y in zip(invs, pre, mv)]
    zms = [jnp.concatenate([sv, jnp.concatenate([zeros, q["vs"]], axis=1)], axis=0) for sv, q in zip(solved, pre)]
    ry1s = [jnp.concatenate([q["r_s"], zeros], axis=1) + _bdot(m, z) for q, m, z in zip(pre, m_lows, zms)]
    pqs = [_bdot_tn(z, q["bk_end"]) for z, q in zip(zms, pre)]
    out = []
    for ry1, pq, q in zip(ry1s, pqs, pre):
        folded = ry1[:t] + ry1[t:]
        out.append((folded[:, :w], folded[:, w:], pq[:w], pq[w:], q["g"]))
    return out


def _rwkv_kernel(rr_ref, rk_ref, rv_ref, rz_ref, wd_ref, ad_ref, mu_ref, kk_ref, ka_ref, rkk_ref,
                 lnw_ref, lnb_ref, w0_ref, a0_ref, wup_ref, aup_ref, out_ref,
                 r_scr, v_scr, kk_scr, lw_scr, ka_scr, kt_scr, bonus_scr, y_scr, ry_scr, pt_scr, qt_scr, g_scr,
                 *, n_ctx):
    s = rr_ref.shape[1]
    w = rr_ref.shape[2]
    p_rows = RWKV_PREP_ROWS
    t = RWKV_CHUNK
    n_chunks = s // t
    n_ctx_chunks = n_ctx // t
    head_sum = ((_iota((w, w), 0) // RWKV_HEAD) == (_iota((w, w), 1) // RWKV_HEAD)).astype(BF16)
    inv_head = 1.0 / RWKV_HEAD

    def prep(j, carry):
        t0 = pl.multiple_of(j * p_rows, p_rows)
        mixed = []
        for idx, src in enumerate((rr_ref, rk_ref, rv_ref)):
            cur, down, up = _chunk_with_neighbors(src, t0, p_rows, s, s - n_ctx)
            mixed.append(cur + mu_ref[idx:idx + 1, :] * (0.5 * (down + up) - cur))
        r, kr, v = mixed
        kk = kr * kk_ref[...]
        norm = jnp.sqrt(_dot_sel(kk * kk, head_sum))
        kk = kk / jnp.maximum(norm, 1e-12)
        w_raw = _bdot(jnp.tanh(wd_ref[0, pl.ds(t0, p_rows), :].astype(F32)), wup_ref[0]) + w0_ref[0]
        a = _sigmoid(_bdot(ad_ref[0, pl.ds(t0, p_rows), :], aup_ref[0]) + a0_ref[0])
        lw = -math.exp(-0.5) * _sigmoid(w_raw)
        kt_sum = jnp.zeros_like(kr)
        for d in range(2):
            a_d = a[:, d * w:(d + 1) * w]
            kt_d = kr * (1.0 + (a_d - 1.0) * ka_ref[...])
            kt_sum = kt_sum + kt_d
            lw_scr[d, pl.ds(t0, p_rows), :] = lw[:, d * w:(d + 1) * w]
            ka_scr[d, pl.ds(t0, p_rows), :] = kk * a_d
            kt_scr[d, pl.ds(t0, p_rows), :] = kt_d
        coef = _dot_sel(r * kt_sum * rkk_ref[...], head_sum)
        r_scr[pl.ds(t0, p_rows), :] = r
        v_scr[pl.ds(t0, p_rows), :] = v
        kk_scr[pl.ds(t0, p_rows), :] = kk
        bonus_scr[pl.ds(t0, p_rows), :] = coef * v
        return carry

    lax.fori_loop(0, s // p_rows, prep, 0)

    n2 = 2 * t
    r_i = _iota((n2, n2), 0)
    c_i = _iota((n2, n2), 1)
    same = (r_i // t) == (c_i // t)
    rt = r_i % t
    ct = c_i % t
    eye = (r_i == c_i).astype(F32)
    lane_lo = _iota((t, w), 1) < RWKV_HEAD
    consts = []
    for reverse in (False, True):
        strict = jnp.logical_and(same, (ct > rt) if reverse else (ct < rt))
        incl = jnp.logical_and(same, (ct >= rt) if reverse else (ct <= rt))
        merges = []
        c = 1
        while c < t:
            hi_r = (r_i % (2 * c)) >= c
            hi_c = (c_i % (2 * c)) >= c
            cross = jnp.logical_and(hi_c, jnp.logical_not(hi_r)) if reverse else jnp.logical_and(hi_r, jnp.logical_not(hi_c))
            mask = jnp.logical_and((r_i // (2 * c)) == (c_i // (2 * c)), cross)
            merges.append((c, _half_rows(mask, c, not reverse)) if c >= SUBLANES else (c, mask))
            c *= 2
        consts.append(dict(strict=strict, incl2=jnp.concatenate([incl, incl], axis=1), merges=merges))

    def operators(gi, carry):
        problems, where = [], []
        for k in range(RWKV_GROUP):
            chunk = gi * RWKV_GROUP + k
            sl = pl.ds(pl.multiple_of(chunk * t, t), t)
            r, v, kk = r_scr[sl, :], v_scr[sl, :], kk_scr[sl, :]
            for d in range(2):
                problems.append((r, v, kk, lw_scr[d, sl, :], ka_scr[d, sl, :], kt_scr[d, sl, :], d))
                where.append((d, chunk, sl))
        for (d, chunk, sl), (ry, y1, pt, qt, g) in zip(where, _rwkv_chunk_operators(problems, consts, eye, lane_lo)):
            ry_scr[d, sl, :] = ry.astype(BF16)
            y_scr[d, sl, :] = y1
            pt_scr[d, chunk] = pt.astype(BF16)
            qt_scr[d, chunk] = qt
            g_scr[d, chunk] = g
        return carry

    lax.fori_loop(0, n_chunks // RWKV_GROUP, operators, 0)

    def scan(i, carry):
        states = []
        for d, (chunk, ht) in enumerate(zip(_scan_chunks(i, n_chunks, n_ctx_chunks), carry)):
            sl = pl.ds(pl.multiple_of(chunk * t, t), t)
            y_scr[d, sl, :] = _bdot_nt(ry_scr[d, sl, :], ht) + y_scr[d, sl, :]
            states.append(ht * g_scr[d, chunk] + _bdot(ht, pt_scr[d, chunk]) + qt_scr[d, chunk])
        return tuple(states)

    zero_state = jnp.zeros((w, w), F32)
    lax.fori_loop(0, n_chunks, scan, (zero_state, zero_state))

    f_rows = RWKV_FINISH_ROWS

    def finish(j, carry):
        sl = pl.ds(pl.multiple_of(j * f_rows, f_rows), f_rows)
        y = y_scr[0, sl, :] + y_scr[1, sl, :] + bonus_scr[sl, :]
        mu = _dot_sel(y, head_sum) * inv_head
        yc = y - mu
        var = _dot_sel(yc * yc, head_sum) * inv_head
        yn = yc * lax.rsqrt(var + RWKV_LN_EPSILON) * lnw_ref[...] + lnb_ref[...]
        out_ref[0, sl, :] = (yn * _silu(rz_ref[0, sl, :].astype(F32))).astype(out_ref.dtype)
        return carry

    lax.fori_loop(0, s // f_rows, finish, 0)


def _rwkv(u, col0, p, n_ctx):
    bsz, s, _ = u.shape
    w = LANES
    width = p["mu"].shape[1]
    n_pairs = width // w
    base = col0 // w
    col = lambda k: pl.BlockSpec((1, s, w), lambda b, h, k=k: (b, 0, base + k * n_pairs + h))
    lora = lambda k: pl.BlockSpec((1, s, w), lambda b, h, k=k: (b, 0, base + 4 * n_pairs + k))
    vec = lambda rows: pl.BlockSpec((rows, w), lambda b, h: (0, h))
    cat = pl.BlockSpec((1, 1, 2 * w), lambda b, h: (h, 0, 0))
    up = pl.BlockSpec((1, w, 2 * w), lambda b, h: (h, 0, 0))
    seq = pltpu.VMEM((s, w), F32)
    seq2 = pltpu.VMEM((2, s, w), F32)
    n_chunks = s // RWKV_CHUNK
    assert n_chunks % RWKV_GROUP == 0 and s % RWKV_PREP_ROWS == 0 and n_ctx % RWKV_PREP_ROWS == 0
    assert s % RWKV_FINISH_ROWS == 0
    operators = [pltpu.VMEM((2, s, w), BF16), pltpu.VMEM((2, n_chunks, w, w), BF16),
                 pltpu.VMEM((2, n_chunks, w, w), F32), pltpu.VMEM((2, n_chunks, 1, w), F32)]
    return pl.pallas_call(
        functools.partial(_rwkv_kernel, n_ctx=n_ctx),
        grid=(bsz, n_pairs),
        in_specs=[col(0), col(1), col(2), col(3), lora(0), lora(1),
                  vec(3), vec(1), vec(1), vec(1), vec(1), vec(1), cat, cat, up, up],
        out_specs=pl.BlockSpec((1, s, w), lambda b, h: (b, 0, h)),
        out_shape=jax.ShapeDtypeStruct((bsz, s, width), BF16),
        scratch_shapes=[seq, seq, seq, seq2, seq2, seq2, seq, seq2] + operators,
        compiler_params=_compiler_params(("parallel", "arbitrary")),
        name="rwkv7_mixer",
    )(u, u, u, u, u, u, p["mu"], p["k_k"], p["k_a"], p["r_k"], p["ln_w"], p["ln_b"],
      p["w0"], p["a0"], p["w_up"], p["a_up"])


def _hgrn_level_masks(t, w):
    rid = _iota((t, w), 0)
    r_i = _iota((t, t), 0)
    c_i = _iota((t, t), 1)
    levels = []
    c = 1
    while c < t:
        same_block = (r_i // (2 * c)) == (c_i // (2 * c))
        up_r = (r_i % (2 * c)) >= c
        up_c = (c_i % (2 * c)) >= c
        pair = [jnp.logical_and(same_block, jnp.logical_and(up_r, jnp.logical_not(up_c))),
                jnp.logical_and(same_block, jnp.logical_and(up_c, jnp.logical_not(up_r)))]
        levels.append((c, (rid % (2 * c)) >= c, pair))
        c *= 2
    return levels


def _hgrn_chunk_operators(problems, lb, tris, levels):
    t, w = problems[0][0].shape
    r_i = _iota((t, t), 0)
    c_i = _iota((t, t), 1)
    zero_row = jnp.zeros((1, w), F32)
    dirs = [p[3] for p in problems]
    lgs, ks = [], []
    for q, v, ff, d in problems:
        e = jnp.exp(-jnp.abs(ff))
        big = 1.0 / (1.0 + e)
        small = e / (1.0 + e)
        pos = ff >= 0.0
        lgs.append(jnp.log(lb + (1.0 - lb) * jnp.where(pos, big, small)) * LOG2_E)
        ks.append((1.0 - lb) * jnp.where(pos, small, big))
    bs = [_sel_dot(tris[d], lg) for lg, d in zip(lgs, dirs)]
    befores = [_shifted(b, zero_row, zero_row)[1 if d == 1 else 0] for b, d in zip(bs, dirs)]
    edges = list(bs)
    accs = [jnp.where(r_i == c_i, jnp.sum(p[0] * k, axis=-1, keepdims=True), 0.0) for p, k in zip(problems, ks)]
    for c, upper, pair in levels:
        qts = [p[0] * jnp.exp2(b - before) for p, b, before in zip(problems, bs, befores)]
        kts = [k * jnp.exp2(edge - b) for k, b, edge in zip(ks, bs, edges)]
        prods = [_bdot_nt(qt, kt) for qt, kt in zip(qts, kts)]
        accs = [jnp.where(pair[d], pr, a) for a, pr, d in zip(accs, prods, dirs)]
        for i, d in enumerate(dirs):
            if d == 1:
                befores[i] = jnp.where(upper, befores[i], pltpu.roll(befores[i], t - c, 0))
                edges[i] = jnp.where(upper, pltpu.roll(edges[i], c, 0), edges[i])
            else:
                befores[i] = jnp.where(upper, pltpu.roll(befores[i], c, 0), befores[i])
                edges[i] = jnp.where(upper, edges[i], pltpu.roll(edges[i], t - c, 0))
    o_intras = [_bdot(a, p[1]) for a, p in zip(accs, problems)]
    b_ends = [b[(0 if d == 1 else t - 1):(1 if d == 1 else t)] for b, d in zip(bs, dirs)]
    kvs = [_bdot_tn(p[1], k * jnp.exp2(be - b)) for p, k, b, be in zip(problems, ks, bs, b_ends)]
    return [(p[0] * jnp.exp2(b), oi, kv, jnp.exp2(be)) for p, b, oi, kv, be in zip(problems, bs, o_intras, kvs, b_ends)]


def _hgrn_kernel(q_ref, i_ref, ff_ref, fb_ref, z_ref, lb_ref, ng_ref, out_ref, o_scr, qe_scr, kv_scr, g_scr,
                 *, n_ctx, layer):
    s = q_ref.shape[1]
    dh = q_ref.shape[2]
    t = MIX_CHUNK
    n_chunks = s // t
    n_ctx_chunks = n_ctx // t
    lbs = lb_ref[...]
    ex = jnp.exp(lbs - jnp.max(lbs, axis=0, keepdims=True))
    probs = ex / jnp.sum(ex, axis=0, keepdims=True)
    csum = probs[0:1]
    for l in range(1, layer + 1):
        csum = csum + probs[l:l + 1]
    lb = csum - probs[0:1]
    tri_r = _iota((t, t), 0)
    tri_c = _iota((t, t), 1)
    tris = [(tri_c <= tri_r).astype(BF16), (tri_c >= tri_r).astype(BF16)]
    levels = _hgrn_level_masks(t, dh)

    def operators(gi, carry):
        problems, where = [], []
        for kk in range(HGRN_GROUP):
            chunk = gi * HGRN_GROUP + kk
            sl = pl.ds(pl.multiple_of(chunk * t, t), t)
            q, v = q_ref[0, sl, :].astype(F32), i_ref[0, sl, :].astype(F32)
            for d, f_ref in enumerate((ff_ref, fb_ref)):
                problems.append((q, v, f_ref[0, sl, :].astype(F32), d))
                where.append((d, chunk, sl))
        for (d, chunk, sl), (qe, o_intra, kv, g) in zip(where, _hgrn_chunk_operators(problems, lb, tris, levels)):
            qe_scr[d, sl, :] = qe.astype(BF16)
            o_scr[d, sl, :] = o_intra
            kv_scr[d, chunk] = kv
            g_scr[d, chunk] = g
        return carry

    lax.fori_loop(0, n_chunks // HGRN_GROUP, operators, 0)

    def scan(trip, carry):
        steps = []
        for k in range(SCAN_UNROLL):
            chunks = _scan_chunks(trip * SCAN_UNROLL + k, n_chunks, n_ctx_chunks)
            steps += [(d, pl.ds(pl.multiple_of(chunk * t, t), t), st) for d, (chunk, st) in enumerate(zip(chunks, carry))]
            carry = tuple(st * g_scr[d, chunk] + kv_scr[d, chunk] for d, (chunk, st) in enumerate(zip(chunks, carry)))
        inters = [_bdot_nt(qe_scr[d, sl, :], st) for d, sl, st in steps]
        for (d, sl, _), inter in zip(steps, inters):
            o_scr[d, sl, :] = o_scr[d, sl, :] + inter
        return carry

    zero_state = jnp.zeros((dh, dh), F32)
    lax.fori_loop(0, n_chunks // SCAN_UNROLL, scan, (zero_state, zero_state))

    f_rows = MIX_FINISH_ROWS

    def finish(j, carry):
        sl = pl.ds(pl.multiple_of(j * f_rows, f_rows), f_rows)
        o = o_scr[0, sl, :] + o_scr[1, sl, :]
        y = o * lax.rsqrt(jnp.mean(o * o, axis=-1, keepdims=True) + RMS_EPS) * ng_ref[...]
        out_ref[0, sl, :] = (y * _silu(z_ref[0, sl, :].astype(F32))).astype(out_ref.dtype)
        return carry

    lax.fori_loop(0, (s - n_ctx) // f_rows, finish, 0)


def _hgrn(u, lb_all, norm_g, n_ctx, layer):
    bsz, s, _ = u.shape
    nh = HGRN_N_HEADS
    dh = LANES
    width = nh * dh
    depth = lb_all.shape[0]
    n_chunks = s // MIX_CHUNK
    assert n_chunks % HGRN_GROUP == 0 and n_chunks % SCAN_UNROLL == 0 and n_ctx % MIX_CHUNK == 0
    col = lambda k: pl.BlockSpec((1, s, dh), lambda b, h, k=k: (b, 0, k * nh + h))
    return pl.pallas_call(
        functools.partial(_hgrn_kernel, n_ctx=n_ctx, layer=layer),
        grid=(bsz, nh),
        in_specs=[col(0), col(1), col(2), col(3), col(4),
                  pl.BlockSpec((depth, dh), lambda b, h: (0, h)),
                  pl.BlockSpec((1, dh), lambda b, h: (0, h))],
        out_specs=pl.BlockSpec((1, s - n_ctx, dh), lambda b, h: (b, 0, h)),
        out_shape=jax.ShapeDtypeStruct((bsz, s - n_ctx, width), BF16),
        scratch_shapes=[pltpu.VMEM((2, s, dh), F32), pltpu.VMEM((2, s, dh), BF16),
                        pltpu.VMEM((2, n_chunks, dh, dh), F32), pltpu.VMEM((2, n_chunks, 1, dh), F32)],
        compiler_params=_compiler_params(("parallel", "arbitrary")),
        name="hgrn2_mixer",
    )(u, u, u, u, u, lb_all, norm_g.reshape(1, width))


def _hyena_filter_kernel(z_ref, w1_ref, b1_ref, w2_ref, b2_ref, w3f_ref, w3b_ref, dl_ref, hf_ref, hb_ref):
    hp = functools.partial(jnp.dot, precision=lax.Precision.HIGHEST, preferred_element_type=F32)
    n = z_ref.shape[0]
    hid = jnp.sin(hp(z_ref[...], w1_ref[...]) + b1_ref[...])
    hid = jnp.sin(hp(hid, w2_ref[...]) + b2_ref[...])
    pos = _iota((n, 1), 0).astype(F32) * (1.0 / n)
    window = jnp.exp(-pos * dl_ref[...]) + HYENA_SHIFT
    f0 = hp(hid, w3f_ref[...]) * window
    f1 = hp(hid, w3b_ref[...]) * window
    nrm = jnp.sum(jnp.abs(f0), axis=0, keepdims=True) + jnp.sum(jnp.abs(f1), axis=0, keepdims=True)
    hf_ref[...] = f0 / nrm
    hb_ref[...] = f1 / nrm


def _hyena_filters(n, w1, b1, w2, b2, w3, width):
    pos = np.arange(n, dtype=np.float64)
    bands = np.linspace(1e-4, HYENA_N_BANDS - 1, HYENA_N_BANDS)
    ang = (2.0 * math.pi / n) * pos[:, None] * bands
    z = np.concatenate([(pos / n)[:, None], np.cos(ang), np.sin(ang)], axis=-1)
    z = np.pad(z, ((0, 0), (0, LANES - z.shape[1]))).astype(np.float32)
    max_decay = math.log(HYENA_TGT) / HYENA_FAST
    min_decay = math.log(HYENA_TGT) / HYENA_SLOW
    deltas = np.abs(np.linspace(min_decay, max_decay, width)).astype(np.float32)[None]
    feat, hid = w1.shape
    w1p = jnp.pad(w1, ((0, LANES - feat), (0, LANES - hid)))
    w2p = jnp.pad(w2, ((0, LANES - hid), (0, LANES - hid)))
    w3p = jnp.pad(w3, ((0, LANES - hid), (0, 0)))
    b1p = jnp.pad(b1, (0, LANES - hid)).reshape(1, LANES)
    b2p = jnp.pad(b2, (0, LANES - hid)).reshape(1, LANES)
    n_tiles = width // LANES
    full = lambda shape: pl.BlockSpec(shape, lambda j: (0, 0))
    out = pl.BlockSpec((n, LANES), lambda j: (0, j))
    hf, hb = pl.pallas_call(
        _hyena_filter_kernel,
        grid=(n_tiles,),
        in_specs=[full((n, LANES)), full((LANES, LANES)), full((1, LANES)), full((LANES, LANES)), full((1, LANES)),
                  pl.BlockSpec((LANES, LANES), lambda j: (0, j)),
                  pl.BlockSpec((LANES, LANES), lambda j: (0, n_tiles + j)),
                  pl.BlockSpec((1, LANES), lambda j: (0, j))],
        out_specs=[out, out],
        out_shape=[jax.ShapeDtypeStruct((n, width), F32)] * 2,
        compiler_params=_compiler_params(("arbitrary",)),
        name="hyena_filters",
    )(jnp.asarray(z), w1p, b1p, w2p, b2p, w3p, w3p, jnp.asarray(deltas))
    return jnp.concatenate([hf, hb], axis=1)


def _hyena_pre_kernel(yv_ref, y0_ref, y1_ref, yz_ref, swv_ref, sw0_ref, sw1_ref, sbv_ref, sb0_ref, sb1_ref,
                      yb_ref, p_ref, e_ref, pb_ref, *, n_ctx):
    s = yv_ref.shape[1]
    rows = MIX_CHUNK

    def body(j, carry):
        t0 = pl.multiple_of(j * rows, rows)
        conv = []
        for src, sw, sb in ((yv_ref, swv_ref, sbv_ref), (y0_ref, sw0_ref, sb0_ref), (y1_ref, sw1_ref, sb1_ref)):
            cur, down, up = _chunk_with_neighbors(src, t0, rows, s, s - n_ctx)
            conv.append(down * sw[0:1, :] + cur * sw[1:2, :] + up * sw[2:3, :] + sb[...])
        v, x0, x1 = conv
        p = x1 * v
        o0 = pl.multiple_of(j * rows, rows)
        p_ref[0, pl.ds(o0, rows), :] = p.astype(BF16)
        pb_ref[0, pl.ds(o0, rows), :] = p * yb_ref[...]
        e_ref[0, pl.ds(o0, rows), :] = x0 * _silu(yz_ref[0, pl.ds(t0, rows), :].astype(F32))
        return carry

    lax.fori_loop(0, (s - n_ctx) // rows, body, 0)


def _hyena_pre(u, col0, short_w, short_b, y_bias, n_ctx):
    bsz, s, _ = u.shape
    w = y_bias.shape[0]
    tiles = w // LANES
    base = col0 // LANES
    n = s - n_ctx
    col = lambda k: pl.BlockSpec((1, s, LANES), lambda b, j, k=k: (b, 0, base + k * tiles + j))
    par = lambda rows, k: pl.BlockSpec((rows, LANES), lambda b, j, k=k: (0, k * tiles + j))
    out = pl.BlockSpec((1, n, LANES), lambda b, j: (b, 0, j))
    sb = short_b.reshape(1, 3 * w)
    return pl.pallas_call(
        functools.partial(_hyena_pre_kernel, n_ctx=n_ctx),
        grid=(bsz, tiles),
        in_specs=[col(0), col(1), col(2), col(3),
                  par(3, 0), par(3, 1), par(3, 2), par(1, 0), par(1, 1), par(1, 2), par(1, 0)],
        out_specs=[out, out, out],
        out_shape=[jax.ShapeDtypeStruct((bsz, n, w), BF16),
                   jax.ShapeDtypeStruct((bsz, n, w), F32),
                   jax.ShapeDtypeStruct((bsz, n, w), F32)],
        compiler_params=_compiler_params(("parallel", "arbitrary")),
        name="hyena_short_conv",
    )(u, u, u, u, short_w, short_w, short_w, sb, sb, sb, y_bias.reshape(1, w))


def _dft_tables(n):
    big = 2 * n
    half = DFT_TILE // 2
    idx = jnp.arange(n, dtype=jnp.int32)
    split = DFT_SPLIT
    lo = jnp.arange(split, dtype=jnp.int32)
    hi = jnp.arange(n // split, dtype=jnp.int32)
    ang_lo = ((lo[:, None] * idx[None, :]) % big).astype(F32) * (2.0 * math.pi / big)
    ang_hi = ((hi[:, None] * idx[None, :]) % (big // split)).astype(F32) * (2.0 * math.pi * split / big)
    c_lo, s_lo = jnp.cos(ang_lo)[None], jnp.sin(ang_lo)[None]
    c_hi, s_hi = jnp.cos(ang_hi)[:, None], jnp.sin(ang_hi)[:, None]
    cos = (c_hi * c_lo - s_hi * s_lo).reshape(n, n)
    sin = (s_hi * c_lo + c_hi * s_lo).reshape(n, n)
    alt = jnp.where(idx % 2 == 0, 1.0, -1.0).astype(F32)
    first_row = (idx == 0)[:, None]
    im = jnp.where(first_row, alt[None, :], -sin)
    fwd = jnp.stack([cos.reshape(n // half, half, n), im.reshape(n // half, half, n)], axis=1).reshape(big, n)
    weight = jnp.where(first_row, 1.0, 2.0) * (1.0 / big)
    weight = jnp.stack([weight.reshape(n // half, half, 1)] * 2, axis=1).reshape(big, 1)
    return fwd.astype(BF16), (fwd * weight).T.astype(BF16)


def _spectrum_kernel(f_ref, x_ref, o_ref):
    half = DFT_TILE // 2
    w = o_ref.shape[1]
    acc = jnp.dot(f_ref[...], x_ref[...].astype(BF16), preferred_element_type=F32)
    o_ref[:half, :] = acc[:half, :w] + acc[:half, w:]
    k_im = acc[half:, :w] - acc[half:, w:]
    packed = jnp.logical_and(_iota((half, w), 0) == 0, pl.program_id(0) == 0)
    o_ref[half:, :] = jnp.where(packed, acc[half:, :w] + acc[half:, w:], k_im)


def _filter_spectrum(fwd, hk):
    big, n = fwd.shape
    cols = hk.shape[1]
    return pl.pallas_call(
        _spectrum_kernel,
        grid=(big // DFT_TILE,),
        in_specs=[pl.BlockSpec((DFT_TILE, n), lambda i: (i, 0)),
                  pl.BlockSpec((n, cols), lambda i: (0, 0))],
        out_specs=pl.BlockSpec((DFT_TILE, cols // 2), lambda i: (i, 0)),
        out_shape=jax.ShapeDtypeStruct((big, cols // 2), F32),
        compiler_params=_compiler_params(("arbitrary",)),
        name="hyena_filter_spectrum",
    )(fwd, hk)


def _conv_spectrum_kernel(f_ref, p_ref, ks_ref, z_ref):
    i = pl.program_id(1)
    half = DFT_TILE // 2
    w = p_ref.shape[2]
    units = [slice(k * DFT_TILE, (k + 1) * DFT_TILE) for k in range(DFT_UNITS)]
    accs = [jnp.dot(f_ref[u, :], p_ref[0], preferred_element_type=F32) for u in units]
    for k, (u, acc) in enumerate(zip(units, accs)):
        s_re, s_im = acc[:half], acc[half:]
        k_re, k_im = ks_ref[u, :][:half], ks_ref[u, :][half:]
        z_re = s_re * k_re - s_im * k_im
        z_im = s_re * k_im + s_im * k_re
        if k == 0:
            packed = jnp.logical_and(_iota((half, w), 0) == 0, i == 0)
            z_re = jnp.where(packed, s_re * k_re, z_re)
            z_im = jnp.where(packed, s_im * k_im, z_im)
        z_ref[0, u, :] = jnp.concatenate([z_re, z_im], axis=0).astype(BF16)


def _conv_spectrum(fwd, p16, kspec):
    bsz, n, w = p16.shape
    big = fwd.shape[0]
    rows = DFT_TILE * DFT_UNITS
    return pl.pallas_call(
        _conv_spectrum_kernel,
        grid=(bsz, big // rows),
        in_specs=[pl.BlockSpec((rows, n), lambda b, i: (i, 0)),
                  pl.BlockSpec((1, n, w), lambda b, i: (b, 0, 0)),
                  pl.BlockSpec((rows, w), lambda b, i: (i, 0))],
        out_specs=pl.BlockSpec((1, rows, w), lambda b, i: (b, i, 0)),
        out_shape=jax.ShapeDtypeStruct((bsz, big, w), BF16),
        compiler_params=_compiler_params(("parallel", "arbitrary")),
        name="hyena_forward_dft",
    )(fwd, p16, kspec)


def _conv_inverse_kernel(g_ref, z_ref, e_ref, pb_ref, o_ref):
    y = jnp.dot(g_ref[...], z_ref[0], preferred_element_type=F32)
    o_ref[0] = (e_ref[0] * (y + pb_ref[0])).astype(o_ref.dtype)


def _conv_inverse(inv, z16, e, pb):
    bsz, big, w = z16.shape
    n = inv.shape[0]
    tile = DFT_TILE
    tok = pl.BlockSpec((1, tile, w), lambda b, i: (b, i, 0))
    return pl.pallas_call(
        _conv_inverse_kernel,
        grid=(bsz, n // tile),
        in_specs=[pl.BlockSpec((tile, big), lambda b, i: (i, 0)),
                  pl.BlockSpec((1, big, w), lambda b, i: (b, 0, 0)),
                  tok, tok],
        out_specs=tok,
        out_shape=jax.ShapeDtypeStruct((bsz, n, w), BF16),
        compiler_params=_compiler_params(("parallel", "arbitrary")),
        name="hyena_inverse_dft",
    )(inv, z16, e, pb)


def _even_weight_layout(w_in, gate_b):
    d = w_in.shape[0]
    mw = MLSTM_N_HEADS * LANES
    g0 = 5 * mw
    g1 = g0 + 4 * MLSTM_N_HEADS
    main = jnp.concatenate([w_in[:, :g0], w_in[:, g1:]], axis=1).astype(BF16)
    wg = w_in[:, g0:g1].reshape(d, 2, 2, MLSTM_N_HEADS)
    wg = jnp.transpose(wg, (3, 1, 2, 0)).reshape(MLSTM_N_HEADS, 4, d)
    wg = jnp.concatenate([wg, jnp.zeros_like(wg)], axis=1).reshape(MLSTM_N_HEADS * 8, d).astype(BF16)
    gb = jnp.transpose(gate_b.reshape(2, 2, MLSTM_N_HEADS), (2, 0, 1)).reshape(MLSTM_N_HEADS, 4)
    gb = jnp.concatenate([gb, jnp.zeros_like(gb)], axis=1).reshape(MLSTM_N_HEADS * 8, 1)
    return main, wg, jnp.broadcast_to(gb, (MLSTM_N_HEADS * 8, LANES))


def _rwkv_params(mu, w0, w_up, a0, a_up, k_k, k_a, r_k, ln_w, ln_b):
    width = mu.shape[1]
    n_pairs = width // LANES
    row = lambda x: x.reshape(1, width)

    def cat_dirs(x):
        return jnp.transpose(x.reshape(2, n_pairs, LANES), (1, 0, 2)).reshape(n_pairs, 1, 2 * LANES)

    def block_up(x):
        lora = x.shape[1]
        xp = jnp.transpose(x.reshape(2, lora, n_pairs, LANES), (2, 0, 1, 3))
        z = jnp.zeros_like(xp[:, 0])
        top = jnp.concatenate([xp[:, 0], z], axis=2)
        bot = jnp.concatenate([z, xp[:, 1]], axis=2)
        return jnp.concatenate([top, bot], axis=1).astype(BF16)

    return {"mu": mu, "k_k": row(k_k), "k_a": row(k_a), "r_k": row(r_k), "ln_w": row(ln_w), "ln_b": row(ln_b),
            "w0": cat_dirs(w0), "a0": cat_dirs(a0), "w_up": block_up(w_up), "a_up": block_up(a_up)}


def _raster_to_column(h):
    b, n, d = h.shape
    rows = n // GRID_WIDTH
    return h.reshape(b, rows, GRID_WIDTH, d).transpose(0, 2, 1, 3).reshape(b, n, d)


def _column_to_raster(h):
    b, n, d = h.shape
    rows = n // GRID_WIDTH
    return h.reshape(b, GRID_WIDTH, rows, d).transpose(0, 2, 1, 3).reshape(b, n, d)


def kernel(x, c, ctx, c_ctx, l0_norm_g, l0_mod_w, l0_mod_b, l0_w_in, l0_w_out, l0_mlstm_conv_w, l0_mlstm_gate_b, l0_mlstm_norm_g, l0_rwkv_mu, l0_rwkv_w0, l0_rwkv_w_up, l0_rwkv_a0, l0_rwkv_a_up, l0_rwkv_k_k, l0_rwkv_k_a, l0_rwkv_r_k, l0_rwkv_ln_w, l0_rwkv_ln_b, hgrn_lower_bounds, l1_norm_g, l1_mod_w, l1_mod_b, l1_w_in, l1_w_out, l1_hgrn_norm_g, l1_hyena_short_w, l1_hyena_short_b, l1_hyena_w1, l1_hyena_b1, l1_hyena_w2, l1_hyena_b2, l1_hyena_w3, l1_hyena_bias, final_norm_g):
    bsz, n_lat, d = x.shape
    n_ctx = ctx.shape[1]

    pad = (-(bsz + 1)) % 8
    cc = jnp.concatenate([c, c_ctx[None], jnp.zeros((pad, d), F32)], axis=0)
    mod0, mod1 = _modulation(cc, l0_mod_w, l0_mod_b, l1_mod_w, l1_mod_b)
    mod0 = mod0[:bsz + 1].reshape(bsz + 1, 3, d)
    mod1 = mod1[:bsz + 1].reshape(bsz + 1, 3, d)

    w_main, w_gate, b_gate = _even_weight_layout(l0_w_in, l0_mlstm_gate_b)
    n0 = w_main.shape[1]
    u0, gt0 = _proj_in(x, ctx, l0_norm_g, mod0, w_main, n0 // 2, w_gate, b_gate)
    gt0 = gt0.reshape(bsz, MLSTM_N_HEADS, 8, n_ctx + n_lat)
    y_m = _mlstm(u0, gt0, l0_mlstm_conv_w, l0_mlstm_norm_g, n_ctx)
    rp = _rwkv_params(l0_rwkv_mu, l0_rwkv_w0, l0_rwkv_w_up, l0_rwkv_a0, l0_rwkv_a_up, l0_rwkv_k_k,
                      l0_rwkv_k_a, l0_rwkv_r_k, l0_rwkv_ln_w, l0_rwkv_ln_b)
    y_r = _rwkv(u0, 5 * MLSTM_N_HEADS * LANES, rp, n_ctx)
    x1, ctx1 = _proj_out(y_m, y_r, x, ctx, mod0, l0_w_out.astype(BF16))

    x1c = _raster_to_column(x1)
    w1 = l1_w_in.astype(BF16)
    (u1,) = _proj_in(x1c, ctx1, l1_norm_g, mod1, w1, w1.shape[1] // 2)
    y_g = _hgrn(u1, hgrn_lower_bounds, l1_hgrn_norm_g, n_ctx, layer=1)
    hw = l1_hyena_bias.shape[0]
    hk = _hyena_filters(n_lat, l1_hyena_w1, l1_hyena_b1, l1_hyena_w2, l1_hyena_b2, l1_hyena_w3, hw)
    fwd, inv = _dft_tables(n_lat)
    kspec = _filter_spectrum(fwd, hk)
    p16, e, pb = _hyena_pre(u1, 5 * HGRN_N_HEADS * LANES, l1_hyena_short_w, l1_hyena_short_b, l1_hyena_bias, n_ctx)
    z16 = _conv_spectrum(fwd, p16, kspec)
    y_y = _conv_inverse(inv, z16, e, pb)
    out_c = _proj_out_final(y_g, y_y, x1c, mod1, l1_w_out.astype(BF16), final_norm_g)
    return _column_to_raster(out_c)
```

```python
import functools
import math

import jax
import jax.numpy as jnp
import numpy as np
from jax import lax
from jax.experimental import pallas as pl
from jax.experimental.pallas import tpu as pltpu

F32 = jnp.float32
BF16 = jnp.bfloat16

GRID_WIDTH = 64
RMS_EPS = 1e-6
MLSTM_N_HEADS = 4
RWKV_HEAD = 64
RWKV_LN_EPSILON = 64e-5
HGRN_N_HEADS = 4
HYENA_N_BANDS = 16
HYENA_FAST = 0.3
HYENA_SLOW = 1.5
HYENA_TGT = 1e-2
HYENA_SHIFT = 0.05
LOG2_E = 1.0 / math.log(2.0)

LANES = 128
SUBLANES = 8
ROW_GROUP = 16
MXU_DIM = 256
VMEM_LIMIT = 52 * 1024 * 1024

MIX_CHUNK = 128
RWKV_CHUNK = 64
RWKV_GROUP = 9
MIX_PREP_ROWS = 256
MIX_FINISH_ROWS = 256
RWKV_PREP_ROWS = 256
RWKV_FINISH_ROWS = 768
RWKV_SCAN_UNROLL = 2
SCAN_UNROLL = 3
MLSTM_GROUP = 6
HGRN_GROUP = 3
PROJ_ROWS = 768
FINAL_ROWS = 512
DFT_TILE = 512
DFT_UNITS = 2
DFT_SPLIT = 64


def _bdot(a, b):
    return jnp.dot(a.astype(BF16), b.astype(BF16), preferred_element_type=F32)


def _bdot_nt(a, b):
    return lax.dot_general(a.astype(BF16), b.astype(BF16), (((1,), (1,)), ((), ())),
                           preferred_element_type=F32)


def _bdot_tn(a, b):
    return lax.dot_general(a.astype(BF16), b.astype(BF16), (((0,), (0,)), ((), ())),
                           preferred_element_type=F32)


def _split3(x):
    hi = x.astype(BF16)
    r1 = x - hi.astype(F32)
    mid = r1.astype(BF16)
    lo = (r1 - mid.astype(F32)).astype(BF16)
    return hi, mid, lo


def _sel_dot(sel, x):
    hi, mid, lo = _split3(x)
    d = functools.partial(jnp.dot, preferred_element_type=F32)
    return d(sel, hi) + d(sel, mid) + d(sel, lo)


def _dot_sel(x, sel):
    hi, mid, lo = _split3(x)
    d = functools.partial(jnp.dot, preferred_element_type=F32)
    return d(hi, sel) + d(mid, sel) + d(lo, sel)


def _sigmoid(x):
    return 1.0 / (1.0 + jnp.exp(-x))


def _silu(x):
    return x * _sigmoid(x)


def _iota(shape, dim):
    return lax.broadcasted_iota(jnp.int32, shape, dim)


def _neighbor_rows(ref, t0, rows, n_total, split):
    has_prev = jnp.logical_and(t0 != 0, t0 != split)
    has_next = jnp.logical_and(t0 + rows != split, t0 + rows != n_total)
    g = ROW_GROUP
    before = ref[0, pl.ds(pl.multiple_of(jnp.maximum(t0 - g, 0), g), g), :].astype(F32)
    after = ref[0, pl.ds(pl.multiple_of(jnp.minimum(t0 + rows, n_total - g), g), g), :].astype(F32)
    return jnp.where(has_prev, before[g - 1:g], 0.0), jnp.where(has_next, after[0:1], 0.0)


def _shifted(cur, prev_row, next_row):
    rows = cur.shape[0]
    rid = _iota(cur.shape, 0)
    down = jnp.where(rid == 0, prev_row, pltpu.roll(cur, 1, 0))
    up = jnp.where(rid == rows - 1, next_row, pltpu.roll(cur, rows - 1, 0))
    return down, up


def _chunk_with_neighbors(ref, t0, rows, n_total, split):
    cur = ref[0, pl.ds(t0, rows), :].astype(F32)
    prev_row, next_row = _neighbor_rows(ref, t0, rows, n_total, split)
    down, up = _shifted(cur, prev_row, next_row)
    return cur, down, up


def _scan_chunks(i, n_chunks, n_ctx_chunks):
    fwd = jnp.where(i < n_ctx_chunks, n_chunks - n_ctx_chunks + i, i - n_ctx_chunks)
    return fwd, n_chunks - 1 - i


def _compiler_params(semantics):
    return pltpu.CompilerParams(dimension_semantics=semantics, vmem_limit_bytes=VMEM_LIMIT)


def _mod_kernel(c_ref, w0_ref, b0_ref, w1_ref, b1_ref, o0_ref, o1_ref):
    s = _silu(c_ref[...])
    o0_ref[...] = _bdot(s, w0_ref[...]) + b0_ref[...]
    o1_ref[...] = _bdot(s, w1_ref[...]) + b1_ref[...]


def _modulation(cc, w0, b0, w1, b1):
    rows, d = cc.shape
    n = w0.shape[1]
    tile = d
    grid = (n // tile,)
    wspec = pl.BlockSpec((d, tile), lambda j: (0, j))
    bspec = pl.BlockSpec((1, tile), lambda j: (0, j))
    ospec = pl.BlockSpec((rows, tile), lambda j: (0, j))
    return pl.pallas_call(
        _mod_kernel,
        grid=grid,
        in_specs=[pl.BlockSpec((rows, d), lambda j: (0, 0)), wspec, bspec, wspec, bspec],
        out_specs=[ospec, ospec],
        out_shape=[jax.ShapeDtypeStruct((rows, n), F32)] * 2,
        compiler_params=_compiler_params(("arbitrary",)),
        name="adaln_modulation",
    )(cc, w0, b0.reshape(1, n), w1, b1.reshape(1, n))


def _token_tile(x_ref, c_ref, i, rows):
    n_lat_tail = rows - c_ref.shape[1]
    is_ctx = jnp.logical_and(i == pl.num_programs(1) - 1, _iota((rows, 1), 0) >= n_lat_tail)
    ctx_rows = jnp.concatenate([jnp.zeros((n_lat_tail, c_ref.shape[2]), F32), c_ref[0]], axis=0)
    return jnp.where(is_ctx, ctx_rows, x_ref[0]), is_ctx


def _proj_in_kernel(*refs, rows, with_gates):
    if with_gates:
        x_ref, c_ref, g_ref, ml_ref, mc_ref, w_ref, wg_ref, gb_ref, u_ref, gt_ref, h_scr = refs
    else:
        x_ref, c_ref, g_ref, ml_ref, mc_ref, w_ref, u_ref, h_scr = refs
    i = pl.program_id(1)
    n = pl.program_id(2)

    @pl.when(n == 0)
    def _():
        x, is_ctx = _token_tile(x_ref, c_ref, i, rows)
        y = x * lax.rsqrt(jnp.mean(x * x, axis=-1, keepdims=True) + RMS_EPS) * g_ref[...]
        ml = ml_ref[0]
        mc = mc_ref[0]
        shift = jnp.where(is_ctx, mc[0:1], ml[0:1])
        scale = jnp.where(is_ctx, mc[1:2], ml[1:2])
        h = (y * (1.0 + scale) + shift).astype(BF16)
        h_scr[...] = h
        if with_gates:
            gt_ref[0] = _bdot_nt(wg_ref[...], h) + gb_ref[:, 0:1]

    u_ref[0] = jnp.dot(h_scr[...], w_ref[...], preferred_element_type=F32).astype(u_ref.dtype)


def _proj_in(x, ctx, norm_g, mod3, w16, n_tile, gate_w=None, gate_b=None):
    bsz, n_lat, d = x.shape
    n_ctx = ctx.shape[1]
    s = n_lat + n_ctx
    n = w16.shape[1]
    rows = PROJ_ROWS
    assert s % rows == 0 and (n_lat % rows) + n_ctx == rows
    grid = (bsz, s // rows, n // n_tile)
    ctx_row = mod3.shape[0] - 1
    with_gates = gate_w is not None
    in_specs = [
        pl.BlockSpec((1, rows, d), lambda b, i, j: (b, i, 0)),
        pl.BlockSpec((1, n_ctx, d), lambda b, i, j: (b, 0, 0)),
        pl.BlockSpec((1, d), lambda b, i, j: (0, 0)),
        pl.BlockSpec((1, 3, d), lambda b, i, j: (b, 0, 0)),
        pl.BlockSpec((1, 3, d), lambda b, i, j: (ctx_row, 0, 0)),
        pl.BlockSpec((d, n_tile), lambda b, i, j: (0, j)),
    ]
    args = [x, ctx, norm_g.reshape(1, d), mod3, mod3, w16]
    out_specs = [pl.BlockSpec((1, rows, n_tile), lambda b, i, j: (b, i, j))]
    out_shape = [jax.ShapeDtypeStruct((bsz, s, n), BF16)]
    if with_gates:
        ng = gate_w.shape[0]
        in_specs += [pl.BlockSpec((ng, d), lambda b, i, j: (0, 0)),
                     pl.BlockSpec((ng, LANES), lambda b, i, j: (0, 0))]
        args += [gate_w, gate_b]
        out_specs.append(pl.BlockSpec((1, ng, rows), lambda b, i, j: (b, 0, i)))
        out_shape.append(jax.ShapeDtypeStruct((bsz, ng, s), F32))
    return pl.pallas_call(
        functools.partial(_proj_in_kernel, rows=rows, with_gates=with_gates),
        grid=grid,
        in_specs=in_specs,
        out_specs=out_specs,
        out_shape=out_shape,
        scratch_shapes=[pltpu.VMEM((rows, d), BF16)],
        compiler_params=_compiler_params(("parallel", "arbitrary", "arbitrary")),
        name="norm_mod_proj_in",
    )(*args)


def _proj_out_kernel(ya_ref, yb_ref, x_ref, c_ref, ml_ref, mc_ref, w_ref, ox_ref, oc_ref, *, rows):
    i = pl.program_id(1)
    half = ya_ref.shape[2]
    y = _bdot(ya_ref[0], w_ref[0:half, :]) + _bdot(yb_ref[0], w_ref[half:, :])
    x, is_ctx = _token_tile(x_ref, c_ref, i, rows)
    x = x + jnp.where(is_ctx, mc_ref[0][2:3], ml_ref[0][2:3]) * y
    ox_ref[0] = x

    @pl.when(i == pl.num_programs(1) - 1)
    def _():
        oc_ref[0] = x[rows - c_ref.shape[1]:]


def _proj_out(ya, yb, x, ctx, mod3, w16):
    bsz, n_lat, d = x.shape
    n_ctx = ctx.shape[1]
    s = n_lat + n_ctx
    half = ya.shape[2]
    rows = PROJ_ROWS
    assert s % rows == 0 and (n_lat % rows) + n_ctx == rows
    ctx_row = mod3.shape[0] - 1
    tok = lambda w: pl.BlockSpec((1, rows, w), lambda b, i: (b, i, 0))
    seg = pl.BlockSpec((1, n_ctx, d), lambda b, i: (b, 0, 0))
    return pl.pallas_call(
        functools.partial(_proj_out_kernel, rows=rows),
        grid=(bsz, s // rows),
        in_specs=[tok(half), tok(half), tok(d), seg,
                  pl.BlockSpec((1, 3, d), lambda b, i: (b, 0, 0)),
                  pl.BlockSpec((1, 3, d), lambda b, i: (ctx_row, 0, 0)),
                  pl.BlockSpec((2 * half, d), lambda b, i: (0, 0))],
        out_specs=[tok(d), seg],
        out_shape=[jax.ShapeDtypeStruct((bsz, n_lat, d), F32), jax.ShapeDtypeStruct((bsz, n_ctx, d), F32)],
        compiler_params=_compiler_params(("parallel", "arbitrary")),
        name="proj_out_residual",
    )(ya, yb, x, ctx, mod3, mod3, w16)


def _proj_out_final_kernel(ya_ref, yb_ref, x_ref, ml_ref, w_ref, fg_ref, o_ref):
    half = ya_ref.shape[2]
    y = _bdot(ya_ref[0], w_ref[0:half, :]) + _bdot(yb_ref[0], w_ref[half:, :])
    x = x_ref[0] + ml_ref[0][2:3] * y
    o_ref[0] = x * lax.rsqrt(jnp.mean(x * x, axis=-1, keepdims=True) + RMS_EPS) * fg_ref[...]


def _proj_out_final(ya, yb, x, mod3, w16, final_g):
    bsz, n_lat, d = x.shape
    half = ya.shape[2]
    rows = FINAL_ROWS
    assert n_lat % rows == 0
    tok = lambda w: pl.BlockSpec((1, rows, w), lambda b, i: (b, i, 0))
    return pl.pallas_call(
        _proj_out_final_kernel,
        grid=(bsz, n_lat // rows),
        in_specs=[tok(half), tok(half), tok(d),
                  pl.BlockSpec((1, 3, d), lambda b, i: (b, 0, 0)),
                  pl.BlockSpec((2 * half, d), lambda b, i: (0, 0)),
                  pl.BlockSpec((1, d), lambda b, i: (0, 0))],
        out_specs=tok(d),
        out_shape=jax.ShapeDtypeStruct((bsz, n_lat, d), F32),
        compiler_params=_compiler_params(("parallel", "arbitrary")),
        name="proj_out_final_norm",
    )(ya, yb, x, mod3, w16, final_g.reshape(1, d))


def _mlstm_chunk_operators(chunks, causal):
    t = chunks[0][0].shape[0]
    lane = _iota((8, t), 1)
    row_id = _iota((8, t), 0)
    log_fs = [jnp.minimum(c[3], 0.0) - jnp.log1p(jnp.exp(-jnp.abs(c[3]))) for c in chunks]
    cum_f, cum_b = list(log_fs), list(log_fs)
    sh = 1
    while sh < t:
        cum_f = [x + jnp.where(lane >= sh, pltpu.roll(x, sh, 1), 0.0) for x in cum_f]
        cum_b = [x + jnp.where(lane < t - sh, pltpu.roll(x, t - sh, 1), 0.0) for x in cum_b]
        sh *= 2
    pad = jnp.zeros((t - 8, t), F32)
    tiles = [jnp.concatenate([jnp.where(row_id % 2 == 0, c[3], jnp.where(row_id == 1, f, b)), pad], axis=0)
             for c, f, b in zip(chunks, cum_f, cum_b)]
    cols = [x.T for x in tiles]
    problems = []
    for c, f, b, col in zip(chunks, cum_f, cum_b, cols):
        for d in range(2):
            b_row = (f, b)[d][2 * d + 1:2 * d + 2]
            problems.append(dict(q=c[0], k=c[1], v_ext=c[2], d=d, ig_row=c[3][2 * d:2 * d + 1], b_row=b_row,
                                 ig_col=col[:, 2 * d:2 * d + 1], b_col=col[:, 2 * d + 1:2 * d + 2]))
    logws = [jnp.where(causal[p["d"]], p["b_col"] + (p["ig_row"] - p["b_row"]), -jnp.inf) for p in problems]
    mus = [jnp.max(x, axis=-1, keepdims=True) for x in logws]
    ws = [jnp.exp(x - mu) for x, mu in zip(logws, mus)]
    lasts = [0 if p["d"] == 1 else t - 1 for p in problems]
    b_lasts = [p["b_col"][i:i + 1] for p, i in zip(problems, lasts)]
    gammas = [mu[i:i + 1] for mu, i in zip(mus, lasts)]
    gks = [jnp.exp(bl - p["b_col"] + p["ig_col"] - gm) * p["k"] for p, bl, gm in zip(problems, b_lasts, gammas)]
    qks = [_bdot_nt(p["q"], p["k"]) * w for p, w in zip(problems, ws)]
    intras = [_bdot(qk, p["v_ext"]) for qk, p in zip(qks, problems)]
    kvs = [_bdot_tn(gk, p["v_ext"]) for gk, p in zip(gks, problems)]
    dh = chunks[0][0].shape[1]
    return [(intra, kv, jnp.broadcast_to(mu - p["b_col"], (t, dh)), jnp.broadcast_to(mu, (t, dh)), bl, gm)
            for intra, kv, p, mu, bl, gm in zip(intras, kvs, problems, mus, b_lasts, gammas)]


def _mlstm_kernel(q_ref, k_ref, v_ref, o_ref, z_ref, gt_ref, cwq_ref, cwk_ref, ng_ref, out_ref,
                  qa_scr, ka_scr, h_scr, intra_scr, kv_scr, delta_scr, mu_scr, tail_scr, *, n_ctx):
    s = q_ref.shape[1]
    dh = q_ref.shape[2]
    t = MIX_CHUNK
    n_chunks = s // t
    n_ctx_chunks = n_ctx // t
    k_scale = dh ** -0.5

    p_rows = MIX_PREP_ROWS

    def prep(j, carry):
        t0 = pl.multiple_of(j * p_rows, p_rows)
        for src, cw, dst, scale in ((q_ref, cwq_ref, qa_scr, 1.0), (k_ref, cwk_ref, ka_scr, k_scale)):
            cur, down, up = _chunk_with_neighbors(src, t0, p_rows, s, s - n_ctx)
            conv = down * cw[0:1, :] + cur * cw[1:2, :] + up * cw[2:3, :]
            dst[pl.ds(t0, p_rows), :] = _silu(conv) * scale
        return carry

    lax.fori_loop(0, s // p_rows, prep, 0)

    ones_col = jnp.ones((t, dh), F32)
    causal = [_iota((t, t), 1) <= _iota((t, t), 0), _iota((t, t), 1) >= _iota((t, t), 0)]

    def operators(gi, carry):
        chunks, where = [], []
        for kk in range(MLSTM_GROUP):
            chunk = gi * MLSTM_GROUP + kk
            sl = pl.ds(pl.multiple_of(chunk * t, t), t)
            v_ext = jnp.concatenate([v_ref[0, sl, :].astype(F32), ones_col], axis=1)
            chunks.append((qa_scr[sl, :], ka_scr[sl, :], v_ext, gt_ref[0, 0, :, sl]))
            where += [(0, chunk, sl), (1, chunk, sl)]
        for (d, chunk, sl), (intra, kv, delta, mu, b_last, gamma) in zip(where, _mlstm_chunk_operators(chunks, causal)):
            intra_scr[d, sl, :] = intra
            kv_scr[d, chunk] = kv
            delta_scr[d, sl, :] = delta
            mu_scr[d, sl, :] = mu
            tail_scr[d, chunk] = jnp.concatenate([jnp.broadcast_to(b_last, (1, dh)), jnp.broadcast_to(gamma, (1, dh))],
                                                 axis=0)
        return carry

    lax.fori_loop(0, n_chunks // MLSTM_GROUP, operators, 0)

    def scan(trip, carry):
        steps = []
        for k in range(SCAN_UNROLL):
            chunks = _scan_chunks(trip * SCAN_UNROLL + k, n_chunks, n_ctx_chunks)
            new = []
            for d, (chunk, (c_ext, m)) in enumerate(zip(chunks, carry)):
                steps.append((d, chunk, pl.ds(pl.multiple_of(chunk * t, t), t), c_ext, m))
                tail = tail_scr[d, chunk]
                b_last, gamma = tail[0:1, 0:1], tail[1:2, 0:1]
                m_new = jnp.maximum(b_last + m, gamma)
                new.append((jnp.exp(b_last + m - m_new) * c_ext + jnp.exp(gamma - m_new) * kv_scr[d, chunk], m_new))
            carry = tuple(new)
        inters = [_bdot(qa_scr[sl, :], c_ext) for _, _, sl, c_ext, _ in steps]
        for (d, chunk, sl, _, m), inter in zip(steps, inters):
            z = delta_scr[d, sl, :] - m
            s_inter = jnp.exp(-jnp.maximum(z, 0.0))
            s_intra = jnp.exp(jnp.minimum(z, 0.0))
            floor = jnp.exp(jnp.minimum(z, 0.0) - mu_scr[d, sl, :])
            intra = intra_scr[d, sl, :]
            num = s_inter * inter[:, :dh] + s_intra * intra[:, :dh]
            den = s_inter * inter[:, dh:] + s_intra * intra[:, dh:]
            h_scr[d, sl, :] = num / jnp.maximum(jnp.abs(den), floor)
        return carry

    zero = (jnp.zeros((dh, 2 * dh), F32), jnp.zeros((1, 1), F32))
    lax.fori_loop(0, n_chunks // SCAN_UNROLL, scan, (zero, zero))

    f_rows = MIX_FINISH_ROWS

    def finish(j, carry):
        sl = pl.ds(pl.multiple_of(j * f_rows, f_rows), f_rows)
        h = h_scr[0, sl, :] + h_scr[1, sl, :]
        y = h * lax.rsqrt(jnp.mean(h * h, axis=-1, keepdims=True) + RMS_EPS) * ng_ref[...]
        gated = y * _sigmoid(o_ref[0, sl, :].astype(F32)) * _silu(z_ref[0, sl, :].astype(F32))
        out_ref[0, sl, :] = gated.astype(out_ref.dtype)
        return carry

    lax.fori_loop(0, s // f_rows, finish, 0)


def _mlstm(u, gt, conv_w, norm_g, n_ctx):
    bsz, s, _ = u.shape
    nh = MLSTM_N_HEADS
    dh = LANES
    width = nh * dh
    n_chunks = s // MIX_CHUNK
    assert n_chunks % MLSTM_GROUP == 0 and n_chunks % SCAN_UNROLL == 0 and n_ctx % MIX_CHUNK == 0
    col = lambda k: pl.BlockSpec((1, s, dh), lambda b, h, k=k: (b, 0, k * nh + h))
    par = lambda k: pl.BlockSpec((3, dh), lambda b, h, k=k: (0, k * nh + h))
    return pl.pallas_call(
        functools.partial(_mlstm_kernel, n_ctx=n_ctx),
        grid=(bsz, nh),
        in_specs=[col(0), col(1), col(2), col(3), col(4),
                  pl.BlockSpec((1, 1, 8, s), lambda b, h: (b, h, 0, 0)),
                  par(0), par(1),
                  pl.BlockSpec((1, dh), lambda b, h: (0, h))],
        out_specs=pl.BlockSpec((1, s, dh), lambda b, h: (b, 0, h)),
        out_shape=jax.ShapeDtypeStruct((bsz, s, width), BF16),
        scratch_shapes=[pltpu.VMEM((s, dh), F32), pltpu.VMEM((s, dh), F32), pltpu.VMEM((2, s, dh), F32),
                        pltpu.VMEM((2, s, 2 * dh), F32), pltpu.VMEM((2, n_chunks, dh, 2 * dh), F32),
                        pltpu.VMEM((2, s, dh), F32), pltpu.VMEM((2, s, dh), F32),
                        pltpu.VMEM((2, n_chunks, 2, dh), F32)],
        compiler_params=_compiler_params(("parallel", "arbitrary")),
        name="mlstm_mixer",
    )(u, u, u, u, u, gt, conv_w, conv_w, norm_g.reshape(1, width))


def _head_stack(x, lane_lo):
    return jnp.concatenate([jnp.where(lane_lo, x, 0.0), jnp.where(lane_lo, 0.0, x)], axis=0)


def _half_rows(x, c, upper):
    start = c if upper else 0
    return jnp.concatenate([x[r + start:r + start + c] for r in range(0, x.shape[0], 2 * c)], axis=0)


def _merge_rows(other, part, c, upper):
    pieces = []
    for k in range(part.shape[0] // c):
        pair = (other[k * c:(k + 1) * c], part[k * c:(k + 1) * c])
        pieces += pair if upper else pair[::-1]
    return jnp.concatenate(pieces, axis=0)


def _spread_rows(part, c, upper):
    return _merge_rows(jnp.zeros_like(part), part, c, upper)


def _rwkv_chunk_operators(problems, consts, eye, lane_lo):
    t, w = problems[0][0].shape
    n2 = 2 * t
    stack = lambda x: _head_stack(x, lane_lo)
    zeros = jnp.zeros((n2, w), F32)
    dirs = [p[6] for p in problems]
    rid = _iota((t, w), 0)
    cums = [p[3] for p in problems]
    sh = 1
    while sh < t:
        cums = [x + (jnp.where(rid < t - sh, pltpu.roll(x, t - sh, 0), 0.0) if d == 1 else
                     jnp.where(rid >= sh, pltpu.roll(x, sh, 0), 0.0)) for x, d in zip(cums, dirs)]
        sh *= 2
    pre = []
    for (r, v, kk, lw, ka, kt, d), cum in zip(problems, cums):
        last = 0 if d == 1 else t - 1
        cum_end = cum[last:last + 1]
        e_inv = jnp.exp(-cum)
        e_end = jnp.exp(cum_end - cum)
        a_s = stack(-kk * jnp.exp(cum - lw))
        r_s = stack(r * jnp.exp(cum))
        pre.append(dict(a_s=a_s, r_s=r_s, vs=stack(v), g=jnp.exp(cum_end),
                        ar=jnp.concatenate([a_s, r_s], axis=0),
                        bk=jnp.concatenate([stack(ka * e_inv), stack(kt * e_inv)], axis=0),
                        bk_end=jnp.concatenate([stack(ka * e_end), stack(kt * e_end)], axis=0)))
    m_alls = [_bdot_nt(q["ar"], q["bk"]) for q in pre]
    m_abs = [jnp.where(consts[d]["strict"], m[:n2, :n2], 0.0) for m, d in zip(m_alls, dirs)]
    m_aks = [jnp.where(consts[d]["strict"], m[:n2, n2:], 0.0) for m, d in zip(m_alls, dirs)]
    m_lows = [jnp.where(consts[d]["incl2"], m[n2:, :], 0.0) for m, d in zip(m_alls, dirs)]
    invs = [eye + jnp.where(consts[d]["merges"][0][1], m, 0.0) for m, d in zip(m_abs, dirs)]
    for level in range(1, len(consts[0]["merges"])):
        c = consts[0]["merges"][level][0]
        if c < SUBLANES:
            inner = [_bdot(jnp.where(consts[d]["merges"][level][1], m, 0.0), x) for m, x, d in zip(m_abs, invs, dirs)]
            invs = [x + _bdot(x, y) for x, y in zip(invs, inner)]
        else:
            ups = [d == 0 for d in dirs]
            c_rows = [jnp.where(consts[d]["merges"][level][1], _half_rows(m, c, up), 0.0) for m, d, up in zip(m_abs, dirs, ups)]
            inner = [_bdot(cr, x) for cr, x in zip(c_rows, invs)]
            x_rows = [_half_rows(x, c, up) for x, up in zip(invs, ups)]
            upd = [xr + _bdot(xr, _spread_rows(y, c, up)) for xr, y, up in zip(x_rows, inner, ups)]
            invs = [_merge_rows(_half_rows(x, c, not up), u, c, up) for x, u, up in zip(invs, upd, ups)]
    mv = [_bdot(m, q["vs"]) for m, q in zip(m_aks, pre)]
    solved = [_bdot(x, jnp.concatenate([q["a_s"], y], axis=1)) for x, q, y in zip(invs, pre, mv)]
    zms = [jnp.concatenate([sv, jnp.concatenate([zeros, q["vs"]], axis=1)], axis=0) for sv, q in zip(solved, pre)]
    ry1s = [jnp.concatenate([q["r_s"], zeros], axis=1) + _bdot(m, z) for q, m, z in zip(pre, m_lows, zms)]
    pqs = [_bdot_tn(z, q["bk_end"]) for z, q in zip(zms, pre)]
    out = []
    for ry1, pq, q in zip(ry1s, pqs, pre):
        folded = ry1[:t] + ry1[t:]
        out.append((folded[:, :w], folded[:, w:], pq[:w], pq[w:], q["g"]))
    return out


def _rwkv_kernel(rr_ref, rk_ref, rv_ref, rz_ref, wd_ref, ad_ref, mu_ref, kk_ref, ka_ref, rkk_ref,
                 lnw_ref, lnb_ref, w0_ref, a0_ref, wup_ref, aup_ref, out_ref,
                 r_scr, v_scr, kk_scr, lw_scr, ka_scr, kt_scr, bonus_scr, y_scr, ry_scr, pt_scr, qt_scr, g_scr,
                 *, n_ctx):
    s = rr_ref.shape[1]
    w = rr_ref.shape[2]
    p_rows = RWKV_PREP_ROWS
    t = RWKV_CHUNK
    n_chunks = s // t
    n_ctx_chunks = n_ctx // t
    head_sum = ((_iota((w, w), 0) // RWKV_HEAD) == (_iota((w, w), 1) // RWKV_HEAD)).astype(BF16)
    inv_head = 1.0 / RWKV_HEAD

    def prep(j, carry):
        t0 = pl.multiple_of(j * p_rows, p_rows)
        mixed = []
        for idx, src in enumerate((rr_ref, rk_ref, rv_ref)):
            cur, down, up = _chunk_with_neighbors(src, t0, p_rows, s, s - n_ctx)
            mixed.append(cur + mu_ref[idx:idx + 1, :] * (0.5 * (down + up) - cur))
        r, kr, v = mixed
        kk = kr * kk_ref[...]
        norm = jnp.sqrt(_dot_sel(kk * kk, head_sum))
        kk = kk / jnp.maximum(norm, 1e-12)
        w_raw = _bdot(jnp.tanh(wd_ref[0, pl.ds(t0, p_rows), :].astype(F32)), wup_ref[0]) + w0_ref[0]
        a = _sigmoid(_bdot(ad_ref[0, pl.ds(t0, p_rows), :], aup_ref[0]) + a0_ref[0])
        lw = -math.exp(-0.5) * _sigmoid(w_raw)
        kt_sum = jnp.zeros_like(kr)
        for d in range(2):
            a_d = a[:, d * w:(d + 1) * w]
            kt_d = kr * (1.0 + (a_d - 1.0) * ka_ref[...])
            kt_sum = kt_sum + kt_d
            lw_scr[d, pl.ds(t0, p_rows), :] = lw[:, d * w:(d + 1) * w]
            ka_scr[d, pl.ds(t0, p_rows), :] = kk * a_d
            kt_scr[d, pl.ds(t0, p_rows), :] = kt_d
        coef = _dot_sel(r * kt_sum * rkk_ref[...], head_sum)
        r_scr[pl.ds(t0, p_rows), :] = r
        v_scr[pl.ds(t0, p_rows), :] = v
        kk_scr[pl.ds(t0, p_rows), :] = kk
        bonus_scr[pl.ds(t0, p_rows), :] = coef * v
        return carry

    lax.fori_loop(0, s // p_rows, prep, 0)

    n2 = 2 * t
    r_i = _iota((n2, n2), 0)
    c_i = _iota((n2, n2), 1)
    same = (r_i // t) == (c_i // t)
    rt = r_i % t
    ct = c_i % t
    eye = (r_i == c_i).astype(F32)
    lane_lo = _iota((t, w), 1) < RWKV_HEAD
    consts = []
    for reverse in (False, True):
        strict = jnp.logical_and(same, (ct > rt) if reverse else (ct < rt))
        incl = jnp.logical_and(same, (ct >= rt) if reverse else (ct <= rt))
        merges = []
        c = 1
        while c < t:
            hi_r = (r_i % (2 * c)) >= c
            hi_c = (c_i % (2 * c)) >= c
            cross = jnp.logical_and(hi_c, jnp.logical_not(hi_r)) if reverse else jnp.logical_and(hi_r, jnp.logical_not(hi_c))
            mask = jnp.logical_and((r_i // (2 * c)) == (c_i // (2 * c)), cross)
            merges.append((c, _half_rows(mask, c, not reverse)) if c >= SUBLANES else (c, mask))
            c *= 2
        consts.append(dict(strict=strict, incl2=jnp.concatenate([incl, incl], axis=1), merges=merges))

    def operators(gi, carry):
        problems, where = [], []
        for k in range(RWKV_GROUP):
            chunk = gi * RWKV_GROUP + k
            sl = pl.ds(pl.multiple_of(chunk * t, t), t)
            r, v, kk = r_scr[sl, :], v_scr[sl, :], kk_scr[sl, :]
            for d in range(2):
                problems.append((r, v, kk, lw_scr[d, sl, :], ka_scr[d, sl, :], kt_scr[d, sl, :], d))
                where.append((d, chunk, sl))
        for (d, chunk, sl), (ry, y1, pt, qt, g) in zip(where, _rwkv_chunk_operators(problems, consts, eye, lane_lo)):
            ry_scr[d, sl, :] = ry.astype(BF16)
            y_scr[d, sl, :] = y1
            pt_scr[d, chunk] = pt.astype(BF16)
            qt_scr[d, chunk] = qt
            g_scr[d, chunk] = g
        return carry

    lax.fori_loop(0, n_chunks // RWKV_GROUP, operators, 0)

    def scan(trip, carry):
        outs = []
        for k in range(RWKV_SCAN_UNROLL):
            chunks = _scan_chunks(trip * RWKV_SCAN_UNROLL + k, n_chunks, n_ctx_chunks)
            sls = [pl.ds(pl.multiple_of(chunk * t, t), t) for chunk in chunks]
            moved = [_bdot(ht, pt_scr[d, chunk]) for d, (chunk, ht) in enumerate(zip(chunks, carry))]
            outs += [(d, sl, _bdot_nt(ry_scr[d, sl, :], ht)) for d, (sl, ht) in enumerate(zip(sls, carry))]
            carry = tuple(ht * g_scr[d, chunk] + mv + qt_scr[d, chunk]
                          for d, (chunk, ht, mv) in enumerate(zip(chunks, carry, moved)))
        for d, sl, y in outs:
            y_scr[d, sl, :] = y + y_scr[d, sl, :]
        return carry

    zero_state = jnp.zeros((w, w), F32)
    lax.fori_loop(0, n_chunks // RWKV_SCAN_UNROLL, scan, (zero_state, zero_state))

    f_rows = RWKV_FINISH_ROWS

    def finish(j, carry):
        sl = pl.ds(pl.multiple_of(j * f_rows, f_rows), f_rows)
        y = y_scr[0, sl, :] + y_scr[1, sl, :] + bonus_scr[sl, :]
        mu = _dot_sel(y, head_sum) * inv_head
        yc = y - mu
        var = _dot_sel(yc * yc, head_sum) * inv_head
        yn = yc * lax.rsqrt(var + RWKV_LN_EPSILON) * lnw_ref[...] + lnb_ref[...]
        out_ref[0, sl, :] = (yn * _silu(rz_ref[0, sl, :].astype(F32))).astype(out_ref.dtype)
        return carry

    lax.fori_loop(0, s // f_rows, finish, 0)


def _rwkv(u, col0, p, n_ctx):
    bsz, s, _ = u.shape
    w = LANES
    width = p["mu"].shape[1]
    n_pairs = width // w
    base = col0 // w
    col = lambda k: pl.BlockSpec((1, s, w), lambda b, h, k=k: (b, 0, base + k * n_pairs + h))
    lora = lambda k: pl.BlockSpec((1, s, w), lambda b, h, k=k: (b, 0, base + 4 * n_pairs + k))
    vec = lambda rows: pl.BlockSpec((rows, w), lambda b, h: (0, h))
    cat = pl.BlockSpec((1, 1, 2 * w), lambda b, h: (h, 0, 0))
    up = pl.BlockSpec((1, w, 2 * w), lambda b, h: (h, 0, 0))
    seq = pltpu.VMEM((s, w), F32)
    seq2 = pltpu.VMEM((2, s, w), F32)
    n_chunks = s // RWKV_CHUNK
    assert n_chunks % RWKV_GROUP == 0 and s % RWKV_PREP_ROWS == 0 and n_ctx % RWKV_PREP_ROWS == 0
    assert s % RWKV_FINISH_ROWS == 0 and n_chunks % RWKV_SCAN_UNROLL == 0
    operators = [pltpu.VMEM((2, s, w), BF16), pltpu.VMEM((2, n_chunks, w, w), BF16),
                 pltpu.VMEM((2, n_chunks, w, w), F32), pltpu.VMEM((2, n_chunks, 1, w), F32)]
    return pl.pallas_call(
        functools.partial(_rwkv_kernel, n_ctx=n_ctx),
        grid=(bsz, n_pairs),
        in_specs=[col(0), col(1), col(2), col(3), lora(0), lora(1),
                  vec(3), vec(1), vec(1), vec(1), vec(1), vec(1), cat, cat, up, up],
        out_specs=pl.BlockSpec((1, s, w), lambda b, h: (b, 0, h)),
        out_shape=jax.ShapeDtypeStruct((bsz, s, width), BF16),
        scratch_shapes=[seq, seq, seq, seq2, seq2, seq2, seq, seq2] + operators,
        compiler_params=_compiler_params(("parallel", "arbitrary")),
        name="rwkv7_mixer",
    )(u, u, u, u, u, u, p["mu"], p["k_k"], p["k_a"], p["r_k"], p["ln_w"], p["ln_b"],
      p["w0"], p["a0"], p["w_up"], p["a_up"])


def _hgrn_level_masks(t, w):
    rid = _iota((t, w), 0)
    r_i = _iota((t, t), 0)
    c_i = _iota((t, t), 1)
    levels = []
    c = 1
    while c < t:
        same_block = (r_i // (2 * c)) == (c_i // (2 * c))
        up_r = (r_i % (2 * c)) >= c
        up_c = (c_i % (2 * c)) >= c
        pair = [jnp.logical_and(same_block, jnp.logical_and(up_r, jnp.logical_not(up_c))),
                jnp.logical_and(same_block, jnp.logical_and(up_c, jnp.logical_not(up_r)))]
        levels.append((c, (rid % (2 * c)) >= c, pair))
        c *= 2
    return levels


def _hgrn_chunk_operators(problems, lb, tris, levels):
    t, w = problems[0][0].shape
    r_i = _iota((t, t), 0)
    c_i = _iota((t, t), 1)
    zero_row = jnp.zeros((1, w), F32)
    dirs = [p[3] for p in problems]
    lgs, ks = [], []
    for q, v, ff, d in problems:
        e = jnp.exp(-jnp.abs(ff))
        big = 1.0 / (1.0 + e)
        small = e / (1.0 + e)
        pos = ff >= 0.0
        lgs.append(jnp.log(lb + (1.0 - lb) * jnp.where(pos, big, small)) * LOG2_E)
        ks.append((1.0 - lb) * jnp.where(pos, small, big))
    bs = [_sel_dot(tris[d], lg) for lg, d in zip(lgs, dirs)]
    befores = [_shifted(b, zero_row, zero_row)[1 if d == 1 else 0] for b, d in zip(bs, dirs)]
    edges = list(bs)
    accs = [jnp.where(r_i == c_i, jnp.sum(p[0] * k, axis=-1, keepdims=True), 0.0) for p, k in zip(problems, ks)]
    for c, upper, pair in levels:
        qts = [p[0] * jnp.exp2(b - before) for p, b, before in zip(problems, bs, befores)]
        kts = [k * jnp.exp2(edge - b) for k, b, edge in zip(ks, bs, edges)]
        prods = [_bdot_nt(qt, kt) for qt, kt in zip(qts, kts)]
        accs = [jnp.where(pair[d], pr, a) for a, pr, d in zip(accs, prods, dirs)]
        for i, d in enumerate(dirs):
            if d == 1:
                befores[i] = jnp.where(upper, befores[i], pltpu.roll(befores[i], t - c, 0))
                edges[i] = jnp.where(upper, pltpu.roll(edges[i], c, 0), edges[i])
            else:
                befores[i] = jnp.where(upper, pltpu.roll(befores[i], c, 0), befores[i])
                edges[i] = jnp.where(upper, edges[i], pltpu.roll(edges[i], t - c, 0))
    o_intras = [_bdot(a, p[1]) for a, p in zip(accs, problems)]
    b_ends = [b[(0 if d == 1 else t - 1):(1 if d == 1 else t)] for b, d in zip(bs, dirs)]
    kvs = [_bdot_tn(p[1], k * jnp.exp2(be - b)) for p, k, b, be in zip(problems, ks, bs, b_ends)]
    return [(p[0] * jnp.exp2(b), oi, kv, jnp.exp2(be)) for p, b, oi, kv, be in zip(problems, bs, o_intras, kvs, b_ends)]


def _hgrn_kernel(q_ref, i_ref, ff_ref, fb_ref, z_ref, lb_ref, ng_ref, out_ref, o_scr, qe_scr, kv_scr, g_scr,
                 *, n_ctx, layer):
    s = q_ref.shape[1]
    dh = q_ref.shape[2]
    t = MIX_CHUNK
    n_chunks = s // t
    n_ctx_chunks = n_ctx // t
    lbs = lb_ref[...]
    ex = jnp.exp(lbs - jnp.max(lbs, axis=0, keepdims=True))
    probs = ex / jnp.sum(ex, axis=0, keepdims=True)
    csum = probs[0:1]
    for l in range(1, layer + 1):
        csum = csum + probs[l:l + 1]
    lb = csum - probs[0:1]
    tri_r = _iota((t, t), 0)
    tri_c = _iota((t, t), 1)
    tris = [(tri_c <= tri_r).astype(BF16), (tri_c >= tri_r).astype(BF16)]
    levels = _hgrn_level_masks(t, dh)

    def operators(gi, carry):
        problems, where = [], []
        for kk in range(HGRN_GROUP):
            chunk = gi * HGRN_GROUP + kk
            sl = pl.ds(pl.multiple_of(chunk * t, t), t)
            q, v = q_ref[0, sl, :].astype(F32), i_ref[0, sl, :].astype(F32)
            for d, f_ref in enumerate((ff_ref, fb_ref)):
                problems.append((q, v, f_ref[0, sl, :].astype(F32), d))
                where.append((d, chunk, sl))
        for (d, chunk, sl), (qe, o_intra, kv, g) in zip(where, _hgrn_chunk_operators(problems, lb, tris, levels)):
            qe_scr[d, sl, :] = qe.astype(BF16)
            o_scr[d, sl, :] = o_intra
            kv_scr[d, chunk] = kv
            g_scr[d, chunk] = g
        return carry

    lax.fori_loop(0, n_chunks // HGRN_GROUP, operators, 0)

    def scan(trip, carry):
        steps = []
        for k in range(SCAN_UNROLL):
            chunks = _scan_chunks(trip * SCAN_UNROLL + k, n_chunks, n_ctx_chunks)
            steps += [(d, pl.ds(pl.multiple_of(chunk * t, t), t), st) for d, (chunk, st) in enumerate(zip(chunks, carry))]
            carry = tuple(st * g_scr[d, chunk] + kv_scr[d, chunk] for d, (chunk, st) in enumerate(zip(chunks, carry)))
        inters = [_bdot_nt(qe_scr[d, sl, :], st) for d, sl, st in steps]
        for (d, sl, _), inter in zip(steps, inters):
            o_scr[d, sl, :] = o_scr[d, sl, :] + inter
        return carry

    zero_state = jnp.zeros((dh, dh), F32)
    lax.fori_loop(0, n_chunks // SCAN_UNROLL, scan, (zero_state, zero_state))

    f_rows = MIX_FINISH_ROWS

    def finish(j, carry):
        sl = pl.ds(pl.multiple_of(j * f_rows, f_rows), f_rows)
        o = o_scr[0, sl, :] + o_scr[1, sl, :]
        y = o * lax.rsqrt(jnp.mean(o * o, axis=-1, keepdims=True) + RMS_EPS) * ng_ref[...]
        out_ref[0, sl, :] = (y * _silu(z_ref[0, sl, :].astype(F32))).astype(out_ref.dtype)
        return carry

    lax.fori_loop(0, (s - n_ctx) // f_rows, finish, 0)


def _hgrn(u, lb_all, norm_g, n_ctx, layer):
    bsz, s, _ = u.shape
    nh = HGRN_N_HEADS
    dh = LANES
    width = nh * dh
    depth = lb_all.shape[0]
    n_chunks = s // MIX_CHUNK
    assert n_chunks % HGRN_GROUP == 0 and n_chunks % SCAN_UNROLL == 0 and n_ctx % MIX_CHUNK == 0
    col = lambda k: pl.BlockSpec((1, s, dh), lambda b, h, k=k: (b, 0, k * nh + h))
    return pl.pallas_call(
        functools.partial(_hgrn_kernel, n_ctx=n_ctx, layer=layer),
        grid=(bsz, nh),
        in_specs=[col(0), col(1), col(2), col(3), col(4),
                  pl.BlockSpec((depth, dh), lambda b, h: (0, h)),
                  pl.BlockSpec((1, dh), lambda b, h: (0, h))],
        out_specs=pl.BlockSpec((1, s - n_ctx, dh), lambda b, h: (b, 0, h)),
        out_shape=jax.ShapeDtypeStruct((bsz, s - n_ctx, width), BF16),
        scratch_shapes=[pltpu.VMEM((2, s, dh), F32), pltpu.VMEM((2, s, dh), BF16),
                        pltpu.VMEM((2, n_chunks, dh, dh), F32), pltpu.VMEM((2, n_chunks, 1, dh), F32)],
        compiler_params=_compiler_params(("parallel", "arbitrary")),
        name="hgrn2_mixer",
    )(u, u, u, u, u, lb_all, norm_g.reshape(1, width))


def _hyena_filter_kernel(z_ref, w1_ref, b1_ref, w2_ref, b2_ref, w3f_ref, w3b_ref, dl_ref, hf_ref, hb_ref):
    hp = functools.partial(jnp.dot, precision=lax.Precision.HIGHEST, preferred_element_type=F32)
    n = z_ref.shape[0]
    hid = jnp.sin(hp(z_ref[...], w1_ref[...]) + b1_ref[...])
    hid = jnp.sin(hp(hid, w2_ref[...]) + b2_ref[...])
    pos = _iota((n, 1), 0).astype(F32) * (1.0 / n)
    window = jnp.exp(-pos * dl_ref[...]) + HYENA_SHIFT
    f0 = hp(hid, w3f_ref[...]) * window
    f1 = hp(hid, w3b_ref[...]) * window
    nrm = jnp.sum(jnp.abs(f0), axis=0, keepdims=True) + jnp.sum(jnp.abs(f1), axis=0, keepdims=True)
    hf_ref[...] = f0 / nrm
    hb_ref[...] = f1 / nrm


def _hyena_filters(n, w1, b1, w2, b2, w3, width):
    pos = np.arange(n, dtype=np.float64)
    bands = np.linspace(1e-4, HYENA_N_BANDS - 1, HYENA_N_BANDS)
    ang = (2.0 * math.pi / n) * pos[:, None] * bands
    z = np.concatenate([(pos / n)[:, None], np.cos(ang), np.sin(ang)], axis=-1)
    z = np.pad(z, ((0, 0), (0, LANES - z.shape[1]))).astype(np.float32)
    max_decay = math.log(HYENA_TGT) / HYENA_FAST
    min_decay = math.log(HYENA_TGT) / HYENA_SLOW
    deltas = np.abs(np.linspace(min_decay, max_decay, width)).astype(np.float32)[None]
    feat, hid = w1.shape
    w1p = jnp.pad(w1, ((0, LANES - feat), (0, LANES - hid)))
    w2p = jnp.pad(w2, ((0, LANES - hid), (0, LANES - hid)))
    w3p = jnp.pad(w3, ((0, LANES - hid), (0, 0)))
    b1p = jnp.pad(b1, (0, LANES - hid)).reshape(1, LANES)
    b2p = jnp.pad(b2, (0, LANES - hid)).reshape(1, LANES)
    n_tiles = width // LANES
    full = lambda shape: pl.BlockSpec(shape, lambda j: (0, 0))
    out = pl.BlockSpec((n, LANES), lambda j: (0, j))
    hf, hb = pl.pallas_call(
        _hyena_filter_kernel,
        grid=(n_tiles,),
        in_specs=[full((n, LANES)), full((LANES, LANES)), full((1, LANES)), full((LANES, LANES)), full((1, LANES)),
                  pl.BlockSpec((LANES, LANES), lambda j: (0, j)),
                  pl.BlockSpec((LANES, LANES), lambda j: (0, n_tiles + j)),
                  pl.BlockSpec((1, LANES), lambda j: (0, j))],
        out_specs=[out, out],
        out_shape=[jax.ShapeDtypeStruct((n, width), F32)] * 2,
        compiler_params=_compiler_params(("arbitrary",)),
        name="hyena_filters",
    )(jnp.asarray(z), w1p, b1p, w2p, b2p, w3p, w3p, jnp.asarray(deltas))
    return jnp.concatenate([hf, hb], axis=1)


def _hyena_pre_kernel(yv_ref, y0_ref, y1_ref, yz_ref, swv_ref, sw0_ref, sw1_ref, sbv_ref, sb0_ref, sb1_ref,
                      yb_ref, p_ref, e_ref, pb_ref, *, n_ctx):
    s = yv_ref.shape[1]
    rows = MIX_CHUNK

    def body(j, carry):
        t0 = pl.multiple_of(j * rows, rows)
        conv = []
        for src, sw, sb in ((yv_ref, swv_ref, sbv_ref), (y0_ref, sw0_ref, sb0_ref), (y1_ref, sw1_ref, sb1_ref)):
            cur, down, up = _chunk_with_neighbors(src, t0, rows, s, s - n_ctx)
            conv.append(down * sw[0:1, :] + cur * sw[1:2, :] + up * sw[2:3, :] + sb[...])
        v, x0, x1 = conv
        p = x1 * v
        o0 = pl.multiple_of(j * rows, rows)
        p_ref[0, pl.ds(o0, rows), :] = p.astype(BF16)
        pb_ref[0, pl.ds(o0, rows), :] = p * yb_ref[...]
        e_ref[0, pl.ds(o0, rows), :] = x0 * _silu(yz_ref[0, pl.ds(t0, rows), :].astype(F32))
        return carry

    lax.fori_loop(0, (s - n_ctx) // rows, body, 0)


def _hyena_pre(u, col0, short_w, short_b, y_bias, n_ctx):
    bsz, s, _ = u.shape
    w = y_bias.shape[0]
    tiles = w // LANES
    base = col0 // LANES
    n = s - n_ctx
    col = lambda k: pl.BlockSpec((1, s, LANES), lambda b, j, k=k: (b, 0, base + k * tiles + j))
    par = lambda rows, k: pl.BlockSpec((rows, LANES), lambda b, j, k=k: (0, k * tiles + j))
    out = pl.BlockSpec((1, n, LANES), lambda b, j: (b, 0, j))
    sb = short_b.reshape(1, 3 * w)
    return pl.pallas_call(
        functools.partial(_hyena_pre_kernel, n_ctx=n_ctx),
        grid=(bsz, tiles),
        in_specs=[col(0), col(1), col(2), col(3),
                  par(3, 0), par(3, 1), par(3, 2), par(1, 0), par(1, 1), par(1, 2), par(1, 0)],
        out_specs=[out, out, out],
        out_shape=[jax.ShapeDtypeStruct((bsz, n, w), BF16),
                   jax.ShapeDtypeStruct((bsz, n, w), F32),
                   jax.ShapeDtypeStruct((bsz, n, w), F32)],
        compiler_params=_compiler_params(("parallel", "arbitrary")),
        name="hyena_short_conv",
    )(u, u, u, u, short_w, short_w, short_w, sb, sb, sb, y_bias.reshape(1, w))


def _dft_tables(n):
    big = 2 * n
    half = DFT_TILE // 2
    idx = jnp.arange(n, dtype=jnp.int32)
    split = DFT_SPLIT
    lo = jnp.arange(split, dtype=jnp.int32)
    hi = jnp.arange(n // split, dtype=jnp.int32)
    ang_lo = ((lo[:, None] * idx[None, :]) % big).astype(F32) * (2.0 * math.pi / big)
    ang_hi = ((hi[:, None] * idx[None, :]) % (big // split)).astype(F32) * (2.0 * math.pi * split / big)
    c_lo, s_lo = jnp.cos(ang_lo)[None], jnp.sin(ang_lo)[None]
    c_hi, s_hi = jnp.cos(ang_hi)[:, None], jnp.sin(ang_hi)[:, None]
    cos = (c_hi * c_lo - s_hi * s_lo).reshape(n, n)
    sin = (s_hi * c_lo + c_hi * s_lo).reshape(n, n)
    alt = jnp.where(idx % 2 == 0, 1.0, -1.0).astype(F32)
    first_row = (idx == 0)[:, None]
    im = jnp.where(first_row, alt[None, :], -sin)
    fwd = jnp.stack([cos.reshape(n // half, half, n), im.reshape(n // half, half, n)], axis=1).reshape(big, n)
    weight = jnp.where(first_row, 1.0, 2.0) * (1.0 / big)
    weight = jnp.stack([weight.reshape(n // half, half, 1)] * 2, axis=1).reshape(big, 1)
    return fwd.astype(BF16), (fwd * weight).T.astype(BF16)


def _spectrum_kernel(f_ref, lo_ref, hi_ref, o_ref):
    acc_lo = jnp.dot(f_ref[...], lo_ref[...].astype(BF16), preferred_element_type=F32)
    acc_hi = jnp.dot(f_ref[...], hi_ref[...].astype(BF16), preferred_element_type=F32)
    odd = _iota(acc_lo.shape, 0) % 2 == 1
    o_ref[...] = acc_lo + jnp.where(odd, -acc_hi, acc_hi)


def _filter_spectrum(fwd, hk):
    big, h = fwd.shape
    w = hk.shape[1] // 2
    hf, hb = hk[:, :w], hk[:, w:]
    zero = jnp.zeros((1, w), F32)
    diag_lo = jnp.concatenate([hf[0:1] + hb[0:1], hf[1:h]], axis=0)
    diag_hi = jnp.concatenate([zero, hb[1:h][::-1]], axis=0)
    below_lo = hf[h:]
    below_hi = jnp.concatenate([zero, hf[1:h]], axis=0)
    above_lo = hb[1:h + 1][::-1]
    above_hi = jnp.concatenate([zero, hb[h + 1:][::-1]], axis=0)
    lo = jnp.concatenate([diag_lo, below_lo, above_lo], axis=1)
    hi = jnp.concatenate([diag_hi, below_hi, above_hi], axis=1)
    return pl.pallas_call(
        _spectrum_kernel,
        grid=(big // DFT_TILE,),
        in_specs=[pl.BlockSpec((DFT_TILE, h), lambda i: (i, 0)),
                  pl.BlockSpec((h, 3 * w), lambda i: (0, 0)),
                  pl.BlockSpec((h, 3 * w), lambda i: (0, 0))],
        out_specs=pl.BlockSpec((DFT_TILE, 3 * w), lambda i: (i, 0)),
        out_shape=jax.ShapeDtypeStruct((big, 3 * w), F32),
        compiler_params=_compiler_params(("arbitrary",)),
        name="hyena_filter_spectrum",
    )(fwd, lo, hi)


def _spectral_products(pairs, packed):
    re = sum(s[0] * k[0] - s[1] * k[1] for s, k in pairs)
    im = sum(s[0] * k[1] + s[1] * k[0] for s, k in pairs)
    if packed is not None:
        re = jnp.where(packed, sum(s[0] * k[0] for s, k in pairs), re)
        im = jnp.where(packed, sum(s[1] * k[1] for s, k in pairs), im)
    return jnp.concatenate([re, im], axis=0).astype(BF16)


def _conv_spectrum_kernel(f_ref, pa_ref, pb_ref, ks_ref, z_ref):
    i = pl.program_id(1)
    half = DFT_TILE // 2
    w = pa_ref.shape[2]
    units = [slice(k * DFT_TILE, (k + 1) * DFT_TILE) for k in range(DFT_UNITS)]
    accs = [(jnp.dot(f_ref[u, :], pa_ref[0], preferred_element_type=F32),
             jnp.dot(f_ref[u, :], pb_ref[0], preferred_element_type=F32)) for u in units]
    for k, (u, (acc_a, acc_b)) in enumerate(zip(units, accs)):
        s_a, s_b = (acc_a[:half], acc_a[half:]), (acc_b[:half], acc_b[half:])
        ks = ks_ref[u, :]
        k_diag, k_below, k_above = [(ks[:half, j * w:(j + 1) * w], ks[half:, j * w:(j + 1) * w]) for j in range(3)]
        packed = jnp.logical_and(_iota((half, w), 0) == 0, i == 0) if k == 0 else None
        z_ref[0, 0, u, :] = _spectral_products([(s_a, k_diag), (s_b, k_above)], packed)
        z_ref[0, 1, u, :] = _spectral_products([(s_a, k_below), (s_b, k_diag)], packed)


def _conv_spectrum(fwd, p16, kspec):
    bsz, n, w = p16.shape
    big, h = fwd.shape
    assert n == 2 * h
    rows = DFT_TILE * DFT_UNITS
    return pl.pallas_call(
        _conv_spectrum_kernel,
        grid=(bsz, big // rows),
        in_specs=[pl.BlockSpec((rows, h), lambda b, i: (i, 0)),
                  pl.BlockSpec((1, h, w), lambda b, i: (b, 0, 0)),
                  pl.BlockSpec((1, h, w), lambda b, i: (b, 1, 0)),
                  pl.BlockSpec((rows, 3 * w), lambda b, i: (i, 0))],
        out_specs=pl.BlockSpec((1, 2, rows, w), lambda b, i: (b, 0, i, 0)),
        out_shape=jax.ShapeDtypeStruct((bsz, 2, big, w), BF16),
        compiler_params=_compiler_params(("parallel", "arbitrary")),
        name="hyena_forward_dft",
    )(fwd, p16, p16, kspec)


def _conv_inverse_kernel(g_ref, z_ref, e_ref, pb_ref, o_ref):
    y = jnp.dot(g_ref[...], z_ref[0, 0], preferred_element_type=F32)
    o_ref[0] = (e_ref[0] * (y + pb_ref[0])).astype(o_ref.dtype)


def _conv_inverse(inv, z16, e, pb):
    bsz, _, big, w = z16.shape
    h = inv.shape[0]
    tile = DFT_TILE
    per_half = h // tile
    tok = pl.BlockSpec((1, tile, w), lambda b, j, i: (b, j * per_half + i, 0))
    return pl.pallas_call(
        _conv_inverse_kernel,
        grid=(bsz, 2, per_half),
        in_specs=[pl.BlockSpec((tile, big), lambda b, j, i: (i, 0)),
                  pl.BlockSpec((1, 1, big, w), lambda b, j, i: (b, j, 0, 0)),
                  tok, tok],
        out_specs=tok,
        out_shape=jax.ShapeDtypeStruct((bsz, 2 * h, w), BF16),
        compiler_params=_compiler_params(("parallel", "arbitrary", "arbitrary")),
        name="hyena_inverse_dft",
    )(inv, z16, e, pb)


def _even_weight_layout(w_in, gate_b):
    d = w_in.shape[0]
    mw = MLSTM_N_HEADS * LANES
    g0 = 5 * mw
    g1 = g0 + 4 * MLSTM_N_HEADS
    main = jnp.concatenate([w_in[:, :g0], w_in[:, g1:]], axis=1).astype(BF16)
    wg = w_in[:, g0:g1].reshape(d, 2, 2, MLSTM_N_HEADS)
    wg = jnp.transpose(wg, (3, 1, 2, 0)).reshape(MLSTM_N_HEADS, 4, d)
    wg = jnp.concatenate([wg, jnp.zeros_like(wg)], axis=1).reshape(MLSTM_N_HEADS * 8, d).astype(BF16)
    gb = jnp.transpose(gate_b.reshape(2, 2, MLSTM_N_HEADS), (2, 0, 1)).reshape(MLSTM_N_HEADS, 4)
    gb = jnp.concatenate([gb, jnp.zeros_like(gb)], axis=1).reshape(MLSTM_N_HEADS * 8, 1)
    return main, wg, jnp.broadcast_to(gb, (MLSTM_N_HEADS * 8, LANES))


def _rwkv_params(mu, w0, w_up, a0, a_up, k_k, k_a, r_k, ln_w, ln_b):
    width = mu.shape[1]
    n_pairs = width // LANES
    row = lambda x: x.reshape(1, width)

    def cat_dirs(x):
        return jnp.transpose(x.reshape(2, n_pairs, LANES), (1, 0, 2)).reshape(n_pairs, 1, 2 * LANES)

    def block_up(x):
        lora = x.shape[1]
        xp = jnp.transpose(x.reshape(2, lora, n_pairs, LANES), (2, 0, 1, 3))
        z = jnp.zeros_like(xp[:, 0])
        top = jnp.concatenate([xp[:, 0], z], axis=2)
        bot = jnp.concatenate([z, xp[:, 1]], axis=2)
        return jnp.concatenate([top, bot], axis=1).astype(BF16)

    return {"mu": mu, "k_k": row(k_k), "k_a": row(k_a), "r_k": row(r_k), "ln_w": row(ln_w), "ln_b": row(ln_b),
            "w0": cat_dirs(w0), "a0": cat_dirs(a0), "w_up": block_up(w_up), "a_up": block_up(a_up)}


def _raster_to_column(h):
    b, n, d = h.shape
    rows = n // GRID_WIDTH
    return h.reshape(b, rows, GRID_WIDTH, d).transpose(0, 2, 1, 3).reshape(b, n, d)


def _column_to_raster(h):
    b, n, d = h.shape
    rows = n // GRID_WIDTH
    return h.reshape(b, GRID_WIDTH, rows, d).transpose(0, 2, 1, 3).reshape(b, n, d)


def kernel(x, c, ctx, c_ctx, l0_norm_g, l0_mod_w, l0_mod_b, l0_w_in, l0_w_out, l0_mlstm_conv_w, l0_mlstm_gate_b, l0_mlstm_norm_g, l0_rwkv_mu, l0_rwkv_w0, l0_rwkv_w_up, l0_rwkv_a0, l0_rwkv_a_up, l0_rwkv_k_k, l0_rwkv_k_a, l0_rwkv_r_k, l0_rwkv_ln_w, l0_rwkv_ln_b, hgrn_lower_bounds, l1_norm_g, l1_mod_w, l1_mod_b, l1_w_in, l1_w_out, l1_hgrn_norm_g, l1_hyena_short_w, l1_hyena_short_b, l1_hyena_w1, l1_hyena_b1, l1_hyena_w2, l1_hyena_b2, l1_hyena_w3, l1_hyena_bias, final_norm_g):
    bsz, n_lat, d = x.shape
    n_ctx = ctx.shape[1]

    pad = (-(bsz + 1)) % 8
    cc = jnp.concatenate([c, c_ctx[None], jnp.zeros((pad, d), F32)], axis=0)
    mod0, mod1 = _modulation(cc, l0_mod_w, l0_mod_b, l1_mod_w, l1_mod_b)
    mod0 = mod0[:bsz + 1].reshape(bsz + 1, 3, d)
    mod1 = mod1[:bsz + 1].reshape(bsz + 1, 3, d)

    w_main, w_gate, b_gate = _even_weight_layout(l0_w_in, l0_mlstm_gate_b)
    n0 = w_main.shape[1]
    u0, gt0 = _proj_in(x, ctx, l0_norm_g, mod0, w_main, n0 // 2, w_gate, b_gate)
    gt0 = gt0.reshape(bsz, MLSTM_N_HEADS, 8, n_ctx + n_lat)
    y_m = _mlstm(u0, gt0, l0_mlstm_conv_w, l0_mlstm_norm_g, n_ctx)
    rp = _rwkv_params(l0_rwkv_mu, l0_rwkv_w0, l0_rwkv_w_up, l0_rwkv_a0, l0_rwkv_a_up, l0_rwkv_k_k,
                      l0_rwkv_k_a, l0_rwkv_r_k, l0_rwkv_ln_w, l0_rwkv_ln_b)
    y_r = _rwkv(u0, 5 * MLSTM_N_HEADS * LANES, rp, n_ctx)
    x1, ctx1 = _proj_out(y_m, y_r, x, ctx, mod0, l0_w_out.astype(BF16))

    x1c = _raster_to_column(x1)
    w1 = l1_w_in.astype(BF16)
    (u1,) = _proj_in(x1c, ctx1, l1_norm_g, mod1, w1, w1.shape[1] // 2)
    y_g = _hgrn(u1, hgrn_lower_bounds, l1_hgrn_norm_g, n_ctx, layer=1)
    hw = l1_hyena_bias.shape[0]
    hk = _hyena_filters(n_lat, l1_hyena_w1, l1_hyena_b1, l1_hyena_w2, l1_hyena_b2, l1_hyena_w3, hw)
    fwd, inv = _dft_tables(n_lat // 2)
    kspec = _filter_spectrum(fwd, hk)
    p16, e, pb = _hyena_pre(u1, 5 * HGRN_N_HEADS * LANES, l1_hyena_short_w, l1_hyena_short_b, l1_hyena_bias, n_ctx)
    z16 = _conv_spectrum(fwd, p16, kspec)
    y_y = _conv_inverse(inv, z16, e, pb)
    out_c = _proj_out_final(y_g, y_y, x1c, mod1, l1_w_out.astype(BF16), final_norm_g)
    return _column_to_raster(out_c)
```

```python
import functools
import math

import jax
import jax.numpy as jnp
import numpy as np
from jax import lax
from jax.experimental import pallas as pl
from jax.experimental.pallas import tpu as pltpu

F32 = jnp.float32
BF16 = jnp.bfloat16

GRID_WIDTH = 64
RMS_EPS = 1e-6
MLSTM_N_HEADS = 4
RWKV_HEAD = 64
RWKV_LN_EPSILON = 64e-5
HGRN_N_HEADS = 4
HYENA_N_BANDS = 16
HYENA_FAST = 0.3
HYENA_SLOW = 1.5
HYENA_TGT = 1e-2
HYENA_SHIFT = 0.05
LOG2_E = 1.0 / math.log(2.0)

LANES = 128
SUBLANES = 8
ROW_GROUP = 16
MXU_DIM = 256
VMEM_LIMIT = 52 * 1024 * 1024

MIX_CHUNK = 128
RWKV_CHUNK = 64
RWKV_GROUP = 12
MIX_PREP_ROWS = 256
MIX_FINISH_ROWS = 256
RWKV_PREP_ROWS = 256
RWKV_FINISH_ROWS = 768
RWKV_SCAN_UNROLL = 3
SCAN_UNROLL = 3
MLSTM_GROUP = 6
HGRN_GROUP = 3
PROJ_ROWS = 768
FINAL_ROWS = 512
DFT_TILE = 512
DFT_UNITS = 2
DFT_SPLIT = 64


def _bdot(a, b):
    return jnp.dot(a.astype(BF16), b.astype(BF16), preferred_element_type=F32)


def _bdot_nt(a, b):
    return lax.dot_general(a.astype(BF16), b.astype(BF16), (((1,), (1,)), ((), ())),
                           preferred_element_type=F32)


def _bdot_tn(a, b):
    return lax.dot_general(a.astype(BF16), b.astype(BF16), (((0,), (0,)), ((), ())),
                           preferred_element_type=F32)


def _split3(x):
    hi = x.astype(BF16)
    r1 = x - hi.astype(F32)
    mid = r1.astype(BF16)
    lo = (r1 - mid.astype(F32)).astype(BF16)
    return hi, mid, lo


def _sel_dot(sel, x):
    hi, mid, lo = _split3(x)
    d = functools.partial(jnp.dot, preferred_element_type=F32)
    return d(sel, hi) + d(sel, mid) + d(sel, lo)


def _dot_sel(x, sel):
    hi, mid, lo = _split3(x)
    d = functools.partial(jnp.dot, preferred_element_type=F32)
    return d(hi, sel) + d(mid, sel) + d(lo, sel)


def _sigmoid(x):
    return 1.0 / (1.0 + jnp.exp(-x))


def _silu(x):
    return x * _sigmoid(x)


def _iota(shape, dim):
    return lax.broadcasted_iota(jnp.int32, shape, dim)


def _neighbor_rows(ref, t0, rows, n_total, split):
    has_prev = jnp.logical_and(t0 != 0, t0 != split)
    has_next = jnp.logical_and(t0 + rows != split, t0 + rows != n_total)
    g = ROW_GROUP
    before = ref[0, pl.ds(pl.multiple_of(jnp.maximum(t0 - g, 0), g), g), :].astype(F32)
    after = ref[0, pl.ds(pl.multiple_of(jnp.minimum(t0 + rows, n_total - g), g), g), :].astype(F32)
    return jnp.where(has_prev, before[g - 1:g], 0.0), jnp.where(has_next, after[0:1], 0.0)


def _shifted(cur, prev_row, next_row):
    rows = cur.shape[0]
    rid = _iota(cur.shape, 0)
    down = jnp.where(rid == 0, prev_row, pltpu.roll(cur, 1, 0))
    up = jnp.where(rid == rows - 1, next_row, pltpu.roll(cur, rows - 1, 0))
    return down, up


def _chunk_with_neighbors(ref, t0, rows, n_total, split):
    cur = ref[0, pl.ds(t0, rows), :].astype(F32)
    prev_row, next_row = _neighbor_rows(ref, t0, rows, n_total, split)
    down, up = _shifted(cur, prev_row, next_row)
    return cur, down, up


def _scan_chunks(i, n_chunks, n_ctx_chunks):
    fwd = jnp.where(i < n_ctx_chunks, n_chunks - n_ctx_chunks + i, i - n_ctx_chunks)
    return fwd, n_chunks - 1 - i


def _compiler_params(semantics):
    return pltpu.CompilerParams(dimension_semantics=semantics, vmem_limit_bytes=VMEM_LIMIT)


def _mod_kernel(c_ref, w0_ref, b0_ref, w1_ref, b1_ref, o0_ref, o1_ref):
    s = _silu(c_ref[...])
    o0_ref[...] = _bdot(s, w0_ref[...]) + b0_ref[...]
    o1_ref[...] = _bdot(s, w1_ref[...]) + b1_ref[...]


def _modulation(cc, w0, b0, w1, b1):
    rows, d = cc.shape
    n = w0.shape[1]
    tile = d
    grid = (n // tile,)
    wspec = pl.BlockSpec((d, tile), lambda j: (0, j))
    bspec = pl.BlockSpec((1, tile), lambda j: (0, j))
    ospec = pl.BlockSpec((rows, tile), lambda j: (0, j))
    return pl.pallas_call(
        _mod_kernel,
        grid=grid,
        in_specs=[pl.BlockSpec((rows, d), lambda j: (0, 0)), wspec, bspec, wspec, bspec],
        out_specs=[ospec, ospec],
        out_shape=[jax.ShapeDtypeStruct((rows, n), F32)] * 2,
        compiler_params=_compiler_params(("arbitrary",)),
        name="adaln_modulation",
    )(cc, w0, b0.reshape(1, n), w1, b1.reshape(1, n))


def _token_tile(x_ref, c_ref, i, rows):
    n_lat_tail = rows - c_ref.shape[1]
    is_ctx = jnp.logical_and(i == pl.num_programs(1) - 1, _iota((rows, 1), 0) >= n_lat_tail)
    ctx_rows = jnp.concatenate([jnp.zeros((n_lat_tail, c_ref.shape[2]), F32), c_ref[0]], axis=0)
    return jnp.where(is_ctx, ctx_rows, x_ref[0]), is_ctx


def _proj_in_kernel(*refs, rows, with_gates):
    if with_gates:
        x_ref, c_ref, g_ref, ml_ref, mc_ref, w_ref, wg_ref, gb_ref, u_ref, gt_ref, h_scr = refs
    else:
        x_ref, c_ref, g_ref, ml_ref, mc_ref, w_ref, u_ref, h_scr = refs
    i = pl.program_id(1)
    n = pl.program_id(2)

    @pl.when(n == 0)
    def _():
        x, is_ctx = _token_tile(x_ref, c_ref, i, rows)
        y = x * lax.rsqrt(jnp.mean(x * x, axis=-1, keepdims=True) + RMS_EPS) * g_ref[...]
        ml = ml_ref[0]
        mc = mc_ref[0]
        shift = jnp.where(is_ctx, mc[0:1], ml[0:1])
        scale = jnp.where(is_ctx, mc[1:2], ml[1:2])
        h = (y * (1.0 + scale) + shift).astype(BF16)
        h_scr[...] = h
        if with_gates:
            gt_ref[0] = _bdot_nt(wg_ref[...], h) + gb_ref[:, 0:1]

    u_ref[0] = jnp.dot(h_scr[...], w_ref[...], preferred_element_type=F32).astype(u_ref.dtype)


def _proj_in(x, ctx, norm_g, mod3, w16, n_tile, gate_w=None, gate_b=None):
    bsz, n_lat, d = x.shape
    n_ctx = ctx.shape[1]
    s = n_lat + n_ctx
    n = w16.shape[1]
    rows = PROJ_ROWS
    assert s % rows == 0 and (n_lat % rows) + n_ctx == rows
    grid = (bsz, s // rows, n // n_tile)
    ctx_row = mod3.shape[0] - 1
    with_gates = gate_w is not None
    in_specs = [
        pl.BlockSpec((1, rows, d), lambda b, i, j: (b, i, 0)),
        pl.BlockSpec((1, n_ctx, d), lambda b, i, j: (b, 0, 0)),
        pl.BlockSpec((1, d), lambda b, i, j: (0, 0)),
        pl.BlockSpec((1, 3, d), lambda b, i, j: (b, 0, 0)),
        pl.BlockSpec((1, 3, d), lambda b, i, j: (ctx_row, 0, 0)),
        pl.BlockSpec((d, n_tile), lambda b, i, j: (0, j)),
    ]
    args = [x, ctx, norm_g.reshape(1, d), mod3, mod3, w16]
    out_specs = [pl.BlockSpec((1, rows, n_tile), lambda b, i, j: (b, i, j))]
    out_shape = [jax.ShapeDtypeStruct((bsz, s, n), BF16)]
    if with_gates:
        ng = gate_w.shape[0]
        in_specs += [pl.BlockSpec((ng, d), lambda b, i, j: (0, 0)),
                     pl.BlockSpec((ng, LANES), lambda b, i, j: (0, 0))]
        args += [gate_w, gate_b]
        out_specs.append(pl.BlockSpec((1, ng, rows), lambda b, i, j: (b, 0, i)))
        out_shape.append(jax.ShapeDtypeStruct((bsz, ng, s), F32))
    return pl.pallas_call(
        functools.partial(_proj_in_kernel, rows=rows, with_gates=with_gates),
        grid=grid,
        in_specs=in_specs,
        out_specs=out_specs,
        out_shape=out_shape,
        scratch_shapes=[pltpu.VMEM((rows, d), BF16)],
        compiler_params=_compiler_params(("parallel", "arbitrary", "arbitrary")),
        name="norm_mod_proj_in",
    )(*args)


def _proj_out_kernel(ya_ref, yb_ref, x_ref, c_ref, ml_ref, mc_ref, w_ref, ox_ref, oc_ref, *, rows):
    i = pl.program_id(1)
    half = ya_ref.shape[2]
    y = _bdot(ya_ref[0], w_ref[0:half, :]) + _bdot(yb_ref[0], w_ref[half:, :])
    x, is_ctx = _token_tile(x_ref, c_ref, i, rows)
    x = x + jnp.where(is_ctx, mc_ref[0][2:3], ml_ref[0][2:3]) * y
    ox_ref[0] = x

    @pl.when(i == pl.num_programs(1) - 1)
    def _():
        oc_ref[0] = x[rows - c_ref.shape[1]:]


def _proj_out(ya, yb, x, ctx, mod3, w16):
    bsz, n_lat, d = x.shape
    n_ctx = ctx.shape[1]
    s = n_lat + n_ctx
    half = ya.shape[2]
    rows = PROJ_ROWS
    assert s % rows == 0 and (n_lat % rows) + n_ctx == rows
    ctx_row = mod3.shape[0] - 1
    tok = lambda w: pl.BlockSpec((1, rows, w), lambda b, i: (b, i, 0))
    seg = pl.BlockSpec((1, n_ctx, d), lambda b, i: (b, 0, 0))
    return pl.pallas_call(
        functools.partial(_proj_out_kernel, rows=rows),
        grid=(bsz, s // rows),
        in_specs=[tok(half), tok(half), tok(d), seg,
                  pl.BlockSpec((1, 3, d), lambda b, i: (b, 0, 0)),
                  pl.BlockSpec((1, 3, d), lambda b, i: (ctx_row, 0, 0)),
                  pl.BlockSpec((2 * half, d), lambda b, i: (0, 0))],
        out_specs=[tok(d), seg],
        out_shape=[jax.ShapeDtypeStruct((bsz, n_lat, d), F32), jax.ShapeDtypeStruct((bsz, n_ctx, d), F32)],
        compiler_params=_compiler_params(("parallel", "arbitrary")),
        name="proj_out_residual",
    )(ya, yb, x, ctx, mod3, mod3, w16)


def _proj_out_final_kernel(ya_ref, yb_ref, x_ref, ml_ref, w_ref, fg_ref, o_ref):
    half = ya_ref.shape[2]
    y = _bdot(ya_ref[0], w_ref[0:half, :]) + _bdot(yb_ref[0], w_ref[half:, :])
    x = x_ref[0] + ml_ref[0][2:3] * y
    o_ref[0] = x * lax.rsqrt(jnp.mean(x * x, axis=-1, keepdims=True) + RMS_EPS) * fg_ref[...]


def _proj_out_final(ya, yb, x, mod3, w16, final_g):
    bsz, n_lat, d = x.shape
    half = ya.shape[2]
    rows = FINAL_ROWS
    assert n_lat % rows == 0
    tok = lambda w: pl.BlockSpec((1, rows, w), lambda b, i: (b, i, 0))
    return pl.pallas_call(
        _proj_out_final_kernel,
        grid=(bsz, n_lat // rows),
        in_specs=[tok(half), tok(half), tok(d),
                  pl.BlockSpec((1, 3, d), lambda b, i: (b, 0, 0)),
                  pl.BlockSpec((2 * half, d), lambda b, i: (0, 0)),
                  pl.BlockSpec((1, d), lambda b, i: (0, 0))],
        out_specs=tok(d),
        out_shape=jax.ShapeDtypeStruct((bsz, n_lat, d), F32),
        compiler_params=_compiler_params(("parallel", "arbitrary")),
        name="proj_out_final_norm",
    )(ya, yb, x, mod3, w16, final_g.reshape(1, d))


def _mlstm_chunk_operators(chunks, causal):
    t = chunks[0][0].shape[0]
    lane = _iota((8, t), 1)
    row_id = _iota((8, t), 0)
    log_fs = [jnp.minimum(c[3], 0.0) - jnp.log1p(jnp.exp(-jnp.abs(c[3]))) for c in chunks]
    cum_f, cum_b = list(log_fs), list(log_fs)
    sh = 1
    while sh < t:
        cum_f = [x + jnp.where(lane >= sh, pltpu.roll(x, sh, 1), 0.0) for x in cum_f]
        cum_b = [x + jnp.where(lane < t - sh, pltpu.roll(x, t - sh, 1), 0.0) for x in cum_b]
        sh *= 2
    pad = jnp.zeros((t - 8, t), F32)
    tiles = [jnp.concatenate([jnp.where(row_id % 2 == 0, c[3], jnp.where(row_id == 1, f, b)), pad], axis=0)
             for c, f, b in zip(chunks, cum_f, cum_b)]
    cols = [x.T for x in tiles]
    problems = []
    for c, f, b, col in zip(chunks, cum_f, cum_b, cols):
        for d in range(2):
            b_row = (f, b)[d][2 * d + 1:2 * d + 2]
            problems.append(dict(q=c[0], k=c[1], v_ext=c[2], d=d, ig_row=c[3][2 * d:2 * d + 1], b_row=b_row,
                                 ig_col=col[:, 2 * d:2 * d + 1], b_col=col[:, 2 * d + 1:2 * d + 2]))
    logws = [jnp.where(causal[p["d"]], p["b_col"] + (p["ig_row"] - p["b_row"]), -jnp.inf) for p in problems]
    mus = [jnp.max(x, axis=-1, keepdims=True) for x in logws]
    ws = [jnp.exp(x - mu) for x, mu in zip(logws, mus)]
    lasts = [0 if p["d"] == 1 else t - 1 for p in problems]
    b_lasts = [p["b_col"][i:i + 1] for p, i in zip(problems, lasts)]
    gammas = [mu[i:i + 1] for mu, i in zip(mus, lasts)]
    gks = [jnp.exp(bl - p["b_col"] + p["ig_col"] - gm) * p["k"] for p, bl, gm in zip(problems, b_lasts, gammas)]
    qks = [_bdot_nt(p["q"], p["k"]) * w for p, w in zip(problems, ws)]
    intras = [_bdot(qk, p["v_ext"]) for qk, p in zip(qks, problems)]
    kvs = [_bdot_tn(gk, p["v_ext"]) for gk, p in zip(gks, problems)]
    dh = chunks[0][0].shape[1]
    return [(intra, kv, jnp.broadcast_to(mu - p["b_col"], (t, dh)), jnp.broadcast_to(mu, (t, dh)), bl, gm)
            for intra, kv, p, mu, bl, gm in zip(intras, kvs, problems, mus, b_lasts, gammas)]


def _mlstm_kernel(q_ref, k_ref, v_ref, o_ref, z_ref, gt_ref, cwq_ref, cwk_ref, ng_ref, out_ref,
                  qa_scr, ka_scr, h_scr, intra_scr, kv_scr, delta_scr, mu_scr, tail_scr, *, n_ctx):
    s = q_ref.shape[1]
    dh = q_ref.shape[2]
    t = MIX_CHUNK
    n_chunks = s // t
    n_ctx_chunks = n_ctx // t
    k_scale = dh ** -0.5

    p_rows = MIX_PREP_ROWS

    def prep(j, carry):
        t0 = pl.multiple_of(j * p_rows, p_rows)
        for src, cw, dst, scale in ((q_ref, cwq_ref, qa_scr, 1.0), (k_ref, cwk_ref, ka_scr, k_scale)):
            cur, down, up = _chunk_with_neighbors(src, t0, p_rows, s, s - n_ctx)
            conv = down * cw[0:1, :] + cur * cw[1:2, :] + up * cw[2:3, :]
            dst[pl.ds(t0, p_rows), :] = _silu(conv) * scale
        return carry

    lax.fori_loop(0, s // p_rows, prep, 0)

    ones_col = jnp.ones((t, dh), F32)
    causal = [_iota((t, t), 1) <= _iota((t, t), 0), _iota((t, t), 1) >= _iota((t, t), 0)]

    def operators(gi, carry):
        chunks, where = [], []
        for kk in range(MLSTM_GROUP):
            chunk = gi * MLSTM_GROUP + kk
            sl = pl.ds(pl.multiple_of(chunk * t, t), t)
            v_ext = jnp.concatenate([v_ref[0, sl, :].astype(F32), ones_col], axis=1)
            chunks.append((qa_scr[sl, :], ka_scr[sl, :], v_ext, gt_ref[0, 0, :, sl]))
            where += [(0, chunk, sl), (1, chunk, sl)]
        for (d, chunk, sl), (intra, kv, delta, mu, b_last, gamma) in zip(where, _mlstm_chunk_operators(chunks, causal)):
            intra_scr[d, sl, :] = intra
            kv_scr[d, chunk] = kv
            delta_scr[d, sl, :] = delta
            mu_scr[d, sl, :] = mu
            tail_scr[d, chunk] = jnp.concatenate([jnp.broadcast_to(b_last, (1, dh)), jnp.broadcast_to(gamma, (1, dh))],
                                                 axis=0)
        return carry

    lax.fori_loop(0, n_chunks // MLSTM_GROUP, operators, 0)

    def scan(trip, carry):
        steps = []
        for k in range(SCAN_UNROLL):
            chunks = _scan_chunks(trip * SCAN_UNROLL + k, n_chunks, n_ctx_chunks)
            new = []
            for d, (chunk, (c_ext, m)) in enumerate(zip(chunks, carry)):
                steps.append((d, chunk, pl.ds(pl.multiple_of(chunk * t, t), t), c_ext, m))
                tail = tail_scr[d, chunk]
                b_last, gamma = tail[0:1, 0:1], tail[1:2, 0:1]
                m_new = jnp.maximum(b_last + m, gamma)
                new.append((jnp.exp(b_last + m - m_new) * c_ext + jnp.exp(gamma - m_new) * kv_scr[d, chunk], m_new))
            carry = tuple(new)
        inters = [_bdot(qa_scr[sl, :], c_ext) for _, _, sl, c_ext, _ in steps]
        for (d, chunk, sl, _, m), inter in zip(steps, inters):
            z = delta_scr[d, sl, :] - m
            s_inter = jnp.exp(-jnp.maximum(z, 0.0))
            s_intra = jnp.exp(jnp.minimum(z, 0.0))
            floor = jnp.exp(jnp.minimum(z, 0.0) - mu_scr[d, sl, :])
            intra = intra_scr[d, sl, :]
            num = s_inter * inter[:, :dh] + s_intra * intra[:, :dh]
            den = s_inter * inter[:, dh:] + s_intra * intra[:, dh:]
            h_scr[d, sl, :] = num / jnp.maximum(jnp.abs(den), floor)
        return carry

    zero = (jnp.zeros((dh, 2 * dh), F32), jnp.zeros((1, 1), F32))
    lax.fori_loop(0, n_chunks // SCAN_UNROLL, scan, (zero, zero))

    f_rows = MIX_FINISH_ROWS

    def finish(j, carry):
        sl = pl.ds(pl.multiple_of(j * f_rows, f_rows), f_rows)
        h = h_scr[0, sl, :] + h_scr[1, sl, :]
        y = h * lax.rsqrt(jnp.mean(h * h, axis=-1, keepdims=True) + RMS_EPS) * ng_ref[...]
        gated = y * _sigmoid(o_ref[0, sl, :].astype(F32)) * _silu(z_ref[0, sl, :].astype(F32))
        out_ref[0, sl, :] = gated.astype(out_ref.dtype)
        return carry

    lax.fori_loop(0, s // f_rows, finish, 0)


def _mlstm(u, gt, conv_w, norm_g, n_ctx):
    bsz, s, _ = u.shape
    nh = MLSTM_N_HEADS
    dh = LANES
    width = nh * dh
    n_chunks = s // MIX_CHUNK
    assert n_chunks % MLSTM_GROUP == 0 and n_chunks % SCAN_UNROLL == 0 and n_ctx % MIX_CHUNK == 0
    col = lambda k: pl.BlockSpec((1, s, dh), lambda b, h, k=k: (b, 0, k * nh + h))
    par = lambda k: pl.BlockSpec((3, dh), lambda b, h, k=k: (0, k * nh + h))
    return pl.pallas_call(
        functools.partial(_mlstm_kernel, n_ctx=n_ctx),
        grid=(bsz, nh),
        in_specs=[col(0), col(1), col(2), col(3), col(4),
                  pl.BlockSpec((1, 1, 8, s), lambda b, h: (b, h, 0, 0)),
                  par(0), par(1),
                  pl.BlockSpec((1, dh), lambda b, h: (0, h))],
        out_specs=pl.BlockSpec((1, s, dh), lambda b, h: (b, 0, h)),
        out_shape=jax.ShapeDtypeStruct((bsz, s, width), BF16),
        scratch_shapes=[pltpu.VMEM((s, dh), F32), pltpu.VMEM((s, dh), F32), pltpu.VMEM((2, s, dh), F32),
                        pltpu.VMEM((2, s, 2 * dh), F32), pltpu.VMEM((2, n_chunks, dh, 2 * dh), F32),
                        pltpu.VMEM((2, s, dh), F32), pltpu.VMEM((2, s, dh), F32),
                        pltpu.VMEM((2, n_chunks, 2, dh), F32)],
        compiler_params=_compiler_params(("parallel", "arbitrary")),
        name="mlstm_mixer",
    )(u, u, u, u, u, gt, conv_w, conv_w, norm_g.reshape(1, width))


def _head_stack(x, lane_lo):
    return jnp.concatenate([jnp.where(lane_lo, x, 0.0), jnp.where(lane_lo, 0.0, x)], axis=0)


def _half_rows(x, c, upper):
    start = c if upper else 0
    return jnp.concatenate([x[r + start:r + start + c] for r in range(0, x.shape[0], 2 * c)], axis=0)


def _merge_rows(other, part, c, upper):
    pieces = []
    for k in range(part.shape[0] // c):
        pair = (other[k * c:(k + 1) * c], part[k * c:(k + 1) * c])
        pieces += pair if upper else pair[::-1]
    return jnp.concatenate(pieces, axis=0)


def _spread_rows(part, c, upper):
    return _merge_rows(jnp.zeros_like(part), part, c, upper)


def _rwkv_chunk_operators(problems, consts, eye, lane_lo):
    t, w = problems[0][0].shape
    n2 = 2 * t
    stack = lambda x: _head_stack(x, lane_lo)
    zeros = jnp.zeros((n2, w), F32)
    dirs = [p[6] for p in problems]
    rid = _iota((t, w), 0)
    cums = [p[3] for p in problems]
    sh = 1
    while sh < t:
        cums = [x + (jnp.where(rid < t - sh, pltpu.roll(x, t - sh, 0), 0.0) if d == 1 else
                     jnp.where(rid >= sh, pltpu.roll(x, sh, 0), 0.0)) for x, d in zip(cums, dirs)]
        sh *= 2
    pre = []
    for (r, v, kk, lw, ka, kt, d), cum in zip(problems, cums):
        last = 0 if d == 1 else t - 1
        cum_end = cum[last:last + 1]
        e_inv = jnp.exp(-cum)
        e_end = jnp.exp(cum_end - cum)
        a_s = stack(-kk * jnp.exp(cum - lw))
        r_s = stack(r * jnp.exp(cum))
        pre.append(dict(a_s=a_s, r_s=r_s, vs=stack(v), g=jnp.exp(cum_end),
                        ar=jnp.concatenate([a_s, r_s], axis=0),
                        bk=jnp.concatenate([stack(ka * e_inv), stack(kt * e_inv)], axis=0),
                        bk_end=jnp.concatenate([stack(ka * e_end), stack(kt * e_end)], axis=0)))
    m_alls = [_bdot_nt(q["ar"], q["bk"]) for q in pre]
    m_abs = [jnp.where(consts[d]["strict"], m[:n2, :n2], 0.0) for m, d in zip(m_alls, dirs)]
    m_aks = [jnp.where(consts[d]["strict"], m[:n2, n2:], 0.0) for m, d in zip(m_alls, dirs)]
    m_lows = [jnp.where(consts[d]["incl2"], m[n2:, :], 0.0) for m, d in zip(m_alls, dirs)]
    invs = [eye + jnp.where(consts[d]["merges"][0][1], m, 0.0) for m, d in zip(m_abs, dirs)]
    for level in range(1, len(consts[0]["merges"])):
        c = consts[0]["merges"][level][0]
        if c < SUBLANES:
            inner = [_bdot(jnp.where(consts[d]["merges"][level][1], m, 0.0), x) for m, x, d in zip(m_abs, invs, dirs)]
            invs = [x + _bdot(x, y) for x, y in zip(invs, inner)]
        else:
            ups = [d == 0 for d in dirs]
            c_rows = [jnp.where(consts[d]["merges"][level][1], _half_rows(m, c, up), 0.0) for m, d, up in zip(m_abs, dirs, ups)]
            inner = [_bdot(cr, x) for cr, x in zip(c_rows, invs)]
            x_rows = [_half_rows(x, c, up) for x, up in zip(invs, ups)]
            upd = [xr + _bdot(xr, _spread_rows(y, c, up)) for xr, y, up in zip(x_rows, inner, ups)]
            invs = [_merge_rows(_half_rows(x, c, not up), u, c, up) for x, u, up in zip(invs, upd, ups)]
    mv = [_bdot(m, q["vs"]) for m, q in zip(m_aks, pre)]
    solved = [_bdot(x, jnp.concatenate([q["a_s"], y], axis=1)) for x, q, y in zip(invs, pre, mv)]
    zms = [jnp.concatenate([sv, jnp.concatenate([zeros, q["vs"]], axis=1)], axis=0) for sv, q in zip(solved, pre)]
    ry1s = [jnp.concatenate([q["r_s"], zeros], axis=1) + _bdot(m, z) for q, m, z in zip(pre, m_lows, zms)]
    pqs = [_bdot_tn(z, q["bk_end"]) for z, q in zip(zms, pre)]
    out = []
    for ry1, pq, q in zip(ry1s, pqs, pre):
        folded = ry1[:t] + ry1[t:]
        out.append((folded[:, :w], folded[:, w:], pq[:w], pq[w:], q["g"]))
    return out


def _rwkv_kernel(rr_ref, rk_ref, rv_ref, rz_ref, wd_ref, ad_ref, mu_ref, kk_ref, ka_ref, rkk_ref,
                 lnw_ref, lnb_ref, w0_ref, a0_ref, wup_ref, aup_ref, out_ref,
                 r_scr, v_scr, kk_scr, lw_scr, ka_scr, kt_scr, bonus_scr, y_scr, ry_scr, pp_scr, qq_scr, qt_scr, g_scr,
                 *, n_ctx):
    s = rr_ref.shape[1]
    w = rr_ref.shape[2]
    p_rows = RWKV_PREP_ROWS
    t = RWKV_CHUNK
    n_chunks = s // t
    n_ctx_chunks = n_ctx // t
    head_sum = ((_iota((w, w), 0) // RWKV_HEAD) == (_iota((w, w), 1) // RWKV_HEAD)).astype(BF16)
    inv_head = 1.0 / RWKV_HEAD

    def prep(j, carry):
        t0 = pl.multiple_of(j * p_rows, p_rows)
        mixed = []
        for idx, src in enumerate((rr_ref, rk_ref, rv_ref)):
            cur, down, up = _chunk_with_neighbors(src, t0, p_rows, s, s - n_ctx)
            mixed.append(cur + mu_ref[idx:idx + 1, :] * (0.5 * (down + up) - cur))
        r, kr, v = mixed
        kk = kr * kk_ref[...]
        norm = jnp.sqrt(_dot_sel(kk * kk, head_sum))
        kk = kk / jnp.maximum(norm, 1e-12)
        w_raw = _bdot(jnp.tanh(wd_ref[0, pl.ds(t0, p_rows), :].astype(F32)), wup_ref[0]) + w0_ref[0]
        a = _sigmoid(_bdot(ad_ref[0, pl.ds(t0, p_rows), :], aup_ref[0]) + a0_ref[0])
        lw = -math.exp(-0.5) * _sigmoid(w_raw)
        kt_sum = jnp.zeros_like(kr)
        for d in range(2):
            a_d = a[:, d * w:(d + 1) * w]
            kt_d = kr * (1.0 + (a_d - 1.0) * ka_ref[...])
            kt_sum = kt_sum + kt_d
            lw_scr[d, pl.ds(t0, p_rows), :] = lw[:, d * w:(d + 1) * w]
            ka_scr[d, pl.ds(t0, p_rows), :] = kk * a_d
            kt_scr[d, pl.ds(t0, p_rows), :] = kt_d
        coef = _dot_sel(r * kt_sum * rkk_ref[...], head_sum)
        r_scr[pl.ds(t0, p_rows), :] = r
        v_scr[pl.ds(t0, p_rows), :] = v
        kk_scr[pl.ds(t0, p_rows), :] = kk
        bonus_scr[pl.ds(t0, p_rows), :] = coef * v
        return carry

    lax.fori_loop(0, s // p_rows, prep, 0)

    n2 = 2 * t
    r_i = _iota((n2, n2), 0)
    c_i = _iota((n2, n2), 1)
    same = (r_i // t) == (c_i // t)
    rt = r_i % t
    ct = c_i % t
    eye = (r_i == c_i).astype(F32)
    lane_lo = _iota((t, w), 1) < RWKV_HEAD
    consts = []
    for reverse in (False, True):
        strict = jnp.logical_and(same, (ct > rt) if reverse else (ct < rt))
        incl = jnp.logical_and(same, (ct >= rt) if reverse else (ct <= rt))
        merges = []
        c = 1
        while c < t:
            hi_r = (r_i % (2 * c)) >= c
            hi_c = (c_i % (2 * c)) >= c
            cross = jnp.logical_and(hi_c, jnp.logical_not(hi_r)) if reverse else jnp.logical_and(hi_r, jnp.logical_not(hi_c))
            mask = jnp.logical_and((r_i // (2 * c)) == (c_i // (2 * c)), cross)
            merges.append((c, _half_rows(mask, c, not reverse)) if c >= SUBLANES else (c, mask))
            c *= 2
        consts.append(dict(strict=strict, incl2=jnp.concatenate([incl, incl], axis=1), merges=merges))

    def operators(gi, carry):
        problems, where = [], []
        for k in range(RWKV_GROUP):
            chunk = gi * RWKV_GROUP + k
            sl = pl.ds(pl.multiple_of(chunk * t, t), t)
            r, v, kk = r_scr[sl, :], v_scr[sl, :], kk_scr[sl, :]
            for d in range(2):
                problems.append((r, v, kk, lw_scr[d, sl, :], ka_scr[d, sl, :], kt_scr[d, sl, :], d))
                where.append((d, chunk, sl))
        ops = _rwkv_chunk_operators(problems, consts, eye, lane_lo)
        for (d, chunk, sl), (ry, y1, pt, qt, g) in zip(where, ops):
            ry_scr[d, sl, :] = ry.astype(BF16)
            y_scr[d, sl, :] = y1
            qt_scr[d, chunk] = qt
            g_scr[d, chunk] = g
        pairs = []
        for k in range(0, RWKV_GROUP, 2):
            for d in range(2):
                first, second = (ops[2 * k + d], ops[2 * (k + 1) + d]) if d == 0 else (ops[2 * (k + 1) + d], ops[2 * k + d])
                pairs.append((d, (gi * RWKV_GROUP + k) // 2, first, second))
        prods = [_bdot(jnp.concatenate([eye * a[4] + a[2], a[3]], axis=0), b[2]) for _, _, a, b in pairs]
        for (d, pair, a, b), prod in zip(pairs, prods):
            pp_scr[d, pair] = jnp.concatenate([a[2], prod[:w] + a[2] * b[4]], axis=1).astype(BF16)
            qq_scr[d, pair] = a[3] * b[4] + prod[w:] + b[3]
        return carry

    lax.fori_loop(0, n_chunks // RWKV_GROUP, operators, 0)

    def scan(trip, carry):
        outs = []
        for k in range(RWKV_SCAN_UNROLL):
            step = trip * RWKV_SCAN_UNROLL + k
            first = _scan_chunks(2 * step, n_chunks, n_ctx_chunks)
            second = _scan_chunks(2 * step + 1, n_chunks, n_ctx_chunks)
            pair = [jnp.minimum(a, b) // 2 for a, b in zip(first, second)]
            moved = [_bdot(ht, pp_scr[d, p]) for d, (p, ht) in enumerate(zip(pair, carry))]
            mids = [ht * g_scr[d, a] + mv[:, :w] + qt_scr[d, a] for d, (a, ht, mv) in enumerate(zip(first, carry, moved))]
            for d, (a, b, ht, mid) in enumerate(zip(first, second, carry, mids)):
                sl_a = pl.ds(pl.multiple_of(a * t, t), t)
                sl_b = pl.ds(pl.multiple_of(b * t, t), t)
                outs.append((d, sl_a, _bdot_nt(ry_scr[d, sl_a, :], ht)))
                outs.append((d, sl_b, _bdot_nt(ry_scr[d, sl_b, :], mid)))
            carry = tuple(ht * (g_scr[d, a] * g_scr[d, b]) + mv[:, w:] + qq_scr[d, p]
                          for d, (a, b, p, ht, mv) in enumerate(zip(first, second, pair, carry, moved)))
        for d, sl, y in outs:
            y_scr[d, sl, :] = y + y_scr[d, sl, :]
        return carry

    zero_state = jnp.zeros((w, w), F32)
    lax.fori_loop(0, n_chunks // (2 * RWKV_SCAN_UNROLL), scan, (zero_state, zero_state))

    f_rows = RWKV_FINISH_ROWS

    def finish(j, carry):
        sl = pl.ds(pl.multiple_of(j * f_rows, f_rows), f_rows)
        y = y_scr[0, sl, :] + y_scr[1, sl, :] + bonus_scr[sl, :]
        mu = _dot_sel(y, head_sum) * inv_head
        yc = y - mu
        var = _dot_sel(yc * yc, head_sum) * inv_head
        yn = yc * lax.rsqrt(var + RWKV_LN_EPSILON) * lnw_ref[...] + lnb_ref[...]
        out_ref[0, sl, :] = (yn * _silu(rz_ref[0, sl, :].astype(F32))).astype(out_ref.dtype)
        return carry

    lax.fori_loop(0, s // f_rows, finish, 0)


def _rwkv(u, col0, p, n_ctx):
    bsz, s, _ = u.shape
    w = LANES
    width = p["mu"].shape[1]
    n_pairs = width // w
    base = col0 // w
    col = lambda k: pl.BlockSpec((1, s, w), lambda b, h, k=k: (b, 0, base + k * n_pairs + h))
    lora = lambda k: pl.BlockSpec((1, s, w), lambda b, h, k=k: (b, 0, base + 4 * n_pairs + k))
    vec = lambda rows: pl.BlockSpec((rows, w), lambda b, h: (0, h))
    cat = pl.BlockSpec((1, 1, 2 * w), lambda b, h: (h, 0, 0))
    up = pl.BlockSpec((1, w, 2 * w), lambda b, h: (h, 0, 0))
    seq = pltpu.VMEM((s, w), F32)
    seq2 = pltpu.VMEM((2, s, w), F32)
    n_chunks = s // RWKV_CHUNK
    assert n_chunks % RWKV_GROUP == 0 and s % RWKV_PREP_ROWS == 0 and n_ctx % RWKV_PREP_ROWS == 0
    assert s % RWKV_FINISH_ROWS == 0 and n_chunks % (2 * RWKV_SCAN_UNROLL) == 0
    assert RWKV_GROUP % 2 == 0 and (n_ctx // RWKV_CHUNK) % 2 == 0
    operators = [pltpu.VMEM((2, s, w), BF16), pltpu.VMEM((2, n_chunks // 2, w, 2 * w), BF16),
                 pltpu.VMEM((2, n_chunks // 2, w, w), F32),
                 pltpu.VMEM((2, n_chunks, w, w), F32), pltpu.VMEM((2, n_chunks, 1, w), F32)]
    return pl.pallas_call(
        functools.partial(_rwkv_kernel, n_ctx=n_ctx),
        grid=(bsz, n_pairs),
        in_specs=[col(0), col(1), col(2), col(3), lora(0), lora(1),
                  vec(3), vec(1), vec(1), vec(1), vec(1), vec(1), cat, cat, up, up],
        out_specs=pl.BlockSpec((1, s, w), lambda b, h: (b, 0, h)),
        out_shape=jax.ShapeDtypeStruct((bsz, s, width), BF16),
        scratch_shapes=[seq, seq, seq, seq2, seq2, seq2, seq, seq2] + operators,
        compiler_params=_compiler_params(("parallel", "arbitrary")),
        name="rwkv7_mixer",
    )(u, u, u, u, u, u, p["mu"], p["k_k"], p["k_a"], p["r_k"], p["ln_w"], p["ln_b"],
      p["w0"], p["a0"], p["w_up"], p["a_up"])


def _hgrn_level_masks(t, w):
    rid = _iota((t, w), 0)
    r_i = _iota((t, t), 0)
    c_i = _iota((t, t), 1)
    levels = []
    c = 1
    while c < t:
        same_block = (r_i // (2 * c)) == (c_i // (2 * c))
        up_r = (r_i % (2 * c)) >= c
        up_c = (c_i % (2 * c)) >= c
        pair = [jnp.logical_and(same_block, jnp.logical_and(up_r, jnp.logical_not(up_c))),
                jnp.logical_and(same_block, jnp.logical_and(up_c, jnp.logical_not(up_r)))]
        levels.append((c, (rid % (2 * c)) >= c, pair))
        c *= 2
    return levels


def _hgrn_chunk_operators(problems, lb, tris, levels):
    t, w = problems[0][0].shape
    r_i = _iota((t, t), 0)
    c_i = _iota((t, t), 1)
    zero_row = jnp.zeros((1, w), F32)
    dirs = [p[3] for p in problems]
    lgs, ks = [], []
    for q, v, ff, d in problems:
        e = jnp.exp(-jnp.abs(ff))
        big = 1.0 / (1.0 + e)
        small = e / (1.0 + e)
        pos = ff >= 0.0
        lgs.append(jnp.log(lb + (1.0 - lb) * jnp.where(pos, big, small)) * LOG2_E)
        ks.append((1.0 - lb) * jnp.where(pos, small, big))
    bs = [_sel_dot(tris[d], lg) for lg, d in zip(lgs, dirs)]
    befores = [_shifted(b, zero_row, zero_row)[1 if d == 1 else 0] for b, d in zip(bs, dirs)]
    edges = list(bs)
    accs = [jnp.where(r_i == c_i, jnp.sum(p[0] * k, axis=-1, keepdims=True), 0.0) for p, k in zip(problems, ks)]
    for c, upper, pair in levels:
        qts = [p[0] * jnp.exp2(b - before) for p, b, before in zip(problems, bs, befores)]
        kts = [k * jnp.exp2(edge - b) for k, b, edge in zip(ks, bs, edges)]
        prods = [_bdot_nt(qt, kt) for qt, kt in zip(qts, kts)]
        accs = [jnp.where(pair[d], pr, a) for a, pr, d in zip(accs, prods, dirs)]
        for i, d in enumerate(dirs):
            if d == 1:
                befores[i] = jnp.where(upper, befores[i], pltpu.roll(befores[i], t - c, 0))
                edges[i] = jnp.where(upper, pltpu.roll(edges[i], c, 0), edges[i])
            else:
                befores[i] = jnp.where(upper, pltpu.roll(befores[i], c, 0), befores[i])
                edges[i] = jnp.where(upper, edges[i], pltpu.roll(edges[i], t - c, 0))
    o_intras = [_bdot(a, p[1]) for a, p in zip(accs, problems)]
    b_ends = [b[(0 if d == 1 else t - 1):(1 if d == 1 else t)] for b, d in zip(bs, dirs)]
    kvs = [_bdot_tn(p[1], k * jnp.exp2(be - b)) for p, k, b, be in zip(problems, ks, bs, b_ends)]
    return [(p[0] * jnp.exp2(b), oi, kv, jnp.exp2(be)) for p, b, oi, kv, be in zip(problems, bs, o_intras, kvs, b_ends)]


def _hgrn_kernel(q_ref, i_ref, ff_ref, fb_ref, z_ref, lb_ref, ng_ref, out_ref, o_scr, qe_scr, kv_scr, g_scr,
                 *, n_ctx, layer):
    s = q_ref.shape[1]
    dh = q_ref.shape[2]
    t = MIX_CHUNK
    n_chunks = s // t
    n_ctx_chunks = n_ctx // t
    lbs = lb_ref[...]
    ex = jnp.exp(lbs - jnp.max(lbs, axis=0, keepdims=True))
    probs = ex / jnp.sum(ex, axis=0, keepdims=True)
    csum = probs[0:1]
    for l in range(1, layer + 1):
        csum = csum + probs[l:l + 1]
    lb = csum - probs[0:1]
    tri_r = _iota((t, t), 0)
    tri_c = _iota((t, t), 1)
    tris = [(tri_c <= tri_r).astype(BF16), (tri_c >= tri_r).astype(BF16)]
    levels = _hgrn_level_masks(t, dh)

    def operators(gi, carry):
        problems, where = [], []
        for kk in range(HGRN_GROUP):
            chunk = gi * HGRN_GROUP + kk
            sl = pl.ds(pl.multiple_of(chunk * t, t), t)
            q, v = q_ref[0, sl, :].astype(F32), i_ref[0, sl, :].astype(F32)
            for d, f_ref in enumerate((ff_ref, fb_ref)):
                problems.append((q, v, f_ref[0, sl, :].astype(F32), d))
                where.append((d, chunk, sl))
        for (d, chunk, sl), (qe, o_intra, kv, g) in zip(where, _hgrn_chunk_operators(problems, lb, tris, levels)):
            qe_scr[d, sl, :] = qe.astype(BF16)
            o_scr[d, sl, :] = o_intra
            kv_scr[d, chunk] = kv
            g_scr[d, chunk] = g
        return carry

    lax.fori_loop(0, n_chunks // HGRN_GROUP, operators, 0)

    def scan(trip, carry):
        steps = []
        for k in range(SCAN_UNROLL):
            chunks = _scan_chunks(trip * SCAN_UNROLL + k, n_chunks, n_ctx_chunks)
            steps += [(d, pl.ds(pl.multiple_of(chunk * t, t), t), st) for d, (chunk, st) in enumerate(zip(chunks, carry))]
            carry = tuple(st * g_scr[d, chunk] + kv_scr[d, chunk] for d, (chunk, st) in enumerate(zip(chunks, carry)))
        inters = [_bdot_nt(qe_scr[d, sl, :], st) for d, sl, st in steps]
        for (d, sl, _), inter in zip(steps, inters):
            o_scr[d, sl, :] = o_scr[d, sl, :] + inter
        return carry

    zero_state = jnp.zeros((dh, dh), F32)
    lax.fori_loop(0, n_chunks // SCAN_UNROLL, scan, (zero_state, zero_state))

    f_rows = MIX_FINISH_ROWS

    def finish(j, carry):
        sl = pl.ds(pl.multiple_of(j * f_rows, f_rows), f_rows)
        o = o_scr[0, sl, :] + o_scr[1, sl, :]
        y = o * lax.rsqrt(jnp.mean(o * o, axis=-1, keepdims=True) + RMS_EPS) * ng_ref[...]
        out_ref[0, sl, :] = (y * _silu(z_ref[0, sl, :].astype(F32))).astype(out_ref.dtype)
        return carry

    lax.fori_loop(0, (s - n_ctx) // f_rows, finish, 0)


def _hgrn(u, lb_all, norm_g, n_ctx, layer):
    bsz, s, _ = u.shape
    nh = HGRN_N_HEADS
    dh = LANES
    width = nh * dh
    depth = lb_all.shape[0]
    n_chunks = s // MIX_CHUNK
    assert n_chunks % HGRN_GROUP == 0 and n_chunks % SCAN_UNROLL == 0 and n_ctx % MIX_CHUNK == 0
    col = lambda k: pl.BlockSpec((1, s, dh), lambda b, h, k=k: (b, 0, k * nh + h))
    return pl.pallas_call(
        functools.partial(_hgrn_kernel, n_ctx=n_ctx, layer=layer),
        grid=(bsz, nh),
        in_specs=[col(0), col(1), col(2), col(3), col(4),
                  pl.BlockSpec((depth, dh), lambda b, h: (0, h)),
                  pl.BlockSpec((1, dh), lambda b, h: (0, h))],
        out_specs=pl.BlockSpec((1, s - n_ctx, dh), lambda b, h: (b, 0, h)),
        out_shape=jax.ShapeDtypeStruct((bsz, s - n_ctx, width), BF16),
        scratch_shapes=[pltpu.VMEM((2, s, dh), F32), pltpu.VMEM((2, s, dh), BF16),
                        pltpu.VMEM((2, n_chunks, dh, dh), F32), pltpu.VMEM((2, n_chunks, 1, dh), F32)],
        compiler_params=_compiler_params(("parallel", "arbitrary")),
        name="hgrn2_mixer",
    )(u, u, u, u, u, lb_all, norm_g.reshape(1, width))


def _hyena_filter_kernel(z_ref, w1_ref, b1_ref, w2_ref, b2_ref, w3f_ref, w3b_ref, dl_ref, hf_ref, hb_ref):
    hp = functools.partial(jnp.dot, precision=lax.Precision.HIGHEST, preferred_element_type=F32)
    n = z_ref.shape[0]
    hid = jnp.sin(hp(z_ref[...], w1_ref[...]) + b1_ref[...])
    hid = jnp.sin(hp(hid, w2_ref[...]) + b2_ref[...])
    pos = _iota((n, 1), 0).astype(F32) * (1.0 / n)
    window = jnp.exp(-pos * dl_ref[...]) + HYENA_SHIFT
    f0 = hp(hid, w3f_ref[...]) * window
    f1 = hp(hid, w3b_ref[...]) * window
    nrm = jnp.sum(jnp.abs(f0), axis=0, keepdims=True) + jnp.sum(jnp.abs(f1), axis=0, keepdims=True)
    hf_ref[...] = f0 / nrm
    hb_ref[...] = f1 / nrm


def _hyena_filters(n, w1, b1, w2, b2, w3, width):
    pos = np.arange(n, dtype=np.float64)
    bands = np.linspace(1e-4, HYENA_N_BANDS - 1, HYENA_N_BANDS)
    ang = (2.0 * math.pi / n) * pos[:, None] * bands
    z = np.concatenate([(pos / n)[:, None], np.cos(ang), np.sin(ang)], axis=-1)
    z = np.pad(z, ((0, 0), (0, LANES - z.shape[1]))).astype(np.float32)
    max_decay = math.log(HYENA_TGT) / HYENA_FAST
    min_decay = math.log(HYENA_TGT) / HYENA_SLOW
    deltas = np.abs(np.linspace(min_decay, max_decay, width)).astype(np.float32)[None]
    feat, hid = w1.shape
    w1p = jnp.pad(w1, ((0, LANES - feat), (0, LANES - hid)))
    w2p = jnp.pad(w2, ((0, LANES - hid), (0, LANES - hid)))
    w3p = jnp.pad(w3, ((0, LANES - hid), (0, 0)))
    b1p = jnp.pad(b1, (0, LANES - hid)).reshape(1, LANES)
    b2p = jnp.pad(b2, (0, LANES - hid)).reshape(1, LANES)
    n_tiles = width // LANES
    full = lambda shape: pl.BlockSpec(shape, lambda j: (0, 0))
    out = pl.BlockSpec((n, LANES), lambda j: (0, j))
    hf, hb = pl.pallas_call(
        _hyena_filter_kernel,
        grid=(n_tiles,),
        in_specs=[full((n, LANES)), full((LANES, LANES)), full((1, LANES)), full((LANES, LANES)), full((1, LANES)),
                  pl.BlockSpec((LANES, LANES), lambda j: (0, j)),
                  pl.BlockSpec((LANES, LANES), lambda j: (0, n_tiles + j)),
                  pl.BlockSpec((1, LANES), lambda j: (0, j))],
        out_specs=[out, out],
        out_shape=[jax.ShapeDtypeStruct((n, width), F32)] * 2,
        compiler_params=_compiler_params(("arbitrary",)),
        name="hyena_filters",
    )(jnp.asarray(z), w1p, b1p, w2p, b2p, w3p, w3p, jnp.asarray(deltas))
    return jnp.concatenate([hf, hb], axis=1)


def _hyena_pre_kernel(yv_ref, y0_ref, y1_ref, yz_ref, swv_ref, sw0_ref, sw1_ref, sbv_ref, sb0_ref, sb1_ref,
                      p_ref, e_ref, *, n_ctx):
    s = yv_ref.shape[1]
    rows = MIX_CHUNK

    def body(j, carry):
        t0 = pl.multiple_of(j * rows, rows)
        conv = []
        for src, sw, sb in ((yv_ref, swv_ref, sbv_ref), (y0_ref, sw0_ref, sb0_ref), (y1_ref, sw1_ref, sb1_ref)):
            cur, down, up = _chunk_with_neighbors(src, t0, rows, s, s - n_ctx)
            conv.append(down * sw[0:1, :] + cur * sw[1:2, :] + up * sw[2:3, :] + sb[...])
        v, x0, x1 = conv
        p = x1 * v
        o0 = pl.multiple_of(j * rows, rows)
        p_ref[0, pl.ds(o0, rows), :] = p.astype(BF16)
        e_ref[0, pl.ds(o0, rows), :] = (x0 * _silu(yz_ref[0, pl.ds(t0, rows), :].astype(F32))).astype(BF16)
        return carry

    lax.fori_loop(0, (s - n_ctx) // rows, body, 0)


def _hyena_pre(u, col0, short_w, short_b, n_ctx):
    bsz, s, _ = u.shape
    w = short_b.shape[0] // 3
    tiles = w // LANES
    base = col0 // LANES
    n = s - n_ctx
    col = lambda k: pl.BlockSpec((1, s, LANES), lambda b, j, k=k: (b, 0, base + k * tiles + j))
    par = lambda rows, k: pl.BlockSpec((rows, LANES), lambda b, j, k=k: (0, k * tiles + j))
    out = pl.BlockSpec((1, n, LANES), lambda b, j: (b, 0, j))
    sb = short_b.reshape(1, 3 * w)
    return pl.pallas_call(
        functools.partial(_hyena_pre_kernel, n_ctx=n_ctx),
        grid=(bsz, tiles),
        in_specs=[col(0), col(1), col(2), col(3),
                  par(3, 0), par(3, 1), par(3, 2), par(1, 0), par(1, 1), par(1, 2)],
        out_specs=[out, out],
        out_shape=[jax.ShapeDtypeStruct((bsz, n, w), BF16)] * 2,
        compiler_params=_compiler_params(("parallel", "arbitrary")),
        name="hyena_short_conv",
    )(u, u, u, u, short_w, short_w, short_w, sb, sb, sb)


def _dft_tables(n):
    big = 2 * n
    half = DFT_TILE // 2
    idx = jnp.arange(n, dtype=jnp.int32)
    split = DFT_SPLIT
    lo = jnp.arange(split, dtype=jnp.int32)
    hi = jnp.arange(n // split, dtype=jnp.int32)
    ang_lo = ((lo[:, None] * idx[None, :]) % big).astype(F32) * (2.0 * math.pi / big)
    ang_hi = ((hi[:, None] * idx[None, :]) % (big // split)).astype(F32) * (2.0 * math.pi * split / big)
    c_lo, s_lo = jnp.cos(ang_lo)[None], jnp.sin(ang_lo)[None]
    c_hi, s_hi = jnp.cos(ang_hi)[:, None], jnp.sin(ang_hi)[:, None]
    cos = (c_hi * c_lo - s_hi * s_lo).reshape(n, n)
    sin = (s_hi * c_lo + c_hi * s_lo).reshape(n, n)
    alt = jnp.where(idx % 2 == 0, 1.0, -1.0).astype(F32)
    first_row = (idx == 0)[:, None]
    im = jnp.where(first_row, alt[None, :], -sin)
    fwd = jnp.stack([cos.reshape(n // half, half, n), im.reshape(n // half, half, n)], axis=1).reshape(big, n)
    weight = jnp.where(first_row, 1.0, 2.0) * (1.0 / big)
    weight = jnp.stack([weight.reshape(n // half, half, 1)] * 2, axis=1).reshape(big, 1)
    return fwd.astype(BF16), (fwd * weight).T.astype(BF16)


def _spectrum_kernel(f_ref, lo_ref, hi_ref, bias_ref, o_ref):
    half = DFT_TILE // 2
    w = lo_ref.shape[1] // 2
    lo = lo_ref[...].astype(BF16)
    acc_lo = jnp.dot(f_ref[...], lo, preferred_element_type=F32)
    acc_hi = jnp.dot(f_ref[...], hi_ref[...].astype(BF16), preferred_element_type=F32)
    rows = _iota((DFT_TILE, w), 0)
    sign = jnp.where(rows % 2 == 1, -1.0, 1.0)
    real_slot = jnp.logical_or(rows < half, jnp.logical_and(rows == half, pl.program_id(0) == 0))
    conj = lambda x: jnp.where(real_slot, x, -x)
    lag0 = lo[0:1, :].astype(F32)
    a, c_all = acc_lo[:, :w], acc_lo[:, w:]
    b, d = acc_hi[:, :w], acc_hi[:, w:]
    c = c_all - jnp.where(real_slot, lag0[:, w:], 0.0)
    o_ref[:, 0:w] = a + conj(c_all) + jnp.where(real_slot, bias_ref[...], 0.0)
    o_ref[:, w:2 * w] = b + sign * (a - jnp.where(real_slot, lag0[:, :w], 0.0))
    o_ref[:, 2 * w:] = sign * conj(c) + conj(d)


def _filter_spectrum(fwd, hk, bias):
    big, h = fwd.shape
    w = hk.shape[1] // 2
    assert hk.shape[0] == 2 * h
    return pl.pallas_call(
        _spectrum_kernel,
        grid=(big // DFT_TILE,),
        in_specs=[pl.BlockSpec((DFT_TILE, h), lambda i: (i, 0)),
                  pl.BlockSpec((h, 2 * w), lambda i: (0, 0)),
                  pl.BlockSpec((h, 2 * w), lambda i: (1, 0)),
                  pl.BlockSpec((1, w), lambda i: (0, 0))],
        out_specs=pl.BlockSpec((DFT_TILE, 3 * w), lambda i: (i, 0)),
        out_shape=jax.ShapeDtypeStruct((big, 3 * w), F32),
        compiler_params=_compiler_params(("arbitrary",)),
        name="hyena_filter_spectrum",
    )(fwd, hk, hk, bias.reshape(1, w))


def _spectral_products(pairs, packed):
    re = sum(s[0] * k[0] - s[1] * k[1] for s, k in pairs)
    im = sum(s[0] * k[1] + s[1] * k[0] for s, k in pairs)
    if packed is not None:
        re = jnp.where(packed, sum(s[0] * k[0] for s, k in pairs), re)
        im = jnp.where(packed, sum(s[1] * k[1] for s, k in pairs), im)
    return jnp.concatenate([re, im], axis=0).astype(BF16)


def _conv_spectrum_kernel(f_ref, pa_ref, pb_ref, ks_ref, z_ref):
    i = pl.program_id(1)
    half = DFT_TILE // 2
    w = pa_ref.shape[2]
    units = [slice(k * DFT_TILE, (k + 1) * DFT_TILE) for k in range(DFT_UNITS)]
    accs = [(jnp.dot(f_ref[u, :], pa_ref[0], preferred_element_type=F32),
             jnp.dot(f_ref[u, :], pb_ref[0], preferred_element_type=F32)) for u in units]
    for k, (u, (acc_a, acc_b)) in enumerate(zip(units, accs)):
        s_a, s_b = (acc_a[:half], acc_a[half:]), (acc_b[:half], acc_b[half:])
        ks = ks_ref[u, :]
        k_diag, k_below, k_above = [(ks[:half, j * w:(j + 1) * w], ks[half:, j * w:(j + 1) * w]) for j in range(3)]
        packed = jnp.logical_and(_iota((half, w), 0) == 0, i == 0) if k == 0 else None
        z_ref[0, 0, u, :] = _spectral_products([(s_a, k_diag), (s_b, k_above)], packed)
        z_ref[0, 1, u, :] = _spectral_products([(s_a, k_below), (s_b, k_diag)], packed)


def _conv_spectrum(fwd, p16, kspec):
    bsz, n, w = p16.shape
    big, h = fwd.shape
    assert n == 2 * h
    rows = DFT_TILE * DFT_UNITS
    return pl.pallas_call(
        _conv_spectrum_kernel,
        grid=(bsz, big // rows),
        in_specs=[pl.BlockSpec((rows, h), lambda b, i: (i, 0)),
                  pl.BlockSpec((1, h, w), lambda b, i: (b, 0, 0)),
                  pl.BlockSpec((1, h, w), lambda b, i: (b, 1, 0)),
                  pl.BlockSpec((rows, 3 * w), lambda b, i: (i, 0))],
        out_specs=pl.BlockSpec((1, 2, rows, w), lambda b, i: (b, 0, i, 0)),
        out_shape=jax.ShapeDtypeStruct((bsz, 2, big, w), BF16),
        compiler_params=_compiler_params(("parallel", "arbitrary")),
        name="hyena_forward_dft",
    )(fwd, p16, p16, kspec)


def _conv_inverse_kernel(g_ref, z_ref, e_ref, o_ref):
    y = jnp.dot(g_ref[...], z_ref[0, 0], preferred_element_type=F32)
    o_ref[0] = (e_ref[0].astype(F32) * y).astype(o_ref.dtype)


def _conv_inverse(inv, z16, e):
    bsz, _, big, w = z16.shape
    h = inv.shape[0]
    tile = DFT_TILE
    per_half = h // tile
    tok = pl.BlockSpec((1, tile, w), lambda b, j, i: (b, j * per_half + i, 0))
    return pl.pallas_call(
        _conv_inverse_kernel,
        grid=(bsz, 2, per_half),
        in_specs=[pl.BlockSpec((tile, big), lambda b, j, i: (i, 0)),
                  pl.BlockSpec((1, 1, big, w), lambda b, j, i: (b, j, 0, 0)),
                  tok],
        out_specs=tok,
        out_shape=jax.ShapeDtypeStruct((bsz, 2 * h, w), BF16),
        compiler_params=_compiler_params(("parallel", "arbitrary", "arbitrary")),
        name="hyena_inverse_dft",
    )(inv, z16, e)


def _even_weight_layout(w_in, gate_b):
    d = w_in.shape[0]
    mw = MLSTM_N_HEADS * LANES
    g0 = 5 * mw
    g1 = g0 + 4 * MLSTM_N_HEADS
    main = jnp.concatenate([w_in[:, :g0], w_in[:, g1:]], axis=1).astype(BF16)
    wg = w_in[:, g0:g1].reshape(d, 2, 2, MLSTM_N_HEADS)
    wg = jnp.transpose(wg, (3, 1, 2, 0)).reshape(MLSTM_N_HEADS, 4, d)
    wg = jnp.concatenate([wg, jnp.zeros_like(wg)], axis=1).reshape(MLSTM_N_HEADS * 8, d).astype(BF16)
    gb = jnp.transpose(gate_b.reshape(2, 2, MLSTM_N_HEADS), (2, 0, 1)).reshape(MLSTM_N_HEADS, 4)
    gb = jnp.concatenate([gb, jnp.zeros_like(gb)], axis=1).reshape(MLSTM_N_HEADS * 8, 1)
    return main, wg, jnp.broadcast_to(gb, (MLSTM_N_HEADS * 8, LANES))


def _rwkv_params(mu, w0, w_up, a0, a_up, k_k, k_a, r_k, ln_w, ln_b):
    width = mu.shape[1]
    n_pairs = width // LANES
    row = lambda x: x.reshape(1, width)

    def cat_dirs(x):
        return jnp.transpose(x.reshape(2, n_pairs, LANES), (1, 0, 2)).reshape(n_pairs, 1, 2 * LANES)

    def block_up(x):
        lora = x.shape[1]
        xp = jnp.transpose(x.reshape(2, lora, n_pairs, LANES), (2, 0, 1, 3))
        z = jnp.zeros_like(xp[:, 0])
        top = jnp.concatenate([xp[:, 0], z], axis=2)
        bot = jnp.concatenate([z, xp[:, 1]], axis=2)
        return jnp.concatenate([top, bot], axis=1).astype(BF16)

    return {"mu": mu, "k_k": row(k_k), "k_a": row(k_a), "r_k": row(r_k), "ln_w": row(ln_w), "ln_b": row(ln_b),
            "w0": cat_dirs(w0), "a0": cat_dirs(a0), "w_up": block_up(w_up), "a_up": block_up(a_up)}


def _raster_to_column(h):
    b, n, d = h.shape
    rows = n // GRID_WIDTH
    return h.reshape(b, rows, GRID_WIDTH, d).transpose(0, 2, 1, 3).reshape(b, n, d)


def _column_to_raster(h):
    b, n, d = h.shape
    rows = n // GRID_WIDTH
    return h.reshape(b, GRID_WIDTH, rows, d).transpose(0, 2, 1, 3).reshape(b, n, d)


def kernel(x, c, ctx, c_ctx, l0_norm_g, l0_mod_w, l0_mod_b, l0_w_in, l0_w_out, l0_mlstm_conv_w, l0_mlstm_gate_b, l0_mlstm_norm_g, l0_rwkv_mu, l0_rwkv_w0, l0_rwkv_w_up, l0_rwkv_a0, l0_rwkv_a_up, l0_rwkv_k_k, l0_rwkv_k_a, l0_rwkv_r_k, l0_rwkv_ln_w, l0_rwkv_ln_b, hgrn_lower_bounds, l1_norm_g, l1_mod_w, l1_mod_b, l1_w_in, l1_w_out, l1_hgrn_norm_g, l1_hyena_short_w, l1_hyena_short_b, l1_hyena_w1, l1_hyena_b1, l1_hyena_w2, l1_hyena_b2, l1_hyena_w3, l1_hyena_bias, final_norm_g):
    bsz, n_lat, d = x.shape
    n_ctx = ctx.shape[1]

    pad = (-(bsz + 1)) % 8
    cc = jnp.concatenate([c, c_ctx[None], jnp.zeros((pad, d), F32)], axis=0)
    mod0, mod1 = _modulation(cc, l0_mod_w, l0_mod_b, l1_mod_w, l1_mod_b)
    mod0 = mod0[:bsz + 1].reshape(bsz + 1, 3, d)
    mod1 = mod1[:bsz + 1].reshape(bsz + 1, 3, d)

    w_main, w_gate, b_gate = _even_weight_layout(l0_w_in, l0_mlstm_gate_b)
    n0 = w_main.shape[1]
    u0, gt0 = _proj_in(x, ctx, l0_norm_g, mod0, w_main, n0 // 2, w_gate, b_gate)
    gt0 = gt0.reshape(bsz, MLSTM_N_HEADS, 8, n_ctx + n_lat)
    y_m = _mlstm(u0, gt0, l0_mlstm_conv_w, l0_mlstm_norm_g, n_ctx)
    rp = _rwkv_params(l0_rwkv_mu, l0_rwkv_w0, l0_rwkv_w_up, l0_rwkv_a0, l0_rwkv_a_up, l0_rwkv_k_k,
                      l0_rwkv_k_a, l0_rwkv_r_k, l0_rwkv_ln_w, l0_rwkv_ln_b)
    y_r = _rwkv(u0, 5 * MLSTM_N_HEADS * LANES, rp, n_ctx)
    x1, ctx1 = _proj_out(y_m, y_r, x, ctx, mod0, l0_w_out.astype(BF16))

    x1c = _raster_to_column(x1)
    w1 = l1_w_in.astype(BF16)
    (u1,) = _proj_in(x1c, ctx1, l1_norm_g, mod1, w1, w1.shape[1] // 2)
    y_g = _hgrn(u1, hgrn_lower_bounds, l1_hgrn_norm_g, n_ctx, layer=1)
    hw = l1_hyena_bias.shape[0]
    hk = _hyena_filters(n_lat, l1_hyena_w1, l1_hyena_b1, l1_hyena_w2, l1_hyena_b2, l1_hyena_w3, hw)
    fwd, inv = _dft_tables(n_lat // 2)
    kspec = _filter_spectrum(fwd, hk, l1_hyena_bias)
    p16, e16 = _hyena_pre(u1, 5 * HGRN_N_HEADS * LANES, l1_hyena_short_w, l1_hyena_short_b, n_ctx)
    z16 = _conv_spectrum(fwd, p16, kspec)
    y_y = _conv_inverse(inv, z16, e16)
    out_c = _proj_out_final(y_g, y_y, x1c, mod1, l1_w_out.astype(BF16), final_norm_g)
    return _column_to_raster(out_c)
```

```python
import functools
import math

import jax
import jax.numpy as jnp
import numpy as np
from jax import lax
from jax.experimental import pallas as pl
from jax.experimental.pallas import tpu as pltpu

F32 = jnp.float32
BF16 = jnp.bfloat16

GRID_WIDTH = 64
RMS_EPS = 1e-6
MLSTM_N_HEADS = 4
RWKV_HEAD = 64
RWKV_LN_EPSILON = 64e-5
HGRN_N_HEADS = 4
HYENA_N_BANDS = 16
HYENA_FAST = 0.3
HYENA_SLOW = 1.5
HYENA_TGT = 1e-2
HYENA_SHIFT = 0.05
LOG2_E = 1.0 / math.log(2.0)

LANES = 128
SUBLANES = 8
ROW_GROUP = 16
MXU_DIM = 256
VMEM_LIMIT = 52 * 1024 * 1024

MIX_CHUNK = 128
RWKV_CHUNK = 64
RWKV_GROUP = 12
MIX_PREP_ROWS = 256
MIX_FINISH_ROWS = 256
RWKV_PREP_ROWS = 256
RWKV_PREP_BLOCKS = 3
RWKV_FINISH_ROWS = 768
RWKV_SCAN_UNROLL = 3
SCAN_UNROLL = 3
MLSTM_GROUP = 6
HGRN_GROUP = 3
PROJ_ROWS = 768
FINAL_ROWS = 512
DFT_TILE = 512
DFT_UNITS = 2
DFT_SPLIT = 64


def _bdot(a, b):
    return jnp.dot(a.astype(BF16), b.astype(BF16), preferred_element_type=F32)


def _bdot_nt(a, b):
    return lax.dot_general(a.astype(BF16), b.astype(BF16), (((1,), (1,)), ((), ())),
                           preferred_element_type=F32)


def _bdot_tn(a, b):
    return lax.dot_general(a.astype(BF16), b.astype(BF16), (((0,), (0,)), ((), ())),
                           preferred_element_type=F32)


def _split3(x):
    hi = x.astype(BF16)
    r1 = x - hi.astype(F32)
    mid = r1.astype(BF16)
    lo = (r1 - mid.astype(F32)).astype(BF16)
    return hi, mid, lo


def _sel_dot(sel, x):
    hi, mid, lo = _split3(x)
    d = functools.partial(jnp.dot, preferred_element_type=F32)
    return d(sel, hi) + d(sel, mid) + d(sel, lo)


def _dot_sel(x, sel):
    hi = x.astype(BF16)
    mid = (x - hi.astype(F32)).astype(BF16)
    d = functools.partial(jnp.dot, preferred_element_type=F32)
    return d(hi, sel) + d(mid, sel)


def _sigmoid(x):
    return 1.0 / (1.0 + jnp.exp(-x))


def _silu(x):
    return x * _sigmoid(x)


def _iota(shape, dim):
    return lax.broadcasted_iota(jnp.int32, shape, dim)


def _neighbor_rows(ref, t0, rows, n_total, split):
    has_prev = jnp.logical_and(t0 != 0, t0 != split)
    has_next = jnp.logical_and(t0 + rows != split, t0 + rows != n_total)
    g = ROW_GROUP
    before = ref[0, pl.ds(pl.multiple_of(jnp.maximum(t0 - g, 0), g), g), :].astype(F32)
    after = ref[0, pl.ds(pl.multiple_of(jnp.minimum(t0 + rows, n_total - g), g), g), :].astype(F32)
    return jnp.where(has_prev, before[g - 1:g], 0.0), jnp.where(has_next, after[0:1], 0.0)


def _shifted(cur, prev_row, next_row):
    rows = cur.shape[0]
    rid = _iota(cur.shape, 0)
    down = jnp.where(rid == 0, prev_row, pltpu.roll(cur, 1, 0))
    up = jnp.where(rid == rows - 1, next_row, pltpu.roll(cur, rows - 1, 0))
    return down, up


def _chunk_with_neighbors(ref, t0, rows, n_total, split):
    cur = ref[0, pl.ds(t0, rows), :].astype(F32)
    prev_row, next_row = _neighbor_rows(ref, t0, rows, n_total, split)
    down, up = _shifted(cur, prev_row, next_row)
    return cur, down, up


def _scan_chunks(i, n_chunks, n_ctx_chunks):
    fwd = jnp.where(i < n_ctx_chunks, n_chunks - n_ctx_chunks + i, i - n_ctx_chunks)
    return fwd, n_chunks - 1 - i


def _compiler_params(semantics):
    return pltpu.CompilerParams(dimension_semantics=semantics, vmem_limit_bytes=VMEM_LIMIT)


def _mod_kernel(c_ref, w0_ref, b0_ref, w1_ref, b1_ref, o0_ref, o1_ref):
    s = _silu(c_ref[...])
    o0_ref[...] = _bdot(s, w0_ref[...]) + b0_ref[...]
    o1_ref[...] = _bdot(s, w1_ref[...]) + b1_ref[...]


def _modulation(cc, w0, b0, w1, b1):
    rows, d = cc.shape
    n = w0.shape[1]
    tile = d
    grid = (n // tile,)
    wspec = pl.BlockSpec((d, tile), lambda j: (0, j))
    bspec = pl.BlockSpec((1, tile), lambda j: (0, j))
    ospec = pl.BlockSpec((rows, tile), lambda j: (0, j))
    return pl.pallas_call(
        _mod_kernel,
        grid=grid,
        in_specs=[pl.BlockSpec((rows, d), lambda j: (0, 0)), wspec, bspec, wspec, bspec],
        out_specs=[ospec, ospec],
        out_shape=[jax.ShapeDtypeStruct((rows, n), F32)] * 2,
        compiler_params=_compiler_params(("arbitrary",)),
        name="adaln_modulation",
    )(cc, w0, b0.reshape(1, n), w1, b1.reshape(1, n))


def _token_tile(x_ref, c_ref, i, rows):
    n_lat_tail = rows - c_ref.shape[1]
    is_ctx = jnp.logical_and(i == pl.num_programs(1) - 1, _iota((rows, 1), 0) >= n_lat_tail)
    ctx_rows = jnp.concatenate([jnp.zeros((n_lat_tail, c_ref.shape[2]), F32), c_ref[0]], axis=0)
    return jnp.where(is_ctx, ctx_rows, x_ref[0]), is_ctx


def _proj_in_kernel(*refs, rows, with_gates):
    if with_gates:
        x_ref, c_ref, g_ref, ml_ref, mc_ref, w_ref, wg_ref, gb_ref, u_ref, gt_ref, h_scr = refs
    else:
        x_ref, c_ref, g_ref, ml_ref, mc_ref, w_ref, u_ref, h_scr = refs
    i = pl.program_id(1)
    n = pl.program_id(2)

    @pl.when(n == 0)
    def _():
        x, is_ctx = _token_tile(x_ref, c_ref, i, rows)
        y = x * lax.rsqrt(jnp.mean(x * x, axis=-1, keepdims=True) + RMS_EPS) * g_ref[...]
        ml = ml_ref[0]
        mc = mc_ref[0]
        shift = jnp.where(is_ctx, mc[0:1], ml[0:1])
        scale = jnp.where(is_ctx, mc[1:2], ml[1:2])
        h = (y * (1.0 + scale) + shift).astype(BF16)
        h_scr[...] = h
        if with_gates:
            gt_ref[0] = _bdot_nt(wg_ref[...], h) + gb_ref[:, 0:1]

    u_ref[0] = jnp.dot(h_scr[...], w_ref[...], preferred_element_type=F32).astype(u_ref.dtype)


def _proj_in(x, ctx, norm_g, mod3, w16, n_tile, gate_w=None, gate_b=None):
    bsz, n_lat, d = x.shape
    n_ctx = ctx.shape[1]
    s = n_lat + n_ctx
    n = w16.shape[1]
    rows = PROJ_ROWS
    assert s % rows == 0 and (n_lat % rows) + n_ctx == rows
    grid = (bsz, s // rows, n // n_tile)
    ctx_row = mod3.shape[0] - 1
    with_gates = gate_w is not None
    in_specs = [
        pl.BlockSpec((1, rows, d), lambda b, i, j: (b, i, 0)),
        pl.BlockSpec((1, n_ctx, d), lambda b, i, j: (b, 0, 0)),
        pl.BlockSpec((1, d), lambda b, i, j: (0, 0)),
        pl.BlockSpec((1, 3, d), lambda b, i, j: (b, 0, 0)),
        pl.BlockSpec((1, 3, d), lambda b, i, j: (ctx_row, 0, 0)),
        pl.BlockSpec((d, n_tile), lambda b, i, j: (0, j)),
    ]
    args = [x, ctx, norm_g.reshape(1, d), mod3, mod3, w16]
    out_specs = [pl.BlockSpec((1, rows, n_tile), lambda b, i, j: (b, i, j))]
    out_shape = [jax.ShapeDtypeStruct((bsz, s, n), BF16)]
    if with_gates:
        ng = gate_w.shape[0]
        in_specs += [pl.BlockSpec((ng, d), lambda b, i, j: (0, 0)),
                     pl.BlockSpec((ng, LANES), lambda b, i, j: (0, 0))]
        args += [gate_w, gate_b]
        out_specs.append(pl.BlockSpec((1, ng, rows), lambda b, i, j: (b, 0, i)))
        out_shape.append(jax.ShapeDtypeStruct((bsz, ng, s), F32))
    return pl.pallas_call(
        functools.partial(_proj_in_kernel, rows=rows, with_gates=with_gates),
        grid=grid,
        in_specs=in_specs,
        out_specs=out_specs,
        out_shape=out_shape,
        scratch_shapes=[pltpu.VMEM((rows, d), BF16)],
        compiler_params=_compiler_params(("parallel", "arbitrary", "arbitrary")),
        name="norm_mod_proj_in",
    )(*args)


def _proj_out_kernel(ya_ref, yb_ref, x_ref, c_ref, ml_ref, mc_ref, w_ref, ox_ref, oc_ref, *, rows):
    i = pl.program_id(1)
    half = ya_ref.shape[2]
    y = _bdot(ya_ref[0], w_ref[0:half, :]) + _bdot(yb_ref[0], w_ref[half:, :])
    x, is_ctx = _token_tile(x_ref, c_ref, i, rows)
    x = x + jnp.where(is_ctx, mc_ref[0][2:3], ml_ref[0][2:3]) * y
    ox_ref[0] = x

    @pl.when(i == pl.num_programs(1) - 1)
    def _():
        oc_ref[0] = x[rows - c_ref.shape[1]:]


def _proj_out(ya, yb, x, ctx, mod3, w16):
    bsz, n_lat, d = x.shape
    n_ctx = ctx.shape[1]
    s = n_lat + n_ctx
    half = ya.shape[2]
    rows = PROJ_ROWS
    assert s % rows == 0 and (n_lat % rows) + n_ctx == rows
    ctx_row = mod3.shape[0] - 1
    tok = lambda w: pl.BlockSpec((1, rows, w), lambda b, i: (b, i, 0))
    seg = pl.BlockSpec((1, n_ctx, d), lambda b, i: (b, 0, 0))
    return pl.pallas_call(
        functools.partial(_proj_out_kernel, rows=rows),
        grid=(bsz, s // rows),
        in_specs=[tok(half), tok(half), tok(d), seg,
                  pl.BlockSpec((1, 3, d), lambda b, i: (b, 0, 0)),
                  pl.BlockSpec((1, 3, d), lambda b, i: (ctx_row, 0, 0)),
                  pl.BlockSpec((2 * half, d), lambda b, i: (0, 0))],
        out_specs=[tok(d), seg],
        out_shape=[jax.ShapeDtypeStruct((bsz, n_lat, d), F32), jax.ShapeDtypeStruct((bsz, n_ctx, d), F32)],
        compiler_params=_compiler_params(("parallel", "arbitrary")),
        name="proj_out_residual",
    )(ya, yb, x, ctx, mod3, mod3, w16)


def _proj_out_final_kernel(ya_ref, yb_ref, x_ref, ml_ref, w_ref, fg_ref, o_ref):
    half = ya_ref.shape[2]
    y = _bdot(ya_ref[0], w_ref[0:half, :]) + _bdot(yb_ref[0], w_ref[half:, :])
    x = x_ref[0] + ml_ref[0][2:3] * y
    o_ref[0] = x * lax.rsqrt(jnp.mean(x * x, axis=-1, keepdims=True) + RMS_EPS) * fg_ref[...]


def _proj_out_final(ya, yb, x, mod3, w16, final_g):
    bsz, n_lat, d = x.shape
    half = ya.shape[2]
    rows = FINAL_ROWS
    assert n_lat % rows == 0
    tok = lambda w: pl.BlockSpec((1, rows, w), lambda b, i: (b, i, 0))
    return pl.pallas_call(
        _proj_out_final_kernel,
        grid=(bsz, n_lat // rows),
        in_specs=[tok(half), tok(half), tok(d),
                  pl.BlockSpec((1, 3, d), lambda b, i: (b, 0, 0)),
                  pl.BlockSpec((2 * half, d), lambda b, i: (0, 0)),
                  pl.BlockSpec((1, d), lambda b, i: (0, 0))],
        out_specs=tok(d),
        out_shape=jax.ShapeDtypeStruct((bsz, n_lat, d), F32),
        compiler_params=_compiler_params(("parallel", "arbitrary")),
        name="proj_out_final_norm",
    )(ya, yb, x, mod3, w16, final_g.reshape(1, d))


def _mlstm_chunk_operators(chunks, causal):
    t = chunks[0][0].shape[0]
    lane = _iota((8, t), 1)
    row_id = _iota((8, t), 0)
    log_fs = [jnp.minimum(c[3], 0.0) - jnp.log1p(jnp.exp(-jnp.abs(c[3]))) for c in chunks]
    cum_f, cum_b = list(log_fs), list(log_fs)
    sh = 1
    while sh < t:
        cum_f = [x + jnp.where(lane >= sh, pltpu.roll(x, sh, 1), 0.0) for x in cum_f]
        cum_b = [x + jnp.where(lane < t - sh, pltpu.roll(x, t - sh, 1), 0.0) for x in cum_b]
        sh *= 2
    pad = jnp.zeros((t - 8, t), F32)
    tiles = [jnp.concatenate([jnp.where(row_id % 2 == 0, c[3], jnp.where(row_id == 1, f, b)), pad], axis=0)
             for c, f, b in zip(chunks, cum_f, cum_b)]
    cols = [x.T for x in tiles]
    k_ts = [c[1].T for c in chunks]
    problems = []
    for c, f, b, col, k_t in zip(chunks, cum_f, cum_b, cols, k_ts):
        for d in range(2):
            b_row = (f, b)[d][2 * d + 1:2 * d + 2]
            problems.append(dict(q=c[0], k=c[1], k_t=k_t, v_ext=c[2], d=d, ig_row=c[3][2 * d:2 * d + 1], b_row=b_row,
                                 b_col=col[:, 2 * d + 1:2 * d + 2]))
    logws = [jnp.where(causal[p["d"]], p["b_col"] + (p["ig_row"] - p["b_row"]), -jnp.inf) for p in problems]
    mus = [jnp.max(x, axis=-1, keepdims=True) for x in logws]
    ws = [jnp.exp(x - mu) for x, mu in zip(logws, mus)]
    lasts = [0 if p["d"] == 1 else t - 1 for p in problems]
    b_lasts = [p["b_col"][i:i + 1] for p, i in zip(problems, lasts)]
    gammas = [mu[i:i + 1] for mu, i in zip(mus, lasts)]
    gk_ts = [p["k_t"] * jnp.exp(bl - p["b_row"] + p["ig_row"] - gm) for p, bl, gm in zip(problems, b_lasts, gammas)]
    qks = [_bdot_nt(p["q"], p["k"]) * w for p, w in zip(problems, ws)]
    intras = [_bdot(qk, p["v_ext"]) for qk, p in zip(qks, problems)]
    kvs = [_bdot(gk_t, p["v_ext"]) for gk_t, p in zip(gk_ts, problems)]
    dh = chunks[0][0].shape[1]
    return [(intra, kv, jnp.broadcast_to(mu - p["b_col"], (t, dh)), jnp.broadcast_to(mu, (t, dh)), bl, gm)
            for intra, kv, p, mu, bl, gm in zip(intras, kvs, problems, mus, b_lasts, gammas)]


def _mlstm_kernel(q_ref, k_ref, v_ref, o_ref, z_ref, gt_ref, cwq_ref, cwk_ref, ng_ref, out_ref,
                  qa_scr, ka_scr, h_scr, intra_scr, kv_scr, delta_scr, mu_scr, tail_scr, *, n_ctx):
    s = q_ref.shape[1]
    dh = q_ref.shape[2]
    t = MIX_CHUNK
    n_chunks = s // t
    n_ctx_chunks = n_ctx // t
    k_scale = dh ** -0.5

    p_rows = MIX_PREP_ROWS

    def prep(j, carry):
        t0 = pl.multiple_of(j * p_rows, p_rows)
        for src, cw, dst, scale in ((q_ref, cwq_ref, qa_scr, 1.0), (k_ref, cwk_ref, ka_scr, k_scale)):
            cur, down, up = _chunk_with_neighbors(src, t0, p_rows, s, s - n_ctx)
            conv = down * cw[0:1, :] + cur * cw[1:2, :] + up * cw[2:3, :]
            dst[pl.ds(t0, p_rows), :] = _silu(conv) * scale
        return carry

    lax.fori_loop(0, s // p_rows, prep, 0)

    ones_col = jnp.ones((t, dh), F32)
    causal = [_iota((t, t), 1) <= _iota((t, t), 0), _iota((t, t), 1) >= _iota((t, t), 0)]

    def operators(gi, carry):
        chunks, where = [], []
        for kk in range(MLSTM_GROUP):
            chunk = gi * MLSTM_GROUP + kk
            sl = pl.ds(pl.multiple_of(chunk * t, t), t)
            v_ext = jnp.concatenate([v_ref[0, sl, :].astype(F32), ones_col], axis=1)
            chunks.append((qa_scr[sl, :], ka_scr[sl, :], v_ext, gt_ref[0, 0, :, sl]))
            where += [(0, chunk, sl), (1, chunk, sl)]
        for (d, chunk, sl), (intra, kv, delta, mu, b_last, gamma) in zip(where, _mlstm_chunk_operators(chunks, causal)):
            intra_scr[d, sl, :] = intra
            kv_scr[d, chunk] = kv
            delta_scr[d, sl, :] = delta
            mu_scr[d, sl, :] = mu
            tail_scr[d, chunk] = jnp.concatenate([jnp.broadcast_to(b_last, (1, dh)), jnp.broadcast_to(gamma, (1, dh))],
                                                 axis=0)
        return carry

    lax.fori_loop(0, n_chunks // MLSTM_GROUP, operators, 0)

    def scan(trip, carry):
        steps = []
        for k in range(SCAN_UNROLL):
            chunks = _scan_chunks(trip * SCAN_UNROLL + k, n_chunks, n_ctx_chunks)
            new = []
            for d, (chunk, (c_ext, m)) in enumerate(zip(chunks, carry)):
                steps.append((d, chunk, pl.ds(pl.multiple_of(chunk * t, t), t), c_ext, m))
                tail = tail_scr[d, chunk]
                b_last, gamma = tail[0:1, 0:1], tail[1:2, 0:1]
                m_new = jnp.maximum(b_last + m, gamma)
                new.append((jnp.exp(b_last + m - m_new) * c_ext + jnp.exp(gamma - m_new) * kv_scr[d, chunk], m_new))
            carry = tuple(new)
        inters = [_bdot(qa_scr[sl, :], c_ext) for _, _, sl, c_ext, _ in steps]
        for (d, chunk, sl, _, m), inter in zip(steps, inters):
            z = delta_scr[d, sl, :] - m
            s_inter = jnp.exp(-jnp.maximum(z, 0.0))
            s_intra = jnp.exp(jnp.minimum(z, 0.0))
            floor = jnp.exp(jnp.minimum(z, 0.0) - mu_scr[d, sl, :])
            intra = intra_scr[d, sl, :]
            num = s_inter * inter[:, :dh] + s_intra * intra[:, :dh]
            den = s_inter * inter[:, dh:] + s_intra * intra[:, dh:]
            h_scr[d, sl, :] = num / jnp.maximum(jnp.abs(den), floor)
        return carry

    zero = (jnp.zeros((dh, 2 * dh), F32), jnp.zeros((1, 1), F32))
    lax.fori_loop(0, n_chunks // SCAN_UNROLL, scan, (zero, zero))

    f_rows = MIX_FINISH_ROWS

    def finish(j, carry):
        sl = pl.ds(pl.multiple_of(j * f_rows, f_rows), f_rows)
        h = h_scr[0, sl, :] + h_scr[1, sl, :]
        y = h * lax.rsqrt(jnp.mean(h * h, axis=-1, keepdims=True) + RMS_EPS) * ng_ref[...]
        gated = y * _sigmoid(o_ref[0, sl, :].astype(F32)) * _silu(z_ref[0, sl, :].astype(F32))
        out_ref[0, sl, :] = gated.astype(out_ref.dtype)
        return carry

    lax.fori_loop(0, s // f_rows, finish, 0)


def _mlstm(u, gt, conv_w, norm_g, n_ctx):
    bsz, s, _ = u.shape
    nh = MLSTM_N_HEADS
    dh = LANES
    width = nh * dh
    n_chunks = s // MIX_CHUNK
    assert n_chunks % MLSTM_GROUP == 0 and n_chunks % SCAN_UNROLL == 0 and n_ctx % MIX_CHUNK == 0
    col = lambda k: pl.BlockSpec((1, s, dh), lambda b, h, k=k: (b, 0, k * nh + h))
    par = lambda k: pl.BlockSpec((3, dh), lambda b, h, k=k: (0, k * nh + h))
    return pl.pallas_call(
        functools.partial(_mlstm_kernel, n_ctx=n_ctx),
        grid=(bsz, nh),
        in_specs=[col(0), col(1), col(2), col(3), col(4),
                  pl.BlockSpec((1, 1, 8, s), lambda b, h: (b, h, 0, 0)),
                  par(0), par(1),
                  pl.BlockSpec((1, dh), lambda b, h: (0, h))],
        out_specs=pl.BlockSpec((1, s, dh), lambda b, h: (b, 0, h)),
        out_shape=jax.ShapeDtypeStruct((bsz, s, width), BF16),
        scratch_shapes=[pltpu.VMEM((s, dh), F32), pltpu.VMEM((s, dh), F32), pltpu.VMEM((2, s, dh), F32),
                        pltpu.VMEM((2, s, 2 * dh), F32), pltpu.VMEM((2, n_chunks, dh, 2 * dh), F32),
                        pltpu.VMEM((2, s, dh), F32), pltpu.VMEM((2, s, dh), F32),
                        pltpu.VMEM((2, n_chunks, 2, dh), F32)],
        compiler_params=_compiler_params(("parallel", "arbitrary")),
        name="mlstm_mixer",
    )(u, u, u, u, u, gt, conv_w, conv_w, norm_g.reshape(1, width))


def _head_stack(x, lane_lo):
    return jnp.concatenate([jnp.where(lane_lo, x, 0.0), jnp.where(lane_lo, 0.0, x)], axis=0)


def _half_rows(x, c, upper):
    start = c if upper else 0
    return jnp.concatenate([x[r + start:r + start + c] for r in range(0, x.shape[0], 2 * c)], axis=0)


def _merge_rows(other, part, c, upper):
    pieces = []
    for k in range(part.shape[0] // c):
        pair = (other[k * c:(k + 1) * c], part[k * c:(k + 1) * c])
        pieces += pair if upper else pair[::-1]
    return jnp.concatenate(pieces, axis=0)


def _spread_rows(part, c, upper):
    return _merge_rows(jnp.zeros_like(part), part, c, upper)


def _rwkv_chunk_operators(problems, consts, eye, lane_lo):
    t, w = problems[0][0].shape
    n2 = 2 * t
    stack = lambda x: _head_stack(x, lane_lo)
    zeros = jnp.zeros((n2, w), F32)
    dirs = [p[6] for p in problems]
    rid = _iota((t, w), 0)
    cums = [p[3] for p in problems]
    sh = 1
    while sh < t:
        cums = [x + (jnp.where(rid < t - sh, pltpu.roll(x, t - sh, 0), 0.0) if d == 1 else
                     jnp.where(rid >= sh, pltpu.roll(x, sh, 0), 0.0)) for x, d in zip(cums, dirs)]
        sh *= 2
    pre = []
    for (r, v, kk, lw, ka, kt, d), cum in zip(problems, cums):
        last = 0 if d == 1 else t - 1
        cum_end = cum[last:last + 1]
        e_inv = jnp.exp(-cum)
        e_end = jnp.exp(cum_end - cum)
        a_s = stack(-kk * jnp.exp(cum - lw))
        r_s = stack(r * jnp.exp(cum))
        pre.append(dict(a_s=a_s, r_s=r_s, vs=stack(v), g=jnp.exp(cum_end),
                        ar=jnp.concatenate([a_s, r_s], axis=0),
                        bk=jnp.concatenate([stack(ka * e_inv), stack(kt * e_inv)], axis=0),
                        bk_end=jnp.concatenate([stack(ka * e_end), stack(kt * e_end)], axis=0)))
    m_alls = [_bdot_nt(q["ar"], q["bk"]) for q in pre]
    m_abs = [jnp.where(consts[d]["strict"], m[:n2, :n2], 0.0) for m, d in zip(m_alls, dirs)]
    m_aks = [jnp.where(consts[d]["strict"], m[:n2, n2:], 0.0) for m, d in zip(m_alls, dirs)]
    m_lows = [jnp.where(consts[d]["incl2"], m[n2:, :], 0.0) for m, d in zip(m_alls, dirs)]
    invs = [eye + jnp.where(consts[d]["merges"][0][1], m, 0.0) for m, d in zip(m_abs, dirs)]
    for level in range(1, len(consts[0]["merges"])):
        c = consts[0]["merges"][level][0]
        if c < SUBLANES:
            inner = [_bdot(jnp.where(consts[d]["merges"][level][1], m, 0.0), x) for m, x, d in zip(m_abs, invs, dirs)]
            invs = [x + _bdot(x, y) for x, y in zip(invs, inner)]
        else:
            ups = [d == 0 for d in dirs]
            c_rows = [jnp.where(consts[d]["merges"][level][1], _half_rows(m, c, up), 0.0) for m, d, up in zip(m_abs, dirs, ups)]
            inner = [_bdot(cr, x) for cr, x in zip(c_rows, invs)]
            x_rows = [_half_rows(x, c, up) for x, up in zip(invs, ups)]
            upd = [xr + _bdot(xr, _spread_rows(y, c, up)) for xr, y, up in zip(x_rows, inner, ups)]
            invs = [_merge_rows(_half_rows(x, c, not up), u, c, up) for x, u, up in zip(invs, upd, ups)]
    mv = [_bdot(m, q["vs"]) for m, q in zip(m_aks, pre)]
    solved = [_bdot(x, jnp.concatenate([q["a_s"], y], axis=1)) for x, q, y in zip(invs, pre, mv)]
    zms = [jnp.concatenate([sv, jnp.concatenate([zeros, q["vs"]], axis=1)], axis=0) for sv, q in zip(solved, pre)]
    ry1s = [jnp.concatenate([q["r_s"], zeros], axis=1) + _bdot(m, z) for q, m, z in zip(pre, m_lows, zms)]
    pqs = [_bdot_tn(z, q["bk_end"]) for z, q in zip(zms, pre)]
    out = []
    for ry1, pq, q in zip(ry1s, pqs, pre):
        folded = ry1[:t] + ry1[t:]
        out.append((folded[:, :w], folded[:, w:], pq[:w], pq[w:], q["g"]))
    return out


def _rwkv_kernel(rr_ref, rk_ref, rv_ref, rz_ref, wd_ref, ad_ref, mu_ref, kk_ref, ka_ref, rkk_ref,
                 lnw_ref, lnb_ref, w0_ref, a0_ref, wup_ref, aup_ref, out_ref,
                 r_scr, v_scr, kk_scr, lw_scr, ka_scr, kt_scr, bonus_scr, y_scr, ry_scr, pp_scr, qq_scr, qt_scr, g_scr,
                 *, n_ctx):
    s = rr_ref.shape[1]
    w = rr_ref.shape[2]
    p_rows = RWKV_PREP_ROWS
    t = RWKV_CHUNK
    n_chunks = s // t
    n_ctx_chunks = n_ctx // t
    head_sum = ((_iota((w, w), 0) // RWKV_HEAD) == (_iota((w, w), 1) // RWKV_HEAD)).astype(BF16)
    inv_head = 1.0 / RWKV_HEAD

    def prep(j, carry):
        sls = [pl.ds(pl.multiple_of((j * RWKV_PREP_BLOCKS + k) * p_rows, p_rows), p_rows) for k in range(RWKV_PREP_BLOCKS)]
        mixed = []
        for sl in sls:
            shifted = []
            for idx, src in enumerate((rr_ref, rk_ref, rv_ref)):
                cur, down, up = _chunk_with_neighbors(src, sl.start, p_rows, s, s - n_ctx)
                shifted.append(cur + mu_ref[idx:idx + 1, :] * (0.5 * (down + up) - cur))
            mixed.append(shifted)
        kks = [kr * kk_ref[...] for _, kr, _ in mixed]
        norms = [jnp.sqrt(_dot_sel(kk * kk, head_sum)) for kk in kks]
        kks = [kk / jnp.maximum(norm, 1e-12) for kk, norm in zip(kks, norms)]
        w_raws = [_bdot(jnp.tanh(wd_ref[0, sl, :].astype(F32)), wup_ref[0]) + w0_ref[0] for sl in sls]
        a_raws = [_bdot(ad_ref[0, sl, :], aup_ref[0]) + a0_ref[0] for sl in sls]
        kt_sums = []
        for sl, (r, kr, v), kk, w_raw, a_raw in zip(sls, mixed, kks, w_raws, a_raws):
            a = _sigmoid(a_raw)
            lw = -math.exp(-0.5) * _sigmoid(w_raw)
            kt_sum = jnp.zeros_like(kr)
            for d in range(2):
                a_d = a[:, d * w:(d + 1) * w]
                kt_d = kr * (1.0 + (a_d - 1.0) * ka_ref[...])
                kt_sum = kt_sum + kt_d
                lw_scr[d, sl, :] = lw[:, d * w:(d + 1) * w]
                ka_scr[d, sl, :] = kk * a_d
                kt_scr[d, sl, :] = kt_d
            kt_sums.append(kt_sum)
            r_scr[sl, :] = r
            v_scr[sl, :] = v
            kk_scr[sl, :] = kk
        coefs = [_dot_sel(r * kt_sum * rkk_ref[...], head_sum) for (r, _, _), kt_sum in zip(mixed, kt_sums)]
        for sl, (_, _, v), coef in zip(sls, mixed, coefs):
            bonus_scr[sl, :] = coef * v
        return carry

    lax.fori_loop(0, s // (p_rows * RWKV_PREP_BLOCKS), prep, 0)

    n2 = 2 * t
    r_i = _iota((n2, n2), 0)
    c_i = _iota((n2, n2), 1)
    same = (r_i // t) == (c_i // t)
    rt = r_i % t
    ct = c_i % t
    eye = (r_i == c_i).astype(F32)
    lane_lo = _iota((t, w), 1) < RWKV_HEAD
    consts = []
    for reverse in (False, True):
        strict = jnp.logical_and(same, (ct > rt) if reverse else (ct < rt))
        incl = jnp.logical_and(same, (ct >= rt) if reverse else (ct <= rt))
        merges = []
        c = 1
        while c < t:
            hi_r = (r_i % (2 * c)) >= c
            hi_c = (c_i % (2 * c)) >= c
            cross = jnp.logical_and(hi_c, jnp.logical_not(hi_r)) if reverse else jnp.logical_and(hi_r, jnp.logical_not(hi_c))
            mask = jnp.logical_and((r_i // (2 * c)) == (c_i // (2 * c)), cross)
            merges.append((c, _half_rows(mask, c, not reverse)) if c >= SUBLANES else (c, mask))
            c *= 2
        consts.append(dict(strict=strict, incl2=jnp.concatenate([incl, incl], axis=1), merges=merges))

    def operators(gi, carry):
        problems, where = [], []
        for k in range(RWKV_GROUP):
            chunk = gi * RWKV_GROUP + k
            sl = pl.ds(pl.multiple_of(chunk * t, t), t)
            r, v, kk = r_scr[sl, :], v_scr[sl, :], kk_scr[sl, :]
            for d in range(2):
                problems.append((r, v, kk, lw_scr[d, sl, :], ka_scr[d, sl, :], kt_scr[d, sl, :], d))
                where.append((d, chunk, sl))
        ops = _rwkv_chunk_operators(problems, consts, eye, lane_lo)
        for (d, chunk, sl), (ry, y1, pt, qt, g) in zip(where, ops):
            ry_scr[d, sl, :] = ry.astype(BF16)
            y_scr[d, sl, :] = y1
            qt_scr[d, chunk] = qt
            g_scr[d, chunk] = g
        pairs = []
        for k in range(0, RWKV_GROUP, 2):
            for d in range(2):
                first, second = (ops[2 * k + d], ops[2 * (k + 1) + d]) if d == 0 else (ops[2 * (k + 1) + d], ops[2 * k + d])
                pairs.append((d, (gi * RWKV_GROUP + k) // 2, first, second))
        prods = [_bdot(jnp.concatenate([eye * a[4] + a[2], a[3]], axis=0), b[2]) for _, _, a, b in pairs]
        for (d, pair, a, b), prod in zip(pairs, prods):
            pp_scr[d, pair] = jnp.concatenate([a[2], prod[:w] + a[2] * b[4]], axis=1).astype(BF16)
            qq_scr[d, pair] = a[3] * b[4] + prod[w:] + b[3]
        return carry

    lax.fori_loop(0, n_chunks // RWKV_GROUP, operators, 0)

    def scan(trip, carry):
        outs = []
        for k in range(RWKV_SCAN_UNROLL):
            step = trip * RWKV_SCAN_UNROLL + k
            first = _scan_chunks(2 * step, n_chunks, n_ctx_chunks)
            second = _scan_chunks(2 * step + 1, n_chunks, n_ctx_chunks)
            pair = [jnp.minimum(a, b) // 2 for a, b in zip(first, second)]
            moved = [_bdot(ht, pp_scr[d, p]) for d, (p, ht) in enumerate(zip(pair, carry))]
            mids = [ht * g_scr[d, a] + mv[:, :w] + qt_scr[d, a] for d, (a, ht, mv) in enumerate(zip(first, carry, moved))]
            for d, (a, b, ht, mid) in enumerate(zip(first, second, carry, mids)):
                sl_a = pl.ds(pl.multiple_of(a * t, t), t)
                sl_b = pl.ds(pl.multiple_of(b * t, t), t)
                outs.append((d, sl_a, _bdot_nt(ry_scr[d, sl_a, :], ht)))
                outs.append((d, sl_b, _bdot_nt(ry_scr[d, sl_b, :], mid)))
            carry = tuple(ht * (g_scr[d, a] * g_scr[d, b]) + mv[:, w:] + qq_scr[d, p]
                          for d, (a, b, p, ht, mv) in enumerate(zip(first, second, pair, carry, moved)))
        for d, sl, y in outs:
            y_scr[d, sl, :] = y + y_scr[d, sl, :]
        return carry

    zero_state = jnp.zeros((w, w), F32)
    lax.fori_loop(0, n_chunks // (2 * RWKV_SCAN_UNROLL), scan, (zero_state, zero_state))

    f_rows = RWKV_FINISH_ROWS

    def finish(j, carry):
        sl = pl.ds(pl.multiple_of(j * f_rows, f_rows), f_rows)
        y = y_scr[0, sl, :] + y_scr[1, sl, :] + bonus_scr[sl, :]
        mu = _dot_sel(y, head_sum) * inv_head
        yc = y - mu
        var = _dot_sel(yc * yc, head_sum) * inv_head
        yn = yc * lax.rsqrt(var + RWKV_LN_EPSILON) * lnw_ref[...] + lnb_ref[...]
        out_ref[0, sl, :] = (yn * _silu(rz_ref[0, sl, :].astype(F32))).astype(out_ref.dtype)
        return carry

    lax.fori_loop(0, s // f_rows, finish, 0)


def _rwkv(u, col0, p, n_ctx):
    bsz, s, _ = u.shape
    w = LANES
    width = p["mu"].shape[1]
    n_pairs = width // w
    base = col0 // w
    col = lambda k: pl.BlockSpec((1, s, w), lambda b, h, k=k: (b, 0, base + k * n_pairs + h))
    lora = lambda k: pl.BlockSpec((1, s, w), lambda b, h, k=k: (b, 0, base + 4 * n_pairs + k))
    vec = lambda rows: pl.BlockSpec((rows, w), lambda b, h: (0, h))
    cat = pl.BlockSpec((1, 1, 2 * w), lambda b, h: (h, 0, 0))
    up = pl.BlockSpec((1, w, 2 * w), lambda b, h: (h, 0, 0))
    seq = pltpu.VMEM((s, w), F32)
    seq2 = pltpu.VMEM((2, s, w), F32)
    n_chunks = s // RWKV_CHUNK
    assert n_chunks % RWKV_GROUP == 0 and s % (RWKV_PREP_ROWS * RWKV_PREP_BLOCKS) == 0 and n_ctx % RWKV_PREP_ROWS == 0
    assert s % RWKV_FINISH_ROWS == 0 and n_chunks % (2 * RWKV_SCAN_UNROLL) == 0
    assert RWKV_GROUP % 2 == 0 and (n_ctx // RWKV_CHUNK) % 2 == 0
    operators = [pltpu.VMEM((2, s, w), BF16), pltpu.VMEM((2, n_chunks // 2, w, 2 * w), BF16),
                 pltpu.VMEM((2, n_chunks // 2, w, w), F32),
                 pltpu.VMEM((2, n_chunks, w, w), F32), pltpu.VMEM((2, n_chunks, 1, w), F32)]
    return pl.pallas_call(
        functools.partial(_rwkv_kernel, n_ctx=n_ctx),
        grid=(bsz, n_pairs),
        in_specs=[col(0), col(1), col(2), col(3), lora(0), lora(1),
                  vec(3), vec(1), vec(1), vec(1), vec(1), vec(1), cat, cat, up, up],
        out_specs=pl.BlockSpec((1, s, w), lambda b, h: (b, 0, h)),
        out_shape=jax.ShapeDtypeStruct((bsz, s, width), BF16),
        scratch_shapes=[seq, seq, seq, seq2, seq2, seq2, seq, seq2] + operators,
        compiler_params=_compiler_params(("parallel", "arbitrary")),
        name="rwkv7_mixer",
    )(u, u, u, u, u, u, p["mu"], p["k_k"], p["k_a"], p["r_k"], p["ln_w"], p["ln_b"],
      p["w0"], p["a0"], p["w_up"], p["a_up"])


def _hgrn_level_masks(t, w):
    rid = _iota((t, w), 0)
    r_i = _iota((t, t), 0)
    c_i = _iota((t, t), 1)
    levels = []
    c = 1
    while c < t:
        same_block = (r_i // (2 * c)) == (c_i // (2 * c))
        up_r = (r_i % (2 * c)) >= c
        up_c = (c_i % (2 * c)) >= c
        pair = [jnp.logical_and(same_block, jnp.logical_and(up_r, jnp.logical_not(up_c))),
                jnp.logical_and(same_block, jnp.logical_and(up_c, jnp.logical_not(up_r)))]
        levels.append((c, (rid % (2 * c)) >= c, pair))
        c *= 2
    return levels


def _hgrn_chunk_operators(problems, lb, tris, levels):
    t, w = problems[0][0].shape
    r_i = _iota((t, t), 0)
    c_i = _iota((t, t), 1)
    zero_row = jnp.zeros((1, w), F32)
    dirs = [p[3] for p in problems]
    lgs, ks = [], []
    for q, v, ff, d in problems:
        e = jnp.exp(-jnp.abs(ff))
        big = 1.0 / (1.0 + e)
        small = e / (1.0 + e)
        pos = ff >= 0.0
        lgs.append(jnp.log(lb + (1.0 - lb) * jnp.where(pos, big, small)) * LOG2_E)
        ks.append((1.0 - lb) * jnp.where(pos, small, big))
    bs = [_sel_dot(tris[d], lg) for lg, d in zip(lgs, dirs)]
    befores = [_shifted(b, zero_row, zero_row)[1 if d == 1 else 0] for b, d in zip(bs, dirs)]
    edges = list(bs)
    accs = [jnp.where(r_i == c_i, jnp.sum(p[0] * k, axis=-1, keepdims=True), 0.0) for p, k in zip(problems, ks)]
    for c, upper, pair in levels:
        qts = [p[0] * jnp.exp2(b - before) for p, b, before in zip(problems, bs, befores)]
        kts = [k * jnp.exp2(edge - b) for k, b, edge in zip(ks, bs, edges)]
        prods = [_bdot_nt(qt, kt) for qt, kt in zip(qts, kts)]
        accs = [jnp.where(pair[d], pr, a) for a, pr, d in zip(accs, prods, dirs)]
        for i, d in enumerate(dirs):
            if d == 1:
                befores[i] = jnp.where(upper, befores[i], pltpu.roll(befores[i], t - c, 0))
                edges[i] = jnp.where(upper, pltpu.roll(edges[i], c, 0), edges[i])
            else:
                befores[i] = jnp.where(upper, pltpu.roll(befores[i], c, 0), befores[i])
                edges[i] = jnp.where(upper, edges[i], pltpu.roll(edges[i], t - c, 0))
    o_intras = [_bdot(a, p[1]) for a, p in zip(accs, problems)]
    b_ends = [b[(0 if d == 1 else t - 1):(1 if d == 1 else t)] for b, d in zip(bs, dirs)]
    kvs = [_bdot_tn(p[1], k * jnp.exp2(be - b)) for p, k, b, be in zip(problems, ks, bs, b_ends)]
    return [(p[0] * jnp.exp2(b), oi, kv, jnp.exp2(be)) for p, b, oi, kv, be in zip(problems, bs, o_intras, kvs, b_ends)]


def _hgrn_kernel(q_ref, i_ref, ff_ref, fb_ref, z_ref, lb_ref, ng_ref, out_ref, o_scr, qe_scr, kv_scr, g_scr,
                 *, n_ctx, layer):
    s = q_ref.shape[1]
    dh = q_ref.shape[2]
    t = MIX_CHUNK
    n_chunks = s // t
    n_ctx_chunks = n_ctx // t
    lbs = lb_ref[...]
    ex = jnp.exp(lbs - jnp.max(lbs, axis=0, keepdims=True))
    probs = ex / jnp.sum(ex, axis=0, keepdims=True)
    csum = probs[0:1]
    for l in range(1, layer + 1):
        csum = csum + probs[l:l + 1]
    lb = csum - probs[0:1]
    tri_r = _iota((t, t), 0)
    tri_c = _iota((t, t), 1)
    tris = [(tri_c <= tri_r).astype(BF16), (tri_c >= tri_r).astype(BF16)]
    levels = _hgrn_level_masks(t, dh)

    def operators(gi, carry):
        problems, where = [], []
        for kk in range(HGRN_GROUP):
            chunk = gi * HGRN_GROUP + kk
            sl = pl.ds(pl.multiple_of(chunk * t, t), t)
            q, v = q_ref[0, sl, :].astype(F32), i_ref[0, sl, :].astype(F32)
            for d, f_ref in enumerate((ff_ref, fb_ref)):
                problems.append((q, v, f_ref[0, sl, :].astype(F32), d))
                where.append((d, chunk, sl))
        for (d, chunk, sl), (qe, o_intra, kv, g) in zip(where, _hgrn_chunk_operators(problems, lb, tris, levels)):
            qe_scr[d, sl, :] = qe.astype(BF16)
            o_scr[d, sl, :] = o_intra
            kv_scr[d, chunk] = kv
            g_scr[d, chunk] = g
        return carry

    lax.fori_loop(0, n_chunks // HGRN_GROUP, operators, 0)

    def scan(trip, carry):
        steps = []
        for k in range(SCAN_UNROLL):
            chunks = _scan_chunks(trip * SCAN_UNROLL + k, n_chunks, n_ctx_chunks)
            steps += [(d, pl.ds(pl.multiple_of(chunk * t, t), t), st) for d, (chunk, st) in enumerate(zip(chunks, carry))]
            carry = tuple(st * g_scr[d, chunk] + kv_scr[d, chunk] for d, (chunk, st) in enumerate(zip(chunks, carry)))
        inters = [_bdot_nt(qe_scr[d, sl, :], st) for d, sl, st in steps]
        for (d, sl, _), inter in zip(steps, inters):
            o_scr[d, sl, :] = o_scr[d, sl, :] + inter
        return carry

    zero_state = jnp.zeros((dh, dh), F32)
    lax.fori_loop(0, n_chunks // SCAN_UNROLL, scan, (zero_state, zero_state))

    f_rows = MIX_FINISH_ROWS

    def finish(j, carry):
        sl = pl.ds(pl.multiple_of(j * f_rows, f_rows), f_rows)
        o = o_scr[0, sl, :] + o_scr[1, sl, :]
        y = o * lax.rsqrt(jnp.mean(o * o, axis=-1, keepdims=True) + RMS_EPS) * ng_ref[...]
        out_ref[0, sl, :] = (y * _silu(z_ref[0, sl, :].astype(F32))).astype(out_ref.dtype)
        return carry

    lax.fori_loop(0, (s - n_ctx) // f_rows, finish, 0)


def _hgrn(u, lb_all, norm_g, n_ctx, layer):
    bsz, s, _ = u.shape
    nh = HGRN_N_HEADS
    dh = LANES
    width = nh * dh
    depth = lb_all.shape[0]
    n_chunks = s // MIX_CHUNK
    assert n_chunks % HGRN_GROUP == 0 and n_chunks % SCAN_UNROLL == 0 and n_ctx % MIX_CHUNK == 0
    col = lambda k: pl.BlockSpec((1, s, dh), lambda b, h, k=k: (b, 0, k * nh + h))
    return pl.pallas_call(
        functools.partial(_hgrn_kernel, n_ctx=n_ctx, layer=layer),
        grid=(bsz, nh),
        in_specs=[col(0), col(1), col(2), col(3), col(4),
                  pl.BlockSpec((depth, dh), lambda b, h: (0, h)),
                  pl.BlockSpec((1, dh), lambda b, h: (0, h))],
        out_specs=pl.BlockSpec((1, s - n_ctx, dh), lambda b, h: (b, 0, h)),
        out_shape=jax.ShapeDtypeStruct((bsz, s - n_ctx, width), BF16),
        scratch_shapes=[pltpu.VMEM((2, s, dh), F32), pltpu.VMEM((2, s, dh), BF16),
                        pltpu.VMEM((2, n_chunks, dh, dh), F32), pltpu.VMEM((2, n_chunks, 1, dh), F32)],
        compiler_params=_compiler_params(("parallel", "arbitrary")),
        name="hgrn2_mixer",
    )(u, u, u, u, u, lb_all, norm_g.reshape(1, width))


def _hyena_filter_kernel(z_ref, w1_ref, b1_ref, w2_ref, b2_ref, w3f_ref, w3b_ref, dl_ref, hf_ref, hb_ref):
    hp = functools.partial(jnp.dot, precision=lax.Precision.HIGHEST, preferred_element_type=F32)
    n = z_ref.shape[0]
    hid = jnp.sin(hp(z_ref[...], w1_ref[...]) + b1_ref[...])
    hid = jnp.sin(hp(hid, w2_ref[...]) + b2_ref[...])
    pos = _iota((n, 1), 0).astype(F32) * (1.0 / n)
    window = jnp.exp(-pos * dl_ref[...]) + HYENA_SHIFT
    f0 = hp(hid, w3f_ref[...]) * window
    f1 = hp(hid, w3b_ref[...]) * window
    nrm = jnp.sum(jnp.abs(f0), axis=0, keepdims=True) + jnp.sum(jnp.abs(f1), axis=0, keepdims=True)
    hf_ref[...] = f0 / nrm
    hb_ref[...] = f1 / nrm


def _hyena_filters(n, w1, b1, w2, b2, w3, width):
    pos = np.arange(n, dtype=np.float64)
    bands = np.linspace(1e-4, HYENA_N_BANDS - 1, HYENA_N_BANDS)
    ang = (2.0 * math.pi / n) * pos[:, None] * bands
    z = np.concatenate([(pos / n)[:, None], np.cos(ang), np.sin(ang)], axis=-1)
    z = np.pad(z, ((0, 0), (0, LANES - z.shape[1]))).astype(np.float32)
    max_decay = math.log(HYENA_TGT) / HYENA_FAST
    min_decay = math.log(HYENA_TGT) / HYENA_SLOW
    deltas = np.abs(np.linspace(min_decay, max_decay, width)).astype(np.float32)[None]
    feat, hid = w1.shape
    w1p = jnp.pad(w1, ((0, LANES - feat), (0, LANES - hid)))
    w2p = jnp.pad(w2, ((0, LANES - hid), (0, LANES - hid)))
    w3p = jnp.pad(w3, ((0, LANES - hid), (0, 0)))
    b1p = jnp.pad(b1, (0, LANES - hid)).reshape(1, LANES)
    b2p = jnp.pad(b2, (0, LANES - hid)).reshape(1, LANES)
    n_tiles = width // LANES
    full = lambda shape: pl.BlockSpec(shape, lambda j: (0, 0))
    out = pl.BlockSpec((n, LANES), lambda j: (0, j))
    hf, hb = pl.pallas_call(
        _hyena_filter_kernel,
        grid=(n_tiles,),
        in_specs=[full((n, LANES)), full((LANES, LANES)), full((1, LANES)), full((LANES, LANES)), full((1, LANES)),
                  pl.BlockSpec((LANES, LANES), lambda j: (0, j)),
                  pl.BlockSpec((LANES, LANES), lambda j: (0, n_tiles + j)),
                  pl.BlockSpec((1, LANES), lambda j: (0, j))],
        out_specs=[out, out],
        out_shape=[jax.ShapeDtypeStruct((n, width), F32)] * 2,
        compiler_params=_compiler_params(("arbitrary",)),
        name="hyena_filters",
    )(jnp.asarray(z), w1p, b1p, w2p, b2p, w3p, w3p, jnp.asarray(deltas))
    return jnp.concatenate([hf, hb], axis=1)


def _hyena_pre_kernel(yv_ref, y0_ref, y1_ref, yz_ref, swv_ref, sw0_ref, sw1_ref, sbv_ref, sb0_ref, sb1_ref,
                      p_ref, e_ref, *, n_ctx):
    s = yv_ref.shape[1]
    rows = MIX_CHUNK

    def body(j, carry):
        t0 = pl.multiple_of(j * rows, rows)
        conv = []
        for src, sw, sb in ((yv_ref, swv_ref, sbv_ref), (y0_ref, sw0_ref, sb0_ref), (y1_ref, sw1_ref, sb1_ref)):
            cur, down, up = _chunk_with_neighbors(src, t0, rows, s, s - n_ctx)
            conv.append(down * sw[0:1, :] + cur * sw[1:2, :] + up * sw[2:3, :] + sb[...])
        v, x0, x1 = conv
        p = x1 * v
        o0 = pl.multiple_of(j * rows, rows)
        p_ref[0, pl.ds(o0, rows), :] = p.astype(BF16)
        e_ref[0, pl.ds(o0, rows), :] = (x0 * _silu(yz_ref[0, pl.ds(t0, rows), :].astype(F32))).astype(BF16)
        return carry

    lax.fori_loop(0, (s - n_ctx) // rows, body, 0)


def _hyena_pre(u, col0, short_w, short_b, n_ctx):
    bsz, s, _ = u.shape
    w = short_b.shape[0] // 3
    tiles = w // LANES
    base = col0 // LANES
    n = s - n_ctx
    col = lambda k: pl.BlockSpec((1, s, LANES), lambda b, j, k=k: (b, 0, base + k * tiles + j))
    par = lambda rows, k: pl.BlockSpec((rows, LANES), lambda b, j, k=k: (0, k * tiles + j))
    out = pl.BlockSpec((1, n, LANES), lambda b, j: (b, 0, j))
    sb = short_b.reshape(1, 3 * w)
    return pl.pallas_call(
        functools.partial(_hyena_pre_kernel, n_ctx=n_ctx),
        grid=(bsz, tiles),
        in_specs=[col(0), col(1), col(2), col(3),
                  par(3, 0), par(3, 1), par(3, 2), par(1, 0), par(1, 1), par(1, 2)],
        out_specs=[out, out],
        out_shape=[jax.ShapeDtypeStruct((bsz, n, w), BF16)] * 2,
        compiler_params=_compiler_params(("parallel", "arbitrary")),
        name="hyena_short_conv",
    )(u, u, u, u, short_w, short_w, short_w, sb, sb, sb)


def _dft_tables(n):
    big = 2 * n
    half = DFT_TILE // 2
    idx = jnp.arange(n, dtype=jnp.int32)
    split = DFT_SPLIT
    lo = jnp.arange(split, dtype=jnp.int32)
    hi = jnp.arange(n // split, dtype=jnp.int32)
    ang_lo = ((lo[:, None] * idx[None, :]) % big).astype(F32) * (2.0 * math.pi / big)
    ang_hi = ((hi[:, None] * idx[None, :]) % (big // split)).astype(F32) * (2.0 * math.pi * split / big)
    c_lo, s_lo = jnp.cos(ang_lo)[None], jnp.sin(ang_lo)[None]
    c_hi, s_hi = jnp.cos(ang_hi)[:, None], jnp.sin(ang_hi)[:, None]
    cos = (c_hi * c_lo - s_hi * s_lo).reshape(n, n)
    sin = (s_hi * c_lo + c_hi * s_lo).reshape(n, n)
    alt = jnp.where(idx % 2 == 0, 1.0, -1.0).astype(F32)
    first_row = (idx == 0)[:, None]
    im = jnp.where(first_row, alt[None, :], -sin)
    fwd = jnp.stack([cos.reshape(n // half, half, n), im.reshape(n // half, half, n)], axis=1).reshape(big, n)
    weight = jnp.where(first_row, 1.0, 2.0) * (1.0 / big)
    weight = jnp.stack([weight.reshape(n // half, half, 1)] * 2, axis=1).reshape(big, 1)
    return fwd.astype(BF16), (fwd * weight).T.astype(BF16)


def _spectrum_kernel(f_ref, lo_ref, hi_ref, bias_ref, o_ref):
    half = DFT_TILE // 2
    w = lo_ref.shape[1] // 2
    lo = lo_ref[...].astype(BF16)
    acc_lo = jnp.dot(f_ref[...], lo, preferred_element_type=F32)
    acc_hi = jnp.dot(f_ref[...], hi_ref[...].astype(BF16), preferred_element_type=F32)
    rows = _iota((DFT_TILE, w), 0)
    sign = jnp.where(rows % 2 == 1, -1.0, 1.0)
    real_slot = jnp.logical_or(rows < half, jnp.logical_and(rows == half, pl.program_id(0) == 0))
    conj = lambda x: jnp.where(real_slot, x, -x)
    lag0 = lo[0:1, :].astype(F32)
    a, c_all = acc_lo[:, :w], acc_lo[:, w:]
    b, d = acc_hi[:, :w], acc_hi[:, w:]
    c = c_all - jnp.where(real_slot, lag0[:, w:], 0.0)
    o_ref[:, 0:w] = a + conj(c_all) + jnp.where(real_slot, bias_ref[...], 0.0)
    o_ref[:, w:2 * w] = b + sign * (a - jnp.where(real_slot, lag0[:, :w], 0.0))
    o_ref[:, 2 * w:] = sign * conj(c) + conj(d)


def _filter_spectrum(fwd, hk, bias):
    big, h = fwd.shape
    w = hk.shape[1] // 2
    assert hk.shape[0] == 2 * h
    return pl.pallas_call(
        _spectrum_kernel,
        grid=(big // DFT_TILE,),
        in_specs=[pl.BlockSpec((DFT_TILE, h), lambda i: (i, 0)),
                  pl.BlockSpec((h, 2 * w), lambda i: (0, 0)),
                  pl.BlockSpec((h, 2 * w), lambda i: (1, 0)),
                  pl.BlockSpec((1, w), lambda i: (0, 0))],
        out_specs=pl.BlockSpec((DFT_TILE, 3 * w), lambda i: (i, 0)),
        out_shape=jax.ShapeDtypeStruct((big, 3 * w), F32),
        compiler_params=_compiler_params(("arbitrary",)),
        name="hyena_filter_spectrum",
    )(fwd, hk, hk, bias.reshape(1, w))


def _spectral_products(pairs, packed):
    re = sum(s[0] * k[0] - s[1] * k[1] for s, k in pairs)
    im = sum(s[0] * k[1] + s[1] * k[0] for s, k in pairs)
    if packed is not None:
        re = jnp.where(packed, sum(s[0] * k[0] for s, k in pairs), re)
        im = jnp.where(packed, sum(s[1] * k[1] for s, k in pairs), im)
    return jnp.concatenate([re, im], axis=0).astype(BF16)


def _conv_spectrum_kernel(f_ref, pa_ref, pb_ref, ks_ref, z_ref):
    i = pl.program_id(1)
    half = DFT_TILE // 2
    w = pa_ref.shape[2]
    units = [slice(k * DFT_TILE, (k + 1) * DFT_TILE) for k in range(DFT_UNITS)]
    accs = [(jnp.dot(f_ref[u, :], pa_ref[0], preferred_element_type=F32),
             jnp.dot(f_ref[u, :], pb_ref[0], preferred_element_type=F32)) for u in units]
    for k, (u, (acc_a, acc_b)) in enumerate(zip(units, accs)):
        s_a, s_b = (acc_a[:half], acc_a[half:]), (acc_b[:half], acc_b[half:])
        ks = ks_ref[u, :]
        k_diag, k_below, k_above = [(ks[:half, j * w:(j + 1) * w], ks[half:, j * w:(j + 1) * w]) for j in range(3)]
        packed = jnp.logical_and(_iota((half, w), 0) == 0, i == 0) if k == 0 else None
        z_ref[0, 0, u, :] = _spectral_products([(s_a, k_diag), (s_b, k_above)], packed)
        z_ref[0, 1, u, :] = _spectral_products([(s_a, k_below), (s_b, k_diag)], packed)


def _conv_spectrum(fwd, p16, kspec):
    bsz, n, w = p16.shape
    big, h = fwd.shape
    assert n == 2 * h
    rows = DFT_TILE * DFT_UNITS
    return pl.pallas_call(
        _conv_spectrum_kernel,
        grid=(bsz, big // rows),
        in_specs=[pl.BlockSpec((rows, h), lambda b, i: (i, 0)),
                  pl.BlockSpec((1, h, w), lambda b, i: (b, 0, 0)),
                  pl.BlockSpec((1, h, w), lambda b, i: (b, 1, 0)),
                  pl.BlockSpec((rows, 3 * w), lambda b, i: (i, 0))],
        out_specs=pl.BlockSpec((1, 2, rows, w), lambda b, i: (b, 0, i, 0)),
        out_shape=jax.ShapeDtypeStruct((bsz, 2, big, w), BF16),
        compiler_params=_compiler_params(("parallel", "arbitrary")),
        name="hyena_forward_dft",
    )(fwd, p16, p16, kspec)


def _conv_inverse_kernel(g_ref, z_ref, e_ref, o_ref):
    y = jnp.dot(g_ref[...], z_ref[0, 0], preferred_element_type=F32)
    o_ref[0] = (e_ref[0].astype(F32) * y).astype(o_ref.dtype)


def _conv_inverse(inv, z16, e):
    bsz, _, big, w = z16.shape
    h = inv.shape[0]
    tile = DFT_TILE
    per_half = h // tile
    tok = pl.BlockSpec((1, tile, w), lambda b, j, i: (b, j * per_half + i, 0))
    return pl.pallas_call(
        _conv_inverse_kernel,
        grid=(bsz, 2, per_half),
        in_specs=[pl.BlockSpec((tile, big), lambda b, j, i: (i, 0)),
                  pl.BlockSpec((1, 1, big, w), lambda b, j, i: (b, j, 0, 0)),
                  tok],
        out_specs=tok,
        out_shape=jax.ShapeDtypeStruct((bsz, 2 * h, w), BF16),
        compiler_params=_compiler_params(("parallel", "arbitrary", "arbitrary")),
        name="hyena_inverse_dft",
    )(inv, z16, e)


def _even_weight_layout(w_in, gate_b):
    d = w_in.shape[0]
    mw = MLSTM_N_HEADS * LANES
    g0 = 5 * mw
    g1 = g0 + 4 * MLSTM_N_HEADS
    main = jnp.concatenate([w_in[:, :g0], w_in[:, g1:]], axis=1).astype(BF16)
    wg = w_in[:, g0:g1].reshape(d, 2, 2, MLSTM_N_HEADS)
    wg = jnp.transpose(wg, (3, 1, 2, 0)).reshape(MLSTM_N_HEADS, 4, d)
    wg = jnp.concatenate([wg, jnp.zeros_like(wg)], axis=1).reshape(MLSTM_N_HEADS * 8, d).astype(BF16)
    gb = jnp.transpose(gate_b.reshape(2, 2, MLSTM_N_HEADS), (2, 0, 1)).reshape(MLSTM_N_HEADS, 4)
    gb = jnp.concatenate([gb, jnp.zeros_like(gb)], axis=1).reshape(MLSTM_N_HEADS * 8, 1)
    return main, wg, jnp.broadcast_to(gb, (MLSTM_N_HEADS * 8, LANES))


def _rwkv_params(mu, w0, w_up, a0, a_up, k_k, k_a, r_k, ln_w, ln_b):
    width = mu.shape[1]
    n_pairs = width // LANES
    row = lambda x: x.reshape(1, width)

    def cat_dirs(x):
        return jnp.transpose(x.reshape(2, n_pairs, LANES), (1, 0, 2)).reshape(n_pairs, 1, 2 * LANES)

    def block_up(x):
        lora = x.shape[1]
        xp = jnp.transpose(x.reshape(2, lora, n_pairs, LANES), (2, 0, 1, 3))
        z = jnp.zeros_like(xp[:, 0])
        top = jnp.concatenate([xp[:, 0], z], axis=2)
        bot = jnp.concatenate([z, xp[:, 1]], axis=2)
        return jnp.concatenate([top, bot], axis=1).astype(BF16)

    return {"mu": mu, "k_k": row(k_k), "k_a": row(k_a), "r_k": row(r_k), "ln_w": row(ln_w), "ln_b": row(ln_b),
            "w0": cat_dirs(w0), "a0": cat_dirs(a0), "w_up": block_up(w_up), "a_up": block_up(a_up)}


def _raster_to_column(h):
    b, n, d = h.shape
    rows = n // GRID_WIDTH
    return h.reshape(b, rows, GRID_WIDTH, d).transpose(0, 2, 1, 3).reshape(b, n, d)


def _column_to_raster(h):
    b, n, d = h.shape
    rows = n // GRID_WIDTH
    return h.reshape(b, GRID_WIDTH, rows, d).transpose(0, 2, 1, 3).reshape(b, n, d)


def kernel(x, c, ctx, c_ctx, l0_norm_g, l0_mod_w, l0_mod_b, l0_w_in, l0_w_out, l0_mlstm_conv_w, l0_mlstm_gate_b, l0_mlstm_norm_g, l0_rwkv_mu, l0_rwkv_w0, l0_rwkv_w_up, l0_rwkv_a0, l0_rwkv_a_up, l0_rwkv_k_k, l0_rwkv_k_a, l0_rwkv_r_k, l0_rwkv_ln_w, l0_rwkv_ln_b, hgrn_lower_bounds, l1_norm_g, l1_mod_w, l1_mod_b, l1_w_in, l1_w_out, l1_hgrn_norm_g, l1_hyena_short_w, l1_hyena_short_b, l1_hyena_w1, l1_hyena_b1, l1_hyena_w2, l1_hyena_b2, l1_hyena_w3, l1_hyena_bias, final_norm_g):
    bsz, n_lat, d = x.shape
    n_ctx = ctx.shape[1]

    pad = (-(bsz + 1)) % 8
    cc = jnp.concatenate([c, c_ctx[None], jnp.zeros((pad, d), F32)], axis=0)
    mod0, mod1 = _modulation(cc, l0_mod_w, l0_mod_b, l1_mod_w, l1_mod_b)
    mod0 = mod0[:bsz + 1].reshape(bsz + 1, 3, d)
    mod1 = mod1[:bsz + 1].reshape(bsz + 1, 3, d)

    w_main, w_gate, b_gate = _even_weight_layout(l0_w_in, l0_mlstm_gate_b)
    n0 = w_main.shape[1]
    u0, gt0 = _proj_in(x, ctx, l0_norm_g, mod0, w_main, n0 // 2, w_gate, b_gate)
    gt0 = gt0.reshape(bsz, MLSTM_N_HEADS, 8, n_ctx + n_lat)
    y_m = _mlstm(u0, gt0, l0_mlstm_conv_w, l0_mlstm_norm_g, n_ctx)
    rp = _rwkv_params(l0_rwkv_mu, l0_rwkv_w0, l0_rwkv_w_up, l0_rwkv_a0, l0_rwkv_a_up, l0_rwkv_k_k,
                      l0_rwkv_k_a, l0_rwkv_r_k, l0_rwkv_ln_w, l0_rwkv_ln_b)
    y_r = _rwkv(u0, 5 * MLSTM_N_HEADS * LANES, rp, n_ctx)
    x1, ctx1 = _proj_out(y_m, y_r, x, ctx, mod0, l0_w_out.astype(BF16))

    x1c = _raster_to_column(x1)
    w1 = l1_w_in.astype(BF16)
    (u1,) = _proj_in(x1c, ctx1, l1_norm_g, mod1, w1, w1.shape[1] // 2)
    y_g = _hgrn(u1, hgrn_lower_bounds, l1_hgrn_norm_g, n_ctx, layer=1)
    hw = l1_hyena_bias.shape[0]
    hk = _hyena_filters(n_lat, l1_hyena_w1, l1_hyena_b1, l1_hyena_w2, l1_hyena_b2, l1_hyena_w3, hw)
    fwd, inv = _dft_tables(n_lat // 2)
    kspec = _filter_spectrum(fwd, hk, l1_hyena_bias)
    p16, e16 = _hyena_pre(u1, 5 * HGRN_N_HEADS * LANES, l1_hyena_short_w, l1_hyena_short_b, n_ctx)
    z16 = _conv_spectrum(fwd, p16, kspec)
    y_y = _conv_inverse(inv, z16, e16)
    out_c = _proj_out_final(y_g, y_y, x1c, mod1, l1_w_out.astype(BF16), final_norm_g)
    return _column_to_raster(out_c)
```

```python
import functools
import math

import jax
import jax.numpy as jnp
import numpy as np
from jax import lax
from jax.experimental import pallas as pl
from jax.experimental.pallas import tpu as pltpu

F32 = jnp.float32
BF16 = jnp.bfloat16

GRID_WIDTH = 64
RMS_EPS = 1e-6
MLSTM_N_HEADS = 4
RWKV_HEAD = 64
RWKV_LN_EPSILON = 64e-5
HGRN_N_HEADS = 4
HYENA_N_BANDS = 16
HYENA_FAST = 0.3
HYENA_SLOW = 1.5
HYENA_TGT = 1e-2
HYENA_SHIFT = 0.05
LOG2_E = 1.0 / math.log(2.0)

LANES = 128
SUBLANES = 8
ROW_GROUP = 16
VMEM_LIMIT = 52 * 1024 * 1024

MIX_CHUNK = 128
RWKV_CHUNK = 64
RWKV_GROUP = 12
MIX_PREP_ROWS = 256
MIX_FINISH_ROWS = 256
RWKV_PREP_ROWS = 256
RWKV_PREP_BLOCKS = 3
RWKV_FINISH_ROWS = 768
RWKV_SCAN_UNROLL = 3
SCAN_UNROLL = 3
MLSTM_GROUP = 9
HGRN_GROUP = 6
PROJ_ROWS = 768
FINAL_ROWS = 512
DFT_TILE = 512
DFT_UNITS = 2
DFT_SPLIT = 64


def _bdot(a, b):
    return jnp.dot(a.astype(BF16), b.astype(BF16), preferred_element_type=F32)


def _bdot_nt(a, b):
    return lax.dot_general(a.astype(BF16), b.astype(BF16), (((1,), (1,)), ((), ())),
                           preferred_element_type=F32)


def _bdot_tn(a, b):
    return lax.dot_general(a.astype(BF16), b.astype(BF16), (((0,), (0,)), ((), ())),
                           preferred_element_type=F32)


def _split3(x):
    hi = x.astype(BF16)
    r1 = x - hi.astype(F32)
    mid = r1.astype(BF16)
    lo = (r1 - mid.astype(F32)).astype(BF16)
    return hi, mid, lo


def _sel_dot(sel, x):
    hi, mid, lo = _split3(x)
    d = functools.partial(jnp.dot, preferred_element_type=F32)
    return d(sel, hi) + d(sel, mid) + d(sel, lo)


def _dot_sel(x, sel):
    hi = x.astype(BF16)
    mid = (x - hi.astype(F32)).astype(BF16)
    d = functools.partial(jnp.dot, preferred_element_type=F32)
    return d(hi, sel) + d(mid, sel)


def _sigmoid(x):
    return 1.0 / (1.0 + jnp.exp(-x))


def _silu(x):
    return x * _sigmoid(x)


def _iota(shape, dim):
    return lax.broadcasted_iota(jnp.int32, shape, dim)


def _neighbor_rows(ref, t0, rows, n_total, split):
    has_prev = jnp.logical_and(t0 != 0, t0 != split)
    has_next = jnp.logical_and(t0 + rows != split, t0 + rows != n_total)
    g = ROW_GROUP
    before = ref[0, pl.ds(pl.multiple_of(jnp.maximum(t0 - g, 0), g), g), :].astype(F32)
    after = ref[0, pl.ds(pl.multiple_of(jnp.minimum(t0 + rows, n_total - g), g), g), :].astype(F32)
    return jnp.where(has_prev, before[g - 1:g], 0.0), jnp.where(has_next, after[0:1], 0.0)


def _shifted(cur, prev_row, next_row):
    rows = cur.shape[0]
    rid = _iota(cur.shape, 0)
    down = jnp.where(rid == 0, prev_row, pltpu.roll(cur, 1, 0))
    up = jnp.where(rid == rows - 1, next_row, pltpu.roll(cur, rows - 1, 0))
    return down, up


def _chunk_with_neighbors(ref, t0, rows, n_total, split):
    cur = ref[0, pl.ds(t0, rows), :].astype(F32)
    prev_row, next_row = _neighbor_rows(ref, t0, rows, n_total, split)
    down, up = _shifted(cur, prev_row, next_row)
    return cur, down, up


def _scan_chunks(i, n_chunks, n_ctx_chunks):
    fwd = jnp.where(i < n_ctx_chunks, n_chunks - n_ctx_chunks + i, i - n_ctx_chunks)
    return fwd, n_chunks - 1 - i


def _compiler_params(semantics):
    return pltpu.CompilerParams(dimension_semantics=semantics, vmem_limit_bytes=VMEM_LIMIT)


def _mod_kernel(c_ref, w0_ref, b0_ref, w1_ref, b1_ref, o0_ref, o1_ref):
    s = _silu(c_ref[...])
    o0_ref[...] = _bdot(s, w0_ref[...]) + b0_ref[...]
    o1_ref[...] = _bdot(s, w1_ref[...]) + b1_ref[...]


def _modulation(cc, w0, b0, w1, b1):
    rows, d = cc.shape
    n = w0.shape[1]
    tile = d
    grid = (n // tile,)
    wspec = pl.BlockSpec((d, tile), lambda j: (0, j))
    bspec = pl.BlockSpec((1, tile), lambda j: (0, j))
    ospec = pl.BlockSpec((rows, tile), lambda j: (0, j))
    return pl.pallas_call(
        _mod_kernel,
        grid=grid,
        in_specs=[pl.BlockSpec((rows, d), lambda j: (0, 0)), wspec, bspec, wspec, bspec],
        out_specs=[ospec, ospec],
        out_shape=[jax.ShapeDtypeStruct((rows, n), F32)] * 2,
        compiler_params=_compiler_params(("arbitrary",)),
        name="adaln_modulation",
    )(cc, w0, b0.reshape(1, n), w1, b1.reshape(1, n))


def _token_tile(x_ref, c_ref, last_of_batch, rows):
    n_lat_tail = rows - c_ref.shape[1]
    is_ctx = jnp.logical_and(last_of_batch, _iota((rows, 1), 0) >= n_lat_tail)
    ctx_rows = jnp.concatenate([jnp.zeros((n_lat_tail, c_ref.shape[2]), F32), c_ref[0]], axis=0)
    return jnp.where(is_ctx, ctx_rows, x_ref[0]), is_ctx


def _proj_in_kernel(*refs, rows, with_gates):
    if with_gates:
        x_ref, c_ref, g_ref, ml_ref, mc_ref, w_ref, wg_ref, gb_ref, u_ref, gt_ref, h_scr = refs
    else:
        x_ref, c_ref, g_ref, ml_ref, mc_ref, w_ref, u_ref, h_scr = refs
    i = pl.program_id(1)
    n = pl.program_id(2)

    @pl.when(n == 0)
    def _():
        x, is_ctx = _token_tile(x_ref, c_ref, i == pl.num_programs(1) - 1, rows)
        y = x * lax.rsqrt(jnp.mean(x * x, axis=-1, keepdims=True) + RMS_EPS) * g_ref[...]
        ml = ml_ref[0]
        mc = mc_ref[0]
        shift = jnp.where(is_ctx, mc[0:1], ml[0:1])
        scale = jnp.where(is_ctx, mc[1:2], ml[1:2])
        h = (y * (1.0 + scale) + shift).astype(BF16)
        h_scr[...] = h
        if with_gates:
            gt_ref[0] = _bdot_nt(wg_ref[...], h) + gb_ref[:, 0:1]

    u_ref[0] = jnp.dot(h_scr[...], w_ref[...], preferred_element_type=F32).astype(u_ref.dtype)


def _proj_in(x, ctx, norm_g, mod3, w16, n_tile, gate_w=None, gate_b=None):
    bsz, n_lat, d = x.shape
    n_ctx = ctx.shape[1]
    s = n_lat + n_ctx
    n = w16.shape[1]
    rows = PROJ_ROWS
    assert s % rows == 0 and (n_lat % rows) + n_ctx == rows
    grid = (bsz, s // rows, n // n_tile)
    ctx_row = mod3.shape[0] - 1
    with_gates = gate_w is not None
    in_specs = [
        pl.BlockSpec((1, rows, d), lambda b, i, j: (b, i, 0)),
        pl.BlockSpec((1, n_ctx, d), lambda b, i, j: (b, 0, 0)),
        pl.BlockSpec((1, d), lambda b, i, j: (0, 0)),
        pl.BlockSpec((1, 3, d), lambda b, i, j: (b, 0, 0)),
        pl.BlockSpec((1, 3, d), lambda b, i, j: (ctx_row, 0, 0)),
        pl.BlockSpec((d, n_tile), lambda b, i, j: (0, j)),
    ]
    args = [x, ctx, norm_g.reshape(1, d), mod3, mod3, w16]
    out_specs = [pl.BlockSpec((1, rows, n_tile), lambda b, i, j: (b, i, j))]
    out_shape = [jax.ShapeDtypeStruct((bsz, s, n), BF16)]
    if with_gates:
        ng = gate_w.shape[0]
        in_specs += [pl.BlockSpec((ng, d), lambda b, i, j: (0, 0)),
                     pl.BlockSpec((ng, LANES), lambda b, i, j: (0, 0))]
        args += [gate_w, gate_b]
        out_specs.append(pl.BlockSpec((1, ng, rows), lambda b, i, j: (b, 0, i)))
        out_shape.append(jax.ShapeDtypeStruct((bsz, ng, s), F32))
    return pl.pallas_call(
        functools.partial(_proj_in_kernel, rows=rows, with_gates=with_gates),
        grid=grid,
        in_specs=in_specs,
        out_specs=out_specs,
        out_shape=out_shape,
        scratch_shapes=[pltpu.VMEM((rows, d), BF16)],
        compiler_params=_compiler_params(("parallel", "arbitrary", "arbitrary")),
        name="norm_mod_proj_in",
    )(*args)


def _proj_out_kernel(ya_ref, yb_ref, x_ref, c_ref, ml_ref, mc_ref, w_ref, ox_ref, oc_ref, *, rows):
    i = pl.program_id(1)
    half = ya_ref.shape[2]
    y = _bdot(ya_ref[0], w_ref[0:half, :]) + _bdot(yb_ref[0], w_ref[half:, :])
    x, is_ctx = _token_tile(x_ref, c_ref, i == pl.num_programs(1) - 1, rows)
    x = x + jnp.where(is_ctx, mc_ref[0][2:3], ml_ref[0][2:3]) * y
    ox_ref[0] = x

    @pl.when(i == pl.num_programs(1) - 1)
    def _():
        oc_ref[0] = x[rows - c_ref.shape[1]:]


def _proj_out(ya, yb, x, ctx, mod3, w16):
    bsz, n_lat, d = x.shape
    n_ctx = ctx.shape[1]
    s = n_lat + n_ctx
    half = ya.shape[2]
    rows = PROJ_ROWS
    assert s % rows == 0 and (n_lat % rows) + n_ctx == rows
    ctx_row = mod3.shape[0] - 1
    tok = lambda w: pl.BlockSpec((1, rows, w), lambda b, i: (b, i, 0))
    seg = pl.BlockSpec((1, n_ctx, d), lambda b, i: (b, 0, 0))
    return pl.pallas_call(
        functools.partial(_proj_out_kernel, rows=rows),
        grid=(bsz, s // rows),
        in_specs=[tok(half), tok(half), tok(d), seg,
                  pl.BlockSpec((1, 3, d), lambda b, i: (b, 0, 0)),
                  pl.BlockSpec((1, 3, d), lambda b, i: (ctx_row, 0, 0)),
                  pl.BlockSpec((2 * half, d), lambda b, i: (0, 0))],
        out_specs=[tok(d), seg],
        out_shape=[jax.ShapeDtypeStruct((bsz, n_lat, d), F32), jax.ShapeDtypeStruct((bsz, n_ctx, d), F32)],
        compiler_params=_compiler_params(("parallel", "arbitrary")),
        name="proj_out_residual",
    )(ya, yb, x, ctx, mod3, mod3, w16)


def _proj_out_final_kernel(ya_ref, yb_ref, x_ref, ml_ref, w_ref, fg_ref, o_ref):
    half = ya_ref.shape[2]
    y = _bdot(ya_ref[0], w_ref[0:half, :]) + _bdot(yb_ref[0], w_ref[half:, :])
    x = x_ref[0] + ml_ref[0][2:3] * y
    o_ref[0] = x * lax.rsqrt(jnp.mean(x * x, axis=-1, keepdims=True) + RMS_EPS) * fg_ref[...]


def _proj_out_final(ya, yb, x, mod3, w16, final_g):
    bsz, n_lat, d = x.shape
    half = ya.shape[2]
    rows = FINAL_ROWS
    assert n_lat % rows == 0
    tok = lambda w: pl.BlockSpec((1, rows, w), lambda b, i: (b, i, 0))
    return pl.pallas_call(
        _proj_out_final_kernel,
        grid=(bsz, n_lat // rows),
        in_specs=[tok(half), tok(half), tok(d),
                  pl.BlockSpec((1, 3, d), lambda b, i: (b, 0, 0)),
                  pl.BlockSpec((2 * half, d), lambda b, i: (0, 0)),
                  pl.BlockSpec((1, d), lambda b, i: (0, 0))],
        out_specs=tok(d),
        out_shape=jax.ShapeDtypeStruct((bsz, n_lat, d), F32),
        compiler_params=_compiler_params(("parallel", "arbitrary")),
        name="proj_out_final_norm",
    )(ya, yb, x, mod3, w16, final_g.reshape(1, d))


def _mlstm_chunk_operators(chunks, causal):
    t = chunks[0][0].shape[0]
    lane = _iota((8, t), 1)
    row_id = _iota((8, t), 0)
    log_fs = [jnp.minimum(c[3], 0.0) - jnp.log1p(jnp.exp(-jnp.abs(c[3]))) for c in chunks]
    cum_f, cum_b = list(log_fs), list(log_fs)
    sh = 1
    while sh < t:
        cum_f = [x + jnp.where(lane >= sh, pltpu.roll(x, sh, 1), 0.0) for x in cum_f]
        cum_b = [x + jnp.where(lane < t - sh, pltpu.roll(x, t - sh, 1), 0.0) for x in cum_b]
        sh *= 2
    pad = jnp.zeros((t - 8, t), F32)
    tiles = [jnp.concatenate([jnp.where(row_id % 2 == 0, c[3], jnp.where(row_id == 1, f, b)), pad], axis=0)
             for c, f, b in zip(chunks, cum_f, cum_b)]
    cols = [x.T for x in tiles]
    k_ts = [c[1].T for c in chunks]
    problems = []
    for c, f, b, col, k_t in zip(chunks, cum_f, cum_b, cols, k_ts):
        for d in range(2):
            b_row = (f, b)[d][2 * d + 1:2 * d + 2]
            problems.append(dict(q=c[0], k=c[1], k_t=k_t, v_ext=c[2], d=d, ig_row=c[3][2 * d:2 * d + 1], b_row=b_row,
                                 b_col=col[:, 2 * d + 1:2 * d + 2]))
    logws = [jnp.where(causal[p["d"]], p["b_col"] + (p["ig_row"] - p["b_row"]), -jnp.inf) for p in problems]
    mus = [jnp.max(x, axis=-1, keepdims=True) for x in logws]
    ws = [jnp.exp(x - mu) for x, mu in zip(logws, mus)]
    lasts = [0 if p["d"] == 1 else t - 1 for p in problems]
    b_lasts = [p["b_col"][i:i + 1] for p, i in zip(problems, lasts)]
    gammas = [mu[i:i + 1] for mu, i in zip(mus, lasts)]
    gk_ts = [p["k_t"] * jnp.exp(bl - p["b_row"] + p["ig_row"] - gm) for p, bl, gm in zip(problems, b_lasts, gammas)]
    qks = [_bdot_nt(p["q"], p["k"]) * w for p, w in zip(problems, ws)]
    intras = [_bdot(qk, p["v_ext"]) for qk, p in zip(qks, problems)]
    kvs = [_bdot(gk_t, p["v_ext"]) for gk_t, p in zip(gk_ts, problems)]
    dh = chunks[0][0].shape[1]
    return [(intra, kv, jnp.broadcast_to(mu - p["b_col"], (t, dh)), jnp.broadcast_to(mu, (t, dh)), bl, gm)
            for intra, kv, p, mu, bl, gm in zip(intras, kvs, problems, mus, b_lasts, gammas)]


def _mlstm_kernel(q_ref, k_ref, v_ref, o_ref, z_ref, gt_ref, cwq_ref, cwk_ref, ng_ref, out_ref,
                  qa_scr, ka_scr, h_scr, intra_scr, kv_scr, delta_scr, mu_scr, tail_scr, *, n_ctx):
    s = q_ref.shape[1]
    dh = q_ref.shape[2]
    t = MIX_CHUNK
    n_chunks = s // t
    n_ctx_chunks = n_ctx // t
    k_scale = dh ** -0.5

    p_rows = MIX_PREP_ROWS

    def prep(j, carry):
        t0 = pl.multiple_of(j * p_rows, p_rows)
        for src, cw, dst, scale in ((q_ref, cwq_ref, qa_scr, 1.0), (k_ref, cwk_ref, ka_scr, k_scale)):
            cur, down, up = _chunk_with_neighbors(src, t0, p_rows, s, s - n_ctx)
            conv = down * cw[0:1, :] + cur * cw[1:2, :] + up * cw[2:3, :]
            dst[pl.ds(t0, p_rows), :] = _silu(conv) * scale
        return carry

    lax.fori_loop(0, s // p_rows, prep, 0)

    ones_col = jnp.ones((t, dh), F32)
    causal = [_iota((t, t), 1) <= _iota((t, t), 0), _iota((t, t), 1) >= _iota((t, t), 0)]

    def operators(gi, carry):
        chunks, where = [], []
        for kk in range(MLSTM_GROUP):
            chunk = gi * MLSTM_GROUP + kk
            sl = pl.ds(pl.multiple_of(chunk * t, t), t)
            v_ext = jnp.concatenate([v_ref[0, sl, :].astype(F32), ones_col], axis=1)
            chunks.append((qa_scr[sl, :], ka_scr[sl, :], v_ext, gt_ref[0, 0, :, sl]))
            where += [(0, chunk, sl), (1, chunk, sl)]
        for (d, chunk, sl), (intra, kv, delta, mu, b_last, gamma) in zip(where, _mlstm_chunk_operators(chunks, causal)):
            intra_scr[d, sl, :] = intra
            kv_scr[d, chunk] = kv
            delta_scr[d, sl, :] = delta
            mu_scr[d, sl, :] = mu
            tail_scr[d, chunk] = jnp.concatenate([jnp.broadcast_to(b_last, (1, dh)), jnp.broadcast_to(gamma, (1, dh))],
                                                 axis=0)
        return carry

    lax.fori_loop(0, n_chunks // MLSTM_GROUP, operators, 0)

    def scan(trip, carry):
        steps = []
        for k in range(SCAN_UNROLL):
            chunks = _scan_chunks(trip * SCAN_UNROLL + k, n_chunks, n_ctx_chunks)
            new = []
            for d, (chunk, (c_ext, m)) in enumerate(zip(chunks, carry)):
                steps.append((d, chunk, pl.ds(pl.multiple_of(chunk * t, t), t), c_ext, m))
                tail = tail_scr[d, chunk]
                b_last, gamma = tail[0:1, 0:1], tail[1:2, 0:1]
                m_new = jnp.maximum(b_last + m, gamma)
                new.append((jnp.exp(b_last + m - m_new) * c_ext + jnp.exp(gamma - m_new) * kv_scr[d, chunk], m_new))
            carry = tuple(new)
        inters = [_bdot(qa_scr[sl, :], c_ext) for _, _, sl, c_ext, _ in steps]
        for (d, chunk, sl, _, m), inter in zip(steps, inters):
            z = delta_scr[d, sl, :] - m
            s_inter = jnp.exp(-jnp.maximum(z, 0.0))
            s_intra = jnp.exp(jnp.minimum(z, 0.0))
            floor = jnp.exp(jnp.minimum(z, 0.0) - mu_scr[d, sl, :])
            intra = intra_scr[d, sl, :]
            num = s_inter * inter[:, :dh] + s_intra * intra[:, :dh]
            den = s_inter * inter[:, dh:] + s_intra * intra[:, dh:]
            h_scr[d, sl, :] = num / jnp.maximum(jnp.abs(den), floor)
        return carry

    zero = (jnp.zeros((dh, 2 * dh), F32), jnp.zeros((1, 1), F32))
    lax.fori_loop(0, n_chunks // SCAN_UNROLL, scan, (zero, zero))

    f_rows = MIX_FINISH_ROWS

    def finish(j, carry):
        sl = pl.ds(pl.multiple_of(j * f_rows, f_rows), f_rows)
        h = h_scr[0, sl, :] + h_scr[1, sl, :]
        y = h * lax.rsqrt(jnp.mean(h * h, axis=-1, keepdims=True) + RMS_EPS) * ng_ref[...]
        gated = y * _sigmoid(o_ref[0, sl, :].astype(F32)) * _silu(z_ref[0, sl, :].astype(F32))
        out_ref[0, sl, :] = gated.astype(out_ref.dtype)
        return carry

    lax.fori_loop(0, s // f_rows, finish, 0)


def _mlstm(u, gt, conv_w, norm_g, n_ctx):
    bsz, s, _ = u.shape
    nh = MLSTM_N_HEADS
    dh = LANES
    width = nh * dh
    n_chunks = s // MIX_CHUNK
    assert n_chunks % MLSTM_GROUP == 0 and n_chunks % SCAN_UNROLL == 0 and n_ctx % MIX_CHUNK == 0
    col = lambda k: pl.BlockSpec((1, s, dh), lambda b, h, k=k: (b, 0, k * nh + h))
    par = lambda k: pl.BlockSpec((3, dh), lambda b, h, k=k: (0, k * nh + h))
    return pl.pallas_call(
        functools.partial(_mlstm_kernel, n_ctx=n_ctx),
        grid=(bsz, nh),
        in_specs=[col(0), col(1), col(2), col(3), col(4),
                  pl.BlockSpec((1, 1, 8, s), lambda b, h: (b, h, 0, 0)),
                  par(0), par(1),
                  pl.BlockSpec((1, dh), lambda b, h: (0, h))],
        out_specs=pl.BlockSpec((1, s, dh), lambda b, h: (b, 0, h)),
        out_shape=jax.ShapeDtypeStruct((bsz, s, width), BF16),
        scratch_shapes=[pltpu.VMEM((s, dh), F32), pltpu.VMEM((s, dh), F32), pltpu.VMEM((2, s, dh), F32),
                        pltpu.VMEM((2, s, 2 * dh), F32), pltpu.VMEM((2, n_chunks, dh, 2 * dh), F32),
                        pltpu.VMEM((2, s, dh), F32), pltpu.VMEM((2, s, dh), F32),
                        pltpu.VMEM((2, n_chunks, 2, dh), F32)],
        compiler_params=_compiler_params(("parallel", "arbitrary")),
        name="mlstm_mixer",
    )(u, u, u, u, u, gt, conv_w, conv_w, norm_g.reshape(1, width))


def _head_stack(x, lane_lo):
    return jnp.concatenate([jnp.where(lane_lo, x, 0.0), jnp.where(lane_lo, 0.0, x)], axis=0)


def _half_rows(x, c, upper):
    start = c if upper else 0
    return jnp.concatenate([x[r + start:r + start + c] for r in range(0, x.shape[0], 2 * c)], axis=0)


def _merge_rows(other, part, c, upper):
    pieces = []
    for k in range(part.shape[0] // c):
        pair = (other[k * c:(k + 1) * c], part[k * c:(k + 1) * c])
        pieces += pair if upper else pair[::-1]
    return jnp.concatenate(pieces, axis=0)


def _spread_rows(part, c, upper):
    return _merge_rows(jnp.zeros_like(part), part, c, upper)


def _rwkv_chunk_operators(problems, consts, eye, lane_lo):
    t, w = problems[0][0].shape
    n2 = 2 * t
    stack = lambda x: _head_stack(x, lane_lo)
    zeros = jnp.zeros((n2, w), F32)
    dirs = [p[6] for p in problems]
    rid = _iota((t, w), 0)
    cums = [p[3] for p in problems]
    sh = 1
    while sh < t:
        cums = [x + (jnp.where(rid < t - sh, pltpu.roll(x, t - sh, 0), 0.0) if d == 1 else
                     jnp.where(rid >= sh, pltpu.roll(x, sh, 0), 0.0)) for x, d in zip(cums, dirs)]
        sh *= 2
    pre = []
    for (r, v, kk, lw, ka, kt, d), cum in zip(problems, cums):
        last = 0 if d == 1 else t - 1
        cum_end = cum[last:last + 1]
        e_inv = jnp.exp(-cum)
        e_end = jnp.exp(cum_end - cum)
        a_s = stack(-kk * jnp.exp(cum - lw))
        r_s = stack(r * jnp.exp(cum))
        pre.append(dict(a_s=a_s, r_s=r_s, vs=stack(v), g=jnp.exp(cum_end),
                        ar=jnp.concatenate([a_s, r_s], axis=0),
                        bk=jnp.concatenate([stack(ka * e_inv), stack(kt * e_inv)], axis=0),
                        bk_end=jnp.concatenate([stack(ka * e_end), stack(kt * e_end)], axis=0)))
    m_alls = [_bdot_nt(q["ar"], q["bk"]) for q in pre]
    m_abs = [jnp.where(consts[d]["strict"], m[:n2, :n2], 0.0) for m, d in zip(m_alls, dirs)]
    m_aks = [jnp.where(consts[d]["strict"], m[:n2, n2:], 0.0) for m, d in zip(m_alls, dirs)]
    m_lows = [jnp.where(consts[d]["incl2"], m[n2:, :], 0.0) for m, d in zip(m_alls, dirs)]
    invs = [eye + jnp.where(consts[d]["merges"][0][1], m, 0.0) for m, d in zip(m_abs, dirs)]
    for level in range(1, len(consts[0]["merges"])):
        c = consts[0]["merges"][level][0]
        if c < SUBLANES:
            inner = [_bdot(jnp.where(consts[d]["merges"][level][1], m, 0.0), x) for m, x, d in zip(m_abs, invs, dirs)]
            invs = [x + _bdot(x, y) for x, y in zip(invs, inner)]
        else:
            ups = [d == 0 for d in dirs]
            c_rows = [jnp.where(consts[d]["merges"][level][1], _half_rows(m, c, up), 0.0) for m, d, up in zip(m_abs, dirs, ups)]
            inner = [_bdot(cr, x) for cr, x in zip(c_rows, invs)]
            x_rows = [_half_rows(x, c, up) for x, up in zip(invs, ups)]
            upd = [xr + _bdot(xr, _spread_rows(y, c, up)) for xr, y, up in zip(x_rows, inner, ups)]
            invs = [_merge_rows(_half_rows(x, c, not up), u, c, up) for x, u, up in zip(invs, upd, ups)]
    mv = [_bdot(m, q["vs"]) for m, q in zip(m_aks, pre)]
    solved = [_bdot(x, jnp.concatenate([q["a_s"], y], axis=1)) for x, q, y in zip(invs, pre, mv)]
    zms = [jnp.concatenate([sv, jnp.concatenate([zeros, q["vs"]], axis=1)], axis=0) for sv, q in zip(solved, pre)]
    ry1s = [jnp.concatenate([q["r_s"], zeros], axis=1) + _bdot(m, z) for q, m, z in zip(pre, m_lows, zms)]
    pqs = [_bdot_tn(z, q["bk_end"]) for z, q in zip(zms, pre)]
    out = []
    for ry1, pq, q in zip(ry1s, pqs, pre):
        folded = ry1[:t] + ry1[t:]
        out.append((folded[:, :w], folded[:, w:], pq[:w], pq[w:], q["g"]))
    return out


def _rwkv_kernel(rr_ref, rk_ref, rv_ref, rz_ref, wd_ref, ad_ref, mu_ref, kk_ref, ka_ref, rkk_ref,
                 lnw_ref, lnb_ref, w0_ref, a0_ref, wup_ref, aup_ref, out_ref,
                 r_scr, v_scr, kk_scr, lw_scr, ka_scr, kt_scr, bonus_scr, y_scr, ry_scr, pp_scr, qq_scr, qt_scr, g_scr,
                 *, n_ctx):
    s = rr_ref.shape[1]
    w = rr_ref.shape[2]
    p_rows = RWKV_PREP_ROWS
    t = RWKV_CHUNK
    n_chunks = s // t
    n_ctx_chunks = n_ctx // t
    head_sum = ((_iota((w, w), 0) // RWKV_HEAD) == (_iota((w, w), 1) // RWKV_HEAD)).astype(BF16)
    inv_head = 1.0 / RWKV_HEAD

    def prep(j, carry):
        sls = [pl.ds(pl.multiple_of((j * RWKV_PREP_BLOCKS + k) * p_rows, p_rows), p_rows) for k in range(RWKV_PREP_BLOCKS)]
        mixed = []
        for sl in sls:
            shifted = []
            for idx, src in enumerate((rr_ref, rk_ref, rv_ref)):
                cur, down, up = _chunk_with_neighbors(src, sl.start, p_rows, s, s - n_ctx)
                shifted.append(cur + mu_ref[idx:idx + 1, :] * (0.5 * (down + up) - cur))
            mixed.append(shifted)
        kks = [kr * kk_ref[...] for _, kr, _ in mixed]
        norms = [jnp.sqrt(_dot_sel(kk * kk, head_sum)) for kk in kks]
        kks = [kk / jnp.maximum(norm, 1e-12) for kk, norm in zip(kks, norms)]
        w_raws = [_bdot(jnp.tanh(wd_ref[0, sl, :].astype(F32)), wup_ref[0]) + w0_ref[0] for sl in sls]
        a_raws = [_bdot(ad_ref[0, sl, :], aup_ref[0]) + a0_ref[0] for sl in sls]
        kt_sums = []
        for sl, (r, kr, v), kk, w_raw, a_raw in zip(sls, mixed, kks, w_raws, a_raws):
            a = _sigmoid(a_raw)
            lw = -math.exp(-0.5) * _sigmoid(w_raw)
            kt_sum = jnp.zeros_like(kr)
            for d in range(2):
                a_d = a[:, d * w:(d + 1) * w]
                kt_d = kr * (1.0 + (a_d - 1.0) * ka_ref[...])
                kt_sum = kt_sum + kt_d
                lw_scr[d, sl, :] = lw[:, d * w:(d + 1) * w]
                ka_scr[d, sl, :] = kk * a_d
                kt_scr[d, sl, :] = kt_d
            kt_sums.append(kt_sum)
            r_scr[sl, :] = r
            v_scr[sl, :] = v
            kk_scr[sl, :] = kk
        coefs = [_dot_sel(r * kt_sum * rkk_ref[...], head_sum) for (r, _, _), kt_sum in zip(mixed, kt_sums)]
        for sl, (_, _, v), coef in zip(sls, mixed, coefs):
            bonus_scr[sl, :] = coef * v
        return carry

    lax.fori_loop(0, s // (p_rows * RWKV_PREP_BLOCKS), prep, 0)

    n2 = 2 * t
    r_i = _iota((n2, n2), 0)
    c_i = _iota((n2, n2), 1)
    same = (r_i // t) == (c_i // t)
    rt = r_i % t
    ct = c_i % t
    eye = (r_i == c_i).astype(F32)
    lane_lo = _iota((t, w), 1) < RWKV_HEAD
    consts = []
    for reverse in (False, True):
        strict = jnp.logical_and(same, (ct > rt) if reverse else (ct < rt))
        incl = jnp.logical_and(same, (ct >= rt) if reverse else (ct <= rt))
        merges = []
        c = 1
        while c < t:
            hi_r = (r_i % (2 * c)) >= c
            hi_c = (c_i % (2 * c)) >= c
            cross = jnp.logical_and(hi_c, jnp.logical_not(hi_r)) if reverse else jnp.logical_and(hi_r, jnp.logical_not(hi_c))
            mask = jnp.logical_and((r_i // (2 * c)) == (c_i // (2 * c)), cross)
            merges.append((c, _half_rows(mask, c, not reverse)) if c >= SUBLANES else (c, mask))
            c *= 2
        consts.append(dict(strict=strict, incl2=jnp.concatenate([incl, incl], axis=1), merges=merges))

    def operators(gi, carry):
        problems, where = [], []
        for k in range(RWKV_GROUP):
            chunk = gi * RWKV_GROUP + k
            sl = pl.ds(pl.multiple_of(chunk * t, t), t)
            r, v, kk = r_scr[sl, :], v_scr[sl, :], kk_scr[sl, :]
            for d in range(2):
                problems.append((r, v, kk, lw_scr[d, sl, :], ka_scr[d, sl, :], kt_scr[d, sl, :], d))
                where.append((d, chunk, sl))
        ops = _rwkv_chunk_operators(problems, consts, eye, lane_lo)
        for (d, chunk, sl), (ry, y1, pt, qt, g) in zip(where, ops):
            ry_scr[d, sl, :] = ry.astype(BF16)
            y_scr[d, sl, :] = y1
            qt_scr[d, chunk] = qt
            g_scr[d, chunk] = g
        pairs = []
        for k in range(0, RWKV_GROUP, 2):
            for d in range(2):
                first, second = (ops[2 * k + d], ops[2 * (k + 1) + d]) if d == 0 else (ops[2 * (k + 1) + d], ops[2 * k + d])
                pairs.append((d, (gi * RWKV_GROUP + k) // 2, first, second))
        prods = [_bdot(jnp.concatenate([eye * a[4] + a[2], a[3]], axis=0), b[2]) for _, _, a, b in pairs]
        for (d, pair, a, b), prod in zip(pairs, prods):
            pp_scr[d, pair] = jnp.concatenate([a[2], prod[:w] + a[2] * b[4]], axis=1).astype(BF16)
            qq_scr[d, pair] = a[3] * b[4] + prod[w:] + b[3]
        return carry

    lax.fori_loop(0, n_chunks // RWKV_GROUP, operators, 0)

    def scan(trip, carry):
        outs = []
        for k in range(RWKV_SCAN_UNROLL):
            step = trip * RWKV_SCAN_UNROLL + k
            first = _scan_chunks(2 * step, n_chunks, n_ctx_chunks)
            second = _scan_chunks(2 * step + 1, n_chunks, n_ctx_chunks)
            pair = [jnp.minimum(a, b) // 2 for a, b in zip(first, second)]
            moved = [_bdot(ht, pp_scr[d, p]) for d, (p, ht) in enumerate(zip(pair, carry))]
            mids = [ht * g_scr[d, a] + mv[:, :w] + qt_scr[d, a] for d, (a, ht, mv) in enumerate(zip(first, carry, moved))]
            for d, (a, b, ht, mid) in enumerate(zip(first, second, carry, mids)):
                sl_a = pl.ds(pl.multiple_of(a * t, t), t)
                sl_b = pl.ds(pl.multiple_of(b * t, t), t)
                outs.append((d, sl_a, _bdot_nt(ry_scr[d, sl_a, :], ht)))
                outs.append((d, sl_b, _bdot_nt(ry_scr[d, sl_b, :], mid)))
            carry = tuple(ht * (g_scr[d, a] * g_scr[d, b]) + mv[:, w:] + qq_scr[d, p]
                          for d, (a, b, p, ht, mv) in enumerate(zip(first, second, pair, carry, moved)))
        for d, sl, y in outs:
            y_scr[d, sl, :] = y + y_scr[d, sl, :]
        return carry

    zero_state = jnp.zeros((w, w), F32)
    lax.fori_loop(0, n_chunks // (2 * RWKV_SCAN_UNROLL), scan, (zero_state, zero_state))

    f_rows = RWKV_FINISH_ROWS

    def finish(j, carry):
        sl = pl.ds(pl.multiple_of(j * f_rows, f_rows), f_rows)
        y = y_scr[0, sl, :] + y_scr[1, sl, :] + bonus_scr[sl, :]
        mu = _dot_sel(y, head_sum) * inv_head
        yc = y - mu
        var = _dot_sel(yc * yc, head_sum) * inv_head
        yn = yc * lax.rsqrt(var + RWKV_LN_EPSILON) * lnw_ref[...] + lnb_ref[...]
        out_ref[0, sl, :] = (yn * _silu(rz_ref[0, sl, :].astype(F32))).astype(out_ref.dtype)
        return carry

    lax.fori_loop(0, s // f_rows, finish, 0)


def _rwkv(u, col0, p, n_ctx):
    bsz, s, _ = u.shape
    w = LANES
    width = p["mu"].shape[1]
    n_pairs = width // w
    base = col0 // w
    col = lambda k: pl.BlockSpec((1, s, w), lambda b, h, k=k: (b, 0, base + k * n_pairs + h))
    lora = lambda k: pl.BlockSpec((1, s, w), lambda b, h, k=k: (b, 0, base + 4 * n_pairs + k))
    vec = lambda rows: pl.BlockSpec((rows, w), lambda b, h: (0, h))
    cat = pl.BlockSpec((1, 1, 2 * w), lambda b, h: (h, 0, 0))
    up = pl.BlockSpec((1, w, 2 * w), lambda b, h: (h, 0, 0))
    seq = pltpu.VMEM((s, w), F32)
    seq2 = pltpu.VMEM((2, s, w), F32)
    n_chunks = s // RWKV_CHUNK
    assert n_chunks % RWKV_GROUP == 0 and s % (RWKV_PREP_ROWS * RWKV_PREP_BLOCKS) == 0 and n_ctx % RWKV_PREP_ROWS == 0
    assert s % RWKV_FINISH_ROWS == 0 and n_chunks % (2 * RWKV_SCAN_UNROLL) == 0
    assert RWKV_GROUP % 2 == 0 and (n_ctx // RWKV_CHUNK) % 2 == 0
    operators = [pltpu.VMEM((2, s, w), BF16), pltpu.VMEM((2, n_chunks // 2, w, 2 * w), BF16),
                 pltpu.VMEM((2, n_chunks // 2, w, w), F32),
                 pltpu.VMEM((2, n_chunks, w, w), F32), pltpu.VMEM((2, n_chunks, 1, w), F32)]
    return pl.pallas_call(
        functools.partial(_rwkv_kernel, n_ctx=n_ctx),
        grid=(bsz, n_pairs),
        in_specs=[col(0), col(1), col(2), col(3), lora(0), lora(1),
                  vec(3), vec(1), vec(1), vec(1), vec(1), vec(1), cat, cat, up, up],
        out_specs=pl.BlockSpec((1, s, w), lambda b, h: (b, 0, h)),
        out_shape=jax.ShapeDtypeStruct((bsz, s, width), BF16),
        scratch_shapes=[seq, seq, seq, seq2, seq2, seq2, seq, seq2] + operators,
        compiler_params=_compiler_params(("parallel", "arbitrary")),
        name="rwkv7_mixer",
    )(u, u, u, u, u, u, p["mu"], p["k_k"], p["k_a"], p["r_k"], p["ln_w"], p["ln_b"],
      p["w0"], p["a0"], p["w_up"], p["a_up"])


def _hgrn_level_masks(t, w):
    rid = _iota((t, w), 0)
    r_i = _iota((t, t), 0)
    c_i = _iota((t, t), 1)
    levels = []
    c = 1
    while c < t:
        same_block = (r_i // (2 * c)) == (c_i // (2 * c))
        up_r = (r_i % (2 * c)) >= c
        up_c = (c_i % (2 * c)) >= c
        pair = [jnp.logical_and(same_block, jnp.logical_and(up_r, jnp.logical_not(up_c))),
                jnp.logical_and(same_block, jnp.logical_and(up_c, jnp.logical_not(up_r)))]
        levels.append((c, (rid % (2 * c)) >= c, pair))
        c *= 2
    return levels


def _hgrn_chunk_operators(problems, lb, tris, levels):
    t, w = problems[0][0].shape
    r_i = _iota((t, t), 0)
    c_i = _iota((t, t), 1)
    zero_row = jnp.zeros((1, w), F32)
    dirs = [p[3] for p in problems]
    lgs, ks = [], []
    for q, v, ff, d in problems:
        e = jnp.exp(-jnp.abs(ff))
        big = 1.0 / (1.0 + e)
        small = e / (1.0 + e)
        pos = ff >= 0.0
        lgs.append(jnp.log(lb + (1.0 - lb) * jnp.where(pos, big, small)) * LOG2_E)
        ks.append((1.0 - lb) * jnp.where(pos, small, big))
    bs = [_sel_dot(tris[d], lg) for lg, d in zip(lgs, dirs)]
    befores = [_shifted(b, zero_row, zero_row)[1 if d == 1 else 0] for b, d in zip(bs, dirs)]
    edges = list(bs)
    accs = [jnp.where(r_i == c_i, jnp.sum(p[0] * k, axis=-1, keepdims=True), 0.0) for p, k in zip(problems, ks)]
    for c, upper, pair in levels:
        qts = [p[0] * jnp.exp2(b - before) for p, b, before in zip(problems, bs, befores)]
        kts = [k * jnp.exp2(edge - b) for k, b, edge in zip(ks, bs, edges)]
        prods = [_bdot_nt(qt, kt) for qt, kt in zip(qts, kts)]
        accs = [jnp.where(pair[d], pr, a) for a, pr, d in zip(accs, prods, dirs)]
        for i, d in enumerate(dirs):
            if d == 1:
                befores[i] = jnp.where(upper, befores[i], pltpu.roll(befores[i], t - c, 0))
                edges[i] = jnp.where(upper, pltpu.roll(edges[i], c, 0), edges[i])
            else:
                befores[i] = jnp.where(upper, pltpu.roll(befores[i], c, 0), befores[i])
                edges[i] = jnp.where(upper, edges[i], pltpu.roll(edges[i], t - c, 0))
    o_intras = [_bdot(a, p[1]) for a, p in zip(accs, problems)]
    b_ends = [b[(0 if d == 1 else t - 1):(1 if d == 1 else t)] for b, d in zip(bs, dirs)]
    kvs = [_bdot_tn(p[1], k * jnp.exp2(be - b)) for p, k, b, be in zip(problems, ks, bs, b_ends)]
    return [(p[0] * jnp.exp2(b), oi, kv, jnp.exp2(be)) for p, b, oi, kv, be in zip(problems, bs, o_intras, kvs, b_ends)]


def _hgrn_kernel(q_ref, i_ref, ff_ref, fb_ref, z_ref, lb_ref, ng_ref, out_ref, o_scr, qe_scr, kv_scr, g_scr,
                 *, n_ctx, layer):
    s = q_ref.shape[1]
    dh = q_ref.shape[2]
    t = MIX_CHUNK
    n_chunks = s // t
    n_ctx_chunks = n_ctx // t
    lbs = lb_ref[...]
    ex = jnp.exp(lbs - jnp.max(lbs, axis=0, keepdims=True))
    probs = ex / jnp.sum(ex, axis=0, keepdims=True)
    csum = probs[0:1]
    for l in range(1, layer + 1):
        csum = csum + probs[l:l + 1]
    lb = csum - probs[0:1]
    tri_r = _iota((t, t), 0)
    tri_c = _iota((t, t), 1)
    tris = [(tri_c <= tri_r).astype(BF16), (tri_c >= tri_r).astype(BF16)]
    levels = _hgrn_level_masks(t, dh)

    def operators(gi, carry):
        problems, where = [], []
        for kk in range(HGRN_GROUP):
            chunk = gi * HGRN_GROUP + kk
            sl = pl.ds(pl.multiple_of(chunk * t, t), t)
            q, v = q_ref[0, sl, :].astype(F32), i_ref[0, sl, :].astype(F32)
            for d, f_ref in enumerate((ff_ref, fb_ref)):
                problems.append((q, v, f_ref[0, sl, :].astype(F32), d))
                where.append((d, chunk, sl))
        for (d, chunk, sl), (qe, o_intra, kv, g) in zip(where, _hgrn_chunk_operators(problems, lb, tris, levels)):
            qe_scr[d, sl, :] = qe.astype(BF16)
            o_scr[d, sl, :] = o_intra
            kv_scr[d, chunk] = kv
            g_scr[d, chunk] = g
        return carry

    lax.fori_loop(0, n_chunks // HGRN_GROUP, operators, 0)

    def scan(trip, carry):
        steps = []
        for k in range(SCAN_UNROLL):
            chunks = _scan_chunks(trip * SCAN_UNROLL + k, n_chunks, n_ctx_chunks)
            steps += [(d, pl.ds(pl.multiple_of(chunk * t, t), t), st) for d, (chunk, st) in enumerate(zip(chunks, carry))]
            carry = tuple(st * g_scr[d, chunk] + kv_scr[d, chunk] for d, (chunk, st) in enumerate(zip(chunks, carry)))
        inters = [_bdot_nt(qe_scr[d, sl, :], st) for d, sl, st in steps]
        for (d, sl, _), inter in zip(steps, inters):
            o_scr[d, sl, :] = o_scr[d, sl, :] + inter
        return carry

    zero_state = jnp.zeros((dh, dh), F32)
    lax.fori_loop(0, n_chunks // SCAN_UNROLL, scan, (zero_state, zero_state))

    f_rows = MIX_FINISH_ROWS

    def finish(j, carry):
        sl = pl.ds(pl.multiple_of(j * f_rows, f_rows), f_rows)
        o = o_scr[0, sl, :] + o_scr[1, sl, :]
        y = o * lax.rsqrt(jnp.mean(o * o, axis=-1, keepdims=True) + RMS_EPS) * ng_ref[...]
        out_ref[0, sl, :] = (y * _silu(z_ref[0, sl, :].astype(F32))).astype(out_ref.dtype)
        return carry

    lax.fori_loop(0, (s - n_ctx) // f_rows, finish, 0)


def _hgrn(u, lb_all, norm_g, n_ctx, layer):
    bsz, s, _ = u.shape
    nh = HGRN_N_HEADS
    dh = LANES
    width = nh * dh
    depth = lb_all.shape[0]
    n_chunks = s // MIX_CHUNK
    assert n_chunks % HGRN_GROUP == 0 and n_chunks % SCAN_UNROLL == 0 and n_ctx % MIX_CHUNK == 0
    col = lambda k: pl.BlockSpec((1, s, dh), lambda b, h, k=k: (b, 0, k * nh + h))
    return pl.pallas_call(
        functools.partial(_hgrn_kernel, n_ctx=n_ctx, layer=layer),
        grid=(bsz, nh),
        in_specs=[col(0), col(1), col(2), col(3), col(4),
                  pl.BlockSpec((depth, dh), lambda b, h: (0, h)),
                  pl.BlockSpec((1, dh), lambda b, h: (0, h))],
        out_specs=pl.BlockSpec((1, s - n_ctx, dh), lambda b, h: (b, 0, h)),
        out_shape=jax.ShapeDtypeStruct((bsz, s - n_ctx, width), BF16),
        scratch_shapes=[pltpu.VMEM((2, s, dh), F32), pltpu.VMEM((2, s, dh), BF16),
                        pltpu.VMEM((2, n_chunks, dh, dh), F32), pltpu.VMEM((2, n_chunks, 1, dh), F32)],
        compiler_params=_compiler_params(("parallel", "arbitrary")),
        name="hgrn2_mixer",
    )(u, u, u, u, u, lb_all, norm_g.reshape(1, width))


def _hyena_filter_kernel(z_ref, w1_ref, b1_ref, w2_ref, b2_ref, w3f_ref, w3b_ref, dl_ref, hf_ref, hb_ref):
    hp = functools.partial(jnp.dot, precision=lax.Precision.HIGHEST, preferred_element_type=F32)
    n = z_ref.shape[0]
    hid = jnp.sin(hp(z_ref[...], w1_ref[...]) + b1_ref[...])
    hid = jnp.sin(hp(hid, w2_ref[...]) + b2_ref[...])
    pos = _iota((n, 1), 0).astype(F32) * (1.0 / n)
    window = jnp.exp(-pos * dl_ref[...]) + HYENA_SHIFT
    f0 = hp(hid, w3f_ref[...]) * window
    f1 = hp(hid, w3b_ref[...]) * window
    nrm = jnp.sum(jnp.abs(f0), axis=0, keepdims=True) + jnp.sum(jnp.abs(f1), axis=0, keepdims=True)
    hf_ref[...] = f0 / nrm
    hb_ref[...] = f1 / nrm


def _hyena_filters(n, w1, b1, w2, b2, w3, width):
    pos = np.arange(n, dtype=np.float64)
    bands = np.linspace(1e-4, HYENA_N_BANDS - 1, HYENA_N_BANDS)
    ang = (2.0 * math.pi / n) * pos[:, None] * bands
    z = np.concatenate([(pos / n)[:, None], np.cos(ang), np.sin(ang)], axis=-1)
    z = np.pad(z, ((0, 0), (0, LANES - z.shape[1]))).astype(np.float32)
    max_decay = math.log(HYENA_TGT) / HYENA_FAST
    min_decay = math.log(HYENA_TGT) / HYENA_SLOW
    deltas = np.abs(np.linspace(min_decay, max_decay, width)).astype(np.float32)[None]
    feat, hid = w1.shape
    w1p = jnp.pad(w1, ((0, LANES - feat), (0, LANES - hid)))
    w2p = jnp.pad(w2, ((0, LANES - hid), (0, LANES - hid)))
    w3p = jnp.pad(w3, ((0, LANES - hid), (0, 0)))
    b1p = jnp.pad(b1, (0, LANES - hid)).reshape(1, LANES)
    b2p = jnp.pad(b2, (0, LANES - hid)).reshape(1, LANES)
    n_tiles = width // LANES
    full = lambda shape: pl.BlockSpec(shape, lambda j: (0, 0))
    out = pl.BlockSpec((n, LANES), lambda j: (0, j))
    hf, hb = pl.pallas_call(
        _hyena_filter_kernel,
        grid=(n_tiles,),
        in_specs=[full((n, LANES)), full((LANES, LANES)), full((1, LANES)), full((LANES, LANES)), full((1, LANES)),
                  pl.BlockSpec((LANES, LANES), lambda j: (0, j)),
                  pl.BlockSpec((LANES, LANES), lambda j: (0, n_tiles + j)),
                  pl.BlockSpec((1, LANES), lambda j: (0, j))],
        out_specs=[out, out],
        out_shape=[jax.ShapeDtypeStruct((n, width), F32)] * 2,
        compiler_params=_compiler_params(("arbitrary",)),
        name="hyena_filters",
    )(jnp.asarray(z), w1p, b1p, w2p, b2p, w3p, w3p, jnp.asarray(deltas))
    return jnp.concatenate([hf, hb], axis=1)


def _hyena_pre_kernel(yv_ref, y0_ref, y1_ref, yz_ref, swv_ref, sw0_ref, sw1_ref, sbv_ref, sb0_ref, sb1_ref,
                      p_ref, e_ref, *, n_ctx):
    s = yv_ref.shape[1]
    rows = MIX_CHUNK

    def body(j, carry):
        t0 = pl.multiple_of(j * rows, rows)
        conv = []
        for src, sw, sb in ((yv_ref, swv_ref, sbv_ref), (y0_ref, sw0_ref, sb0_ref), (y1_ref, sw1_ref, sb1_ref)):
            cur, down, up = _chunk_with_neighbors(src, t0, rows, s, s - n_ctx)
            conv.append(down * sw[0:1, :] + cur * sw[1:2, :] + up * sw[2:3, :] + sb[...])
        v, x0, x1 = conv
        p = x1 * v
        o0 = pl.multiple_of(j * rows, rows)
        p_ref[0, pl.ds(o0, rows), :] = p.astype(BF16)
        e_ref[0, pl.ds(o0, rows), :] = (x0 * _silu(yz_ref[0, pl.ds(t0, rows), :].astype(F32))).astype(BF16)
        return carry

    lax.fori_loop(0, (s - n_ctx) // rows, body, 0)


def _hyena_pre(u, col0, short_w, short_b, n_ctx):
    bsz, s, _ = u.shape
    w = short_b.shape[0] // 3
    tiles = w // LANES
    base = col0 // LANES
    n = s - n_ctx
    col = lambda k: pl.BlockSpec((1, s, LANES), lambda b, j, k=k: (b, 0, base + k * tiles + j))
    par = lambda rows, k: pl.BlockSpec((rows, LANES), lambda b, j, k=k: (0, k * tiles + j))
    out = pl.BlockSpec((1, n, LANES), lambda b, j: (b, 0, j))
    sb = short_b.reshape(1, 3 * w)
    return pl.pallas_call(
        functools.partial(_hyena_pre_kernel, n_ctx=n_ctx),
        grid=(bsz, tiles),
        in_specs=[col(0), col(1), col(2), col(3),
                  par(3, 0), par(3, 1), par(3, 2), par(1, 0), par(1, 1), par(1, 2)],
        out_specs=[out, out],
        out_shape=[jax.ShapeDtypeStruct((bsz, n, w), BF16)] * 2,
        compiler_params=_compiler_params(("parallel", "arbitrary")),
        name="hyena_short_conv",
    )(u, u, u, u, short_w, short_w, short_w, sb, sb, sb)


def _dft_tables(n):
    big = 2 * n
    half = DFT_TILE // 2
    idx = jnp.arange(n, dtype=jnp.int32)
    split = DFT_SPLIT
    lo = jnp.arange(split, dtype=jnp.int32)
    hi = jnp.arange(n // split, dtype=jnp.int32)
    ang_lo = ((lo[:, None] * idx[None, :]) % big).astype(F32) * (2.0 * math.pi / big)
    ang_hi = ((hi[:, None] * idx[None, :]) % (big // split)).astype(F32) * (2.0 * math.pi * split / big)
    c_lo, s_lo = jnp.cos(ang_lo)[None], jnp.sin(ang_lo)[None]
    c_hi, s_hi = jnp.cos(ang_hi)[:, None], jnp.sin(ang_hi)[:, None]
    cos = (c_hi * c_lo - s_hi * s_lo).reshape(n, n)
    sin = (s_hi * c_lo + c_hi * s_lo).reshape(n, n)
    alt = jnp.where(idx % 2 == 0, 1.0, -1.0).astype(F32)
    first_row = (idx == 0)[:, None]
    im = jnp.where(first_row, alt[None, :], -sin)
    fwd = jnp.stack([cos.reshape(n // half, half, n), im.reshape(n // half, half, n)], axis=1).reshape(big, n)
    weight = jnp.where(first_row, 1.0, 2.0) * (1.0 / big)
    weight = jnp.stack([weight.reshape(n // half, half, 1)] * 2, axis=1).reshape(big, 1)
    return fwd.astype(BF16), (fwd * weight).T.astype(BF16)


def _spectrum_kernel(f_ref, lo_ref, hi_ref, bias_ref, o_ref):
    half = DFT_TILE // 2
    w = lo_ref.shape[1] // 2
    lo = lo_ref[...].astype(BF16)
    acc_lo = jnp.dot(f_ref[...], lo, preferred_element_type=F32)
    acc_hi = jnp.dot(f_ref[...], hi_ref[...].astype(BF16), preferred_element_type=F32)
    rows = _iota((DFT_TILE, w), 0)
    sign = jnp.where(rows % 2 == 1, -1.0, 1.0)
    real_slot = jnp.logical_or(rows < half, jnp.logical_and(rows == half, pl.program_id(0) == 0))
    conj = lambda x: jnp.where(real_slot, x, -x)
    lag0 = lo[0:1, :].astype(F32)
    a, c_all = acc_lo[:, :w], acc_lo[:, w:]
    b, d = acc_hi[:, :w], acc_hi[:, w:]
    c = c_all - jnp.where(real_slot, lag0[:, w:], 0.0)
    o_ref[:, 0:w] = a + conj(c_all) + jnp.where(real_slot, bias_ref[...], 0.0)
    o_ref[:, w:2 * w] = b + sign * (a - jnp.where(real_slot, lag0[:, :w], 0.0))
    o_ref[:, 2 * w:] = sign * conj(c) + conj(d)


def _filter_spectrum(fwd, hk, bias):
    big, h = fwd.shape
    w = hk.shape[1] // 2
    assert hk.shape[0] == 2 * h
    return pl.pallas_call(
        _spectrum_kernel,
        grid=(big // DFT_TILE,),
        in_specs=[pl.BlockSpec((DFT_TILE, h), lambda i: (i, 0)),
                  pl.BlockSpec((h, 2 * w), lambda i: (0, 0)),
                  pl.BlockSpec((h, 2 * w), lambda i: (1, 0)),
                  pl.BlockSpec((1, w), lambda i: (0, 0))],
        out_specs=pl.BlockSpec((DFT_TILE, 3 * w), lambda i: (i, 0)),
        out_shape=jax.ShapeDtypeStruct((big, 3 * w), F32),
        compiler_params=_compiler_params(("arbitrary",)),
        name="hyena_filter_spectrum",
    )(fwd, hk, hk, bias.reshape(1, w))


def _spectral_products(pairs, packed):
    re = sum(s[0] * k[0] - s[1] * k[1] for s, k in pairs)
    im = sum(s[0] * k[1] + s[1] * k[0] for s, k in pairs)
    if packed is not None:
        re = jnp.where(packed, sum(s[0] * k[0] for s, k in pairs), re)
        im = jnp.where(packed, sum(s[1] * k[1] for s, k in pairs), im)
    return jnp.concatenate([re, im], axis=0).astype(BF16)


def _conv_spectrum_kernel(f_ref, pa_ref, pb_ref, ks_ref, z_ref):
    i = pl.program_id(1)
    half = DFT_TILE // 2
    w = pa_ref.shape[2]
    units = [slice(k * DFT_TILE, (k + 1) * DFT_TILE) for k in range(DFT_UNITS)]
    accs = [(jnp.dot(f_ref[u, :], pa_ref[0], preferred_element_type=F32),
             jnp.dot(f_ref[u, :], pb_ref[0], preferred_element_type=F32)) for u in units]
    for k, (u, (acc_a, acc_b)) in enumerate(zip(units, accs)):
        s_a, s_b = (acc_a[:half], acc_a[half:]), (acc_b[:half], acc_b[half:])
        ks = ks_ref[u, :]
        k_diag, k_below, k_above = [(ks[:half, j * w:(j + 1) * w], ks[half:, j * w:(j + 1) * w]) for j in range(3)]
        packed = jnp.logical_and(_iota((half, w), 0) == 0, i == 0) if k == 0 else None
        z_ref[0, 0, u, :] = _spectral_products([(s_a, k_diag), (s_b, k_above)], packed)
        z_ref[0, 1, u, :] = _spectral_products([(s_a, k_below), (s_b, k_diag)], packed)


def _conv_spectrum(fwd, p16, kspec):
    bsz, n, w = p16.shape
    big, h = fwd.shape
    assert n == 2 * h
    rows = DFT_TILE * DFT_UNITS
    return pl.pallas_call(
        _conv_spectrum_kernel,
        grid=(bsz, big // rows),
        in_specs=[pl.BlockSpec((rows, h), lambda b, i: (i, 0)),
                  pl.BlockSpec((1, h, w), lambda b, i: (b, 0, 0)),
                  pl.BlockSpec((1, h, w), lambda b, i: (b, 1, 0)),
                  pl.BlockSpec((rows, 3 * w), lambda b, i: (i, 0))],
        out_specs=pl.BlockSpec((1, 2, rows, w), lambda b, i: (b, 0, i, 0)),
        out_shape=jax.ShapeDtypeStruct((bsz, 2, big, w), BF16),
        compiler_params=_compiler_params(("parallel", "arbitrary")),
        name="hyena_forward_dft",
    )(fwd, p16, p16, kspec)


def _conv_inverse_kernel(g_ref, z_ref, e_ref, o_ref):
    y = jnp.dot(g_ref[...], z_ref[0, 0], preferred_element_type=F32)
    o_ref[0] = (e_ref[0].astype(F32) * y).astype(o_ref.dtype)


def _conv_inverse(inv, z16, e):
    bsz, _, big, w = z16.shape
    h = inv.shape[0]
    tile = DFT_TILE
    per_half = h // tile
    tok = pl.BlockSpec((1, tile, w), lambda b, j, i: (b, j * per_half + i, 0))
    return pl.pallas_call(
        _conv_inverse_kernel,
        grid=(bsz, 2, per_half),
        in_specs=[pl.BlockSpec((tile, big), lambda b, j, i: (i, 0)),
                  pl.BlockSpec((1, 1, big, w), lambda b, j, i: (b, j, 0, 0)),
                  tok],
        out_specs=tok,
        out_shape=jax.ShapeDtypeStruct((bsz, 2 * h, w), BF16),
        compiler_params=_compiler_params(("parallel", "arbitrary", "arbitrary")),
        name="hyena_inverse_dft",
    )(inv, z16, e)


def _even_weight_layout(w_in, gate_b):
    d = w_in.shape[0]
    mw = MLSTM_N_HEADS * LANES
    g0 = 5 * mw
    g1 = g0 + 4 * MLSTM_N_HEADS
    main = jnp.concatenate([w_in[:, :g0], w_in[:, g1:]], axis=1).astype(BF16)
    wg = w_in[:, g0:g1].reshape(d, 2, 2, MLSTM_N_HEADS)
    wg = jnp.transpose(wg, (3, 1, 2, 0)).reshape(MLSTM_N_HEADS, 4, d)
    wg = jnp.concatenate([wg, jnp.zeros_like(wg)], axis=1).reshape(MLSTM_N_HEADS * 8, d).astype(BF16)
    gb = jnp.transpose(gate_b.reshape(2, 2, MLSTM_N_HEADS), (2, 0, 1)).reshape(MLSTM_N_HEADS, 4)
    gb = jnp.concatenate([gb, jnp.zeros_like(gb)], axis=1).reshape(MLSTM_N_HEADS * 8, 1)
    return main, wg, jnp.broadcast_to(gb, (MLSTM_N_HEADS * 8, LANES))


def _rwkv_params(mu, w0, w_up, a0, a_up, k_k, k_a, r_k, ln_w, ln_b):
    width = mu.shape[1]
    n_pairs = width // LANES
    row = lambda x: x.reshape(1, width)

    def cat_dirs(x):
        return jnp.transpose(x.reshape(2, n_pairs, LANES), (1, 0, 2)).reshape(n_pairs, 1, 2 * LANES)

    def block_up(x):
        lora = x.shape[1]
        xp = jnp.transpose(x.reshape(2, lora, n_pairs, LANES), (2, 0, 1, 3))
        z = jnp.zeros_like(xp[:, 0])
        top = jnp.concatenate([xp[:, 0], z], axis=2)
        bot = jnp.concatenate([z, xp[:, 1]], axis=2)
        return jnp.concatenate([top, bot], axis=1).astype(BF16)

    return {"mu": mu, "k_k": row(k_k), "k_a": row(k_a), "r_k": row(r_k), "ln_w": row(ln_w), "ln_b": row(ln_b),
            "w0": cat_dirs(w0), "a0": cat_dirs(a0), "w_up": block_up(w_up), "a_up": block_up(a_up)}


def _raster_to_column(h):
    b, n, d = h.shape
    rows = n // GRID_WIDTH
    return h.reshape(b, rows, GRID_WIDTH, d).transpose(0, 2, 1, 3).reshape(b, n, d)


def _column_to_raster(h):
    b, n, d = h.shape
    rows = n // GRID_WIDTH
    return h.reshape(b, GRID_WIDTH, rows, d).transpose(0, 2, 1, 3).reshape(b, n, d)


def kernel(x, c, ctx, c_ctx, l0_norm_g, l0_mod_w, l0_mod_b, l0_w_in, l0_w_out, l0_mlstm_conv_w, l0_mlstm_gate_b, l0_mlstm_norm_g, l0_rwkv_mu, l0_rwkv_w0, l0_rwkv_w_up, l0_rwkv_a0, l0_rwkv_a_up, l0_rwkv_k_k, l0_rwkv_k_a, l0_rwkv_r_k, l0_rwkv_ln_w, l0_rwkv_ln_b, hgrn_lower_bounds, l1_norm_g, l1_mod_w, l1_mod_b, l1_w_in, l1_w_out, l1_hgrn_norm_g, l1_hyena_short_w, l1_hyena_short_b, l1_hyena_w1, l1_hyena_b1, l1_hyena_w2, l1_hyena_b2, l1_hyena_w3, l1_hyena_bias, final_norm_g):
    bsz, n_lat, d = x.shape
    n_ctx = ctx.shape[1]

    pad = (-(bsz + 1)) % 8
    cc = jnp.concatenate([c, c_ctx[None], jnp.zeros((pad, d), F32)], axis=0)
    mod0, mod1 = _modulation(cc, l0_mod_w, l0_mod_b, l1_mod_w, l1_mod_b)
    mod0 = mod0[:bsz + 1].reshape(bsz + 1, 3, d)
    mod1 = mod1[:bsz + 1].reshape(bsz + 1, 3, d)

    w_main, w_gate, b_gate = _even_weight_layout(l0_w_in, l0_mlstm_gate_b)
    n0 = w_main.shape[1]
    u0, gt0 = _proj_in(x, ctx, l0_norm_g, mod0, w_main, n0 // 2, w_gate, b_gate)
    gt0 = gt0.reshape(bsz, MLSTM_N_HEADS, 8, n_ctx + n_lat)
    y_m = _mlstm(u0, gt0, l0_mlstm_conv_w, l0_mlstm_norm_g, n_ctx)
    rp = _rwkv_params(l0_rwkv_mu, l0_rwkv_w0, l0_rwkv_w_up, l0_rwkv_a0, l0_rwkv_a_up, l0_rwkv_k_k,
                      l0_rwkv_k_a, l0_rwkv_r_k, l0_rwkv_ln_w, l0_rwkv_ln_b)
    y_r = _rwkv(u0, 5 * MLSTM_N_HEADS * LANES, rp, n_ctx)
    x1, ctx1 = _proj_out(y_m, y_r, x, ctx, mod0, l0_w_out.astype(BF16))

    x1c = _raster_to_column(x1)
    w1 = l1_w_in.astype(BF16)
    (u1,) = _proj_in(x1c, ctx1, l1_norm_g, mod1, w1, w1.shape[1] // 2)
    y_g = _hgrn(u1, hgrn_lower_bounds, l1_hgrn_norm_g, n_ctx, layer=1)
    hw = l1_hyena_bias.shape[0]
    hk = _hyena_filters(n_lat, l1_hyena_w1, l1_hyena_b1, l1_hyena_w2, l1_hyena_b2, l1_hyena_w3, hw)
    fwd, inv = _dft_tables(n_lat // 2)
    kspec = _filter_spectrum(fwd, hk, l1_hyena_bias)
    p16, e16 = _hyena_pre(u1, 5 * HGRN_N_HEADS * LANES, l1_hyena_short_w, l1_hyena_short_b, n_ctx)
    z16 = _conv_spectrum(fwd, p16, kspec)
    y_y = _conv_inverse(inv, z16, e16)
    out_c = _proj_out_final(y_g, y_y, x1c, mod1, l1_w_out.astype(BF16), final_norm_g)
    return _column_to_raster(out_c)
```

```python
import functools
import math

import jax
import jax.numpy as jnp
import numpy as np
from jax import lax
from jax.experimental import pallas as pl
from jax.experimental.pallas import tpu as pltpu

F32 = jnp.float32
BF16 = jnp.bfloat16

GRID_WIDTH = 64
RMS_EPS = 1e-6
MLSTM_N_HEADS = 4
RWKV_HEAD = 64
RWKV_LN_EPSILON = 64e-5
HGRN_N_HEADS = 4
HYENA_N_BANDS = 16
HYENA_FAST = 0.3
HYENA_SLOW = 1.5
HYENA_TGT = 1e-2
HYENA_SHIFT = 0.05
LOG2_E = 1.0 / math.log(2.0)

LANES = 128
SUBLANES = 8
ROW_GROUP = 16
VMEM_LIMIT = 52 * 1024 * 1024

MIX_CHUNK = 128
RWKV_CHUNK = 64
RWKV_GROUP = 12
MIX_PREP_ROWS = 256
MIX_FINISH_ROWS = 256
RWKV_PREP_ROWS = 256
RWKV_PREP_BLOCKS = 3
RWKV_FINISH_ROWS = 768
RWKV_SCAN_UNROLL = 3
SCAN_UNROLL = 3
MLSTM_GROUP = 9
HGRN_GROUP = 6
PROJ_ROWS = 768
FINAL_ROWS = 512
DFT_TILE = 512
DFT_UNITS = 2
DFT_SPLIT = 64


def _bdot(a, b):
    return jnp.dot(a.astype(BF16), b.astype(BF16), preferred_element_type=F32)


def _bdot_nt(a, b):
    return lax.dot_general(a.astype(BF16), b.astype(BF16), (((1,), (1,)), ((), ())),
                           preferred_element_type=F32)


def _bdot_tn(a, b):
    return lax.dot_general(a.astype(BF16), b.astype(BF16), (((0,), (0,)), ((), ())),
                           preferred_element_type=F32)


def _split3(x):
    hi = x.astype(BF16)
    r1 = x - hi.astype(F32)
    mid = r1.astype(BF16)
    lo = (r1 - mid.astype(F32)).astype(BF16)
    return hi, mid, lo


def _sel_dot(sel, x):
    hi, mid, lo = _split3(x)
    d = functools.partial(jnp.dot, preferred_element_type=F32)
    return d(sel, hi) + d(sel, mid) + d(sel, lo)


def _dot_sel(x, sel):
    hi = x.astype(BF16)
    mid = (x - hi.astype(F32)).astype(BF16)
    d = functools.partial(jnp.dot, preferred_element_type=F32)
    return d(hi, sel) + d(mid, sel)


def _sigmoid(x):
    return 1.0 / (1.0 + jnp.exp(-x))


def _silu(x):
    return x * _sigmoid(x)


def _iota(shape, dim):
    return lax.broadcasted_iota(jnp.int32, shape, dim)


def _neighbor_rows(ref, t0, rows, n_total, split):
    has_prev = jnp.logical_and(t0 != 0, t0 != split)
    has_next = jnp.logical_and(t0 + rows != split, t0 + rows != n_total)
    g = ROW_GROUP
    before = ref[0, pl.ds(pl.multiple_of(jnp.maximum(t0 - g, 0), g), g), :].astype(F32)
    after = ref[0, pl.ds(pl.multiple_of(jnp.minimum(t0 + rows, n_total - g), g), g), :].astype(F32)
    return jnp.where(has_prev, before[g - 1:g], 0.0), jnp.where(has_next, after[0:1], 0.0)


def _shifted(cur, prev_row, next_row):
    rows = cur.shape[0]
    rid = _iota(cur.shape, 0)
    down = jnp.where(rid == 0, prev_row, pltpu.roll(cur, 1, 0))
    up = jnp.where(rid == rows - 1, next_row, pltpu.roll(cur, rows - 1, 0))
    return down, up


def _chunk_with_neighbors(ref, t0, rows, n_total, split):
    cur = ref[0, pl.ds(t0, rows), :].astype(F32)
    prev_row, next_row = _neighbor_rows(ref, t0, rows, n_total, split)
    down, up = _shifted(cur, prev_row, next_row)
    return cur, down, up


def _scan_chunks(i, n_chunks, n_ctx_chunks):
    fwd = jnp.where(i < n_ctx_chunks, n_chunks - n_ctx_chunks + i, i - n_ctx_chunks)
    return fwd, n_chunks - 1 - i


def _compiler_params(semantics):
    return pltpu.CompilerParams(dimension_semantics=semantics, vmem_limit_bytes=VMEM_LIMIT)


def _mod_kernel(c_ref, w0_ref, b0_ref, w1_ref, b1_ref, o0_ref, o1_ref):
    s = _silu(c_ref[...])
    o0_ref[...] = _bdot(s, w0_ref[...]) + b0_ref[...]
    o1_ref[...] = _bdot(s, w1_ref[...]) + b1_ref[...]


def _modulation(cc, w0, b0, w1, b1):
    rows, d = cc.shape
    n = w0.shape[1]
    tile = d
    grid = (n // tile,)
    wspec = pl.BlockSpec((d, tile), lambda j: (0, j))
    bspec = pl.BlockSpec((1, tile), lambda j: (0, j))
    ospec = pl.BlockSpec((rows, tile), lambda j: (0, j))
    return pl.pallas_call(
        _mod_kernel,
        grid=grid,
        in_specs=[pl.BlockSpec((rows, d), lambda j: (0, 0)), wspec, bspec, wspec, bspec],
        out_specs=[ospec, ospec],
        out_shape=[jax.ShapeDtypeStruct((rows, n), F32)] * 2,
        compiler_params=_compiler_params(("arbitrary",)),
        name="adaln_modulation",
    )(cc, w0, b0.reshape(1, n), w1, b1.reshape(1, n))


def _load_tokens(x_ref):
    if len(x_ref.shape) == 3:
        return x_ref[0]
    return jnp.concatenate([x_ref[0, :, j, :] for j in range(x_ref.shape[2])], axis=0)


def _token_spec(x, rows, index_map):
    if x.ndim == 3:
        return pl.BlockSpec((1, rows, x.shape[2]), lambda *g: (index_map(*g)[0], index_map(*g)[1], 0))
    assert rows % x.shape[1] == 0
    return pl.BlockSpec((1, x.shape[1], rows // x.shape[1], x.shape[3]),
                        lambda *g: (index_map(*g)[0], 0, index_map(*g)[1], 0))


def _token_tile(x_ref, c_ref, last_of_batch, rows):
    n_lat_tail = rows - c_ref.shape[1]
    is_ctx = jnp.logical_and(last_of_batch, _iota((rows, 1), 0) >= n_lat_tail)
    ctx_rows = jnp.concatenate([jnp.zeros((n_lat_tail, c_ref.shape[2]), F32), c_ref[0]], axis=0)
    return jnp.where(is_ctx, ctx_rows, _load_tokens(x_ref)), is_ctx


def _proj_in_kernel(*refs, rows, with_gates):
    if with_gates:
        x_ref, c_ref, g_ref, ml_ref, mc_ref, w_ref, wg_ref, gb_ref, u_ref, gt_ref, h_scr = refs
    else:
        x_ref, c_ref, g_ref, ml_ref, mc_ref, w_ref, u_ref, h_scr = refs
    i = pl.program_id(1)
    n = pl.program_id(2)

    @pl.when(n == 0)
    def _():
        x, is_ctx = _token_tile(x_ref, c_ref, i == pl.num_programs(1) - 1, rows)
        y = x * lax.rsqrt(jnp.mean(x * x, axis=-1, keepdims=True) + RMS_EPS) * g_ref[...]
        ml = ml_ref[0]
        mc = mc_ref[0]
        shift = jnp.where(is_ctx, mc[0:1], ml[0:1])
        scale = jnp.where(is_ctx, mc[1:2], ml[1:2])
        h = (y * (1.0 + scale) + shift).astype(BF16)
        h_scr[...] = h
        if with_gates:
            gt_ref[0] = _bdot_nt(wg_ref[...], h) + gb_ref[:, 0:1]

    u_ref[0] = jnp.dot(h_scr[...], w_ref[...], preferred_element_type=F32).astype(u_ref.dtype)


def _proj_in(x, ctx, norm_g, mod3, w16, n_tile, gate_w=None, gate_b=None):
    bsz, d = x.shape[0], x.shape[-1]
    n_lat = math.prod(x.shape[1:-1])
    n_ctx = ctx.shape[1]
    s = n_lat + n_ctx
    n = w16.shape[1]
    rows = PROJ_ROWS
    assert s % rows == 0 and (n_lat % rows) + n_ctx == rows
    grid = (bsz, s // rows, n // n_tile)
    ctx_row = mod3.shape[0] - 1
    with_gates = gate_w is not None
    in_specs = [
        _token_spec(x, rows, lambda b, i, j: (b, i)),
        pl.BlockSpec((1, n_ctx, d), lambda b, i, j: (b, 0, 0)),
        pl.BlockSpec((1, d), lambda b, i, j: (0, 0)),
        pl.BlockSpec((1, 3, d), lambda b, i, j: (b, 0, 0)),
        pl.BlockSpec((1, 3, d), lambda b, i, j: (ctx_row, 0, 0)),
        pl.BlockSpec((d, n_tile), lambda b, i, j: (0, j)),
    ]
    args = [x, ctx, norm_g.reshape(1, d), mod3, mod3, w16]
    out_specs = [pl.BlockSpec((1, rows, n_tile), lambda b, i, j: (b, i, j))]
    out_shape = [jax.ShapeDtypeStruct((bsz, s, n), BF16)]
    if with_gates:
        ng = gate_w.shape[0]
        in_specs += [pl.BlockSpec((ng, d), lambda b, i, j: (0, 0)),
                     pl.BlockSpec((ng, LANES), lambda b, i, j: (0, 0))]
        args += [gate_w, gate_b]
        out_specs.append(pl.BlockSpec((1, ng, rows), lambda b, i, j: (b, 0, i)))
        out_shape.append(jax.ShapeDtypeStruct((bsz, ng, s), F32))
    return pl.pallas_call(
        functools.partial(_proj_in_kernel, rows=rows, with_gates=with_gates),
        grid=grid,
        in_specs=in_specs,
        out_specs=out_specs,
        out_shape=out_shape,
        scratch_shapes=[pltpu.VMEM((rows, d), BF16)],
        compiler_params=_compiler_params(("parallel", "arbitrary", "arbitrary")),
        name="norm_mod_proj_in",
    )(*args)


def _proj_out_kernel(ya_ref, yb_ref, x_ref, c_ref, ml_ref, mc_ref, w_ref, ox_ref, oc_ref, *, rows):
    i = pl.program_id(1)
    half = ya_ref.shape[2]
    y = _bdot(ya_ref[0], w_ref[0:half, :]) + _bdot(yb_ref[0], w_ref[half:, :])
    x, is_ctx = _token_tile(x_ref, c_ref, i == pl.num_programs(1) - 1, rows)
    x = x + jnp.where(is_ctx, mc_ref[0][2:3], ml_ref[0][2:3]) * y
    ox_ref[0] = x

    @pl.when(i == pl.num_programs(1) - 1)
    def _():
        oc_ref[0] = x[rows - c_ref.shape[1]:]


def _proj_out(ya, yb, x, ctx, mod3, w16):
    bsz, n_lat, d = x.shape
    n_ctx = ctx.shape[1]
    s = n_lat + n_ctx
    half = ya.shape[2]
    rows = PROJ_ROWS
    assert s % rows == 0 and (n_lat % rows) + n_ctx == rows
    ctx_row = mod3.shape[0] - 1
    tok = lambda w: pl.BlockSpec((1, rows, w), lambda b, i: (b, i, 0))
    seg = pl.BlockSpec((1, n_ctx, d), lambda b, i: (b, 0, 0))
    return pl.pallas_call(
        functools.partial(_proj_out_kernel, rows=rows),
        grid=(bsz, s // rows),
        in_specs=[tok(half), tok(half), tok(d), seg,
                  pl.BlockSpec((1, 3, d), lambda b, i: (b, 0, 0)),
                  pl.BlockSpec((1, 3, d), lambda b, i: (ctx_row, 0, 0)),
                  pl.BlockSpec((2 * half, d), lambda b, i: (0, 0))],
        out_specs=[tok(d), seg],
        out_shape=[jax.ShapeDtypeStruct((bsz, n_lat, d), F32), jax.ShapeDtypeStruct((bsz, n_ctx, d), F32)],
        compiler_params=_compiler_params(("parallel", "arbitrary")),
        name="proj_out_residual",
    )(ya, yb, x, ctx, mod3, mod3, w16)


def _proj_out_final_kernel(ya_ref, yb_ref, x_ref, ml_ref, w_ref, fg_ref, o_ref):
    half = ya_ref.shape[2]
    y = _bdot(ya_ref[0], w_ref[0:half, :]) + _bdot(yb_ref[0], w_ref[half:, :])
    x = _load_tokens(x_ref) + ml_ref[0][2:3] * y
    out = x * lax.rsqrt(jnp.mean(x * x, axis=-1, keepdims=True) + RMS_EPS) * fg_ref[...]
    grid_rows = o_ref.shape[1]
    for j in range(o_ref.shape[2]):
        o_ref[0, :, j, :] = out[j * grid_rows:(j + 1) * grid_rows]


def _proj_out_final(ya, yb, x, mod3, w16, final_g):
    bsz, gr, gc, d = x.shape
    n_lat = gr * gc
    half = ya.shape[2]
    rows = FINAL_ROWS
    assert n_lat % rows == 0
    tok = lambda w: pl.BlockSpec((1, rows, w), lambda b, i: (b, i, 0))
    raster = _token_spec(x, rows, lambda b, i: (b, i))
    return pl.pallas_call(
        _proj_out_final_kernel,
        grid=(bsz, n_lat // rows),
        in_specs=[tok(half), tok(half), raster,
                  pl.BlockSpec((1, 3, d), lambda b, i: (b, 0, 0)),
                  pl.BlockSpec((2 * half, d), lambda b, i: (0, 0)),
                  pl.BlockSpec((1, d), lambda b, i: (0, 0))],
        out_specs=raster,
        out_shape=jax.ShapeDtypeStruct(x.shape, F32),
        compiler_params=_compiler_params(("parallel", "arbitrary")),
        name="proj_out_final_norm",
    )(ya, yb, x, mod3, w16, final_g.reshape(1, d))


def _mlstm_chunk_operators(chunks, causal):
    t = chunks[0][0].shape[0]
    lane = _iota((8, t), 1)
    row_id = _iota((8, t), 0)
    log_fs = [jnp.minimum(c[3], 0.0) - jnp.log1p(jnp.exp(-jnp.abs(c[3]))) for c in chunks]
    cum_f, cum_b = list(log_fs), list(log_fs)
    sh = 1
    while sh < t:
        cum_f = [x + jnp.where(lane >= sh, pltpu.roll(x, sh, 1), 0.0) for x in cum_f]
        cum_b = [x + jnp.where(lane < t - sh, pltpu.roll(x, t - sh, 1), 0.0) for x in cum_b]
        sh *= 2
    pad = jnp.zeros((t - 8, t), F32)
    tiles = [jnp.concatenate([jnp.where(row_id % 2 == 0, c[3], jnp.where(row_id == 1, f, b)), pad], axis=0)
             for c, f, b in zip(chunks, cum_f, cum_b)]
    cols = [x.T for x in tiles]
    k_ts = [c[1].T for c in chunks]
    problems = []
    for c, f, b, col, k_t in zip(chunks, cum_f, cum_b, cols, k_ts):
        for d in range(2):
            b_row = (f, b)[d][2 * d + 1:2 * d + 2]
            problems.append(dict(q=c[0], k=c[1], k_t=k_t, v_ext=c[2], d=d, ig_row=c[3][2 * d:2 * d + 1], b_row=b_row,
                                 b_col=col[:, 2 * d + 1:2 * d + 2]))
    logws = [jnp.where(causal[p["d"]], p["b_col"] + (p["ig_row"] - p["b_row"]), -jnp.inf) for p in problems]
    mus = [jnp.max(x, axis=-1, keepdims=True) for x in logws]
    ws = [jnp.exp(x - mu) for x, mu in zip(logws, mus)]
    lasts = [0 if p["d"] == 1 else t - 1 for p in problems]
    b_lasts = [p["b_col"][i:i + 1] for p, i in zip(problems, lasts)]
    gammas = [mu[i:i + 1] for mu, i in zip(mus, lasts)]
    gk_ts = [p["k_t"] * jnp.exp(bl - p["b_row"] + p["ig_row"] - gm) for p, bl, gm in zip(problems, b_lasts, gammas)]
    qks = [_bdot_nt(p["q"], p["k"]) * w for p, w in zip(problems, ws)]
    intras = [_bdot(qk, p["v_ext"]) for qk, p in zip(qks, problems)]
    kvs = [_bdot(gk_t, p["v_ext"]) for gk_t, p in zip(gk_ts, problems)]
    dh = chunks[0][0].shape[1]
    return [(intra, kv, jnp.broadcast_to(mu - p["b_col"], (t, dh)), jnp.broadcast_to(mu, (t, dh)), bl, gm)
            for intra, kv, p, mu, bl, gm in zip(intras, kvs, problems, mus, b_lasts, gammas)]


def _mlstm_kernel(q_ref, k_ref, v_ref, o_ref, z_ref, gt_ref, cwq_ref, cwk_ref, ng_ref, out_ref,
                  qa_scr, ka_scr, h_scr, intra_scr, kv_scr, delta_scr, mu_scr, tail_scr, *, n_ctx):
    s = q_ref.shape[1]
    dh = q_ref.shape[2]
    t = MIX_CHUNK
    n_chunks = s // t
    n_ctx_chunks = n_ctx // t
    k_scale = dh ** -0.5

    p_rows = MIX_PREP_ROWS

    def prep(j, carry):
        t0 = pl.multiple_of(j * p_rows, p_rows)
        for src, cw, dst, scale in ((q_ref, cwq_ref, qa_scr, 1.0), (k_ref, cwk_ref, ka_scr, k_scale)):
            cur, down, up = _chunk_with_neighbors(src, t0, p_rows, s, s - n_ctx)
            conv = down * cw[0:1, :] + cur * cw[1:2, :] + up * cw[2:3, :]
            dst[pl.ds(t0, p_rows), :] = _silu(conv) * scale
        return carry

    lax.fori_loop(0, s // p_rows, prep, 0)

    ones_col = jnp.ones((t, dh), F32)
    causal = [_iota((t, t), 1) <= _iota((t, t), 0), _iota((t, t), 1) >= _iota((t, t), 0)]

    def operators(gi, carry):
        chunks, where = [], []
        for kk in range(MLSTM_GROUP):
            chunk = gi * MLSTM_GROUP + kk
            sl = pl.ds(pl.multiple_of(chunk * t, t), t)
            v_ext = jnp.concatenate([v_ref[0, sl, :].astype(F32), ones_col], axis=1)
            chunks.append((qa_scr[sl, :], ka_scr[sl, :], v_ext, gt_ref[0, 0, :, sl]))
            where += [(0, chunk, sl), (1, chunk, sl)]
        for (d, chunk, sl), (intra, kv, delta, mu, b_last, gamma) in zip(where, _mlstm_chunk_operators(chunks, causal)):
            intra_scr[d, sl, :] = intra
            kv_scr[d, chunk] = kv
            delta_scr[d, sl, :] = delta
            mu_scr[d, sl, :] = mu
            tail_scr[d, chunk] = jnp.concatenate([jnp.broadcast_to(b_last, (1, dh)), jnp.broadcast_to(gamma, (1, dh))],
                                                 axis=0)
        return carry

    lax.fori_loop(0, n_chunks // MLSTM_GROUP, operators, 0)

    def scan(trip, carry):
        steps = []
        for k in range(SCAN_UNROLL):
            chunks = _scan_chunks(trip * SCAN_UNROLL + k, n_chunks, n_ctx_chunks)
            new = []
            for d, (chunk, (c_ext, m)) in enumerate(zip(chunks, carry)):
                steps.append((d, chunk, pl.ds(pl.multiple_of(chunk * t, t), t), c_ext, m))
                tail = tail_scr[d, chunk]
                b_last, gamma = tail[0:1, 0:1], tail[1:2, 0:1]
                m_new = jnp.maximum(b_last + m, gamma)
                new.append((jnp.exp(b_last + m - m_new) * c_ext + jnp.exp(gamma - m_new) * kv_scr[d, chunk], m_new))
            carry = tuple(new)
        inters = [_bdot(qa_scr[sl, :], c_ext) for _, _, sl, c_ext, _ in steps]
        for (d, chunk, sl, _, m), inter in zip(steps, inters):
            z = delta_scr[d, sl, :] - m
            s_inter = jnp.exp(-jnp.maximum(z, 0.0))
            s_intra = jnp.exp(jnp.minimum(z, 0.0))
            floor = jnp.exp(jnp.minimum(z, 0.0) - mu_scr[d, sl, :])
            intra = intra_scr[d, sl, :]
            num = s_inter * inter[:, :dh] + s_intra * intra[:, :dh]
            den = s_inter * inter[:, dh:] + s_intra * intra[:, dh:]
            h_scr[d, sl, :] = num / jnp.maximum(jnp.abs(den), floor)
        return carry

    zero = (jnp.zeros((dh, 2 * dh), F32), jnp.zeros((1, 1), F32))
    lax.fori_loop(0, n_chunks // SCAN_UNROLL, scan, (zero, zero))

    f_rows = MIX_FINISH_ROWS

    def finish(j, carry):
        sl = pl.ds(pl.multiple_of(j * f_rows, f_rows), f_rows)
        h = h_scr[0, sl, :] + h_scr[1, sl, :]
        y = h * lax.rsqrt(jnp.mean(h * h, axis=-1, keepdims=True) + RMS_EPS) * ng_ref[...]
        gated = y * _sigmoid(o_ref[0, sl, :].astype(F32)) * _silu(z_ref[0, sl, :].astype(F32))
        out_ref[0, sl, :] = gated.astype(out_ref.dtype)
        return carry

    lax.fori_loop(0, s // f_rows, finish, 0)


def _mlstm(u, gt, conv_w, norm_g, n_ctx):
    bsz, s, _ = u.shape
    nh = MLSTM_N_HEADS
    dh = LANES
    width = nh * dh
    n_chunks = s // MIX_CHUNK
    assert n_chunks % MLSTM_GROUP == 0 and n_chunks % SCAN_UNROLL == 0 and n_ctx % MIX_CHUNK == 0
    col = lambda k: pl.BlockSpec((1, s, dh), lambda b, h, k=k: (b, 0, k * nh + h))
    par = lambda k: pl.BlockSpec((3, dh), lambda b, h, k=k: (0, k * nh + h))
    return pl.pallas_call(
        functools.partial(_mlstm_kernel, n_ctx=n_ctx),
        grid=(bsz, nh),
        in_specs=[col(0), col(1), col(2), col(3), col(4),
                  pl.BlockSpec((1, 1, 8, s), lambda b, h: (b, h, 0, 0)),
                  par(0), par(1),
                  pl.BlockSpec((1, dh), lambda b, h: (0, h))],
        out_specs=pl.BlockSpec((1, s, dh), lambda b, h: (b, 0, h)),
        out_shape=jax.ShapeDtypeStruct((bsz, s, width), BF16),
        scratch_shapes=[pltpu.VMEM((s, dh), F32), pltpu.VMEM((s, dh), F32), pltpu.VMEM((2, s, dh), F32),
                        pltpu.VMEM((2, s, 2 * dh), F32), pltpu.VMEM((2, n_chunks, dh, 2 * dh), F32),
                        pltpu.VMEM((2, s, dh), F32), pltpu.VMEM((2, s, dh), F32),
                        pltpu.VMEM((2, n_chunks, 2, dh), F32)],
        compiler_params=_compiler_params(("parallel", "arbitrary")),
        name="mlstm_mixer",
    )(u, u, u, u, u, gt, conv_w, conv_w, norm_g.reshape(1, width))


def _head_stack(x, lane_lo):
    return jnp.concatenate([jnp.where(lane_lo, x, 0.0), jnp.where(lane_lo, 0.0, x)], axis=0)


def _half_rows(x, c, upper):
    start = c if upper else 0
    return jnp.concatenate([x[r + start:r + start + c] for r in range(0, x.shape[0], 2 * c)], axis=0)


def _merge_rows(other, part, c, upper):
    pieces = []
    for k in range(part.shape[0] // c):
        pair = (other[k * c:(k + 1) * c], part[k * c:(k + 1) * c])
        pieces += pair if upper else pair[::-1]
    return jnp.concatenate(pieces, axis=0)


def _spread_rows(part, c, upper):
    return _merge_rows(jnp.zeros_like(part), part, c, upper)


def _rwkv_chunk_operators(problems, consts, eye, lane_lo):
    t, w = problems[0][0].shape
    n2 = 2 * t
    stack = lambda x: _head_stack(x, lane_lo)
    zeros = jnp.zeros((n2, w), F32)
    dirs = [p[6] for p in problems]
    rid = _iota((t, w), 0)
    cums = [p[3] for p in problems]
    sh = 1
    while sh < t:
        cums = [x + (jnp.where(rid < t - sh, pltpu.roll(x, t - sh, 0), 0.0) if d == 1 else
                     jnp.where(rid >= sh, pltpu.roll(x, sh, 0), 0.0)) for x, d in zip(cums, dirs)]
        sh *= 2
    pre = []
    for (r, v, kk, lw, ka, kt, d), cum in zip(problems, cums):
        last = 0 if d == 1 else t - 1
        cum_end = cum[last:last + 1]
        e_inv = jnp.exp(-cum)
        e_end = jnp.exp(cum_end - cum)
        a_s = stack(-kk * jnp.exp(cum - lw))
        r_s = stack(r * jnp.exp(cum))
        pre.append(dict(a_s=a_s, r_s=r_s, vs=stack(v), g=jnp.exp(cum_end),
                        ar=jnp.concatenate([a_s, r_s], axis=0),
                        bk=jnp.concatenate([stack(ka * e_inv), stack(kt * e_inv)], axis=0),
                        bk_end=jnp.concatenate([stack(ka * e_end), stack(kt * e_end)], axis=0)))
    m_alls = [_bdot_nt(q["ar"], q["bk"]) for q in pre]
    m_abs = [jnp.where(consts[d]["strict"], m[:n2, :n2], 0.0) for m, d in zip(m_alls, dirs)]
    m_aks = [jnp.where(consts[d]["strict"], m[:n2, n2:], 0.0) for m, d in zip(m_alls, dirs)]
    m_lows = [jnp.where(consts[d]["incl2"], m[n2:, :], 0.0) for m, d in zip(m_alls, dirs)]
    invs = [eye + jnp.where(consts[d]["merges"][0][1], m, 0.0) for m, d in zip(m_abs, dirs)]
    for level in range(1, len(consts[0]["merges"])):
        c = consts[0]["merges"][level][0]
        if c < SUBLANES:
            inner = [_bdot(jnp.where(consts[d]["merges"][level][1], m, 0.0), x) for m, x, d in zip(m_abs, invs, dirs)]
            invs = [x + _bdot(x, y) for x, y in zip(invs, inner)]
        else:
            ups = [d == 0 for d in dirs]
            c_rows = [jnp.where(consts[d]["merges"][level][1], _half_rows(m, c, up), 0.0) for m, d, up in zip(m_abs, dirs, ups)]
            inner = [_bdot(cr, x) for cr, x in zip(c_rows, invs)]
            x_rows = [_half_rows(x, c, up) for x, up in zip(invs, ups)]
            upd = [xr + _bdot(xr, _spread_rows(y, c, up)) for xr, y, up in zip(x_rows, inner, ups)]
            invs = [_merge_rows(_half_rows(x, c, not up), u, c, up) for x, u, up in zip(invs, upd, ups)]
    mv = [_bdot(m, q["vs"]) for m, q in zip(m_aks, pre)]
    solved = [_bdot(x, jnp.concatenate([q["a_s"], y], axis=1)) for x, q, y in zip(invs, pre, mv)]
    zms = [jnp.concatenate([sv, jnp.concatenate([zeros, q["vs"]], axis=1)], axis=0) for sv, q in zip(solved, pre)]
    ry1s = [jnp.concatenate([q["r_s"], zeros], axis=1) + _bdot(m, z) for q, m, z in zip(pre, m_lows, zms)]
    pqs = [_bdot_tn(z, q["bk_end"]) for z, q in zip(zms, pre)]
    out = []
    for ry1, pq, q in zip(ry1s, pqs, pre):
        folded = ry1[:t] + ry1[t:]
        out.append((folded[:, :w], folded[:, w:], pq[:w], pq[w:], q["g"]))
    return out


def _rwkv_kernel(rr_ref, rk_ref, rv_ref, rz_ref, wd_ref, ad_ref, mu_ref, kk_ref, ka_ref, rkk_ref,
                 lnw_ref, lnb_ref, w0_ref, a0_ref, wup_ref, aup_ref, out_ref,
                 r_scr, v_scr, kk_scr, lw_scr, ka_scr, kt_scr, bonus_scr, y_scr, ry_scr, pp_scr, qq_scr, qt_scr, g_scr,
                 *, n_ctx):
    s = rr_ref.shape[1]
    w = rr_ref.shape[2]
    p_rows = RWKV_PREP_ROWS
    t = RWKV_CHUNK
    n_chunks = s // t
    n_ctx_chunks = n_ctx // t
    head_sum = ((_iota((w, w), 0) // RWKV_HEAD) == (_iota((w, w), 1) // RWKV_HEAD)).astype(BF16)
    inv_head = 1.0 / RWKV_HEAD

    def prep(j, carry):
        sls = [pl.ds(pl.multiple_of((j * RWKV_PREP_BLOCKS + k) * p_rows, p_rows), p_rows) for k in range(RWKV_PREP_BLOCKS)]
        mixed = []
        for sl in sls:
            shifted = []
            for idx, src in enumerate((rr_ref, rk_ref, rv_ref)):
                cur, down, up = _chunk_with_neighbors(src, sl.start, p_rows, s, s - n_ctx)
                shifted.append(cur + mu_ref[idx:idx + 1, :] * (0.5 * (down + up) - cur))
            mixed.append(shifted)
        kks = [kr * kk_ref[...] for _, kr, _ in mixed]
        norms = [jnp.sqrt(_dot_sel(kk * kk, head_sum)) for kk in kks]
        kks = [kk / jnp.maximum(norm, 1e-12) for kk, norm in zip(kks, norms)]
        w_raws = [_bdot(jnp.tanh(wd_ref[0, sl, :].astype(F32)), wup_ref[0]) + w0_ref[0] for sl in sls]
        a_raws = [_bdot(ad_ref[0, sl, :], aup_ref[0]) + a0_ref[0] for sl in sls]
        kt_sums = []
        for sl, (r, kr, v), kk, w_raw, a_raw in zip(sls, mixed, kks, w_raws, a_raws):
            a = _sigmoid(a_raw)
            lw = -math.exp(-0.5) * _sigmoid(w_raw)
            kt_sum = jnp.zeros_like(kr)
            for d in range(2):
                a_d = a[:, d * w:(d + 1) * w]
                kt_d = kr * (1.0 + (a_d - 1.0) * ka_ref[...])
                kt_sum = kt_sum + kt_d
                lw_scr[d, sl, :] = lw[:, d * w:(d + 1) * w]
                ka_scr[d, sl, :] = kk * a_d
                kt_scr[d, sl, :] = kt_d
            kt_sums.append(kt_sum)
            r_scr[sl, :] = r
            v_scr[sl, :] = v
            kk_scr[sl, :] = kk
        coefs = [_dot_sel(r * kt_sum * rkk_ref[...], head_sum) for (r, _, _), kt_sum in zip(mixed, kt_sums)]
        for sl, (_, _, v), coef in zip(sls, mixed, coefs):
            bonus_scr[sl, :] = coef * v
        return carry

    lax.fori_loop(0, s // (p_rows * RWKV_PREP_BLOCKS), prep, 0)

    n2 = 2 * t
    r_i = _iota((n2, n2), 0)
    c_i = _iota((n2, n2), 1)
    same = (r_i // t) == (c_i // t)
    rt = r_i % t
    ct = c_i % t
    eye = (r_i == c_i).astype(F32)
    lane_lo = _iota((t, w), 1) < RWKV_HEAD
    consts = []
    for reverse in (False, True):
        strict = jnp.logical_and(same, (ct > rt) if reverse else (ct < rt))
        incl = jnp.logical_and(same, (ct >= rt) if reverse else (ct <= rt))
        merges = []
        c = 1
        while c < t:
            hi_r = (r_i % (2 * c)) >= c
            hi_c = (c_i % (2 * c)) >= c
            cross = jnp.logical_and(hi_c, jnp.logical_not(hi_r)) if reverse else jnp.logical_and(hi_r, jnp.logical_not(hi_c))
            mask = jnp.logical_and((r_i // (2 * c)) == (c_i // (2 * c)), cross)
            merges.append((c, _half_rows(mask, c, not reverse)) if c >= SUBLANES else (c, mask))
            c *= 2
        consts.append(dict(strict=strict, incl2=jnp.concatenate([incl, incl], axis=1), merges=merges))

    def operators(gi, carry):
        problems, where = [], []
        for k in range(RWKV_GROUP):
            chunk = gi * RWKV_GROUP + k
            sl = pl.ds(pl.multiple_of(chunk * t, t), t)
            r, v, kk = r_scr[sl, :], v_scr[sl, :], kk_scr[sl, :]
            for d in range(2):
                problems.append((r, v, kk, lw_scr[d, sl, :], ka_scr[d, sl, :], kt_scr[d, sl, :], d))
                where.append((d, chunk, sl))
        ops = _rwkv_chunk_operators(problems, consts, eye, lane_lo)
        for (d, chunk, sl), (ry, y1, pt, qt, g) in zip(where, ops):
            ry_scr[d, sl, :] = ry.astype(BF16)
            y_scr[d, sl, :] = y1
            qt_scr[d, chunk] = qt
            g_scr[d, chunk] = g
        pairs = []
        for k in range(0, RWKV_GROUP, 2):
            for d in range(2):
                first, second = (ops[2 * k + d], ops[2 * (k + 1) + d]) if d == 0 else (ops[2 * (k + 1) + d], ops[2 * k + d])
                pairs.append((d, (gi * RWKV_GROUP + k) // 2, first, second))
        prods = [_bdot(jnp.concatenate([eye * a[4] + a[2], a[3]], axis=0), b[2]) for _, _, a, b in pairs]
        for (d, pair, a, b), prod in zip(pairs, prods):
            pp_scr[d, pair] = jnp.concatenate([a[2], prod[:w] + a[2] * b[4]], axis=1).astype(BF16)
            qq_scr[d, pair] = a[3] * b[4] + prod[w:] + b[3]
        return carry

    lax.fori_loop(0, n_chunks // RWKV_GROUP, operators, 0)

    def scan(trip, carry):
        outs = []
        for k in range(RWKV_SCAN_UNROLL):
            step = trip * RWKV_SCAN_UNROLL + k
            first = _scan_chunks(2 * step, n_chunks, n_ctx_chunks)
            second = _scan_chunks(2 * step + 1, n_chunks, n_ctx_chunks)
            pair = [jnp.minimum(a, b) // 2 for a, b in zip(first, second)]
            moved = [_bdot(ht, pp_scr[d, p]) for d, (p, ht) in enumerate(zip(pair, carry))]
            mids = [ht * g_scr[d, a] + mv[:, :w] + qt_scr[d, a] for d, (a, ht, mv) in enumerate(zip(first, carry, moved))]
            for d, (a, b, ht, mid) in enumerate(zip(first, second, carry, mids)):
                sl_a = pl.ds(pl.multiple_of(a * t, t), t)
                sl_b = pl.ds(pl.multiple_of(b * t, t), t)
                outs.append((d, sl_a, _bdot_nt(ry_scr[d, sl_a, :], ht)))
                outs.append((d, sl_b, _bdot_nt(ry_scr[d, sl_b, :], mid)))
            carry = tuple(ht * (g_scr[d, a] * g_scr[d, b]) + mv[:, w:] + qq_scr[d, p]
                          for d, (a, b, p, ht, mv) in enumerate(zip(first, second, pair, carry, moved)))
        for d, sl, y in outs:
            y_scr[d, sl, :] = y + y_scr[d, sl, :]
        return carry

    zero_state = jnp.zeros((w, w), F32)
    lax.fori_loop(0, n_chunks // (2 * RWKV_SCAN_UNROLL), scan, (zero_state, zero_state))

    f_rows = RWKV_FINISH_ROWS

    def finish(j, carry):
        sl = pl.ds(pl.multiple_of(j * f_rows, f_rows), f_rows)
        y = y_scr[0, sl, :] + y_scr[1, sl, :] + bonus_scr[sl, :]
        mu = _dot_sel(y, head_sum) * inv_head
        yc = y - mu
        var = _dot_sel(yc * yc, head_sum) * inv_head
        yn = yc * lax.rsqrt(var + RWKV_LN_EPSILON) * lnw_ref[...] + lnb_ref[...]
        out_ref[0, sl, :] = (yn * _silu(rz_ref[0, sl, :].astype(F32))).astype(out_ref.dtype)
        return carry

    lax.fori_loop(0, s // f_rows, finish, 0)


def _rwkv(u, col0, p, n_ctx):
    bsz, s, _ = u.shape
    w = LANES
    width = p["mu"].shape[1]
    n_pairs = width // w
    base = col0 // w
    col = lambda k: pl.BlockSpec((1, s, w), lambda b, h, k=k: (b, 0, base + k * n_pairs + h))
    lora = lambda k: pl.BlockSpec((1, s, w), lambda b, h, k=k: (b, 0, base + 4 * n_pairs + k))
    vec = lambda rows: pl.BlockSpec((rows, w), lambda b, h: (0, h))
    cat = pl.BlockSpec((1, 1, 2 * w), lambda b, h: (h, 0, 0))
    up = pl.BlockSpec((1, w, 2 * w), lambda b, h: (h, 0, 0))
    seq = pltpu.VMEM((s, w), F32)
    seq2 = pltpu.VMEM((2, s, w), F32)
    n_chunks = s // RWKV_CHUNK
    assert n_chunks % RWKV_GROUP == 0 and s % (RWKV_PREP_ROWS * RWKV_PREP_BLOCKS) == 0 and n_ctx % RWKV_PREP_ROWS == 0
    assert s % RWKV_FINISH_ROWS == 0 and n_chunks % (2 * RWKV_SCAN_UNROLL) == 0
    assert RWKV_GROUP % 2 == 0 and (n_ctx // RWKV_CHUNK) % 2 == 0
    operators = [pltpu.VMEM((2, s, w), BF16), pltpu.VMEM((2, n_chunks // 2, w, 2 * w), BF16),
                 pltpu.VMEM((2, n_chunks // 2, w, w), F32),
                 pltpu.VMEM((2, n_chunks, w, w), F32), pltpu.VMEM((2, n_chunks, 1, w), F32)]
    return pl.pallas_call(
        functools.partial(_rwkv_kernel, n_ctx=n_ctx),
        grid=(bsz, n_pairs),
        in_specs=[col(0), col(1), col(2), col(3), lora(0), lora(1),
                  vec(3), vec(1), vec(1), vec(1), vec(1), vec(1), cat, cat, up, up],
        out_specs=pl.BlockSpec((1, s, w), lambda b, h: (b, 0, h)),
        out_shape=jax.ShapeDtypeStruct((bsz, s, width), BF16),
        scratch_shapes=[seq, seq, seq, seq2, seq2, seq2, seq, seq2] + operators,
        compiler_params=_compiler_params(("parallel", "arbitrary")),
        name="rwkv7_mixer",
    )(u, u, u, u, u, u, p["mu"], p["k_k"], p["k_a"], p["r_k"], p["ln_w"], p["ln_b"],
      p["w0"], p["a0"], p["w_up"], p["a_up"])


def _hgrn_level_masks(t, w):
    rid = _iota((t, w), 0)
    r_i = _iota((t, t), 0)
    c_i = _iota((t, t), 1)
    levels = []
    c = 1
    while c < t:
        same_block = (r_i // (2 * c)) == (c_i // (2 * c))
        up_r = (r_i % (2 * c)) >= c
        up_c = (c_i % (2 * c)) >= c
        pair = [jnp.logical_and(same_block, jnp.logical_and(up_r, jnp.logical_not(up_c))),
                jnp.logical_and(same_block, jnp.logical_and(up_c, jnp.logical_not(up_r)))]
        levels.append((c, (rid % (2 * c)) >= c, pair))
        c *= 2
    return levels


def _hgrn_chunk_operators(problems, lb, tris, levels):
    t, w = problems[0][0].shape
    r_i = _iota((t, t), 0)
    c_i = _iota((t, t), 1)
    zero_row = jnp.zeros((1, w), F32)
    dirs = [p[3] for p in problems]
    lgs, ks = [], []
    for q, v, ff, d in problems:
        e = jnp.exp(-jnp.abs(ff))
        big = 1.0 / (1.0 + e)
        small = e / (1.0 + e)
        pos = ff >= 0.0
        lgs.append(jnp.log(lb + (1.0 - lb) * jnp.where(pos, big, small)) * LOG2_E)
        ks.append((1.0 - lb) * jnp.where(pos, small, big))
    bs = [_sel_dot(tris[d], lg) for lg, d in zip(lgs, dirs)]
    befores = [_shifted(b, zero_row, zero_row)[1 if d == 1 else 0] for b, d in zip(bs, dirs)]
    edges = list(bs)
    accs = [jnp.where(r_i == c_i, jnp.sum(p[0] * k, axis=-1, keepdims=True), 0.0) for p, k in zip(problems, ks)]
    for c, upper, pair in levels:
        qts = [p[0] * jnp.exp2(b - before) for p, b, before in zip(problems, bs, befores)]
        kts = [k * jnp.exp2(edge - b) for k, b, edge in zip(ks, bs, edges)]
        prods = [_bdot_nt(qt, kt) for qt, kt in zip(qts, kts)]
        accs = [jnp.where(pair[d], pr, a) for a, pr, d in zip(accs, prods, dirs)]
        for i, d in enumerate(dirs):
            if d == 1:
                befores[i] = jnp.where(upper, befores[i], pltpu.roll(befores[i], t - c, 0))
                edges[i] = jnp.where(upper, pltpu.roll(edges[i], c, 0), edges[i])
            else:
                befores[i] = jnp.where(upper, pltpu.roll(befores[i], c, 0), befores[i])
                edges[i] = jnp.where(upper, edges[i], pltpu.roll(edges[i], t - c, 0))
    o_intras = [_bdot(a, p[1]) for a, p in zip(accs, problems)]
    b_ends = [b[(0 if d == 1 else t - 1):(1 if d == 1 else t)] for b, d in zip(bs, dirs)]
    kvs = [_bdot_tn(p[1], k * jnp.exp2(be - b)) for p, k, b, be in zip(problems, ks, bs, b_ends)]
    return [(p[0] * jnp.exp2(b), oi, kv, jnp.exp2(be)) for p, b, oi, kv, be in zip(problems, bs, o_intras, kvs, b_ends)]


def _hgrn_kernel(q_ref, i_ref, ff_ref, fb_ref, z_ref, lb_ref, ng_ref, out_ref, o_scr, qe_scr, kv_scr, g_scr,
                 *, n_ctx, layer):
    s = q_ref.shape[1]
    dh = q_ref.shape[2]
    t = MIX_CHUNK
    n_chunks = s // t
    n_ctx_chunks = n_ctx // t
    lbs = lb_ref[...]
    ex = jnp.exp(lbs - jnp.max(lbs, axis=0, keepdims=True))
    probs = ex / jnp.sum(ex, axis=0, keepdims=True)
    csum = probs[0:1]
    for l in range(1, layer + 1):
        csum = csum + probs[l:l + 1]
    lb = csum - probs[0:1]
    tri_r = _iota((t, t), 0)
    tri_c = _iota((t, t), 1)
    tris = [(tri_c <= tri_r).astype(BF16), (tri_c >= tri_r).astype(BF16)]
    levels = _hgrn_level_masks(t, dh)

    def operators(gi, carry):
        problems, where = [], []
        for kk in range(HGRN_GROUP):
            chunk = gi * HGRN_GROUP + kk
            sl = pl.ds(pl.multiple_of(chunk * t, t), t)
            q, v = q_ref[0, sl, :].astype(F32), i_ref[0, sl, :].astype(F32)
            for d, f_ref in enumerate((ff_ref, fb_ref)):
                problems.append((q, v, f_ref[0, sl, :].astype(F32), d))
                where.append((d, chunk, sl))
        for (d, chunk, sl), (qe, o_intra, kv, g) in zip(where, _hgrn_chunk_operators(problems, lb, tris, levels)):
            qe_scr[d, sl, :] = qe.astype(BF16)
            o_scr[d, sl, :] = o_intra
            kv_scr[d, chunk] = kv
            g_scr[d, chunk] = g
        return carry

    lax.fori_loop(0, n_chunks // HGRN_GROUP, operators, 0)

    def scan(trip, carry):
        steps = []
        for k in range(SCAN_UNROLL):
            chunks = _scan_chunks(trip * SCAN_UNROLL + k, n_chunks, n_ctx_chunks)
            steps += [(d, pl.ds(pl.multiple_of(chunk * t, t), t), st) for d, (chunk, st) in enumerate(zip(chunks, carry))]
            carry = tuple(st * g_scr[d, chunk] + kv_scr[d, chunk] for d, (chunk, st) in enumerate(zip(chunks, carry)))
        inters = [_bdot_nt(qe_scr[d, sl, :], st) for d, sl, st in steps]
        for (d, sl, _), inter in zip(steps, inters):
            o_scr[d, sl, :] = o_scr[d, sl, :] + inter
        return carry

    zero_state = jnp.zeros((dh, dh), F32)
    lax.fori_loop(0, n_chunks // SCAN_UNROLL, scan, (zero_state, zero_state))

    f_rows = MIX_FINISH_ROWS

    def finish(j, carry):
        sl = pl.ds(pl.multiple_of(j * f_rows, f_rows), f_rows)
        o = o_scr[0, sl, :] + o_scr[1, sl, :]
        y = o * lax.rsqrt(jnp.mean(o * o, axis=-1, keepdims=True) + RMS_EPS) * ng_ref[...]
        out_ref[0, sl, :] = (y * _silu(z_ref[0, sl, :].astype(F32))).astype(out_ref.dtype)
        return carry

    lax.fori_loop(0, (s - n_ctx) // f_rows, finish, 0)


def _hgrn(u, lb_all, norm_g, n_ctx, layer):
    bsz, s, _ = u.shape
    nh = HGRN_N_HEADS
    dh = LANES
    width = nh * dh
    depth = lb_all.shape[0]
    n_chunks = s // MIX_CHUNK
    assert n_chunks % HGRN_GROUP == 0 and n_chunks % SCAN_UNROLL == 0 and n_ctx % MIX_CHUNK == 0
    col = lambda k: pl.BlockSpec((1, s, dh), lambda b, h, k=k: (b, 0, k * nh + h))
    return pl.pallas_call(
        functools.partial(_hgrn_kernel, n_ctx=n_ctx, layer=layer),
        grid=(bsz, nh),
        in_specs=[col(0), col(1), col(2), col(3), col(4),
                  pl.BlockSpec((depth, dh), lambda b, h: (0, h)),
                  pl.BlockSpec((1, dh), lambda b, h: (0, h))],
        out_specs=pl.BlockSpec((1, s - n_ctx, dh), lambda b, h: (b, 0, h)),
        out_shape=jax.ShapeDtypeStruct((bsz, s - n_ctx, width), BF16),
        scratch_shapes=[pltpu.VMEM((2, s, dh), F32), pltpu.VMEM((2, s, dh), BF16),
                        pltpu.VMEM((2, n_chunks, dh, dh), F32), pltpu.VMEM((2, n_chunks, 1, dh), F32)],
        compiler_params=_compiler_params(("parallel", "arbitrary")),
        name="hgrn2_mixer",
    )(u, u, u, u, u, lb_all, norm_g.reshape(1, width))


def _hyena_filter_kernel(z_ref, w1_ref, b1_ref, w2_ref, b2_ref, w3f_ref, w3b_ref, dl_ref, hf_ref, hb_ref):
    hp = functools.partial(jnp.dot, precision=lax.Precision.HIGHEST, preferred_element_type=F32)
    n = z_ref.shape[0]
    hid = jnp.sin(hp(z_ref[...], w1_ref[...]) + b1_ref[...])
    hid = jnp.sin(hp(hid, w2_ref[...]) + b2_ref[...])
    pos = _iota((n, 1), 0).astype(F32) * (1.0 / n)
    window = jnp.exp(-pos * dl_ref[...]) + HYENA_SHIFT
    f0 = hp(hid, w3f_ref[...]) * window
    f1 = hp(hid, w3b_ref[...]) * window
    nrm = jnp.sum(jnp.abs(f0), axis=0, keepdims=True) + jnp.sum(jnp.abs(f1), axis=0, keepdims=True)
    hf_ref[...] = f0 / nrm
    hb_ref[...] = f1 / nrm


def _hyena_filters(n, w1, b1, w2, b2, w3, width):
    pos = np.arange(n, dtype=np.float64)
    bands = np.linspace(1e-4, HYENA_N_BANDS - 1, HYENA_N_BANDS)
    ang = (2.0 * math.pi / n) * pos[:, None] * bands
    z = np.concatenate([(pos / n)[:, None], np.cos(ang), np.sin(ang)], axis=-1)
    z = np.pad(z, ((0, 0), (0, LANES - z.shape[1]))).astype(np.float32)
    max_decay = math.log(HYENA_TGT) / HYENA_FAST
    min_decay = math.log(HYENA_TGT) / HYENA_SLOW
    deltas = np.abs(np.linspace(min_decay, max_decay, width)).astype(np.float32)[None]
    feat, hid = w1.shape
    w1p = jnp.pad(w1, ((0, LANES - feat), (0, LANES - hid)))
    w2p = jnp.pad(w2, ((0, LANES - hid), (0, LANES - hid)))
    w3p = jnp.pad(w3, ((0, LANES - hid), (0, 0)))
    b1p = jnp.pad(b1, (0, LANES - hid)).reshape(1, LANES)
    b2p = jnp.pad(b2, (0, LANES - hid)).reshape(1, LANES)
    n_tiles = width // LANES
    full = lambda shape: pl.BlockSpec(shape, lambda j: (0, 0))
    out = pl.BlockSpec((n, LANES), lambda j: (0, j))
    hf, hb = pl.pallas_call(
        _hyena_filter_kernel,
        grid=(n_tiles,),
        in_specs=[full((n, LANES)), full((LANES, LANES)), full((1, LANES)), full((LANES, LANES)), full((1, LANES)),
                  pl.BlockSpec((LANES, LANES), lambda j: (0, j)),
                  pl.BlockSpec((LANES, LANES), lambda j: (0, n_tiles + j)),
                  pl.BlockSpec((1, LANES), lambda j: (0, j))],
        out_specs=[out, out],
        out_shape=[jax.ShapeDtypeStruct((n, width), F32)] * 2,
        compiler_params=_compiler_params(("arbitrary",)),
        name="hyena_filters",
    )(jnp.asarray(z), w1p, b1p, w2p, b2p, w3p, w3p, jnp.asarray(deltas))
    return jnp.concatenate([hf, hb], axis=1)


def _hyena_pre_kernel(yv_ref, y0_ref, y1_ref, yz_ref, swv_ref, sw0_ref, sw1_ref, sbv_ref, sb0_ref, sb1_ref,
                      p_ref, e_ref, *, n_ctx):
    s = yv_ref.shape[1]
    rows = MIX_CHUNK

    def body(j, carry):
        t0 = pl.multiple_of(j * rows, rows)
        conv = []
        for src, sw, sb in ((yv_ref, swv_ref, sbv_ref), (y0_ref, sw0_ref, sb0_ref), (y1_ref, sw1_ref, sb1_ref)):
            cur, down, up = _chunk_with_neighbors(src, t0, rows, s, s - n_ctx)
            conv.append(down * sw[0:1, :] + cur * sw[1:2, :] + up * sw[2:3, :] + sb[...])
        v, x0, x1 = conv
        p = x1 * v
        o0 = pl.multiple_of(j * rows, rows)
        p_ref[0, pl.ds(o0, rows), :] = p.astype(BF16)
        e_ref[0, pl.ds(o0, rows), :] = (x0 * _silu(yz_ref[0, pl.ds(t0, rows), :].astype(F32))).astype(BF16)
        return carry

    lax.fori_loop(0, (s - n_ctx) // rows, body, 0)


def _hyena_pre(u, col0, short_w, short_b, n_ctx):
    bsz, s, _ = u.shape
    w = short_b.shape[0] // 3
    tiles = w // LANES
    base = col0 // LANES
    n = s - n_ctx
    col = lambda k: pl.BlockSpec((1, s, LANES), lambda b, j, k=k: (b, 0, base + k * tiles + j))
    par = lambda rows, k: pl.BlockSpec((rows, LANES), lambda b, j, k=k: (0, k * tiles + j))
    out = pl.BlockSpec((1, n, LANES), lambda b, j: (b, 0, j))
    sb = short_b.reshape(1, 3 * w)
    return pl.pallas_call(
        functools.partial(_hyena_pre_kernel, n_ctx=n_ctx),
        grid=(bsz, tiles),
        in_specs=[col(0), col(1), col(2), col(3),
                  par(3, 0), par(3, 1), par(3, 2), par(1, 0), par(1, 1), par(1, 2)],
        out_specs=[out, out],
        out_shape=[jax.ShapeDtypeStruct((bsz, n, w), BF16)] * 2,
        compiler_params=_compiler_params(("parallel", "arbitrary")),
        name="hyena_short_conv",
    )(u, u, u, u, short_w, short_w, short_w, sb, sb, sb)


def _dft_tables(n):
    big = 2 * n
    half = DFT_TILE // 2
    idx = jnp.arange(n, dtype=jnp.int32)
    split = DFT_SPLIT
    lo = jnp.arange(split, dtype=jnp.int32)
    hi = jnp.arange(n // split, dtype=jnp.int32)
    ang_lo = ((lo[:, None] * idx[None, :]) % big).astype(F32) * (2.0 * math.pi / big)
    ang_hi = ((hi[:, None] * idx[None, :]) % (big // split)).astype(F32) * (2.0 * math.pi * split / big)
    c_lo, s_lo = jnp.cos(ang_lo)[None], jnp.sin(ang_lo)[None]
    c_hi, s_hi = jnp.cos(ang_hi)[:, None], jnp.sin(ang_hi)[:, None]
    cos = (c_hi * c_lo - s_hi * s_lo).reshape(n, n)
    sin = (s_hi * c_lo + c_hi * s_lo).reshape(n, n)
    alt = jnp.where(idx % 2 == 0, 1.0, -1.0).astype(F32)
    first_row = (idx == 0)[:, None]
    im = jnp.where(first_row, alt[None, :], -sin)
    fwd = jnp.stack([cos.reshape(n // half, half, n), im.reshape(n // half, half, n)], axis=1).reshape(big, n)
    weight = jnp.where(first_row, 1.0, 2.0) * (1.0 / big)
    weight = jnp.stack([weight.reshape(n // half, half, 1)] * 2, axis=1).reshape(big, 1)
    return fwd.astype(BF16), (fwd * weight).T.astype(BF16)


def _spectrum_kernel(f_ref, lo_ref, hi_ref, bias_ref, o_ref):
    half = DFT_TILE // 2
    w = lo_ref.shape[1] // 2
    lo = lo_ref[...].astype(BF16)
    acc_lo = jnp.dot(f_ref[...], lo, preferred_element_type=F32)
    acc_hi = jnp.dot(f_ref[...], hi_ref[...].astype(BF16), preferred_element_type=F32)
    rows = _iota((DFT_TILE, w), 0)
    sign = jnp.where(rows % 2 == 1, -1.0, 1.0)
    real_slot = jnp.logical_or(rows < half, jnp.logical_and(rows == half, pl.program_id(0) == 0))
    conj = lambda x: jnp.where(real_slot, x, -x)
    lag0 = lo[0:1, :].astype(F32)
    a, c_all = acc_lo[:, :w], acc_lo[:, w:]
    b, d = acc_hi[:, :w], acc_hi[:, w:]
    c = c_all - jnp.where(real_slot, lag0[:, w:], 0.0)
    o_ref[:, 0:w] = a + conj(c_all) + jnp.where(real_slot, bias_ref[...], 0.0)
    o_ref[:, w:2 * w] = b + sign * (a - jnp.where(real_slot, lag0[:, :w], 0.0))
    o_ref[:, 2 * w:] = sign * conj(c) + conj(d)


def _filter_spectrum(fwd, hk, bias):
    big, h = fwd.shape
    w = hk.shape[1] // 2
    assert hk.shape[0] == 2 * h
    return pl.pallas_call(
        _spectrum_kernel,
        grid=(big // DFT_TILE,),
        in_specs=[pl.BlockSpec((DFT_TILE, h), lambda i: (i, 0)),
                  pl.BlockSpec((h, 2 * w), lambda i: (0, 0)),
                  pl.BlockSpec((h, 2 * w), lambda i: (1, 0)),
                  pl.BlockSpec((1, w), lambda i: (0, 0))],
        out_specs=pl.BlockSpec((DFT_TILE, 3 * w), lambda i: (i, 0)),
        out_shape=jax.ShapeDtypeStruct((big, 3 * w), F32),
        compiler_params=_compiler_params(("arbitrary",)),
        name="hyena_filter_spectrum",
    )(fwd, hk, hk, bias.reshape(1, w))


def _spectral_products(pairs, packed):
    re = sum(s[0] * k[0] - s[1] * k[1] for s, k in pairs)
    im = sum(s[0] * k[1] + s[1] * k[0] for s, k in pairs)
    if packed is not None:
        re = jnp.where(packed, sum(s[0] * k[0] for s, k in pairs), re)
        im = jnp.where(packed, sum(s[1] * k[1] for s, k in pairs), im)
    return jnp.concatenate([re, im], axis=0).astype(BF16)


def _conv_spectrum_kernel(f_ref, pa_ref, pb_ref, ks_ref, z_ref):
    i = pl.program_id(1)
    half = DFT_TILE // 2
    w = pa_ref.shape[2]
    units = [slice(k * DFT_TILE, (k + 1) * DFT_TILE) for k in range(DFT_UNITS)]
    accs = [(jnp.dot(f_ref[u, :], pa_ref[0], preferred_element_type=F32),
             jnp.dot(f_ref[u, :], pb_ref[0], preferred_element_type=F32)) for u in units]
    for k, (u, (acc_a, acc_b)) in enumerate(zip(units, accs)):
        s_a, s_b = (acc_a[:half], acc_a[half:]), (acc_b[:half], acc_b[half:])
        ks = ks_ref[u, :]
        k_diag, k_below, k_above = [(ks[:half, j * w:(j + 1) * w], ks[half:, j * w:(j + 1) * w]) for j in range(3)]
        packed = jnp.logical_and(_iota((half, w), 0) == 0, i == 0) if k == 0 else None
        z_ref[0, 0, u, :] = _spectral_products([(s_a, k_diag), (s_b, k_above)], packed)
        z_ref[0, 1, u, :] = _spectral_products([(s_a, k_below), (s_b, k_diag)], packed)


def _conv_spectrum(fwd, p16, kspec):
    bsz, n, w = p16.shape
    big, h = fwd.shape
    assert n == 2 * h
    rows = DFT_TILE * DFT_UNITS
    return pl.pallas_call(
        _conv_spectrum_kernel,
        grid=(bsz, big // rows),
        in_specs=[pl.BlockSpec((rows, h), lambda b, i: (i, 0)),
                  pl.BlockSpec((1, h, w), lambda b, i: (b, 0, 0)),
                  pl.BlockSpec((1, h, w), lambda b, i: (b, 1, 0)),
                  pl.BlockSpec((rows, 3 * w), lambda b, i: (i, 0))],
        out_specs=pl.BlockSpec((1, 2, rows, w), lambda b, i: (b, 0, i, 0)),
        out_shape=jax.ShapeDtypeStruct((bsz, 2, big, w), BF16),
        compiler_params=_compiler_params(("parallel", "arbitrary")),
        name="hyena_forward_dft",
    )(fwd, p16, p16, kspec)


def _conv_inverse_kernel(g_ref, z_ref, e_ref, o_ref):
    y = jnp.dot(g_ref[...], z_ref[0, 0], preferred_element_type=F32)
    o_ref[0] = (e_ref[0].astype(F32) * y).astype(o_ref.dtype)


def _conv_inverse(inv, z16, e):
    bsz, _, big, w = z16.shape
    h = inv.shape[0]
    tile = DFT_TILE
    per_half = h // tile
    tok = pl.BlockSpec((1, tile, w), lambda b, j, i: (b, j * per_half + i, 0))
    return pl.pallas_call(
        _conv_inverse_kernel,
        grid=(bsz, 2, per_half),
        in_specs=[pl.BlockSpec((tile, big), lambda b, j, i: (i, 0)),
                  pl.BlockSpec((1, 1, big, w), lambda b, j, i: (b, j, 0, 0)),
                  tok],
        out_specs=tok,
        out_shape=jax.ShapeDtypeStruct((bsz, 2 * h, w), BF16),
        compiler_params=_compiler_params(("parallel", "arbitrary", "arbitrary")),
        name="hyena_inverse_dft",
    )(inv, z16, e)


def _even_weight_layout(w_in, gate_b):
    d = w_in.shape[0]
    mw = MLSTM_N_HEADS * LANES
    g0 = 5 * mw
    g1 = g0 + 4 * MLSTM_N_HEADS
    main = jnp.concatenate([w_in[:, :g0], w_in[:, g1:]], axis=1).astype(BF16)
    wg = w_in[:, g0:g1].reshape(d, 2, 2, MLSTM_N_HEADS)
    wg = jnp.transpose(wg, (3, 1, 2, 0)).reshape(MLSTM_N_HEADS, 4, d)
    wg = jnp.concatenate([wg, jnp.zeros_like(wg)], axis=1).reshape(MLSTM_N_HEADS * 8, d).astype(BF16)
    gb = jnp.transpose(gate_b.reshape(2, 2, MLSTM_N_HEADS), (2, 0, 1)).reshape(MLSTM_N_HEADS, 4)
    gb = jnp.concatenate([gb, jnp.zeros_like(gb)], axis=1).reshape(MLSTM_N_HEADS * 8, 1)
    return main, wg, jnp.broadcast_to(gb, (MLSTM_N_HEADS * 8, LANES))


def _rwkv_params(mu, w0, w_up, a0, a_up, k_k, k_a, r_k, ln_w, ln_b):
    width = mu.shape[1]
    n_pairs = width // LANES
    row = lambda x: x.reshape(1, width)

    def cat_dirs(x):
        return jnp.transpose(x.reshape(2, n_pairs, LANES), (1, 0, 2)).reshape(n_pairs, 1, 2 * LANES)

    def block_up(x):
        lora = x.shape[1]
        xp = jnp.transpose(x.reshape(2, lora, n_pairs, LANES), (2, 0, 1, 3))
        z = jnp.zeros_like(xp[:, 0])
        top = jnp.concatenate([xp[:, 0], z], axis=2)
        bot = jnp.concatenate([z, xp[:, 1]], axis=2)
        return jnp.concatenate([top, bot], axis=1).astype(BF16)

    return {"mu": mu, "k_k": row(k_k), "k_a": row(k_a), "r_k": row(r_k), "ln_w": row(ln_w), "ln_b": row(ln_b),
            "w0": cat_dirs(w0), "a0": cat_dirs(a0), "w_up": block_up(w_up), "a_up": block_up(a_up)}


def kernel(x, c, ctx, c_ctx, l0_norm_g, l0_mod_w, l0_mod_b, l0_w_in, l0_w_out, l0_mlstm_conv_w, l0_mlstm_gate_b, l0_mlstm_norm_g, l0_rwkv_mu, l0_rwkv_w0, l0_rwkv_w_up, l0_rwkv_a0, l0_rwkv_a_up, l0_rwkv_k_k, l0_rwkv_k_a, l0_rwkv_r_k, l0_rwkv_ln_w, l0_rwkv_ln_b, hgrn_lower_bounds, l1_norm_g, l1_mod_w, l1_mod_b, l1_w_in, l1_w_out, l1_hgrn_norm_g, l1_hyena_short_w, l1_hyena_short_b, l1_hyena_w1, l1_hyena_b1, l1_hyena_w2, l1_hyena_b2, l1_hyena_w3, l1_hyena_bias, final_norm_g):
    bsz, n_lat, d = x.shape
    n_ctx = ctx.shape[1]

    pad = (-(bsz + 1)) % 8
    cc = jnp.concatenate([c, c_ctx[None], jnp.zeros((pad, d), F32)], axis=0)
    mod0, mod1 = _modulation(cc, l0_mod_w, l0_mod_b, l1_mod_w, l1_mod_b)
    mod0 = mod0[:bsz + 1].reshape(bsz + 1, 3, d)
    mod1 = mod1[:bsz + 1].reshape(bsz + 1, 3, d)

    w_main, w_gate, b_gate = _even_weight_layout(l0_w_in, l0_mlstm_gate_b)
    n0 = w_main.shape[1]
    u0, gt0 = _proj_in(x, ctx, l0_norm_g, mod0, w_main, n0 // 2, w_gate, b_gate)
    gt0 = gt0.reshape(bsz, MLSTM_N_HEADS, 8, n_ctx + n_lat)
    y_m = _mlstm(u0, gt0, l0_mlstm_conv_w, l0_mlstm_norm_g, n_ctx)
    rp = _rwkv_params(l0_rwkv_mu, l0_rwkv_w0, l0_rwkv_w_up, l0_rwkv_a0, l0_rwkv_a_up, l0_rwkv_k_k,
                      l0_rwkv_k_a, l0_rwkv_r_k, l0_rwkv_ln_w, l0_rwkv_ln_b)
    y_r = _rwkv(u0, 5 * MLSTM_N_HEADS * LANES, rp, n_ctx)
    x1, ctx1 = _proj_out(y_m, y_r, x, ctx, mod0, l0_w_out.astype(BF16))

    x1 = x1.reshape(bsz, n_lat // GRID_WIDTH, GRID_WIDTH, d)
    w1 = l1_w_in.astype(BF16)
    (u1,) = _proj_in(x1, ctx1, l1_norm_g, mod1, w1, w1.shape[1] // 2)
    y_g = _hgrn(u1, hgrn_lower_bounds, l1_hgrn_norm_g, n_ctx, layer=1)
    hw = l1_hyena_bias.shape[0]
    hk = _hyena_filters(n_lat, l1_hyena_w1, l1_hyena_b1, l1_hyena_w2, l1_hyena_b2, l1_hyena_w3, hw)
    fwd, inv = _dft_tables(n_lat // 2)
    kspec = _filter_spectrum(fwd, hk, l1_hyena_bias)
    p16, e16 = _hyena_pre(u1, 5 * HGRN_N_HEADS * LANES, l1_hyena_short_w, l1_hyena_short_b, n_ctx)
    z16 = _conv_spectrum(fwd, p16, kspec)
    y_y = _conv_inverse(inv, z16, e16)
    out = _proj_out_final(y_g, y_y, x1, mod1, l1_w_out.astype(BF16), final_norm_g)
    return out.reshape(bsz, n_lat, d)
```

```python
import functools
import math

import jax
import jax.numpy as jnp
import numpy as np
from jax import lax
from jax.experimental import pallas as pl
from jax.experimental.pallas import tpu as pltpu

F32 = jnp.float32
BF16 = jnp.bfloat16

GRID_WIDTH = 64
RMS_EPS = 1e-6
MLSTM_N_HEADS = 4
RWKV_HEAD = 64
RWKV_LN_EPSILON = 64e-5
HGRN_N_HEADS = 4
HYENA_N_BANDS = 16
HYENA_FAST = 0.3
HYENA_SLOW = 1.5
HYENA_TGT = 1e-2
HYENA_SHIFT = 0.05
LOG2_E = 1.0 / math.log(2.0)

LANES = 128
SUBLANES = 8
ROW_GROUP = 16
VMEM_LIMIT = 52 * 1024 * 1024

MIX_CHUNK = 128
RWKV_CHUNK = 64
RWKV_GROUP = 12
MIX_PREP_ROWS = 256
MIX_FINISH_ROWS = 256
RWKV_PREP_ROWS = 256
RWKV_PREP_BLOCKS = 3
RWKV_FINISH_ROWS = 768
RWKV_SCAN_UNROLL = 3
SCAN_UNROLL = 3
HGRN_SCAN_UNROLL = 6
MLSTM_GROUP = 9
HGRN_GROUP = 6
PROJ_ROWS = 768
FINAL_ROWS = 512
DFT_TILE = 512
DFT_UNITS = 2
DFT_SPLIT = 64


def _bdot(a, b):
    return jnp.dot(a.astype(BF16), b.astype(BF16), preferred_element_type=F32)


def _bdot_nt(a, b):
    return lax.dot_general(a.astype(BF16), b.astype(BF16), (((1,), (1,)), ((), ())),
                           preferred_element_type=F32)


def _bdot_tn(a, b):
    return lax.dot_general(a.astype(BF16), b.astype(BF16), (((0,), (0,)), ((), ())),
                           preferred_element_type=F32)


def _split3(x):
    hi = x.astype(BF16)
    r1 = x - hi.astype(F32)
    mid = r1.astype(BF16)
    lo = (r1 - mid.astype(F32)).astype(BF16)
    return hi, mid, lo


def _sel_dot(sel, x):
    hi, mid, lo = _split3(x)
    d = functools.partial(jnp.dot, preferred_element_type=F32)
    return d(sel, hi) + d(sel, mid) + d(sel, lo)


def _dot_sel(x, sel):
    hi = x.astype(BF16)
    mid = (x - hi.astype(F32)).astype(BF16)
    d = functools.partial(jnp.dot, preferred_element_type=F32)
    return d(hi, sel) + d(mid, sel)


def _sigmoid(x):
    return 1.0 / (1.0 + jnp.exp(-x))


def _silu(x):
    return x * _sigmoid(x)


def _iota(shape, dim):
    return lax.broadcasted_iota(jnp.int32, shape, dim)


def _neighbor_rows(ref, t0, rows, n_total, split):
    has_prev = jnp.logical_and(t0 != 0, t0 != split)
    has_next = jnp.logical_and(t0 + rows != split, t0 + rows != n_total)
    g = ROW_GROUP
    before = ref[0, pl.ds(pl.multiple_of(jnp.maximum(t0 - g, 0), g), g), :].astype(F32)
    after = ref[0, pl.ds(pl.multiple_of(jnp.minimum(t0 + rows, n_total - g), g), g), :].astype(F32)
    return jnp.where(has_prev, before[g - 1:g], 0.0), jnp.where(has_next, after[0:1], 0.0)


def _shifted(cur, prev_row, next_row):
    rows = cur.shape[0]
    rid = _iota(cur.shape, 0)
    down = jnp.where(rid == 0, prev_row, pltpu.roll(cur, 1, 0))
    up = jnp.where(rid == rows - 1, next_row, pltpu.roll(cur, rows - 1, 0))
    return down, up


def _chunk_with_neighbors(ref, t0, rows, n_total, split):
    cur = ref[0, pl.ds(t0, rows), :].astype(F32)
    prev_row, next_row = _neighbor_rows(ref, t0, rows, n_total, split)
    down, up = _shifted(cur, prev_row, next_row)
    return cur, down, up


def _scan_chunks(i, n_chunks, n_ctx_chunks):
    fwd = jnp.where(i < n_ctx_chunks, n_chunks - n_ctx_chunks + i, i - n_ctx_chunks)
    return fwd, n_chunks - 1 - i


def _compiler_params(semantics):
    return pltpu.CompilerParams(dimension_semantics=semantics, vmem_limit_bytes=VMEM_LIMIT)


def _mod_kernel(c_ref, w0_ref, b0_ref, w1_ref, b1_ref, o0_ref, o1_ref):
    s = _silu(c_ref[...])
    o0_ref[...] = _bdot(s, w0_ref[...]) + b0_ref[...]
    o1_ref[...] = _bdot(s, w1_ref[...]) + b1_ref[...]


def _modulation(cc, w0, b0, w1, b1):
    rows, d = cc.shape
    n = w0.shape[1]
    tile = d
    grid = (n // tile,)
    wspec = pl.BlockSpec((d, tile), lambda j: (0, j))
    bspec = pl.BlockSpec((1, tile), lambda j: (0, j))
    ospec = pl.BlockSpec((rows, tile), lambda j: (0, j))
    return pl.pallas_call(
        _mod_kernel,
        grid=grid,
        in_specs=[pl.BlockSpec((rows, d), lambda j: (0, 0)), wspec, bspec, wspec, bspec],
        out_specs=[ospec, ospec],
        out_shape=[jax.ShapeDtypeStruct((rows, n), F32)] * 2,
        compiler_params=_compiler_params(("arbitrary",)),
        name="adaln_modulation",
    )(cc, w0, b0.reshape(1, n), w1, b1.reshape(1, n))


def _load_tokens(x_ref):
    if len(x_ref.shape) == 3:
        return x_ref[0]
    return jnp.concatenate([x_ref[0, :, j, :] for j in range(x_ref.shape[2])], axis=0)


def _token_spec(x, rows, index_map):
    if x.ndim == 3:
        return pl.BlockSpec((1, rows, x.shape[2]), lambda *g: (index_map(*g)[0], index_map(*g)[1], 0))
    assert rows % x.shape[1] == 0
    return pl.BlockSpec((1, x.shape[1], rows // x.shape[1], x.shape[3]),
                        lambda *g: (index_map(*g)[0], 0, index_map(*g)[1], 0))


def _token_tile(x_ref, c_ref, last_of_batch, rows):
    n_lat_tail = rows - c_ref.shape[1]
    is_ctx = jnp.logical_and(last_of_batch, _iota((rows, 1), 0) >= n_lat_tail)
    ctx_rows = jnp.concatenate([jnp.zeros((n_lat_tail, c_ref.shape[2]), F32), c_ref[0]], axis=0)
    return jnp.where(is_ctx, ctx_rows, _load_tokens(x_ref)), is_ctx


def _proj_in_kernel(*refs, rows, with_gates):
    if with_gates:
        x_ref, c_ref, g_ref, ml_ref, mc_ref, w_ref, wg_ref, gb_ref, u_ref, gt_ref, h_scr = refs
    else:
        x_ref, c_ref, g_ref, ml_ref, mc_ref, w_ref, u_ref, h_scr = refs
    i = pl.program_id(1)
    n = pl.program_id(2)

    @pl.when(n == 0)
    def _():
        x, is_ctx = _token_tile(x_ref, c_ref, i == pl.num_programs(1) - 1, rows)
        y = x * lax.rsqrt(jnp.mean(x * x, axis=-1, keepdims=True) + RMS_EPS) * g_ref[...]
        ml = ml_ref[0]
        mc = mc_ref[0]
        shift = jnp.where(is_ctx, mc[0:1], ml[0:1])
        scale = jnp.where(is_ctx, mc[1:2], ml[1:2])
        h = (y * (1.0 + scale) + shift).astype(BF16)
        h_scr[...] = h
        if with_gates:
            gt_ref[0] = _bdot_nt(wg_ref[...], h) + gb_ref[:, 0:1]

    u_ref[0] = jnp.dot(h_scr[...], w_ref[...], preferred_element_type=F32).astype(u_ref.dtype)


def _proj_in(x, ctx, norm_g, mod3, w16, n_tile, gate_w=None, gate_b=None):
    bsz, d = x.shape[0], x.shape[-1]
    n_lat = math.prod(x.shape[1:-1])
    n_ctx = ctx.shape[1]
    s = n_lat + n_ctx
    n = w16.shape[1]
    rows = PROJ_ROWS
    assert s % rows == 0 and (n_lat % rows) + n_ctx == rows
    grid = (bsz, s // rows, n // n_tile)
    ctx_row = mod3.shape[0] - 1
    with_gates = gate_w is not None
    in_specs = [
        _token_spec(x, rows, lambda b, i, j: (b, i)),
        pl.BlockSpec((1, n_ctx, d), lambda b, i, j: (b, 0, 0)),
        pl.BlockSpec((1, d), lambda b, i, j: (0, 0)),
        pl.BlockSpec((1, 3, d), lambda b, i, j: (b, 0, 0)),
        pl.BlockSpec((1, 3, d), lambda b, i, j: (ctx_row, 0, 0)),
        pl.BlockSpec((d, n_tile), lambda b, i, j: (0, j)),
    ]
    args = [x, ctx, norm_g.reshape(1, d), mod3, mod3, w16]
    out_specs = [pl.BlockSpec((1, rows, n_tile), lambda b, i, j: (b, i, j))]
    out_shape = [jax.ShapeDtypeStruct((bsz, s, n), BF16)]
    if with_gates:
        ng = gate_w.shape[0]
        in_specs += [pl.BlockSpec((ng, d), lambda b, i, j: (0, 0)),
                     pl.BlockSpec((ng, LANES), lambda b, i, j: (0, 0))]
        args += [gate_w, gate_b]
        out_specs.append(pl.BlockSpec((1, ng, rows), lambda b, i, j: (b, 0, i)))
        out_shape.append(jax.ShapeDtypeStruct((bsz, ng, s), F32))
    return pl.pallas_call(
        functools.partial(_proj_in_kernel, rows=rows, with_gates=with_gates),
        grid=grid,
        in_specs=in_specs,
        out_specs=out_specs,
        out_shape=out_shape,
        scratch_shapes=[pltpu.VMEM((rows, d), BF16)],
        compiler_params=_compiler_params(("parallel", "arbitrary", "arbitrary")),
        name="norm_mod_proj_in",
    )(*args)


def _proj_out_kernel(ya_ref, yb_ref, x_ref, c_ref, ml_ref, mc_ref, w_ref, ox_ref, oc_ref, *, rows):
    i = pl.program_id(1)
    half = ya_ref.shape[2]
    y = _bdot(ya_ref[0], w_ref[0:half, :]) + _bdot(yb_ref[0], w_ref[half:, :])
    x, is_ctx = _token_tile(x_ref, c_ref, i == pl.num_programs(1) - 1, rows)
    x = x + jnp.where(is_ctx, mc_ref[0][2:3], ml_ref[0][2:3]) * y
    ox_ref[0] = x

    @pl.when(i == pl.num_programs(1) - 1)
    def _():
        oc_ref[0] = x[rows - c_ref.shape[1]:]


def _proj_out(ya, yb, x, ctx, mod3, w16):
    bsz, n_lat, d = x.shape
    n_ctx = ctx.shape[1]
    s = n_lat + n_ctx
    half = ya.shape[2]
    rows = PROJ_ROWS
    assert s % rows == 0 and (n_lat % rows) + n_ctx == rows
    ctx_row = mod3.shape[0] - 1
    tok = lambda w: pl.BlockSpec((1, rows, w), lambda b, i: (b, i, 0))
    seg = pl.BlockSpec((1, n_ctx, d), lambda b, i: (b, 0, 0))
    return pl.pallas_call(
        functools.partial(_proj_out_kernel, rows=rows),
        grid=(bsz, s // rows),
        in_specs=[tok(half), tok(half), tok(d), seg,
                  pl.BlockSpec((1, 3, d), lambda b, i: (b, 0, 0)),
                  pl.BlockSpec((1, 3, d), lambda b, i: (ctx_row, 0, 0)),
                  pl.BlockSpec((2 * half, d), lambda b, i: (0, 0))],
        out_specs=[tok(d), seg],
        out_shape=[jax.ShapeDtypeStruct((bsz, n_lat, d), F32), jax.ShapeDtypeStruct((bsz, n_ctx, d), F32)],
        compiler_params=_compiler_params(("parallel", "arbitrary")),
        name="proj_out_residual",
    )(ya, yb, x, ctx, mod3, mod3, w16)


def _proj_out_final_kernel(ya_ref, yb_ref, x_ref, ml_ref, w_ref, fg_ref, o_ref):
    half = ya_ref.shape[2]
    y = _bdot(ya_ref[0], w_ref[0:half, :]) + _bdot(yb_ref[0], w_ref[half:, :])
    x = _load_tokens(x_ref) + ml_ref[0][2:3] * y
    out = x * lax.rsqrt(jnp.mean(x * x, axis=-1, keepdims=True) + RMS_EPS) * fg_ref[...]
    grid_rows = o_ref.shape[1]
    for j in range(o_ref.shape[2]):
        o_ref[0, :, j, :] = out[j * grid_rows:(j + 1) * grid_rows]


def _proj_out_final(ya, yb, x, mod3, w16, final_g):
    bsz, gr, gc, d = x.shape
    n_lat = gr * gc
    half = ya.shape[2]
    rows = FINAL_ROWS
    assert n_lat % rows == 0
    tok = lambda w: pl.BlockSpec((1, rows, w), lambda b, i: (b, i, 0))
    raster = _token_spec(x, rows, lambda b, i: (b, i))
    return pl.pallas_call(
        _proj_out_final_kernel,
        grid=(bsz, n_lat // rows),
        in_specs=[tok(half), tok(half), raster,
                  pl.BlockSpec((1, 3, d), lambda b, i: (b, 0, 0)),
                  pl.BlockSpec((2 * half, d), lambda b, i: (0, 0)),
                  pl.BlockSpec((1, d), lambda b, i: (0, 0))],
        out_specs=raster,
        out_shape=jax.ShapeDtypeStruct(x.shape, F32),
        compiler_params=_compiler_params(("parallel", "arbitrary")),
        name="proj_out_final_norm",
    )(ya, yb, x, mod3, w16, final_g.reshape(1, d))


def _mlstm_chunk_operators(chunks, causal):
    t = chunks[0][0].shape[0]
    lane = _iota((SUBLANES, t), 1)
    row_id = _iota((SUBLANES, t), 0)
    log_fs = [jnp.minimum(c[3], 0.0) - jnp.log1p(jnp.exp(-jnp.abs(c[3]))) for c in chunks]
    cum_f, cum_b = list(log_fs), list(log_fs)
    sh = 1
    while sh < t:
        cum_f = [x + jnp.where(lane >= sh, pltpu.roll(x, sh, 1), 0.0) for x in cum_f]
        cum_b = [x + jnp.where(lane < t - sh, pltpu.roll(x, t - sh, 1), 0.0) for x in cum_b]
        sh *= 2
    pad = jnp.zeros((t - SUBLANES, t), F32)
    tiles = [jnp.concatenate([jnp.where(row_id % 2 == 0, c[3], jnp.where(row_id == 1, f, b)), pad], axis=0)
             for c, f, b in zip(chunks, cum_f, cum_b)]
    cols = [x.T for x in tiles]
    k_ts = [c[1].T for c in chunks]
    problems = []
    for c, f, b, col, k_t in zip(chunks, cum_f, cum_b, cols, k_ts):
        for d in range(2):
            b_row = (f, b)[d][2 * d + 1:2 * d + 2]
            problems.append(dict(q=c[0], k=c[1], k_t=k_t, v_ext=c[2], d=d, ig_row=c[3][2 * d:2 * d + 1], b_row=b_row,
                                 b_col=col[:, 2 * d + 1:2 * d + 2]))
    logws = [jnp.where(causal[p["d"]], p["b_col"] + (p["ig_row"] - p["b_row"]), -jnp.inf) for p in problems]
    mus = [jnp.max(x, axis=-1, keepdims=True) for x in logws]
    ws = [jnp.exp(x - mu) for x, mu in zip(logws, mus)]
    lasts = [0 if p["d"] == 1 else t - 1 for p in problems]
    b_lasts = [p["b_col"][i:i + 1] for p, i in zip(problems, lasts)]
    gammas = [mu[i:i + 1] for mu, i in zip(mus, lasts)]
    gk_ts = [p["k_t"] * jnp.exp(bl - p["b_row"] + p["ig_row"] - gm) for p, bl, gm in zip(problems, b_lasts, gammas)]
    qks = [_bdot_nt(p["q"], p["k"]) * w for p, w in zip(problems, ws)]
    intras = [_bdot(qk, p["v_ext"]) for qk, p in zip(qks, problems)]
    kvs = [_bdot(gk_t, p["v_ext"]) for gk_t, p in zip(gk_ts, problems)]
    dh = chunks[0][0].shape[1]
    return [(intra, kv, jnp.broadcast_to(mu - p["b_col"], (t, dh)), jnp.broadcast_to(mu, (t, dh)), bl, gm)
            for intra, kv, p, mu, bl, gm in zip(intras, kvs, problems, mus, b_lasts, gammas)]


def _mlstm_kernel(q_ref, k_ref, v_ref, o_ref, z_ref, gt_ref, cwq_ref, cwk_ref, ng_ref, out_ref,
                  qa_scr, ka_scr, h_scr, intra_scr, kv_scr, delta_scr, mu_scr, tail_scr, *, n_ctx):
    s = q_ref.shape[1]
    dh = q_ref.shape[2]
    t = MIX_CHUNK
    n_chunks = s // t
    n_ctx_chunks = n_ctx // t
    k_scale = dh ** -0.5

    p_rows = MIX_PREP_ROWS

    def prep(j, carry):
        t0 = pl.multiple_of(j * p_rows, p_rows)
        for src, cw, dst, scale in ((q_ref, cwq_ref, qa_scr, 1.0), (k_ref, cwk_ref, ka_scr, k_scale)):
            cur, down, up = _chunk_with_neighbors(src, t0, p_rows, s, s - n_ctx)
            conv = down * cw[0:1, :] + cur * cw[1:2, :] + up * cw[2:3, :]
            dst[pl.ds(t0, p_rows), :] = _silu(conv) * scale
        return carry

    lax.fori_loop(0, s // p_rows, prep, 0)

    ones_col = jnp.ones((t, dh), F32)
    causal = [_iota((t, t), 1) <= _iota((t, t), 0), _iota((t, t), 1) >= _iota((t, t), 0)]

    def operators(gi, carry):
        chunks, where = [], []
        for kk in range(MLSTM_GROUP):
            chunk = gi * MLSTM_GROUP + kk
            sl = pl.ds(pl.multiple_of(chunk * t, t), t)
            v_ext = jnp.concatenate([v_ref[0, sl, :].astype(F32), ones_col], axis=1)
            chunks.append((qa_scr[sl, :], ka_scr[sl, :], v_ext, gt_ref[0, 0, :, sl]))
            where += [(0, chunk, sl), (1, chunk, sl)]
        for (d, chunk, sl), (intra, kv, delta, mu, b_last, gamma) in zip(where, _mlstm_chunk_operators(chunks, causal)):
            intra_scr[d, sl, :] = intra
            kv_scr[d, chunk] = kv
            delta_scr[d, sl, :] = delta
            mu_scr[d, sl, :] = mu
            tail_scr[d, chunk] = jnp.concatenate([jnp.broadcast_to(b_last, (1, dh)), jnp.broadcast_to(gamma, (1, dh))],
                                                 axis=0)
        return carry

    lax.fori_loop(0, n_chunks // MLSTM_GROUP, operators, 0)

    def scan(trip, carry):
        steps = []
        for k in range(SCAN_UNROLL):
            chunks = _scan_chunks(trip * SCAN_UNROLL + k, n_chunks, n_ctx_chunks)
            new = []
            for d, (chunk, (c_ext, m)) in enumerate(zip(chunks, carry)):
                steps.append((d, chunk, pl.ds(pl.multiple_of(chunk * t, t), t), c_ext, m))
                tail = tail_scr[d, chunk]
                b_last, gamma = tail[0:1, 0:1], tail[1:2, 0:1]
                m_new = jnp.maximum(b_last + m, gamma)
                new.append((jnp.exp(b_last + m - m_new) * c_ext + jnp.exp(gamma - m_new) * kv_scr[d, chunk], m_new))
            carry = tuple(new)
        inters = [_bdot(qa_scr[sl, :], c_ext) for _, _, sl, c_ext, _ in steps]
        for (d, chunk, sl, _, m), inter in zip(steps, inters):
            z = delta_scr[d, sl, :] - m
            s_inter = jnp.exp(-jnp.maximum(z, 0.0))
            s_intra = jnp.exp(jnp.minimum(z, 0.0))
            floor = jnp.exp(jnp.minimum(z, 0.0) - mu_scr[d, sl, :])
            intra = intra_scr[d, sl, :]
            num = s_inter * inter[:, :dh] + s_intra * intra[:, :dh]
            den = s_inter * inter[:, dh:] + s_intra * intra[:, dh:]
            h_scr[d, sl, :] = num / jnp.maximum(jnp.abs(den), floor)
        return carry

    zero = (jnp.zeros((dh, 2 * dh), F32), jnp.zeros((1, 1), F32))
    lax.fori_loop(0, n_chunks // SCAN_UNROLL, scan, (zero, zero))

    f_rows = MIX_FINISH_ROWS

    def finish(j, carry):
        sl = pl.ds(pl.multiple_of(j * f_rows, f_rows), f_rows)
        h = h_scr[0, sl, :] + h_scr[1, sl, :]
        y = h * lax.rsqrt(jnp.mean(h * h, axis=-1, keepdims=True) + RMS_EPS) * ng_ref[...]
        gated = y * _sigmoid(o_ref[0, sl, :].astype(F32)) * _silu(z_ref[0, sl, :].astype(F32))
        out_ref[0, sl, :] = gated.astype(out_ref.dtype)
        return carry

    lax.fori_loop(0, s // f_rows, finish, 0)


def _mlstm(u, gt, conv_w, norm_g, n_ctx):
    bsz, s, _ = u.shape
    nh = MLSTM_N_HEADS
    dh = LANES
    width = nh * dh
    n_chunks = s // MIX_CHUNK
    assert n_chunks % MLSTM_GROUP == 0 and n_chunks % SCAN_UNROLL == 0 and n_ctx % MIX_CHUNK == 0
    col = lambda k: pl.BlockSpec((1, s, dh), lambda b, h, k=k: (b, 0, k * nh + h))
    par = lambda k: pl.BlockSpec((3, dh), lambda b, h, k=k: (0, k * nh + h))
    return pl.pallas_call(
        functools.partial(_mlstm_kernel, n_ctx=n_ctx),
        grid=(bsz, nh),
        in_specs=[col(0), col(1), col(2), col(3), col(4),
                  pl.BlockSpec((1, 1, SUBLANES, s), lambda b, h: (b, h, 0, 0)),
                  par(0), par(1),
                  pl.BlockSpec((1, dh), lambda b, h: (0, h))],
        out_specs=pl.BlockSpec((1, s, dh), lambda b, h: (b, 0, h)),
        out_shape=jax.ShapeDtypeStruct((bsz, s, width), BF16),
        scratch_shapes=[pltpu.VMEM((s, dh), F32), pltpu.VMEM((s, dh), F32), pltpu.VMEM((2, s, dh), F32),
                        pltpu.VMEM((2, s, 2 * dh), F32), pltpu.VMEM((2, n_chunks, dh, 2 * dh), F32),
                        pltpu.VMEM((2, s, dh), F32), pltpu.VMEM((2, s, dh), F32),
                        pltpu.VMEM((2, n_chunks, 2, dh), F32)],
        compiler_params=_compiler_params(("parallel", "arbitrary")),
        name="mlstm_mixer",
    )(u, u, u, u, u, gt, conv_w, conv_w, norm_g.reshape(1, width))


def _head_stack(x, lane_lo):
    return jnp.concatenate([jnp.where(lane_lo, x, 0.0), jnp.where(lane_lo, 0.0, x)], axis=0)


def _half_rows(x, c, upper):
    start = c if upper else 0
    return jnp.concatenate([x[r + start:r + start + c] for r in range(0, x.shape[0], 2 * c)], axis=0)


def _merge_rows(other, part, c, upper):
    pieces = []
    for k in range(part.shape[0] // c):
        pair = (other[k * c:(k + 1) * c], part[k * c:(k + 1) * c])
        pieces += pair if upper else pair[::-1]
    return jnp.concatenate(pieces, axis=0)


def _spread_rows(part, c, upper):
    return _merge_rows(jnp.zeros_like(part), part, c, upper)


def _rwkv_chunk_operators(problems, consts, eye, lane_lo):
    t, w = problems[0][0].shape
    n2 = 2 * t
    stack = lambda x: _head_stack(x, lane_lo)
    zeros = jnp.zeros((n2, w), F32)
    dirs = [p[6] for p in problems]
    rid = _iota((t, w), 0)
    cums = [p[3] for p in problems]
    sh = 1
    while sh < t:
        cums = [x + (jnp.where(rid < t - sh, pltpu.roll(x, t - sh, 0), 0.0) if d == 1 else
                     jnp.where(rid >= sh, pltpu.roll(x, sh, 0), 0.0)) for x, d in zip(cums, dirs)]
        sh *= 2
    pre = []
    for (r, v, kk, lw, ka, kt, d), cum in zip(problems, cums):
        last = 0 if d == 1 else t - 1
        cum_end = cum[last:last + 1]
        e_inv = jnp.exp(-cum)
        e_end = jnp.exp(cum_end - cum)
        a_s = stack(-kk * jnp.exp(cum - lw))
        r_s = stack(r * jnp.exp(cum))
        pre.append(dict(a_s=a_s, r_s=r_s, vs=stack(v), g=jnp.exp(cum_end),
                        ar=jnp.concatenate([a_s, r_s], axis=0),
                        bk=jnp.concatenate([stack(ka * e_inv), stack(kt * e_inv)], axis=0),
                        bk_end=jnp.concatenate([stack(ka * e_end), stack(kt * e_end)], axis=0)))
    m_alls = [_bdot_nt(q["ar"], q["bk"]) for q in pre]
    m_abs = [jnp.where(consts[d]["strict"], m[:n2, :n2], 0.0) for m, d in zip(m_alls, dirs)]
    m_aks = [jnp.where(consts[d]["strict"], m[:n2, n2:], 0.0) for m, d in zip(m_alls, dirs)]
    m_lows = [jnp.where(consts[d]["incl2"], m[n2:, :], 0.0) for m, d in zip(m_alls, dirs)]
    invs = [eye + jnp.where(consts[d]["merges"][0][1], m, 0.0) for m, d in zip(m_abs, dirs)]
    for level in range(1, len(consts[0]["merges"])):
        c = consts[0]["merges"][level][0]
        if c < SUBLANES:
            inner = [_bdot(jnp.where(consts[d]["merges"][level][1], m, 0.0), x) for m, x, d in zip(m_abs, invs, dirs)]
            invs = [x + _bdot(x, y) for x, y in zip(invs, inner)]
        else:
            ups = [d == 0 for d in dirs]
            c_rows = [jnp.where(consts[d]["merges"][level][1], _half_rows(m, c, up), 0.0) for m, d, up in zip(m_abs, dirs, ups)]
            inner = [_bdot(cr, x) for cr, x in zip(c_rows, invs)]
            x_rows = [_half_rows(x, c, up) for x, up in zip(invs, ups)]
            upd = [xr + _bdot(xr, _spread_rows(y, c, up)) for xr, y, up in zip(x_rows, inner, ups)]
            invs = [_merge_rows(_half_rows(x, c, not up), u, c, up) for x, u, up in zip(invs, upd, ups)]
    mv = [_bdot(m, q["vs"]) for m, q in zip(m_aks, pre)]
    solved = [_bdot(x, jnp.concatenate([q["a_s"], y], axis=1)) for x, q, y in zip(invs, pre, mv)]
    zms = [jnp.concatenate([sv, jnp.concatenate([zeros, q["vs"]], axis=1)], axis=0) for sv, q in zip(solved, pre)]
    ry1s = [jnp.concatenate([q["r_s"], zeros], axis=1) + _bdot(m, z) for q, m, z in zip(pre, m_lows, zms)]
    pqs = [_bdot_tn(z, q["bk_end"]) for z, q in zip(zms, pre)]
    out = []
    for ry1, pq, q in zip(ry1s, pqs, pre):
        folded = ry1[:t] + ry1[t:]
        out.append((folded[:, :w], folded[:, w:], pq[:w], pq[w:], q["g"]))
    return out


def _rwkv_kernel(rr_ref, rk_ref, rv_ref, rz_ref, wd_ref, ad_ref, mu_ref, kk_ref, ka_ref, rkk_ref,
                 lnw_ref, lnb_ref, w0_ref, a0_ref, wup_ref, aup_ref, out_ref,
                 r_scr, v_scr, kk_scr, lw_scr, ka_scr, kt_scr, bonus_scr, y_scr, ry_scr, pp_scr, qq_scr, qt_scr, g_scr,
                 *, n_ctx):
    s = rr_ref.shape[1]
    w = rr_ref.shape[2]
    p_rows = RWKV_PREP_ROWS
    t = RWKV_CHUNK
    n_chunks = s // t
    n_ctx_chunks = n_ctx // t
    head_sum = ((_iota((w, w), 0) // RWKV_HEAD) == (_iota((w, w), 1) // RWKV_HEAD)).astype(BF16)
    inv_head = 1.0 / RWKV_HEAD

    def prep(j, carry):
        sls = [pl.ds(pl.multiple_of((j * RWKV_PREP_BLOCKS + k) * p_rows, p_rows), p_rows) for k in range(RWKV_PREP_BLOCKS)]
        mixed = []
        for sl in sls:
            shifted = []
            for idx, src in enumerate((rr_ref, rk_ref, rv_ref)):
                cur, down, up = _chunk_with_neighbors(src, sl.start, p_rows, s, s - n_ctx)
                shifted.append(cur + mu_ref[idx:idx + 1, :] * (0.5 * (down + up) - cur))
            mixed.append(shifted)
        kks = [kr * kk_ref[...] for _, kr, _ in mixed]
        norms = [jnp.sqrt(_dot_sel(kk * kk, head_sum)) for kk in kks]
        kks = [kk / jnp.maximum(norm, 1e-12) for kk, norm in zip(kks, norms)]
        w_raws = [_bdot(jnp.tanh(wd_ref[0, sl, :].astype(F32)), wup_ref[0]) + w0_ref[0] for sl in sls]
        a_raws = [_bdot(ad_ref[0, sl, :], aup_ref[0]) + a0_ref[0] for sl in sls]
        kt_sums = []
        for sl, (r, kr, v), kk, w_raw, a_raw in zip(sls, mixed, kks, w_raws, a_raws):
            a = _sigmoid(a_raw)
            lw = -math.exp(-0.5) * _sigmoid(w_raw)
            kt_sum = jnp.zeros_like(kr)
            for d in range(2):
                a_d = a[:, d * w:(d + 1) * w]
                kt_d = kr * (1.0 + (a_d - 1.0) * ka_ref[...])
                kt_sum = kt_sum + kt_d
                lw_scr[d, sl, :] = lw[:, d * w:(d + 1) * w]
                ka_scr[d, sl, :] = kk * a_d
                kt_scr[d, sl, :] = kt_d
            kt_sums.append(kt_sum)
            r_scr[sl, :] = r
            v_scr[sl, :] = v
            kk_scr[sl, :] = kk
        coefs = [_dot_sel(r * kt_sum * rkk_ref[...], head_sum) for (r, _, _), kt_sum in zip(mixed, kt_sums)]
        for sl, (_, _, v), coef in zip(sls, mixed, coefs):
            bonus_scr[sl, :] = coef * v
        return carry

    lax.fori_loop(0, s // (p_rows * RWKV_PREP_BLOCKS), prep, 0)

    n2 = 2 * t
    r_i = _iota((n2, n2), 0)
    c_i = _iota((n2, n2), 1)
    same = (r_i // t) == (c_i // t)
    rt = r_i % t
    ct = c_i % t
    eye = (r_i == c_i).astype(F32)
    lane_lo = _iota((t, w), 1) < RWKV_HEAD
    consts = []
    for reverse in (False, True):
        strict = jnp.logical_and(same, (ct > rt) if reverse else (ct < rt))
        incl = jnp.logical_and(same, (ct >= rt) if reverse else (ct <= rt))
        merges = []
        c = 1
        while c < t:
            hi_r = (r_i % (2 * c)) >= c
            hi_c = (c_i % (2 * c)) >= c
            cross = jnp.logical_and(hi_c, jnp.logical_not(hi_r)) if reverse else jnp.logical_and(hi_r, jnp.logical_not(hi_c))
            mask = jnp.logical_and((r_i // (2 * c)) == (c_i // (2 * c)), cross)
            merges.append((c, _half_rows(mask, c, not reverse)) if c >= SUBLANES else (c, mask))
            c *= 2
        consts.append(dict(strict=strict, incl2=jnp.concatenate([incl, incl], axis=1), merges=merges))

    def operators(gi, carry):
        problems, where = [], []
        for k in range(RWKV_GROUP):
            chunk = gi * RWKV_GROUP + k
            sl = pl.ds(pl.multiple_of(chunk * t, t), t)
            r, v, kk = r_scr[sl, :], v_scr[sl, :], kk_scr[sl, :]
            for d in range(2):
                problems.append((r, v, kk, lw_scr[d, sl, :], ka_scr[d, sl, :], kt_scr[d, sl, :], d))
                where.append((d, chunk, sl))
        ops = _rwkv_chunk_operators(problems, consts, eye, lane_lo)
        for (d, chunk, sl), (ry, y1, pt, qt, g) in zip(where, ops):
            ry_scr[d, sl, :] = ry.astype(BF16)
            y_scr[d, sl, :] = y1
            qt_scr[d, chunk] = qt
            g_scr[d, chunk] = g
        pairs = []
        for k in range(0, RWKV_GROUP, 2):
            for d in range(2):
                first, second = (ops[2 * k + d], ops[2 * (k + 1) + d]) if d == 0 else (ops[2 * (k + 1) + d], ops[2 * k + d])
                pairs.append((d, (gi * RWKV_GROUP + k) // 2, first, second))
        prods = [_bdot(jnp.concatenate([eye * a[4] + a[2], a[3]], axis=0), b[2]) for _, _, a, b in pairs]
        for (d, pair, a, b), prod in zip(pairs, prods):
            pp_scr[d, pair] = jnp.concatenate([a[2], prod[:w] + a[2] * b[4]], axis=1).astype(BF16)
            qq_scr[d, pair] = a[3] * b[4] + prod[w:] + b[3]
        return carry

    lax.fori_loop(0, n_chunks // RWKV_GROUP, operators, 0)

    def scan(trip, carry):
        outs = []
        for k in range(RWKV_SCAN_UNROLL):
            step = trip * RWKV_SCAN_UNROLL + k
            first = _scan_chunks(2 * step, n_chunks, n_ctx_chunks)
            second = _scan_chunks(2 * step + 1, n_chunks, n_ctx_chunks)
            pair = [jnp.minimum(a, b) // 2 for a, b in zip(first, second)]
            moved = [_bdot(ht, pp_scr[d, p]) for d, (p, ht) in enumerate(zip(pair, carry))]
            mids = [ht * g_scr[d, a] + mv[:, :w] + qt_scr[d, a] for d, (a, ht, mv) in enumerate(zip(first, carry, moved))]
            for d, (a, b, ht, mid) in enumerate(zip(first, second, carry, mids)):
                sl_a = pl.ds(pl.multiple_of(a * t, t), t)
                sl_b = pl.ds(pl.multiple_of(b * t, t), t)
                outs.append((d, sl_a, _bdot_nt(ry_scr[d, sl_a, :], ht)))
                outs.append((d, sl_b, _bdot_nt(ry_scr[d, sl_b, :], mid)))
            carry = tuple(ht * (g_scr[d, a] * g_scr[d, b]) + mv[:, w:] + qq_scr[d, p]
                          for d, (a, b, p, ht, mv) in enumerate(zip(first, second, pair, carry, moved)))
        for d, sl, y in outs:
            y_scr[d, sl, :] = y + y_scr[d, sl, :]
        return carry

    zero_state = jnp.zeros((w, w), F32)
    lax.fori_loop(0, n_chunks // (2 * RWKV_SCAN_UNROLL), scan, (zero_state, zero_state))

    f_rows = RWKV_FINISH_ROWS

    def finish(j, carry):
        sl = pl.ds(pl.multiple_of(j * f_rows, f_rows), f_rows)
        y = y_scr[0, sl, :] + y_scr[1, sl, :] + bonus_scr[sl, :]
        mu = _dot_sel(y, head_sum) * inv_head
        yc = y - mu
        var = _dot_sel(yc * yc, head_sum) * inv_head
        yn = yc * lax.rsqrt(var + RWKV_LN_EPSILON) * lnw_ref[...] + lnb_ref[...]
        out_ref[0, sl, :] = (yn * _silu(rz_ref[0, sl, :].astype(F32))).astype(out_ref.dtype)
        return carry

    lax.fori_loop(0, s // f_rows, finish, 0)


def _rwkv(u, col0, p, n_ctx):
    bsz, s, _ = u.shape
    w = LANES
    width = p["mu"].shape[1]
    n_pairs = width // w
    base = col0 // w
    col = lambda k: pl.BlockSpec((1, s, w), lambda b, h, k=k: (b, 0, base + k * n_pairs + h))
    lora = lambda k: pl.BlockSpec((1, s, w), lambda b, h, k=k: (b, 0, base + 4 * n_pairs + k))
    vec = lambda rows: pl.BlockSpec((rows, w), lambda b, h: (0, h))
    cat = pl.BlockSpec((1, 1, 2 * w), lambda b, h: (h, 0, 0))
    up = pl.BlockSpec((1, w, 2 * w), lambda b, h: (h, 0, 0))
    seq = pltpu.VMEM((s, w), F32)
    seq2 = pltpu.VMEM((2, s, w), F32)
    n_chunks = s // RWKV_CHUNK
    assert n_chunks % RWKV_GROUP == 0 and s % (RWKV_PREP_ROWS * RWKV_PREP_BLOCKS) == 0 and n_ctx % RWKV_PREP_ROWS == 0
    assert s % RWKV_FINISH_ROWS == 0 and n_chunks % (2 * RWKV_SCAN_UNROLL) == 0
    assert RWKV_GROUP % 2 == 0 and (n_ctx // RWKV_CHUNK) % 2 == 0
    operators = [pltpu.VMEM((2, s, w), BF16), pltpu.VMEM((2, n_chunks // 2, w, 2 * w), BF16),
                 pltpu.VMEM((2, n_chunks // 2, w, w), F32),
                 pltpu.VMEM((2, n_chunks, w, w), F32), pltpu.VMEM((2, n_chunks, 1, w), F32)]
    return pl.pallas_call(
        functools.partial(_rwkv_kernel, n_ctx=n_ctx),
        grid=(bsz, n_pairs),
        in_specs=[col(0), col(1), col(2), col(3), lora(0), lora(1),
                  vec(3), vec(1), vec(1), vec(1), vec(1), vec(1), cat, cat, up, up],
        out_specs=pl.BlockSpec((1, s, w), lambda b, h: (b, 0, h)),
        out_shape=jax.ShapeDtypeStruct((bsz, s, width), BF16),
        scratch_shapes=[seq, seq, seq, seq2, seq2, seq2, seq, seq2] + operators,
        compiler_params=_compiler_params(("parallel", "arbitrary")),
        name="rwkv7_mixer",
    )(u, u, u, u, u, u, p["mu"], p["k_k"], p["k_a"], p["r_k"], p["ln_w"], p["ln_b"],
      p["w0"], p["a0"], p["w_up"], p["a_up"])


def _hgrn_level_masks(t, w):
    rid = _iota((t, w), 0)
    r_i = _iota((t, t), 0)
    c_i = _iota((t, t), 1)
    levels = []
    c = 1
    while c < t:
        same_block = (r_i // (2 * c)) == (c_i // (2 * c))
        up_r = (r_i % (2 * c)) >= c
        up_c = (c_i % (2 * c)) >= c
        pair = [jnp.logical_and(same_block, jnp.logical_and(up_r, jnp.logical_not(up_c))),
                jnp.logical_and(same_block, jnp.logical_and(up_c, jnp.logical_not(up_r)))]
        levels.append((c, (rid % (2 * c)) >= c, pair))
        c *= 2
    return levels


def _hgrn_chunk_operators(problems, lb, tris, levels):
    t, w = problems[0][0].shape
    r_i = _iota((t, t), 0)
    c_i = _iota((t, t), 1)
    zero_row = jnp.zeros((1, w), F32)
    dirs = [p[3] for p in problems]
    lgs, ks = [], []
    for q, v, ff, d in problems:
        e = jnp.exp(-jnp.abs(ff))
        big = 1.0 / (1.0 + e)
        small = e / (1.0 + e)
        pos = ff >= 0.0
        lgs.append(jnp.log(lb + (1.0 - lb) * jnp.where(pos, big, small)) * LOG2_E)
        ks.append((1.0 - lb) * jnp.where(pos, small, big))
    bs = [_sel_dot(tris[d], lg) for lg, d in zip(lgs, dirs)]
    befores = [_shifted(b, zero_row, zero_row)[1 if d == 1 else 0] for b, d in zip(bs, dirs)]
    edges = list(bs)
    accs = [jnp.where(r_i == c_i, jnp.sum(p[0] * k, axis=-1, keepdims=True), 0.0) for p, k in zip(problems, ks)]
    for c, upper, pair in levels:
        qts = [p[0] * jnp.exp2(b - before) for p, b, before in zip(problems, bs, befores)]
        kts = [k * jnp.exp2(edge - b) for k, b, edge in zip(ks, bs, edges)]
        prods = [_bdot_nt(qt, kt) for qt, kt in zip(qts, kts)]
        accs = [jnp.where(pair[d], pr, a) for a, pr, d in zip(accs, prods, dirs)]
        for i, d in enumerate(dirs):
            if d == 1:
                befores[i] = jnp.where(upper, befores[i], pltpu.roll(befores[i], t - c, 0))
                edges[i] = jnp.where(upper, pltpu.roll(edges[i], c, 0), edges[i])
            else:
                befores[i] = jnp.where(upper, pltpu.roll(befores[i], c, 0), befores[i])
                edges[i] = jnp.where(upper, edges[i], pltpu.roll(edges[i], t - c, 0))
    o_intras = [_bdot(a, p[1]) for a, p in zip(accs, problems)]
    b_ends = [b[(0 if d == 1 else t - 1):(1 if d == 1 else t)] for b, d in zip(bs, dirs)]
    kvs = [_bdot_tn(p[1], k * jnp.exp2(be - b)) for p, k, b, be in zip(problems, ks, bs, b_ends)]
    return [(p[0] * jnp.exp2(b), oi, kv, jnp.exp2(be)) for p, b, oi, kv, be in zip(problems, bs, o_intras, kvs, b_ends)]


def _hgrn_kernel(q_ref, i_ref, ff_ref, fb_ref, z_ref, lb_ref, ng_ref, out_ref, o_scr, qe_scr, kv_scr, g_scr,
                 *, n_ctx, layer):
    s = q_ref.shape[1]
    dh = q_ref.shape[2]
    t = MIX_CHUNK
    n_chunks = s // t
    n_ctx_chunks = n_ctx // t
    lbs = lb_ref[...]
    ex = jnp.exp(lbs - jnp.max(lbs, axis=0, keepdims=True))
    probs = ex / jnp.sum(ex, axis=0, keepdims=True)
    csum = probs[0:1]
    for l in range(1, layer + 1):
        csum = csum + probs[l:l + 1]
    lb = csum - probs[0:1]
    tri_r = _iota((t, t), 0)
    tri_c = _iota((t, t), 1)
    tris = [(tri_c <= tri_r).astype(BF16), (tri_c >= tri_r).astype(BF16)]
    levels = _hgrn_level_masks(t, dh)

    def operators(gi, carry):
        problems, where = [], []
        for kk in range(HGRN_GROUP):
            chunk = gi * HGRN_GROUP + kk
            sl = pl.ds(pl.multiple_of(chunk * t, t), t)
            q, v = q_ref[0, sl, :].astype(F32), i_ref[0, sl, :].astype(F32)
            for d, f_ref in enumerate((ff_ref, fb_ref)):
                problems.append((q, v, f_ref[0, sl, :].astype(F32), d))
                where.append((d, chunk, sl))
        for (d, chunk, sl), (qe, o_intra, kv, g) in zip(where, _hgrn_chunk_operators(problems, lb, tris, levels)):
            qe_scr[d, sl, :] = qe.astype(BF16)
            o_scr[d, sl, :] = o_intra
            kv_scr[d, chunk] = kv
            g_scr[d, chunk] = g
        return carry

    lax.fori_loop(0, n_chunks // HGRN_GROUP, operators, 0)

    def scan(trip, carry):
        steps = []
        for k in range(HGRN_SCAN_UNROLL):
            chunks = _scan_chunks(trip * HGRN_SCAN_UNROLL + k, n_chunks, n_ctx_chunks)
            steps += [(d, pl.ds(pl.multiple_of(chunk * t, t), t), st) for d, (chunk, st) in enumerate(zip(chunks, carry))]
            carry = tuple(st * g_scr[d, chunk] + kv_scr[d, chunk] for d, (chunk, st) in enumerate(zip(chunks, carry)))
        inters = [_bdot_nt(qe_scr[d, sl, :], st) for d, sl, st in steps]
        for (d, sl, _), inter in zip(steps, inters):
            o_scr[d, sl, :] = o_scr[d, sl, :] + inter
        return carry

    zero_state = jnp.zeros((dh, dh), F32)
    lax.fori_loop(0, n_chunks // HGRN_SCAN_UNROLL, scan, (zero_state, zero_state))

    f_rows = MIX_FINISH_ROWS

    def finish(j, carry):
        sl = pl.ds(pl.multiple_of(j * f_rows, f_rows), f_rows)
        o = o_scr[0, sl, :] + o_scr[1, sl, :]
        y = o * lax.rsqrt(jnp.mean(o * o, axis=-1, keepdims=True) + RMS_EPS) * ng_ref[...]
        out_ref[0, sl, :] = (y * _silu(z_ref[0, sl, :].astype(F32))).astype(out_ref.dtype)
        return carry

    lax.fori_loop(0, (s - n_ctx) // f_rows, finish, 0)


def _hgrn(u, lb_all, norm_g, n_ctx, layer):
    bsz, s, _ = u.shape
    nh = HGRN_N_HEADS
    dh = LANES
    width = nh * dh
    depth = lb_all.shape[0]
    n_chunks = s // MIX_CHUNK
    assert n_chunks % HGRN_GROUP == 0 and n_chunks % HGRN_SCAN_UNROLL == 0 and n_ctx % MIX_CHUNK == 0
    col = lambda k: pl.BlockSpec((1, s, dh), lambda b, h, k=k: (b, 0, k * nh + h))
    return pl.pallas_call(
        functools.partial(_hgrn_kernel, n_ctx=n_ctx, layer=layer),
        grid=(bsz, nh),
        in_specs=[col(0), col(1), col(2), col(3), col(4),
                  pl.BlockSpec((depth, dh), lambda b, h: (0, h)),
                  pl.BlockSpec((1, dh), lambda b, h: (0, h))],
        out_specs=pl.BlockSpec((1, s - n_ctx, dh), lambda b, h: (b, 0, h)),
        out_shape=jax.ShapeDtypeStruct((bsz, s - n_ctx, width), BF16),
        scratch_shapes=[pltpu.VMEM((2, s, dh), F32), pltpu.VMEM((2, s, dh), BF16),
                        pltpu.VMEM((2, n_chunks, dh, dh), F32), pltpu.VMEM((2, n_chunks, 1, dh), F32)],
        compiler_params=_compiler_params(("parallel", "arbitrary")),
        name="hgrn2_mixer",
    )(u, u, u, u, u, lb_all, norm_g.reshape(1, width))


def _hyena_filter_kernel(z_ref, w1_ref, b1_ref, w2_ref, b2_ref, w3f_ref, w3b_ref, dl_ref, hf_ref, hb_ref, hid_scr):
    hp = functools.partial(jnp.dot, precision=lax.Precision.HIGHEST, preferred_element_type=F32)
    n = z_ref.shape[0]

    @pl.when(pl.program_id(0) == 0)
    def _():
        first = jnp.sin(hp(z_ref[...], w1_ref[...]) + b1_ref[...])
        hid_scr[...] = jnp.sin(hp(first, w2_ref[...]) + b2_ref[...])

    hid = hid_scr[...]
    pos = _iota((n, 1), 0).astype(F32) * (1.0 / n)
    window = jnp.exp(-pos * dl_ref[...]) + HYENA_SHIFT
    f0 = hp(hid, w3f_ref[...]) * window
    f1 = hp(hid, w3b_ref[...]) * window
    nrm = jnp.sum(jnp.abs(f0), axis=0, keepdims=True) + jnp.sum(jnp.abs(f1), axis=0, keepdims=True)
    hf_ref[...] = f0 / nrm
    hb_ref[...] = f1 / nrm


def _hyena_filters(n, w1, b1, w2, b2, w3, width):
    pos = np.arange(n, dtype=np.float64)
    bands = np.linspace(1e-4, HYENA_N_BANDS - 1, HYENA_N_BANDS)
    ang = (2.0 * math.pi / n) * pos[:, None] * bands
    z = np.concatenate([(pos / n)[:, None], np.cos(ang), np.sin(ang)], axis=-1)
    z = np.pad(z, ((0, 0), (0, LANES - z.shape[1]))).astype(np.float32)
    max_decay = math.log(HYENA_TGT) / HYENA_FAST
    min_decay = math.log(HYENA_TGT) / HYENA_SLOW
    deltas = np.abs(np.linspace(min_decay, max_decay, width)).astype(np.float32)[None]
    feat, hid = w1.shape
    w1p = jnp.pad(w1, ((0, LANES - feat), (0, LANES - hid)))
    w2p = jnp.pad(w2, ((0, LANES - hid), (0, LANES - hid)))
    w3p = jnp.pad(w3, ((0, LANES - hid), (0, 0)))
    b1p = jnp.pad(b1, (0, LANES - hid)).reshape(1, LANES)
    b2p = jnp.pad(b2, (0, LANES - hid)).reshape(1, LANES)
    n_tiles = width // LANES
    full = lambda shape: pl.BlockSpec(shape, lambda j: (0, 0))
    out = pl.BlockSpec((n, LANES), lambda j: (0, j))
    hf, hb = pl.pallas_call(
        _hyena_filter_kernel,
        grid=(n_tiles,),
        in_specs=[full((n, LANES)), full((LANES, LANES)), full((1, LANES)), full((LANES, LANES)), full((1, LANES)),
                  pl.BlockSpec((LANES, LANES), lambda j: (0, j)),
                  pl.BlockSpec((LANES, LANES), lambda j: (0, n_tiles + j)),
                  pl.BlockSpec((1, LANES), lambda j: (0, j))],
        out_specs=[out, out],
        out_shape=[jax.ShapeDtypeStruct((n, width), F32)] * 2,
        scratch_shapes=[pltpu.VMEM((n, LANES), F32)],
        compiler_params=_compiler_params(("arbitrary",)),
        name="hyena_filters",
    )(jnp.asarray(z), w1p, b1p, w2p, b2p, w3p, w3p, jnp.asarray(deltas))
    return jnp.concatenate([hf, hb], axis=1)


def _hyena_pre_kernel(yv_ref, y0_ref, y1_ref, yz_ref, swv_ref, sw0_ref, sw1_ref, sbv_ref, sb0_ref, sb1_ref,
                      p_ref, e_ref, *, n_ctx):
    s = yv_ref.shape[1]
    rows = MIX_CHUNK

    def body(j, carry):
        t0 = pl.multiple_of(j * rows, rows)
        conv = []
        for src, sw, sb in ((yv_ref, swv_ref, sbv_ref), (y0_ref, sw0_ref, sb0_ref), (y1_ref, sw1_ref, sb1_ref)):
            cur, down, up = _chunk_with_neighbors(src, t0, rows, s, s - n_ctx)
            conv.append(down * sw[0:1, :] + cur * sw[1:2, :] + up * sw[2:3, :] + sb[...])
        v, x0, x1 = conv
        p = x1 * v
        o0 = pl.multiple_of(j * rows, rows)
        p_ref[0, pl.ds(o0, rows), :] = p.astype(BF16)
        e_ref[0, pl.ds(o0, rows), :] = (x0 * _silu(yz_ref[0, pl.ds(t0, rows), :].astype(F32))).astype(BF16)
        return carry

    lax.fori_loop(0, (s - n_ctx) // rows, body, 0)


def _hyena_pre(u, col0, short_w, short_b, n_ctx):
    bsz, s, _ = u.shape
    w = short_b.shape[0] // 3
    tiles = w // LANES
    base = col0 // LANES
    n = s - n_ctx
    col = lambda k: pl.BlockSpec((1, s, LANES), lambda b, j, k=k: (b, 0, base + k * tiles + j))
    par = lambda rows, k: pl.BlockSpec((rows, LANES), lambda b, j, k=k: (0, k * tiles + j))
    out = pl.BlockSpec((1, n, LANES), lambda b, j: (b, 0, j))
    sb = short_b.reshape(1, 3 * w)
    return pl.pallas_call(
        functools.partial(_hyena_pre_kernel, n_ctx=n_ctx),
        grid=(bsz, tiles),
        in_specs=[col(0), col(1), col(2), col(3),
                  par(3, 0), par(3, 1), par(3, 2), par(1, 0), par(1, 1), par(1, 2)],
        out_specs=[out, out],
        out_shape=[jax.ShapeDtypeStruct((bsz, n, w), BF16)] * 2,
        compiler_params=_compiler_params(("parallel", "arbitrary")),
        name="hyena_short_conv",
    )(u, u, u, u, short_w, short_w, short_w, sb, sb, sb)


def _dft_tables(n):
    big = 2 * n
    half = DFT_TILE // 2
    idx = jnp.arange(n, dtype=jnp.int32)
    split = DFT_SPLIT
    lo = jnp.arange(split, dtype=jnp.int32)
    hi = jnp.arange(n // split, dtype=jnp.int32)
    ang_lo = ((lo[:, None] * idx[None, :]) % big).astype(F32) * (2.0 * math.pi / big)
    ang_hi = ((hi[:, None] * idx[None, :]) % (big // split)).astype(F32) * (2.0 * math.pi * split / big)
    c_lo, s_lo = jnp.cos(ang_lo)[None], jnp.sin(ang_lo)[None]
    c_hi, s_hi = jnp.cos(ang_hi)[:, None], jnp.sin(ang_hi)[:, None]
    cos = (c_hi * c_lo - s_hi * s_lo).reshape(n, n)
    sin = (s_hi * c_lo + c_hi * s_lo).reshape(n, n)
    alt = jnp.where(idx % 2 == 0, 1.0, -1.0).astype(F32)
    first_row = (idx == 0)[:, None]
    im = jnp.where(first_row, alt[None, :], -sin)
    fwd = jnp.stack([cos.reshape(n // half, half, n), im.reshape(n // half, half, n)], axis=1).reshape(big, n)
    weight = jnp.where(first_row, 1.0, 2.0) * (1.0 / big)
    weight = jnp.stack([weight.reshape(n // half, half, 1)] * 2, axis=1).reshape(big, 1)
    return fwd.astype(BF16), (fwd * weight).T.astype(BF16)


def _spectrum_kernel(f_ref, lo_ref, hi_ref, bias_ref, o_ref):
    half = DFT_TILE // 2
    w = lo_ref.shape[1] // 2
    lo = lo_ref[...].astype(BF16)
    acc_lo = jnp.dot(f_ref[...], lo, preferred_element_type=F32)
    acc_hi = jnp.dot(f_ref[...], hi_ref[...].astype(BF16), preferred_element_type=F32)
    rows = _iota((DFT_TILE, w), 0)
    sign = jnp.where(rows % 2 == 1, -1.0, 1.0)
    real_slot = jnp.logical_or(rows < half, jnp.logical_and(rows == half, pl.program_id(0) == 0))
    conj = lambda x: jnp.where(real_slot, x, -x)
    lag0 = lo[0:1, :].astype(F32)
    a, c_all = acc_lo[:, :w], acc_lo[:, w:]
    b, d = acc_hi[:, :w], acc_hi[:, w:]
    c = c_all - jnp.where(real_slot, lag0[:, w:], 0.0)
    o_ref[:, 0:w] = a + conj(c_all) + jnp.where(real_slot, bias_ref[...], 0.0)
    o_ref[:, w:2 * w] = b + sign * (a - jnp.where(real_slot, lag0[:, :w], 0.0))
    o_ref[:, 2 * w:] = sign * conj(c) + conj(d)


def _filter_spectrum(fwd, hk, bias):
    big, h = fwd.shape
    w = hk.shape[1] // 2
    assert hk.shape[0] == 2 * h
    return pl.pallas_call(
        _spectrum_kernel,
        grid=(big // DFT_TILE,),
        in_specs=[pl.BlockSpec((DFT_TILE, h), lambda i: (i, 0)),
                  pl.BlockSpec((h, 2 * w), lambda i: (0, 0)),
                  pl.BlockSpec((h, 2 * w), lambda i: (1, 0)),
                  pl.BlockSpec((1, w), lambda i: (0, 0))],
        out_specs=pl.BlockSpec((DFT_TILE, 3 * w), lambda i: (i, 0)),
        out_shape=jax.ShapeDtypeStruct((big, 3 * w), F32),
        compiler_params=_compiler_params(("arbitrary",)),
        name="hyena_filter_spectrum",
    )(fwd, hk, hk, bias.reshape(1, w))


def _spectral_products(pairs, packed):
    re = sum(s[0] * k[0] - s[1] * k[1] for s, k in pairs)
    im = sum(s[0] * k[1] + s[1] * k[0] for s, k in pairs)
    if packed is not None:
        re = jnp.where(packed, sum(s[0] * k[0] for s, k in pairs), re)
        im = jnp.where(packed, sum(s[1] * k[1] for s, k in pairs), im)
    return jnp.concatenate([re, im], axis=0).astype(BF16)


def _conv_spectrum_kernel(f_ref, pa_ref, pb_ref, ks_ref, z_ref):
    i = pl.program_id(1)
    half = DFT_TILE // 2
    w = pa_ref.shape[2]
    units = [slice(k * DFT_TILE, (k + 1) * DFT_TILE) for k in range(DFT_UNITS)]
    accs = [(jnp.dot(f_ref[u, :], pa_ref[0], preferred_element_type=F32),
             jnp.dot(f_ref[u, :], pb_ref[0], preferred_element_type=F32)) for u in units]
    for k, (u, (acc_a, acc_b)) in enumerate(zip(units, accs)):
        s_a, s_b = (acc_a[:half], acc_a[half:]), (acc_b[:half], acc_b[half:])
        ks = ks_ref[u, :]
        k_diag, k_below, k_above = [(ks[:half, j * w:(j + 1) * w], ks[half:, j * w:(j + 1) * w]) for j in range(3)]
        packed = jnp.logical_and(_iota((half, w), 0) == 0, i == 0) if k == 0 else None
        z_ref[0, 0, u, :] = _spectral_products([(s_a, k_diag), (s_b, k_above)], packed)
        z_ref[0, 1, u, :] = _spectral_products([(s_a, k_below), (s_b, k_diag)], packed)


def _conv_spectrum(fwd, p16, kspec):
    bsz, n, w = p16.shape
    big, h = fwd.shape
    assert n == 2 * h
    rows = DFT_TILE * DFT_UNITS
    return pl.pallas_call(
        _conv_spectrum_kernel,
        grid=(bsz, big // rows),
        in_specs=[pl.BlockSpec((rows, h), lambda b, i: (i, 0)),
                  pl.BlockSpec((1, h, w), lambda b, i: (b, 0, 0)),
                  pl.BlockSpec((1, h, w), lambda b, i: (b, 1, 0)),
                  pl.BlockSpec((rows, 3 * w), lambda b, i: (i, 0))],
        out_specs=pl.BlockSpec((1, 2, rows, w), lambda b, i: (b, 0, i, 0)),
        out_shape=jax.ShapeDtypeStruct((bsz, 2, big, w), BF16),
        compiler_params=_compiler_params(("parallel", "arbitrary")),
        name="hyena_forward_dft",
    )(fwd, p16, p16, kspec)


def _conv_inverse_kernel(g_ref, z_ref, e_ref, o_ref):
    y = jnp.dot(g_ref[...], z_ref[0, 0], preferred_element_type=F32)
    o_ref[0] = (e_ref[0].astype(F32) * y).astype(o_ref.dtype)


def _conv_inverse(inv, z16, e):
    bsz, _, big, w = z16.shape
    h = inv.shape[0]
    tile = DFT_TILE
    per_half = h // tile
    tok = pl.BlockSpec((1, tile, w), lambda b, j, i: (b, j * per_half + i, 0))
    return pl.pallas_call(
        _conv_inverse_kernel,
        grid=(bsz, 2, per_half),
        in_specs=[pl.BlockSpec((tile, big), lambda b, j, i: (i, 0)),
                  pl.BlockSpec((1, 1, big, w), lambda b, j, i: (b, j, 0, 0)),
                  tok],
        out_specs=tok,
        out_shape=jax.ShapeDtypeStruct((bsz, 2 * h, w), BF16),
        compiler_params=_compiler_params(("parallel", "arbitrary", "arbitrary")),
        name="hyena_inverse_dft",
    )(inv, z16, e)


def _even_weight_layout(w_in, gate_b):
    d = w_in.shape[0]
    mw = MLSTM_N_HEADS * LANES
    g0 = 5 * mw
    g1 = g0 + 4 * MLSTM_N_HEADS
    main = jnp.concatenate([w_in[:, :g0], w_in[:, g1:]], axis=1).astype(BF16)
    wg = w_in[:, g0:g1].reshape(d, 2, 2, MLSTM_N_HEADS)
    wg = jnp.transpose(wg, (3, 1, 2, 0)).reshape(MLSTM_N_HEADS, 4, d)
    rows = MLSTM_N_HEADS * SUBLANES
    wg = jnp.concatenate([wg, jnp.zeros_like(wg)], axis=1).reshape(rows, d).astype(BF16)
    gb = jnp.transpose(gate_b.reshape(2, 2, MLSTM_N_HEADS), (2, 0, 1)).reshape(MLSTM_N_HEADS, 4)
    gb = jnp.concatenate([gb, jnp.zeros_like(gb)], axis=1).reshape(rows, 1)
    return main, wg, jnp.broadcast_to(gb, (rows, LANES))


def _rwkv_params(mu, w0, w_up, a0, a_up, k_k, k_a, r_k, ln_w, ln_b):
    width = mu.shape[1]
    n_pairs = width // LANES
    row = lambda x: x.reshape(1, width)

    def cat_dirs(x):
        return jnp.transpose(x.reshape(2, n_pairs, LANES), (1, 0, 2)).reshape(n_pairs, 1, 2 * LANES)

    def block_up(x):
        lora = x.shape[1]
        xp = jnp.transpose(x.reshape(2, lora, n_pairs, LANES), (2, 0, 1, 3))
        z = jnp.zeros_like(xp[:, 0])
        top = jnp.concatenate([xp[:, 0], z], axis=2)
        bot = jnp.concatenate([z, xp[:, 1]], axis=2)
        return jnp.concatenate([top, bot], axis=1).astype(BF16)

    return {"mu": mu, "k_k": row(k_k), "k_a": row(k_a), "r_k": row(r_k), "ln_w": row(ln_w), "ln_b": row(ln_b),
            "w0": cat_dirs(w0), "a0": cat_dirs(a0), "w_up": block_up(w_up), "a_up": block_up(a_up)}


def kernel(x, c, ctx, c_ctx, l0_norm_g, l0_mod_w, l0_mod_b, l0_w_in, l0_w_out, l0_mlstm_conv_w, l0_mlstm_gate_b, l0_mlstm_norm_g, l0_rwkv_mu, l0_rwkv_w0, l0_rwkv_w_up, l0_rwkv_a0, l0_rwkv_a_up, l0_rwkv_k_k, l0_rwkv_k_a, l0_rwkv_r_k, l0_rwkv_ln_w, l0_rwkv_ln_b, hgrn_lower_bounds, l1_norm_g, l1_mod_w, l1_mod_b, l1_w_in, l1_w_out, l1_hgrn_norm_g, l1_hyena_short_w, l1_hyena_short_b, l1_hyena_w1, l1_hyena_b1, l1_hyena_w2, l1_hyena_b2, l1_hyena_w3, l1_hyena_bias, final_norm_g):
    bsz, n_lat, d = x.shape
    n_ctx = ctx.shape[1]

    pad = (-(bsz + 1)) % 8
    cc = jnp.concatenate([c, c_ctx[None], jnp.zeros((pad, d), F32)], axis=0)
    mod0, mod1 = _modulation(cc, l0_mod_w, l0_mod_b, l1_mod_w, l1_mod_b)
    mod0 = mod0[:bsz + 1].reshape(bsz + 1, 3, d)
    mod1 = mod1[:bsz + 1].reshape(bsz + 1, 3, d)

    w_main, w_gate, b_gate = _even_weight_layout(l0_w_in, l0_mlstm_gate_b)
    n0 = w_main.shape[1]
    u0, gt0 = _proj_in(x, ctx, l0_norm_g, mod0, w_main, n0 // 2, w_gate, b_gate)
    gt0 = gt0.reshape(bsz, MLSTM_N_HEADS, SUBLANES, n_ctx + n_lat)
    y_m = _mlstm(u0, gt0, l0_mlstm_conv_w, l0_mlstm_norm_g, n_ctx)
    rp = _rwkv_params(l0_rwkv_mu, l0_rwkv_w0, l0_rwkv_w_up, l0_rwkv_a0, l0_rwkv_a_up, l0_rwkv_k_k,
                      l0_rwkv_k_a, l0_rwkv_r_k, l0_rwkv_ln_w, l0_rwkv_ln_b)
    y_r = _rwkv(u0, 5 * MLSTM_N_HEADS * LANES, rp, n_ctx)
    x1, ctx1 = _proj_out(y_m, y_r, x, ctx, mod0, l0_w_out.astype(BF16))

    x1 = x1.reshape(bsz, n_lat // GRID_WIDTH, GRID_WIDTH, d)
    w1 = l1_w_in.astype(BF16)
    (u1,) = _proj_in(x1, ctx1, l1_norm_g, mod1, w1, w1.shape[1] // 2)
    y_g = _hgrn(u1, hgrn_lower_bounds, l1_hgrn_norm_g, n_ctx, layer=1)
    hw = l1_hyena_bias.shape[0]
    hk = _hyena_filters(n_lat, l1_hyena_w1, l1_hyena_b1, l1_hyena_w2, l1_hyena_b2, l1_hyena_w3, hw)
    fwd, inv = _dft_tables(n_lat // 2)
    kspec = _filter_spectrum(fwd, hk, l1_hyena_bias)
    p16, e16 = _hyena_pre(u1, 5 * HGRN_N_HEADS * LANES, l1_hyena_short_w, l1_hyena_short_b, n_ctx)
    z16 = _conv_spectrum(fwd, p16, kspec)
    y_y = _conv_inverse(inv, z16, e16)
    out = _proj_out_final(y_g, y_y, x1, mod1, l1_w_out.astype(BF16), final_norm_g)
    return out.reshape(bsz, n_lat, d)
```

```python
import functools
import math

import jax
import jax.numpy as jnp
import numpy as np
from jax import lax
from jax.experimental import pallas as pl
from jax.experimental.pallas import tpu as pltpu

F32 = jnp.float32
BF16 = jnp.bfloat16

GRID_WIDTH = 64
RMS_EPS = 1e-6
MLSTM_N_HEADS = 4
RWKV_HEAD = 64
RWKV_LN_EPSILON = 64e-5
HGRN_N_HEADS = 4
HYENA_N_BANDS = 16
HYENA_FAST = 0.3
HYENA_SLOW = 1.5
HYENA_TGT = 1e-2
HYENA_SHIFT = 0.05
LOG2_E = 1.0 / math.log(2.0)

LANES = 128
SUBLANES = 8
ROW_GROUP = 16
VMEM_LIMIT = 52 * 1024 * 1024

MIX_CHUNK = 128
RWKV_CHUNK = 64
RWKV_GROUP = 12
MIX_PREP_ROWS = 256
MIX_FINISH_ROWS = 256
RWKV_PREP_ROWS = 256
RWKV_PREP_BLOCKS = 3
RWKV_FINISH_ROWS = 768
RWKV_SCAN_UNROLL = 3
SCAN_UNROLL = 3
HGRN_SCAN_UNROLL = 6
MLSTM_GROUP = 18
HGRN_GROUP = 6
PROJ_ROWS = 768
FINAL_ROWS = 512
DFT_TILE = 512
DFT_UNITS = 2
DFT_SPLIT = 64


def _bdot(a, b):
    return jnp.dot(a.astype(BF16), b.astype(BF16), preferred_element_type=F32)


def _bdot_nt(a, b):
    return lax.dot_general(a.astype(BF16), b.astype(BF16), (((1,), (1,)), ((), ())),
                           preferred_element_type=F32)


def _bdot_tn(a, b):
    return lax.dot_general(a.astype(BF16), b.astype(BF16), (((0,), (0,)), ((), ())),
                           preferred_element_type=F32)


def _split3(x):
    hi = x.astype(BF16)
    r1 = x - hi.astype(F32)
    mid = r1.astype(BF16)
    lo = (r1 - mid.astype(F32)).astype(BF16)
    return hi, mid, lo


def _sel_dot(sel, x):
    hi, mid, lo = _split3(x)
    d = functools.partial(jnp.dot, preferred_element_type=F32)
    return d(sel, hi) + d(sel, mid) + d(sel, lo)


def _dot_sel(x, sel):
    hi = x.astype(BF16)
    mid = (x - hi.astype(F32)).astype(BF16)
    d = functools.partial(jnp.dot, preferred_element_type=F32)
    return d(hi, sel) + d(mid, sel)


def _sigmoid(x):
    return 1.0 / (1.0 + jnp.exp(-x))


def _silu(x):
    return x * _sigmoid(x)


def _iota(shape, dim):
    return lax.broadcasted_iota(jnp.int32, shape, dim)


def _neighbor_rows(ref, t0, rows, n_total, split):
    has_prev = jnp.logical_and(t0 != 0, t0 != split)
    has_next = jnp.logical_and(t0 + rows != split, t0 + rows != n_total)
    g = ROW_GROUP
    before = ref[0, pl.ds(pl.multiple_of(jnp.maximum(t0 - g, 0), g), g), :].astype(F32)
    after = ref[0, pl.ds(pl.multiple_of(jnp.minimum(t0 + rows, n_total - g), g), g), :].astype(F32)
    return jnp.where(has_prev, before[g - 1:g], 0.0), jnp.where(has_next, after[0:1], 0.0)


def _shifted(cur, prev_row, next_row):
    rows = cur.shape[0]
    rid = _iota(cur.shape, 0)
    down = jnp.where(rid == 0, prev_row, pltpu.roll(cur, 1, 0))
    up = jnp.where(rid == rows - 1, next_row, pltpu.roll(cur, rows - 1, 0))
    return down, up


def _chunk_with_neighbors(ref, t0, rows, n_total, split):
    cur = ref[0, pl.ds(t0, rows), :].astype(F32)
    prev_row, next_row = _neighbor_rows(ref, t0, rows, n_total, split)
    down, up = _shifted(cur, prev_row, next_row)
    return cur, down, up


def _scan_chunks(i, n_chunks, n_ctx_chunks):
    fwd = jnp.where(i < n_ctx_chunks, n_chunks - n_ctx_chunks + i, i - n_ctx_chunks)
    return fwd, n_chunks - 1 - i


def _compiler_params(semantics):
    return pltpu.CompilerParams(dimension_semantics=semantics, vmem_limit_bytes=VMEM_LIMIT)


def _mod_kernel(c_ref, w0_ref, b0_ref, w1_ref, b1_ref, o0_ref, o1_ref):
    s = _silu(c_ref[...])
    o0_ref[...] = _bdot(s, w0_ref[...]) + b0_ref[...]
    o1_ref[...] = _bdot(s, w1_ref[...]) + b1_ref[...]


def _modulation(cc, w0, b0, w1, b1):
    rows, d = cc.shape
    n = w0.shape[1]
    tile = d
    grid = (n // tile,)
    wspec = pl.BlockSpec((d, tile), lambda j: (0, j))
    bspec = pl.BlockSpec((1, tile), lambda j: (0, j))
    ospec = pl.BlockSpec((rows, tile), lambda j: (0, j))
    return pl.pallas_call(
        _mod_kernel,
        grid=grid,
        in_specs=[pl.BlockSpec((rows, d), lambda j: (0, 0)), wspec, bspec, wspec, bspec],
        out_specs=[ospec, ospec],
        out_shape=[jax.ShapeDtypeStruct((rows, n), F32)] * 2,
        compiler_params=_compiler_params(("arbitrary",)),
        name="adaln_modulation",
    )(cc, w0, b0.reshape(1, n), w1, b1.reshape(1, n))


def _load_tokens(x_ref):
    if len(x_ref.shape) == 3:
        return x_ref[0]
    return jnp.concatenate([x_ref[0, :, j, :] for j in range(x_ref.shape[2])], axis=0)


def _token_spec(x, rows, index_map):
    if x.ndim == 3:
        return pl.BlockSpec((1, rows, x.shape[2]), lambda *g: (index_map(*g)[0], index_map(*g)[1], 0))
    assert rows % x.shape[1] == 0
    return pl.BlockSpec((1, x.shape[1], rows // x.shape[1], x.shape[3]),
                        lambda *g: (index_map(*g)[0], 0, index_map(*g)[1], 0))


def _token_tile(x_ref, c_ref, last_of_batch, rows):
    n_lat_tail = rows - c_ref.shape[1]
    is_ctx = jnp.logical_and(last_of_batch, _iota((rows, 1), 0) >= n_lat_tail)
    ctx_rows = jnp.concatenate([jnp.zeros((n_lat_tail, c_ref.shape[2]), F32), c_ref[0]], axis=0)
    return jnp.where(is_ctx, ctx_rows, _load_tokens(x_ref)), is_ctx


def _proj_in_kernel(*refs, rows, with_gates):
    if with_gates:
        x_ref, c_ref, g_ref, ml_ref, mc_ref, w_ref, wg_ref, gb_ref, u_ref, gt_ref, h_scr = refs
    else:
        x_ref, c_ref, g_ref, ml_ref, mc_ref, w_ref, u_ref, h_scr = refs
    i = pl.program_id(1)
    n = pl.program_id(2)

    @pl.when(n == 0)
    def _():
        x, is_ctx = _token_tile(x_ref, c_ref, i == pl.num_programs(1) - 1, rows)
        y = x * lax.rsqrt(jnp.mean(x * x, axis=-1, keepdims=True) + RMS_EPS) * g_ref[...]
        ml = ml_ref[0]
        mc = mc_ref[0]
        shift = jnp.where(is_ctx, mc[0:1], ml[0:1])
        scale = jnp.where(is_ctx, mc[1:2], ml[1:2])
        h = (y * (1.0 + scale) + shift).astype(BF16)
        h_scr[...] = h
        if with_gates:
            gt_ref[0] = _bdot_nt(wg_ref[...], h) + gb_ref[:, 0:1]

    u_ref[0] = jnp.dot(h_scr[...], w_ref[...], preferred_element_type=F32).astype(u_ref.dtype)


def _proj_in(x, ctx, norm_g, mod3, w16, n_tile, gate_w=None, gate_b=None):
    bsz, d = x.shape[0], x.shape[-1]
    n_lat = math.prod(x.shape[1:-1])
    n_ctx = ctx.shape[1]
    s = n_lat + n_ctx
    n = w16.shape[1]
    rows = PROJ_ROWS
    assert s % rows == 0 and (n_lat % rows) + n_ctx == rows
    grid = (bsz, s // rows, n // n_tile)
    ctx_row = mod3.shape[0] - 1
    with_gates = gate_w is not None
    in_specs = [
        _token_spec(x, rows, lambda b, i, j: (b, i)),
        pl.BlockSpec((1, n_ctx, d), lambda b, i, j: (b, 0, 0)),
        pl.BlockSpec((1, d), lambda b, i, j: (0, 0)),
        pl.BlockSpec((1, 3, d), lambda b, i, j: (b, 0, 0)),
        pl.BlockSpec((1, 3, d), lambda b, i, j: (ctx_row, 0, 0)),
        pl.BlockSpec((d, n_tile), lambda b, i, j: (0, j)),
    ]
    args = [x, ctx, norm_g.reshape(1, d), mod3, mod3, w16]
    out_specs = [pl.BlockSpec((1, rows, n_tile), lambda b, i, j: (b, i, j))]
    out_shape = [jax.ShapeDtypeStruct((bsz, s, n), BF16)]
    if with_gates:
        ng = gate_w.shape[0]
        in_specs += [pl.BlockSpec((ng, d), lambda b, i, j: (0, 0)),
                     pl.BlockSpec((ng, LANES), lambda b, i, j: (0, 0))]
        args += [gate_w, gate_b]
        out_specs.append(pl.BlockSpec((1, ng, rows), lambda b, i, j: (b, 0, i)))
        out_shape.append(jax.ShapeDtypeStruct((bsz, ng, s), F32))
    return pl.pallas_call(
        functools.partial(_proj_in_kernel, rows=rows, with_gates=with_gates),
        grid=grid,
        in_specs=in_specs,
        out_specs=out_specs,
        out_shape=out_shape,
        scratch_shapes=[pltpu.VMEM((rows, d), BF16)],
        compiler_params=_compiler_params(("parallel", "arbitrary", "arbitrary")),
        name="norm_mod_proj_in",
    )(*args)


def _proj_out_kernel(ya_ref, yb_ref, x_ref, c_ref, ml_ref, mc_ref, w_ref, ox_ref, oc_ref, *, rows):
    i = pl.program_id(1)
    half = ya_ref.shape[2]
    y = _bdot(ya_ref[0], w_ref[0:half, :]) + _bdot(yb_ref[0], w_ref[half:, :])
    x, is_ctx = _token_tile(x_ref, c_ref, i == pl.num_programs(1) - 1, rows)
    x = x + jnp.where(is_ctx, mc_ref[0][2:3], ml_ref[0][2:3]) * y
    ox_ref[0] = x

    @pl.when(i == pl.num_programs(1) - 1)
    def _():
        oc_ref[0] = x[rows - c_ref.shape[1]:]


def _proj_out(ya, yb, x, ctx, mod3, w16):
    bsz, n_lat, d = x.shape
    n_ctx = ctx.shape[1]
    s = n_lat + n_ctx
    half = ya.shape[2]
    rows = PROJ_ROWS
    assert s % rows == 0 and (n_lat % rows) + n_ctx == rows
    ctx_row = mod3.shape[0] - 1
    tok = lambda w: pl.BlockSpec((1, rows, w), lambda b, i: (b, i, 0))
    seg = pl.BlockSpec((1, n_ctx, d), lambda b, i: (b, 0, 0))
    return pl.pallas_call(
        functools.partial(_proj_out_kernel, rows=rows),
        grid=(bsz, s // rows),
        in_specs=[tok(half), tok(half), tok(d), seg,
                  pl.BlockSpec((1, 3, d), lambda b, i: (b, 0, 0)),
                  pl.BlockSpec((1, 3, d), lambda b, i: (ctx_row, 0, 0)),
                  pl.BlockSpec((2 * half, d), lambda b, i: (0, 0))],
        out_specs=[tok(d), seg],
        out_shape=[jax.ShapeDtypeStruct((bsz, n_lat, d), F32), jax.ShapeDtypeStruct((bsz, n_ctx, d), F32)],
        compiler_params=_compiler_params(("parallel", "arbitrary")),
        name="proj_out_residual",
    )(ya, yb, x, ctx, mod3, mod3, w16)


def _proj_out_final_kernel(ya_ref, yb_ref, x_ref, ml_ref, w_ref, fg_ref, o_ref):
    half = ya_ref.shape[2]
    y = _bdot(ya_ref[0], w_ref[0:half, :]) + _bdot(yb_ref[0], w_ref[half:, :])
    x = _load_tokens(x_ref) + ml_ref[0][2:3] * y
    out = x * lax.rsqrt(jnp.mean(x * x, axis=-1, keepdims=True) + RMS_EPS) * fg_ref[...]
    grid_rows = o_ref.shape[1]
    for j in range(o_ref.shape[2]):
        o_ref[0, :, j, :] = out[j * grid_rows:(j + 1) * grid_rows]


def _proj_out_final(ya, yb, x, mod3, w16, final_g):
    bsz, gr, gc, d = x.shape
    n_lat = gr * gc
    half = ya.shape[2]
    rows = FINAL_ROWS
    assert n_lat % rows == 0
    tok = lambda w: pl.BlockSpec((1, rows, w), lambda b, i: (b, i, 0))
    raster = _token_spec(x, rows, lambda b, i: (b, i))
    return pl.pallas_call(
        _proj_out_final_kernel,
        grid=(bsz, n_lat // rows),
        in_specs=[tok(half), tok(half), raster,
                  pl.BlockSpec((1, 3, d), lambda b, i: (b, 0, 0)),
                  pl.BlockSpec((2 * half, d), lambda b, i: (0, 0)),
                  pl.BlockSpec((1, d), lambda b, i: (0, 0))],
        out_specs=raster,
        out_shape=jax.ShapeDtypeStruct(x.shape, F32),
        compiler_params=_compiler_params(("parallel", "arbitrary")),
        name="proj_out_final_norm",
    )(ya, yb, x, mod3, w16, final_g.reshape(1, d))


def _mlstm_chunk_operators(chunks, causal):
    t = chunks[0][0].shape[0]
    lane = _iota((SUBLANES, t), 1)
    row_id = _iota((SUBLANES, t), 0)
    log_fs = [jnp.minimum(c[3], 0.0) - jnp.log1p(jnp.exp(-jnp.abs(c[3]))) for c in chunks]
    cum_f, cum_b = list(log_fs), list(log_fs)
    sh = 1
    while sh < t:
        cum_f = [x + jnp.where(lane >= sh, pltpu.roll(x, sh, 1), 0.0) for x in cum_f]
        cum_b = [x + jnp.where(lane < t - sh, pltpu.roll(x, t - sh, 1), 0.0) for x in cum_b]
        sh *= 2
    pad = jnp.zeros((t - SUBLANES, t), F32)
    tiles = [jnp.concatenate([jnp.where(row_id % 2 == 0, c[3], jnp.where(row_id == 1, f, b)), pad], axis=0)
             for c, f, b in zip(chunks, cum_f, cum_b)]
    cols = [x.T for x in tiles]
    k_ts = [c[1].T for c in chunks]
    problems = []
    for c, f, b, col, k_t in zip(chunks, cum_f, cum_b, cols, k_ts):
        for d in range(2):
            b_row = (f, b)[d][2 * d + 1:2 * d + 2]
            problems.append(dict(q=c[0], k=c[1], k_t=k_t, v_ext=c[2], d=d, ig_row=c[3][2 * d:2 * d + 1], b_row=b_row,
                                 b_col=col[:, 2 * d + 1:2 * d + 2]))
    logws = [jnp.where(causal[p["d"]], p["b_col"] + (p["ig_row"] - p["b_row"]), -jnp.inf) for p in problems]
    mus = [jnp.max(x, axis=-1, keepdims=True) for x in logws]
    ws = [jnp.exp(x - mu) for x, mu in zip(logws, mus)]
    lasts = [0 if p["d"] == 1 else t - 1 for p in problems]
    b_lasts = [p["b_col"][i:i + 1] for p, i in zip(problems, lasts)]
    gammas = [mu[i:i + 1] for mu, i in zip(mus, lasts)]
    gk_ts = [p["k_t"] * jnp.exp(bl - p["b_row"] + p["ig_row"] - gm) for p, bl, gm in zip(problems, b_lasts, gammas)]
    qks = [_bdot_nt(p["q"], p["k"]) * w for p, w in zip(problems, ws)]
    intras = [_bdot(qk, p["v_ext"]) for qk, p in zip(qks, problems)]
    kvs = [_bdot(gk_t, p["v_ext"]) for gk_t, p in zip(gk_ts, problems)]
    dh = chunks[0][0].shape[1]
    return [(intra, kv, jnp.broadcast_to(mu - p["b_col"], (t, dh)), jnp.broadcast_to(mu, (t, dh)), bl, gm)
            for intra, kv, p, mu, bl, gm in zip(intras, kvs, problems, mus, b_lasts, gammas)]


def _mlstm_kernel(q_ref, k_ref, v_ref, o_ref, z_ref, gt_ref, cwq_ref, cwk_ref, ng_ref, out_ref,
                  qa_scr, ka_scr, h_scr, intra_scr, kv_scr, delta_scr, mu_scr, tail_scr, *, n_ctx):
    s = q_ref.shape[1]
    dh = q_ref.shape[2]
    t = MIX_CHUNK
    n_chunks = s // t
    n_ctx_chunks = n_ctx // t
    k_scale = dh ** -0.5

    p_rows = MIX_PREP_ROWS

    def prep(j, carry):
        t0 = pl.multiple_of(j * p_rows, p_rows)
        for src, cw, dst, scale in ((q_ref, cwq_ref, qa_scr, 1.0), (k_ref, cwk_ref, ka_scr, k_scale)):
            cur, down, up = _chunk_with_neighbors(src, t0, p_rows, s, s - n_ctx)
            conv = down * cw[0:1, :] + cur * cw[1:2, :] + up * cw[2:3, :]
            dst[pl.ds(t0, p_rows), :] = _silu(conv) * scale
        return carry

    lax.fori_loop(0, s // p_rows, prep, 0)

    ones_col = jnp.ones((t, dh), F32)
    causal = [_iota((t, t), 1) <= _iota((t, t), 0), _iota((t, t), 1) >= _iota((t, t), 0)]

    def operators(gi, carry):
        chunks, where = [], []
        for kk in range(MLSTM_GROUP):
            chunk = gi * MLSTM_GROUP + kk
            sl = pl.ds(pl.multiple_of(chunk * t, t), t)
            v_ext = jnp.concatenate([v_ref[0, sl, :].astype(F32), ones_col], axis=1)
            chunks.append((qa_scr[sl, :], ka_scr[sl, :], v_ext, gt_ref[0, 0, :, sl]))
            where += [(0, chunk, sl), (1, chunk, sl)]
        for (d, chunk, sl), (intra, kv, delta, mu, b_last, gamma) in zip(where, _mlstm_chunk_operators(chunks, causal)):
            intra_scr[d, sl, :] = intra
            kv_scr[d, chunk] = kv
            delta_scr[d, sl, :] = delta
            mu_scr[d, sl, :] = mu
            tail_scr[d, chunk] = jnp.concatenate([jnp.broadcast_to(b_last, (1, dh)), jnp.broadcast_to(gamma, (1, dh))],
                                                 axis=0)
        return carry

    lax.fori_loop(0, n_chunks // MLSTM_GROUP, operators, 0)

    def scan(trip, carry):
        steps = []
        for k in range(SCAN_UNROLL):
            chunks = _scan_chunks(trip * SCAN_UNROLL + k, n_chunks, n_ctx_chunks)
            new = []
            for d, (chunk, (c_ext, m)) in enumerate(zip(chunks, carry)):
                steps.append((d, chunk, pl.ds(pl.multiple_of(chunk * t, t), t), c_ext, m))
                tail = tail_scr[d, chunk]
                b_last, gamma = tail[0:1, 0:1], tail[1:2, 0:1]
                m_new = jnp.maximum(b_last + m, gamma)
                new.append((jnp.exp(b_last + m - m_new) * c_ext + jnp.exp(gamma - m_new) * kv_scr[d, chunk], m_new))
            carry = tuple(new)
        inters = [_bdot(qa_scr[sl, :], c_ext) for _, _, sl, c_ext, _ in steps]
        for (d, chunk, sl, _, m), inter in zip(steps, inters):
            z = delta_scr[d, sl, :] - m
            s_inter = jnp.exp(-jnp.maximum(z, 0.0))
            s_intra = jnp.exp(jnp.minimum(z, 0.0))
            floor = jnp.exp(jnp.minimum(z, 0.0) - mu_scr[d, sl, :])
            intra = intra_scr[d, sl, :]
            num = s_inter * inter[:, :dh] + s_intra * intra[:, :dh]
            den = s_inter * inter[:, dh:] + s_intra * intra[:, dh:]
            h_scr[d, sl, :] = num / jnp.maximum(jnp.abs(den), floor)
        return carry

    zero = (jnp.zeros((dh, 2 * dh), F32), jnp.zeros((1, 1), F32))
    lax.fori_loop(0, n_chunks // SCAN_UNROLL, scan, (zero, zero))

    f_rows = MIX_FINISH_ROWS

    def finish(j, carry):
        sl = pl.ds(pl.multiple_of(j * f_rows, f_rows), f_rows)
        h = h_scr[0, sl, :] + h_scr[1, sl, :]
        y = h * lax.rsqrt(jnp.mean(h * h, axis=-1, keepdims=True) + RMS_EPS) * ng_ref[...]
        gated = y * _sigmoid(o_ref[0, sl, :].astype(F32)) * _silu(z_ref[0, sl, :].astype(F32))
        out_ref[0, sl, :] = gated.astype(out_ref.dtype)
        return carry

    lax.fori_loop(0, s // f_rows, finish, 0)


def _mlstm(u, gt, conv_w, norm_g, n_ctx):
    bsz, s, _ = u.shape
    nh = MLSTM_N_HEADS
    dh = LANES
    width = nh * dh
    n_chunks = s // MIX_CHUNK
    assert n_chunks % MLSTM_GROUP == 0 and n_chunks % SCAN_UNROLL == 0 and n_ctx % MIX_CHUNK == 0
    col = lambda k: pl.BlockSpec((1, s, dh), lambda b, h, k=k: (b, 0, k * nh + h))
    par = lambda k: pl.BlockSpec((3, dh), lambda b, h, k=k: (0, k * nh + h))
    return pl.pallas_call(
        functools.partial(_mlstm_kernel, n_ctx=n_ctx),
        grid=(bsz, nh),
        in_specs=[col(0), col(1), col(2), col(3), col(4),
                  pl.BlockSpec((1, 1, SUBLANES, s), lambda b, h: (b, h, 0, 0)),
                  par(0), par(1),
                  pl.BlockSpec((1, dh), lambda b, h: (0, h))],
        out_specs=pl.BlockSpec((1, s, dh), lambda b, h: (b, 0, h)),
        out_shape=jax.ShapeDtypeStruct((bsz, s, width), BF16),
        scratch_shapes=[pltpu.VMEM((s, dh), F32), pltpu.VMEM((s, dh), F32), pltpu.VMEM((2, s, dh), F32),
                        pltpu.VMEM((2, s, 2 * dh), F32), pltpu.VMEM((2, n_chunks, dh, 2 * dh), F32),
                        pltpu.VMEM((2, s, dh), F32), pltpu.VMEM((2, s, dh), F32),
                        pltpu.VMEM((2, n_chunks, 2, dh), F32)],
        compiler_params=_compiler_params(("parallel", "arbitrary")),
        name="mlstm_mixer",
    )(u, u, u, u, u, gt, conv_w, conv_w, norm_g.reshape(1, width))


def _head_stack(x, lane_lo):
    return jnp.concatenate([jnp.where(lane_lo, x, 0.0), jnp.where(lane_lo, 0.0, x)], axis=0)


def _half_rows(x, c, upper):
    start = c if upper else 0
    return jnp.concatenate([x[r + start:r + start + c] for r in range(0, x.shape[0], 2 * c)], axis=0)


def _merge_rows(other, part, c, upper):
    pieces = []
    for k in range(part.shape[0] // c):
        pair = (other[k * c:(k + 1) * c], part[k * c:(k + 1) * c])
        pieces += pair if upper else pair[::-1]
    return jnp.concatenate(pieces, axis=0)


def _spread_rows(part, c, upper):
    return _merge_rows(jnp.zeros_like(part), part, c, upper)


def _rwkv_chunk_operators(problems, consts, eye, lane_lo):
    t, w = problems[0][0].shape
    n2 = 2 * t
    stack = lambda x: _head_stack(x, lane_lo)
    zeros = jnp.zeros((n2, w), F32)
    dirs = [p[6] for p in problems]
    rid = _iota((t, w), 0)
    cums = [p[3] for p in problems]
    sh = 1
    while sh < t:
        cums = [x + (jnp.where(rid < t - sh, pltpu.roll(x, t - sh, 0), 0.0) if d == 1 else
                     jnp.where(rid >= sh, pltpu.roll(x, sh, 0), 0.0)) for x, d in zip(cums, dirs)]
        sh *= 2
    pre = []
    for (r, v, kk, lw, ka, kt, d), cum in zip(problems, cums):
        last = 0 if d == 1 else t - 1
        cum_end = cum[last:last + 1]
        e_inv = jnp.exp(-cum)
        e_end = jnp.exp(cum_end - cum)
        a_s = stack(-kk * jnp.exp(cum - lw))
        r_s = stack(r * jnp.exp(cum))
        pre.append(dict(a_s=a_s, r_s=r_s, vs=stack(v), g=jnp.exp(cum_end),
                        ar=jnp.concatenate([a_s, r_s], axis=0),
                        bk=jnp.concatenate([stack(ka * e_inv), stack(kt * e_inv)], axis=0),
                        bk_end=jnp.concatenate([stack(ka * e_end), stack(kt * e_end)], axis=0)))
    m_alls = [_bdot_nt(q["ar"], q["bk"]) for q in pre]
    m_abs = [jnp.where(consts[d]["strict"], m[:n2, :n2], 0.0) for m, d in zip(m_alls, dirs)]
    m_aks = [jnp.where(consts[d]["strict"], m[:n2, n2:], 0.0) for m, d in zip(m_alls, dirs)]
    m_lows = [jnp.where(consts[d]["incl2"], m[n2:, :], 0.0) for m, d in zip(m_alls, dirs)]
    invs = [eye + jnp.where(consts[d]["merges"][0][1], m, 0.0) for m, d in zip(m_abs, dirs)]
    for level in range(1, len(consts[0]["merges"])):
        c = consts[0]["merges"][level][0]
        if c < SUBLANES:
            inner = [_bdot(jnp.where(consts[d]["merges"][level][1], m, 0.0), x) for m, x, d in zip(m_abs, invs, dirs)]
            invs = [x + _bdot(x, y) for x, y in zip(invs, inner)]
        else:
            ups = [d == 0 for d in dirs]
            c_rows = [jnp.where(consts[d]["merges"][level][1], _half_rows(m, c, up), 0.0) for m, d, up in zip(m_abs, dirs, ups)]
            inner = [_bdot(cr, x) for cr, x in zip(c_rows, invs)]
            x_rows = [_half_rows(x, c, up) for x, up in zip(invs, ups)]
            upd = [xr + _bdot(xr, _spread_rows(y, c, up)) for xr, y, up in zip(x_rows, inner, ups)]
            invs = [_merge_rows(_half_rows(x, c, not up), u, c, up) for x, u, up in zip(invs, upd, ups)]
    mv = [_bdot(m, q["vs"]) for m, q in zip(m_aks, pre)]
    solved = [_bdot(x, jnp.concatenate([q["a_s"], y], axis=1)) for x, q, y in zip(invs, pre, mv)]
    zms = [jnp.concatenate([sv, jnp.concatenate([zeros, q["vs"]], axis=1)], axis=0) for sv, q in zip(solved, pre)]
    ry1s = [jnp.concatenate([q["r_s"], zeros], axis=1) + _bdot(m, z) for q, m, z in zip(pre, m_lows, zms)]
    pqs = [_bdot_tn(z, q["bk_end"]) for z, q in zip(zms, pre)]
    out = []
    for ry1, pq, q in zip(ry1s, pqs, pre):
        folded = ry1[:t] + ry1[t:]
        out.append((folded[:, :w], folded[:, w:], pq[:w], pq[w:], q["g"]))
    return out


def _rwkv_kernel(rr_ref, rk_ref, rv_ref, rz_ref, wd_ref, ad_ref, mu_ref, kk_ref, ka_ref, rkk_ref,
                 lnw_ref, lnb_ref, w0_ref, a0_ref, wup_ref, aup_ref, out_ref,
                 r_scr, v_scr, kk_scr, lw_scr, ka_scr, kt_scr, bonus_scr, y_scr, ry_scr, pp_scr, qq_scr, qt_scr, g_scr,
                 *, n_ctx):
    s = rr_ref.shape[1]
    w = rr_ref.shape[2]
    p_rows = RWKV_PREP_ROWS
    t = RWKV_CHUNK
    n_chunks = s // t
    n_ctx_chunks = n_ctx // t
    head_sum = ((_iota((w, w), 0) // RWKV_HEAD) == (_iota((w, w), 1) // RWKV_HEAD)).astype(BF16)
    inv_head = 1.0 / RWKV_HEAD

    def prep(j, carry):
        sls = [pl.ds(pl.multiple_of((j * RWKV_PREP_BLOCKS + k) * p_rows, p_rows), p_rows) for k in range(RWKV_PREP_BLOCKS)]
        mixed = []
        for sl in sls:
            shifted = []
            for idx, src in enumerate((rr_ref, rk_ref, rv_ref)):
                cur, down, up = _chunk_with_neighbors(src, sl.start, p_rows, s, s - n_ctx)
                shifted.append(cur + mu_ref[idx:idx + 1, :] * (0.5 * (down + up) - cur))
            mixed.append(shifted)
        kks = [kr * kk_ref[...] for _, kr, _ in mixed]
        norms = [jnp.sqrt(_dot_sel(kk * kk, head_sum)) for kk in kks]
        kks = [kk / jnp.maximum(norm, 1e-12) for kk, norm in zip(kks, norms)]
        w_raws = [_bdot(jnp.tanh(wd_ref[0, sl, :].astype(F32)), wup_ref[0]) + w0_ref[0] for sl in sls]
        a_raws = [_bdot(ad_ref[0, sl, :], aup_ref[0]) + a0_ref[0] for sl in sls]
        kt_sums = []
        for sl, (r, kr, v), kk, w_raw, a_raw in zip(sls, mixed, kks, w_raws, a_raws):
            a = _sigmoid(a_raw)
            lw = -math.exp(-0.5) * _sigmoid(w_raw)
            kt_sum = jnp.zeros_like(kr)
            for d in range(2):
                a_d = a[:, d * w:(d + 1) * w]
                kt_d = kr * (1.0 + (a_d - 1.0) * ka_ref[...])
                kt_sum = kt_sum + kt_d
                lw_scr[d, sl, :] = lw[:, d * w:(d + 1) * w]
                ka_scr[d, sl, :] = kk * a_d
                kt_scr[d, sl, :] = kt_d
            kt_sums.append(kt_sum)
            r_scr[sl, :] = r
            v_scr[sl, :] = v
            kk_scr[sl, :] = kk
        coefs = [_dot_sel(r * kt_sum * rkk_ref[...], head_sum) for (r, _, _), kt_sum in zip(mixed, kt_sums)]
        for sl, (_, _, v), coef in zip(sls, mixed, coefs):
            bonus_scr[sl, :] = coef * v
        return carry

    lax.fori_loop(0, s // (p_rows * RWKV_PREP_BLOCKS), prep, 0)

    n2 = 2 * t
    r_i = _iota((n2, n2), 0)
    c_i = _iota((n2, n2), 1)
    same = (r_i // t) == (c_i // t)
    rt = r_i % t
    ct = c_i % t
    eye = (r_i == c_i).astype(F32)
    lane_lo = _iota((t, w), 1) < RWKV_HEAD
    consts = []
    for reverse in (False, True):
        strict = jnp.logical_and(same, (ct > rt) if reverse else (ct < rt))
        incl = jnp.logical_and(same, (ct >= rt) if reverse else (ct <= rt))
        merges = []
        c = 1
        while c < t:
            hi_r = (r_i % (2 * c)) >= c
            hi_c = (c_i % (2 * c)) >= c
            cross = jnp.logical_and(hi_c, jnp.logical_not(hi_r)) if reverse else jnp.logical_and(hi_r, jnp.logical_not(hi_c))
            mask = jnp.logical_and((r_i // (2 * c)) == (c_i // (2 * c)), cross)
            merges.append((c, _half_rows(mask, c, not reverse)) if c >= SUBLANES else (c, mask))
            c *= 2
        consts.append(dict(strict=strict, incl2=jnp.concatenate([incl, incl], axis=1), merges=merges))

    def operators(gi, carry):
        problems, where = [], []
        for k in range(RWKV_GROUP):
            chunk = gi * RWKV_GROUP + k
            sl = pl.ds(pl.multiple_of(chunk * t, t), t)
            r, v, kk = r_scr[sl, :], v_scr[sl, :], kk_scr[sl, :]
            for d in range(2):
                problems.append((r, v, kk, lw_scr[d, sl, :], ka_scr[d, sl, :], kt_scr[d, sl, :], d))
                where.append((d, chunk, sl))
        ops = _rwkv_chunk_operators(problems, consts, eye, lane_lo)
        for (d, chunk, sl), (ry, y1, pt, qt, g) in zip(where, ops):
            ry_scr[d, sl, :] = ry.astype(BF16)
            y_scr[d, sl, :] = y1
            qt_scr[d, chunk] = qt
            g_scr[d, chunk] = g
        pairs = []
        for k in range(0, RWKV_GROUP, 2):
            for d in range(2):
                first, second = (ops[2 * k + d], ops[2 * (k + 1) + d]) if d == 0 else (ops[2 * (k + 1) + d], ops[2 * k + d])
                pairs.append((d, (gi * RWKV_GROUP + k) // 2, first, second))
        prods = [_bdot(jnp.concatenate([eye * a[4] + a[2], a[3]], axis=0), b[2]) for _, _, a, b in pairs]
        for (d, pair, a, b), prod in zip(pairs, prods):
            pp_scr[d, pair] = jnp.concatenate([a[2], prod[:w] + a[2] * b[4]], axis=1).astype(BF16)
            qq_scr[d, pair] = a[3] * b[4] + prod[w:] + b[3]
        return carry

    lax.fori_loop(0, n_chunks // RWKV_GROUP, operators, 0)

    def scan(trip, carry):
        outs = []
        for k in range(RWKV_SCAN_UNROLL):
            step = trip * RWKV_SCAN_UNROLL + k
            first = _scan_chunks(2 * step, n_chunks, n_ctx_chunks)
            second = _scan_chunks(2 * step + 1, n_chunks, n_ctx_chunks)
            pair = [jnp.minimum(a, b) // 2 for a, b in zip(first, second)]
            moved = [_bdot(ht, pp_scr[d, p]) for d, (p, ht) in enumerate(zip(pair, carry))]
            mids = [ht * g_scr[d, a] + mv[:, :w] + qt_scr[d, a] for d, (a, ht, mv) in enumerate(zip(first, carry, moved))]
            for d, (a, b, ht, mid) in enumerate(zip(first, second, carry, mids)):
                sl_a = pl.ds(pl.multiple_of(a * t, t), t)
                sl_b = pl.ds(pl.multiple_of(b * t, t), t)
                outs.append((d, sl_a, _bdot_nt(ry_scr[d, sl_a, :], ht)))
                outs.append((d, sl_b, _bdot_nt(ry_scr[d, sl_b, :], mid)))
            carry = tuple(ht * (g_scr[d, a] * g_scr[d, b]) + mv[:, w:] + qq_scr[d, p]
                          for d, (a, b, p, ht, mv) in enumerate(zip(first, second, pair, carry, moved)))
        for d, sl, y in outs:
            y_scr[d, sl, :] = y + y_scr[d, sl, :]
        return carry

    zero_state = jnp.zeros((w, w), F32)
    lax.fori_loop(0, n_chunks // (2 * RWKV_SCAN_UNROLL), scan, (zero_state, zero_state))

    f_rows = RWKV_FINISH_ROWS

    def finish(j, carry):
        sl = pl.ds(pl.multiple_of(j * f_rows, f_rows), f_rows)
        y = y_scr[0, sl, :] + y_scr[1, sl, :] + bonus_scr[sl, :]
        mu = _dot_sel(y, head_sum) * inv_head
        yc = y - mu
        var = _dot_sel(yc * yc, head_sum) * inv_head
        yn = yc * lax.rsqrt(var + RWKV_LN_EPSILON) * lnw_ref[...] + lnb_ref[...]
        out_ref[0, sl, :] = (yn * _silu(rz_ref[0, sl, :].astype(F32))).astype(out_ref.dtype)
        return carry

    lax.fori_loop(0, s // f_rows, finish, 0)


def _rwkv(u, col0, p, n_ctx):
    bsz, s, _ = u.shape
    w = LANES
    width = p["mu"].shape[1]
    n_pairs = width // w
    base = col0 // w
    col = lambda k: pl.BlockSpec((1, s, w), lambda b, h, k=k: (b, 0, base + k * n_pairs + h))
    lora = lambda k: pl.BlockSpec((1, s, w), lambda b, h, k=k: (b, 0, base + 4 * n_pairs + k))
    vec = lambda rows: pl.BlockSpec((rows, w), lambda b, h: (0, h))
    cat = pl.BlockSpec((1, 1, 2 * w), lambda b, h: (h, 0, 0))
    up = pl.BlockSpec((1, w, 2 * w), lambda b, h: (h, 0, 0))
    seq = pltpu.VMEM((s, w), F32)
    seq2 = pltpu.VMEM((2, s, w), F32)
    n_chunks = s // RWKV_CHUNK
    assert n_chunks % RWKV_GROUP == 0 and s % (RWKV_PREP_ROWS * RWKV_PREP_BLOCKS) == 0 and n_ctx % RWKV_PREP_ROWS == 0
    assert s % RWKV_FINISH_ROWS == 0 and n_chunks % (2 * RWKV_SCAN_UNROLL) == 0
    assert RWKV_GROUP % 2 == 0 and (n_ctx // RWKV_CHUNK) % 2 == 0
    operators = [pltpu.VMEM((2, s, w), BF16), pltpu.VMEM((2, n_chunks // 2, w, 2 * w), BF16),
                 pltpu.VMEM((2, n_chunks // 2, w, w), F32),
                 pltpu.VMEM((2, n_chunks, w, w), F32), pltpu.VMEM((2, n_chunks, 1, w), F32)]
    return pl.pallas_call(
        functools.partial(_rwkv_kernel, n_ctx=n_ctx),
        grid=(bsz, n_pairs),
        in_specs=[col(0), col(1), col(2), col(3), lora(0), lora(1),
                  vec(3), vec(1), vec(1), vec(1), vec(1), vec(1), cat, cat, up, up],
        out_specs=pl.BlockSpec((1, s, w), lambda b, h: (b, 0, h)),
        out_shape=jax.ShapeDtypeStruct((bsz, s, width), BF16),
        scratch_shapes=[seq, seq, seq, seq2, seq2, seq2, seq, seq2] + operators,
        compiler_params=_compiler_params(("parallel", "arbitrary")),
        name="rwkv7_mixer",
    )(u, u, u, u, u, u, p["mu"], p["k_k"], p["k_a"], p["r_k"], p["ln_w"], p["ln_b"],
      p["w0"], p["a0"], p["w_up"], p["a_up"])


def _hgrn_level_masks(t, w):
    rid = _iota((t, w), 0)
    r_i = _iota((t, t), 0)
    c_i = _iota((t, t), 1)
    levels = []
    c = 1
    while c < t:
        same_block = (r_i // (2 * c)) == (c_i // (2 * c))
        up_r = (r_i % (2 * c)) >= c
        up_c = (c_i % (2 * c)) >= c
        pair = [jnp.logical_and(same_block, jnp.logical_and(up_r, jnp.logical_not(up_c))),
                jnp.logical_and(same_block, jnp.logical_and(up_c, jnp.logical_not(up_r)))]
        levels.append((c, (rid % (2 * c)) >= c, pair))
        c *= 2
    return levels


def _hgrn_chunk_operators(problems, lb, tris, levels):
    t, w = problems[0][0].shape
    r_i = _iota((t, t), 0)
    c_i = _iota((t, t), 1)
    zero_row = jnp.zeros((1, w), F32)
    dirs = [p[3] for p in problems]
    lgs, ks = [], []
    for q, v, ff, d in problems:
        e = jnp.exp(-jnp.abs(ff))
        big = 1.0 / (1.0 + e)
        small = e / (1.0 + e)
        pos = ff >= 0.0
        lgs.append(jnp.log(lb + (1.0 - lb) * jnp.where(pos, big, small)) * LOG2_E)
        ks.append((1.0 - lb) * jnp.where(pos, small, big))
    bs = [_sel_dot(tris[d], lg) for lg, d in zip(lgs, dirs)]
    befores = [_shifted(b, zero_row, zero_row)[1 if d == 1 else 0] for b, d in zip(bs, dirs)]
    edges = list(bs)
    accs = [jnp.where(r_i == c_i, jnp.sum(p[0] * k, axis=-1, keepdims=True), 0.0) for p, k in zip(problems, ks)]
    for c, upper, pair in levels:
        qts = [p[0] * jnp.exp2(b - before) for p, b, before in zip(problems, bs, befores)]
        kts = [k * jnp.exp2(edge - b) for k, b, edge in zip(ks, bs, edges)]
        prods = [_bdot_nt(qt, kt) for qt, kt in zip(qts, kts)]
        accs = [jnp.where(pair[d], pr, a) for a, pr, d in zip(accs, prods, dirs)]
        for i, d in enumerate(dirs):
            if d == 1:
                befores[i] = jnp.where(upper, befores[i], pltpu.roll(befores[i], t - c, 0))
                edges[i] = jnp.where(upper, pltpu.roll(edges[i], c, 0), edges[i])
            else:
                befores[i] = jnp.where(upper, pltpu.roll(befores[i], c, 0), befores[i])
                edges[i] = jnp.where(upper, edges[i], pltpu.roll(edges[i], t - c, 0))
    o_intras = [_bdot(a, p[1]) for a, p in zip(accs, problems)]
    b_ends = [b[(0 if d == 1 else t - 1):(1 if d == 1 else t)] for b, d in zip(bs, dirs)]
    kvs = [_bdot_tn(p[1], k * jnp.exp2(be - b)) for p, k, b, be in zip(problems, ks, bs, b_ends)]
    return [(p[0] * jnp.exp2(b), oi, kv, jnp.exp2(be)) for p, b, oi, kv, be in zip(problems, bs, o_intras, kvs, b_ends)]


def _hgrn_kernel(q_ref, i_ref, ff_ref, fb_ref, z_ref, lb_ref, ng_ref, out_ref, o_scr, qe_scr, kv_scr, g_scr,
                 *, n_ctx, layer):
    s = q_ref.shape[1]
    dh = q_ref.shape[2]
    t = MIX_CHUNK
    n_chunks = s // t
    n_ctx_chunks = n_ctx // t
    lbs = lb_ref[...]
    ex = jnp.exp(lbs - jnp.max(lbs, axis=0, keepdims=True))
    probs = ex / jnp.sum(ex, axis=0, keepdims=True)
    csum = probs[0:1]
    for l in range(1, layer + 1):
        csum = csum + probs[l:l + 1]
    lb = csum - probs[0:1]
    tri_r = _iota((t, t), 0)
    tri_c = _iota((t, t), 1)
    tris = [(tri_c <= tri_r).astype(BF16), (tri_c >= tri_r).astype(BF16)]
    levels = _hgrn_level_masks(t, dh)

    def operators(gi, carry):
        problems, where = [], []
        for kk in range(HGRN_GROUP):
            chunk = gi * HGRN_GROUP + kk
            sl = pl.ds(pl.multiple_of(chunk * t, t), t)
            q, v = q_ref[0, sl, :].astype(F32), i_ref[0, sl, :].astype(F32)
            for d, f_ref in enumerate((ff_ref, fb_ref)):
                problems.append((q, v, f_ref[0, sl, :].astype(F32), d))
                where.append((d, chunk, sl))
        for (d, chunk, sl), (qe, o_intra, kv, g) in zip(where, _hgrn_chunk_operators(problems, lb, tris, levels)):
            qe_scr[d, sl, :] = qe.astype(BF16)
            o_scr[d, sl, :] = o_intra
            kv_scr[d, chunk] = kv
            g_scr[d, chunk] = g
        return carry

    lax.fori_loop(0, n_chunks // HGRN_GROUP, operators, 0)

    def scan(trip, carry):
        steps = []
        for k in range(HGRN_SCAN_UNROLL):
            chunks = _scan_chunks(trip * HGRN_SCAN_UNROLL + k, n_chunks, n_ctx_chunks)
            steps += [(d, pl.ds(pl.multiple_of(chunk * t, t), t), st) for d, (chunk, st) in enumerate(zip(chunks, carry))]
            carry = tuple(st * g_scr[d, chunk] + kv_scr[d, chunk] for d, (chunk, st) in enumerate(zip(chunks, carry)))
        inters = [_bdot_nt(qe_scr[d, sl, :], st) for d, sl, st in steps]
        for (d, sl, _), inter in zip(steps, inters):
            o_scr[d, sl, :] = o_scr[d, sl, :] + inter
        return carry

    zero_state = jnp.zeros((dh, dh), F32)
    lax.fori_loop(0, n_chunks // HGRN_SCAN_UNROLL, scan, (zero_state, zero_state))

    f_rows = MIX_FINISH_ROWS

    def finish(j, carry):
        sl = pl.ds(pl.multiple_of(j * f_rows, f_rows), f_rows)
        o = o_scr[0, sl, :] + o_scr[1, sl, :]
        y = o * lax.rsqrt(jnp.mean(o * o, axis=-1, keepdims=True) + RMS_EPS) * ng_ref[...]
        out_ref[0, sl, :] = (y * _silu(z_ref[0, sl, :].astype(F32))).astype(out_ref.dtype)
        return carry

    lax.fori_loop(0, (s - n_ctx) // f_rows, finish, 0)


def _hgrn(u, lb_all, norm_g, n_ctx, layer):
    bsz, s, _ = u.shape
    nh = HGRN_N_HEADS
    dh = LANES
    width = nh * dh
    depth = lb_all.shape[0]
    n_chunks = s // MIX_CHUNK
    assert n_chunks % HGRN_GROUP == 0 and n_chunks % HGRN_SCAN_UNROLL == 0 and n_ctx % MIX_CHUNK == 0
    col = lambda k: pl.BlockSpec((1, s, dh), lambda b, h, k=k: (b, 0, k * nh + h))
    return pl.pallas_call(
        functools.partial(_hgrn_kernel, n_ctx=n_ctx, layer=layer),
        grid=(bsz, nh),
        in_specs=[col(0), col(1), col(2), col(3), col(4),
                  pl.BlockSpec((depth, dh), lambda b, h: (0, h)),
                  pl.BlockSpec((1, dh), lambda b, h: (0, h))],
        out_specs=pl.BlockSpec((1, s - n_ctx, dh), lambda b, h: (b, 0, h)),
        out_shape=jax.ShapeDtypeStruct((bsz, s - n_ctx, width), BF16),
        scratch_shapes=[pltpu.VMEM((2, s, dh), F32), pltpu.VMEM((2, s, dh), BF16),
                        pltpu.VMEM((2, n_chunks, dh, dh), F32), pltpu.VMEM((2, n_chunks, 1, dh), F32)],
        compiler_params=_compiler_params(("parallel", "arbitrary")),
        name="hgrn2_mixer",
    )(u, u, u, u, u, lb_all, norm_g.reshape(1, width))


def _hyena_filter_kernel(z_ref, w1_ref, b1_ref, w2_ref, b2_ref, w3f_ref, w3b_ref, dl_ref, hf_ref, hb_ref, hid_scr):
    hp = functools.partial(jnp.dot, precision=lax.Precision.HIGHEST, preferred_element_type=F32)
    n = z_ref.shape[0]

    @pl.when(pl.program_id(0) == 0)
    def _():
        first = jnp.sin(hp(z_ref[...], w1_ref[...]) + b1_ref[...])
        hid_scr[...] = jnp.sin(hp(first, w2_ref[...]) + b2_ref[...])

    hid = hid_scr[...]
    pos = _iota((n, 1), 0).astype(F32) * (1.0 / n)
    window = jnp.exp(-pos * dl_ref[...]) + HYENA_SHIFT
    f0 = hp(hid, w3f_ref[...]) * window
    f1 = hp(hid, w3b_ref[...]) * window
    nrm = jnp.sum(jnp.abs(f0), axis=0, keepdims=True) + jnp.sum(jnp.abs(f1), axis=0, keepdims=True)
    hf_ref[...] = f0 / nrm
    hb_ref[...] = f1 / nrm


def _hyena_filters(n, w1, b1, w2, b2, w3, width):
    pos = np.arange(n, dtype=np.float64)
    bands = np.linspace(1e-4, HYENA_N_BANDS - 1, HYENA_N_BANDS)
    ang = (2.0 * math.pi / n) * pos[:, None] * bands
    z = np.concatenate([(pos / n)[:, None], np.cos(ang), np.sin(ang)], axis=-1)
    z = np.pad(z, ((0, 0), (0, LANES - z.shape[1]))).astype(np.float32)
    max_decay = math.log(HYENA_TGT) / HYENA_FAST
    min_decay = math.log(HYENA_TGT) / HYENA_SLOW
    deltas = np.abs(np.linspace(min_decay, max_decay, width)).astype(np.float32)[None]
    feat, hid = w1.shape
    w1p = jnp.pad(w1, ((0, LANES - feat), (0, LANES - hid)))
    w2p = jnp.pad(w2, ((0, LANES - hid), (0, LANES - hid)))
    w3p = jnp.pad(w3, ((0, LANES - hid), (0, 0)))
    b1p = jnp.pad(b1, (0, LANES - hid)).reshape(1, LANES)
    b2p = jnp.pad(b2, (0, LANES - hid)).reshape(1, LANES)
    n_tiles = width // LANES
    full = lambda shape: pl.BlockSpec(shape, lambda j: (0, 0))
    out = pl.BlockSpec((n, LANES), lambda j: (0, j))
    hf, hb = pl.pallas_call(
        _hyena_filter_kernel,
        grid=(n_tiles,),
        in_specs=[full((n, LANES)), full((LANES, LANES)), full((1, LANES)), full((LANES, LANES)), full((1, LANES)),
                  pl.BlockSpec((LANES, LANES), lambda j: (0, j)),
                  pl.BlockSpec((LANES, LANES), lambda j: (0, n_tiles + j)),
                  pl.BlockSpec((1, LANES), lambda j: (0, j))],
        out_specs=[out, out],
        out_shape=[jax.ShapeDtypeStruct((n, width), F32)] * 2,
        scratch_shapes=[pltpu.VMEM((n, LANES), F32)],
        compiler_params=_compiler_params(("arbitrary",)),
        name="hyena_filters",
    )(jnp.asarray(z), w1p, b1p, w2p, b2p, w3p, w3p, jnp.asarray(deltas))
    return jnp.concatenate([hf, hb], axis=1)


def _hyena_pre_kernel(yv_ref, y0_ref, y1_ref, yz_ref, swv_ref, sw0_ref, sw1_ref, sbv_ref, sb0_ref, sb1_ref,
                      p_ref, e_ref, *, n_ctx):
    s = yv_ref.shape[1]
    rows = MIX_CHUNK

    def body(j, carry):
        t0 = pl.multiple_of(j * rows, rows)
        conv = []
        for src, sw, sb in ((yv_ref, swv_ref, sbv_ref), (y0_ref, sw0_ref, sb0_ref), (y1_ref, sw1_ref, sb1_ref)):
            cur, down, up = _chunk_with_neighbors(src, t0, rows, s, s - n_ctx)
            conv.append(down * sw[0:1, :] + cur * sw[1:2, :] + up * sw[2:3, :] + sb[...])
        v, x0, x1 = conv
        p = x1 * v
        o0 = pl.multiple_of(j * rows, rows)
        p_ref[0, pl.ds(o0, rows), :] = p.astype(BF16)
        e_ref[0, pl.ds(o0, rows), :] = (x0 * _silu(yz_ref[0, pl.ds(t0, rows), :].astype(F32))).astype(BF16)
        return carry

    lax.fori_loop(0, (s - n_ctx) // rows, body, 0)


def _hyena_pre(u, col0, short_w, short_b, n_ctx):
    bsz, s, _ = u.shape
    w = short_b.shape[0] // 3
    tiles = w // LANES
    base = col0 // LANES
    n = s - n_ctx
    col = lambda k: pl.BlockSpec((1, s, LANES), lambda b, j, k=k: (b, 0, base + k * tiles + j))
    par = lambda rows, k: pl.BlockSpec((rows, LANES), lambda b, j, k=k: (0, k * tiles + j))
    out = pl.BlockSpec((1, n, LANES), lambda b, j: (b, 0, j))
    sb = short_b.reshape(1, 3 * w)
    return pl.pallas_call(
        functools.partial(_hyena_pre_kernel, n_ctx=n_ctx),
        grid=(bsz, tiles),
        in_specs=[col(0), col(1), col(2), col(3),
                  par(3, 0), par(3, 1), par(3, 2), par(1, 0), par(1, 1), par(1, 2)],
        out_specs=[out, out],
        out_shape=[jax.ShapeDtypeStruct((bsz, n, w), BF16)] * 2,
        compiler_params=_compiler_params(("parallel", "arbitrary")),
        name="hyena_short_conv",
    )(u, u, u, u, short_w, short_w, short_w, sb, sb, sb)


def _dft_tables(n):
    big = 2 * n
    half = DFT_TILE // 2
    idx = jnp.arange(n, dtype=jnp.int32)
    split = DFT_SPLIT
    lo = jnp.arange(split, dtype=jnp.int32)
    hi = jnp.arange(n // split, dtype=jnp.int32)
    ang_lo = ((lo[:, None] * idx[None, :]) % big).astype(F32) * (2.0 * math.pi / big)
    ang_hi = ((hi[:, None] * idx[None, :]) % (big // split)).astype(F32) * (2.0 * math.pi * split / big)
    c_lo, s_lo = jnp.cos(ang_lo)[None], jnp.sin(ang_lo)[None]
    c_hi, s_hi = jnp.cos(ang_hi)[:, None], jnp.sin(ang_hi)[:, None]
    cos = (c_hi * c_lo - s_hi * s_lo).reshape(n, n)
    sin = (s_hi * c_lo + c_hi * s_lo).reshape(n, n)
    alt = jnp.where(idx % 2 == 0, 1.0, -1.0).astype(F32)
    first_row = (idx == 0)[:, None]
    im = jnp.where(first_row, alt[None, :], -sin)
    fwd = jnp.stack([cos.reshape(n // half, half, n), im.reshape(n // half, half, n)], axis=1).reshape(big, n)
    weight = jnp.where(first_row, 1.0, 2.0) * (1.0 / big)
    weight = jnp.stack([weight.reshape(n // half, half, 1)] * 2, axis=1).reshape(big, 1)
    return fwd.astype(BF16), (fwd * weight).T.astype(BF16)


def _spectrum_kernel(f_ref, lo_ref, hi_ref, bias_ref, o_ref):
    half = DFT_TILE // 2
    w = lo_ref.shape[1] // 2
    lo = lo_ref[...].astype(BF16)
    acc_lo = jnp.dot(f_ref[...], lo, preferred_element_type=F32)
    acc_hi = jnp.dot(f_ref[...], hi_ref[...].astype(BF16), preferred_element_type=F32)
    rows = _iota((DFT_TILE, w), 0)
    sign = jnp.where(rows % 2 == 1, -1.0, 1.0)
    real_slot = jnp.logical_or(rows < half, jnp.logical_and(rows == half, pl.program_id(0) == 0))
    conj = lambda x: jnp.where(real_slot, x, -x)
    lag0 = lo[0:1, :].astype(F32)
    a, c_all = acc_lo[:, :w], acc_lo[:, w:]
    b, d = acc_hi[:, :w], acc_hi[:, w:]
    c = c_all - jnp.where(real_slot, lag0[:, w:], 0.0)
    o_ref[:, 0:w] = a + conj(c_all) + jnp.where(real_slot, bias_ref[...], 0.0)
    o_ref[:, w:2 * w] = b + sign * (a - jnp.where(real_slot, lag0[:, :w], 0.0))
    o_ref[:, 2 * w:] = sign * conj(c) + conj(d)


def _filter_spectrum(fwd, hk, bias):
    big, h = fwd.shape
    w = hk.shape[1] // 2
    assert hk.shape[0] == 2 * h
    return pl.pallas_call(
        _spectrum_kernel,
        grid=(big // DFT_TILE,),
        in_specs=[pl.BlockSpec((DFT_TILE, h), lambda i: (i, 0)),
                  pl.BlockSpec((h, 2 * w), lambda i: (0, 0)),
                  pl.BlockSpec((h, 2 * w), lambda i: (1, 0)),
                  pl.BlockSpec((1, w), lambda i: (0, 0))],
        out_specs=pl.BlockSpec((DFT_TILE, 3 * w), lambda i: (i, 0)),
        out_shape=jax.ShapeDtypeStruct((big, 3 * w), F32),
        compiler_params=_compiler_params(("arbitrary",)),
        name="hyena_filter_spectrum",
    )(fwd, hk, hk, bias.reshape(1, w))


def _spectral_products(pairs, packed):
    re = sum(s[0] * k[0] - s[1] * k[1] for s, k in pairs)
    im = sum(s[0] * k[1] + s[1] * k[0] for s, k in pairs)
    if packed is not None:
        re = jnp.where(packed, sum(s[0] * k[0] for s, k in pairs), re)
        im = jnp.where(packed, sum(s[1] * k[1] for s, k in pairs), im)
    return jnp.concatenate([re, im], axis=0).astype(BF16)


def _conv_spectrum_kernel(f_ref, pa_ref, pb_ref, ks_ref, z_ref):
    i = pl.program_id(1)
    half = DFT_TILE // 2
    w = pa_ref.shape[2]
    units = [slice(k * DFT_TILE, (k + 1) * DFT_TILE) for k in range(DFT_UNITS)]
    accs = [(jnp.dot(f_ref[u, :], pa_ref[0], preferred_element_type=F32),
             jnp.dot(f_ref[u, :], pb_ref[0], preferred_element_type=F32)) for u in units]
    for k, (u, (acc_a, acc_b)) in enumerate(zip(units, accs)):
        s_a, s_b = (acc_a[:half], acc_a[half:]), (acc_b[:half], acc_b[half:])
        ks = ks_ref[u, :]
        k_diag, k_below, k_above = [(ks[:half, j * w:(j + 1) * w], ks[half:, j * w:(j + 1) * w]) for j in range(3)]
        packed = jnp.logical_and(_iota((half, w), 0) == 0, i == 0) if k == 0 else None
        z_ref[0, 0, u, :] = _spectral_products([(s_a, k_diag), (s_b, k_above)], packed)
        z_ref[0, 1, u, :] = _spectral_products([(s_a, k_below), (s_b, k_diag)], packed)


def _conv_spectrum(fwd, p16, kspec):
    bsz, n, w = p16.shape
    big, h = fwd.shape
    assert n == 2 * h
    rows = DFT_TILE * DFT_UNITS
    return pl.pallas_call(
        _conv_spectrum_kernel,
        grid=(bsz, big // rows),
        in_specs=[pl.BlockSpec((rows, h), lambda b, i: (i, 0)),
                  pl.BlockSpec((1, h, w), lambda b, i: (b, 0, 0)),
                  pl.BlockSpec((1, h, w), lambda b, i: (b, 1, 0)),
                  pl.BlockSpec((rows, 3 * w), lambda b, i: (i, 0))],
        out_specs=pl.BlockSpec((1, 2, rows, w), lambda b, i: (b, 0, i, 0)),
        out_shape=jax.ShapeDtypeStruct((bsz, 2, big, w), BF16),
        compiler_params=_compiler_params(("parallel", "arbitrary")),
        name="hyena_forward_dft",
    )(fwd, p16, p16, kspec)


def _conv_inverse_kernel(g_ref, z_ref, e_ref, o_ref):
    y = jnp.dot(g_ref[...], z_ref[0, 0], preferred_element_type=F32)
    o_ref[0] = (e_ref[0].astype(F32) * y).astype(o_ref.dtype)


def _conv_inverse(inv, z16, e):
    bsz, _, big, w = z16.shape
    h = inv.shape[0]
    tile = DFT_TILE
    per_half = h // tile
    tok = pl.BlockSpec((1, tile, w), lambda b, j, i: (b, j * per_half + i, 0))
    return pl.pallas_call(
        _conv_inverse_kernel,
        grid=(bsz, 2, per_half),
        in_specs=[pl.BlockSpec((tile, big), lambda b, j, i: (i, 0)),
                  pl.BlockSpec((1, 1, big, w), lambda b, j, i: (b, j, 0, 0)),
                  tok],
        out_specs=tok,
        out_shape=jax.ShapeDtypeStruct((bsz, 2 * h, w), BF16),
        compiler_params=_compiler_params(("parallel", "arbitrary", "arbitrary")),
        name="hyena_inverse_dft",
    )(inv, z16, e)


def _even_weight_layout(w_in, gate_b):
    d = w_in.shape[0]
    mw = MLSTM_N_HEADS * LANES
    g0 = 5 * mw
    g1 = g0 + 4 * MLSTM_N_HEADS
    main = jnp.concatenate([w_in[:, :g0], w_in[:, g1:]], axis=1).astype(BF16)
    wg = w_in[:, g0:g1].reshape(d, 2, 2, MLSTM_N_HEADS)
    wg = jnp.transpose(wg, (3, 1, 2, 0)).reshape(MLSTM_N_HEADS, 4, d)
    rows = MLSTM_N_HEADS * SUBLANES
    wg = jnp.concatenate([wg, jnp.zeros_like(wg)], axis=1).reshape(rows, d).astype(BF16)
    gb = jnp.transpose(gate_b.reshape(2, 2, MLSTM_N_HEADS), (2, 0, 1)).reshape(MLSTM_N_HEADS, 4)
    gb = jnp.concatenate([gb, jnp.zeros_like(gb)], axis=1).reshape(rows, 1)
    return main, wg, jnp.broadcast_to(gb, (rows, LANES))


def _rwkv_params(mu, w0, w_up, a0, a_up, k_k, k_a, r_k, ln_w, ln_b):
    width = mu.shape[1]
    n_pairs = width // LANES
    row = lambda x: x.reshape(1, width)

    def cat_dirs(x):
        return jnp.transpose(x.reshape(2, n_pairs, LANES), (1, 0, 2)).reshape(n_pairs, 1, 2 * LANES)

    def block_up(x):
        lora = x.shape[1]
        xp = jnp.transpose(x.reshape(2, lora, n_pairs, LANES), (2, 0, 1, 3))
        z = jnp.zeros_like(xp[:, 0])
        top = jnp.concatenate([xp[:, 0], z], axis=2)
        bot = jnp.concatenate([z, xp[:, 1]], axis=2)
        return jnp.concatenate([top, bot], axis=1).astype(BF16)

    return {"mu": mu, "k_k": row(k_k), "k_a": row(k_a), "r_k": row(r_k), "ln_w": row(ln_w), "ln_b": row(ln_b),
            "w0": cat_dirs(w0), "a0": cat_dirs(a0), "w_up": block_up(w_up), "a_up": block_up(a_up)}


def kernel(x, c, ctx, c_ctx, l0_norm_g, l0_mod_w, l0_mod_b, l0_w_in, l0_w_out, l0_mlstm_conv_w, l0_mlstm_gate_b, l0_mlstm_norm_g, l0_rwkv_mu, l0_rwkv_w0, l0_rwkv_w_up, l0_rwkv_a0, l0_rwkv_a_up, l0_rwkv_k_k, l0_rwkv_k_a, l0_rwkv_r_k, l0_rwkv_ln_w, l0_rwkv_ln_b, hgrn_lower_bounds, l1_norm_g, l1_mod_w, l1_mod_b, l1_w_in, l1_w_out, l1_hgrn_norm_g, l1_hyena_short_w, l1_hyena_short_b, l1_hyena_w1, l1_hyena_b1, l1_hyena_w2, l1_hyena_b2, l1_hyena_w3, l1_hyena_bias, final_norm_g):
    bsz, n_lat, d = x.shape
    n_ctx = ctx.shape[1]

    pad = (-(bsz + 1)) % 8
    cc = jnp.concatenate([c, c_ctx[None], jnp.zeros((pad, d), F32)], axis=0)
    mod0, mod1 = _modulation(cc, l0_mod_w, l0_mod_b, l1_mod_w, l1_mod_b)
    mod0 = mod0[:bsz + 1].reshape(bsz + 1, 3, d)
    mod1 = mod1[:bsz + 1].reshape(bsz + 1, 3, d)

    w_main, w_gate, b_gate = _even_weight_layout(l0_w_in, l0_mlstm_gate_b)
    n0 = w_main.shape[1]
    u0, gt0 = _proj_in(x, ctx, l0_norm_g, mod0, w_main, n0 // 2, w_gate, b_gate)
    gt0 = gt0.reshape(bsz, MLSTM_N_HEADS, SUBLANES, n_ctx + n_lat)
    y_m = _mlstm(u0, gt0, l0_mlstm_conv_w, l0_mlstm_norm_g, n_ctx)
    rp = _rwkv_params(l0_rwkv_mu, l0_rwkv_w0, l0_rwkv_w_up, l0_rwkv_a0, l0_rwkv_a_up, l0_rwkv_k_k,
                      l0_rwkv_k_a, l0_rwkv_r_k, l0_rwkv_ln_w, l0_rwkv_ln_b)
    y_r = _rwkv(u0, 5 * MLSTM_N_HEADS * LANES, rp, n_ctx)
    x1, ctx1 = _proj_out(y_m, y_r, x, ctx, mod0, l0_w_out.astype(BF16))

    x1 = x1.reshape(bsz, n_lat // GRID_WIDTH, GRID_WIDTH, d)
    w1 = l1_w_in.astype(BF16)
    (u1,) = _proj_in(x1, ctx1, l1_norm_g, mod1, w1, w1.shape[1] // 2)
    y_g = _hgrn(u1, hgrn_lower_bounds, l1_hgrn_norm_g, n_ctx, layer=1)
    hw = l1_hyena_bias.shape[0]
    hk = _hyena_filters(n_lat, l1_hyena_w1, l1_hyena_b1, l1_hyena_w2, l1_hyena_b2, l1_hyena_w3, hw)
    fwd, inv = _dft_tables(n_lat // 2)
    kspec = _filter_spectrum(fwd, hk, l1_hyena_bias)
    p16, e16 = _hyena_pre(u1, 5 * HGRN_N_HEADS * LANES, l1_hyena_short_w, l1_hyena_short_b, n_ctx)
    z16 = _conv_spectrum(fwd, p16, kspec)
    y_y = _conv_inverse(inv, z16, e16)
    out = _proj_out_final(y_g, y_y, x1, mod1, l1_w_out.astype(BF16), final_norm_g)
    return out.reshape(bsz, n_lat, d)
```
